```python
import math
import jax, jax.numpy as jnp
from jax import lax
import numpy as np

D_MODEL = 1024
BATCH = 16
SEQ = 4096
DEPTH = 4

DN_ALPHA = (2.0 * DEPTH) ** 0.25
DN_BETA = (8.0 * DEPTH) ** -0.25
NORM_EPS = 1e-5
N_MOD = 9
D_FF = ((8 * D_MODEL // 3 + 127) // 128) * 128

SSD_HEAD_DIM = 64
SSD_HEADS = D_MODEL // SSD_HEAD_DIM
SSD_D_INNER = SSD_HEADS * SSD_HEAD_DIM
SSD_GROUPS = 2
SSD_STATE = 128
SSD_CONV = 4
SSD_CHUNK = 128
SSD_CONV_DIM = SSD_D_INNER + 2 * SSD_GROUPS * SSD_STATE
SSD_IN = SSD_D_INNER + SSD_CONV_DIM + SSD_HEADS

POOL_WINDOWS = (2, 4, 8, 16)
POOL_GROUPS = len(POOL_WINDOWS)
POOL_GROUP_DIM = D_MODEL // 8
POOL_DIM = POOL_GROUPS * POOL_GROUP_DIM

EVEN_IN = SSD_IN + POOL_DIM
EVEN_MIX = SSD_D_INNER + POOL_DIM

CONF_DIM = D_MODEL // 2
CONF_KERNEL = 31
LRU_DIM = D_MODEL
LRU_HEADS = 8
LRU_HEAD_DIM = LRU_DIM // LRU_HEADS
LRU_CONV = 4
LRU_C = 8.0

ODD_IN = 2 * CONF_DIM + 2 * LRU_DIM
ODD_MIX = CONF_DIM + LRU_DIM

N_EVEN = (DEPTH + 1) // 2
N_ODD = DEPTH // 2

kernel_name = "hybrid_ssd_pool_conformer_rglru_trunk"


def layer_norm(x, g, b):
    xf = x.astype(jnp.float32)
    mu = jnp.mean(xf, axis=-1, keepdims=True)
    xc = xf - mu
    var = jnp.mean(xc * xc, axis=-1, keepdims=True)
    return (xc * lax.rsqrt(var + NORM_EPS) * g.astype(jnp.float32) + b.astype(jnp.float32)).astype(x.dtype)


def rms_norm(x, g):
    xf = x.astype(jnp.float32)
    ms = jnp.mean(xf * xf, axis=-1, keepdims=True)
    return xf * lax.rsqrt(ms + NORM_EPS) * g.astype(jnp.float32)


def causal_dwconv(x, w, b):
    k, ch = w.shape
    y = lax.conv_general_dilated(
        x, w[:, None, :].astype(x.dtype), window_strides=(1,), padding=[(k - 1, 0)],
        dimension_numbers=("NWC", "WIO", "NWC"), feature_group_count=ch)
    return y + b.astype(x.dtype)


def modulate(h, shift, scale):
    return h * (1.0 + scale[:, None, :]) + shift[:, None, :]


def post_norm(x, y, g, b):
    return layer_norm(DN_ALPHA * x + y, g, b)


def swiglu(h, w_in, w_out):
    gu = jnp.einsum("btd,df->btf", h, w_in)
    gate, up = jnp.split(gu, 2, axis=-1)
    return jnp.einsum("btf,fd->btd", jax.nn.silu(gate) * up, w_out)


def ssd_mixer(zxbcdt, conv_w, conv_b, dt_bias, a_log, d_skip, norm_g):
    bsz, t_len, _ = zxbcdt.shape
    z = zxbcdt[..., :SSD_D_INNER]
    xbc = zxbcdt[..., SSD_D_INNER:SSD_D_INNER + SSD_CONV_DIM]
    dt = zxbcdt[..., SSD_D_INNER + SSD_CONV_DIM:]
    xbc = jax.nn.silu(causal_dwconv(xbc, conv_w, conv_b)).astype(jnp.float32)
    g_n = SSD_GROUPS * SSD_STATE
    nc = t_len // SSD_CHUNK
    r_h = SSD_HEADS // SSD_GROUPS
    x = xbc[..., :SSD_D_INNER].reshape(bsz, nc, SSD_CHUNK, SSD_GROUPS, r_h, SSD_HEAD_DIM)
    bm = xbc[..., SSD_D_INNER:SSD_D_INNER + g_n].reshape(bsz, nc, SSD_CHUNK, SSD_GROUPS, SSD_STATE)
    cm = xbc[..., SSD_D_INNER + g_n:].reshape(bsz, nc, SSD_CHUNK, SSD_GROUPS, SSD_STATE)
    dt = jax.nn.softplus(dt.astype(jnp.float32) + dt_bias.astype(jnp.float32))
    dt = dt.reshape(bsz, nc, SSD_CHUNK, SSD_GROUPS, r_h)
    a = -jnp.exp(a_log.astype(jnp.float32)).reshape(SSD_GROUPS, r_h)
    a_cum = jnp.cumsum(dt * a, axis=2)
    xdt = x * dt[..., None]
    seg = a_cum[:, :, :, None] - a_cum[:, :, None]
    causal = jnp.tril(jnp.ones((SSD_CHUNK, SSD_CHUNK), dtype=bool))[:, :, None, None]
    l_mat = jnp.exp(jnp.where(causal, seg, -jnp.inf))
    cb = jnp.einsum("bclgn,bcsgn->bclsg", cm, bm)
    y_diag = jnp.einsum("bclsg,bclsgr,bcsgrp->bclgrp", cb, l_mat, xdt)
    decay_s = jnp.exp(a_cum[:, :, -1:] - a_cum)
    states = jnp.einsum("bcsgn,bcsgr,bcsgrp->bcgrpn", bm, decay_s, xdt)
    chunk_decay = jnp.exp(a_cum[:, :, -1])

    def step(h, inp):
        s_c, dec = inp
        return h * dec[..., None, None] + s_c, h

    h0 = jnp.zeros((bsz, SSD_GROUPS, r_h, SSD_HEAD_DIM, SSD_STATE), jnp.float32)
    _, h_prev = lax.scan(step, h0, (jnp.moveaxis(states, 1, 0), jnp.moveaxis(chunk_decay, 1, 0)))
    h_prev = jnp.moveaxis(h_prev, 0, 1)
    y_off = jnp.einsum("bclgn,bcgrpn,bclgr->bclgrp", cm, h_prev, jnp.exp(a_cum))
    y = y_diag + y_off + x * d_skip.astype(jnp.float32).reshape(SSD_GROUPS, r_h)[:, :, None]
    y = y.reshape(bsz, t_len, SSD_D_INNER) * jax.nn.silu(z.astype(jnp.float32))
    return rms_norm(y, norm_g).astype(zxbcdt.dtype)


def pool_mixer(u, w_grp, scale):
    bsz, t_len, _ = u.shape
    uf = u.astype(jnp.float32).reshape(bsz, t_len, POOL_GROUPS, POOL_GROUP_DIM)
    cs = jnp.cumsum(uf, axis=1)
    pos = jnp.arange(1, t_len + 1, dtype=jnp.float32)
    outs = []
    for g, w in enumerate(POOL_WINDOWS):
        c_g = cs[:, :, g]
        lo = jnp.pad(c_g[:, :t_len - w], ((0, 0), (w, 0), (0, 0)))
        cnt = jnp.minimum(pos, float(w))[None, :, None]
        outs.append((c_g - lo) / cnt - uf[:, :, g])
    pooled = jnp.stack(outs, axis=2)
    mixed = jnp.einsum("btgc,gcd->btgd", pooled, w_grp.astype(jnp.float32))
    return (mixed.reshape(bsz, t_len, POOL_DIM) * scale.astype(jnp.float32)).astype(u.dtype)


def conformer_conv(v, gate, dw_w, dw_b, ln_g, ln_b):
    h = v * jax.nn.sigmoid(gate)
    h = causal_dwconv(h, dw_w, dw_b)
    h = layer_norm(h, ln_g, ln_b)
    return jax.nn.silu(h)


def _lin_combine(e1, e2):
    a1, b1 = e1
    a2, b2 = e2
    return a1 * a2, a2 * b1 + b2


def rglru_mixer(xr, gate, conv_w, conv_b, wa, ba, wx, bx, lam):
    bsz, t_len, _ = xr.shape
    xr = causal_dwconv(xr, conv_w, conv_b)
    xh = xr.reshape(bsz, t_len, LRU_HEADS, LRU_HEAD_DIM)
    r = jax.nn.sigmoid(jnp.einsum("bthi,hij->bthj", xh, wa).reshape(bsz, t_len, LRU_DIM) + ba)
    i = jax.nn.sigmoid(jnp.einsum("bthi,hij->bthj", xh, wx).reshape(bsz, t_len, LRU_DIM) + bx)
    log_a = -LRU_C * r.astype(jnp.float32) * jax.nn.softplus(-lam.astype(jnp.float32))
    a = jnp.exp(log_a)
    b = jnp.sqrt(-jnp.expm1(2.0 * log_a)) * (i * xr).astype(jnp.float32)
    _, h = lax.associative_scan(_lin_combine, (a, b), axis=1)
    return h.astype(xr.dtype) * jax.nn.gelu(gate)


def _fwd_setup_inputs(seed: int = 0) -> dict:
    key = jax.random.key(seed)
    ks = iter(jax.random.split(key, 48))

    def nrm(shape, scale):
        return jax.random.normal(next(ks), shape, jnp.float32) * scale

    def unif(shape, lo, hi):
        return jax.random.uniform(next(ks), shape, jnp.float32, minval=lo, maxval=hi)

    x = nrm((BATCH, SEQ, D_MODEL), 1.0)
    c = nrm((BATCH, D_MODEL), 1.0)
    ada_w = nrm((DEPTH, D_MODEL, N_MOD * D_MODEL), 0.1 * D_MODEL ** -0.5)
    ada_b = nrm((DEPTH, N_MOD * D_MODEL), 0.01)
    ln_g = 1.0 + nrm((DEPTH, 3, D_MODEL), 0.01)
    ln_b = nrm((DEPTH, 3, D_MODEL), 0.01)
    ffn_w_in = nrm((DEPTH, 2, D_MODEL, 2 * D_FF), D_MODEL ** -0.5)
    ffn_w_out = nrm((DEPTH, 2, D_FF, D_MODEL), DN_BETA * D_FF ** -0.5)
    ev_w_in = nrm((N_EVEN, D_MODEL, EVEN_IN), D_MODEL ** -0.5)
    ssd_conv_w = nrm((N_EVEN, SSD_CONV, SSD_CONV_DIM), SSD_CONV ** -0.5)
    ssd_conv_b = nrm((N_EVEN, SSD_CONV_DIM), 0.01)
    dt0 = jnp.exp(unif((N_EVEN, SSD_HEADS), math.log(1e-3), math.log(1e-1)))
    ssd_dt_bias = dt0 + jnp.log(-jnp.expm1(-dt0))
    ssd_a_log = jnp.log(unif((N_EVEN, SSD_HEADS), 1.0, 16.0))
    ssd_d = 1.0 + nrm((N_EVEN, SSD_HEADS), 0.01)
    ssd_norm_g = 1.0 + nrm((N_EVEN, SSD_D_INNER), 0.01)
    pool_w = nrm((N_EVEN, POOL_GROUPS, POOL_GROUP_DIM, POOL_GROUP_DIM), POOL_GROUP_DIM ** -0.5)
    pool_scale = 1.0 + nrm((N_EVEN, POOL_DIM), 0.01)
    ev_w_out = nrm((N_EVEN, EVEN_MIX, D_MODEL), DN_BETA * EVEN_MIX ** -0.5)
    od_w_in = nrm((N_ODD, D_MODEL, ODD_IN), D_MODEL ** -0.5)
    conf_dw_w = nrm((N_ODD, CONF_KERNEL, CONF_DIM), CONF_KERNEL ** -0.5)
    conf_dw_b = nrm((N_ODD, CONF_DIM), 0.01)
    conf_ln_g = 1.0 + nrm((N_ODD, CONF_DIM), 0.01)
    conf_ln_b = nrm((N_ODD, CONF_DIM), 0.01)
    lru_conv_w = nrm((N_ODD, LRU_CONV, LRU_DIM), LRU_CONV ** -0.5)
    lru_conv_b = nrm((N_ODD, LRU_DIM), 0.01)
    lru_wa = nrm((N_ODD, LRU_HEADS, LRU_HEAD_DIM, LRU_HEAD_DIM), LRU_HEAD_DIM ** -0.5)
    lru_ba = nrm((N_ODD, LRU_DIM), 0.01)
    lru_wx = nrm((N_ODD, LRU_HEADS, LRU_HEAD_DIM, LRU_HEAD_DIM), LRU_HEAD_DIM ** -0.5)
    lru_bx = nrm((N_ODD, LRU_DIM), 0.01)
    a_c = unif((N_ODD, LRU_DIM), 0.9, 0.999)
    a_base = a_c ** (1.0 / LRU_C)
    lru_lambda = jnp.log(a_base) - jnp.log1p(-a_base)
    od_w_out = nrm((N_ODD, ODD_MIX, D_MODEL), DN_BETA * ODD_MIX ** -0.5)
    return {
        "x": x, "c": c, "ada_w": ada_w, "ada_b": ada_b, "ln_g": ln_g, "ln_b": ln_b,
        "ffn_w_in": ffn_w_in, "ffn_w_out": ffn_w_out,
        "ev_w_in": ev_w_in, "ssd_conv_w": ssd_conv_w, "ssd_conv_b": ssd_conv_b,
        "ssd_dt_bias": ssd_dt_bias, "ssd_a_log": ssd_a_log, "ssd_d": ssd_d, "ssd_norm_g": ssd_norm_g,
        "pool_w": pool_w, "pool_scale": pool_scale, "ev_w_out": ev_w_out,
        "od_w_in": od_w_in, "conf_dw_w": conf_dw_w, "conf_dw_b": conf_dw_b,
        "conf_ln_g": conf_ln_g, "conf_ln_b": conf_ln_b,
        "lru_conv_w": lru_conv_w, "lru_conv_b": lru_conv_b, "lru_wa": lru_wa, "lru_ba": lru_ba,
        "lru_wx": lru_wx, "lru_bx": lru_bx, "lru_lambda": lru_lambda, "od_w_out": od_w_out,
    }


def _fwd_reference(x, c, ada_w, ada_b, ln_g, ln_b, ffn_w_in, ffn_w_out,
              ev_w_in, ssd_conv_w, ssd_conv_b, ssd_dt_bias, ssd_a_log, ssd_d, ssd_norm_g,
              pool_w, pool_scale, ev_w_out,
              od_w_in, conf_dw_w, conf_dw_b, conf_ln_g, conf_ln_b,
              lru_conv_w, lru_conv_b, lru_wa, lru_ba, lru_wx, lru_bx, lru_lambda, od_w_out):
    cond = jax.nn.silu(c)
    for layer in range(DEPTH):
        mod = cond @ ada_w[layer] + ada_b[layer]
        sh1, sc1, g1, sh2, sc2, g2, sh3, sc3, g3 = jnp.split(mod, N_MOD, axis=-1)
        y = swiglu(modulate(x, sh1, sc1), ffn_w_in[layer, 0], ffn_w_out[layer, 0])
        x = post_norm(x, 0.5 * (1.0 + g1[:, None, :]) * y, ln_g[layer, 0], ln_b[layer, 0])
        h = modulate(x, sh2, sc2)
        if layer % 2 == 0:
            e = layer // 2
            proj = jnp.einsum("btd,de->bte", h, ev_w_in[e])
            y_a = ssd_mixer(proj[..., :SSD_IN], ssd_conv_w[e], ssd_conv_b[e], ssd_dt_bias[e],
                            ssd_a_log[e], ssd_d[e], ssd_norm_g[e])
            y_b = pool_mixer(proj[..., SSD_IN:], pool_w[e], pool_scale[e])
            y = jnp.einsum("bte,ed->btd", jnp.concatenate([y_a, y_b], axis=-1), ev_w_out[e])
        else:
            o = layer // 2
            proj = jnp.einsum("btd,de->bte", h, od_w_in[o])
            v = proj[..., :CONF_DIM]
            gt = proj[..., CONF_DIM:2 * CONF_DIM]
            xr = proj[..., 2 * CONF_DIM:2 * CONF_DIM + LRU_DIM]
            gr = proj[..., 2 * CONF_DIM + LRU_DIM:]
            y_c = conformer_conv(v, gt, conf_dw_w[o], conf_dw_b[o], conf_ln_g[o], conf_ln_b[o])
            y_d = rglru_mixer(xr, gr, lru_conv_w[o], lru_conv_b[o], lru_wa[o], lru_ba[o],
                              lru_wx[o], lru_bx[o], lru_lambda[o])
            y = jnp.einsum("bte,ed->btd", jnp.concatenate([y_c, y_d], axis=-1), od_w_out[o])
        x = post_norm(x, (1.0 + g2[:, None, :]) * y, ln_g[layer, 1], ln_b[layer, 1])
        y = swiglu(modulate(x, sh3, sc3), ffn_w_in[layer, 1], ffn_w_out[layer, 1])
        x = post_norm(x, 0.5 * (1.0 + g3[:, None, :]) * y, ln_g[layer, 2], ln_b[layer, 2])
    return x


import jax as _jax
import jax.numpy as _jnp

TWIN_FORMAT = 'train_step'
FWD_PARAMS = ['x', 'c', 'ada_w', 'ada_b', 'ln_g', 'ln_b', 'ffn_w_in', 'ffn_w_out', 'ev_w_in', 'ssd_conv_w', 'ssd_conv_b', 'ssd_dt_bias', 'ssd_a_log', 'ssd_d', 'ssd_norm_g', 'pool_w', 'pool_scale', 'ev_w_out', 'od_w_in', 'conf_dw_w', 'conf_dw_b', 'conf_ln_g', 'conf_ln_b', 'lru_conv_w', 'lru_conv_b', 'lru_wa', 'lru_ba', 'lru_wx', 'lru_bx', 'lru_lambda', 'od_w_out']
TWIN_WEIGHTS = ['ada_w', 'ada_b', 'ln_g', 'ln_b', 'ffn_w_in', 'ffn_w_out', 'ev_w_in', 'ssd_conv_w', 'ssd_conv_b', 'ssd_dt_bias', 'ssd_a_log', 'ssd_d', 'ssd_norm_g', 'pool_w', 'pool_scale', 'ev_w_out', 'od_w_in', 'conf_dw_w', 'conf_dw_b', 'conf_ln_g', 'conf_ln_b', 'lru_conv_w', 'lru_conv_b', 'lru_wa', 'lru_ba', 'lru_wx', 'lru_bx', 'lru_lambda', 'od_w_out']
TWIN_DIFF_INPUT = 'x'
TWIN_INPUTS = ['x', 'c', 'ada_w', 'ada_b', 'ln_g', 'ln_b', 'ffn_w_in', 'ffn_w_out', 'ev_w_in', 'ssd_conv_w', 'ssd_conv_b', 'ssd_dt_bias', 'ssd_a_log', 'ssd_d', 'ssd_norm_g', 'pool_w', 'pool_scale', 'ev_w_out', 'od_w_in', 'conf_dw_w', 'conf_dw_b', 'conf_ln_g', 'conf_ln_b', 'lru_conv_w', 'lru_conv_b', 'lru_wa', 'lru_ba', 'lru_wx', 'lru_bx', 'lru_lambda', 'od_w_out', 'loss_target', 'm_ada_w', 'm_ada_b', 'm_ln_g', 'm_ln_b', 'm_ffn_w_in', 'm_ffn_w_out', 'm_ev_w_in', 'm_ssd_conv_w', 'm_ssd_conv_b', 'm_ssd_dt_bias', 'm_ssd_a_log', 'm_ssd_d', 'm_ssd_norm_g', 'm_pool_w', 'm_pool_scale', 'm_ev_w_out', 'm_od_w_in', 'm_conf_dw_w', 'm_conf_dw_b', 'm_conf_ln_g', 'm_conf_ln_b', 'm_lru_conv_w', 'm_lru_conv_b', 'm_lru_wa', 'm_lru_ba', 'm_lru_wx', 'm_lru_bx', 'm_lru_lambda', 'm_od_w_out', 'v_ada_w', 'v_ada_b', 'v_ln_g', 'v_ln_b', 'v_ffn_w_in', 'v_ffn_w_out', 'v_ev_w_in', 'v_ssd_conv_w', 'v_ssd_conv_b', 'v_ssd_dt_bias', 'v_ssd_a_log', 'v_ssd_d', 'v_ssd_norm_g', 'v_pool_w', 'v_pool_scale', 'v_ev_w_out', 'v_od_w_in', 'v_conf_dw_w', 'v_conf_dw_b', 'v_conf_ln_g', 'v_conf_ln_b', 'v_lru_conv_w', 'v_lru_conv_b', 'v_lru_wa', 'v_lru_ba', 'v_lru_wx', 'v_lru_bx', 'v_lru_lambda', 'v_od_w_out']
TWIN_OUTPUTS = ['loss', 'grad_x', 'grad_ada_w', 'grad_ada_b', 'grad_ln_g', 'grad_ln_b', 'grad_ffn_w_in', 'grad_ffn_w_out', 'grad_ev_w_in', 'grad_ssd_conv_w', 'grad_ssd_conv_b', 'grad_ssd_dt_bias', 'grad_ssd_a_log', 'grad_ssd_d', 'grad_ssd_norm_g', 'grad_pool_w', 'grad_pool_scale', 'grad_ev_w_out', 'grad_od_w_in', 'grad_conf_dw_w', 'grad_conf_dw_b', 'grad_conf_ln_g', 'grad_conf_ln_b', 'grad_lru_conv_w', 'grad_lru_conv_b', 'grad_lru_wa', 'grad_lru_ba', 'grad_lru_wx', 'grad_lru_bx', 'grad_lru_lambda', 'grad_od_w_out', 'delta_ada_w', 'delta_ada_b', 'delta_ln_g', 'delta_ln_b', 'delta_ffn_w_in', 'delta_ffn_w_out', 'delta_ev_w_in', 'delta_ssd_conv_w', 'delta_ssd_conv_b', 'delta_ssd_dt_bias', 'delta_ssd_a_log', 'delta_ssd_d', 'delta_ssd_norm_g', 'delta_pool_w', 'delta_pool_scale', 'delta_ev_w_out', 'delta_od_w_in', 'delta_conf_dw_w', 'delta_conf_dw_b', 'delta_conf_ln_g', 'delta_conf_ln_b', 'delta_lru_conv_w', 'delta_lru_conv_b', 'delta_lru_wa', 'delta_lru_ba', 'delta_lru_wx', 'delta_lru_bx', 'delta_lru_lambda', 'delta_od_w_out', 'new_m_ada_w', 'new_m_ada_b', 'new_m_ln_g', 'new_m_ln_b', 'new_m_ffn_w_in', 'new_m_ffn_w_out', 'new_m_ev_w_in', 'new_m_ssd_conv_w', 'new_m_ssd_conv_b', 'new_m_ssd_dt_bias', 'new_m_ssd_a_log', 'new_m_ssd_d', 'new_m_ssd_norm_g', 'new_m_pool_w', 'new_m_pool_scale', 'new_m_ev_w_out', 'new_m_od_w_in', 'new_m_conf_dw_w', 'new_m_conf_dw_b', 'new_m_conf_ln_g', 'new_m_conf_ln_b', 'new_m_lru_conv_w', 'new_m_lru_conv_b', 'new_m_lru_wa', 'new_m_lru_ba', 'new_m_lru_wx', 'new_m_lru_bx', 'new_m_lru_lambda', 'new_m_od_w_out', 'new_v_ada_w', 'new_v_ada_b', 'new_v_ln_g', 'new_v_ln_b', 'new_v_ffn_w_in', 'new_v_ffn_w_out', 'new_v_ev_w_in', 'new_v_ssd_conv_w', 'new_v_ssd_conv_b', 'new_v_ssd_dt_bias', 'new_v_ssd_a_log', 'new_v_ssd_d', 'new_v_ssd_norm_g', 'new_v_pool_w', 'new_v_pool_scale', 'new_v_ev_w_out', 'new_v_od_w_in', 'new_v_conf_dw_w', 'new_v_conf_dw_b', 'new_v_conf_ln_g', 'new_v_conf_ln_b', 'new_v_lru_conv_w', 'new_v_lru_conv_b', 'new_v_lru_wa', 'new_v_lru_ba', 'new_v_lru_wx', 'new_v_lru_bx', 'new_v_lru_lambda', 'new_v_od_w_out']
TWIN_LEAF_KINDS = {'loss': 'loss', 'grad_x': 'grad_x', 'grad_ada_w': 'grad_w', 'grad_ada_b': 'grad_w', 'grad_ln_g': 'grad_w', 'grad_ln_b': 'grad_w', 'grad_ffn_w_in': 'grad_w', 'grad_ffn_w_out': 'grad_w', 'grad_ev_w_in': 'grad_w', 'grad_ssd_conv_w': 'grad_w', 'grad_ssd_conv_b': 'grad_w', 'grad_ssd_dt_bias': 'grad_w', 'grad_ssd_a_log': 'grad_w', 'grad_ssd_d': 'grad_w', 'grad_ssd_norm_g': 'grad_w', 'grad_pool_w': 'grad_w', 'grad_pool_scale': 'grad_w', 'grad_ev_w_out': 'grad_w', 'grad_od_w_in': 'grad_w', 'grad_conf_dw_w': 'grad_w', 'grad_conf_dw_b': 'grad_w', 'grad_conf_ln_g': 'grad_w', 'grad_conf_ln_b': 'grad_w', 'grad_lru_conv_w': 'grad_w', 'grad_lru_conv_b': 'grad_w', 'grad_lru_wa': 'grad_w', 'grad_lru_ba': 'grad_w', 'grad_lru_wx': 'grad_w', 'grad_lru_bx': 'grad_w', 'grad_lru_lambda': 'grad_w', 'grad_od_w_out': 'grad_w', 'delta_ada_w': 'delta_w', 'delta_ada_b': 'delta_w', 'delta_ln_g': 'delta_w', 'delta_ln_b': 'delta_w', 'delta_ffn_w_in': 'delta_w', 'delta_ffn_w_out': 'delta_w', 'delta_ev_w_in': 'delta_w', 'delta_ssd_conv_w': 'delta_w', 'delta_ssd_conv_b': 'delta_w', 'delta_ssd_dt_bias': 'delta_w', 'delta_ssd_a_log': 'delta_w', 'delta_ssd_d': 'delta_w', 'delta_ssd_norm_g': 'delta_w', 'delta_pool_w': 'delta_w', 'delta_pool_scale': 'delta_w', 'delta_ev_w_out': 'delta_w', 'delta_od_w_in': 'delta_w', 'delta_conf_dw_w': 'delta_w', 'delta_conf_dw_b': 'delta_w', 'delta_conf_ln_g': 'delta_w', 'delta_conf_ln_b': 'delta_w', 'delta_lru_conv_w': 'delta_w', 'delta_lru_conv_b': 'delta_w', 'delta_lru_wa': 'delta_w', 'delta_lru_ba': 'delta_w', 'delta_lru_wx': 'delta_w', 'delta_lru_bx': 'delta_w', 'delta_lru_lambda': 'delta_w', 'delta_od_w_out': 'delta_w', 'new_m_ada_w': 'new_m', 'new_m_ada_b': 'new_m', 'new_m_ln_g': 'new_m', 'new_m_ln_b': 'new_m', 'new_m_ffn_w_in': 'new_m', 'new_m_ffn_w_out': 'new_m', 'new_m_ev_w_in': 'new_m', 'new_m_ssd_conv_w': 'new_m', 'new_m_ssd_conv_b': 'new_m', 'new_m_ssd_dt_bias': 'new_m', 'new_m_ssd_a_log': 'new_m', 'new_m_ssd_d': 'new_m', 'new_m_ssd_norm_g': 'new_m', 'new_m_pool_w': 'new_m', 'new_m_pool_scale': 'new_m', 'new_m_ev_w_out': 'new_m', 'new_m_od_w_in': 'new_m', 'new_m_conf_dw_w': 'new_m', 'new_m_conf_dw_b': 'new_m', 'new_m_conf_ln_g': 'new_m', 'new_m_conf_ln_b': 'new_m', 'new_m_lru_conv_w': 'new_m', 'new_m_lru_conv_b': 'new_m', 'new_m_lru_wa': 'new_m', 'new_m_lru_ba': 'new_m', 'new_m_lru_wx': 'new_m', 'new_m_lru_bx': 'new_m', 'new_m_lru_lambda': 'new_m', 'new_m_od_w_out': 'new_m', 'new_v_ada_w': 'new_v', 'new_v_ada_b': 'new_v', 'new_v_ln_g': 'new_v', 'new_v_ln_b': 'new_v', 'new_v_ffn_w_in': 'new_v', 'new_v_ffn_w_out': 'new_v', 'new_v_ev_w_in': 'new_v', 'new_v_ssd_conv_w': 'new_v', 'new_v_ssd_conv_b': 'new_v', 'new_v_ssd_dt_bias': 'new_v', 'new_v_ssd_a_log': 'new_v', 'new_v_ssd_d': 'new_v', 'new_v_ssd_norm_g': 'new_v', 'new_v_pool_w': 'new_v', 'new_v_pool_scale': 'new_v', 'new_v_ev_w_out': 'new_v', 'new_v_od_w_in': 'new_v', 'new_v_conf_dw_w': 'new_v', 'new_v_conf_dw_b': 'new_v', 'new_v_conf_ln_g': 'new_v', 'new_v_conf_ln_b': 'new_v', 'new_v_lru_conv_w': 'new_v', 'new_v_lru_conv_b': 'new_v', 'new_v_lru_wa': 'new_v', 'new_v_lru_ba': 'new_v', 'new_v_lru_wx': 'new_v', 'new_v_lru_bx': 'new_v', 'new_v_lru_lambda': 'new_v', 'new_v_od_w_out': 'new_v'}


def _forward(args):
    return _fwd_reference(*[args[k] for k in FWD_PARAMS])


def _output_shape():
    out = _jax.eval_shape(lambda: _forward(_fwd_setup_inputs(0)))
    return out.shape, out.dtype

N_MICROBATCH = 1
ADAM_LR = 0.001
ADAM_B1 = 0.9
ADAM_B2 = 0.999
ADAM_EPS = 1e-08
ADAM_WD = 0.01
ADAM_STEP = 10
PER_EXAMPLE_BATCH_AXIS = {'x': 0, 'c': 0, 'loss_target': 0}
SHARED_INPUTS = []
_WEIGHT_DTYPES = {'ada_w': _jnp.float32, 'ada_b': _jnp.float32, 'ln_g': _jnp.float32, 'ln_b': _jnp.float32, 'ffn_w_in': _jnp.float32, 'ffn_w_out': _jnp.float32, 'ev_w_in': _jnp.float32, 'ssd_conv_w': _jnp.float32, 'ssd_conv_b': _jnp.float32, 'ssd_dt_bias': _jnp.float32, 'ssd_a_log': _jnp.float32, 'ssd_d': _jnp.float32, 'ssd_norm_g': _jnp.float32, 'pool_w': _jnp.float32, 'pool_scale': _jnp.float32, 'ev_w_out': _jnp.float32, 'od_w_in': _jnp.float32, 'conf_dw_w': _jnp.float32, 'conf_dw_b': _jnp.float32, 'conf_ln_g': _jnp.float32, 'conf_ln_b': _jnp.float32, 'lru_conv_w': _jnp.float32, 'lru_conv_b': _jnp.float32, 'lru_wa': _jnp.float32, 'lru_ba': _jnp.float32, 'lru_wx': _jnp.float32, 'lru_bx': _jnp.float32, 'lru_lambda': _jnp.float32, 'od_w_out': _jnp.float32}
MOMENT_SCALE = {'ada_w': 5.161096e-02, 'ada_b': 8.908052e-02, 'ln_g': 1.839779e+01, 'ln_b': 1.111857e+00, 'ffn_w_in': 1.190074e-02, 'ffn_w_out': 4.626160e-02, 'ev_w_in': 4.866631e-02, 'ssd_conv_w': 4.693789e-02, 'ssd_conv_b': 6.881125e-02, 'ssd_dt_bias': 1.100938e-01, 'ssd_a_log': 1.620756e-01, 'ssd_d': 2.338197e-01, 'ssd_norm_g': 5.432756e-02, 'pool_w': 4.518607e-02, 'pool_scale': 4.593751e-02, 'ev_w_out': 1.482245e-01, 'od_w_in': 3.107286e-02, 'conf_dw_w': 3.239983e-02, 'conf_dw_b': 7.365194e-02, 'conf_ln_g': 4.070312e-02, 'conf_ln_b': 4.253106e-02, 'lru_conv_w': 3.663114e-02, 'lru_conv_b': 3.288037e-01, 'lru_wa': 9.562692e-03, 'lru_ba': 9.826631e-03, 'lru_wx': 1.742647e-02, 'lru_bx': 1.480173e-02, 'lru_lambda': 2.040171e-02, 'od_w_out': 9.997182e-02}


def _to_microbatches(a, axis):
    t = _jnp.moveaxis(a, axis, 0)
    t = t.reshape((N_MICROBATCH, t.shape[0] // N_MICROBATCH) + t.shape[1:])
    return _jnp.moveaxis(t, 1, axis + 1)


def setup_inputs(seed: int = 0) -> dict:
    inp = _fwd_setup_inputs(seed)
    key = _jax.random.fold_in(_jax.random.key(seed), 7919)
    shape, _ = _output_shape()
    out = dict(inp)
    out["loss_target"] = _jax.random.normal(_jax.random.fold_in(key, 0), shape, _jnp.float32)
    for i, name in enumerate(TWIN_WEIGHTS):
        w = inp[name].astype(_jnp.float32)
        if MOMENT_SCALE is None:
            s = _jnp.sqrt(_jnp.mean(_jnp.square(w)) + 1e-30)
        else:
            s = MOMENT_SCALE[name]
        km, kv = _jax.random.split(_jax.random.fold_in(key, i + 1))
        out[name] = w
        out["m_" + name] = s * _jax.random.normal(km, w.shape, _jnp.float32)
        out["v_" + name] = (s * s) * _jax.random.uniform(kv, w.shape, _jnp.float32, 0.5, 1.5)
    if N_MICROBATCH > 1:
        for name, axis in PER_EXAMPLE_BATCH_AXIS.items():
            out[name] = _to_microbatches(out[name], axis)
    return {'x': out['x'], 'c': out['c'], 'ada_w': out['ada_w'], 'ada_b': out['ada_b'], 'ln_g': out['ln_g'], 'ln_b': out['ln_b'], 'ffn_w_in': out['ffn_w_in'], 'ffn_w_out': out['ffn_w_out'], 'ev_w_in': out['ev_w_in'], 'ssd_conv_w': out['ssd_conv_w'], 'ssd_conv_b': out['ssd_conv_b'], 'ssd_dt_bias': out['ssd_dt_bias'], 'ssd_a_log': out['ssd_a_log'], 'ssd_d': out['ssd_d'], 'ssd_norm_g': out['ssd_norm_g'], 'pool_w': out['pool_w'], 'pool_scale': out['pool_scale'], 'ev_w_out': out['ev_w_out'], 'od_w_in': out['od_w_in'], 'conf_dw_w': out['conf_dw_w'], 'conf_dw_b': out['conf_dw_b'], 'conf_ln_g': out['conf_ln_g'], 'conf_ln_b': out['conf_ln_b'], 'lru_conv_w': out['lru_conv_w'], 'lru_conv_b': out['lru_conv_b'], 'lru_wa': out['lru_wa'], 'lru_ba': out['lru_ba'], 'lru_wx': out['lru_wx'], 'lru_bx': out['lru_bx'], 'lru_lambda': out['lru_lambda'], 'od_w_out': out['od_w_out'], 'loss_target': out['loss_target'], 'm_ada_w': out['m_ada_w'], 'm_ada_b': out['m_ada_b'], 'm_ln_g': out['m_ln_g'], 'm_ln_b': out['m_ln_b'], 'm_ffn_w_in': out['m_ffn_w_in'], 'm_ffn_w_out': out['m_ffn_w_out'], 'm_ev_w_in': out['m_ev_w_in'], 'm_ssd_conv_w': out['m_ssd_conv_w'], 'm_ssd_conv_b': out['m_ssd_conv_b'], 'm_ssd_dt_bias': out['m_ssd_dt_bias'], 'm_ssd_a_log': out['m_ssd_a_log'], 'm_ssd_d': out['m_ssd_d'], 'm_ssd_norm_g': out['m_ssd_norm_g'], 'm_pool_w': out['m_pool_w'], 'm_pool_scale': out['m_pool_scale'], 'm_ev_w_out': out['m_ev_w_out'], 'm_od_w_in': out['m_od_w_in'], 'm_conf_dw_w': out['m_conf_dw_w'], 'm_conf_dw_b': out['m_conf_dw_b'], 'm_conf_ln_g': out['m_conf_ln_g'], 'm_conf_ln_b': out['m_conf_ln_b'], 'm_lru_conv_w': out['m_lru_conv_w'], 'm_lru_conv_b': out['m_lru_conv_b'], 'm_lru_wa': out['m_lru_wa'], 'm_lru_ba': out['m_lru_ba'], 'm_lru_wx': out['m_lru_wx'], 'm_lru_bx': out['m_lru_bx'], 'm_lru_lambda': out['m_lru_lambda'], 'm_od_w_out': out['m_od_w_out'], 'v_ada_w': out['v_ada_w'], 'v_ada_b': out['v_ada_b'], 'v_ln_g': out['v_ln_g'], 'v_ln_b': out['v_ln_b'], 'v_ffn_w_in': out['v_ffn_w_in'], 'v_ffn_w_out': out['v_ffn_w_out'], 'v_ev_w_in': out['v_ev_w_in'], 'v_ssd_conv_w': out['v_ssd_conv_w'], 'v_ssd_conv_b': out['v_ssd_conv_b'], 'v_ssd_dt_bias': out['v_ssd_dt_bias'], 'v_ssd_a_log': out['v_ssd_a_log'], 'v_ssd_d': out['v_ssd_d'], 'v_ssd_norm_g': out['v_ssd_norm_g'], 'v_pool_w': out['v_pool_w'], 'v_pool_scale': out['v_pool_scale'], 'v_ev_w_out': out['v_ev_w_out'], 'v_od_w_in': out['v_od_w_in'], 'v_conf_dw_w': out['v_conf_dw_w'], 'v_conf_dw_b': out['v_conf_dw_b'], 'v_conf_ln_g': out['v_conf_ln_g'], 'v_conf_ln_b': out['v_conf_ln_b'], 'v_lru_conv_w': out['v_lru_conv_w'], 'v_lru_conv_b': out['v_lru_conv_b'], 'v_lru_wa': out['v_lru_wa'], 'v_lru_ba': out['v_lru_ba'], 'v_lru_wx': out['v_lru_wx'], 'v_lru_bx': out['v_lru_bx'], 'v_lru_lambda': out['v_lru_lambda'], 'v_od_w_out': out['v_od_w_out']}


def _loss(weights, diff, rest, loss_target):
    with _jax.named_scope("forward"):
        args = {**rest, TWIN_DIFF_INPUT: diff, **{k: w.astype(_WEIGHT_DTYPES[k]) for k, w in weights.items()}}
        y = _forward(args)
    with _jax.named_scope("loss_head"):
        err = _jnp.square(y.astype(_jnp.float32) - loss_target)
        return 0.5 * _jnp.sum(_jnp.mean(err, axis=-1)) if err.ndim else 0.5 * err


def _adamw(w, g, m, v):
    m = ADAM_B1 * m + (1.0 - ADAM_B1) * g
    v = ADAM_B2 * v + (1.0 - ADAM_B2) * _jnp.square(g)
    m_hat = m / (1.0 - ADAM_B1 ** ADAM_STEP)
    v_hat = v / (1.0 - ADAM_B2 ** ADAM_STEP)
    delta = -ADAM_LR * (m_hat / (_jnp.sqrt(v_hat) + ADAM_EPS) + ADAM_WD * w)
    return delta, m, v


def reference(x, c, ada_w, ada_b, ln_g, ln_b, ffn_w_in, ffn_w_out, ev_w_in, ssd_conv_w, ssd_conv_b, ssd_dt_bias, ssd_a_log, ssd_d, ssd_norm_g, pool_w, pool_scale, ev_w_out, od_w_in, conf_dw_w, conf_dw_b, conf_ln_g, conf_ln_b, lru_conv_w, lru_conv_b, lru_wa, lru_ba, lru_wx, lru_bx, lru_lambda, od_w_out, loss_target, m_ada_w, m_ada_b, m_ln_g, m_ln_b, m_ffn_w_in, m_ffn_w_out, m_ev_w_in, m_ssd_conv_w, m_ssd_conv_b, m_ssd_dt_bias, m_ssd_a_log, m_ssd_d, m_ssd_norm_g, m_pool_w, m_pool_scale, m_ev_w_out, m_od_w_in, m_conf_dw_w, m_conf_dw_b, m_conf_ln_g, m_conf_ln_b, m_lru_conv_w, m_lru_conv_b, m_lru_wa, m_lru_ba, m_lru_wx, m_lru_bx, m_lru_lambda, m_od_w_out, v_ada_w, v_ada_b, v_ln_g, v_ln_b, v_ffn_w_in, v_ffn_w_out, v_ev_w_in, v_ssd_conv_w, v_ssd_conv_b, v_ssd_dt_bias, v_ssd_a_log, v_ssd_d, v_ssd_norm_g, v_pool_w, v_pool_scale, v_ev_w_out, v_od_w_in, v_conf_dw_w, v_conf_dw_b, v_conf_ln_g, v_conf_ln_b, v_lru_conv_w, v_lru_conv_b, v_lru_wa, v_lru_ba, v_lru_wx, v_lru_bx, v_lru_lambda, v_od_w_out):
    given = dict(x=x, c=c, ada_w=ada_w, ada_b=ada_b, ln_g=ln_g, ln_b=ln_b, ffn_w_in=ffn_w_in, ffn_w_out=ffn_w_out, ev_w_in=ev_w_in, ssd_conv_w=ssd_conv_w, ssd_conv_b=ssd_conv_b, ssd_dt_bias=ssd_dt_bias, ssd_a_log=ssd_a_log, ssd_d=ssd_d, ssd_norm_g=ssd_norm_g, pool_w=pool_w, pool_scale=pool_scale, ev_w_out=ev_w_out, od_w_in=od_w_in, conf_dw_w=conf_dw_w, conf_dw_b=conf_dw_b, conf_ln_g=conf_ln_g, conf_ln_b=conf_ln_b, lru_conv_w=lru_conv_w, lru_conv_b=lru_conv_b, lru_wa=lru_wa, lru_ba=lru_ba, lru_wx=lru_wx, lru_bx=lru_bx, lru_lambda=lru_lambda, od_w_out=od_w_out, loss_target=loss_target, m_ada_w=m_ada_w, m_ada_b=m_ada_b, m_ln_g=m_ln_g, m_ln_b=m_ln_b, m_ffn_w_in=m_ffn_w_in, m_ffn_w_out=m_ffn_w_out, m_ev_w_in=m_ev_w_in, m_ssd_conv_w=m_ssd_conv_w, m_ssd_conv_b=m_ssd_conv_b, m_ssd_dt_bias=m_ssd_dt_bias, m_ssd_a_log=m_ssd_a_log, m_ssd_d=m_ssd_d, m_ssd_norm_g=m_ssd_norm_g, m_pool_w=m_pool_w, m_pool_scale=m_pool_scale, m_ev_w_out=m_ev_w_out, m_od_w_in=m_od_w_in, m_conf_dw_w=m_conf_dw_w, m_conf_dw_b=m_conf_dw_b, m_conf_ln_g=m_conf_ln_g, m_conf_ln_b=m_conf_ln_b, m_lru_conv_w=m_lru_conv_w, m_lru_conv_b=m_lru_conv_b, m_lru_wa=m_lru_wa, m_lru_ba=m_lru_ba, m_lru_wx=m_lru_wx, m_lru_bx=m_lru_bx, m_lru_lambda=m_lru_lambda, m_od_w_out=m_od_w_out, v_ada_w=v_ada_w, v_ada_b=v_ada_b, v_ln_g=v_ln_g, v_ln_b=v_ln_b, v_ffn_w_in=v_ffn_w_in, v_ffn_w_out=v_ffn_w_out, v_ev_w_in=v_ev_w_in, v_ssd_conv_w=v_ssd_conv_w, v_ssd_conv_b=v_ssd_conv_b, v_ssd_dt_bias=v_ssd_dt_bias, v_ssd_a_log=v_ssd_a_log, v_ssd_d=v_ssd_d, v_ssd_norm_g=v_ssd_norm_g, v_pool_w=v_pool_w, v_pool_scale=v_pool_scale, v_ev_w_out=v_ev_w_out, v_od_w_in=v_od_w_in, v_conf_dw_w=v_conf_dw_w, v_conf_dw_b=v_conf_dw_b, v_conf_ln_g=v_conf_ln_g, v_conf_ln_b=v_conf_ln_b, v_lru_conv_w=v_lru_conv_w, v_lru_conv_b=v_lru_conv_b, v_lru_wa=v_lru_wa, v_lru_ba=v_lru_ba, v_lru_wx=v_lru_wx, v_lru_bx=v_lru_bx, v_lru_lambda=v_lru_lambda, v_od_w_out=v_od_w_out)
    weights = {n: given[n] for n in TWIN_WEIGHTS}
    shared = {n: given[n] for n in SHARED_INPUTS}
    per_example = {n: given[n] for n in ['x', 'c']}
    grad_fn = _jax.value_and_grad(_loss, argnums=(0, 1))

    def one_microbatch(ex, loss_target):
        ex = dict(ex)
        diff = ex.pop(TWIN_DIFF_INPUT)
        return grad_fn(weights, diff, {**shared, **ex}, loss_target)

    if N_MICROBATCH == 1:
        loss, (grad_w, grad_x) = one_microbatch(per_example, given["loss_target"])
    else:
        def body(carry, xs):
            loss_sum, grad_sum = carry
            l_k, (gw_k, gx_k) = one_microbatch(xs[0], xs[1])
            with _jax.named_scope("update"):
                return (loss_sum + l_k, _jax.tree.map(_jnp.add, grad_sum, gw_k)), gx_k

        init = (_jnp.zeros((), _jnp.float32), _jax.tree.map(_jnp.zeros_like, weights))
        (loss, grad_w), grad_x = _jax.lax.scan(body, init, (per_example, given["loss_target"]))
    with _jax.named_scope("update"):
        delta_w, new_m, new_v = {}, {}, {}
        for n in TWIN_WEIGHTS:
            delta_w[n], new_m[n], new_v[n] = _adamw(weights[n], grad_w[n], given["m_" + n], given["v_" + n])
    return (loss, grad_x, *[grad_w[n] for n in TWIN_WEIGHTS], *[delta_w[n] for n in TWIN_WEIGHTS],
            *[new_m[n] for n in TWIN_WEIGHTS], *[new_v[n] for n in TWIN_WEIGHTS])
```

```python
import functools
import math

import jax
import jax.numpy as jnp
from jax import lax
from jax.experimental import pallas as pl
from jax.experimental.pallas import tpu as pltpu

F32 = jnp.float32
BF16 = jnp.bfloat16
HI = lax.Precision.HIGHEST

N_DEV = 8
D = 1024
DEPTH = 4
N_MOD = 9
FF = 2816
ALPHA = (2.0 * DEPTH) ** 0.25
EPS = 1e-5
SSD_Q = 128
SSD_HEADS = 16
SSD_P = 64
SSD_N = 128
SSD_XBC = 1536
POOL_WINDOWS = (2, 4, 8, 16)
POOL_DIM = 512
CONF_DIM = 512
CONF_K = 31
LRU_DIM = 1024
LRU_HEADS = 8
LRU_C = 8.0
EV_SPLITS = (1024, 1536, 512, 128)
OD_SPLITS = (1024, 1024, 1024)
LR, B1, B2, AEPS, WD, STEP = 0.001, 0.9, 0.999, 1e-08, 0.01, 10

LANE = 128
SUB = 8
MIB = 1024 * 1024
VMEM_LIMIT = 48 * MIB
TM = 512


def _params(sem, vmem=VMEM_LIMIT):
    return pltpu.CompilerParams(dimension_semantics=sem, vmem_limit_bytes=vmem)


def _sds(shape, dtype):
    return jax.ShapeDtypeStruct(shape, dtype)


def _modulate(x, sh, sc):
    return x * (1.0 + sc) + sh


def _postnorm(x, y, g, lng, lnb, *, coef):
    z = ALPHA * x + coef * (1.0 + g) * y
    mu = jnp.mean(z, axis=-1, keepdims=True)
    zc = z - mu
    var = jnp.mean(zc * zc, axis=-1, keepdims=True)
    return zc * lax.rsqrt(var + EPS) * lng + lnb


def ffn_up(x, sh, sc, w, seq, name):
    n = x.shape[0]
    tn = FF // 2
    nj = FF // tn
    tps = seq // TM

    def body(x_ref, sh_ref, sc_ref, wg_ref, wu_ref, h_ref, g_ref, u_ref, a_ref):
        @pl.when(pl.program_id(1) == 0)
        def _():
            h_ref[...] = _modulate(x_ref[...], sh_ref[0], sc_ref[0]).astype(BF16)

        h = h_ref[...]
        g = jnp.dot(h, wg_ref[...], preferred_element_type=F32)
        u = jnp.dot(h, wu_ref[...], preferred_element_type=F32)
        g_ref[...] = g.astype(BF16)
        u_ref[...] = u.astype(BF16)
        a_ref[...] = (g * jax.nn.sigmoid(g) * u).astype(BF16)

    vec = pl.BlockSpec((1, 1, D), lambda i, j: (i // tps, 0, 0))
    col = pl.BlockSpec((TM, tn), lambda i, j: (i, j))
    return pl.pallas_call(
        body,
        grid=(n // TM, nj),
        in_specs=[
            pl.BlockSpec((TM, D), lambda i, j: (i, 0)),
            vec,
            vec,
            pl.BlockSpec((D, tn), lambda i, j: (0, j)),
            pl.BlockSpec((D, tn), lambda i, j: (0, j + nj)),
        ],
        out_specs=[pl.BlockSpec((TM, D), lambda i, j: (i, 0)), col, col, col],
        out_shape=[_sds((n, D), BF16), _sds((n, FF), BF16), _sds((n, FF), BF16), _sds((n, FF), BF16)],
        compiler_params=_params(("arbitrary", "arbitrary")),
        name=name,
    )(x, sh, sc, w, w)


def mod_mm(x, sh, sc, w, splits, seq, name):
    n = x.shape[0]
    m = w.shape[1]
    tps = seq // TM
    offs = [sum(splits[:k]) for k in range(len(splits))]

    def body(x_ref, sh_ref, sc_ref, w_ref, h_ref, *outs):
        h = _modulate(x_ref[...], sh_ref[0], sc_ref[0]).astype(BF16)
        h_ref[...] = h
        for o_ref, off, wd in zip(outs, offs, splits):
            o_ref[...] = jnp.dot(h, w_ref[:, off : off + wd], preferred_element_type=F32)

    vec = pl.BlockSpec((1, 1, D), lambda i: (i // tps, 0, 0))
    return pl.pallas_call(
        body,
        grid=(n // TM,),
        in_specs=[pl.BlockSpec((TM, D), lambda i: (i, 0)), vec, vec, pl.BlockSpec((D, m), lambda i: (0, 0))],
        out_specs=[pl.BlockSpec((TM, D), lambda i: (i, 0))] + [pl.BlockSpec((TM, wd), lambda i: (i, 0)) for wd in splits],
        out_shape=[_sds((n, D), BF16)] + [_sds((n, wd), F32) for wd in splits],
        compiler_params=_params(("arbitrary",)),
        name=name,
    )(x, sh, sc, w)


def mm_postnorm(parts, w, x, g, lng, lnb, coef, seq, name):
    n = x.shape[0]
    tps = seq // TM
    ks = [p.shape[1] for p in parts]
    offs = [sum(ks[:k]) for k in range(len(ks))]
    npart = len(parts)

    def body(*refs):
        a_refs = refs[:npart]
        w_ref, x_ref, g_ref, lng_ref, lnb_ref, xn_ref, y_ref = refs[npart:]
        y = None
        for a_ref, off, k in zip(a_refs, offs, ks):
            t = jnp.dot(a_ref[...], w_ref[off : off + k, :], preferred_element_type=F32)
            y = t if y is None else y + t
        y_ref[...] = y
        xn_ref[...] = _postnorm(x_ref[...], y, g_ref[0], lng_ref[...], lnb_ref[...], coef=coef)

    row = pl.BlockSpec((TM, D), lambda i: (i, 0))
    one = pl.BlockSpec((1, D), lambda i: (0, 0))
    return pl.pallas_call(
        body,
        grid=(n // TM,),
        in_specs=[pl.BlockSpec((TM, k), lambda i: (i, 0)) for k in ks]
        + [pl.BlockSpec((sum(ks), D), lambda i: (0, 0)), row, pl.BlockSpec((1, 1, D), lambda i: (i // tps, 0, 0)), one, one],
        out_specs=[row, row],
        out_shape=[_sds((n, D), F32), _sds((n, D), F32)],
        compiler_params=_params(("arbitrary",)),
        name=name,
    )(*parts, w, x, g, lng, lnb)


def postnorm_bwd(dxn, x, y, g, lng, lnb, w, ks, coef, seq, name):
    n = x.shape[0]
    tps = seq // TM
    nb = n // seq
    offs = [sum(ks[:k]) for k in range(len(ks))]
    npart = len(ks)
    f = functools.partial(_postnorm, coef=coef)

    def body(dxn_ref, x_ref, y_ref, g_ref, lng_ref, lnb_ref, w_ref, dx_ref, dy_ref, *rest):
        da_refs = rest[:npart]
        dg_ref, dlng_ref, dlnb_ref = rest[npart:]
        i = pl.program_id(0)
        _, vjp = jax.vjp(f, x_ref[...], y_ref[...], g_ref[0], lng_ref[...], lnb_ref[...])
        dx, dy, dg, dlng, dlnb = vjp(dxn_ref[...])
        dx_ref[...] = dx
        dyb = dy.astype(BF16)
        dy_ref[...] = dyb
        for da_ref, off, k in zip(da_refs, offs, ks):
            da_ref[...] = lax.dot_general(
                dyb, w_ref[off : off + k, :], (((1,), (1,)), ((), ())), preferred_element_type=F32
            ).astype(BF16)

        @pl.when(i % tps == 0)
        def _():
            dg_ref[...] = jnp.zeros_like(dg_ref)

        @pl.when(i == 0)
        def _():
            dlng_ref[...] = jnp.zeros_like(dlng_ref)
            dlnb_ref[...] = jnp.zeros_like(dlnb_ref)

        dg_ref[0] += dg
        dlng_ref[...] += dlng
        dlnb_ref[...] += dlnb

    row = pl.BlockSpec((TM, D), lambda i: (i, 0))
    one = pl.BlockSpec((1, D), lambda i: (0, 0))
    vec = pl.BlockSpec((1, 1, D), lambda i: (i // tps, 0, 0))
    return pl.pallas_call(
        body,
        grid=(n // TM,),
        in_specs=[row, row, row, vec, one, one, pl.BlockSpec((sum(ks), D), lambda i: (0, 0))],
        out_specs=[row, row] + [pl.BlockSpec((TM, k), lambda i: (i, 0)) for k in ks] + [vec, one, one],
        out_shape=[_sds((n, D), F32), _sds((n, D), BF16)]
        + [_sds((n, k), BF16) for k in ks]
        + [_sds((nb, 1, D), F32), _sds((1, D), F32), _sds((1, D), F32)],
        compiler_params=_params(("arbitrary",)),
        name=name,
    )(dxn, x, y, g, lng, lnb, w)


def _mod_bwd_finish(dh, x_ref, sc_ref, dxres_ref, dx_ref, dsh_ref, dsc_ref, first_of_seq):
    dx_ref[...] = dxres_ref[...] + dh * (1.0 + sc_ref[0])

    @pl.when(first_of_seq)
    def _():
        dsh_ref[...] = jnp.zeros_like(dsh_ref)
        dsc_ref[...] = jnp.zeros_like(dsc_ref)

    dsh_ref[0] += jnp.sum(dh, axis=0, keepdims=True)
    dsc_ref[0] += jnp.sum(dh * x_ref[...], axis=0, keepdims=True)


def ffn_bwd_in(da, g, u, w, x, sc, dxres, seq, name):
    n = x.shape[0]
    tn = FF // 2
    nj = FF // tn
    tm = TM // 2
    tps = seq // tm
    nb = n // seq

    def body(da_ref, g_ref, u_ref, wg_ref, wu_ref, x_ref, sc_ref, dxres_ref, dg_ref, du_ref, dx_ref, dsh_ref, dsc_ref, acc):
        i, j = pl.program_id(0), pl.program_id(1)
        gv = g_ref[...].astype(F32)
        uv = u_ref[...].astype(F32)
        dav = da_ref[...].astype(F32)
        s = jax.nn.sigmoid(gv)
        dgv = (dav * uv * s * (1.0 + gv * (1.0 - s))).astype(BF16)
        duv = (dav * gv * s).astype(BF16)
        dg_ref[...] = dgv
        du_ref[...] = duv
        nt = (((1,), (1,)), ((), ()))
        t = lax.dot_general(dgv, wg_ref[...], nt, preferred_element_type=F32) + lax.dot_general(
            duv, wu_ref[...], nt, preferred_element_type=F32
        )

        @pl.when(j == 0)
        def _():
            acc[...] = t

        @pl.when(j > 0)
        def _():
            acc[...] += t

        @pl.when(j == nj - 1)
        def _():
            _mod_bwd_finish(acc[...], x_ref, sc_ref, dxres_ref, dx_ref, dsh_ref, dsc_ref, i % tps == 0)

    row = pl.BlockSpec((tm, D), lambda i, j: (i, 0))
    col = pl.BlockSpec((tm, tn), lambda i, j: (i, j))
    vec = pl.BlockSpec((1, 1, D), lambda i, j: (i // tps, 0, 0))
    return pl.pallas_call(
        body,
        grid=(n // tm, nj),
        in_specs=[
            col,
            col,
            col,
            pl.BlockSpec((D, tn), lambda i, j: (0, j)),
            pl.BlockSpec((D, tn), lambda i, j: (0, j + nj)),
            row,
            vec,
            row,
        ],
        out_specs=[col, col, row, vec, vec],
        out_shape=[_sds((n, FF), BF16), _sds((n, FF), BF16), _sds((n, D), F32), _sds((nb, 1, D), F32), _sds((nb, 1, D), F32)],
        scratch_shapes=[pltpu.VMEM((tm, D), F32)],
        compiler_params=_params(("arbitrary", "arbitrary")),
        name=name,
    )(da, g, u, w, w, x, sc, dxres)


def proj_bwd_in(dparts, w, x, sc, dxres, seq, name):
    n = x.shape[0]
    tps = seq // TM
    nb = n // seq
    ms = [p.shape[1] for p in dparts]
    offs = [sum(ms[:k]) for k in range(len(ms))]
    npart = len(ms)

    def body(*refs):
        d_refs = refs[:npart]
        w_ref, x_ref, sc_ref, dxres_ref, dx_ref, dsh_ref, dsc_ref = refs[npart:]
        dh = None
        for d_ref, off, m in zip(d_refs, offs, ms):
            t = lax.dot_general(d_ref[...], w_ref[:, off : off + m], (((1,), (1,)), ((), ())), preferred_element_type=F32)
            dh = t if dh is None else dh + t
        _mod_bwd_finish(dh, x_ref, sc_ref, dxres_ref, dx_ref, dsh_ref, dsc_ref, pl.program_id(0) % tps == 0)

    row = pl.BlockSpec((TM, D), lambda i: (i, 0))
    vec = pl.BlockSpec((1, 1, D), lambda i: (i // tps, 0, 0))
    return pl.pallas_call(
        body,
        grid=(n // TM,),
        in_specs=[pl.BlockSpec((TM, m), lambda i: (i, 0)) for m in ms] + [pl.BlockSpec((D, sum(ms)), lambda i: (0, 0)), row, vec, row],
        out_specs=[row, vec, vec],
        out_shape=[_sds((n, D), F32), _sds((nb, 1, D), F32), _sds((nb, 1, D), F32)],
        compiler_params=_params(("arbitrary",)),
        name=name,
    )(*dparts, w, x, sc, dxres)


def mm_tn(a, b, name):
    n, k1 = a.shape
    k2 = b.shape[1]
    t1 = k1 if k1 <= 1536 else _tile(k1, 1536)
    t2 = k2 if k2 <= 1536 else _tile(k2, 1536)
    tk = 1024 if n % 1024 == 0 else n
    nk = n // tk

    def body(a_ref, b_ref, o_ref):
        t = lax.dot_general(a_ref[...], b_ref[...], (((0,), (0,)), ((), ())), preferred_element_type=F32)

        @pl.when(pl.program_id(2) == 0)
        def _():
            o_ref[...] = t

        @pl.when(pl.program_id(2) > 0)
        def _():
            o_ref[...] += t

    return pl.pallas_call(
        body,
        grid=(k1 // t1, k2 // t2, nk),
        in_specs=[pl.BlockSpec((tk, t1), lambda i, j, k: (k, i)), pl.BlockSpec((tk, t2), lambda i, j, k: (k, j))],
        out_specs=pl.BlockSpec((t1, t2), lambda i, j, k: (i, j)),
        out_shape=_sds((k1, k2), F32),
        compiler_params=_params(("arbitrary", "arbitrary", "arbitrary")),
        name=name,
    )(a, b)


def _tile(n, cap):
    best = LANE
    for t in range(LANE, cap + 1, LANE):
        if n % t == 0:
            best = t
    return best


def _peers():
    mx, my, mc = lax.axis_index("x"), lax.axis_index("y"), lax.axis_index("c")
    me = 4 * mx + 2 * my + mc
    out = []
    for k in range(1, N_DEV):
        px = 1 - mx if k & 4 else mx
        py = 1 - my if k & 2 else my
        pc = 1 - mc if k & 1 else mc
        out.append(((px, py, pc), 4 * px + 2 * py + pc))
    return me, out


def _exchange(x, gather, name):
    out_shape = (N_DEV,) + x.shape if gather else x.shape

    def body(x_ref, o_ref, send_sems, recv_sems, local_sem):
        me, peers = _peers()
        local = pltpu.make_async_copy(x_ref if gather else x_ref.at[me], o_ref.at[me], local_sem)
        local.start()
        sends = []
        for k, (pos, pid) in enumerate(peers):
            cp = pltpu.make_async_remote_copy(
                src_ref=x_ref if gather else x_ref.at[pid],
                dst_ref=o_ref.at[me],
                send_sem=send_sems.at[k],
                recv_sem=recv_sems.at[k],
                device_id=pos,
                device_id_type=pl.DeviceIdType.MESH,
            )
            cp.start()
            sends.append(cp)
        for k, (pos, pid) in enumerate(peers):
            pltpu.make_async_remote_copy(
                src_ref=x_ref if gather else x_ref.at[pid],
                dst_ref=o_ref.at[pid],
                send_sem=send_sems.at[k],
                recv_sem=recv_sems.at[k],
                device_id=pos,
                device_id_type=pl.DeviceIdType.MESH,
            ).wait_recv()
        for cp in sends:
            cp.wait_send()
        local.wait()

    hbm = pl.BlockSpec(memory_space=pltpu.HBM)
    return pl.pallas_call(
        body,
        in_specs=[hbm],
        out_specs=hbm,
        out_shape=_sds(out_shape, x.dtype),
        scratch_shapes=[pltpu.SemaphoreType.DMA((N_DEV - 1,)), pltpu.SemaphoreType.DMA((N_DEV - 1,)), pltpu.SemaphoreType.DMA],
        compiler_params=pltpu.CompilerParams(has_side_effects=True),
        name=name,
    )(x)


def all_gather(x, name):
    return _exchange(x, True, name)


def all_to_all(x, name):
    return _exchange(x, False, name)


def _conv_taps(ext_ref, w_ref, halo, tq, kk):
    acc = None
    for k in range(kk):
        t = w_ref[k : k + 1, :] * ext_ref[pl.ds(halo - (kk - 1 - k), tq), :]
        acc = t if acc is None else acc + t
    return acc


def _conv_taps_t(ext2_ref, w_ref, tq, kk):
    acc = None
    for k in range(kk):
        t = w_ref[k : k + 1, :] * ext2_ref[pl.ds(kk - 1 - k, tq), :]
        acc = t if acc is None else acc + t
    return acc


def _conv_dw(ext_ref, dy, dw_ref, halo, tq, kk):
    for k in range(kk):
        dw_ref[k : k + 1, :] += jnp.sum(dy * ext_ref[pl.ds(halo - (kk - 1 - k), tq), :], axis=0, keepdims=True)


def _halo_spec(rows, width, tq, shift):
    per = tq // rows

    if shift < 0:
        return lambda nblocks: pl.BlockSpec((rows, width), lambda i: (jnp.maximum(i * per - 1, 0), 0))
    return lambda nblocks: pl.BlockSpec((rows, width), lambda i: (jnp.minimum((i + 1) * per, nblocks - 1), 0))


def _ln(c, g, b):
    mu = jnp.mean(c, axis=-1, keepdims=True)
    cc = c - mu
    var = jnp.mean(cc * cc, axis=-1, keepdims=True)
    return cc * lax.rsqrt(var + EPS) * g + b


def _silu(v):
    return v * jax.nn.sigmoid(v)


def _ssd_chunk(conv, dtr, z, s, dt_bias, a_log, dskip, norm_g, tril, e):
    q = SSD_Q
    act = _silu(conv)
    xs, bm, cm = act[:, :1024], act[:, 1024:1280], act[:, 1280:1536]
    lane = lax.broadcasted_iota(jnp.int32, (1, LANE), 1)
    lane_q = lax.broadcasted_iota(jnp.int32, (q, LANE), 1)
    sub_q = lax.broadcasted_iota(jnp.int32, (LANE, q), 0)
    causal = lax.broadcasted_iota(jnp.int32, (q, q), 0) >= lax.broadcasted_iota(jnp.int32, (q, q), 1)
    real = lane < SSD_HEADS
    dt = jnp.where(real, jax.nn.softplus(dtr + dt_bias), 0.0)
    a = jnp.where(real, -jnp.exp(a_log), 0.0)
    da = dt * a
    acs = jnp.dot(tril, da, precision=HI, preferred_element_type=F32)
    acs_t = lax.dot_general(da, tril, (((0,), (1,)), ((), ())), precision=HI, preferred_element_type=F32)
    dt_e = jnp.dot(dt, e, precision=HI, preferred_element_type=F32)
    acs_e = jnp.dot(acs, e, precision=HI, preferred_element_type=F32)
    one8 = jnp.ones((SUB, 1), F32)
    alast_e = jnp.dot(one8 * jnp.sum(da, axis=0, keepdims=True), e, precision=HI, preferred_element_type=F32)[0:1]
    d_e = jnp.dot(one8 * jnp.where(real, dskip, 0.0), e, precision=HI, preferred_element_type=F32)[0:1]
    xdt = xs * dt_e
    nt = (((1,), (1,)), ((), ()))
    tn = (((0,), (0,)), ((), ()))
    ys, snews = [], []
    for g in range(2):
        gl = slice(g * 512, (g + 1) * 512)
        bg = bm[:, g * 128 : (g + 1) * 128].astype(BF16)
        cg = cm[:, g * 128 : (g + 1) * 128].astype(BF16)
        cb = lax.dot_general(cg, bg, nt, preferred_element_type=F32)
        sg = s[:, gl]
        yoff = jnp.dot(cg, sg.astype(BF16), preferred_element_type=F32) * jnp.exp(acs_e[:, gl])
        pairs = []
        for j in range(4):
            xp = xdt[:, g * 512 + j * 128 : g * 512 + (j + 1) * 128].astype(BF16)
            outs = []
            for hh in (g * 8 + 2 * j, g * 8 + 2 * j + 1):
                col = jnp.sum(jnp.where(lane_q == hh, acs, 0.0), axis=1, keepdims=True)
                row = jnp.sum(jnp.where(sub_q == hh, acs_t, 0.0), axis=0, keepdims=True)
                m = cb * jnp.exp(jnp.where(causal, col - row, -1e30))
                outs.append(jnp.dot(m.astype(BF16), xp, preferred_element_type=F32))
            pairs.append(jnp.where(lane_q < SSD_P, outs[0], outs[1]))
        ys.append(jnp.concatenate(pairs, axis=1) + yoff)
        decay = jnp.exp(alast_e[:, gl] - acs_e[:, gl])
        snews.append(
            sg * jnp.exp(alast_e[:, gl]) + lax.dot_general(bg, (xdt[:, gl] * decay).astype(BF16), tn, preferred_element_type=F32)
        )
    y = jnp.concatenate(ys, axis=1) + xs * d_e
    gated = y * _silu(z)
    out = gated * lax.rsqrt(jnp.mean(gated * gated, axis=-1, keepdims=True) + EPS) * norm_g
    return out, jnp.concatenate(snews, axis=1)


def _ssd_consts():
    tril = (lax.broadcasted_iota(jnp.int32, (SSD_Q, SSD_Q), 0) >= lax.broadcasted_iota(jnp.int32, (SSD_Q, SSD_Q), 1)).astype(F32)
    e = (lax.broadcasted_iota(jnp.int32, (LANE, 1024), 0) == lax.broadcasted_iota(jnp.int32, (LANE, 1024), 1) // SSD_P).astype(F32)
    return tril, e


def ssd_fwd(z, xbc, dtr, cw, cb, dt_bias, a_log, dskip, norm_g, seq, name):
    n = z.shape[0]
    q = SSD_Q
    nc = seq // q
    tril, e = _ssd_consts()

    def body(z_ref, xbc_ref, halo_ref, dtr_ref, cw_ref, cb_ref, dtb_ref, alog_ref, dsk_ref, ng_ref, tril_ref, e_ref, y_ref, sprev_ref, s_scr, ext):
        c = pl.program_id(0) % nc

        @pl.when(c == 0)
        def _():
            s_scr[...] = jnp.zeros_like(s_scr)

        ext[0:SUB, :] = jnp.where(c == 0, 0.0, halo_ref[...])
        ext[SUB:, :] = xbc_ref[...]
        conv = _conv_taps(ext, cw_ref, SUB, q, 4) + cb_ref[...]
        sprev_ref[0] = s_scr[...]
        y, snew = _ssd_chunk(conv, dtr_ref[...], z_ref[...], s_scr[...], dtb_ref[...], alog_ref[...], dsk_ref[...], ng_ref[...], tril_ref[...], e_ref[...])
        y_ref[...] = y.astype(BF16)
        s_scr[...] = snew

    def full(shape):
        return pl.BlockSpec(shape, lambda i: (0,) * len(shape))

    return pl.pallas_call(
        body,
        grid=(n // q,),
        in_specs=[
            pl.BlockSpec((q, 1024), lambda i: (i, 0)),
            pl.BlockSpec((q, SSD_XBC), lambda i: (i, 0)),
            _halo_spec(SUB, SSD_XBC, q, -1)(n // SUB),
            pl.BlockSpec((q, LANE), lambda i: (i, 0)),
            full((4, SSD_XBC)),
            full((1, SSD_XBC)),
            full((1, LANE)),
            full((1, LANE)),
            full((1, LANE)),
            full((1, 1024)),
            full((q, q)),
            full((LANE, 1024)),
        ],
        out_specs=[pl.BlockSpec((q, 1024), lambda i: (i, 0)), pl.BlockSpec((1, LANE, 1024), lambda i: (i, 0, 0))],
        out_shape=[_sds((n, 1024), BF16), _sds((n // q, LANE, 1024), F32)],
        scratch_shapes=[pltpu.VMEM((LANE, 1024), F32), pltpu.VMEM((SUB + q, SSD_XBC), F32)],
        compiler_params=_params(("arbitrary",)),
        name=name,
    )(z, xbc, xbc, dtr, cw, cb, dt_bias, a_log, dskip, norm_g, tril, e)


def ssd_bwd(dy, z, xbc, dtr, sprev, cw, cb, dt_bias, a_log, dskip, norm_g, seq, name):
    n = z.shape[0]
    q = SSD_Q
    nc = seq // q
    nchunks = n // q
    tril, e = _ssd_consts()

    def rev(i):
        return (i // nc) * nc + (nc - 1 - i % nc)

    def body(dy_ref, z_ref, xbc_ref, halo_ref, dtr_ref, sprev_ref, cw_ref, cb_ref, dtb_ref, alog_ref, dsk_ref, ng_ref, tril_ref, e_ref,
             dz_ref, dxbc_ref, ddt_ref, dcw_ref, dcb_ref, ddtb_ref, dalog_ref, ddsk_ref, dng_ref, ds_scr, ext, ext2):
        i = pl.program_id(0)
        step = i % nc
        c = nc - 1 - step

        @pl.when(step == 0)
        def _():
            ds_scr[...] = jnp.zeros_like(ds_scr)
            ext2[q:, :] = jnp.zeros((SUB, SSD_XBC), F32)

        @pl.when(i == 0)
        def _():
            for r in (dcw_ref, dcb_ref, ddtb_ref, dalog_ref, ddsk_ref, dng_ref):
                r[...] = jnp.zeros_like(r)

        ext[0:SUB, :] = jnp.where(c == 0, 0.0, halo_ref[...])
        ext[SUB:, :] = xbc_ref[...]
        conv = _conv_taps(ext, cw_ref, SUB, q, 4) + cb_ref[...]
        tril_v, e_v = tril_ref[...], e_ref[...]

        def f(conv, dtr, z, s, dtb, alog, dsk, ng):
            return _ssd_chunk(conv, dtr, z, s, dtb, alog, dsk, ng, tril_v, e_v)

        _, vjp = jax.vjp(f, conv, dtr_ref[...], z_ref[...], sprev_ref[0], dtb_ref[...], alog_ref[...], dsk_ref[...], ng_ref[...])
        dconv, ddtr, dz, dsprev, ddtb, dalog, ddsk, dng = vjp((dy_ref[...].astype(F32), ds_scr[...]))
        ds_scr[...] = dsprev
        dz_ref[...] = dz.astype(BF16)
        ddt_ref[...] = ddtr.astype(BF16)
        ext2[0:q, :] = dconv
        dxbc_ref[...] = _conv_taps_t(ext2, cw_ref, q, 4).astype(BF16)
        ext2[q:, :] = dconv[0:SUB, :]
        _conv_dw(ext, dconv, dcw_ref, SUB, q, 4)
        dcb_ref[...] += jnp.sum(dconv, axis=0, keepdims=True)
        ddtb_ref[...] += ddtb
        dalog_ref[...] += dalog
        ddsk_ref[...] += ddsk
        dng_ref[...] += dng

    def full(shape):
        return pl.BlockSpec(shape, lambda i: (0,) * len(shape))

    per = q // SUB
    return pl.pallas_call(
        body,
        grid=(nchunks,),
        in_specs=[
            pl.BlockSpec((q, 1024), lambda i: (rev(i), 0)),
            pl.BlockSpec((q, 1024), lambda i: (rev(i), 0)),
            pl.BlockSpec((q, SSD_XBC), lambda i: (rev(i), 0)),
            pl.BlockSpec((SUB, SSD_XBC), lambda i: (jnp.maximum(rev(i) * per - 1, 0), 0)),
            pl.BlockSpec((q, LANE), lambda i: (rev(i), 0)),
            pl.BlockSpec((1, LANE, 1024), lambda i: (rev(i), 0, 0)),
            full((4, SSD_XBC)),
            full((1, SSD_XBC)),
            full((1, LANE)),
            full((1, LANE)),
            full((1, LANE)),
            full((1, 1024)),
            full((q, q)),
            full((LANE, 1024)),
        ],
        out_specs=[
            pl.BlockSpec((q, 1024), lambda i: (rev(i), 0)),
            pl.BlockSpec((q, SSD_XBC), lambda i: (rev(i), 0)),
            pl.BlockSpec((q, LANE), lambda i: (rev(i), 0)),
            full((4, SSD_XBC)),
            full((1, SSD_XBC)),
            full((1, LANE)),
            full((1, LANE)),
            full((1, LANE)),
            full((1, 1024)),
        ],
        out_shape=[
            _sds((n, 1024), BF16),
            _sds((n, SSD_XBC), BF16),
            _sds((n, LANE), BF16),
            _sds((4, SSD_XBC), F32),
            _sds((1, SSD_XBC), F32),
            _sds((1, LANE), F32),
            _sds((1, LANE), F32),
            _sds((1, LANE), F32),
            _sds((1, 1024), F32),
        ],
        scratch_shapes=[pltpu.VMEM((LANE, 1024), F32), pltpu.VMEM((SUB + q, SSD_XBC), F32), pltpu.VMEM((q + SUB, SSD_XBC), F32)],
        compiler_params=_params(("arbitrary",)),
        name=name,
    )(dy, z, xbc, xbc, dtr, sprev, cw, cb, dt_bias, a_log, dskip, norm_g, tril, e)


POOL_HALO = 16
TQ = 512


def _pool_count(pos, w):
    return jnp.minimum(pos + 1.0, float(w))


def pool_fwd(u, pw, scale, seq, name):
    n = u.shape[0]
    tq, halo = TQ, POOL_HALO
    tps = seq // tq

    def body(u_ref, halo_ref, pw_ref, sc_ref, y_ref, ext):
        t0 = pl.program_id(0) % tps
        ext[0:halo, :] = jnp.where(t0 == 0, 0.0, halo_ref[...])
        ext[halo:, :] = u_ref[...]
        pos = (t0 * tq + lax.broadcasted_iota(jnp.int32, (tq, 1), 0)).astype(F32)
        for g, w in enumerate(POOL_WINDOWS):
            ln = slice(g * LANE, (g + 1) * LANE)
            acc = ext[pl.ds(halo, tq), ln]
            for j in range(1, w):
                acc = acc + ext[pl.ds(halo - j, tq), ln]
            pooled = acc / _pool_count(pos, w) - u_ref[:, ln]
            mixed = jnp.dot(pooled.astype(BF16), pw_ref[g].astype(BF16), preferred_element_type=F32)
            y_ref[:, ln] = (mixed * sc_ref[:, ln]).astype(BF16)

    return pl.pallas_call(
        body,
        grid=(n // tq,),
        in_specs=[
            pl.BlockSpec((tq, POOL_DIM), lambda i: (i, 0)),
            _halo_spec(halo, POOL_DIM, tq, -1)(n // halo),
            pl.BlockSpec((4, LANE, LANE), lambda i: (0, 0, 0)),
            pl.BlockSpec((1, POOL_DIM), lambda i: (0, 0)),
        ],
        out_specs=pl.BlockSpec((tq, POOL_DIM), lambda i: (i, 0)),
        out_shape=_sds((n, POOL_DIM), BF16),
        scratch_shapes=[pltpu.VMEM((halo + tq, POOL_DIM), F32)],
        compiler_params=_params(("arbitrary",)),
        name=name,
    )(u, u, pw, scale)


def pool_bwd(dy, u, pw, scale, seq, name):
    n = u.shape[0]
    tq, halo = TQ, POOL_HALO
    tps = seq // tq
    nt = (((1,), (1,)), ((), ()))
    tn = (((0,), (0,)), ((), ()))

    def body(dy_ref, dyn_ref, u_ref, halo_ref, pw_ref, sc_ref, du_ref, dpw_ref, dsc_ref, ext, ext2):
        i = pl.program_id(0)
        t0 = i % tps

        @pl.when(i == 0)
        def _():
            dpw_ref[...] = jnp.zeros_like(dpw_ref)
            dsc_ref[...] = jnp.zeros_like(dsc_ref)

        ext[0:halo, :] = jnp.where(t0 == 0, 0.0, halo_ref[...])
        ext[halo:, :] = u_ref[...]
        pos = (t0 * tq + lax.broadcasted_iota(jnp.int32, (tq, 1), 0)).astype(F32)
        dyv = dy_ref[...].astype(F32)
        dynv = jnp.where(t0 == tps - 1, 0.0, dyn_ref[...].astype(F32))
        for g, w in enumerate(POOL_WINDOWS):
            ln = slice(g * LANE, (g + 1) * LANE)
            wg = pw_ref[g].astype(BF16)
            acc = ext[pl.ds(halo, tq), ln]
            for j in range(1, w):
                acc = acc + ext[pl.ds(halo - j, tq), ln]
            pooled = (acc / _pool_count(pos, w) - u_ref[:, ln]).astype(BF16)
            mixed = jnp.dot(pooled, wg, preferred_element_type=F32)
            dsc_ref[:, ln] += jnp.sum(dyv[:, ln] * mixed, axis=0, keepdims=True)
            dmix = (dyv[:, ln] * sc_ref[:, ln]).astype(BF16)
            dpw_ref[g] += lax.dot_general(pooled, dmix, tn, preferred_element_type=F32)
            dpool = lax.dot_general(dmix, wg, nt, preferred_element_type=F32)
            dmix_n = (dynv[:, ln] * sc_ref[:, ln]).astype(BF16)
            dpool_n = lax.dot_general(dmix_n, wg, nt, preferred_element_type=F32)
            ext2[0:tq, ln] = dpool / _pool_count(pos, w)
            ext2[tq:, ln] = dpool_n * (1.0 / w)
            acc2 = ext2[pl.ds(0, tq), ln]
            for j in range(1, w):
                acc2 = acc2 + ext2[pl.ds(j, tq), ln]
            du_ref[:, ln] = (acc2 - dpool).astype(BF16)

    return pl.pallas_call(
        body,
        grid=(n // tq,),
        in_specs=[
            pl.BlockSpec((tq, POOL_DIM), lambda i: (i, 0)),
            _halo_spec(halo, POOL_DIM, tq, +1)(n // halo),
            pl.BlockSpec((tq, POOL_DIM), lambda i: (i, 0)),
            _halo_spec(halo, POOL_DIM, tq, -1)(n // halo),
            pl.BlockSpec((4, LANE, LANE), lambda i: (0, 0, 0)),
            pl.BlockSpec((1, POOL_DIM), lambda i: (0, 0)),
        ],
        out_specs=[
            pl.BlockSpec((tq, POOL_DIM), lambda i: (i, 0)),
            pl.BlockSpec((4, LANE, LANE), lambda i: (0, 0, 0)),
            pl.BlockSpec((1, POOL_DIM), lambda i: (0, 0)),
        ],
        out_shape=[_sds((n, POOL_DIM), BF16), _sds((4, LANE, LANE), F32), _sds((1, POOL_DIM), F32)],
        scratch_shapes=[pltpu.VMEM((halo + tq, POOL_DIM), F32), pltpu.VMEM((tq + halo, POOL_DIM), F32)],
        compiler_params=_params(("arbitrary",)),
        name=name,
    )(dy, dy, u, u, pw, scale)


CONF_HALO = 32


def _conf_post(c, g, b):
    return _silu(_ln(c, g, b))


def conf_fwd(vg, w, b, lng, lnb, seq, name):
    n = vg.shape[0]
    tq, halo, kk = TQ, CONF_HALO, CONF_K
    tps = seq // tq
    c = CONF_DIM

    def body(vg_ref, halo_ref, w_ref, b_ref, lng_ref, lnb_ref, y_ref, conv_ref, ext):
        t0 = pl.program_id(0) % tps
        hv = halo_ref[...]
        ext[0:halo, :] = jnp.where(t0 == 0, 0.0, hv[:, :c] * jax.nn.sigmoid(hv[:, c:]))
        ext[halo:, :] = vg_ref[:, :c] * jax.nn.sigmoid(vg_ref[:, c:])
        conv = _conv_taps(ext, w_ref, halo, tq, kk) + b_ref[...]
        conv_ref[...] = conv
        y_ref[...] = _conf_post(conv, lng_ref[...], lnb_ref[...]).astype(BF16)

    one = pl.BlockSpec((1, c), lambda i: (0, 0))
    return pl.pallas_call(
        body,
        grid=(n // tq,),
        in_specs=[pl.BlockSpec((tq, 2 * c), lambda i: (i, 0)), _halo_spec(halo, 2 * c, tq, -1)(n // halo), pl.BlockSpec((kk, c), lambda i: (0, 0)), one, one, one],
        out_specs=[pl.BlockSpec((tq, c), lambda i: (i, 0)), pl.BlockSpec((tq, c), lambda i: (i, 0))],
        out_shape=[_sds((n, c), BF16), _sds((n, c), F32)],
        scratch_shapes=[pltpu.VMEM((halo + tq, c), F32)],
        compiler_params=_params(("arbitrary",)),
        name=name,
    )(vg, vg, w, b, lng, lnb)


def conf_bwd(dy, conv, vg, w, lng, lnb, seq, name):
    n = vg.shape[0]
    tq, halo, kk = TQ, CONF_HALO, CONF_K
    tps = seq // tq
    c = CONF_DIM

    def body(dy_ref, dyn_ref, conv_ref, convn_ref, vg_ref, halo_ref, w_ref, lng_ref, lnb_ref, dvg_ref, dw_ref, db_ref, dlng_ref, dlnb_ref, ext, ext2):
        i = pl.program_id(0)
        t0 = i % tps

        @pl.when(i == 0)
        def _():
            for r in (dw_ref, db_ref, dlng_ref, dlnb_ref):
                r[...] = jnp.zeros_like(r)

        _, vjp = jax.vjp(_conf_post, conv_ref[...], lng_ref[...], lnb_ref[...])
        dconv, dlng, dlnb = vjp(dy_ref[...].astype(F32))
        _, vjpn = jax.vjp(_conf_post, convn_ref[...], lng_ref[...], lnb_ref[...])
        dconv_n = vjpn(dyn_ref[...].astype(F32))[0]
        ext2[0:tq, :] = dconv
        ext2[tq:, :] = jnp.where(t0 == tps - 1, 0.0, dconv_n)
        dh = _conv_taps_t(ext2, w_ref, tq, kk)
        hv = halo_ref[...]
        ext[0:halo, :] = jnp.where(t0 == 0, 0.0, hv[:, :c] * jax.nn.sigmoid(hv[:, c:]))
        v = vg_ref[:, :c]
        s = jax.nn.sigmoid(vg_ref[:, c:])
        ext[halo:, :] = v * s
        _conv_dw(ext, dconv, dw_ref, halo, tq, kk)
        db_ref[...] += jnp.sum(dconv, axis=0, keepdims=True)
        dlng_ref[...] += dlng
        dlnb_ref[...] += dlnb
        dvg_ref[:, :c] = (dh * s).astype(BF16)
        dvg_ref[:, c:] = (dh * v * s * (1.0 - s)).astype(BF16)

    one = pl.BlockSpec((1, c), lambda i: (0, 0))
    tile = pl.BlockSpec((tq, c), lambda i: (i, 0))
    nxt = _halo_spec(halo, c, tq, +1)(n // halo)
    return pl.pallas_call(
        body,
        grid=(n // tq,),
        in_specs=[tile, nxt, tile, nxt, pl.BlockSpec((tq, 2 * c), lambda i: (i, 0)), _halo_spec(halo, 2 * c, tq, -1)(n // halo),
                  pl.BlockSpec((kk, c), lambda i: (0, 0)), one, one],
        out_specs=[pl.BlockSpec((tq, 2 * c), lambda i: (i, 0)), pl.BlockSpec((kk, c), lambda i: (0, 0)), one, one, one],
        out_shape=[_sds((n, 2 * c), BF16), _sds((kk, c), F32), _sds((1, c), F32), _sds((1, c), F32), _sds((1, c), F32)],
        scratch_shapes=[pltpu.VMEM((halo + tq, c), F32), pltpu.VMEM((tq + halo, c), F32)],
        compiler_params=_params(("arbitrary",)),
        name=name,
    )(dy, dy, conv, conv, vg, vg, w, lng, lnb)


TL = 256


def _expm1_neg(t):
    p = t * (1.0 + t * (1.0 / 2 + t * (1.0 / 6 + t * (1.0 / 24 + t * (1.0 / 120 + t * (1.0 / 720 + t * (1.0 / 5040)))))))
    return jnp.where(t > -0.35, p, jnp.exp(t) - 1.0)


def _lru_gate(xc, ra, ia, ba, bx, lam):
    r = jax.nn.sigmoid(ra + ba)
    i = jax.nn.sigmoid(ia + bx)
    log_a = -LRU_C * r * jax.nn.softplus(-lam)
    return jnp.exp(log_a), jnp.sqrt(-_expm1_neg(2.0 * log_a)) * (i * xc)


def _lru_out(h, gr):
    return h * jax.nn.gelu(gr)


def _scan_rows(a, b, tq, reverse):
    row = lax.broadcasted_iota(jnp.int32, (tq, 1), 0)
    d = 1
    while d < tq:
        sh = tq - d if reverse else d
        valid = (row < tq - d) if reverse else (row >= d)
        a_s = pltpu.roll(a, sh, 0)
        b_s = pltpu.roll(b, sh, 0)
        b = jnp.where(valid, a * b_s, 0.0) + b
        a = jnp.where(valid, a * a_s, a)
        d *= 2
    return a, b


def _row_of(v, r, tq):
    row = lax.broadcasted_iota(jnp.int32, (tq, 1), 0)
    return jnp.sum(jnp.where(row == r, v, 0.0), axis=0, keepdims=True)


def _head_mm(xc, w_ref):
    return jnp.concatenate(
        [
            jnp.dot(xc[:, h * LANE : (h + 1) * LANE].astype(BF16), w_ref[h].astype(BF16), preferred_element_type=F32)
            for h in range(LRU_HEADS)
        ],
        axis=1,
    )


def lru_fwd(xr, gr, cw, cb, wa, ba, wx, bx, lam, seq, name):
    n = xr.shape[0]
    tq = TL
    tps = seq // tq
    c = LRU_DIM

    def body(xr_ref, halo_ref, gr_ref, cw_ref, cb_ref, wa_ref, ba_ref, wx_ref, bx_ref, lam_ref, y_ref, h_ref, hc, ext):
        t0 = pl.program_id(0) % tps

        @pl.when(t0 == 0)
        def _():
            hc[...] = jnp.zeros_like(hc)

        ext[0:SUB, :] = jnp.where(t0 == 0, 0.0, halo_ref[...])
        ext[SUB:, :] = xr_ref[...]
        xc = _conv_taps(ext, cw_ref, SUB, tq, 4) + cb_ref[...]
        a, b = _lru_gate(xc, _head_mm(xc, wa_ref), _head_mm(xc, wx_ref), ba_ref[...], bx_ref[...], lam_ref[...])
        acum, h0 = _scan_rows(a, b, tq, False)
        h = h0 + acum * hc[0:1, :]
        h_ref[...] = h
        hc[0:1, :] = h_ref[tq - 1 : tq, :]
        y_ref[...] = _lru_out(h, gr_ref[...]).astype(BF16)

    one = pl.BlockSpec((1, c), lambda i: (0, 0))
    tile = pl.BlockSpec((tq, c), lambda i: (i, 0))
    hw = pl.BlockSpec((LRU_HEADS, LANE, LANE), lambda i: (0, 0, 0))
    return pl.pallas_call(
        body,
        grid=(n // tq,),
        in_specs=[tile, _halo_spec(SUB, c, tq, -1)(n // SUB), tile, pl.BlockSpec((4, c), lambda i: (0, 0)), one, hw, one, hw, one, one],
        out_specs=[tile, tile],
        out_shape=[_sds((n, c), BF16), _sds((n, c), F32)],
        scratch_shapes=[pltpu.VMEM((SUB, c), F32), pltpu.VMEM((SUB + tq, c), F32)],
        compiler_params=_params(("arbitrary",)),
        name=name,
    )(xr, xr, gr, cw, cb, wa, ba, wx, bx, lam)


def lru_bwd(dy, xr, gr, h, cw, cb, wa, ba, wx, bx, lam, seq, name):
    n = xr.shape[0]
    tq = TL
    tps = seq // tq
    ntile = n // tq
    c = LRU_DIM
    per = tq // SUB
    nt = (((1,), (1,)), ((), ()))
    tn = (((0,), (0,)), ((), ()))

    def rev(i):
        return (i // tps) * tps + (tps - 1 - i % tps)

    def body(dy_ref, xr_ref, halo_ref, gr_ref, h_ref, hprev_ref, cw_ref, cb_ref, wa_ref, ba_ref, wx_ref, bx_ref, lam_ref,
             dxr_ref, dgr_ref, dcw_ref, dcb_ref, dwa_ref, dba_ref, dwx_ref, dbx_ref, dlam_ref, carry, ext, ext2):
        i = pl.program_id(0)
        step = i % tps
        t0 = tps - 1 - step

        @pl.when(step == 0)
        def _():
            carry[...] = jnp.zeros_like(carry)
            ext2[tq:, :] = jnp.zeros((SUB, c), F32)

        @pl.when(i == 0)
        def _():
            for r in (dcw_ref, dcb_ref, dwa_ref, dba_ref, dwx_ref, dbx_ref, dlam_ref):
                r[...] = jnp.zeros_like(r)

        ext[0:SUB, :] = jnp.where(t0 == 0, 0.0, halo_ref[...])
        ext[SUB:, :] = xr_ref[...]
        xc = _conv_taps(ext, cw_ref, SUB, tq, 4) + cb_ref[...]
        (a, _), vjp_gate = jax.vjp(_lru_gate, xc, _head_mm(xc, wa_ref), _head_mm(xc, wx_ref), ba_ref[...], bx_ref[...], lam_ref[...])
        hv = h_ref[...]
        _, vjp_out = jax.vjp(_lru_out, hv, gr_ref[...])
        dh, dgr = vjp_out(dy_ref[...].astype(F32))
        dgr_ref[...] = dgr.astype(BF16)
        row = lax.broadcasted_iota(jnp.int32, (tq, 1), 0)
        a_up = jnp.where(row == tq - 1, carry[0:1, :], pltpu.roll(a, tq - 1, 0))
        acum, l0 = _scan_rows(a_up, dh, tq, True)
        lamv = l0 + acum * carry[1:2, :]
        carry[0:1, :] = _row_of(a, 0, tq)
        carry[1:2, :] = _row_of(lamv, 0, tq)
        hprev = jnp.where(row == 0, jnp.where(t0 == 0, 0.0, hprev_ref[SUB - 1 : SUB, :]), pltpu.roll(hv, 1, 0))
        dxc, dra, dia, dba, dbx, dlam = vjp_gate((lamv * hprev, lamv))
        dba_ref[...] += dba
        dbx_ref[...] += dbx
        dlam_ref[...] += dlam
        pieces = []
        for hh in range(LRU_HEADS):
            ln = slice(hh * LANE, (hh + 1) * LANE)
            xh = xc[:, ln].astype(BF16)
            drh = dra[:, ln].astype(BF16)
            dih = dia[:, ln].astype(BF16)
            dwa_ref[hh] += lax.dot_general(xh, drh, tn, preferred_element_type=F32)
            dwx_ref[hh] += lax.dot_general(xh, dih, tn, preferred_element_type=F32)
            pieces.append(
                lax.dot_general(drh, wa_ref[hh].astype(BF16), nt, preferred_element_type=F32)
                + lax.dot_general(dih, wx_ref[hh].astype(BF16), nt, preferred_element_type=F32)
            )
        dxc = dxc + jnp.concatenate(pieces, axis=1)
        ext2[0:tq, :] = dxc
        dxr_ref[...] = _conv_taps_t(ext2, cw_ref, tq, 4).astype(BF16)
        ext2[tq:, :] = ext2[0:SUB, :]
        _conv_dw(ext, dxc, dcw_ref, SUB, tq, 4)
        dcb_ref[...] += jnp.sum(dxc, axis=0, keepdims=True)

    one = pl.BlockSpec((1, c), lambda i: (0, 0))
    tile = pl.BlockSpec((tq, c), lambda i: (rev(i), 0))
    prev = pl.BlockSpec((SUB, c), lambda i: (jnp.maximum(rev(i) * per - 1, 0), 0))
    hw = pl.BlockSpec((LRU_HEADS, LANE, LANE), lambda i: (0, 0, 0))
    cw4 = pl.BlockSpec((4, c), lambda i: (0, 0))
    return pl.pallas_call(
        body,
        grid=(ntile,),
        in_specs=[tile, tile, prev, tile, tile, prev, cw4, one, hw, one, hw, one, one],
        out_specs=[tile, tile, cw4, one, hw, one, hw, one, one],
        out_shape=[_sds((n, c), BF16), _sds((n, c), BF16), _sds((4, c), F32), _sds((1, c), F32), _sds((LRU_HEADS, LANE, LANE), F32),
                   _sds((1, c), F32), _sds((LRU_HEADS, LANE, LANE), F32), _sds((1, c), F32), _sds((1, c), F32)],
        scratch_shapes=[pltpu.VMEM((SUB, c), F32), pltpu.VMEM((SUB + tq, c), F32), pltpu.VMEM((tq + SUB, c), F32)],
        compiler_params=_params(("arbitrary",)),
        name=name,
    )(dy, xr, xr, gr, h, h, cw, cb, wa, ba, wx, bx, lam)


def ada_fwd(c_all, w, b, name):
    nl, _, cols = w.shape
    nb = c_all.shape[0]

    def body(c_ref, w_ref, b_ref, o_ref):
        sc = _silu(c_ref[...]).astype(BF16)
        o_ref[0] = jnp.dot(sc, w_ref[0].astype(BF16), preferred_element_type=F32) + b_ref[0]

    return pl.pallas_call(
        body,
        grid=(nl,),
        in_specs=[pl.BlockSpec((nb, D), lambda l: (0, 0)), pl.BlockSpec((1, D, cols), lambda l: (l, 0, 0)), pl.BlockSpec((1, 1, cols), lambda l: (l, 0, 0))],
        out_specs=pl.BlockSpec((1, nb, cols), lambda l: (l, 0, 0)),
        out_shape=_sds((nl, nb, cols), F32),
        compiler_params=_params(("arbitrary",)),
        name=name,
    )(c_all, w, b)


def ada_bwd(c_all, dmod, name):
    nl, nb, cols = dmod.shape

    def body(c_ref, d_ref, o_ref):
        sc = _silu(c_ref[...]).astype(BF16)
        o_ref[0] = lax.dot_general(sc, d_ref[0].astype(BF16), (((0,), (0,)), ((), ())), preferred_element_type=F32)

    return pl.pallas_call(
        body,
        grid=(nl,),
        in_specs=[pl.BlockSpec((nb, D), lambda l: (0, 0)), pl.BlockSpec((1, nb, cols), lambda l: (l, 0, 0))],
        out_specs=pl.BlockSpec((1, D, cols), lambda l: (l, 0, 0)),
        out_shape=_sds((nl, D, cols), F32),
        compiler_params=_params(("arbitrary",)),
        name=name,
    )(c_all, dmod)


def loss_grad(y, target, name):
    n = y.shape[0]

    def body(y_ref, t_ref, dy_ref, l_ref, acc):
        i = pl.program_id(0)

        @pl.when(i == 0)
        def _():
            acc[...] = jnp.zeros_like(acc)

        e = y_ref[...] - t_ref[...]
        dy_ref[...] = e * (1.0 / D)
        acc[...] += jnp.sum(e * e, axis=0, keepdims=True)

        @pl.when(i == n // TM - 1)
        def _():
            l_ref[...] = jnp.full((1, LANE), 0.5 / D, F32) * jnp.sum(acc[...])

    row = pl.BlockSpec((TM, D), lambda i: (i, 0))
    return pl.pallas_call(
        body,
        grid=(n // TM,),
        in_specs=[row, row],
        out_specs=[row, pl.BlockSpec((1, LANE), lambda i: (0, 0))],
        out_shape=[_sds((n, D), F32), _sds((1, LANE), F32)],
        scratch_shapes=[pltpu.VMEM((1, D), F32)],
        compiler_params=_params(("arbitrary",)),
        name=name,
    )(y, target)


def sum_parts(parts, name):
    ns, r, _ = parts.shape
    tr = _row_tile(r, 1024)

    def body(p_ref, o_ref):
        acc = p_ref[0]
        for k in range(1, ns):
            acc = acc + p_ref[k]
        o_ref[...] = acc

    return pl.pallas_call(
        body,
        grid=(r // tr,),
        in_specs=[pl.BlockSpec((ns, tr, LANE), lambda i: (0, i, 0))],
        out_specs=pl.BlockSpec((tr, LANE), lambda i: (i, 0)),
        out_shape=_sds((r, LANE), F32),
        compiler_params=_params(("arbitrary",)),
        name=name,
    )(parts)


def _row_tile(r, cap):
    if r <= cap:
        return r
    best = None
    for t in range(16, cap + 1, 16):
        if r % t == 0:
            best = t
    assert best is not None, r
    return best


def adamw(w, m, v, gparts, name):
    r, c = w.shape
    ns = gparts.shape[0]
    tr = _row_tile(r, min(512, 256 * 1024 // c))
    c1 = 1.0 - B1**STEP
    c2 = 1.0 - B2**STEP

    def body(w_ref, m_ref, v_ref, g_ref, go_ref, d_ref, mo_ref, vo_ref):
        g = g_ref[0].astype(F32)
        for k in range(1, ns):
            g = g + g_ref[k].astype(F32)
        mn = B1 * m_ref[...] + (1.0 - B1) * g
        vn = B2 * v_ref[...] + (1.0 - B2) * (g * g)
        go_ref[...] = g
        mo_ref[...] = mn
        vo_ref[...] = vn
        d_ref[...] = -LR * ((mn / c1) / (jnp.sqrt(vn / c2) + AEPS) + WD * w_ref[...])

    tile = pl.BlockSpec((tr, c), lambda i: (i, 0))
    return pl.pallas_call(
        body,
        grid=(r // tr,),
        in_specs=[tile, tile, tile, pl.BlockSpec((ns, tr, c), lambda i: (0, i, 0))],
        out_specs=[tile, tile, tile, tile],
        out_shape=[_sds((r, c), F32)] * 4,
        compiler_params=_params(("arbitrary",)),
        name=name,
    )(w, m, v, gparts)


WEIGHTS = ["ada_w", "ada_b", "ln_g", "ln_b", "ffn_w_in", "ffn_w_out", "ev_w_in", "ssd_conv_w", "ssd_conv_b", "ssd_dt_bias",
           "ssd_a_log", "ssd_d", "ssd_norm_g", "pool_w", "pool_scale", "ev_w_out", "od_w_in", "conf_dw_w", "conf_dw_b",
           "conf_ln_g", "conf_ln_b", "lru_conv_w", "lru_conv_b", "lru_wa", "lru_ba", "lru_wx", "lru_bx", "lru_lambda", "od_w_out"]
BIG = ("ada_w", "ffn_w_in", "ffn_w_out", "ev_w_in", "ev_w_out", "od_w_in", "od_w_out")
SMALL = {
    "ada_b": ((4, 9216), None), "ln_g": ((4, 3, 1024), 2), "ln_b": ((4, 3, 1024), 2),
    "ssd_conv_w": ((2, 4, 1536), 2), "ssd_conv_b": ((2, 1536), None), "ssd_dt_bias": ((2, 16), None),
    "ssd_a_log": ((2, 16), None), "ssd_d": ((2, 16), None), "ssd_norm_g": ((2, 1024), None),
    "pool_w": ((2, 4, 128, 128), None), "pool_scale": ((2, 512), None),
    "conf_dw_w": ((2, 31, 512), 2), "conf_dw_b": ((2, 512), 1), "conf_ln_g": ((2, 512), 1), "conf_ln_b": ((2, 512), 1),
    "lru_conv_w": ((2, 4, 1024), 2), "lru_conv_b": ((2, 1024), 1), "lru_wa": ((2, 8, 128, 128), None),
    "lru_ba": ((2, 1024), 1), "lru_wx": ((2, 8, 128, 128), None), "lru_bx": ((2, 1024), 1), "lru_lambda": ((2, 1024), 1),
}
PACK_ROWS = 2 * SUB * LANE


def _pack(arrs):
    flat = jnp.concatenate([a.reshape(-1) for a in arrs])
    pad = (-flat.shape[0]) % PACK_ROWS
    return jnp.pad(flat, (0, pad)).reshape(-1, LANE)


def _unpack(buf, shapes, lead=()):
    flat = buf.reshape(lead + (-1,))
    out, off = [], 0
    for s in shapes:
        k = math.prod(s)
        out.append(flat[..., off : off + k].reshape(lead + tuple(s)))
        off += k
    return out


def _pad_lanes(v):
    return jnp.pad(v, (0, LANE - v.shape[0]))[None]


def kernel(x, c, ada_w, ada_b, ln_g, ln_b, ffn_w_in, ffn_w_out, ev_w_in, ssd_conv_w, ssd_conv_b, ssd_dt_bias, ssd_a_log, ssd_d, ssd_norm_g, pool_w, pool_scale, ev_w_out, od_w_in, conf_dw_w, conf_dw_b, conf_ln_g, conf_ln_b, lru_conv_w, lru_conv_b, lru_wa, lru_ba, lru_wx, lru_bx, lru_lambda, od_w_out, loss_target, m_ada_w, m_ada_b, m_ln_g, m_ln_b, m_ffn_w_in, m_ffn_w_out, m_ev_w_in, m_ssd_conv_w, m_ssd_conv_b, m_ssd_dt_bias, m_ssd_a_log, m_ssd_d, m_ssd_norm_g, m_pool_w, m_pool_scale, m_ev_w_out, m_od_w_in, m_conf_dw_w, m_conf_dw_b, m_conf_ln_g, m_conf_ln_b, m_lru_conv_w, m_lru_conv_b, m_lru_wa, m_lru_ba, m_lru_wx, m_lru_bx, m_lru_lambda, m_od_w_out, v_ada_w, v_ada_b, v_ln_g, v_ln_b, v_ffn_w_in, v_ffn_w_out, v_ev_w_in, v_ssd_conv_w, v_ssd_conv_b, v_ssd_dt_bias, v_ssd_a_log, v_ssd_d, v_ssd_norm_g, v_pool_w, v_pool_scale, v_ev_w_out, v_od_w_in, v_conf_dw_w, v_conf_dw_b, v_conf_ln_g, v_conf_ln_b, v_lru_conv_w, v_lru_conv_b, v_lru_wa, v_lru_ba, v_lru_wx, v_lru_bx, v_lru_lambda, v_od_w_out):
    p = dict(locals())
    nb, seq, _ = x.shape
    n = nb * seq
    me = 4 * lax.axis_index("x") + 2 * lax.axis_index("y") + lax.axis_index("c")
    sharded = [k for k, (_, ax) in SMALL.items() if ax is not None]

    def gather_cols(w, nm):
        g = all_gather(w.astype(BF16), nm)
        g = jnp.moveaxis(g, 0, -2)
        return g.reshape(g.shape[:-2] + (g.shape[-2] * g.shape[-1],))

    def gather_rows(w, nm):
        g = all_gather(w.astype(BF16), nm)
        g = jnp.moveaxis(g, 0, -3)
        return g.reshape(g.shape[:-3] + (g.shape[-3] * g.shape[-2], g.shape[-1]))

    w_ffn_in = gather_cols(ffn_w_in, "ag_ffn_in")
    w_ffn_out = gather_rows(ffn_w_out, "ag_ffn_out")
    w_ev = gather_cols(ev_w_in, "ag_ev_in")
    w_ev_in = jnp.concatenate([w_ev[..., :2560], w_ev[..., 2576:], jnp.pad(w_ev[..., 2560:2576], ((0, 0), (0, 0), (0, LANE - SSD_HEADS)))], axis=-1)
    w_ev_out = gather_rows(ev_w_out, "ag_ev_out")
    w_od_in = gather_cols(od_w_in, "ag_od_in")
    w_od_out = gather_rows(od_w_out, "ag_od_out")

    sm_local_shapes = [p[k].shape for k in sharded]
    sm_all = all_gather(_pack([p[k] for k in sharded] + [c]), "ag_small")
    got = _unpack(sm_all, sm_local_shapes + [c.shape], lead=(N_DEV,))
    full = {k: p[k] for k, (_, ax) in SMALL.items() if ax is None}
    for k, g in zip(sharded, got[:-1]):
        full[k] = jnp.moveaxis(g, 0, SMALL[k][1]).reshape(SMALL[k][0])
    c_all = got[-1].reshape(N_DEV * nb, D)

    cols = ada_w.shape[-1]
    ada_b_loc = lax.dynamic_slice_in_dim(ada_b, me * cols, cols, axis=1)[:, None, :]
    mod_cols = ada_fwd(c_all, ada_w, ada_b_loc, "ada_fwd")
    mod_x = all_to_all(mod_cols.reshape(DEPTH, N_DEV, nb, cols).transpose(1, 0, 2, 3), "a2a_mod")
    mod = mod_x.transpose(1, 2, 0, 3).reshape(DEPTH, nb, N_MOD, 1, D)

    def vec(l, j):
        return mod[l, :, j]

    def row(a):
        return a[None]

    xs = x.reshape(n, D)
    saved = []
    for l in range(DEPTH):
        s = {"x0": xs}
        s["h1"], s["g1"], s["u1"], s["a1"] = ffn_up(xs, vec(l, 0), vec(l, 1), w_ffn_in[l, 0], seq, "ffn_up")
        x1, s["y1"] = mm_postnorm([s["a1"]], w_ffn_out[l, 0], xs, vec(l, 2), row(full["ln_g"][l, 0]), row(full["ln_b"][l, 0]), 0.5, seq, "ffn_down")
        s["x1"] = x1
        e = l // 2
        if l % 2 == 0:
            s["h2"], s["z"], s["xbc"], s["u"], s["dtr"] = mod_mm(x1, vec(l, 3), vec(l, 4), w_ev_in[e], EV_SPLITS, seq, "ev_in")
            s["ya"], s["sprev"] = ssd_fwd(s["z"], s["xbc"], s["dtr"], full["ssd_conv_w"][e], row(full["ssd_conv_b"][e]), _pad_lanes(full["ssd_dt_bias"][e]),
                                          _pad_lanes(full["ssd_a_log"][e]), _pad_lanes(full["ssd_d"][e]), row(full["ssd_norm_g"][e]), seq, "ssd_fwd")
            s["yb"] = pool_fwd(s["u"], full["pool_w"][e], row(full["pool_scale"][e]), seq, "pool_fwd")
            w_out = w_ev_out[e]
        else:
            s["h2"], s["vg"], s["xr"], s["gr"] = mod_mm(x1, vec(l, 3), vec(l, 4), w_od_in[e], OD_SPLITS, seq, "od_in")
            s["ya"], s["conv"] = conf_fwd(s["vg"], full["conf_dw_w"][e], row(full["conf_dw_b"][e]), row(full["conf_ln_g"][e]), row(full["conf_ln_b"][e]), seq, "conf_fwd")
            s["yb"], s["hst"] = lru_fwd(s["xr"], s["gr"], full["lru_conv_w"][e], row(full["lru_conv_b"][e]), full["lru_wa"][e], row(full["lru_ba"][e]),
                                        full["lru_wx"][e], row(full["lru_bx"][e]), row(full["lru_lambda"][e]), seq, "lru_fwd")
            w_out = w_od_out[e]
        x2, s["y2"] = mm_postnorm([s["ya"], s["yb"]], w_out, x1, vec(l, 5), row(full["ln_g"][l, 1]), row(full["ln_b"][l, 1]), 1.0, seq, "mix_out")
        s["x2"] = x2
        s["h3"], s["g3"], s["u3"], s["a3"] = ffn_up(x2, vec(l, 6), vec(l, 7), w_ffn_in[l, 1], seq, "ffn_up")
        xs, s["y3"] = mm_postnorm([s["a3"]], w_ffn_out[l, 1], x2, vec(l, 8), row(full["ln_g"][l, 2]), row(full["ln_b"][l, 2]), 0.5, seq, "ffn_down")
        saved.append(s)

    dx, loss_row = loss_grad(xs, loss_target.reshape(n, D), "loss")
    loss = lax.psum(loss_row[0, 0], ("x", "y", "c"))

    sg = {k: [None] * shape[0] for k, (shape, _) in SMALL.items()}
    sg["ln_g"] = [[None] * 3 for _ in range(DEPTH)]
    sg["ln_b"] = [[None] * 3 for _ in range(DEPTH)]
    dmod = [[None] * N_MOD for _ in range(DEPTH)]
    gw_ffn_in = [[None, None] for _ in range(DEPTH)]
    gw_ffn_out = [[None, None] for _ in range(DEPTH)]
    gw_ev_in, gw_ev_out, gw_od_in, gw_od_out = [None] * 2, [None] * 2, [None] * 2, [None] * 2

    def cut_cols(g):
        r, cc = g.shape
        return g.reshape(r, N_DEV, cc // N_DEV).transpose(1, 0, 2).astype(BF16)

    def cut_rows(g):
        r, cc = g.shape
        return g.reshape(N_DEV, r // N_DEV, cc).astype(BF16)

    def ffn_backward(l, i, dxo, s, xin, hk, gk, uk, ak, yk, jbase, lnj):
        dxres, dy, (da,), dmod[l][jbase + 2], sg["ln_g"][l][lnj], sg["ln_b"][l][lnj] = _six(postnorm_bwd(
            dxo, xin, s[yk], vec(l, jbase + 2), row(full["ln_g"][l, lnj]), row(full["ln_b"][l, lnj]), w_ffn_out[l, i], [FF], 0.5, seq, "ffn_down_bwd"))
        dg, du, dxi, dmod[l][jbase], dmod[l][jbase + 1] = ffn_bwd_in(da, s[gk], s[uk], w_ffn_in[l, i], xin, vec(l, jbase + 1), dxres, seq, "ffn_up_bwd")
        gw_ffn_out[l][i] = cut_rows(mm_tn(s[ak], dy, "wg_ffn_out"))
        gw_ffn_in[l][i] = cut_cols(jnp.concatenate([mm_tn(s[hk], dg, "wg_ffn_in"), mm_tn(s[hk], du, "wg_ffn_in")], axis=1))
        return dxi

    for l in reversed(range(DEPTH)):
        s = saved[l]
        e = l // 2
        dx = ffn_backward(l, 1, dx, s, s["x2"], "h3", "g3", "u3", "a3", "y3", 6, 2)
        if l % 2 == 0:
            w_out, ks = w_ev_out[e], [1024, POOL_DIM]
        else:
            w_out, ks = w_od_out[e], [CONF_DIM, LRU_DIM]
        dxres, dy, (dya, dyb), dmod[l][5], sg["ln_g"][l][1], sg["ln_b"][l][1] = _six(postnorm_bwd(
            dx, s["x1"], s["y2"], vec(l, 5), row(full["ln_g"][l, 1]), row(full["ln_b"][l, 1]), w_out, ks, 1.0, seq, "mix_out_bwd"))
        g_out = cut_rows(jnp.concatenate([mm_tn(s["ya"], dy, "wg_mix_a"), mm_tn(s["yb"], dy, "wg_mix_b")], axis=0))
        if l % 2 == 0:
            gw_ev_out[e] = g_out
            (dz, dxbc, ddt, sg["ssd_conv_w"][e], dcb, ddtb, dalog, ddsk, dng) = ssd_bwd(
                dya, s["z"], s["xbc"], s["dtr"], s["sprev"], full["ssd_conv_w"][e], row(full["ssd_conv_b"][e]), _pad_lanes(full["ssd_dt_bias"][e]),
                _pad_lanes(full["ssd_a_log"][e]), _pad_lanes(full["ssd_d"][e]), row(full["ssd_norm_g"][e]), seq, "ssd_bwd")
            sg["ssd_conv_b"][e], sg["ssd_norm_g"][e] = dcb[0], dng[0]
            sg["ssd_dt_bias"][e], sg["ssd_a_log"][e], sg["ssd_d"][e] = ddtb[0, :SSD_HEADS], dalog[0, :SSD_HEADS], ddsk[0, :SSD_HEADS]
            du, sg["pool_w"][e], dps = pool_bwd(dyb, s["u"], full["pool_w"][e], row(full["pool_scale"][e]), seq, "pool_bwd")
            sg["pool_scale"][e] = dps[0]
            dparts = [dz, dxbc, du, ddt]
            dx, dmod[l][3], dmod[l][4] = proj_bwd_in(dparts, w_ev_in[e], s["x1"], vec(l, 4), dxres, seq, "ev_in_bwd")
            gz, gxbc, gu, gdt = [mm_tn(s["h2"], dp, "wg_ev_in") for dp in dparts]
            gw_ev_in[e] = cut_cols(jnp.concatenate([gz, gxbc, gdt[:, :SSD_HEADS], gu], axis=1))
        else:
            gw_od_out[e] = g_out
            dvg, sg["conf_dw_w"][e], dcb, dlg, dlb = conf_bwd(dya, s["conv"], s["vg"], full["conf_dw_w"][e], row(full["conf_ln_g"][e]), row(full["conf_ln_b"][e]), seq, "conf_bwd")
            sg["conf_dw_b"][e], sg["conf_ln_g"][e], sg["conf_ln_b"][e] = dcb[0], dlg[0], dlb[0]
            (dxr, dgr, sg["lru_conv_w"][e], dcb, sg["lru_wa"][e], dba, sg["lru_wx"][e], dbx, dlam) = lru_bwd(
                dyb, s["xr"], s["gr"], s["hst"], full["lru_conv_w"][e], row(full["lru_conv_b"][e]), full["lru_wa"][e], row(full["lru_ba"][e]),
                full["lru_wx"][e], row(full["lru_bx"][e]), row(full["lru_lambda"][e]), seq, "lru_bwd")
            sg["lru_conv_b"][e], sg["lru_ba"][e], sg["lru_bx"][e], sg["lru_lambda"][e] = dcb[0], dba[0], dbx[0], dlam[0]
            dparts = [dvg, dxr, dgr]
            dx, dmod[l][3], dmod[l][4] = proj_bwd_in(dparts, w_od_in[e], s["x1"], vec(l, 4), dxres, seq, "od_in_bwd")
            gw_od_in[e] = cut_cols(jnp.concatenate([mm_tn(s["h2"], dp, "wg_od_in") for dp in dparts], axis=1))
        dx = ffn_backward(l, 0, dx, s, s["x0"], "h1", "g1", "u1", "a1", "y1", 0, 0)
    grad_x = dx.reshape(nb, seq, D)

    dmod_mine = jnp.stack([jnp.concatenate([d[:, 0, :] for d in dmod[l]], axis=-1) for l in range(DEPTH)])
    sg["ada_b"] = [jnp.sum(dmod_mine[l], axis=0) for l in range(DEPTH)]
    dmod_x = all_to_all(dmod_mine.reshape(DEPTH, nb, N_DEV, cols).transpose(2, 0, 1, 3), "a2a_dmod")
    g_ada_w = ada_bwd(c_all, dmod_x.transpose(1, 0, 2, 3).reshape(DEPTH, N_DEV * nb, cols), "ada_bwd")

    sg["ln_g"] = [jnp.concatenate(r, axis=0) for r in sg["ln_g"]]
    sg["ln_b"] = [jnp.concatenate(r, axis=0) for r in sg["ln_b"]]
    small_names = list(SMALL)
    parts = all_gather(_pack([jnp.stack(sg[k]).reshape(SMALL[k][0]) for k in small_names]), "ag_smallgrad")
    summed = _unpack(sum_parts(parts, "sum_smallgrad"), [SMALL[k][0] for k in small_names])
    grads = {}
    for k, g in zip(small_names, summed):
        ax = SMALL[k][1]
        grads[k] = g if ax is None else lax.dynamic_slice_in_dim(g, me * p[k].shape[ax], p[k].shape[ax], axis=ax)
    loc_shapes = [p[k].shape for k in small_names]
    _, d_s, m_s, v_s = adamw(_pack([p[k] for k in small_names]), _pack([p["m_" + k] for k in small_names]), _pack([p["v_" + k] for k in small_names]),
                             _pack([grads[k] for k in small_names])[None], "adamw_small")
    delta = dict(zip(small_names, _unpack(d_s, loc_shapes)))
    new_m = dict(zip(small_names, _unpack(m_s, loc_shapes)))
    new_v = dict(zip(small_names, _unpack(v_s, loc_shapes)))

    big_parts = {
        "ada_w": g_ada_w[None],
        "ffn_w_in": all_to_all(jnp.stack([jnp.stack(r, axis=1) for r in gw_ffn_in], axis=1), "a2a_ffn_in"),
        "ffn_w_out": all_to_all(jnp.stack([jnp.stack(r, axis=1) for r in gw_ffn_out], axis=1), "a2a_ffn_out"),
        "ev_w_in": all_to_all(jnp.stack(gw_ev_in, axis=1), "a2a_ev_in"),
        "ev_w_out": all_to_all(jnp.stack(gw_ev_out, axis=1), "a2a_ev_out"),
        "od_w_in": all_to_all(jnp.stack(gw_od_in, axis=1), "a2a_od_in"),
        "od_w_out": all_to_all(jnp.stack(gw_od_out, axis=1), "a2a_od_out"),
    }
    for k in BIG:
        w = p[k]
        r2 = (math.prod(w.shape[:-1]), w.shape[-1])
        gp = big_parts[k]
        out = adamw(w.reshape(r2), p["m_" + k].reshape(r2), p["v_" + k].reshape(r2), gp.reshape((gp.shape[0],) + r2), "adamw_" + k)
        grads[k], delta[k], new_m[k], new_v[k] = [o.reshape(w.shape) for o in out]

    return (loss, grad_x, *[grads[k] for k in WEIGHTS], *[delta[k] for k in WEIGHTS], *[new_m[k] for k in WEIGHTS], *[new_v[k] for k in WEIGHTS])


def _six(outs):
    return outs[0], outs[1], tuple(outs[2:-3]), outs[-3], outs[-2], outs[-1]
```

```python
import functools
import math

import jax
import jax.numpy as jnp
from jax import lax
from jax.experimental import pallas as pl
from jax.experimental.pallas import tpu as pltpu

F32 = jnp.float32
BF16 = jnp.bfloat16
HI = lax.Precision.HIGHEST

N_DEV = 8
D = 1024
DEPTH = 4
N_MOD = 9
FF = 2816
ALPHA = (2.0 * DEPTH) ** 0.25
EPS = 1e-5
SSD_Q = 128
SSD_HEADS = 16
SSD_P = 64
SSD_N = 128
SSD_XBC = 1536
POOL_WINDOWS = (2, 4, 8, 16)
POOL_DIM = 512
CONF_DIM = 512
CONF_K = 31
LRU_DIM = 1024
LRU_HEADS = 8
LRU_C = 8.0
EV_SPLITS = (1024, 1536, 512, 128)
OD_SPLITS = (1024, 1024, 1024)
LR, B1, B2, AEPS, WD, STEP = 0.001, 0.9, 0.999, 1e-08, 0.01, 10

LANE = 128
SUB = 8
MIB = 1024 * 1024
VMEM_LIMIT = 48 * MIB
TM = 512


def _params(sem, vmem=VMEM_LIMIT):
    return pltpu.CompilerParams(dimension_semantics=sem, vmem_limit_bytes=vmem)


def _sds(shape, dtype):
    return jax.ShapeDtypeStruct(shape, dtype)


def _modulate(x, sh, sc):
    return x * (1.0 + sc) + sh


def _postnorm(x, y, g, lng, lnb, *, coef):
    z = ALPHA * x + coef * (1.0 + g) * y
    mu = jnp.mean(z, axis=-1, keepdims=True)
    zc = z - mu
    var = jnp.mean(zc * zc, axis=-1, keepdims=True)
    return zc * lax.rsqrt(var + EPS) * lng + lnb


def _peers():
    mx, my, mc = lax.axis_index("x"), lax.axis_index("y"), lax.axis_index("c")
    me = 4 * mx + 2 * my + mc
    out = []
    for k in range(1, N_DEV):
        px = 1 - mx if k & 4 else mx
        py = 1 - my if k & 2 else my
        pc = 1 - mc if k & 1 else mc
        out.append(((px, py, pc), 4 * px + 2 * py + pc))
    return me, out


def _carry_plan(items):
    hbm = pl.BlockSpec(memory_space=pltpu.HBM)
    k = len(items)
    shapes = [_sds((N_DEV,) + a.shape if g else a.shape, a.dtype) for a, g in items]
    scratch = [pltpu.SemaphoreType.DMA((k * (N_DEV - 1),)), pltpu.SemaphoreType.DMA((k * (N_DEV - 1),)), pltpu.SemaphoreType.DMA((k,))] if k else []
    return [hbm] * k, [hbm] * k, shapes, scratch


def _carry_copies(gathers, x_refs, o_refs, sems):
    send_sems, recv_sems, local_sems = sems
    me, peers = _peers()
    local, sends, recvs = [], [], []
    for a, (gather, x_ref, o_ref) in enumerate(zip(gathers, x_refs, o_refs)):
        local.append(pltpu.make_async_copy(x_ref if gather else x_ref.at[me], o_ref.at[me], local_sems.at[a]))
        for k, (pos, pid) in enumerate(peers):
            s = a * (N_DEV - 1) + k
            src = x_ref if gather else x_ref.at[pid]
            for dst, lst in ((o_ref.at[me], sends), (o_ref.at[pid], recvs)):
                lst.append(pltpu.make_async_remote_copy(src_ref=src, dst_ref=dst, send_sem=send_sems.at[s], recv_sem=recv_sems.at[s],
                                                        device_id=pos, device_id_type=pl.DeviceIdType.MESH))
    return local, sends, recvs


def _carry_start(gathers, x_refs, o_refs, sems):
    local, sends, _ = _carry_copies(gathers, x_refs, o_refs, sems)
    for cp in local + sends:
        cp.start()


def _carry_wait(gathers, x_refs, o_refs, sems):
    local, sends, recvs = _carry_copies(gathers, x_refs, o_refs, sems)
    for cp in recvs:
        cp.wait_recv()
    for cp in sends:
        cp.wait_send()
    for cp in local:
        cp.wait()


def exchange(items, name):
    gathers = [g for _, g in items]
    k = len(items)
    in_specs, out_specs, shapes, scratch = _carry_plan(items)

    def body(*refs):
        x_refs, o_refs, sems = refs[:k], refs[k : 2 * k], refs[2 * k :]
        _carry_start(gathers, x_refs, o_refs, sems)
        _carry_wait(gathers, x_refs, o_refs, sems)

    return pl.pallas_call(
        body,
        in_specs=in_specs,
        out_specs=out_specs,
        out_shape=shapes,
        scratch_shapes=scratch,
        compiler_params=pltpu.CompilerParams(has_side_effects=True),
        name=name,
    )(*[a for a, _ in items])


def ffn_up(x, sh, sc, w, seq, name, carry=()):
    n = x.shape[0]
    tn = FF // 2
    nj = FF // tn
    ni = n // TM
    tps = seq // TM
    k = len(carry)
    gathers = [g for _, g in carry]
    c_in, c_out, c_shapes, c_scratch = _carry_plan(carry)

    def body(x_ref, sh_ref, sc_ref, wg_ref, wu_ref, *rest):
        cx, (h_ref, g_ref, u_ref, a_ref), co, sems = rest[:k], rest[k : k + 4], rest[k + 4 : 2 * k + 4], rest[2 * k + 4 :]
        i, j = pl.program_id(0), pl.program_id(1)
        if k:
            @pl.when((i == 0) & (j == 0))
            def _():
                _carry_start(gathers, cx, co, sems)

        @pl.when(j == 0)
        def _():
            h_ref[...] = _modulate(x_ref[...], sh_ref[0], sc_ref[0]).astype(BF16)

        h = h_ref[...]
        g = jnp.dot(h, wg_ref[...], preferred_element_type=F32)
        u = jnp.dot(h, wu_ref[...], preferred_element_type=F32)
        g_ref[...] = g.astype(BF16)
        u_ref[...] = u.astype(BF16)
        a_ref[...] = (g * jax.nn.sigmoid(g) * u).astype(BF16)
        if k:
            @pl.when((i == ni - 1) & (j == nj - 1))
            def _():
                _carry_wait(gathers, cx, co, sems)

    vec = pl.BlockSpec((1, 1, D), lambda i, j: (i // tps, 0, 0))
    col = pl.BlockSpec((TM, tn), lambda i, j: (i, j))
    return pl.pallas_call(
        body,
        grid=(ni, nj),
        in_specs=[
            pl.BlockSpec((TM, D), lambda i, j: (i, 0)),
            vec,
            vec,
            pl.BlockSpec((D, tn), lambda i, j: (0, j)),
            pl.BlockSpec((D, tn), lambda i, j: (0, j + nj)),
        ]
        + c_in,
        out_specs=[pl.BlockSpec((TM, D), lambda i, j: (i, 0)), col, col, col] + c_out,
        out_shape=[_sds((n, D), BF16), _sds((n, FF), BF16), _sds((n, FF), BF16), _sds((n, FF), BF16)] + c_shapes,
        scratch_shapes=c_scratch,
        compiler_params=_params(("arbitrary", "arbitrary")),
        name=name,
    )(x, sh, sc, w, w, *[a for a, _ in carry])


def mod_mm(x, sh, sc, w, splits, seq, name):
    n = x.shape[0]
    m = w.shape[1]
    tps = seq // TM
    offs = [sum(splits[:k]) for k in range(len(splits))]

    def body(x_ref, sh_ref, sc_ref, w_ref, h_ref, *outs):
        h = _modulate(x_ref[...], sh_ref[0], sc_ref[0]).astype(BF16)
        h_ref[...] = h
        for o_ref, off, wd in zip(outs, offs, splits):
            o_ref[...] = jnp.dot(h, w_ref[:, off : off + wd], preferred_element_type=F32)

    vec = pl.BlockSpec((1, 1, D), lambda i: (i // tps, 0, 0))
    return pl.pallas_call(
        body,
        grid=(n // TM,),
        in_specs=[pl.BlockSpec((TM, D), lambda i: (i, 0)), vec, vec, pl.BlockSpec((D, m), lambda i: (0, 0))],
        out_specs=[pl.BlockSpec((TM, D), lambda i: (i, 0))] + [pl.BlockSpec((TM, wd), lambda i: (i, 0)) for wd in splits],
        out_shape=[_sds((n, D), BF16)] + [_sds((n, wd), F32) for wd in splits],
        compiler_params=_params(("arbitrary",)),
        name=name,
    )(x, sh, sc, w)


def mm_postnorm(parts, w, x, g, lng, lnb, coef, seq, name):
    n = x.shape[0]
    tps = seq // TM
    ks = [p.shape[1] for p in parts]
    offs = [sum(ks[:k]) for k in range(len(ks))]
    npart = len(parts)

    def body(*refs):
        a_refs = refs[:npart]
        w_ref, x_ref, g_ref, lng_ref, lnb_ref, xn_ref, y_ref = refs[npart:]
        y = None
        for a_ref, off, k in zip(a_refs, offs, ks):
            t = jnp.dot(a_ref[...], w_ref[off : off + k, :], preferred_element_type=F32)
            y = t if y is None else y + t
        y_ref[...] = y
        xn_ref[...] = _postnorm(x_ref[...], y, g_ref[0], lng_ref[...], lnb_ref[...], coef=coef)

    row = pl.BlockSpec((TM, D), lambda i: (i, 0))
    one = pl.BlockSpec((1, D), lambda i: (0, 0))
    return pl.pallas_call(
        body,
        grid=(n // TM,),
        in_specs=[pl.BlockSpec((TM, k), lambda i: (i, 0)) for k in ks]
        + [pl.BlockSpec((sum(ks), D), lambda i: (0, 0)), row, pl.BlockSpec((1, 1, D), lambda i: (i // tps, 0, 0)), one, one],
        out_specs=[row, row],
        out_shape=[_sds((n, D), F32), _sds((n, D), F32)],
        compiler_params=_params(("arbitrary",)),
        name=name,
    )(*parts, w, x, g, lng, lnb)


def postnorm_bwd(dxn, x, y, g, lng, lnb, w, ks, coef, seq, name):
    n = x.shape[0]
    tps = seq // TM
    nb = n // seq
    offs = [sum(ks[:k]) for k in range(len(ks))]
    npart = len(ks)
    f = functools.partial(_postnorm, coef=coef)

    def body(dxn_ref, x_ref, y_ref, g_ref, lng_ref, lnb_ref, w_ref, dx_ref, dy_ref, *rest):
        da_refs = rest[:npart]
        dg_ref, dlng_ref, dlnb_ref = rest[npart:]
        i = pl.program_id(0)
        _, vjp = jax.vjp(f, x_ref[...], y_ref[...], g_ref[0], lng_ref[...], lnb_ref[...])
        dx, dy, dg, dlng, dlnb = vjp(dxn_ref[...])
        dx_ref[...] = dx
        dyb = dy.astype(BF16)
        dy_ref[...] = dyb
        for da_ref, off, k in zip(da_refs, offs, ks):
            da_ref[...] = lax.dot_general(
                dyb, w_ref[off : off + k, :], (((1,), (1,)), ((), ())), preferred_element_type=F32
            ).astype(BF16)

        @pl.when(i % tps == 0)
        def _():
            dg_ref[...] = jnp.zeros_like(dg_ref)

        @pl.when(i == 0)
        def _():
            dlng_ref[...] = jnp.zeros_like(dlng_ref)
            dlnb_ref[...] = jnp.zeros_like(dlnb_ref)

        dg_ref[0] += dg
        dlng_ref[...] += dlng
        dlnb_ref[...] += dlnb

    row = pl.BlockSpec((TM, D), lambda i: (i, 0))
    one = pl.BlockSpec((1, D), lambda i: (0, 0))
    vec = pl.BlockSpec((1, 1, D), lambda i: (i // tps, 0, 0))
    return pl.pallas_call(
        body,
        grid=(n // TM,),
        in_specs=[row, row, row, vec, one, one, pl.BlockSpec((sum(ks), D), lambda i: (0, 0))],
        out_specs=[row, row] + [pl.BlockSpec((TM, k), lambda i: (i, 0)) for k in ks] + [vec, one, one],
        out_shape=[_sds((n, D), F32), _sds((n, D), BF16)]
        + [_sds((n, k), BF16) for k in ks]
        + [_sds((nb, 1, D), F32), _sds((1, D), F32), _sds((1, D), F32)],
        compiler_params=_params(("arbitrary",)),
        name=name,
    )(dxn, x, y, g, lng, lnb, w)


def _mod_bwd_finish(dh, x_ref, sc_ref, dxres_ref, dx_ref, dsh_ref, dsc_ref, first_of_seq):
    dx_ref[...] = dxres_ref[...] + dh * (1.0 + sc_ref[0])

    @pl.when(first_of_seq)
    def _():
        dsh_ref[...] = jnp.zeros_like(dsh_ref)
        dsc_ref[...] = jnp.zeros_like(dsc_ref)

    dsh_ref[0] += jnp.sum(dh, axis=0, keepdims=True)
    dsc_ref[0] += jnp.sum(dh * x_ref[...], axis=0, keepdims=True)


def ffn_bwd_in(da, g, u, w, x, sc, dxres, seq, name, carry=()):
    n = x.shape[0]
    tn = FF // 2
    nj = FF // tn
    tm = TM // 2
    ni = n // tm
    tps = seq // tm
    nb = n // seq
    k = len(carry)
    gathers = [gt for _, gt in carry]
    c_in, c_out, c_shapes, c_scratch = _carry_plan(carry)

    def body(da_ref, g_ref, u_ref, wg_ref, wu_ref, x_ref, sc_ref, dxres_ref, *rest):
        cx, (dg_ref, du_ref, dx_ref, dsh_ref, dsc_ref), co, (acc, *sems) = rest[:k], rest[k : k + 5], rest[k + 5 : 2 * k + 5], rest[2 * k + 5 :]
        i, j = pl.program_id(0), pl.program_id(1)
        if k:
            @pl.when((i == 0) & (j == 0))
            def _():
                _carry_start(gathers, cx, co, sems)

        gv = g_ref[...].astype(F32)
        uv = u_ref[...].astype(F32)
        dav = da_ref[...].astype(F32)
        s = jax.nn.sigmoid(gv)
        dgv = (dav * uv * s * (1.0 + gv * (1.0 - s))).astype(BF16)
        duv = (dav * gv * s).astype(BF16)
        dg_ref[...] = dgv
        du_ref[...] = duv
        nt = (((1,), (1,)), ((), ()))
        t = lax.dot_general(dgv, wg_ref[...], nt, preferred_element_type=F32) + lax.dot_general(
            duv, wu_ref[...], nt, preferred_element_type=F32
        )

        @pl.when(j == 0)
        def _():
            acc[...] = t

        @pl.when(j > 0)
        def _():
            acc[...] += t

        @pl.when(j == nj - 1)
        def _():
            _mod_bwd_finish(acc[...], x_ref, sc_ref, dxres_ref, dx_ref, dsh_ref, dsc_ref, i % tps == 0)

        if k:
            @pl.when((i == ni - 1) & (j == nj - 1))
            def _():
                _carry_wait(gathers, cx, co, sems)

    row = pl.BlockSpec((tm, D), lambda i, j: (i, 0))
    col = pl.BlockSpec((tm, tn), lambda i, j: (i, j))
    vec = pl.BlockSpec((1, 1, D), lambda i, j: (i // tps, 0, 0))
    return pl.pallas_call(
        body,
        grid=(ni, nj),
        in_specs=[
            col,
            col,
            col,
            pl.BlockSpec((D, tn), lambda i, j: (0, j)),
            pl.BlockSpec((D, tn), lambda i, j: (0, j + nj)),
            row,
            vec,
            row,
        ]
        + c_in,
        out_specs=[col, col, row, vec, vec] + c_out,
        out_shape=[_sds((n, FF), BF16), _sds((n, FF), BF16), _sds((n, D), F32), _sds((nb, 1, D), F32), _sds((nb, 1, D), F32)] + c_shapes,
        scratch_shapes=[pltpu.VMEM((tm, D), F32)] + c_scratch,
        compiler_params=_params(("arbitrary", "arbitrary")),
        name=name,
    )(da, g, u, w, w, x, sc, dxres, *[a for a, _ in carry])


def proj_bwd_in(dparts, w, x, sc, dxres, seq, name):
    n = x.shape[0]
    tps = seq // TM
    nb = n // seq
    ms = [p.shape[1] for p in dparts]
    offs = [sum(ms[:k]) for k in range(len(ms))]
    npart = len(ms)

    def body(*refs):
        d_refs = refs[:npart]
        w_ref, x_ref, sc_ref, dxres_ref, dx_ref, dsh_ref, dsc_ref = refs[npart:]
        dh = None
        for d_ref, off, m in zip(d_refs, offs, ms):
            t = lax.dot_general(d_ref[...], w_ref[:, off : off + m], (((1,), (1,)), ((), ())), preferred_element_type=F32)
            dh = t if dh is None else dh + t
        _mod_bwd_finish(dh, x_ref, sc_ref, dxres_ref, dx_ref, dsh_ref, dsc_ref, pl.program_id(0) % tps == 0)

    row = pl.BlockSpec((TM, D), lambda i: (i, 0))
    vec = pl.BlockSpec((1, 1, D), lambda i: (i // tps, 0, 0))
    return pl.pallas_call(
        body,
        grid=(n // TM,),
        in_specs=[pl.BlockSpec((TM, m), lambda i: (i, 0)) for m in ms] + [pl.BlockSpec((D, sum(ms)), lambda i: (0, 0)), row, vec, row],
        out_specs=[row, vec, vec],
        out_shape=[_sds((n, D), F32), _sds((nb, 1, D), F32), _sds((nb, 1, D), F32)],
        compiler_params=_params(("arbitrary",)),
        name=name,
    )(*dparts, w, x, sc, dxres)


def mm_tn(a, b, name):
    n, k1 = a.shape
    k2 = b.shape[1]
    t1 = k1 if k1 <= 1536 else _tile(k1, 1536)
    t2 = k2 if k2 <= 1536 else _tile(k2, 1536)
    tk = 1024 if n % 1024 == 0 else n
    nk = n // tk

    def body(a_ref, b_ref, o_ref):
        t = lax.dot_general(a_ref[...], b_ref[...], (((0,), (0,)), ((), ())), preferred_element_type=F32)

        @pl.when(pl.program_id(2) == 0)
        def _():
            o_ref[...] = t

        @pl.when(pl.program_id(2) > 0)
        def _():
            o_ref[...] += t

    return pl.pallas_call(
        body,
        grid=(k1 // t1, k2 // t2, nk),
        in_specs=[pl.BlockSpec((tk, t1), lambda i, j, k: (k, i)), pl.BlockSpec((tk, t2), lambda i, j, k: (k, j))],
        out_specs=pl.BlockSpec((t1, t2), lambda i, j, k: (i, j)),
        out_shape=_sds((k1, k2), F32),
        compiler_params=_params(("arbitrary", "arbitrary", "arbitrary")),
        name=name,
    )(a, b)


def _tile(n, cap):
    best = LANE
    for t in range(LANE, cap + 1, LANE):
        if n % t == 0:
            best = t
    return best


def _conv_taps(ext_ref, w_ref, halo, tq, kk):
    acc = None
    for k in range(kk):
        t = w_ref[k : k + 1, :] * ext_ref[pl.ds(halo - (kk - 1 - k), tq), :]
        acc = t if acc is None else acc + t
    return acc


def _conv_taps_t(ext2_ref, w_ref, tq, kk):
    acc = None
    for k in range(kk):
        t = w_ref[k : k + 1, :] * ext2_ref[pl.ds(kk - 1 - k, tq), :]
        acc = t if acc is None else acc + t
    return acc


def _conv_dw(ext_ref, dy, dw_ref, halo, tq, kk):
    for k in range(kk):
        dw_ref[k : k + 1, :] += jnp.sum(dy * ext_ref[pl.ds(halo - (kk - 1 - k), tq), :], axis=0, keepdims=True)


def _halo_spec(rows, width, tq, shift):
    per = tq // rows

    if shift < 0:
        return lambda nblocks: pl.BlockSpec((rows, width), lambda i: (jnp.maximum(i * per - 1, 0), 0))
    return lambda nblocks: pl.BlockSpec((rows, width), lambda i: (jnp.minimum((i + 1) * per, nblocks - 1), 0))


def _ln(c, g, b):
    mu = jnp.mean(c, axis=-1, keepdims=True)
    cc = c - mu
    var = jnp.mean(cc * cc, axis=-1, keepdims=True)
    return cc * lax.rsqrt(var + EPS) * g + b


def _silu(v):
    return v * jax.nn.sigmoid(v)


def _ssd_chunk(conv, dtr, z, s, dt_bias, a_log, dskip, norm_g, tril, e):
    q = SSD_Q
    act = _silu(conv)
    xs, bm, cm = act[:, :1024], act[:, 1024:1280], act[:, 1280:1536]
    lane = lax.broadcasted_iota(jnp.int32, (1, LANE), 1)
    lane_q = lax.broadcasted_iota(jnp.int32, (q, LANE), 1)
    sub_q = lax.broadcasted_iota(jnp.int32, (LANE, q), 0)
    causal = lax.broadcasted_iota(jnp.int32, (q, q), 0) >= lax.broadcasted_iota(jnp.int32, (q, q), 1)
    real = lane < SSD_HEADS
    dt = jnp.where(real, jax.nn.softplus(dtr + dt_bias), 0.0)
    a = jnp.where(real, -jnp.exp(a_log), 0.0)
    da = dt * a
    acs = jnp.dot(tril, da, precision=HI, preferred_element_type=F32)
    acs_t = lax.dot_general(da, tril, (((0,), (1,)), ((), ())), precision=HI, preferred_element_type=F32)
    dt_e = jnp.dot(dt, e, precision=HI, preferred_element_type=F32)
    acs_e = jnp.dot(acs, e, precision=HI, preferred_element_type=F32)
    one8 = jnp.ones((SUB, 1), F32)
    alast_e = jnp.dot(one8 * jnp.sum(da, axis=0, keepdims=True), e, precision=HI, preferred_element_type=F32)[0:1]
    d_e = jnp.dot(one8 * jnp.where(real, dskip, 0.0), e, precision=HI, preferred_element_type=F32)[0:1]
    xdt = xs * dt_e
    nt = (((1,), (1,)), ((), ()))
    tn = (((0,), (0,)), ((), ()))
    ys, snews = [], []
    for g in range(2):
        gl = slice(g * 512, (g + 1) * 512)
        bg = bm[:, g * 128 : (g + 1) * 128].astype(BF16)
        cg = cm[:, g * 128 : (g + 1) * 128].astype(BF16)
        cb = lax.dot_general(cg, bg, nt, preferred_element_type=F32)
        sg = s[:, gl]
        yoff = jnp.dot(cg, sg.astype(BF16), preferred_element_type=F32) * jnp.exp(acs_e[:, gl])
        pairs = []
        for j in range(4):
            xp = xdt[:, g * 512 + j * 128 : g * 512 + (j + 1) * 128].astype(BF16)
            outs = []
            for hh in (g * 8 + 2 * j, g * 8 + 2 * j + 1):
                col = jnp.sum(jnp.where(lane_q == hh, acs, 0.0), axis=1, keepdims=True)
                row = jnp.sum(jnp.where(sub_q == hh, acs_t, 0.0), axis=0, keepdims=True)
                m = cb * jnp.exp(jnp.where(causal, col - row, -1e30))
                outs.append(jnp.dot(m.astype(BF16), xp, preferred_element_type=F32))
            pairs.append(jnp.where(lane_q < SSD_P, outs[0], outs[1]))
        ys.append(jnp.concatenate(pairs, axis=1) + yoff)
        decay = jnp.exp(alast_e[:, gl] - acs_e[:, gl])
        snews.append(
            sg * jnp.exp(alast_e[:, gl]) + lax.dot_general(bg, (xdt[:, gl] * decay).astype(BF16), tn, preferred_element_type=F32)
        )
    y = jnp.concatenate(ys, axis=1) + xs * d_e
    gated = y * _silu(z)
    out = gated * lax.rsqrt(jnp.mean(gated * gated, axis=-1, keepdims=True) + EPS) * norm_g
    return out, jnp.concatenate(snews, axis=1)


def _ssd_consts():
    tril = (lax.broadcasted_iota(jnp.int32, (SSD_Q, SSD_Q), 0) >= lax.broadcasted_iota(jnp.int32, (SSD_Q, SSD_Q), 1)).astype(F32)
    e = (lax.broadcasted_iota(jnp.int32, (LANE, 1024), 0) == lax.broadcasted_iota(jnp.int32, (LANE, 1024), 1) // SSD_P).astype(F32)
    return tril, e


def ssd_fwd(z, xbc, dtr, cw, cb, dt_bias, a_log, dskip, norm_g, seq, name):
    n = z.shape[0]
    q = SSD_Q
    nc = seq // q
    tril, e = _ssd_consts()

    def body(z_ref, xbc_ref, halo_ref, dtr_ref, cw_ref, cb_ref, dtb_ref, alog_ref, dsk_ref, ng_ref, tril_ref, e_ref, y_ref, sprev_ref, s_scr, ext):
        c = pl.program_id(0) % nc

        @pl.when(c == 0)
        def _():
            s_scr[...] = jnp.zeros_like(s_scr)

        ext[0:SUB, :] = jnp.where(c == 0, 0.0, halo_ref[...])
        ext[SUB:, :] = xbc_ref[...]
        conv = _conv_taps(ext, cw_ref, SUB, q, 4) + cb_ref[...]
        sprev_ref[0] = s_scr[...]
        y, snew = _ssd_chunk(conv, dtr_ref[...], z_ref[...], s_scr[...], dtb_ref[...], alog_ref[...], dsk_ref[...], ng_ref[...], tril_ref[...], e_ref[...])
        y_ref[...] = y.astype(BF16)
        s_scr[...] = snew

    def full(shape):
        return pl.BlockSpec(shape, lambda i: (0,) * len(shape))

    return pl.pallas_call(
        body,
        grid=(n // q,),
        in_specs=[
            pl.BlockSpec((q, 1024), lambda i: (i, 0)),
            pl.BlockSpec((q, SSD_XBC), lambda i: (i, 0)),
            _halo_spec(SUB, SSD_XBC, q, -1)(n // SUB),
            pl.BlockSpec((q, LANE), lambda i: (i, 0)),
            full((4, SSD_XBC)),
            full((1, SSD_XBC)),
            full((1, LANE)),
            full((1, LANE)),
            full((1, LANE)),
            full((1, 1024)),
            full((q, q)),
            full((LANE, 1024)),
        ],
        out_specs=[pl.BlockSpec((q, 1024), lambda i: (i, 0)), pl.BlockSpec((1, LANE, 1024), lambda i: (i, 0, 0))],
        out_shape=[_sds((n, 1024), BF16), _sds((n // q, LANE, 1024), F32)],
        scratch_shapes=[pltpu.VMEM((LANE, 1024), F32), pltpu.VMEM((SUB + q, SSD_XBC), F32)],
        compiler_params=_params(("arbitrary",)),
        name=name,
    )(z, xbc, xbc, dtr, cw, cb, dt_bias, a_log, dskip, norm_g, tril, e)


def ssd_bwd(dy, z, xbc, dtr, sprev, cw, cb, dt_bias, a_log, dskip, norm_g, seq, name):
    n = z.shape[0]
    q = SSD_Q
    nc = seq // q
    nchunks = n // q
    tril, e = _ssd_consts()

    def rev(i):
        return (i // nc) * nc + (nc - 1 - i % nc)

    def body(dy_ref, z_ref, xbc_ref, halo_ref, dtr_ref, sprev_ref, cw_ref, cb_ref, dtb_ref, alog_ref, dsk_ref, ng_ref, tril_ref, e_ref,
             dz_ref, dxbc_ref, ddt_ref, dcw_ref, dcb_ref, ddtb_ref, dalog_ref, ddsk_ref, dng_ref, ds_scr, ext, ext2):
        i = pl.program_id(0)
        step = i % nc
        c = nc - 1 - step

        @pl.when(step == 0)
        def _():
            ds_scr[...] = jnp.zeros_like(ds_scr)
            ext2[q:, :] = jnp.zeros((SUB, SSD_XBC), F32)

        @pl.when(i == 0)
        def _():
            for r in (dcw_ref, dcb_ref, ddtb_ref, dalog_ref, ddsk_ref, dng_ref):
                r[...] = jnp.zeros_like(r)

        ext[0:SUB, :] = jnp.where(c == 0, 0.0, halo_ref[...])
        ext[SUB:, :] = xbc_ref[...]
        conv = _conv_taps(ext, cw_ref, SUB, q, 4) + cb_ref[...]
        tril_v, e_v = tril_ref[...], e_ref[...]

        def f(conv, dtr, z, s, dtb, alog, dsk, ng):
            return _ssd_chunk(conv, dtr, z, s, dtb, alog, dsk, ng, tril_v, e_v)

        _, vjp = jax.vjp(f, conv, dtr_ref[...], z_ref[...], sprev_ref[0], dtb_ref[...], alog_ref[...], dsk_ref[...], ng_ref[...])
        dconv, ddtr, dz, dsprev, ddtb, dalog, ddsk, dng = vjp((dy_ref[...].astype(F32), ds_scr[...]))
        ds_scr[...] = dsprev
        dz_ref[...] = dz.astype(BF16)
        ddt_ref[...] = ddtr.astype(BF16)
        ext2[0:q, :] = dconv
        dxbc_ref[...] = _conv_taps_t(ext2, cw_ref, q, 4).astype(BF16)
        ext2[q:, :] = dconv[0:SUB, :]
        _conv_dw(ext, dconv, dcw_ref, SUB, q, 4)
        dcb_ref[...] += jnp.sum(dconv, axis=0, keepdims=True)
        ddtb_ref[...] += ddtb
        dalog_ref[...] += dalog
        ddsk_ref[...] += ddsk
        dng_ref[...] += dng

    def full(shape):
        return pl.BlockSpec(shape, lambda i: (0,) * len(shape))

    per = q // SUB
    return pl.pallas_call(
        body,
        grid=(nchunks,),
        in_specs=[
            pl.BlockSpec((q, 1024), lambda i: (rev(i), 0)),
            pl.BlockSpec((q, 1024), lambda i: (rev(i), 0)),
            pl.BlockSpec((q, SSD_XBC), lambda i: (rev(i), 0)),
            pl.BlockSpec((SUB, SSD_XBC), lambda i: (jnp.maximum(rev(i) * per - 1, 0), 0)),
            pl.BlockSpec((q, LANE), lambda i: (rev(i), 0)),
            pl.BlockSpec((1, LANE, 1024), lambda i: (rev(i), 0, 0)),
            full((4, SSD_XBC)),
            full((1, SSD_XBC)),
            full((1, LANE)),
            full((1, LANE)),
            full((1, LANE)),
            full((1, 1024)),
            full((q, q)),
            full((LANE, 1024)),
        ],
        out_specs=[
            pl.BlockSpec((q, 1024), lambda i: (rev(i), 0)),
            pl.BlockSpec((q, SSD_XBC), lambda i: (rev(i), 0)),
            pl.BlockSpec((q, LANE), lambda i: (rev(i), 0)),
            full((4, SSD_XBC)),
            full((1, SSD_XBC)),
            full((1, LANE)),
            full((1, LANE)),
            full((1, LANE)),
            full((1, 1024)),
        ],
        out_shape=[
            _sds((n, 1024), BF16),
            _sds((n, SSD_XBC), BF16),
            _sds((n, LANE), BF16),
            _sds((4, SSD_XBC), F32),
            _sds((1, SSD_XBC), F32),
            _sds((1, LANE), F32),
            _sds((1, LANE), F32),
            _sds((1, LANE), F32),
            _sds((1, 1024), F32),
        ],
        scratch_shapes=[pltpu.VMEM((LANE, 1024), F32), pltpu.VMEM((SUB + q, SSD_XBC), F32), pltpu.VMEM((q + SUB, SSD_XBC), F32)],
        compiler_params=_params(("arbitrary",)),
        name=name,
    )(dy, z, xbc, xbc, dtr, sprev, cw, cb, dt_bias, a_log, dskip, norm_g, tril, e)


POOL_HALO = 16
TQ = 512


def _pool_count(pos, w):
    return jnp.minimum(pos + 1.0, float(w))


def pool_fwd(u, pw, scale, seq, name):
    n = u.shape[0]
    tq, halo = TQ, POOL_HALO
    tps = seq // tq

    def body(u_ref, halo_ref, pw_ref, sc_ref, y_ref, ext):
        t0 = pl.program_id(0) % tps
        ext[0:halo, :] = jnp.where(t0 == 0, 0.0, halo_ref[...])
        ext[halo:, :] = u_ref[...]
        pos = (t0 * tq + lax.broadcasted_iota(jnp.int32, (tq, 1), 0)).astype(F32)
        for g, w in enumerate(POOL_WINDOWS):
            ln = slice(g * LANE, (g + 1) * LANE)
            acc = ext[pl.ds(halo, tq), ln]
            for j in range(1, w):
                acc = acc + ext[pl.ds(halo - j, tq), ln]
            pooled = acc / _pool_count(pos, w) - u_ref[:, ln]
            mixed = jnp.dot(pooled.astype(BF16), pw_ref[g].astype(BF16), preferred_element_type=F32)
            y_ref[:, ln] = (mixed * sc_ref[:, ln]).astype(BF16)

    return pl.pallas_call(
        body,
        grid=(n // tq,),
        in_specs=[
            pl.BlockSpec((tq, POOL_DIM), lambda i: (i, 0)),
            _halo_spec(halo, POOL_DIM, tq, -1)(n // halo),
            pl.BlockSpec((4, LANE, LANE), lambda i: (0, 0, 0)),
            pl.BlockSpec((1, POOL_DIM), lambda i: (0, 0)),
        ],
        out_specs=pl.BlockSpec((tq, POOL_DIM), lambda i: (i, 0)),
        out_shape=_sds((n, POOL_DIM), BF16),
        scratch_shapes=[pltpu.VMEM((halo + tq, POOL_DIM), F32)],
        compiler_params=_params(("arbitrary",)),
        name=name,
    )(u, u, pw, scale)


def pool_bwd(dy, u, pw, scale, seq, name):
    n = u.shape[0]
    tq, halo = TQ, POOL_HALO
    tps = seq // tq
    nt = (((1,), (1,)), ((), ()))
    tn = (((0,), (0,)), ((), ()))

    def body(dy_ref, dyn_ref, u_ref, halo_ref, pw_ref, sc_ref, du_ref, dpw_ref, dsc_ref, ext, ext2):
        i = pl.program_id(0)
        t0 = i % tps

        @pl.when(i == 0)
        def _():
            dpw_ref[...] = jnp.zeros_like(dpw_ref)
            dsc_ref[...] = jnp.zeros_like(dsc_ref)

        ext[0:halo, :] = jnp.where(t0 == 0, 0.0, halo_ref[...])
        ext[halo:, :] = u_ref[...]
        pos = (t0 * tq + lax.broadcasted_iota(jnp.int32, (tq, 1), 0)).astype(F32)
        dyv = dy_ref[...].astype(F32)
        dynv = jnp.where(t0 == tps - 1, 0.0, dyn_ref[...].astype(F32))
        for g, w in enumerate(POOL_WINDOWS):
            ln = slice(g * LANE, (g + 1) * LANE)
            wg = pw_ref[g].astype(BF16)
            acc = ext[pl.ds(halo, tq), ln]
            for j in range(1, w):
                acc = acc + ext[pl.ds(halo - j, tq), ln]
            pooled = (acc / _pool_count(pos, w) - u_ref[:, ln]).astype(BF16)
            mixed = jnp.dot(pooled, wg, preferred_element_type=F32)
            dsc_ref[:, ln] += jnp.sum(dyv[:, ln] * mixed, axis=0, keepdims=True)
            dmix = (dyv[:, ln] * sc_ref[:, ln]).astype(BF16)
            dpw_ref[g] += lax.dot_general(pooled, dmix, tn, preferred_element_type=F32)
            dpool = lax.dot_general(dmix, wg, nt, preferred_element_type=F32)
            dmix_n = (dynv[:, ln] * sc_ref[:, ln]).astype(BF16)
            dpool_n = lax.dot_general(dmix_n, wg, nt, preferred_element_type=F32)
            ext2[0:tq, ln] = dpool / _pool_count(pos, w)
            ext2[tq:, ln] = dpool_n * (1.0 / w)
            acc2 = ext2[pl.ds(0, tq), ln]
            for j in range(1, w):
                acc2 = acc2 + ext2[pl.ds(j, tq), ln]
            du_ref[:, ln] = (acc2 - dpool).astype(BF16)

    return pl.pallas_call(
        body,
        grid=(n // tq,),
        in_specs=[
            pl.BlockSpec((tq, POOL_DIM), lambda i: (i, 0)),
            _halo_spec(halo, POOL_DIM, tq, +1)(n // halo),
            pl.BlockSpec((tq, POOL_DIM), lambda i: (i, 0)),
            _halo_spec(halo, POOL_DIM, tq, -1)(n // halo),
            pl.BlockSpec((4, LANE, LANE), lambda i: (0, 0, 0)),
            pl.BlockSpec((1, POOL_DIM), lambda i: (0, 0)),
        ],
        out_specs=[
            pl.BlockSpec((tq, POOL_DIM), lambda i: (i, 0)),
            pl.BlockSpec((4, LANE, LANE), lambda i: (0, 0, 0)),
            pl.BlockSpec((1, POOL_DIM), lambda i: (0, 0)),
        ],
        out_shape=[_sds((n, POOL_DIM), BF16), _sds((4, LANE, LANE), F32), _sds((1, POOL_DIM), F32)],
        scratch_shapes=[pltpu.VMEM((halo + tq, POOL_DIM), F32), pltpu.VMEM((tq + halo, POOL_DIM), F32)],
        compiler_params=_params(("arbitrary",)),
        name=name,
    )(dy, dy, u, u, pw, scale)


CONF_HALO = 32


def _conf_post(c, g, b):
    return _silu(_ln(c, g, b))


def conf_fwd(vg, w, b, lng, lnb, seq, name):
    n = vg.shape[0]
    tq, halo, kk = TQ, CONF_HALO, CONF_K
    tps = seq // tq
    c = CONF_DIM

    def body(vg_ref, halo_ref, w_ref, b_ref, lng_ref, lnb_ref, y_ref, conv_ref, ext):
        t0 = pl.program_id(0) % tps
        hv = halo_ref[...]
        ext[0:halo, :] = jnp.where(t0 == 0, 0.0, hv[:, :c] * jax.nn.sigmoid(hv[:, c:]))
        ext[halo:, :] = vg_ref[:, :c] * jax.nn.sigmoid(vg_ref[:, c:])
        conv = _conv_taps(ext, w_ref, halo, tq, kk) + b_ref[...]
        conv_ref[...] = conv
        y_ref[...] = _conf_post(conv, lng_ref[...], lnb_ref[...]).astype(BF16)

    one = pl.BlockSpec((1, c), lambda i: (0, 0))
    return pl.pallas_call(
        body,
        grid=(n // tq,),
        in_specs=[pl.BlockSpec((tq, 2 * c), lambda i: (i, 0)), _halo_spec(halo, 2 * c, tq, -1)(n // halo), pl.BlockSpec((kk, c), lambda i: (0, 0)), one, one, one],
        out_specs=[pl.BlockSpec((tq, c), lambda i: (i, 0)), pl.BlockSpec((tq, c), lambda i: (i, 0))],
        out_shape=[_sds((n, c), BF16), _sds((n, c), F32)],
        scratch_shapes=[pltpu.VMEM((halo + tq, c), F32)],
        compiler_params=_params(("arbitrary",)),
        name=name,
    )(vg, vg, w, b, lng, lnb)


def conf_bwd(dy, conv, vg, w, lng, lnb, seq, name):
    n = vg.shape[0]
    tq, halo, kk = TQ, CONF_HALO, CONF_K
    tps = seq // tq
    c = CONF_DIM

    def body(dy_ref, dyn_ref, conv_ref, convn_ref, vg_ref, halo_ref, w_ref, lng_ref, lnb_ref, dvg_ref, dw_ref, db_ref, dlng_ref, dlnb_ref, ext, ext2):
        i = pl.program_id(0)
        t0 = i % tps

        @pl.when(i == 0)
        def _():
            for r in (dw_ref, db_ref, dlng_ref, dlnb_ref):
                r[...] = jnp.zeros_like(r)

        _, vjp = jax.vjp(_conf_post, conv_ref[...], lng_ref[...], lnb_ref[...])
        dconv, dlng, dlnb = vjp(dy_ref[...].astype(F32))
        _, vjpn = jax.vjp(_conf_post, convn_ref[...], lng_ref[...], lnb_ref[...])
        dconv_n = vjpn(dyn_ref[...].astype(F32))[0]
        ext2[0:tq, :] = dconv
        ext2[tq:, :] = jnp.where(t0 == tps - 1, 0.0, dconv_n)
        dh = _conv_taps_t(ext2, w_ref, tq, kk)
        hv = halo_ref[...]
        ext[0:halo, :] = jnp.where(t0 == 0, 0.0, hv[:, :c] * jax.nn.sigmoid(hv[:, c:]))
        v = vg_ref[:, :c]
        s = jax.nn.sigmoid(vg_ref[:, c:])
        ext[halo:, :] = v * s
        _conv_dw(ext, dconv, dw_ref, halo, tq, kk)
        db_ref[...] += jnp.sum(dconv, axis=0, keepdims=True)
        dlng_ref[...] += dlng
        dlnb_ref[...] += dlnb
        dvg_ref[:, :c] = (dh * s).astype(BF16)
        dvg_ref[:, c:] = (dh * v * s * (1.0 - s)).astype(BF16)

    one = pl.BlockSpec((1, c), lambda i: (0, 0))
    tile = pl.BlockSpec((tq, c), lambda i: (i, 0))
    nxt = _halo_spec(halo, c, tq, +1)(n // halo)
    return pl.pallas_call(
        body,
        grid=(n // tq,),
        in_specs=[tile, nxt, tile, nxt, pl.BlockSpec((tq, 2 * c), lambda i: (i, 0)), _halo_spec(halo, 2 * c, tq, -1)(n // halo),
                  pl.BlockSpec((kk, c), lambda i: (0, 0)), one, one],
        out_specs=[pl.BlockSpec((tq, 2 * c), lambda i: (i, 0)), pl.BlockSpec((kk, c), lambda i: (0, 0)), one, one, one],
        out_shape=[_sds((n, 2 * c), BF16), _sds((kk, c), F32), _sds((1, c), F32), _sds((1, c), F32), _sds((1, c), F32)],
        scratch_shapes=[pltpu.VMEM((halo + tq, c), F32), pltpu.VMEM((tq + halo, c), F32)],
        compiler_params=_params(("arbitrary",)),
        name=name,
    )(dy, dy, conv, conv, vg, vg, w, lng, lnb)


TL = 256


def _expm1_neg(t):
    p = t * (1.0 + t * (1.0 / 2 + t * (1.0 / 6 + t * (1.0 / 24 + t * (1.0 / 120 + t * (1.0 / 720 + t * (1.0 / 5040)))))))
    return jnp.where(t > -0.35, p, jnp.exp(t) - 1.0)


def _lru_gate(xc, ra, ia, ba, bx, lam):
    r = jax.nn.sigmoid(ra + ba)
    i = jax.nn.sigmoid(ia + bx)
    log_a = -LRU_C * r * jax.nn.softplus(-lam)
    return jnp.exp(log_a), jnp.sqrt(-_expm1_neg(2.0 * log_a)) * (i * xc)


def _lru_out(h, gr):
    return h * jax.nn.gelu(gr)


def _scan_rows(a, b, tq, reverse):
    row = lax.broadcasted_iota(jnp.int32, (tq, 1), 0)
    d = 1
    while d < tq:
        sh = tq - d if reverse else d
        valid = (row < tq - d) if reverse else (row >= d)
        a_s = pltpu.roll(a, sh, 0)
        b_s = pltpu.roll(b, sh, 0)
        b = jnp.where(valid, a * b_s, 0.0) + b
        a = jnp.where(valid, a * a_s, a)
        d *= 2
    return a, b


def _row_of(v, r, tq):
    row = lax.broadcasted_iota(jnp.int32, (tq, 1), 0)
    return jnp.sum(jnp.where(row == r, v, 0.0), axis=0, keepdims=True)


def _head_mm(xc, w_ref):
    return jnp.concatenate(
        [
            jnp.dot(xc[:, h * LANE : (h + 1) * LANE].astype(BF16), w_ref[h].astype(BF16), preferred_element_type=F32)
            for h in range(LRU_HEADS)
        ],
        axis=1,
    )


def lru_fwd(xr, gr, cw, cb, wa, ba, wx, bx, lam, seq, name):
    n = xr.shape[0]
    tq = TL
    tps = seq // tq
    c = LRU_DIM

    def body(xr_ref, halo_ref, gr_ref, cw_ref, cb_ref, wa_ref, ba_ref, wx_ref, bx_ref, lam_ref, y_ref, h_ref, hc, ext):
        t0 = pl.program_id(0) % tps

        @pl.when(t0 == 0)
        def _():
            hc[...] = jnp.zeros_like(hc)

        ext[0:SUB, :] = jnp.where(t0 == 0, 0.0, halo_ref[...])
        ext[SUB:, :] = xr_ref[...]
        xc = _conv_taps(ext, cw_ref, SUB, tq, 4) + cb_ref[...]
        a, b = _lru_gate(xc, _head_mm(xc, wa_ref), _head_mm(xc, wx_ref), ba_ref[...], bx_ref[...], lam_ref[...])
        acum, h0 = _scan_rows(a, b, tq, False)
        h = h0 + acum * hc[0:1, :]
        h_ref[...] = h
        hc[0:1, :] = h_ref[tq - 1 : tq, :]
        y_ref[...] = _lru_out(h, gr_ref[...]).astype(BF16)

    one = pl.BlockSpec((1, c), lambda i: (0, 0))
    tile = pl.BlockSpec((tq, c), lambda i: (i, 0))
    hw = pl.BlockSpec((LRU_HEADS, LANE, LANE), lambda i: (0, 0, 0))
    return pl.pallas_call(
        body,
        grid=(n // tq,),
        in_specs=[tile, _halo_spec(SUB, c, tq, -1)(n // SUB), tile, pl.BlockSpec((4, c), lambda i: (0, 0)), one, hw, one, hw, one, one],
        out_specs=[tile, tile],
        out_shape=[_sds((n, c), BF16), _sds((n, c), F32)],
        scratch_shapes=[pltpu.VMEM((SUB, c), F32), pltpu.VMEM((SUB + tq, c), F32)],
        compiler_params=_params(("arbitrary",)),
        name=name,
    )(xr, xr, gr, cw, cb, wa, ba, wx, bx, lam)


def lru_bwd(dy, xr, gr, h, cw, cb, wa, ba, wx, bx, lam, seq, name):
    n = xr.shape[0]
    tq = TL
    tps = seq // tq
    ntile = n // tq
    c = LRU_DIM
    per = tq // SUB
    nt = (((1,), (1,)), ((), ()))
    tn = (((0,), (0,)), ((), ()))

    def rev(i):
        return (i // tps) * tps + (tps - 1 - i % tps)

    def body(dy_ref, xr_ref, halo_ref, gr_ref, h_ref, hprev_ref, cw_ref, cb_ref, wa_ref, ba_ref, wx_ref, bx_ref, lam_ref,
             dxr_ref, dgr_ref, dcw_ref, dcb_ref, dwa_ref, dba_ref, dwx_ref, dbx_ref, dlam_ref, carry, ext, ext2):
        i = pl.program_id(0)
        step = i % tps
        t0 = tps - 1 - step

        @pl.when(step == 0)
        def _():
            carry[...] = jnp.zeros_like(carry)
            ext2[tq:, :] = jnp.zeros((SUB, c), F32)

        @pl.when(i == 0)
        def _():
            for r in (dcw_ref, dcb_ref, dwa_ref, dba_ref, dwx_ref, dbx_ref, dlam_ref):
                r[...] = jnp.zeros_like(r)

        ext[0:SUB, :] = jnp.where(t0 == 0, 0.0, halo_ref[...])
        ext[SUB:, :] = xr_ref[...]
        xc = _conv_taps(ext, cw_ref, SUB, tq, 4) + cb_ref[...]
        (a, _), vjp_gate = jax.vjp(_lru_gate, xc, _head_mm(xc, wa_ref), _head_mm(xc, wx_ref), ba_ref[...], bx_ref[...], lam_ref[...])
        hv = h_ref[...]
        _, vjp_out = jax.vjp(_lru_out, hv, gr_ref[...])
        dh, dgr = vjp_out(dy_ref[...].astype(F32))
        dgr_ref[...] = dgr.astype(BF16)
        row = lax.broadcasted_iota(jnp.int32, (tq, 1), 0)
        a_up = jnp.where(row == tq - 1, carry[0:1, :], pltpu.roll(a, tq - 1, 0))
        acum, l0 = _scan_rows(a_up, dh, tq, True)
        lamv = l0 + acum * carry[1:2, :]
        carry[0:1, :] = _row_of(a, 0, tq)
        carry[1:2, :] = _row_of(lamv, 0, tq)
        hprev = jnp.where(row == 0, jnp.where(t0 == 0, 0.0, hprev_ref[SUB - 1 : SUB, :]), pltpu.roll(hv, 1, 0))
        dxc, dra, dia, dba, dbx, dlam = vjp_gate((lamv * hprev, lamv))
        dba_ref[...] += dba
        dbx_ref[...] += dbx
        dlam_ref[...] += dlam
        pieces = []
        for hh in range(LRU_HEADS):
            ln = slice(hh * LANE, (hh + 1) * LANE)
            xh = xc[:, ln].astype(BF16)
            drh = dra[:, ln].astype(BF16)
            dih = dia[:, ln].astype(BF16)
            dwa_ref[hh] += lax.dot_general(xh, drh, tn, preferred_element_type=F32)
            dwx_ref[hh] += lax.dot_general(xh, dih, tn, preferred_element_type=F32)
            pieces.append(
                lax.dot_general(drh, wa_ref[hh].astype(BF16), nt, preferred_element_type=F32)
                + lax.dot_general(dih, wx_ref[hh].astype(BF16), nt, preferred_element_type=F32)
            )
        dxc = dxc + jnp.concatenate(pieces, axis=1)
        ext2[0:tq, :] = dxc
        dxr_ref[...] = _conv_taps_t(ext2, cw_ref, tq, 4).astype(BF16)
        ext2[tq:, :] = ext2[0:SUB, :]
        _conv_dw(ext, dxc, dcw_ref, SUB, tq, 4)
        dcb_ref[...] += jnp.sum(dxc, axis=0, keepdims=True)

    one = pl.BlockSpec((1, c), lambda i: (0, 0))
    tile = pl.BlockSpec((tq, c), lambda i: (rev(i), 0))
    prev = pl.BlockSpec((SUB, c), lambda i: (jnp.maximum(rev(i) * per - 1, 0), 0))
    hw = pl.BlockSpec((LRU_HEADS, LANE, LANE), lambda i: (0, 0, 0))
    cw4 = pl.BlockSpec((4, c), lambda i: (0, 0))
    return pl.pallas_call(
        body,
        grid=(ntile,),
        in_specs=[tile, tile, prev, tile, tile, prev, cw4, one, hw, one, hw, one, one],
        out_specs=[tile, tile, cw4, one, hw, one, hw, one, one],
        out_shape=[_sds((n, c), BF16), _sds((n, c), BF16), _sds((4, c), F32), _sds((1, c), F32), _sds((LRU_HEADS, LANE, LANE), F32),
                   _sds((1, c), F32), _sds((LRU_HEADS, LANE, LANE), F32), _sds((1, c), F32), _sds((1, c), F32)],
        scratch_shapes=[pltpu.VMEM((SUB, c), F32), pltpu.VMEM((SUB + tq, c), F32), pltpu.VMEM((tq + SUB, c), F32)],
        compiler_params=_params(("arbitrary",)),
        name=name,
    )(dy, xr, xr, gr, h, h, cw, cb, wa, ba, wx, bx, lam)


def ada_fwd(c_all, w, b, name):
    nl, _, cols = w.shape
    nb = c_all.shape[0]

    def body(c_ref, w_ref, b_ref, o_ref):
        sc = _silu(c_ref[...]).astype(BF16)
        o_ref[0] = jnp.dot(sc, w_ref[0].astype(BF16), preferred_element_type=F32) + b_ref[0]

    return pl.pallas_call(
        body,
        grid=(nl,),
        in_specs=[pl.BlockSpec((nb, D), lambda l: (0, 0)), pl.BlockSpec((1, D, cols), lambda l: (l, 0, 0)), pl.BlockSpec((1, 1, cols), lambda l: (l, 0, 0))],
        out_specs=pl.BlockSpec((1, nb, cols), lambda l: (l, 0, 0)),
        out_shape=_sds((nl, nb, cols), F32),
        compiler_params=_params(("arbitrary",)),
        name=name,
    )(c_all, w, b)


def ada_bwd(c_all, dmod, name):
    nl, nb, cols = dmod.shape

    def body(c_ref, d_ref, o_ref):
        sc = _silu(c_ref[...]).astype(BF16)
        o_ref[0] = lax.dot_general(sc, d_ref[0].astype(BF16), (((0,), (0,)), ((), ())), preferred_element_type=F32)

    return pl.pallas_call(
        body,
        grid=(nl,),
        in_specs=[pl.BlockSpec((nb, D), lambda l: (0, 0)), pl.BlockSpec((1, nb, cols), lambda l: (l, 0, 0))],
        out_specs=pl.BlockSpec((1, D, cols), lambda l: (l, 0, 0)),
        out_shape=_sds((nl, D, cols), F32),
        compiler_params=_params(("arbitrary",)),
        name=name,
    )(c_all, dmod)


def loss_grad(y, target, name):
    n = y.shape[0]

    def body(y_ref, t_ref, dy_ref, l_ref, acc):
        i = pl.program_id(0)

        @pl.when(i == 0)
        def _():
            acc[...] = jnp.zeros_like(acc)

        e = y_ref[...] - t_ref[...]
        dy_ref[...] = e * (1.0 / D)
        acc[...] += jnp.sum(e * e, axis=0, keepdims=True)

        @pl.when(i == n // TM - 1)
        def _():
            l_ref[...] = jnp.full((1, LANE), 0.5 / D, F32) * jnp.sum(acc[...])

    row = pl.BlockSpec((TM, D), lambda i: (i, 0))
    return pl.pallas_call(
        body,
        grid=(n // TM,),
        in_specs=[row, row],
        out_specs=[row, pl.BlockSpec((1, LANE), lambda i: (0, 0))],
        out_shape=[_sds((n, D), F32), _sds((1, LANE), F32)],
        scratch_shapes=[pltpu.VMEM((1, D), F32)],
        compiler_params=_params(("arbitrary",)),
        name=name,
    )(y, target)


def sum_parts(parts, name):
    ns, r, _ = parts.shape
    tr = _row_tile(r, 1024)

    def body(p_ref, o_ref):
        acc = p_ref[0]
        for k in range(1, ns):
            acc = acc + p_ref[k]
        o_ref[...] = acc

    return pl.pallas_call(
        body,
        grid=(r // tr,),
        in_specs=[pl.BlockSpec((ns, tr, LANE), lambda i: (0, i, 0))],
        out_specs=pl.BlockSpec((tr, LANE), lambda i: (i, 0)),
        out_shape=_sds((r, LANE), F32),
        compiler_params=_params(("arbitrary",)),
        name=name,
    )(parts)


def _row_tile(r, cap):
    if r <= cap:
        return r
    best = None
    for t in range(16, cap + 1, 16):
        if r % t == 0:
            best = t
    assert best is not None, r
    return best


def adamw(w, m, v, gparts, name):
    r, c = w.shape
    ns = gparts.shape[0]
    tr = _row_tile(r, min(512, 256 * 1024 // c))
    c1 = 1.0 - B1**STEP
    c2 = 1.0 - B2**STEP

    def body(w_ref, m_ref, v_ref, g_ref, go_ref, d_ref, mo_ref, vo_ref):
        g = g_ref[0].astype(F32)
        for k in range(1, ns):
            g = g + g_ref[k].astype(F32)
        mn = B1 * m_ref[...] + (1.0 - B1) * g
        vn = B2 * v_ref[...] + (1.0 - B2) * (g * g)
        go_ref[...] = g
        mo_ref[...] = mn
        vo_ref[...] = vn
        d_ref[...] = -LR * ((mn / c1) / (jnp.sqrt(vn / c2) + AEPS) + WD * w_ref[...])

    tile = pl.BlockSpec((tr, c), lambda i: (i, 0))
    return pl.pallas_call(
        body,
        grid=(r // tr,),
        in_specs=[tile, tile, tile, pl.BlockSpec((ns, tr, c), lambda i: (0, i, 0))],
        out_specs=[tile, tile, tile, tile],
        out_shape=[_sds((r, c), F32)] * 4,
        compiler_params=_params(("arbitrary",)),
        name=name,
    )(w, m, v, gparts)


WEIGHTS = ["ada_w", "ada_b", "ln_g", "ln_b", "ffn_w_in", "ffn_w_out", "ev_w_in", "ssd_conv_w", "ssd_conv_b", "ssd_dt_bias",
           "ssd_a_log", "ssd_d", "ssd_norm_g", "pool_w", "pool_scale", "ev_w_out", "od_w_in", "conf_dw_w", "conf_dw_b",
           "conf_ln_g", "conf_ln_b", "lru_conv_w", "lru_conv_b", "lru_wa", "lru_ba", "lru_wx", "lru_bx", "lru_lambda", "od_w_out"]
BIG = ("ada_w", "ffn_w_in", "ffn_w_out", "ev_w_in", "ev_w_out", "od_w_in", "od_w_out")
SMALL = {
    "ada_b": ((4, 9216), None), "ln_g": ((4, 3, 1024), 2), "ln_b": ((4, 3, 1024), 2),
    "ssd_conv_w": ((2, 4, 1536), 2), "ssd_conv_b": ((2, 1536), None), "ssd_dt_bias": ((2, 16), None),
    "ssd_a_log": ((2, 16), None), "ssd_d": ((2, 16), None), "ssd_norm_g": ((2, 1024), None),
    "pool_w": ((2, 4, 128, 128), None), "pool_scale": ((2, 512), None),
    "conf_dw_w": ((2, 31, 512), 2), "conf_dw_b": ((2, 512), 1), "conf_ln_g": ((2, 512), 1), "conf_ln_b": ((2, 512), 1),
    "lru_conv_w": ((2, 4, 1024), 2), "lru_conv_b": ((2, 1024), 1), "lru_wa": ((2, 8, 128, 128), None),
    "lru_ba": ((2, 1024), 1), "lru_wx": ((2, 8, 128, 128), None), "lru_bx": ((2, 1024), 1), "lru_lambda": ((2, 1024), 1),
}
PACK_ROWS = 2 * SUB * LANE


def _pack(arrs):
    flat = jnp.concatenate([a.reshape(-1) for a in arrs])
    pad = (-flat.shape[0]) % PACK_ROWS
    return jnp.pad(flat, (0, pad)).reshape(-1, LANE)


def _unpack(buf, shapes, lead=()):
    flat = buf.reshape(lead + (-1,))
    out, off = [], 0
    for s in shapes:
        k = math.prod(s)
        out.append(flat[..., off : off + k].reshape(lead + tuple(s)))
        off += k
    return out


def _pad_lanes(v):
    return jnp.pad(v, (0, LANE - v.shape[0]))[None]


def kernel(x, c, ada_w, ada_b, ln_g, ln_b, ffn_w_in, ffn_w_out, ev_w_in, ssd_conv_w, ssd_conv_b, ssd_dt_bias, ssd_a_log, ssd_d, ssd_norm_g, pool_w, pool_scale, ev_w_out, od_w_in, conf_dw_w, conf_dw_b, conf_ln_g, conf_ln_b, lru_conv_w, lru_conv_b, lru_wa, lru_ba, lru_wx, lru_bx, lru_lambda, od_w_out, loss_target, m_ada_w, m_ada_b, m_ln_g, m_ln_b, m_ffn_w_in, m_ffn_w_out, m_ev_w_in, m_ssd_conv_w, m_ssd_conv_b, m_ssd_dt_bias, m_ssd_a_log, m_ssd_d, m_ssd_norm_g, m_pool_w, m_pool_scale, m_ev_w_out, m_od_w_in, m_conf_dw_w, m_conf_dw_b, m_conf_ln_g, m_conf_ln_b, m_lru_conv_w, m_lru_conv_b, m_lru_wa, m_lru_ba, m_lru_wx, m_lru_bx, m_lru_lambda, m_od_w_out, v_ada_w, v_ada_b, v_ln_g, v_ln_b, v_ffn_w_in, v_ffn_w_out, v_ev_w_in, v_ssd_conv_w, v_ssd_conv_b, v_ssd_dt_bias, v_ssd_a_log, v_ssd_d, v_ssd_norm_g, v_pool_w, v_pool_scale, v_ev_w_out, v_od_w_in, v_conf_dw_w, v_conf_dw_b, v_conf_ln_g, v_conf_ln_b, v_lru_conv_w, v_lru_conv_b, v_lru_wa, v_lru_ba, v_lru_wx, v_lru_bx, v_lru_lambda, v_od_w_out):
    p = dict(locals())
    nb, seq, _ = x.shape
    n = nb * seq
    me = 4 * lax.axis_index("x") + 2 * lax.axis_index("y") + lax.axis_index("c")
    sharded = [k for k, (_, ax) in SMALL.items() if ax is not None]

    def cols_of(g):
        return jnp.moveaxis(g, 0, 1).reshape(g.shape[1], N_DEV * g.shape[2])

    def rows_of(g):
        return g.reshape(N_DEV * g.shape[1], g.shape[2])

    def ev_in_of(g):
        w = cols_of(g)
        return jnp.concatenate([w[:, :2560], w[:, 2576:], jnp.pad(w[:, 2560:2576], ((0, 0), (0, LANE - SSD_HEADS)))], axis=1)

    sh_ffn_in, sh_ffn_out = ffn_w_in.astype(BF16), ffn_w_out.astype(BF16)
    sh_mix_in = [ev_w_in.astype(BF16), od_w_in.astype(BF16)]
    sh_mix_out = [ev_w_out.astype(BF16), od_w_out.astype(BF16)]

    def ffn_items(l, i):
        return [(sh_ffn_in[l, i], True), (sh_ffn_out[l, i], True)]

    def mix_items(l):
        return [(sh_mix_in[l % 2][l // 2], True), (sh_mix_out[l % 2][l // 2], True)]

    sm_local_shapes = [p[k].shape for k in sharded]
    g_in, g_out, sm_all = exchange(ffn_items(0, 0) + [(_pack([p[k] for k in sharded] + [c]), True)], "ag_first")
    w_ffn = {(0, 0): (cols_of(g_in), rows_of(g_out))}
    w_mix = {}
    got = _unpack(sm_all, sm_local_shapes + [c.shape], lead=(N_DEV,))
    full = {k: p[k] for k, (_, ax) in SMALL.items() if ax is None}
    for k, g in zip(sharded, got[:-1]):
        full[k] = jnp.moveaxis(g, 0, SMALL[k][1]).reshape(SMALL[k][0])
    c_all = got[-1].reshape(N_DEV * nb, D)

    cols = ada_w.shape[-1]
    ada_b_loc = lax.dynamic_slice_in_dim(ada_b, me * cols, cols, axis=1)[:, None, :]
    mod_cols = ada_fwd(c_all, ada_w, ada_b_loc, "ada_fwd")
    (mod_x,) = exchange([(mod_cols.reshape(DEPTH, N_DEV, nb, cols).transpose(1, 0, 2, 3), False)], "a2a_mod")
    mod = mod_x.transpose(1, 2, 0, 3).reshape(DEPTH, nb, N_MOD, 1, D)

    def vec(l, j):
        return mod[l, :, j]

    def row(a):
        return a[None]

    xs = x.reshape(n, D)
    saved = []
    for l in range(DEPTH):
        s = {"x0": xs}
        e = l // 2
        s["h1"], s["g1"], s["u1"], s["a1"], gm_in, gm_out, g_in, g_out = ffn_up(
            xs, vec(l, 0), vec(l, 1), w_ffn[l, 0][0], seq, "ffn_up_c4", carry=mix_items(l) + ffn_items(l, 1))
        w_mix[l] = ((ev_in_of if l % 2 == 0 else cols_of)(gm_in), rows_of(gm_out))
        w_ffn[l, 1] = (cols_of(g_in), rows_of(g_out))
        x1, s["y1"] = mm_postnorm([s["a1"]], w_ffn[l, 0][1], xs, vec(l, 2), row(full["ln_g"][l, 0]), row(full["ln_b"][l, 0]), 0.5, seq, "ffn_down")
        s["x1"] = x1
        if l % 2 == 0:
            s["h2"], s["z"], s["xbc"], s["u"], s["dtr"] = mod_mm(x1, vec(l, 3), vec(l, 4), w_mix[l][0], EV_SPLITS, seq, "ev_in")
            s["ya"], s["sprev"] = ssd_fwd(s["z"], s["xbc"], s["dtr"], full["ssd_conv_w"][e], row(full["ssd_conv_b"][e]), _pad_lanes(full["ssd_dt_bias"][e]),
                                          _pad_lanes(full["ssd_a_log"][e]), _pad_lanes(full["ssd_d"][e]), row(full["ssd_norm_g"][e]), seq, "ssd_fwd")
            s["yb"] = pool_fwd(s["u"], full["pool_w"][e], row(full["pool_scale"][e]), seq, "pool_fwd")
        else:
            s["h2"], s["vg"], s["xr"], s["gr"] = mod_mm(x1, vec(l, 3), vec(l, 4), w_mix[l][0], OD_SPLITS, seq, "od_in")
            s["ya"], s["conv"] = conf_fwd(s["vg"], full["conf_dw_w"][e], row(full["conf_dw_b"][e]), row(full["conf_ln_g"][e]), row(full["conf_ln_b"][e]), seq, "conf_fwd")
            s["yb"], s["hst"] = lru_fwd(s["xr"], s["gr"], full["lru_conv_w"][e], row(full["lru_conv_b"][e]), full["lru_wa"][e], row(full["lru_ba"][e]),
                                        full["lru_wx"][e], row(full["lru_bx"][e]), row(full["lru_lambda"][e]), seq, "lru_fwd")
        x2, s["y2"] = mm_postnorm([s["ya"], s["yb"]], w_mix[l][1], x1, vec(l, 5), row(full["ln_g"][l, 1]), row(full["ln_b"][l, 1]), 1.0, seq, "mix_out")
        s["x2"] = x2
        if l + 1 < DEPTH:
            s["h3"], s["g3"], s["u3"], s["a3"], g_in, g_out = ffn_up(x2, vec(l, 6), vec(l, 7), w_ffn[l, 1][0], seq, "ffn_up_c2", carry=ffn_items(l + 1, 0))
            w_ffn[l + 1, 0] = (cols_of(g_in), rows_of(g_out))
        else:
            s["h3"], s["g3"], s["u3"], s["a3"] = ffn_up(x2, vec(l, 6), vec(l, 7), w_ffn[l, 1][0], seq, "ffn_up")
        xs, s["y3"] = mm_postnorm([s["a3"]], w_ffn[l, 1][1], x2, vec(l, 8), row(full["ln_g"][l, 2]), row(full["ln_b"][l, 2]), 0.5, seq, "ffn_down")
        saved.append(s)

    dx, loss_row = loss_grad(xs, loss_target.reshape(n, D), "loss")
    loss = lax.psum(loss_row[0, 0], ("x", "y", "c"))

    sg = {k: [None] * shape[0] for k, (shape, _) in SMALL.items()}
    sg["ln_g"] = [[None] * 3 for _ in range(DEPTH)]
    sg["ln_b"] = [[None] * 3 for _ in range(DEPTH)]
    dmod = [[None] * N_MOD for _ in range(DEPTH)]
    pending, got_w = [], {}

    def cut_cols(g):
        r, cc = g.shape
        return g.reshape(r, N_DEV, cc // N_DEV).transpose(1, 0, 2).astype(BF16)

    def cut_rows(g):
        r, cc = g.shape
        return g.reshape(N_DEV, r // N_DEV, cc).astype(BF16)

    def ffn_backward(l, i, dxo, s, xin, hk, gk, uk, ak, yk, jbase, lnj):
        dxres, dy, (da,), dmod[l][jbase + 2], sg["ln_g"][l][lnj], sg["ln_b"][l][lnj] = _six(postnorm_bwd(
            dxo, xin, s[yk], vec(l, jbase + 2), row(full["ln_g"][l, lnj]), row(full["ln_b"][l, lnj]), w_ffn[l, i][1], [FF], 0.5, seq, "ffn_down_bwd"))
        keys = [k for k, _ in pending]
        outs = ffn_bwd_in(da, s[gk], s[uk], w_ffn[l, i][0], xin, vec(l, jbase + 1), dxres, seq, "ffn_up_bwd_c%d" % len(keys), carry=[(a, False) for _, a in pending])
        dg, du, dxi, dmod[l][jbase], dmod[l][jbase + 1] = outs[:5]
        got_w.update(zip(keys, outs[5:]))
        pending.clear()
        pending.append((("ffn_out", l, i), cut_rows(mm_tn(s[ak], dy, "wg_ffn_out"))))
        pending.append((("ffn_in", l, i), cut_cols(jnp.concatenate([mm_tn(s[hk], dg, "wg_ffn_in"), mm_tn(s[hk], du, "wg_ffn_in")], axis=1))))
        return dxi

    for l in reversed(range(DEPTH)):
        s = saved[l]
        e = l // 2
        dx = ffn_backward(l, 1, dx, s, s["x2"], "h3", "g3", "u3", "a3", "y3", 6, 2)
        ks = [1024, POOL_DIM] if l % 2 == 0 else [CONF_DIM, LRU_DIM]
        dxres, dy, (dya, dyb), dmod[l][5], sg["ln_g"][l][1], sg["ln_b"][l][1] = _six(postnorm_bwd(
            dx, s["x1"], s["y2"], vec(l, 5), row(full["ln_g"][l, 1]), row(full["ln_b"][l, 1]), w_mix[l][1], ks, 1.0, seq, "mix_out_bwd"))
        pending.append((("mix_out", l), cut_rows(jnp.concatenate([mm_tn(s["ya"], dy, "wg_mix_a"), mm_tn(s["yb"], dy, "wg_mix_b")], axis=0))))
        if l % 2 == 0:
            (dz, dxbc, ddt, sg["ssd_conv_w"][e], dcb, ddtb, dalog, ddsk, dng) = ssd_bwd(
                dya, s["z"], s["xbc"], s["dtr"], s["sprev"], full["ssd_conv_w"][e], row(full["ssd_conv_b"][e]), _pad_lanes(full["ssd_dt_bias"][e]),
                _pad_lanes(full["ssd_a_log"][e]), _pad_lanes(full["ssd_d"][e]), row(full["ssd_norm_g"][e]), seq, "ssd_bwd")
            sg["ssd_conv_b"][e], sg["ssd_norm_g"][e] = dcb[0], dng[0]
            sg["ssd_dt_bias"][e], sg["ssd_a_log"][e], sg["ssd_d"][e] = ddtb[0, :SSD_HEADS], dalog[0, :SSD_HEADS], ddsk[0, :SSD_HEADS]
            du, sg["pool_w"][e], dps = pool_bwd(dyb, s["u"], full["pool_w"][e], row(full["pool_scale"][e]), seq, "pool_bwd")
            sg["pool_scale"][e] = dps[0]
            dparts = [dz, dxbc, du, ddt]
            dx, dmod[l][3], dmod[l][4] = proj_bwd_in(dparts, w_mix[l][0], s["x1"], vec(l, 4), dxres, seq, "ev_in_bwd")
            gz, gxbc, gu, gdt = [mm_tn(s["h2"], dp, "wg_ev_in") for dp in dparts]
            pending.append((("mix_in", l), cut_cols(jnp.concatenate([gz, gxbc, gdt[:, :SSD_HEADS], gu], axis=1))))
        else:
            dvg, sg["conf_dw_w"][e], dcb, dlg, dlb = conf_bwd(dya, s["conv"], s["vg"], full["conf_dw_w"][e], row(full["conf_ln_g"][e]), row(full["conf_ln_b"][e]), seq, "conf_bwd")
            sg["conf_dw_b"][e], sg["conf_ln_g"][e], sg["conf_ln_b"][e] = dcb[0], dlg[0], dlb[0]
            (dxr, dgr, sg["lru_conv_w"][e], dcb, sg["lru_wa"][e], dba, sg["lru_wx"][e], dbx, dlam) = lru_bwd(
                dyb, s["xr"], s["gr"], s["hst"], full["lru_conv_w"][e], row(full["lru_conv_b"][e]), full["lru_wa"][e], row(full["lru_ba"][e]),
                full["lru_wx"][e], row(full["lru_bx"][e]), row(full["lru_lambda"][e]), seq, "lru_bwd")
            sg["lru_conv_b"][e], sg["lru_ba"][e], sg["lru_bx"][e], sg["lru_lambda"][e] = dcb[0], dba[0], dbx[0], dlam[0]
            dparts = [dvg, dxr, dgr]
            dx, dmod[l][3], dmod[l][4] = proj_bwd_in(dparts, w_mix[l][0], s["x1"], vec(l, 4), dxres, seq, "od_in_bwd")
            pending.append((("mix_in", l), cut_cols(jnp.concatenate([mm_tn(s["h2"], dp, "wg_od_in") for dp in dparts], axis=1))))
        dx = ffn_backward(l, 0, dx, s, s["x0"], "h1", "g1", "u1", "a1", "y1", 0, 0)
    grad_x = dx.reshape(nb, seq, D)

    dmod_mine = jnp.stack([jnp.concatenate([d[:, 0, :] for d in dmod[l]], axis=-1) for l in range(DEPTH)])
    sg["ada_b"] = [jnp.sum(dmod_mine[l], axis=0) for l in range(DEPTH)]
    sg["ln_g"] = [jnp.concatenate(r, axis=0) for r in sg["ln_g"]]
    sg["ln_b"] = [jnp.concatenate(r, axis=0) for r in sg["ln_b"]]
    small_names = list(SMALL)
    keys = [k for k, _ in pending]
    outs = exchange([(a, False) for _, a in pending]
                    + [(dmod_mine.reshape(DEPTH, nb, N_DEV, cols).transpose(2, 0, 1, 3), False),
                       (_pack([jnp.stack(sg[k]).reshape(SMALL[k][0]) for k in small_names]), True)], "x_last")
    got_w.update(zip(keys, outs))
    dmod_x, parts = outs[len(keys):]
    g_ada_w = ada_bwd(c_all, dmod_x.transpose(1, 0, 2, 3).reshape(DEPTH, N_DEV * nb, cols), "ada_bwd")

    summed = _unpack(sum_parts(parts, "sum_smallgrad"), [SMALL[k][0] for k in small_names])
    grads = {}
    for k, g in zip(small_names, summed):
        ax = SMALL[k][1]
        grads[k] = g if ax is None else lax.dynamic_slice_in_dim(g, me * p[k].shape[ax], p[k].shape[ax], axis=ax)
    loc_shapes = [p[k].shape for k in small_names]
    _, d_s, m_s, v_s = adamw(_pack([p[k] for k in small_names]), _pack([p["m_" + k] for k in small_names]), _pack([p["v_" + k] for k in small_names]),
                             _pack([grads[k] for k in small_names])[None], "adamw_small")
    delta = dict(zip(small_names, _unpack(d_s, loc_shapes)))
    new_m = dict(zip(small_names, _unpack(m_s, loc_shapes)))
    new_v = dict(zip(small_names, _unpack(v_s, loc_shapes)))

    big_parts = {
        "ada_w": g_ada_w[None],
        "ffn_w_in": jnp.stack([jnp.stack([got_w["ffn_in", l, i] for i in range(2)], axis=1) for l in range(DEPTH)], axis=1),
        "ffn_w_out": jnp.stack([jnp.stack([got_w["ffn_out", l, i] for i in range(2)], axis=1) for l in range(DEPTH)], axis=1),
        "ev_w_in": jnp.stack([got_w["mix_in", l] for l in (0, 2)], axis=1),
        "ev_w_out": jnp.stack([got_w["mix_out", l] for l in (0, 2)], axis=1),
        "od_w_in": jnp.stack([got_w["mix_in", l] for l in (1, 3)], axis=1),
        "od_w_out": jnp.stack([got_w["mix_out", l] for l in (1, 3)], axis=1),
    }
    for k in BIG:
        w = p[k]
        r2 = (math.prod(w.shape[:-1]), w.shape[-1])
        gp = big_parts[k]
        out = adamw(w.reshape(r2), p["m_" + k].reshape(r2), p["v_" + k].reshape(r2), gp.reshape((gp.shape[0],) + r2), "adamw_" + k)
        grads[k], delta[k], new_m[k], new_v[k] = [o.reshape(w.shape) for o in out]

    return (loss, grad_x, *[grads[k] for k in WEIGHTS], *[delta[k] for k in WEIGHTS], *[new_m[k] for k in WEIGHTS], *[new_v[k] for k in WEIGHTS])


def _six(outs):
    return outs[0], outs[1], tuple(outs[2:-3]), outs[-3], outs[-2], outs[-1]
```

```python
import functools
import math

import jax
import jax.numpy as jnp
from jax import lax
from jax.experimental import pallas as pl
from jax.experimental.pallas import tpu as pltpu

F32 = jnp.float32
BF16 = jnp.bfloat16
HI = lax.Precision.HIGHEST

N_DEV = 8
D = 1024
DEPTH = 4
N_MOD = 9
FF = 2816
ALPHA = (2.0 * DEPTH) ** 0.25
EPS = 1e-5
SSD_Q = 128
SSD_HEADS = 16
SSD_P = 64
SSD_N = 128
SSD_XBC = 1536
POOL_WINDOWS = (2, 4, 8, 16)
POOL_DIM = 512
CONF_DIM = 512
CONF_K = 31
LRU_DIM = 1024
LRU_HEADS = 8
LRU_C = 8.0
EV_SPLITS = (1024, 1536, 512, 128)
OD_SPLITS = (1024, 1024, 1024)
LR, B1, B2, AEPS, WD, STEP = 0.001, 0.9, 0.999, 1e-08, 0.01, 10

LANE = 128
SUB = 8
MIB = 1024 * 1024
VMEM_LIMIT = 48 * MIB
TM = 512


def _params(sem, vmem=VMEM_LIMIT):
    return pltpu.CompilerParams(dimension_semantics=sem, vmem_limit_bytes=vmem)


def _sds(shape, dtype):
    return jax.ShapeDtypeStruct(shape, dtype)


def _modulate(x, sh, sc):
    return x * (1.0 + sc) + sh


def _postnorm(x, y, g, lng, lnb, *, coef):
    z = ALPHA * x + coef * (1.0 + g) * y
    mu = jnp.mean(z, axis=-1, keepdims=True)
    zc = z - mu
    var = jnp.mean(zc * zc, axis=-1, keepdims=True)
    return zc * lax.rsqrt(var + EPS) * lng + lnb


def _place():
    mx, my, mc = lax.axis_index("x"), lax.axis_index("y"), lax.axis_index("c")

    def at(r):
        px = 1 - mx if r & 4 else mx
        py = 1 - my if r & 2 else my
        pc = 1 - mc if r & 1 else mc
        return (px, py, pc), 4 * px + 2 * py + pc

    return 4 * mx + 2 * my + mc, at


def _carry_plan(items):
    hbm = pl.BlockSpec(memory_space=pltpu.HBM)
    k = len(items)
    shapes = [_sds((N_DEV,) + a.shape if g else a.shape, a.dtype) for a, g in items]
    scratch = [pltpu.SemaphoreType.DMA((k * (N_DEV - 1),)), pltpu.SemaphoreType.DMA((k * (N_DEV - 1),)), pltpu.SemaphoreType.DMA((k,))] if k else []
    return [hbm] * k, [hbm] * k, shapes, scratch


def _remote(src, dst, sems, s, pos):
    return pltpu.make_async_remote_copy(src_ref=src, dst_ref=dst, send_sem=sems[0].at[s], recv_sem=sems[1].at[s],
                                        device_id=pos, device_id_type=pl.DeviceIdType.MESH)


def _carry_start(gathers, x_refs, o_refs, sems):
    me, at = _place()
    for a, (gather, x_ref, o_ref) in enumerate(zip(gathers, x_refs, o_refs)):
        base = a * (N_DEV - 1)
        pltpu.make_async_copy(x_ref if gather else x_ref.at[me], o_ref.at[me], sems[2].at[a]).start()
        if gather:
            for s, r in enumerate((1, 4, 2, 6)):
                _remote(x_ref, o_ref.at[me], sems, base + s, at(r)[0]).start()
        else:
            for r in range(1, N_DEV):
                pos, pid = at(r)
                _remote(x_ref.at[pid], o_ref.at[me], sems, base + r - 1, pos).start()


def _carry_pass_on(gathers, x_refs, o_refs, sems):
    _, at = _place()
    sibling = at(1)[0]
    for a, (gather, x_ref, o_ref) in enumerate(zip(gathers, x_refs, o_refs)):
        if gather:
            base = a * (N_DEV - 1)
            for j, r in enumerate((4, 2, 6)):
                pos, pid = at(r)
                _remote(x_ref, o_ref.at[pid], sems, base + 1 + j, pos).wait_recv()
                _remote(o_ref.at[pid], o_ref.at[pid], sems, base + 4 + j, sibling).start()


def _carry_wait(gathers, x_refs, o_refs, sems):
    me, at = _place()
    for a, (gather, x_ref, o_ref) in enumerate(zip(gathers, x_refs, o_refs)):
        base = a * (N_DEV - 1)
        if gather:
            sib_pos, sib_id = at(1)
            _remote(x_ref, o_ref.at[sib_id], sems, base, sib_pos).wait_recv()
            for j, r in enumerate((4, 2, 6)):
                _remote(x_ref, o_ref.at[at(r | 1)[1]], sems, base + 4 + j, sib_pos).wait_recv()
            for s in range(N_DEV - 1):
                _remote(x_ref, o_ref.at[me], sems, base + s, sib_pos).wait_send()
            pltpu.make_async_copy(x_ref, o_ref.at[me], sems[2].at[a]).wait()
        else:
            for r in range(1, N_DEV):
                pos, pid = at(r)
                _remote(x_ref.at[pid], o_ref.at[pid], sems, base + r - 1, pos).wait_recv()
            for r in range(1, N_DEV):
                pos, pid = at(r)
                _remote(x_ref.at[pid], o_ref.at[me], sems, base + r - 1, pos).wait_send()
            pltpu.make_async_copy(x_ref.at[me], o_ref.at[me], sems[2].at[a]).wait()


def exchange(items, name):
    gathers = [g for _, g in items]
    k = len(items)
    in_specs, out_specs, shapes, scratch = _carry_plan(items)

    def body(*refs):
        x_refs, o_refs, sems = refs[:k], refs[k : 2 * k], refs[2 * k :]
        _carry_start(gathers, x_refs, o_refs, sems)
        _carry_pass_on(gathers, x_refs, o_refs, sems)
        _carry_wait(gathers, x_refs, o_refs, sems)

    return pl.pallas_call(
        body,
        in_specs=in_specs,
        out_specs=out_specs,
        out_shape=shapes,
        scratch_shapes=scratch,
        compiler_params=pltpu.CompilerParams(has_side_effects=True),
        name=name,
    )(*[a for a, _ in items])


def ffn_up(x, sh, sc, w, seq, name, carry=()):
    n = x.shape[0]
    tn = FF // 2
    nj = FF // tn
    ni = n // TM
    tps = seq // TM
    k = len(carry)
    gathers = [g for _, g in carry]
    c_in, c_out, c_shapes, c_scratch = _carry_plan(carry)

    def body(x_ref, sh_ref, sc_ref, wg_ref, wu_ref, *rest):
        cx, (h_ref, g_ref, u_ref, a_ref), co, sems = rest[:k], rest[k : k + 4], rest[k + 4 : 2 * k + 4], rest[2 * k + 4 :]
        i, j = pl.program_id(0), pl.program_id(1)
        if k:
            @pl.when((i == 0) & (j == 0))
            def _():
                _carry_start(gathers, cx, co, sems)

        @pl.when(j == 0)
        def _():
            h_ref[...] = _modulate(x_ref[...], sh_ref[0], sc_ref[0]).astype(BF16)

        h = h_ref[...]
        g = jnp.dot(h, wg_ref[...], preferred_element_type=F32)
        u = jnp.dot(h, wu_ref[...], preferred_element_type=F32)
        g_ref[...] = g.astype(BF16)
        u_ref[...] = u.astype(BF16)
        a_ref[...] = (g * jax.nn.sigmoid(g) * u).astype(BF16)
        if k:
            @pl.when((i == max(ni - 2, 0)) & (j == nj - 1))
            def _():
                _carry_pass_on(gathers, cx, co, sems)

            @pl.when((i == ni - 1) & (j == nj - 1))
            def _():
                _carry_wait(gathers, cx, co, sems)

    vec = pl.BlockSpec((1, 1, D), lambda i, j: (i // tps, 0, 0))
    col = pl.BlockSpec((TM, tn), lambda i, j: (i, j))
    return pl.pallas_call(
        body,
        grid=(ni, nj),
        in_specs=[
            pl.BlockSpec((TM, D), lambda i, j: (i, 0)),
            vec,
            vec,
            pl.BlockSpec((D, tn), lambda i, j: (0, j)),
            pl.BlockSpec((D, tn), lambda i, j: (0, j + nj)),
        ]
        + c_in,
        out_specs=[pl.BlockSpec((TM, D), lambda i, j: (i, 0)), col, col, col] + c_out,
        out_shape=[_sds((n, D), BF16), _sds((n, FF), BF16), _sds((n, FF), BF16), _sds((n, FF), BF16)] + c_shapes,
        scratch_shapes=c_scratch,
        compiler_params=_params(("arbitrary", "arbitrary")),
        name=name,
    )(x, sh, sc, w, w, *[a for a, _ in carry])


def mod_mm(x, sh, sc, w, splits, seq, name):
    n = x.shape[0]
    m = w.shape[1]
    tps = seq // TM
    offs = [sum(splits[:k]) for k in range(len(splits))]

    def body(x_ref, sh_ref, sc_ref, w_ref, h_ref, *outs):
        h = _modulate(x_ref[...], sh_ref[0], sc_ref[0]).astype(BF16)
        h_ref[...] = h
        for o_ref, off, wd in zip(outs, offs, splits):
            o_ref[...] = jnp.dot(h, w_ref[:, off : off + wd], preferred_element_type=F32)

    vec = pl.BlockSpec((1, 1, D), lambda i: (i // tps, 0, 0))
    return pl.pallas_call(
        body,
        grid=(n // TM,),
        in_specs=[pl.BlockSpec((TM, D), lambda i: (i, 0)), vec, vec, pl.BlockSpec((D, m), lambda i: (0, 0))],
        out_specs=[pl.BlockSpec((TM, D), lambda i: (i, 0))] + [pl.BlockSpec((TM, wd), lambda i: (i, 0)) for wd in splits],
        out_shape=[_sds((n, D), BF16)] + [_sds((n, wd), F32) for wd in splits],
        compiler_params=_params(("arbitrary",)),
        name=name,
    )(x, sh, sc, w)


def mm_postnorm(parts, w, x, g, lng, lnb, coef, seq, name):
    n = x.shape[0]
    tps = seq // TM
    ks = [p.shape[1] for p in parts]
    offs = [sum(ks[:k]) for k in range(len(ks))]
    npart = len(parts)

    def body(*refs):
        a_refs = refs[:npart]
        w_ref, x_ref, g_ref, lng_ref, lnb_ref, xn_ref, y_ref = refs[npart:]
        y = None
        for a_ref, off, k in zip(a_refs, offs, ks):
            t = jnp.dot(a_ref[...], w_ref[off : off + k, :], preferred_element_type=F32)
            y = t if y is None else y + t
        y_ref[...] = y
        xn_ref[...] = _postnorm(x_ref[...], y, g_ref[0], lng_ref[...], lnb_ref[...], coef=coef)

    row = pl.BlockSpec((TM, D), lambda i: (i, 0))
    one = pl.BlockSpec((1, D), lambda i: (0, 0))
    return pl.pallas_call(
        body,
        grid=(n // TM,),
        in_specs=[pl.BlockSpec((TM, k), lambda i: (i, 0)) for k in ks]
        + [pl.BlockSpec((sum(ks), D), lambda i: (0, 0)), row, pl.BlockSpec((1, 1, D), lambda i: (i // tps, 0, 0)), one, one],
        out_specs=[row, row],
        out_shape=[_sds((n, D), F32), _sds((n, D), F32)],
        compiler_params=_params(("arbitrary",)),
        name=name,
    )(*parts, w, x, g, lng, lnb)


def postnorm_bwd(dxn, x, y, g, lng, lnb, w, ks, coef, seq, name, carry=()):
    n = x.shape[0]
    ni = n // TM
    tps = seq // TM
    nb = n // seq
    offs = [sum(ks[:k]) for k in range(len(ks))]
    npart = len(ks)
    f = functools.partial(_postnorm, coef=coef)
    nc = len(carry)
    gathers = [gt for _, gt in carry]
    c_in, c_out, c_shapes, c_scratch = _carry_plan(carry)

    def body(dxn_ref, x_ref, y_ref, g_ref, lng_ref, lnb_ref, w_ref, *rest):
        cx, rest = rest[:nc], rest[nc:]
        dx_ref, dy_ref = rest[:2]
        da_refs = rest[2 : 2 + npart]
        dg_ref, dlng_ref, dlnb_ref = rest[2 + npart : 5 + npart]
        co, sems = rest[5 + npart : 5 + npart + nc], rest[5 + npart + nc :]
        i = pl.program_id(0)
        if nc:
            @pl.when(i == 0)
            def _():
                _carry_start(gathers, cx, co, sems)

        _, vjp = jax.vjp(f, x_ref[...], y_ref[...], g_ref[0], lng_ref[...], lnb_ref[...])
        dx, dy, dg, dlng, dlnb = vjp(dxn_ref[...])
        dx_ref[...] = dx
        dyb = dy.astype(BF16)
        dy_ref[...] = dyb
        for da_ref, off, k in zip(da_refs, offs, ks):
            da_ref[...] = lax.dot_general(
                dyb, w_ref[off : off + k, :], (((1,), (1,)), ((), ())), preferred_element_type=F32
            ).astype(BF16)

        @pl.when(i % tps == 0)
        def _():
            dg_ref[...] = jnp.zeros_like(dg_ref)

        @pl.when(i == 0)
        def _():
            dlng_ref[...] = jnp.zeros_like(dlng_ref)
            dlnb_ref[...] = jnp.zeros_like(dlnb_ref)

        dg_ref[0] += dg
        dlng_ref[...] += dlng
        dlnb_ref[...] += dlnb
        if nc:
            @pl.when(i == ni - 1)
            def _():
                _carry_pass_on(gathers, cx, co, sems)
                _carry_wait(gathers, cx, co, sems)

    row = pl.BlockSpec((TM, D), lambda i: (i, 0))
    one = pl.BlockSpec((1, D), lambda i: (0, 0))
    vec = pl.BlockSpec((1, 1, D), lambda i: (i // tps, 0, 0))
    return pl.pallas_call(
        body,
        grid=(ni,),
        in_specs=[row, row, row, vec, one, one, pl.BlockSpec((sum(ks), D), lambda i: (0, 0))] + c_in,
        out_specs=[row, row] + [pl.BlockSpec((TM, k), lambda i: (i, 0)) for k in ks] + [vec, one, one] + c_out,
        out_shape=[_sds((n, D), F32), _sds((n, D), BF16)]
        + [_sds((n, k), BF16) for k in ks]
        + [_sds((nb, 1, D), F32), _sds((1, D), F32), _sds((1, D), F32)]
        + c_shapes,
        scratch_shapes=c_scratch,
        compiler_params=_params(("arbitrary",)),
        name=name,
    )(dxn, x, y, g, lng, lnb, w, *[a for a, _ in carry])


def _mod_bwd_finish(dh, x_ref, sc_ref, dxres_ref, dx_ref, dsh_ref, dsc_ref, first_of_seq):
    dx_ref[...] = dxres_ref[...] + dh * (1.0 + sc_ref[0])

    @pl.when(first_of_seq)
    def _():
        dsh_ref[...] = jnp.zeros_like(dsh_ref)
        dsc_ref[...] = jnp.zeros_like(dsc_ref)

    dsh_ref[0] += jnp.sum(dh, axis=0, keepdims=True)
    dsc_ref[0] += jnp.sum(dh * x_ref[...], axis=0, keepdims=True)


def ffn_bwd_in(da, g, u, w, x, sc, dxres, seq, name, carry=()):
    n = x.shape[0]
    tn = FF // 2
    nj = FF // tn
    tm = TM // 2
    ni = n // tm
    tps = seq // tm
    nb = n // seq
    k = len(carry)
    gathers = [gt for _, gt in carry]
    c_in, c_out, c_shapes, c_scratch = _carry_plan(carry)

    def body(da_ref, g_ref, u_ref, wg_ref, wu_ref, x_ref, sc_ref, dxres_ref, *rest):
        cx, (dg_ref, du_ref, dx_ref, dsh_ref, dsc_ref), co, (acc, *sems) = rest[:k], rest[k : k + 5], rest[k + 5 : 2 * k + 5], rest[2 * k + 5 :]
        i, j = pl.program_id(0), pl.program_id(1)
        if k:
            @pl.when((i == 0) & (j == 0))
            def _():
                _carry_start(gathers, cx, co, sems)

        gv = g_ref[...].astype(F32)
        uv = u_ref[...].astype(F32)
        dav = da_ref[...].astype(F32)
        s = jax.nn.sigmoid(gv)
        dgv = (dav * uv * s * (1.0 + gv * (1.0 - s))).astype(BF16)
        duv = (dav * gv * s).astype(BF16)
        dg_ref[...] = dgv
        du_ref[...] = duv
        nt = (((1,), (1,)), ((), ()))
        t = lax.dot_general(dgv, wg_ref[...], nt, preferred_element_type=F32) + lax.dot_general(
            duv, wu_ref[...], nt, preferred_element_type=F32
        )

        @pl.when(j == 0)
        def _():
            acc[...] = t

        @pl.when(j > 0)
        def _():
            acc[...] += t

        @pl.when(j == nj - 1)
        def _():
            _mod_bwd_finish(acc[...], x_ref, sc_ref, dxres_ref, dx_ref, dsh_ref, dsc_ref, i % tps == 0)

        if k:
            @pl.when((i == ni - 1) & (j == nj - 1))
            def _():
                _carry_wait(gathers, cx, co, sems)

    row = pl.BlockSpec((tm, D), lambda i, j: (i, 0))
    col = pl.BlockSpec((tm, tn), lambda i, j: (i, j))
    vec = pl.BlockSpec((1, 1, D), lambda i, j: (i // tps, 0, 0))
    return pl.pallas_call(
        body,
        grid=(ni, nj),
        in_specs=[
            col,
            col,
            col,
            pl.BlockSpec((D, tn), lambda i, j: (0, j)),
            pl.BlockSpec((D, tn), lambda i, j: (0, j + nj)),
            row,
            vec,
            row,
        ]
        + c_in,
        out_specs=[col, col, row, vec, vec] + c_out,
        out_shape=[_sds((n, FF), BF16), _sds((n, FF), BF16), _sds((n, D), F32), _sds((nb, 1, D), F32), _sds((nb, 1, D), F32)] + c_shapes,
        scratch_shapes=[pltpu.VMEM((tm, D), F32)] + c_scratch,
        compiler_params=_params(("arbitrary", "arbitrary")),
        name=name,
    )(da, g, u, w, w, x, sc, dxres, *[a for a, _ in carry])


def proj_bwd_in(dparts, w, x, sc, dxres, seq, name):
    n = x.shape[0]
    tps = seq // TM
    nb = n // seq
    ms = [p.shape[1] for p in dparts]
    offs = [sum(ms[:k]) for k in range(len(ms))]
    npart = len(ms)

    def body(*refs):
        d_refs = refs[:npart]
        w_ref, x_ref, sc_ref, dxres_ref, dx_ref, dsh_ref, dsc_ref = refs[npart:]
        dh = None
        for d_ref, off, m in zip(d_refs, offs, ms):
            t = lax.dot_general(d_ref[...], w_ref[:, off : off + m], (((1,), (1,)), ((), ())), preferred_element_type=F32)
            dh = t if dh is None else dh + t
        _mod_bwd_finish(dh, x_ref, sc_ref, dxres_ref, dx_ref, dsh_ref, dsc_ref, pl.program_id(0) % tps == 0)

    row = pl.BlockSpec((TM, D), lambda i: (i, 0))
    vec = pl.BlockSpec((1, 1, D), lambda i: (i // tps, 0, 0))
    return pl.pallas_call(
        body,
        grid=(n // TM,),
        in_specs=[pl.BlockSpec((TM, m), lambda i: (i, 0)) for m in ms] + [pl.BlockSpec((D, sum(ms)), lambda i: (0, 0)), row, vec, row],
        out_specs=[row, vec, vec],
        out_shape=[_sds((n, D), F32), _sds((nb, 1, D), F32), _sds((nb, 1, D), F32)],
        compiler_params=_params(("arbitrary",)),
        name=name,
    )(*dparts, w, x, sc, dxres)


def mm_tn(a, b, name):
    n, k1 = a.shape
    k2 = b.shape[1]
    t1 = k1 if k1 <= 1536 else _tile(k1, 1536)
    t2 = k2 if k2 <= 1536 else _tile(k2, 1536)
    tk = 1024 if n % 1024 == 0 else n
    nk = n // tk

    def body(a_ref, b_ref, o_ref):
        t = lax.dot_general(a_ref[...], b_ref[...], (((0,), (0,)), ((), ())), preferred_element_type=F32)

        @pl.when(pl.program_id(2) == 0)
        def _():
            o_ref[...] = t

        @pl.when(pl.program_id(2) > 0)
        def _():
            o_ref[...] += t

    return pl.pallas_call(
        body,
        grid=(k1 // t1, k2 // t2, nk),
        in_specs=[pl.BlockSpec((tk, t1), lambda i, j, k: (k, i)), pl.BlockSpec((tk, t2), lambda i, j, k: (k, j))],
        out_specs=pl.BlockSpec((t1, t2), lambda i, j, k: (i, j)),
        out_shape=_sds((k1, k2), F32),
        compiler_params=_params(("arbitrary", "arbitrary", "arbitrary")),
        name=name,
    )(a, b)


def _tile(n, cap):
    best = LANE
    for t in range(LANE, cap + 1, LANE):
        if n % t == 0:
            best = t
    return best


def _conv_taps(ext_ref, w_ref, halo, tq, kk):
    acc = None
    for k in range(kk):
        t = w_ref[k : k + 1, :] * ext_ref[pl.ds(halo - (kk - 1 - k), tq), :]
        acc = t if acc is None else acc + t
    return acc


def _conv_taps_t(ext2_ref, w_ref, tq, kk):
    acc = None
    for k in range(kk):
        t = w_ref[k : k + 1, :] * ext2_ref[pl.ds(kk - 1 - k, tq), :]
        acc = t if acc is None else acc + t
    return acc


def _conv_dw(ext_ref, dy, dw_ref, halo, tq, kk):
    for k in range(kk):
        dw_ref[k : k + 1, :] += jnp.sum(dy * ext_ref[pl.ds(halo - (kk - 1 - k), tq), :], axis=0, keepdims=True)


def _halo_spec(rows, width, tq, shift):
    per = tq // rows

    if shift < 0:
        return lambda nblocks: pl.BlockSpec((rows, width), lambda i: (jnp.maximum(i * per - 1, 0), 0))
    return lambda nblocks: pl.BlockSpec((rows, width), lambda i: (jnp.minimum((i + 1) * per, nblocks - 1), 0))


def _ln(c, g, b):
    mu = jnp.mean(c, axis=-1, keepdims=True)
    cc = c - mu
    var = jnp.mean(cc * cc, axis=-1, keepdims=True)
    return cc * lax.rsqrt(var + EPS) * g + b


def _silu(v):
    return v * jax.nn.sigmoid(v)


def _ssd_chunk(conv, dtr, z, s, dt_bias, a_log, dskip, norm_g, tril, e):
    q = SSD_Q
    act = _silu(conv)
    xs, bm, cm = act[:, :1024], act[:, 1024:1280], act[:, 1280:1536]
    lane = lax.broadcasted_iota(jnp.int32, (1, LANE), 1)
    lane_q = lax.broadcasted_iota(jnp.int32, (q, LANE), 1)
    sub_q = lax.broadcasted_iota(jnp.int32, (LANE, q), 0)
    causal = lax.broadcasted_iota(jnp.int32, (q, q), 0) >= lax.broadcasted_iota(jnp.int32, (q, q), 1)
    real = lane < SSD_HEADS
    dt = jnp.where(real, jax.nn.softplus(dtr + dt_bias), 0.0)
    a = jnp.where(real, -jnp.exp(a_log), 0.0)
    da = dt * a
    acs = jnp.dot(tril, da, precision=HI, preferred_element_type=F32)
    acs_t = lax.dot_general(da, tril, (((0,), (1,)), ((), ())), precision=HI, preferred_element_type=F32)
    dt_e = jnp.dot(dt, e, precision=HI, preferred_element_type=F32)
    acs_e = jnp.dot(acs, e, precision=HI, preferred_element_type=F32)
    one8 = jnp.ones((SUB, 1), F32)
    alast_e = jnp.dot(one8 * jnp.sum(da, axis=0, keepdims=True), e, precision=HI, preferred_element_type=F32)[0:1]
    d_e = jnp.dot(one8 * jnp.where(real, dskip, 0.0), e, precision=HI, preferred_element_type=F32)[0:1]
    xdt = xs * dt_e
    nt = (((1,), (1,)), ((), ()))
    tn = (((0,), (0,)), ((), ()))
    ys, snews = [], []
    for g in range(2):
        gl = slice(g * 512, (g + 1) * 512)
        bg = bm[:, g * 128 : (g + 1) * 128].astype(BF16)
        cg = cm[:, g * 128 : (g + 1) * 128].astype(BF16)
        cb = lax.dot_general(cg, bg, nt, preferred_element_type=F32)
        sg = s[:, gl]
        yoff = jnp.dot(cg, sg.astype(BF16), preferred_element_type=F32) * jnp.exp(acs_e[:, gl])
        pairs = []
        for j in range(4):
            xp = xdt[:, g * 512 + j * 128 : g * 512 + (j + 1) * 128].astype(BF16)
            outs = []
            for hh in (g * 8 + 2 * j, g * 8 + 2 * j + 1):
                col = jnp.sum(jnp.where(lane_q == hh, acs, 0.0), axis=1, keepdims=True)
                row = jnp.sum(jnp.where(sub_q == hh, acs_t, 0.0), axis=0, keepdims=True)
                m = cb * jnp.exp(jnp.where(causal, col - row, -1e30))
                outs.append(jnp.dot(m.astype(BF16), xp, preferred_element_type=F32))
            pairs.append(jnp.where(lane_q < SSD_P, outs[0], outs[1]))
        ys.append(jnp.concatenate(pairs, axis=1) + yoff)
        decay = jnp.exp(alast_e[:, gl] - acs_e[:, gl])
        snews.append(
            sg * jnp.exp(alast_e[:, gl]) + lax.dot_general(bg, (xdt[:, gl] * decay).astype(BF16), tn, preferred_element_type=F32)
        )
    y = jnp.concatenate(ys, axis=1) + xs * d_e
    gated = y * _silu(z)
    out = gated * lax.rsqrt(jnp.mean(gated * gated, axis=-1, keepdims=True) + EPS) * norm_g
    return out, jnp.concatenate(snews, axis=1)


def _ssd_consts():
    tril = (lax.broadcasted_iota(jnp.int32, (SSD_Q, SSD_Q), 0) >= lax.broadcasted_iota(jnp.int32, (SSD_Q, SSD_Q), 1)).astype(F32)
    e = (lax.broadcasted_iota(jnp.int32, (LANE, 1024), 0) == lax.broadcasted_iota(jnp.int32, (LANE, 1024), 1) // SSD_P).astype(F32)
    return tril, e


def ssd_fwd(z, xbc, dtr, cw, cb, dt_bias, a_log, dskip, norm_g, seq, name):
    n = z.shape[0]
    q = SSD_Q
    nc = seq // q
    tril, e = _ssd_consts()

    def body(z_ref, xbc_ref, halo_ref, dtr_ref, cw_ref, cb_ref, dtb_ref, alog_ref, dsk_ref, ng_ref, tril_ref, e_ref, y_ref, sprev_ref, s_scr, ext):
        c = pl.program_id(0) % nc

        @pl.when(c == 0)
        def _():
            s_scr[...] = jnp.zeros_like(s_scr)

        ext[0:SUB, :] = jnp.where(c == 0, 0.0, halo_ref[...])
        ext[SUB:, :] = xbc_ref[...]
        conv = _conv_taps(ext, cw_ref, SUB, q, 4) + cb_ref[...]
        sprev_ref[0] = s_scr[...]
        y, snew = _ssd_chunk(conv, dtr_ref[...], z_ref[...], s_scr[...], dtb_ref[...], alog_ref[...], dsk_ref[...], ng_ref[...], tril_ref[...], e_ref[...])
        y_ref[...] = y.astype(BF16)
        s_scr[...] = snew

    def full(shape):
        return pl.BlockSpec(shape, lambda i: (0,) * len(shape))

    return pl.pallas_call(
        body,
        grid=(n // q,),
        in_specs=[
            pl.BlockSpec((q, 1024), lambda i: (i, 0)),
            pl.BlockSpec((q, SSD_XBC), lambda i: (i, 0)),
            _halo_spec(SUB, SSD_XBC, q, -1)(n // SUB),
            pl.BlockSpec((q, LANE), lambda i: (i, 0)),
            full((4, SSD_XBC)),
            full((1, SSD_XBC)),
            full((1, LANE)),
            full((1, LANE)),
            full((1, LANE)),
            full((1, 1024)),
            full((q, q)),
            full((LANE, 1024)),
        ],
        out_specs=[pl.BlockSpec((q, 1024), lambda i: (i, 0)), pl.BlockSpec((1, LANE, 1024), lambda i: (i, 0, 0))],
        out_shape=[_sds((n, 1024), BF16), _sds((n // q, LANE, 1024), F32)],
        scratch_shapes=[pltpu.VMEM((LANE, 1024), F32), pltpu.VMEM((SUB + q, SSD_XBC), F32)],
        compiler_params=_params(("arbitrary",)),
        name=name,
    )(z, xbc, xbc, dtr, cw, cb, dt_bias, a_log, dskip, norm_g, tril, e)


def ssd_bwd(dy, z, xbc, dtr, sprev, cw, cb, dt_bias, a_log, dskip, norm_g, seq, name):
    n = z.shape[0]
    q = SSD_Q
    nc = seq // q
    nchunks = n // q
    tril, e = _ssd_consts()

    def rev(i):
        return (i // nc) * nc + (nc - 1 - i % nc)

    def body(dy_ref, z_ref, xbc_ref, halo_ref, dtr_ref, sprev_ref, cw_ref, cb_ref, dtb_ref, alog_ref, dsk_ref, ng_ref, tril_ref, e_ref,
             dz_ref, dxbc_ref, ddt_ref, dcw_ref, dcb_ref, ddtb_ref, dalog_ref, ddsk_ref, dng_ref, ds_scr, ext, ext2):
        i = pl.program_id(0)
        step = i % nc
        c = nc - 1 - step

        @pl.when(step == 0)
        def _():
            ds_scr[...] = jnp.zeros_like(ds_scr)
            ext2[q:, :] = jnp.zeros((SUB, SSD_XBC), F32)

        @pl.when(i == 0)
        def _():
            for r in (dcw_ref, dcb_ref, ddtb_ref, dalog_ref, ddsk_ref, dng_ref):
                r[...] = jnp.zeros_like(r)

        ext[0:SUB, :] = jnp.where(c == 0, 0.0, halo_ref[...])
        ext[SUB:, :] = xbc_ref[...]
        conv = _conv_taps(ext, cw_ref, SUB, q, 4) + cb_ref[...]
        tril_v, e_v = tril_ref[...], e_ref[...]

        def f(conv, dtr, z, s, dtb, alog, dsk, ng):
            return _ssd_chunk(conv, dtr, z, s, dtb, alog, dsk, ng, tril_v, e_v)

        _, vjp = jax.vjp(f, conv, dtr_ref[...], z_ref[...], sprev_ref[0], dtb_ref[...], alog_ref[...], dsk_ref[...], ng_ref[...])
        dconv, ddtr, dz, dsprev, ddtb, dalog, ddsk, dng = vjp((dy_ref[...].astype(F32), ds_scr[...]))
        ds_scr[...] = dsprev
        dz_ref[...] = dz.astype(BF16)
        ddt_ref[...] = ddtr.astype(BF16)
        ext2[0:q, :] = dconv
        dxbc_ref[...] = _conv_taps_t(ext2, cw_ref, q, 4).astype(BF16)
        ext2[q:, :] = dconv[0:SUB, :]
        _conv_dw(ext, dconv, dcw_ref, SUB, q, 4)
        dcb_ref[...] += jnp.sum(dconv, axis=0, keepdims=True)
        ddtb_ref[...] += ddtb
        dalog_ref[...] += dalog
        ddsk_ref[...] += ddsk
        dng_ref[...] += dng

    def full(shape):
        return pl.BlockSpec(shape, lambda i: (0,) * len(shape))

    per = q // SUB
    return pl.pallas_call(
        body,
        grid=(nchunks,),
        in_specs=[
            pl.BlockSpec((q, 1024), lambda i: (rev(i), 0)),
            pl.BlockSpec((q, 1024), lambda i: (rev(i), 0)),
            pl.BlockSpec((q, SSD_XBC), lambda i: (rev(i), 0)),
            pl.BlockSpec((SUB, SSD_XBC), lambda i: (jnp.maximum(rev(i) * per - 1, 0), 0)),
            pl.BlockSpec((q, LANE), lambda i: (rev(i), 0)),
            pl.BlockSpec((1, LANE, 1024), lambda i: (rev(i), 0, 0)),
            full((4, SSD_XBC)),
            full((1, SSD_XBC)),
            full((1, LANE)),
            full((1, LANE)),
            full((1, LANE)),
            full((1, 1024)),
            full((q, q)),
            full((LANE, 1024)),
        ],
        out_specs=[
            pl.BlockSpec((q, 1024), lambda i: (rev(i), 0)),
            pl.BlockSpec((q, SSD_XBC), lambda i: (rev(i), 0)),
            pl.BlockSpec((q, LANE), lambda i: (rev(i), 0)),
            full((4, SSD_XBC)),
            full((1, SSD_XBC)),
            full((1, LANE)),
            full((1, LANE)),
            full((1, LANE)),
            full((1, 1024)),
        ],
        out_shape=[
            _sds((n, 1024), BF16),
            _sds((n, SSD_XBC), BF16),
            _sds((n, LANE), BF16),
            _sds((4, SSD_XBC), F32),
            _sds((1, SSD_XBC), F32),
            _sds((1, LANE), F32),
            _sds((1, LANE), F32),
            _sds((1, LANE), F32),
            _sds((1, 1024), F32),
        ],
        scratch_shapes=[pltpu.VMEM((LANE, 1024), F32), pltpu.VMEM((SUB + q, SSD_XBC), F32), pltpu.VMEM((q + SUB, SSD_XBC), F32)],
        compiler_params=_params(("arbitrary",)),
        name=name,
    )(dy, z, xbc, xbc, dtr, sprev, cw, cb, dt_bias, a_log, dskip, norm_g, tril, e)


POOL_HALO = 16
TQ = 512


def _pool_count(pos, w):
    return jnp.minimum(pos + 1.0, float(w))


def pool_fwd(u, pw, scale, seq, name):
    n = u.shape[0]
    tq, halo = TQ, POOL_HALO
    tps = seq // tq

    def body(u_ref, halo_ref, pw_ref, sc_ref, y_ref, ext):
        t0 = pl.program_id(0) % tps
        ext[0:halo, :] = jnp.where(t0 == 0, 0.0, halo_ref[...])
        ext[halo:, :] = u_ref[...]
        pos = (t0 * tq + lax.broadcasted_iota(jnp.int32, (tq, 1), 0)).astype(F32)
        for g, w in enumerate(POOL_WINDOWS):
            ln = slice(g * LANE, (g + 1) * LANE)
            acc = ext[pl.ds(halo, tq), ln]
            for j in range(1, w):
                acc = acc + ext[pl.ds(halo - j, tq), ln]
            pooled = acc / _pool_count(pos, w) - u_ref[:, ln]
            mixed = jnp.dot(pooled.astype(BF16), pw_ref[g].astype(BF16), preferred_element_type=F32)
            y_ref[:, ln] = (mixed * sc_ref[:, ln]).astype(BF16)

    return pl.pallas_call(
        body,
        grid=(n // tq,),
        in_specs=[
            pl.BlockSpec((tq, POOL_DIM), lambda i: (i, 0)),
            _halo_spec(halo, POOL_DIM, tq, -1)(n // halo),
            pl.BlockSpec((4, LANE, LANE), lambda i: (0, 0, 0)),
            pl.BlockSpec((1, POOL_DIM), lambda i: (0, 0)),
        ],
        out_specs=pl.BlockSpec((tq, POOL_DIM), lambda i: (i, 0)),
        out_shape=_sds((n, POOL_DIM), BF16),
        scratch_shapes=[pltpu.VMEM((halo + tq, POOL_DIM), F32)],
        compiler_params=_params(("arbitrary",)),
        name=name,
    )(u, u, pw, scale)


def pool_bwd(dy, u, pw, scale, seq, name):
    n = u.shape[0]
    tq, halo = TQ, POOL_HALO
    tps = seq // tq
    nt = (((1,), (1,)), ((), ()))
    tn = (((0,), (0,)), ((), ()))

    def body(dy_ref, dyn_ref, u_ref, halo_ref, pw_ref, sc_ref, du_ref, dpw_ref, dsc_ref, ext, ext2):
        i = pl.program_id(0)
        t0 = i % tps

        @pl.when(i == 0)
        def _():
            dpw_ref[...] = jnp.zeros_like(dpw_ref)
            dsc_ref[...] = jnp.zeros_like(dsc_ref)

        ext[0:halo, :] = jnp.where(t0 == 0, 0.0, halo_ref[...])
        ext[halo:, :] = u_ref[...]
        pos = (t0 * tq + lax.broadcasted_iota(jnp.int32, (tq, 1), 0)).astype(F32)
        dyv = dy_ref[...].astype(F32)
        dynv = jnp.where(t0 == tps - 1, 0.0, dyn_ref[...].astype(F32))
        for g, w in enumerate(POOL_WINDOWS):
            ln = slice(g * LANE, (g + 1) * LANE)
            wg = pw_ref[g].astype(BF16)
            acc = ext[pl.ds(halo, tq), ln]
            for j in range(1, w):
                acc = acc + ext[pl.ds(halo - j, tq), ln]
            pooled = (acc / _pool_count(pos, w) - u_ref[:, ln]).astype(BF16)
            mixed = jnp.dot(pooled, wg, preferred_element_type=F32)
            dsc_ref[:, ln] += jnp.sum(dyv[:, ln] * mixed, axis=0, keepdims=True)
            dmix = (dyv[:, ln] * sc_ref[:, ln]).astype(BF16)
            dpw_ref[g] += lax.dot_general(pooled, dmix, tn, preferred_element_type=F32)
            dpool = lax.dot_general(dmix, wg, nt, preferred_element_type=F32)
            dmix_n = (dynv[:, ln] * sc_ref[:, ln]).astype(BF16)
            dpool_n = lax.dot_general(dmix_n, wg, nt, preferred_element_type=F32)
            ext2[0:tq, ln] = dpool / _pool_count(pos, w)
            ext2[tq:, ln] = dpool_n * (1.0 / w)
            acc2 = ext2[pl.ds(0, tq), ln]
            for j in range(1, w):
                acc2 = acc2 + ext2[pl.ds(j, tq), ln]
            du_ref[:, ln] = (acc2 - dpool).astype(BF16)

    return pl.pallas_call(
        body,
        grid=(n // tq,),
        in_specs=[
            pl.BlockSpec((tq, POOL_DIM), lambda i: (i, 0)),
            _halo_spec(halo, POOL_DIM, tq, +1)(n // halo),
            pl.BlockSpec((tq, POOL_DIM), lambda i: (i, 0)),
            _halo_spec(halo, POOL_DIM, tq, -1)(n // halo),
            pl.BlockSpec((4, LANE, LANE), lambda i: (0, 0, 0)),
            pl.BlockSpec((1, POOL_DIM), lambda i: (0, 0)),
        ],
        out_specs=[
            pl.BlockSpec((tq, POOL_DIM), lambda i: (i, 0)),
            pl.BlockSpec((4, LANE, LANE), lambda i: (0, 0, 0)),
            pl.BlockSpec((1, POOL_DIM), lambda i: (0, 0)),
        ],
        out_shape=[_sds((n, POOL_DIM), BF16), _sds((4, LANE, LANE), F32), _sds((1, POOL_DIM), F32)],
        scratch_shapes=[pltpu.VMEM((halo + tq, POOL_DIM), F32), pltpu.VMEM((tq + halo, POOL_DIM), F32)],
        compiler_params=_params(("arbitrary",)),
        name=name,
    )(dy, dy, u, u, pw, scale)


CONF_HALO = 32


def _conf_post(c, g, b):
    return _silu(_ln(c, g, b))


def conf_fwd(vg, w, b, lng, lnb, seq, name):
    n = vg.shape[0]
    tq, halo, kk = TQ, CONF_HALO, CONF_K
    tps = seq // tq
    c = CONF_DIM

    def body(vg_ref, halo_ref, w_ref, b_ref, lng_ref, lnb_ref, y_ref, conv_ref, ext):
        t0 = pl.program_id(0) % tps
        hv = halo_ref[...]
        ext[0:halo, :] = jnp.where(t0 == 0, 0.0, hv[:, :c] * jax.nn.sigmoid(hv[:, c:]))
        ext[halo:, :] = vg_ref[:, :c] * jax.nn.sigmoid(vg_ref[:, c:])
        conv = _conv_taps(ext, w_ref, halo, tq, kk) + b_ref[...]
        conv_ref[...] = conv
        y_ref[...] = _conf_post(conv, lng_ref[...], lnb_ref[...]).astype(BF16)

    one = pl.BlockSpec((1, c), lambda i: (0, 0))
    return pl.pallas_call(
        body,
        grid=(n // tq,),
        in_specs=[pl.BlockSpec((tq, 2 * c), lambda i: (i, 0)), _halo_spec(halo, 2 * c, tq, -1)(n // halo), pl.BlockSpec((kk, c), lambda i: (0, 0)), one, one, one],
        out_specs=[pl.BlockSpec((tq, c), lambda i: (i, 0)), pl.BlockSpec((tq, c), lambda i: (i, 0))],
        out_shape=[_sds((n, c), BF16), _sds((n, c), F32)],
        scratch_shapes=[pltpu.VMEM((halo + tq, c), F32)],
        compiler_params=_params(("arbitrary",)),
        name=name,
    )(vg, vg, w, b, lng, lnb)


def conf_bwd(dy, conv, vg, w, lng, lnb, seq, name):
    n = vg.shape[0]
    tq, halo, kk = TQ, CONF_HALO, CONF_K
    tps = seq // tq
    c = CONF_DIM

    def body(dy_ref, dyn_ref, conv_ref, convn_ref, vg_ref, halo_ref, w_ref, lng_ref, lnb_ref, dvg_ref, dw_ref, db_ref, dlng_ref, dlnb_ref, ext, ext2):
        i = pl.program_id(0)
        t0 = i % tps

        @pl.when(i == 0)
        def _():
            for r in (dw_ref, db_ref, dlng_ref, dlnb_ref):
                r[...] = jnp.zeros_like(r)

        _, vjp = jax.vjp(_conf_post, conv_ref[...], lng_ref[...], lnb_ref[...])
        dconv, dlng, dlnb = vjp(dy_ref[...].astype(F32))
        _, vjpn = jax.vjp(_conf_post, convn_ref[...], lng_ref[...], lnb_ref[...])
        dconv_n = vjpn(dyn_ref[...].astype(F32))[0]
        ext2[0:tq, :] = dconv
        ext2[tq:, :] = jnp.where(t0 == tps - 1, 0.0, dconv_n)
        dh = _conv_taps_t(ext2, w_ref, tq, kk)
        hv = halo_ref[...]
        ext[0:halo, :] = jnp.where(t0 == 0, 0.0, hv[:, :c] * jax.nn.sigmoid(hv[:, c:]))
        v = vg_ref[:, :c]
        s = jax.nn.sigmoid(vg_ref[:, c:])
        ext[halo:, :] = v * s
        _conv_dw(ext, dconv, dw_ref, halo, tq, kk)
        db_ref[...] += jnp.sum(dconv, axis=0, keepdims=True)
        dlng_ref[...] += dlng
        dlnb_ref[...] += dlnb
        dvg_ref[:, :c] = (dh * s).astype(BF16)
        dvg_ref[:, c:] = (dh * v * s * (1.0 - s)).astype(BF16)

    one = pl.BlockSpec((1, c), lambda i: (0, 0))
    tile = pl.BlockSpec((tq, c), lambda i: (i, 0))
    nxt = _halo_spec(halo, c, tq, +1)(n // halo)
    return pl.pallas_call(
        body,
        grid=(n // tq,),
        in_specs=[tile, nxt, tile, nxt, pl.BlockSpec((tq, 2 * c), lambda i: (i, 0)), _halo_spec(halo, 2 * c, tq, -1)(n // halo),
                  pl.BlockSpec((kk, c), lambda i: (0, 0)), one, one],
        out_specs=[pl.BlockSpec((tq, 2 * c), lambda i: (i, 0)), pl.BlockSpec((kk, c), lambda i: (0, 0)), one, one, one],
        out_shape=[_sds((n, 2 * c), BF16), _sds((kk, c), F32), _sds((1, c), F32), _sds((1, c), F32), _sds((1, c), F32)],
        scratch_shapes=[pltpu.VMEM((halo + tq, c), F32), pltpu.VMEM((tq + halo, c), F32)],
        compiler_params=_params(("arbitrary",)),
        name=name,
    )(dy, dy, conv, conv, vg, vg, w, lng, lnb)


TL = 256


def _expm1_neg(t):
    p = t * (1.0 + t * (1.0 / 2 + t * (1.0 / 6 + t * (1.0 / 24 + t * (1.0 / 120 + t * (1.0 / 720 + t * (1.0 / 5040)))))))
    return jnp.where(t > -0.35, p, jnp.exp(t) - 1.0)


def _lru_gate(xc, ra, ia, ba, bx, lam):
    r = jax.nn.sigmoid(ra + ba)
    i = jax.nn.sigmoid(ia + bx)
    log_a = -LRU_C * r * jax.nn.softplus(-lam)
    return jnp.exp(log_a), jnp.sqrt(-_expm1_neg(2.0 * log_a)) * (i * xc)


def _lru_out(h, gr):
    return h * jax.nn.gelu(gr)


def _scan_rows(a, b, tq, reverse):
    row = lax.broadcasted_iota(jnp.int32, (tq, 1), 0)
    d = 1
    while d < tq:
        sh = tq - d if reverse else d
        valid = (row < tq - d) if reverse else (row >= d)
        a_s = pltpu.roll(a, sh, 0)
        b_s = pltpu.roll(b, sh, 0)
        b = jnp.where(valid, a * b_s, 0.0) + b
        a = jnp.where(valid, a * a_s, a)
        d *= 2
    return a, b


def _row_of(v, r, tq):
    row = lax.broadcasted_iota(jnp.int32, (tq, 1), 0)
    return jnp.sum(jnp.where(row == r, v, 0.0), axis=0, keepdims=True)


def _head_mm(xc, w_ref):
    return jnp.concatenate(
        [
            jnp.dot(xc[:, h * LANE : (h + 1) * LANE].astype(BF16), w_ref[h].astype(BF16), preferred_element_type=F32)
            for h in range(LRU_HEADS)
        ],
        axis=1,
    )


def lru_fwd(xr, gr, cw, cb, wa, ba, wx, bx, lam, seq, name):
    n = xr.shape[0]
    tq = TL
    tps = seq // tq
    c = LRU_DIM

    def body(xr_ref, halo_ref, gr_ref, cw_ref, cb_ref, wa_ref, ba_ref, wx_ref, bx_ref, lam_ref, y_ref, h_ref, hc, ext):
        t0 = pl.program_id(0) % tps

        @pl.when(t0 == 0)
        def _():
            hc[...] = jnp.zeros_like(hc)

        ext[0:SUB, :] = jnp.where(t0 == 0, 0.0, halo_ref[...])
        ext[SUB:, :] = xr_ref[...]
        xc = _conv_taps(ext, cw_ref, SUB, tq, 4) + cb_ref[...]
        a, b = _lru_gate(xc, _head_mm(xc, wa_ref), _head_mm(xc, wx_ref), ba_ref[...], bx_ref[...], lam_ref[...])
        acum, h0 = _scan_rows(a, b, tq, False)
        h = h0 + acum * hc[0:1, :]
        h_ref[...] = h
        hc[0:1, :] = h_ref[tq - 1 : tq, :]
        y_ref[...] = _lru_out(h, gr_ref[...]).astype(BF16)

    one = pl.BlockSpec((1, c), lambda i: (0, 0))
    tile = pl.BlockSpec((tq, c), lambda i: (i, 0))
    hw = pl.BlockSpec((LRU_HEADS, LANE, LANE), lambda i: (0, 0, 0))
    return pl.pallas_call(
        body,
        grid=(n // tq,),
        in_specs=[tile, _halo_spec(SUB, c, tq, -1)(n // SUB), tile, pl.BlockSpec((4, c), lambda i: (0, 0)), one, hw, one, hw, one, one],
        out_specs=[tile, tile],
        out_shape=[_sds((n, c), BF16), _sds((n, c), F32)],
        scratch_shapes=[pltpu.VMEM((SUB, c), F32), pltpu.VMEM((SUB + tq, c), F32)],
        compiler_params=_params(("arbitrary",)),
        name=name,
    )(xr, xr, gr, cw, cb, wa, ba, wx, bx, lam)


def lru_bwd(dy, xr, gr, h, cw, cb, wa, ba, wx, bx, lam, seq, name):
    n = xr.shape[0]
    tq = TL
    tps = seq // tq
    ntile = n // tq
    c = LRU_DIM
    per = tq // SUB
    nt = (((1,), (1,)), ((), ()))
    tn = (((0,), (0,)), ((), ()))

    def rev(i):
        return (i // tps) * tps + (tps - 1 - i % tps)

    def body(dy_ref, xr_ref, halo_ref, gr_ref, h_ref, hprev_ref, cw_ref, cb_ref, wa_ref, ba_ref, wx_ref, bx_ref, lam_ref,
             dxr_ref, dgr_ref, dcw_ref, dcb_ref, dwa_ref, dba_ref, dwx_ref, dbx_ref, dlam_ref, carry, ext, ext2):
        i = pl.program_id(0)
        step = i % tps
        t0 = tps - 1 - step

        @pl.when(step == 0)
        def _():
            carry[...] = jnp.zeros_like(carry)
            ext2[tq:, :] = jnp.zeros((SUB, c), F32)

        @pl.when(i == 0)
        def _():
            for r in (dcw_ref, dcb_ref, dwa_ref, dba_ref, dwx_ref, dbx_ref, dlam_ref):
                r[...] = jnp.zeros_like(r)

        ext[0:SUB, :] = jnp.where(t0 == 0, 0.0, halo_ref[...])
        ext[SUB:, :] = xr_ref[...]
        xc = _conv_taps(ext, cw_ref, SUB, tq, 4) + cb_ref[...]
        (a, _), vjp_gate = jax.vjp(_lru_gate, xc, _head_mm(xc, wa_ref), _head_mm(xc, wx_ref), ba_ref[...], bx_ref[...], lam_ref[...])
        hv = h_ref[...]
        _, vjp_out = jax.vjp(_lru_out, hv, gr_ref[...])
        dh, dgr = vjp_out(dy_ref[...].astype(F32))
        dgr_ref[...] = dgr.astype(BF16)
        row = lax.broadcasted_iota(jnp.int32, (tq, 1), 0)
        a_up = jnp.where(row == tq - 1, carry[0:1, :], pltpu.roll(a, tq - 1, 0))
        acum, l0 = _scan_rows(a_up, dh, tq, True)
        lamv = l0 + acum * carry[1:2, :]
        carry[0:1, :] = _row_of(a, 0, tq)
        carry[1:2, :] = _row_of(lamv, 0, tq)
        hprev = jnp.where(row == 0, jnp.where(t0 == 0, 0.0, hprev_ref[SUB - 1 : SUB, :]), pltpu.roll(hv, 1, 0))
        dxc, dra, dia, dba, dbx, dlam = vjp_gate((lamv * hprev, lamv))
        dba_ref[...] += dba
        dbx_ref[...] += dbx
        dlam_ref[...] += dlam
        pieces = []
        for hh in range(LRU_HEADS):
            ln = slice(hh * LANE, (hh + 1) * LANE)
            xh = xc[:, ln].astype(BF16)
            drh = dra[:, ln].astype(BF16)
            dih = dia[:, ln].astype(BF16)
            dwa_ref[hh] += lax.dot_general(xh, drh, tn, preferred_element_type=F32)
            dwx_ref[hh] += lax.dot_general(xh, dih, tn, preferred_element_type=F32)
            pieces.append(
                lax.dot_general(drh, wa_ref[hh].astype(BF16), nt, preferred_element_type=F32)
                + lax.dot_general(dih, wx_ref[hh].astype(BF16), nt, preferred_element_type=F32)
            )
        dxc = dxc + jnp.concatenate(pieces, axis=1)
        ext2[0:tq, :] = dxc
        dxr_ref[...] = _conv_taps_t(ext2, cw_ref, tq, 4).astype(BF16)
        ext2[tq:, :] = ext2[0:SUB, :]
        _conv_dw(ext, dxc, dcw_ref, SUB, tq, 4)
        dcb_ref[...] += jnp.sum(dxc, axis=0, keepdims=True)

    one = pl.BlockSpec((1, c), lambda i: (0, 0))
    tile = pl.BlockSpec((tq, c), lambda i: (rev(i), 0))
    prev = pl.BlockSpec((SUB, c), lambda i: (jnp.maximum(rev(i) * per - 1, 0), 0))
    hw = pl.BlockSpec((LRU_HEADS, LANE, LANE), lambda i: (0, 0, 0))
    cw4 = pl.BlockSpec((4, c), lambda i: (0, 0))
    return pl.pallas_call(
        body,
        grid=(ntile,),
        in_specs=[tile, tile, prev, tile, tile, prev, cw4, one, hw, one, hw, one, one],
        out_specs=[tile, tile, cw4, one, hw, one, hw, one, one],
        out_shape=[_sds((n, c), BF16), _sds((n, c), BF16), _sds((4, c), F32), _sds((1, c), F32), _sds((LRU_HEADS, LANE, LANE), F32),
                   _sds((1, c), F32), _sds((LRU_HEADS, LANE, LANE), F32), _sds((1, c), F32), _sds((1, c), F32)],
        scratch_shapes=[pltpu.VMEM((SUB, c), F32), pltpu.VMEM((SUB + tq, c), F32), pltpu.VMEM((tq + SUB, c), F32)],
        compiler_params=_params(("arbitrary",)),
        name=name,
    )(dy, xr, xr, gr, h, h, cw, cb, wa, ba, wx, bx, lam)


def ada_fwd(c_all, w, b, name):
    nl, _, cols = w.shape
    nb = c_all.shape[0]

    def body(c_ref, w_ref, b_ref, o_ref):
        sc = _silu(c_ref[...]).astype(BF16)
        o_ref[0] = jnp.dot(sc, w_ref[0].astype(BF16), preferred_element_type=F32) + b_ref[0]

    return pl.pallas_call(
        body,
        grid=(nl,),
        in_specs=[pl.BlockSpec((nb, D), lambda l: (0, 0)), pl.BlockSpec((1, D, cols), lambda l: (l, 0, 0)), pl.BlockSpec((1, 1, cols), lambda l: (l, 0, 0))],
        out_specs=pl.BlockSpec((1, nb, cols), lambda l: (l, 0, 0)),
        out_shape=_sds((nl, nb, cols), F32),
        compiler_params=_params(("arbitrary",)),
        name=name,
    )(c_all, w, b)


def ada_bwd(c_all, dmod, name):
    nl, nb, cols = dmod.shape

    def body(c_ref, d_ref, o_ref):
        sc = _silu(c_ref[...]).astype(BF16)
        o_ref[0] = lax.dot_general(sc, d_ref[0].astype(BF16), (((0,), (0,)), ((), ())), preferred_element_type=F32)

    return pl.pallas_call(
        body,
        grid=(nl,),
        in_specs=[pl.BlockSpec((nb, D), lambda l: (0, 0)), pl.BlockSpec((1, nb, cols), lambda l: (l, 0, 0))],
        out_specs=pl.BlockSpec((1, D, cols), lambda l: (l, 0, 0)),
        out_shape=_sds((nl, D, cols), F32),
        compiler_params=_params(("arbitrary",)),
        name=name,
    )(c_all, dmod)


def loss_grad(y, target, name):
    n = y.shape[0]

    def body(y_ref, t_ref, dy_ref, l_ref, acc):
        i = pl.program_id(0)

        @pl.when(i == 0)
        def _():
            acc[...] = jnp.zeros_like(acc)

        e = y_ref[...] - t_ref[...]
        dy_ref[...] = e * (1.0 / D)
        acc[...] += jnp.sum(e * e, axis=0, keepdims=True)

        @pl.when(i == n // TM - 1)
        def _():
            l_ref[...] = jnp.full((1, LANE), 0.5 / D, F32) * jnp.sum(acc[...])

    row = pl.BlockSpec((TM, D), lambda i: (i, 0))
    return pl.pallas_call(
        body,
        grid=(n // TM,),
        in_specs=[row, row],
        out_specs=[row, pl.BlockSpec((1, LANE), lambda i: (0, 0))],
        out_shape=[_sds((n, D), F32), _sds((1, LANE), F32)],
        scratch_shapes=[pltpu.VMEM((1, D), F32)],
        compiler_params=_params(("arbitrary",)),
        name=name,
    )(y, target)


def sum_parts(parts, name):
    ns, r, _ = parts.shape
    tr = _row_tile(r, 1024)

    def body(p_ref, o_ref):
        acc = p_ref[0]
        for k in range(1, ns):
            acc = acc + p_ref[k]
        o_ref[...] = acc

    return pl.pallas_call(
        body,
        grid=(r // tr,),
        in_specs=[pl.BlockSpec((ns, tr, LANE), lambda i: (0, i, 0))],
        out_specs=pl.BlockSpec((tr, LANE), lambda i: (i, 0)),
        out_shape=_sds((r, LANE), F32),
        compiler_params=_params(("arbitrary",)),
        name=name,
    )(parts)


def _row_tile(r, cap):
    if r <= cap:
        return r
    best = None
    for t in range(16, cap + 1, 16):
        if r % t == 0:
            best = t
    assert best is not None, r
    return best


def adamw(w, m, v, gparts, name):
    r, c = w.shape
    ns = gparts.shape[0]
    tr = _row_tile(r, min(512, 256 * 1024 // c))
    c1 = 1.0 - B1**STEP
    c2 = 1.0 - B2**STEP

    def body(w_ref, m_ref, v_ref, g_ref, go_ref, d_ref, mo_ref, vo_ref):
        g = g_ref[0].astype(F32)
        for k in range(1, ns):
            g = g + g_ref[k].astype(F32)
        mn = B1 * m_ref[...] + (1.0 - B1) * g
        vn = B2 * v_ref[...] + (1.0 - B2) * (g * g)
        go_ref[...] = g
        mo_ref[...] = mn
        vo_ref[...] = vn
        d_ref[...] = -LR * ((mn / c1) / (jnp.sqrt(vn / c2) + AEPS) + WD * w_ref[...])

    tile = pl.BlockSpec((tr, c), lambda i: (i, 0))
    return pl.pallas_call(
        body,
        grid=(r // tr,),
        in_specs=[tile, tile, tile, pl.BlockSpec((ns, tr, c), lambda i: (0, i, 0))],
        out_specs=[tile, tile, tile, tile],
        out_shape=[_sds((r, c), F32)] * 4,
        compiler_params=_params(("arbitrary",)),
        name=name,
    )(w, m, v, gparts)


WEIGHTS = ["ada_w", "ada_b", "ln_g", "ln_b", "ffn_w_in", "ffn_w_out", "ev_w_in", "ssd_conv_w", "ssd_conv_b", "ssd_dt_bias",
           "ssd_a_log", "ssd_d", "ssd_norm_g", "pool_w", "pool_scale", "ev_w_out", "od_w_in", "conf_dw_w", "conf_dw_b",
           "conf_ln_g", "conf_ln_b", "lru_conv_w", "lru_conv_b", "lru_wa", "lru_ba", "lru_wx", "lru_bx", "lru_lambda", "od_w_out"]
BIG = ("ada_w", "ffn_w_in", "ffn_w_out", "ev_w_in", "ev_w_out", "od_w_in", "od_w_out")
SMALL = {
    "ada_b": ((4, 9216), None), "ln_g": ((4, 3, 1024), 2), "ln_b": ((4, 3, 1024), 2),
    "ssd_conv_w": ((2, 4, 1536), 2), "ssd_conv_b": ((2, 1536), None), "ssd_dt_bias": ((2, 16), None),
    "ssd_a_log": ((2, 16), None), "ssd_d": ((2, 16), None), "ssd_norm_g": ((2, 1024), None),
    "pool_w": ((2, 4, 128, 128), None), "pool_scale": ((2, 512), None),
    "conf_dw_w": ((2, 31, 512), 2), "conf_dw_b": ((2, 512), 1), "conf_ln_g": ((2, 512), 1), "conf_ln_b": ((2, 512), 1),
    "lru_conv_w": ((2, 4, 1024), 2), "lru_conv_b": ((2, 1024), 1), "lru_wa": ((2, 8, 128, 128), None),
    "lru_ba": ((2, 1024), 1), "lru_wx": ((2, 8, 128, 128), None), "lru_bx": ((2, 1024), 1), "lru_lambda": ((2, 1024), 1),
}
PACK_ROWS = 2 * SUB * LANE


def _pack(arrs, mult=PACK_ROWS):
    flat = jnp.concatenate([a.reshape(-1) for a in arrs])
    pad = (-flat.shape[0]) % mult
    return jnp.pad(flat, (0, pad)).reshape(-1, LANE)


def _unpack(buf, shapes, lead=()):
    flat = buf.reshape(lead + (-1,))
    out, off = [], 0
    for s in shapes:
        k = math.prod(s)
        out.append(flat[..., off : off + k].reshape(lead + tuple(s)))
        off += k
    return out


def _pad_lanes(v):
    return jnp.pad(v, (0, LANE - v.shape[0]))[None]


def kernel(x, c, ada_w, ada_b, ln_g, ln_b, ffn_w_in, ffn_w_out, ev_w_in, ssd_conv_w, ssd_conv_b, ssd_dt_bias, ssd_a_log, ssd_d, ssd_norm_g, pool_w, pool_scale, ev_w_out, od_w_in, conf_dw_w, conf_dw_b, conf_ln_g, conf_ln_b, lru_conv_w, lru_conv_b, lru_wa, lru_ba, lru_wx, lru_bx, lru_lambda, od_w_out, loss_target, m_ada_w, m_ada_b, m_ln_g, m_ln_b, m_ffn_w_in, m_ffn_w_out, m_ev_w_in, m_ssd_conv_w, m_ssd_conv_b, m_ssd_dt_bias, m_ssd_a_log, m_ssd_d, m_ssd_norm_g, m_pool_w, m_pool_scale, m_ev_w_out, m_od_w_in, m_conf_dw_w, m_conf_dw_b, m_conf_ln_g, m_conf_ln_b, m_lru_conv_w, m_lru_conv_b, m_lru_wa, m_lru_ba, m_lru_wx, m_lru_bx, m_lru_lambda, m_od_w_out, v_ada_w, v_ada_b, v_ln_g, v_ln_b, v_ffn_w_in, v_ffn_w_out, v_ev_w_in, v_ssd_conv_w, v_ssd_conv_b, v_ssd_dt_bias, v_ssd_a_log, v_ssd_d, v_ssd_norm_g, v_pool_w, v_pool_scale, v_ev_w_out, v_od_w_in, v_conf_dw_w, v_conf_dw_b, v_conf_ln_g, v_conf_ln_b, v_lru_conv_w, v_lru_conv_b, v_lru_wa, v_lru_ba, v_lru_wx, v_lru_bx, v_lru_lambda, v_od_w_out):
    p = dict(locals())
    nb, seq, _ = x.shape
    n = nb * seq
    me = 4 * lax.axis_index("x") + 2 * lax.axis_index("y") + lax.axis_index("c")
    sharded = [k for k, (_, ax) in SMALL.items() if ax is not None]

    def cols_of(g):
        return jnp.moveaxis(g, 0, 1).reshape(g.shape[1], N_DEV * g.shape[2])

    def rows_of(g):
        return g.reshape(N_DEV * g.shape[1], g.shape[2])

    def ev_in_of(g):
        w = cols_of(g)
        return jnp.concatenate([w[:, :2560], w[:, 2576:], jnp.pad(w[:, 2560:2576], ((0, 0), (0, LANE - SSD_HEADS)))], axis=1)

    sh_ffn_in, sh_ffn_out = ffn_w_in.astype(BF16), ffn_w_out.astype(BF16)
    sh_mix_in = [ev_w_in.astype(BF16), od_w_in.astype(BF16)]
    sh_mix_out = [ev_w_out.astype(BF16), od_w_out.astype(BF16)]

    def ffn_items(l, i):
        return [(sh_ffn_in[l, i], True), (sh_ffn_out[l, i], True)]

    def mix_items(l):
        return [(sh_mix_in[l % 2][l // 2], True), (sh_mix_out[l % 2][l // 2], True)]

    sm_local_shapes = [p[k].shape for k in sharded]
    g_in, g_out, sm_all = exchange(ffn_items(0, 0) + [(_pack([p[k] for k in sharded] + [c]), True)], "ag_first")
    w_ffn = {(0, 0): (cols_of(g_in), rows_of(g_out))}
    w_mix = {}
    got = _unpack(sm_all, sm_local_shapes + [c.shape], lead=(N_DEV,))
    full = {k: p[k] for k, (_, ax) in SMALL.items() if ax is None}
    for k, g in zip(sharded, got[:-1]):
        full[k] = jnp.moveaxis(g, 0, SMALL[k][1]).reshape(SMALL[k][0])
    c_all = got[-1].reshape(N_DEV * nb, D)

    cols = ada_w.shape[-1]
    ada_b_loc = lax.dynamic_slice_in_dim(ada_b, me * cols, cols, axis=1)[:, None, :]
    mod_cols = ada_fwd(c_all, ada_w, ada_b_loc, "ada_fwd")
    (mod_x,) = exchange([(mod_cols.reshape(DEPTH, N_DEV, nb, cols).transpose(1, 0, 2, 3), False)], "a2a_mod")
    mod = mod_x.transpose(1, 2, 0, 3).reshape(DEPTH, nb, N_MOD, 1, D)

    def vec(l, j):
        return mod[l, :, j]

    def row(a):
        return a[None]

    xs = x.reshape(n, D)
    saved = []
    for l in range(DEPTH):
        s = {"x0": xs}
        e = l // 2
        s["h1"], s["g1"], s["u1"], s["a1"], gm_in, gm_out, g_in, g_out = ffn_up(
            xs, vec(l, 0), vec(l, 1), w_ffn[l, 0][0], seq, "ffn_up_c4", carry=mix_items(l) + ffn_items(l, 1))
        w_mix[l] = ((ev_in_of if l % 2 == 0 else cols_of)(gm_in), rows_of(gm_out))
        w_ffn[l, 1] = (cols_of(g_in), rows_of(g_out))
        x1, s["y1"] = mm_postnorm([s["a1"]], w_ffn[l, 0][1], xs, vec(l, 2), row(full["ln_g"][l, 0]), row(full["ln_b"][l, 0]), 0.5, seq, "ffn_down")
        s["x1"] = x1
        if l % 2 == 0:
            s["h2"], s["z"], s["xbc"], s["u"], s["dtr"] = mod_mm(x1, vec(l, 3), vec(l, 4), w_mix[l][0], EV_SPLITS, seq, "ev_in")
            s["ya"], s["sprev"] = ssd_fwd(s["z"], s["xbc"], s["dtr"], full["ssd_conv_w"][e], row(full["ssd_conv_b"][e]), _pad_lanes(full["ssd_dt_bias"][e]),
                                          _pad_lanes(full["ssd_a_log"][e]), _pad_lanes(full["ssd_d"][e]), row(full["ssd_norm_g"][e]), seq, "ssd_fwd")
            s["yb"] = pool_fwd(s["u"], full["pool_w"][e], row(full["pool_scale"][e]), seq, "pool_fwd")
        else:
            s["h2"], s["vg"], s["xr"], s["gr"] = mod_mm(x1, vec(l, 3), vec(l, 4), w_mix[l][0], OD_SPLITS, seq, "od_in")
            s["ya"], s["conv"] = conf_fwd(s["vg"], full["conf_dw_w"][e], row(full["conf_dw_b"][e]), row(full["conf_ln_g"][e]), row(full["conf_ln_b"][e]), seq, "conf_fwd")
            s["yb"], s["hst"] = lru_fwd(s["xr"], s["gr"], full["lru_conv_w"][e], row(full["lru_conv_b"][e]), full["lru_wa"][e], row(full["lru_ba"][e]),
                                        full["lru_wx"][e], row(full["lru_bx"][e]), row(full["lru_lambda"][e]), seq, "lru_fwd")
        x2, s["y2"] = mm_postnorm([s["ya"], s["yb"]], w_mix[l][1], x1, vec(l, 5), row(full["ln_g"][l, 1]), row(full["ln_b"][l, 1]), 1.0, seq, "mix_out")
        s["x2"] = x2
        if l + 1 < DEPTH:
            s["h3"], s["g3"], s["u3"], s["a3"], g_in, g_out = ffn_up(x2, vec(l, 6), vec(l, 7), w_ffn[l, 1][0], seq, "ffn_up_c2", carry=ffn_items(l + 1, 0))
            w_ffn[l + 1, 0] = (cols_of(g_in), rows_of(g_out))
        else:
            s["h3"], s["g3"], s["u3"], s["a3"] = ffn_up(x2, vec(l, 6), vec(l, 7), w_ffn[l, 1][0], seq, "ffn_up")
        xs, s["y3"] = mm_postnorm([s["a3"]], w_ffn[l, 1][1], x2, vec(l, 8), row(full["ln_g"][l, 2]), row(full["ln_b"][l, 2]), 0.5, seq, "ffn_down")
        saved.append(s)

    dx, loss_row = loss_grad(xs, loss_target.reshape(n, D), "loss")
    loss = lax.psum(loss_row[0, 0], ("x", "y", "c"))

    sg = {k: [None] * shape[0] for k, (shape, _) in SMALL.items()}
    sg["ln_g"] = [[None] * 3 for _ in range(DEPTH)]
    sg["ln_b"] = [[None] * 3 for _ in range(DEPTH)]
    dmod = [[None] * N_MOD for _ in range(DEPTH)]
    pending, got_w = [], {}

    def cut_cols(g):
        r, cc = g.shape
        return g.reshape(r, N_DEV, cc // N_DEV).transpose(1, 0, 2).astype(BF16)

    def cut_rows(g):
        r, cc = g.shape
        return g.reshape(N_DEV, r // N_DEV, cc).astype(BF16)

    def take(only_out):
        sel = [j for j, (key, _) in enumerate(pending) if key[0].endswith("_out") or not only_out]
        items = [pending[j] for j in sel]
        pending[:] = [it for j, it in enumerate(pending) if j not in sel]
        return [k for k, _ in items], [(a, False) for _, a in items]

    def postnorm_backward(dxo, xin, y, g, lng, lnb, w, ks, coef, name):
        keys, carry = take(True)
        outs = postnorm_bwd(dxo, xin, y, g, lng, lnb, w, ks, coef, seq, name + "_c%d" % len(keys), carry=carry)
        got_w.update(zip(keys, outs[5 + len(ks):]))
        return outs[0], outs[1], outs[2 : 2 + len(ks)], outs[2 + len(ks)], outs[3 + len(ks)], outs[4 + len(ks)]

    def ffn_backward(l, i, dxo, s, xin, hk, gk, uk, ak, yk, jbase, lnj):
        dxres, dy, (da,), dmod[l][jbase + 2], sg["ln_g"][l][lnj], sg["ln_b"][l][lnj] = postnorm_backward(
            dxo, xin, s[yk], vec(l, jbase + 2), row(full["ln_g"][l, lnj]), row(full["ln_b"][l, lnj]), w_ffn[l, i][1], [FF], 0.5, "ffn_down_bwd")
        keys, carry = take(False)
        outs = ffn_bwd_in(da, s[gk], s[uk], w_ffn[l, i][0], xin, vec(l, jbase + 1), dxres, seq, "ffn_up_bwd_c%d" % len(keys), carry=carry)
        dg, du, dxi, dmod[l][jbase], dmod[l][jbase + 1] = outs[:5]
        got_w.update(zip(keys, outs[5:]))
        pending.append((("ffn_out", l, i), cut_rows(mm_tn(s[ak], dy, "wg_ffn_out"))))
        pending.append((("ffn_in", l, i), cut_cols(jnp.concatenate([mm_tn(s[hk], dg, "wg_ffn_in"), mm_tn(s[hk], du, "wg_ffn_in")], axis=1))))
        return dxi

    for l in reversed(range(DEPTH)):
        s = saved[l]
        e = l // 2
        dx = ffn_backward(l, 1, dx, s, s["x2"], "h3", "g3", "u3", "a3", "y3", 6, 2)
        ks = [1024, POOL_DIM] if l % 2 == 0 else [CONF_DIM, LRU_DIM]
        dxres, dy, (dya, dyb), dmod[l][5], sg["ln_g"][l][1], sg["ln_b"][l][1] = postnorm_backward(
            dx, s["x1"], s["y2"], vec(l, 5), row(full["ln_g"][l, 1]), row(full["ln_b"][l, 1]), w_mix[l][1], ks, 1.0, "mix_out_bwd")
        pending.append((("mix_out", l), cut_rows(jnp.concatenate([mm_tn(s["ya"], dy, "wg_mix_a"), mm_tn(s["yb"], dy, "wg_mix_b")], axis=0))))
        if l % 2 == 0:
            (dz, dxbc, ddt, sg["ssd_conv_w"][e], dcb, ddtb, dalog, ddsk, dng) = ssd_bwd(
                dya, s["z"], s["xbc"], s["dtr"], s["sprev"], full["ssd_conv_w"][e], row(full["ssd_conv_b"][e]), _pad_lanes(full["ssd_dt_bias"][e]),
                _pad_lanes(full["ssd_a_log"][e]), _pad_lanes(full["ssd_d"][e]), row(full["ssd_norm_g"][e]), seq, "ssd_bwd")
            sg["ssd_conv_b"][e], sg["ssd_norm_g"][e] = dcb[0], dng[0]
            sg["ssd_dt_bias"][e], sg["ssd_a_log"][e], sg["ssd_d"][e] = ddtb[0, :SSD_HEADS], dalog[0, :SSD_HEADS], ddsk[0, :SSD_HEADS]
            du, sg["pool_w"][e], dps = pool_bwd(dyb, s["u"], full["pool_w"][e], row(full["pool_scale"][e]), seq, "pool_bwd")
            sg["pool_scale"][e] = dps[0]
            dparts = [dz, dxbc, du, ddt]
            dx, dmod[l][3], dmod[l][4] = proj_bwd_in(dparts, w_mix[l][0], s["x1"], vec(l, 4), dxres, seq, "ev_in_bwd")
            gz, gxbc, gu, gdt = [mm_tn(s["h2"], dp, "wg_ev_in") for dp in dparts]
            pending.append((("mix_in", l), cut_cols(jnp.concatenate([gz, gxbc, gdt[:, :SSD_HEADS], gu], axis=1))))
        else:
            dvg, sg["conf_dw_w"][e], dcb, dlg, dlb = conf_bwd(dya, s["conv"], s["vg"], full["conf_dw_w"][e], row(full["conf_ln_g"][e]), row(full["conf_ln_b"][e]), seq, "conf_bwd")
            sg["conf_dw_b"][e], sg["conf_ln_g"][e], sg["conf_ln_b"][e] = dcb[0], dlg[0], dlb[0]
            (dxr, dgr, sg["lru_conv_w"][e], dcb, sg["lru_wa"][e], dba, sg["lru_wx"][e], dbx, dlam) = lru_bwd(
                dyb, s["xr"], s["gr"], s["hst"], full["lru_conv_w"][e], row(full["lru_conv_b"][e]), full["lru_wa"][e], row(full["lru_ba"][e]),
                full["lru_wx"][e], row(full["lru_bx"][e]), row(full["lru_lambda"][e]), seq, "lru_bwd")
            sg["lru_conv_b"][e], sg["lru_ba"][e], sg["lru_bx"][e], sg["lru_lambda"][e] = dcb[0], dba[0], dbx[0], dlam[0]
            dparts = [dvg, dxr, dgr]
            dx, dmod[l][3], dmod[l][4] = proj_bwd_in(dparts, w_mix[l][0], s["x1"], vec(l, 4), dxres, seq, "od_in_bwd")
            pending.append((("mix_in", l), cut_cols(jnp.concatenate([mm_tn(s["h2"], dp, "wg_od_in") for dp in dparts], axis=1))))
        dx = ffn_backward(l, 0, dx, s, s["x0"], "h1", "g1", "u1", "a1", "y1", 0, 0)
    grad_x = dx.reshape(nb, seq, D)

    dmod_mine = jnp.stack([jnp.concatenate([d[:, 0, :] for d in dmod[l]], axis=-1) for l in range(DEPTH)])
    sg["ada_b"] = [jnp.sum(dmod_mine[l], axis=0) for l in range(DEPTH)]
    sg["ln_g"] = [jnp.concatenate(r, axis=0) for r in sg["ln_g"]]
    sg["ln_b"] = [jnp.concatenate(r, axis=0) for r in sg["ln_b"]]
    small_names = list(SMALL)
    sg_packed = _pack([jnp.stack(sg[k]).reshape(SMALL[k][0]) for k in small_names], N_DEV * PACK_ROWS)
    keys, carry = take(False)
    outs = exchange(carry + [(dmod_mine.reshape(DEPTH, nb, N_DEV, cols).transpose(2, 0, 1, 3), False),
                             (sg_packed.reshape(N_DEV, -1, LANE), False)], "x_last")
    got_w.update(zip(keys, outs))
    dmod_x, parts = outs[len(keys):]
    g_ada_w = ada_bwd(c_all, dmod_x.transpose(1, 0, 2, 3).reshape(DEPTH, N_DEV * nb, cols), "ada_bwd")

    (sg_sum,) = exchange([(sum_parts(parts, "sum_smallgrad"), True)], "ag_smallsum")
    summed = _unpack(sg_sum.reshape(-1, LANE), [SMALL[k][0] for k in small_names])
    grads = {}
    for k, g in zip(small_names, summed):
        ax = SMALL[k][1]
        grads[k] = g if ax is None else lax.dynamic_slice_in_dim(g, me * p[k].shape[ax], p[k].shape[ax], axis=ax)
    loc_shapes = [p[k].shape for k in small_names]
    _, d_s, m_s, v_s = adamw(_pack([p[k] for k in small_names]), _pack([p["m_" + k] for k in small_names]), _pack([p["v_" + k] for k in small_names]),
                             _pack([grads[k] for k in small_names])[None], "adamw_small")
    delta = dict(zip(small_names, _unpack(d_s, loc_shapes)))
    new_m = dict(zip(small_names, _unpack(m_s, loc_shapes)))
    new_v = dict(zip(small_names, _unpack(v_s, loc_shapes)))

    big_parts = {
        "ada_w": g_ada_w[None],
        "ffn_w_in": jnp.stack([jnp.stack([got_w["ffn_in", l, i] for i in range(2)], axis=1) for l in range(DEPTH)], axis=1),
        "ffn_w_out": jnp.stack([jnp.stack([got_w["ffn_out", l, i] for i in range(2)], axis=1) for l in range(DEPTH)], axis=1),
        "ev_w_in": jnp.stack([got_w["mix_in", l] for l in (0, 2)], axis=1),
        "ev_w_out": jnp.stack([got_w["mix_out", l] for l in (0, 2)], axis=1),
        "od_w_in": jnp.stack([got_w["mix_in", l] for l in (1, 3)], axis=1),
        "od_w_out": jnp.stack([got_w["mix_out", l] for l in (1, 3)], axis=1),
    }
    for k in BIG:
        w = p[k]
        r2 = (math.prod(w.shape[:-1]), w.shape[-1])
        gp = big_parts[k]
        out = adamw(w.reshape(r2), p["m_" + k].reshape(r2), p["v_" + k].reshape(r2), gp.reshape((gp.shape[0],) + r2), "adamw_" + k)
        grads[k], delta[k], new_m[k], new_v[k] = [o.reshape(w.shape) for o in out]

    return (loss, grad_x, *[grads[k] for k in WEIGHTS], *[delta[k] for k in WEIGHTS], *[new_m[k] for k in WEIGHTS], *[new_v[k] for k in WEIGHTS])
```

```python
import functools
import math

import jax
import jax.numpy as jnp
from jax import lax
from jax.experimental import pallas as pl
from jax.experimental.pallas import tpu as pltpu

F32 = jnp.float32
BF16 = jnp.bfloat16
HI = lax.Precision.HIGHEST

N_DEV = 8
D = 1024
DEPTH = 4
N_MOD = 9
FF = 2816
ALPHA = (2.0 * DEPTH) ** 0.25
EPS = 1e-5
SSD_Q = 128
SSD_HEADS = 16
SSD_P = 64
SSD_N = 128
SSD_XBC = 1536
POOL_WINDOWS = (2, 4, 8, 16)
POOL_DIM = 512
CONF_DIM = 512
CONF_K = 31
LRU_DIM = 1024
LRU_HEADS = 8
LRU_C = 8.0
EV_SPLITS = (1024, 1536, 512, 128)
OD_SPLITS = (1024, 1024, 1024)
LR, B1, B2, AEPS, WD, STEP = 0.001, 0.9, 0.999, 1e-08, 0.01, 10

LANE = 128
SUB = 8
MIB = 1024 * 1024
VMEM_LIMIT = 48 * MIB
TM = 512
SPLIT_ROWS = 256


def _params(sem, vmem=VMEM_LIMIT):
    return pltpu.CompilerParams(dimension_semantics=sem, vmem_limit_bytes=vmem)


def _sds(shape, dtype):
    return jax.ShapeDtypeStruct(shape, dtype)


def _modulate(x, sh, sc):
    return x * (1.0 + sc) + sh


def _postnorm(x, y, g, lng, lnb, *, coef):
    z = ALPHA * x + coef * (1.0 + g) * y
    mu = jnp.mean(z, axis=-1, keepdims=True)
    zc = z - mu
    var = jnp.mean(zc * zc, axis=-1, keepdims=True)
    return zc * lax.rsqrt(var + EPS) * lng + lnb


def _place():
    mx, my, mc = lax.axis_index("x"), lax.axis_index("y"), lax.axis_index("c")

    def at(r):
        px = 1 - mx if r & 4 else mx
        py = 1 - my if r & 2 else my
        pc = 1 - mc if r & 1 else mc
        return (px, py, pc), 4 * px + 2 * py + pc

    return 4 * mx + 2 * my + mc, at


def _carry_plan(items):
    hbm = pl.BlockSpec(memory_space=pltpu.HBM)
    k = len(items)
    shapes = [_sds((N_DEV,) + a.shape if g else a.shape, a.dtype) for a, g in items]
    scratch = [pltpu.SemaphoreType.DMA((k * (N_DEV - 1),)), pltpu.SemaphoreType.DMA((k * (N_DEV - 1),)), pltpu.SemaphoreType.DMA((k,))] if k else []
    return [hbm] * k, [hbm] * k, shapes, scratch


def _remote(src, dst, sems, s, pos):
    return pltpu.make_async_remote_copy(src_ref=src, dst_ref=dst, send_sem=sems[0].at[s], recv_sem=sems[1].at[s],
                                        device_id=pos, device_id_type=pl.DeviceIdType.MESH)


def _carry_start(gathers, x_refs, o_refs, sems):
    me, at = _place()
    for a, (gather, x_ref, o_ref) in enumerate(zip(gathers, x_refs, o_refs)):
        base = a * (N_DEV - 1)
        pltpu.make_async_copy(x_ref if gather else x_ref.at[me], o_ref.at[me], sems[2].at[a]).start()
        if gather:
            for s, r in enumerate((1, 4, 2, 6)):
                _remote(x_ref, o_ref.at[me], sems, base + s, at(r)[0]).start()
        else:
            for r in range(1, N_DEV):
                pos, pid = at(r)
                _remote(x_ref.at[pid], o_ref.at[me], sems, base + r - 1, pos).start()


def _carry_pass_on(gathers, x_refs, o_refs, sems):
    _, at = _place()
    sibling = at(1)[0]
    for a, (gather, x_ref, o_ref) in enumerate(zip(gathers, x_refs, o_refs)):
        if gather:
            base = a * (N_DEV - 1)
            for j, r in enumerate((4, 2, 6)):
                pos, pid = at(r)
                _remote(x_ref, o_ref.at[pid], sems, base + 1 + j, pos).wait_recv()
                _remote(o_ref.at[pid], o_ref.at[pid], sems, base + 4 + j, sibling).start()


def _carry_wait(gathers, x_refs, o_refs, sems):
    me, at = _place()
    for a, (gather, x_ref, o_ref) in enumerate(zip(gathers, x_refs, o_refs)):
        base = a * (N_DEV - 1)
        if gather:
            sib_pos, sib_id = at(1)
            _remote(x_ref, o_ref.at[sib_id], sems, base, sib_pos).wait_recv()
            for j, r in enumerate((4, 2, 6)):
                _remote(x_ref, o_ref.at[at(r | 1)[1]], sems, base + 4 + j, sib_pos).wait_recv()
            for s in range(N_DEV - 1):
                _remote(x_ref, o_ref.at[me], sems, base + s, sib_pos).wait_send()
            pltpu.make_async_copy(x_ref, o_ref.at[me], sems[2].at[a]).wait()
        else:
            for r in range(1, N_DEV):
                pos, pid = at(r)
                _remote(x_ref.at[pid], o_ref.at[pid], sems, base + r - 1, pos).wait_recv()
            for r in range(1, N_DEV):
                pos, pid = at(r)
                _remote(x_ref.at[pid], o_ref.at[me], sems, base + r - 1, pos).wait_send()
            pltpu.make_async_copy(x_ref.at[me], o_ref.at[me], sems[2].at[a]).wait()


def exchange(items, name):
    gathers = [g for _, g in items]
    k = len(items)
    in_specs, out_specs, shapes, scratch = _carry_plan(items)

    def body(*refs):
        x_refs, o_refs, sems = refs[:k], refs[k : 2 * k], refs[2 * k :]
        _carry_start(gathers, x_refs, o_refs, sems)
        _carry_pass_on(gathers, x_refs, o_refs, sems)
        _carry_wait(gathers, x_refs, o_refs, sems)

    return pl.pallas_call(
        body,
        in_specs=in_specs,
        out_specs=out_specs,
        out_shape=shapes,
        scratch_shapes=scratch,
        compiler_params=pltpu.CompilerParams(has_side_effects=True),
        name=name,
    )(*[a for a, _ in items])


def ffn_up(x, sh, sc, w, seq, name, carry=()):
    n = x.shape[0]
    tn = FF // 2
    nj = FF // tn
    ni = n // TM
    tps = seq // TM
    k = len(carry)
    gathers = [g for _, g in carry]
    c_in, c_out, c_shapes, c_scratch = _carry_plan(carry)

    def body(x_ref, sh_ref, sc_ref, w_hbm, *rest):
        cx, (h_ref, g_ref, u_ref, a_ref), co, (w_ref, w_sem, *sems) = rest[:k], rest[k : k + 4], rest[k + 4 : 2 * k + 4], rest[2 * k + 4 :]
        i = pl.program_id(0)

        @pl.when(i == 0)
        def _():
            if k:
                _carry_start(gathers, cx, co, sems)
            cp = pltpu.make_async_copy(w_hbm, w_ref, w_sem)
            cp.start()
            cp.wait()

        h = _modulate(x_ref[...], sh_ref[0], sc_ref[0]).astype(BF16)
        h_ref[...] = h
        for j in range(nj):
            g = jnp.dot(h, w_ref[:, j * tn : (j + 1) * tn], preferred_element_type=F32)
            u = jnp.dot(h, w_ref[:, FF + j * tn : FF + (j + 1) * tn], preferred_element_type=F32)
            g_ref[:, j * tn : (j + 1) * tn] = g.astype(BF16)
            u_ref[:, j * tn : (j + 1) * tn] = u.astype(BF16)
            a_ref[:, j * tn : (j + 1) * tn] = (g * jax.nn.sigmoid(g) * u).astype(BF16)
        if k:
            @pl.when(i == max(ni - 2, 0))
            def _():
                _carry_pass_on(gathers, cx, co, sems)

            @pl.when(i == ni - 1)
            def _():
                _carry_wait(gathers, cx, co, sems)

    vec = pl.BlockSpec((1, 1, D), lambda i: (i // tps, 0, 0))
    col = pl.BlockSpec((TM, FF), lambda i: (i, 0))
    return pl.pallas_call(
        body,
        grid=(ni,),
        in_specs=[pl.BlockSpec((TM, D), lambda i: (i, 0)), vec, vec, pl.BlockSpec(memory_space=pltpu.HBM)] + c_in,
        out_specs=[pl.BlockSpec((TM, D), lambda i: (i, 0)), col, col, col] + c_out,
        out_shape=[_sds((n, D), BF16), _sds((n, FF), BF16), _sds((n, FF), BF16), _sds((n, FF), BF16)] + c_shapes,
        scratch_shapes=[pltpu.VMEM((D, 2 * FF), BF16), pltpu.SemaphoreType.DMA] + c_scratch,
        compiler_params=_params(("arbitrary",)),
        name=name,
    )(x, sh, sc, w, *[a for a, _ in carry])


def mod_mm(x, sh, sc, w, splits, seq, name):
    n = x.shape[0]
    m = w.shape[1]
    tps = seq // TM
    offs = [sum(splits[:k]) for k in range(len(splits))]

    def body(x_ref, sh_ref, sc_ref, w_ref, h_ref, *outs):
        h = _modulate(x_ref[...], sh_ref[0], sc_ref[0]).astype(BF16)
        h_ref[...] = h
        for o_ref, off, wd in zip(outs, offs, splits):
            o_ref[...] = jnp.dot(h, w_ref[:, off : off + wd], preferred_element_type=F32)

    vec = pl.BlockSpec((1, 1, D), lambda i: (i // tps, 0, 0))
    return pl.pallas_call(
        body,
        grid=(n // TM,),
        in_specs=[pl.BlockSpec((TM, D), lambda i: (i, 0)), vec, vec, pl.BlockSpec((D, m), lambda i: (0, 0))],
        out_specs=[pl.BlockSpec((TM, D), lambda i: (i, 0))] + [pl.BlockSpec((TM, wd), lambda i: (i, 0)) for wd in splits],
        out_shape=[_sds((n, D), BF16)] + [_sds((n, wd), F32) for wd in splits],
        compiler_params=_params(("arbitrary",)),
        name=name,
    )(x, sh, sc, w)


def mm_postnorm(parts, w, x, g, lng, lnb, coef, seq, name):
    n = x.shape[0]
    tps = seq // TM
    ks = [p.shape[1] for p in parts]
    offs = [sum(ks[:k]) for k in range(len(ks))]
    npart = len(parts)

    def body(*refs):
        a_refs = refs[:npart]
        w_ref, x_ref, g_ref, lng_ref, lnb_ref, xn_ref, y_ref = refs[npart:]
        for r0 in range(0, TM, SPLIT_ROWS):
            rows = slice(r0, r0 + SPLIT_ROWS)
            y = None
            for a_ref, off, k in zip(a_refs, offs, ks):
                t = jnp.dot(a_ref[rows, :], w_ref[off : off + k, :], preferred_element_type=F32)
                y = t if y is None else y + t
            y_ref[rows, :] = y
            xn_ref[rows, :] = _postnorm(x_ref[rows, :], y, g_ref[0], lng_ref[...], lnb_ref[...], coef=coef)

    row = pl.BlockSpec((TM, D), lambda i: (i, 0))
    one = pl.BlockSpec((1, D), lambda i: (0, 0))
    return pl.pallas_call(
        body,
        grid=(n // TM,),
        in_specs=[pl.BlockSpec((TM, k), lambda i: (i, 0)) for k in ks]
        + [pl.BlockSpec((sum(ks), D), lambda i: (0, 0)), row, pl.BlockSpec((1, 1, D), lambda i: (i // tps, 0, 0)), one, one],
        out_specs=[row, row],
        out_shape=[_sds((n, D), F32), _sds((n, D), F32)],
        compiler_params=_params(("arbitrary",)),
        name=name,
    )(*parts, w, x, g, lng, lnb)


def postnorm_bwd(dxn, x, y, g, lng, lnb, w, ks, coef, seq, name, carry=()):
    n = x.shape[0]
    ni = n // TM
    tps = seq // TM
    nb = n // seq
    offs = [sum(ks[:k]) for k in range(len(ks))]
    npart = len(ks)
    f = functools.partial(_postnorm, coef=coef)
    nc = len(carry)
    gathers = [gt for _, gt in carry]
    c_in, c_out, c_shapes, c_scratch = _carry_plan(carry)

    def body(dxn_ref, x_ref, y_ref, g_ref, lng_ref, lnb_ref, w_ref, *rest):
        cx, rest = rest[:nc], rest[nc:]
        dx_ref, dy_ref = rest[:2]
        da_refs = rest[2 : 2 + npart]
        dg_ref, dlng_ref, dlnb_ref = rest[2 + npart : 5 + npart]
        co, sems = rest[5 + npart : 5 + npart + nc], rest[5 + npart + nc :]
        i = pl.program_id(0)
        if nc:
            @pl.when(i == 0)
            def _():
                _carry_start(gathers, cx, co, sems)

        @pl.when(i % tps == 0)
        def _():
            dg_ref[...] = jnp.zeros_like(dg_ref)

        @pl.when(i == 0)
        def _():
            dlng_ref[...] = jnp.zeros_like(dlng_ref)
            dlnb_ref[...] = jnp.zeros_like(dlnb_ref)

        for r0 in range(0, TM, SPLIT_ROWS):
            rows = slice(r0, r0 + SPLIT_ROWS)
            _, vjp = jax.vjp(f, x_ref[rows, :], y_ref[rows, :], g_ref[0], lng_ref[...], lnb_ref[...])
            dx, dy, dg, dlng, dlnb = vjp(dxn_ref[rows, :])
            dx_ref[rows, :] = dx
            dyb = dy.astype(BF16)
            dy_ref[rows, :] = dyb
            for da_ref, off, k in zip(da_refs, offs, ks):
                da_ref[rows, :] = lax.dot_general(
                    dyb, w_ref[off : off + k, :], (((1,), (1,)), ((), ())), preferred_element_type=F32
                ).astype(BF16)
            dg_ref[0] += dg
            dlng_ref[...] += dlng
            dlnb_ref[...] += dlnb
        if nc:
            @pl.when(i == ni - 1)
            def _():
                _carry_pass_on(gathers, cx, co, sems)
                _carry_wait(gathers, cx, co, sems)

    row = pl.BlockSpec((TM, D), lambda i: (i, 0))
    one = pl.BlockSpec((1, D), lambda i: (0, 0))
    vec = pl.BlockSpec((1, 1, D), lambda i: (i // tps, 0, 0))
    return pl.pallas_call(
        body,
        grid=(ni,),
        in_specs=[row, row, row, vec, one, one, pl.BlockSpec((sum(ks), D), lambda i: (0, 0))] + c_in,
        out_specs=[row, row] + [pl.BlockSpec((TM, k), lambda i: (i, 0)) for k in ks] + [vec, one, one] + c_out,
        out_shape=[_sds((n, D), F32), _sds((n, D), BF16)]
        + [_sds((n, k), BF16) for k in ks]
        + [_sds((nb, 1, D), F32), _sds((1, D), F32), _sds((1, D), F32)]
        + c_shapes,
        scratch_shapes=c_scratch,
        compiler_params=_params(("arbitrary",)),
        name=name,
    )(dxn, x, y, g, lng, lnb, w, *[a for a, _ in carry])


def _mod_bwd_finish(dh, x_ref, sc_ref, dxres_ref, dx_ref, dsh_ref, dsc_ref, first_of_seq):
    dx_ref[...] = dxres_ref[...] + dh * (1.0 + sc_ref[0])

    @pl.when(first_of_seq)
    def _():
        dsh_ref[...] = jnp.zeros_like(dsh_ref)
        dsc_ref[...] = jnp.zeros_like(dsc_ref)

    dsh_ref[0] += jnp.sum(dh, axis=0, keepdims=True)
    dsc_ref[0] += jnp.sum(dh * x_ref[...], axis=0, keepdims=True)


def ffn_bwd_in(da, g, u, w, x, sc, dxres, seq, name, carry=()):
    n = x.shape[0]
    tn = FF // 2
    nj = FF // tn
    tm = TM // 2
    ni = n // tm
    tps = seq // tm
    nb = n // seq
    k = len(carry)
    gathers = [gt for _, gt in carry]
    c_in, c_out, c_shapes, c_scratch = _carry_plan(carry)

    def body(da_ref, g_ref, u_ref, w_hbm, x_ref, sc_ref, dxres_ref, *rest):
        cx, (dg_ref, du_ref, dx_ref, dsh_ref, dsc_ref), co, (w_ref, w_sem, *sems) = rest[:k], rest[k : k + 5], rest[k + 5 : 2 * k + 5], rest[2 * k + 5 :]
        i = pl.program_id(0)

        @pl.when(i == 0)
        def _():
            if k:
                _carry_start(gathers, cx, co, sems)
            cp = pltpu.make_async_copy(w_hbm, w_ref, w_sem)
            cp.start()
            cp.wait()

        nt = (((1,), (1,)), ((), ()))
        dh = None
        for j in range(nj):
            ln = slice(j * tn, (j + 1) * tn)
            gv = g_ref[:, ln].astype(F32)
            uv = u_ref[:, ln].astype(F32)
            dav = da_ref[:, ln].astype(F32)
            s = jax.nn.sigmoid(gv)
            dgv = (dav * uv * s * (1.0 + gv * (1.0 - s))).astype(BF16)
            duv = (dav * gv * s).astype(BF16)
            dg_ref[:, ln] = dgv
            du_ref[:, ln] = duv
            t = lax.dot_general(dgv, w_ref[:, j * tn : (j + 1) * tn], nt, preferred_element_type=F32) + lax.dot_general(
                duv, w_ref[:, FF + j * tn : FF + (j + 1) * tn], nt, preferred_element_type=F32
            )
            dh = t if dh is None else dh + t
        _mod_bwd_finish(dh, x_ref, sc_ref, dxres_ref, dx_ref, dsh_ref, dsc_ref, i % tps == 0)
        if k:
            @pl.when(i == ni - 1)
            def _():
                _carry_wait(gathers, cx, co, sems)

    row = pl.BlockSpec((tm, D), lambda i: (i, 0))
    col = pl.BlockSpec((tm, FF), lambda i: (i, 0))
    vec = pl.BlockSpec((1, 1, D), lambda i: (i // tps, 0, 0))
    return pl.pallas_call(
        body,
        grid=(ni,),
        in_specs=[col, col, col, pl.BlockSpec(memory_space=pltpu.HBM), row, vec, row] + c_in,
        out_specs=[col, col, row, vec, vec] + c_out,
        out_shape=[_sds((n, FF), BF16), _sds((n, FF), BF16), _sds((n, D), F32), _sds((nb, 1, D), F32), _sds((nb, 1, D), F32)] + c_shapes,
        scratch_shapes=[pltpu.VMEM((D, 2 * FF), BF16), pltpu.SemaphoreType.DMA] + c_scratch,
        compiler_params=_params(("arbitrary",)),
        name=name,
    )(da, g, u, w, x, sc, dxres, *[a for a, _ in carry])


def proj_bwd_in(dparts, w, x, sc, dxres, seq, name):
    n = x.shape[0]
    tps = seq // TM
    nb = n // seq
    ms = [p.shape[1] for p in dparts]
    offs = [sum(ms[:k]) for k in range(len(ms))]
    npart = len(ms)

    def body(*refs):
        d_refs = refs[:npart]
        w_ref, x_ref, sc_ref, dxres_ref, dx_ref, dsh_ref, dsc_ref = refs[npart:]
        dh = None
        for d_ref, off, m in zip(d_refs, offs, ms):
            t = lax.dot_general(d_ref[...], w_ref[:, off : off + m], (((1,), (1,)), ((), ())), preferred_element_type=F32)
            dh = t if dh is None else dh + t
        _mod_bwd_finish(dh, x_ref, sc_ref, dxres_ref, dx_ref, dsh_ref, dsc_ref, pl.program_id(0) % tps == 0)

    row = pl.BlockSpec((TM, D), lambda i: (i, 0))
    vec = pl.BlockSpec((1, 1, D), lambda i: (i // tps, 0, 0))
    return pl.pallas_call(
        body,
        grid=(n // TM,),
        in_specs=[pl.BlockSpec((TM, m), lambda i: (i, 0)) for m in ms] + [pl.BlockSpec((D, sum(ms)), lambda i: (0, 0)), row, vec, row],
        out_specs=[row, vec, vec],
        out_shape=[_sds((n, D), F32), _sds((nb, 1, D), F32), _sds((nb, 1, D), F32)],
        compiler_params=_params(("arbitrary",)),
        name=name,
    )(*dparts, w, x, sc, dxres)


def mm_tn(a, b, name):
    n, k1 = a.shape
    k2 = b.shape[1]
    t1 = k1 if k1 <= 1536 else _tile(k1, 1536)
    t2 = k2 if k2 <= 1536 else _tile(k2, 1536)
    tk = 1024 if n % 1024 == 0 else n
    nk = n // tk

    def body(a_ref, b_ref, o_ref):
        t = lax.dot_general(a_ref[...], b_ref[...], (((0,), (0,)), ((), ())), preferred_element_type=F32)

        @pl.when(pl.program_id(2) == 0)
        def _():
            o_ref[...] = t

        @pl.when(pl.program_id(2) > 0)
        def _():
            o_ref[...] += t

    return pl.pallas_call(
        body,
        grid=(k1 // t1, k2 // t2, nk),
        in_specs=[pl.BlockSpec((tk, t1), lambda i, j, k: (k, i)), pl.BlockSpec((tk, t2), lambda i, j, k: (k, j))],
        out_specs=pl.BlockSpec((t1, t2), lambda i, j, k: (i, j)),
        out_shape=_sds((k1, k2), F32),
        compiler_params=_params(("arbitrary", "arbitrary", "arbitrary")),
        name=name,
    )(a, b)


def _tile(n, cap):
    best = LANE
    for t in range(LANE, cap + 1, LANE):
        if n % t == 0:
            best = t
    return best


def _lane_tiles(ref):
    return [slice(c0, c0 + LANE) for c0 in range(0, ref.shape[1], LANE)]


def _conv_taps(ext_ref, w_ref, halo, tq, kk):
    out = []
    for ln in _lane_tiles(w_ref):
        acc = None
        for k in range(kk):
            t = w_ref[k : k + 1, ln] * ext_ref[pl.ds(halo - (kk - 1 - k), tq), ln]
            acc = t if acc is None else acc + t
        out.append(acc)
    return jnp.concatenate(out, axis=1)


def _conv_taps_t(ext2_ref, w_ref, tq, kk):
    out = []
    for ln in _lane_tiles(w_ref):
        acc = None
        for k in range(kk):
            t = w_ref[k : k + 1, ln] * ext2_ref[pl.ds(kk - 1 - k, tq), ln]
            acc = t if acc is None else acc + t
        out.append(acc)
    return jnp.concatenate(out, axis=1)


def _conv_dw(ext_ref, ext2_ref, dw_ref, halo, tq, kk):
    for ln in _lane_tiles(dw_ref):
        dy = ext2_ref[pl.ds(0, tq), ln]
        for k in range(kk):
            dw_ref[k : k + 1, ln] += jnp.sum(dy * ext_ref[pl.ds(halo - (kk - 1 - k), tq), ln], axis=0, keepdims=True)


def _halo_spec(rows, width, tq, shift):
    per = tq // rows

    if shift < 0:
        return lambda nblocks: pl.BlockSpec((rows, width), lambda i: (jnp.maximum(i * per - 1, 0), 0))
    return lambda nblocks: pl.BlockSpec((rows, width), lambda i: (jnp.minimum((i + 1) * per, nblocks - 1), 0))


def _ln(c, g, b):
    mu = jnp.mean(c, axis=-1, keepdims=True)
    cc = c - mu
    var = jnp.mean(cc * cc, axis=-1, keepdims=True)
    return cc * lax.rsqrt(var + EPS) * g + b


def _silu(v):
    return v * jax.nn.sigmoid(v)


def _ssd_chunk(conv, dtr, z, s, dt_bias, a_log, dskip, norm_g, tril, e):
    q = SSD_Q
    act = _silu(conv)
    xs, bm, cm = act[:, :1024], act[:, 1024:1280], act[:, 1280:1536]
    lane = lax.broadcasted_iota(jnp.int32, (1, LANE), 1)
    lane_q = lax.broadcasted_iota(jnp.int32, (q, LANE), 1)
    sub_q = lax.broadcasted_iota(jnp.int32, (LANE, q), 0)
    causal = lax.broadcasted_iota(jnp.int32, (q, q), 0) >= lax.broadcasted_iota(jnp.int32, (q, q), 1)
    real = lane < SSD_HEADS
    dt = jnp.where(real, jax.nn.softplus(dtr + dt_bias), 0.0)
    a = jnp.where(real, -jnp.exp(a_log), 0.0)
    da = dt * a
    acs = jnp.dot(tril, da, precision=HI, preferred_element_type=F32)
    acs_t = lax.dot_general(da, tril, (((0,), (1,)), ((), ())), precision=HI, preferred_element_type=F32)
    dt_e = jnp.dot(dt, e, precision=HI, preferred_element_type=F32)
    acs_e = jnp.dot(acs, e, precision=HI, preferred_element_type=F32)
    one8 = jnp.ones((SUB, 1), F32)
    alast_e = jnp.dot(one8 * jnp.sum(da, axis=0, keepdims=True), e, precision=HI, preferred_element_type=F32)[0:1]
    d_e = jnp.dot(one8 * jnp.where(real, dskip, 0.0), e, precision=HI, preferred_element_type=F32)[0:1]
    xdt = xs * dt_e
    nt = (((1,), (1,)), ((), ()))
    tn = (((0,), (0,)), ((), ()))
    ys, snews = [], []
    for g in range(2):
        gl = slice(g * 512, (g + 1) * 512)
        bg = bm[:, g * 128 : (g + 1) * 128].astype(BF16)
        cg = cm[:, g * 128 : (g + 1) * 128].astype(BF16)
        cb = lax.dot_general(cg, bg, nt, preferred_element_type=F32)
        sg = s[:, gl]
        yoff = jnp.dot(cg, sg.astype(BF16), preferred_element_type=F32) * jnp.exp(acs_e[:, gl])
        pairs = []
        for j in range(4):
            xp = xdt[:, g * 512 + j * 128 : g * 512 + (j + 1) * 128].astype(BF16)
            outs = []
            for hh in (g * 8 + 2 * j, g * 8 + 2 * j + 1):
                col = jnp.sum(jnp.where(lane_q == hh, acs, 0.0), axis=1, keepdims=True)
                row = jnp.sum(jnp.where(sub_q == hh, acs_t, 0.0), axis=0, keepdims=True)
                m = cb * jnp.exp(jnp.where(causal, col - row, -1e30))
                outs.append(jnp.dot(m.astype(BF16), xp, preferred_element_type=F32))
            pairs.append(jnp.where(lane_q < SSD_P, outs[0], outs[1]))
        ys.append(jnp.concatenate(pairs, axis=1) + yoff)
        decay = jnp.exp(alast_e[:, gl] - acs_e[:, gl])
        snews.append(
            sg * jnp.exp(alast_e[:, gl]) + lax.dot_general(bg, (xdt[:, gl] * decay).astype(BF16), tn, preferred_element_type=F32)
        )
    y = jnp.concatenate(ys, axis=1) + xs * d_e
    gated = y * _silu(z)
    out = gated * lax.rsqrt(jnp.mean(gated * gated, axis=-1, keepdims=True) + EPS) * norm_g
    return out, jnp.concatenate(snews, axis=1)


def _ssd_consts():
    tril = (lax.broadcasted_iota(jnp.int32, (SSD_Q, SSD_Q), 0) >= lax.broadcasted_iota(jnp.int32, (SSD_Q, SSD_Q), 1)).astype(F32)
    e = (lax.broadcasted_iota(jnp.int32, (LANE, 1024), 0) == lax.broadcasted_iota(jnp.int32, (LANE, 1024), 1) // SSD_P).astype(F32)
    return tril, e


def ssd_fwd(z, xbc, dtr, cw, cb, dt_bias, a_log, dskip, norm_g, seq, name):
    n = z.shape[0]
    q = SSD_Q
    nc = seq // q
    tril, e = _ssd_consts()

    def body(z_ref, xbc_ref, halo_ref, dtr_ref, cw_ref, cb_ref, dtb_ref, alog_ref, dsk_ref, ng_ref, tril_ref, e_ref, y_ref, sprev_ref, s_scr, ext):
        c = pl.program_id(0) % nc

        @pl.when(c == 0)
        def _():
            s_scr[...] = jnp.zeros_like(s_scr)

        ext[0:SUB, :] = jnp.where(c == 0, 0.0, halo_ref[...])
        ext[SUB:, :] = xbc_ref[...]
        conv = _conv_taps(ext, cw_ref, SUB, q, 4) + cb_ref[...]
        sprev_ref[0] = s_scr[...]
        y, snew = _ssd_chunk(conv, dtr_ref[...], z_ref[...], s_scr[...], dtb_ref[...], alog_ref[...], dsk_ref[...], ng_ref[...], tril_ref[...], e_ref[...])
        y_ref[...] = y.astype(BF16)
        s_scr[...] = snew

    def full(shape):
        return pl.BlockSpec(shape, lambda i: (0,) * len(shape))

    return pl.pallas_call(
        body,
        grid=(n // q,),
        in_specs=[
            pl.BlockSpec((q, 1024), lambda i: (i, 0)),
            pl.BlockSpec((q, SSD_XBC), lambda i: (i, 0)),
            _halo_spec(SUB, SSD_XBC, q, -1)(n // SUB),
            pl.BlockSpec((q, LANE), lambda i: (i, 0)),
            full((4, SSD_XBC)),
            full((1, SSD_XBC)),
            full((1, LANE)),
            full((1, LANE)),
            full((1, LANE)),
            full((1, 1024)),
            full((q, q)),
            full((LANE, 1024)),
        ],
        out_specs=[pl.BlockSpec((q, 1024), lambda i: (i, 0)), pl.BlockSpec((1, LANE, 1024), lambda i: (i, 0, 0))],
        out_shape=[_sds((n, 1024), BF16), _sds((n // q, LANE, 1024), F32)],
        scratch_shapes=[pltpu.VMEM((LANE, 1024), F32), pltpu.VMEM((SUB + q, SSD_XBC), F32)],
        compiler_params=_params(("arbitrary",)),
        name=name,
    )(z, xbc, xbc, dtr, cw, cb, dt_bias, a_log, dskip, norm_g, tril, e)


def ssd_bwd(dy, z, xbc, dtr, sprev, cw, cb, dt_bias, a_log, dskip, norm_g, seq, name):
    n = z.shape[0]
    q = SSD_Q
    nc = seq // q
    nchunks = n // q
    tril, e = _ssd_consts()

    def rev(i):
        return (i // nc) * nc + (nc - 1 - i % nc)

    def body(dy_ref, z_ref, xbc_ref, halo_ref, dtr_ref, sprev_ref, cw_ref, cb_ref, dtb_ref, alog_ref, dsk_ref, ng_ref, tril_ref, e_ref,
             dz_ref, dxbc_ref, ddt_ref, dcw_ref, dcb_ref, ddtb_ref, dalog_ref, ddsk_ref, dng_ref, ds_scr, ext, ext2):
        i = pl.program_id(0)
        step = i % nc
        c = nc - 1 - step

        @pl.when(step == 0)
        def _():
            ds_scr[...] = jnp.zeros_like(ds_scr)
            ext2[q:, :] = jnp.zeros((SUB, SSD_XBC), F32)

        @pl.when(i == 0)
        def _():
            for r in (dcw_ref, dcb_ref, ddtb_ref, dalog_ref, ddsk_ref, dng_ref):
                r[...] = jnp.zeros_like(r)

        ext[0:SUB, :] = jnp.where(c == 0, 0.0, halo_ref[...])
        ext[SUB:, :] = xbc_ref[...]
        conv = _conv_taps(ext, cw_ref, SUB, q, 4) + cb_ref[...]
        tril_v, e_v = tril_ref[...], e_ref[...]

        def f(conv, dtr, z, s, dtb, alog, dsk, ng):
            return _ssd_chunk(conv, dtr, z, s, dtb, alog, dsk, ng, tril_v, e_v)

        _, vjp = jax.vjp(f, conv, dtr_ref[...], z_ref[...], sprev_ref[0], dtb_ref[...], alog_ref[...], dsk_ref[...], ng_ref[...])
        dconv, ddtr, dz, dsprev, ddtb, dalog, ddsk, dng = vjp((dy_ref[...].astype(F32), ds_scr[...]))
        ds_scr[...] = dsprev
        dz_ref[...] = dz.astype(BF16)
        ddt_ref[...] = ddtr.astype(BF16)
        ext2[0:q, :] = dconv
        dxbc_ref[...] = _conv_taps_t(ext2, cw_ref, q, 4).astype(BF16)
        ext2[q:, :] = dconv[0:SUB, :]
        _conv_dw(ext, ext2, dcw_ref, SUB, q, 4)
        dcb_ref[...] += jnp.sum(dconv, axis=0, keepdims=True)
        ddtb_ref[...] += ddtb
        dalog_ref[...] += dalog
        ddsk_ref[...] += ddsk
        dng_ref[...] += dng

    def full(shape):
        return pl.BlockSpec(shape, lambda i: (0,) * len(shape))

    per = q // SUB
    return pl.pallas_call(
        body,
        grid=(nchunks,),
        in_specs=[
            pl.BlockSpec((q, 1024), lambda i: (rev(i), 0)),
            pl.BlockSpec((q, 1024), lambda i: (rev(i), 0)),
            pl.BlockSpec((q, SSD_XBC), lambda i: (rev(i), 0)),
            pl.BlockSpec((SUB, SSD_XBC), lambda i: (jnp.maximum(rev(i) * per - 1, 0), 0)),
            pl.BlockSpec((q, LANE), lambda i: (rev(i), 0)),
            pl.BlockSpec((1, LANE, 1024), lambda i: (rev(i), 0, 0)),
            full((4, SSD_XBC)),
            full((1, SSD_XBC)),
            full((1, LANE)),
            full((1, LANE)),
            full((1, LANE)),
            full((1, 1024)),
            full((q, q)),
            full((LANE, 1024)),
        ],
        out_specs=[
            pl.BlockSpec((q, 1024), lambda i: (rev(i), 0)),
            pl.BlockSpec((q, SSD_XBC), lambda i: (rev(i), 0)),
            pl.BlockSpec((q, LANE), lambda i: (rev(i), 0)),
            full((4, SSD_XBC)),
            full((1, SSD_XBC)),
            full((1, LANE)),
            full((1, LANE)),
            full((1, LANE)),
            full((1, 1024)),
        ],
        out_shape=[
            _sds((n, 1024), BF16),
            _sds((n, SSD_XBC), BF16),
            _sds((n, LANE), BF16),
            _sds((4, SSD_XBC), F32),
            _sds((1, SSD_XBC), F32),
            _sds((1, LANE), F32),
            _sds((1, LANE), F32),
            _sds((1, LANE), F32),
            _sds((1, 1024), F32),
        ],
        scratch_shapes=[pltpu.VMEM((LANE, 1024), F32), pltpu.VMEM((SUB + q, SSD_XBC), F32), pltpu.VMEM((q + SUB, SSD_XBC), F32)],
        compiler_params=_params(("arbitrary",)),
        name=name,
    )(dy, z, xbc, xbc, dtr, sprev, cw, cb, dt_bias, a_log, dskip, norm_g, tril, e)


POOL_HALO = 16
TQ = 512


def _pool_count(pos, w):
    return jnp.minimum(pos + 1.0, float(w))


def pool_fwd(u, pw, scale, seq, name):
    n = u.shape[0]
    tq, halo = TQ, POOL_HALO
    tps = seq // tq

    def body(u_ref, halo_ref, pw_ref, sc_ref, y_ref, ext):
        t0 = pl.program_id(0) % tps
        ext[0:halo, :] = jnp.where(t0 == 0, 0.0, halo_ref[...])
        ext[halo:, :] = u_ref[...]
        pos = (t0 * tq + lax.broadcasted_iota(jnp.int32, (tq, 1), 0)).astype(F32)
        for g, w in enumerate(POOL_WINDOWS):
            ln = slice(g * LANE, (g + 1) * LANE)
            acc = ext[pl.ds(halo, tq), ln]
            for j in range(1, w):
                acc = acc + ext[pl.ds(halo - j, tq), ln]
            pooled = acc / _pool_count(pos, w) - u_ref[:, ln]
            mixed = jnp.dot(pooled.astype(BF16), pw_ref[g].astype(BF16), preferred_element_type=F32)
            y_ref[:, ln] = (mixed * sc_ref[:, ln]).astype(BF16)

    return pl.pallas_call(
        body,
        grid=(n // tq,),
        in_specs=[
            pl.BlockSpec((tq, POOL_DIM), lambda i: (i, 0)),
            _halo_spec(halo, POOL_DIM, tq, -1)(n // halo),
            pl.BlockSpec((4, LANE, LANE), lambda i: (0, 0, 0)),
            pl.BlockSpec((1, POOL_DIM), lambda i: (0, 0)),
        ],
        out_specs=pl.BlockSpec((tq, POOL_DIM), lambda i: (i, 0)),
        out_shape=_sds((n, POOL_DIM), BF16),
        scratch_shapes=[pltpu.VMEM((halo + tq, POOL_DIM), F32)],
        compiler_params=_params(("arbitrary",)),
        name=name,
    )(u, u, pw, scale)


def pool_bwd(dy, u, pw, scale, seq, name):
    n = u.shape[0]
    tq, halo = TQ, POOL_HALO
    tps = seq // tq
    nt = (((1,), (1,)), ((), ()))
    tn = (((0,), (0,)), ((), ()))

    def body(dy_ref, dyn_ref, u_ref, halo_ref, pw_ref, sc_ref, du_ref, dpw_ref, dsc_ref, ext, ext2):
        i = pl.program_id(0)
        t0 = i % tps

        @pl.when(i == 0)
        def _():
            dpw_ref[...] = jnp.zeros_like(dpw_ref)
            dsc_ref[...] = jnp.zeros_like(dsc_ref)

        ext[0:halo, :] = jnp.where(t0 == 0, 0.0, halo_ref[...])
        ext[halo:, :] = u_ref[...]
        pos = (t0 * tq + lax.broadcasted_iota(jnp.int32, (tq, 1), 0)).astype(F32)
        dyv = dy_ref[...].astype(F32)
        dynv = jnp.where(t0 == tps - 1, 0.0, dyn_ref[...].astype(F32))
        for g, w in enumerate(POOL_WINDOWS):
            ln = slice(g * LANE, (g + 1) * LANE)
            wg = pw_ref[g].astype(BF16)
            acc = ext[pl.ds(halo, tq), ln]
            for j in range(1, w):
                acc = acc + ext[pl.ds(halo - j, tq), ln]
            pooled = (acc / _pool_count(pos, w) - u_ref[:, ln]).astype(BF16)
            mixed = jnp.dot(pooled, wg, preferred_element_type=F32)
            dsc_ref[:, ln] += jnp.sum(dyv[:, ln] * mixed, axis=0, keepdims=True)
            dmix = (dyv[:, ln] * sc_ref[:, ln]).astype(BF16)
            dpw_ref[g] += lax.dot_general(pooled, dmix, tn, preferred_element_type=F32)
            dpool = lax.dot_general(dmix, wg, nt, preferred_element_type=F32)
            dmix_n = (dynv[:, ln] * sc_ref[:, ln]).astype(BF16)
            dpool_n = lax.dot_general(dmix_n, wg, nt, preferred_element_type=F32)
            ext2[0:tq, ln] = dpool / _pool_count(pos, w)
            ext2[tq:, ln] = dpool_n * (1.0 / w)
            acc2 = ext2[pl.ds(0, tq), ln]
            for j in range(1, w):
                acc2 = acc2 + ext2[pl.ds(j, tq), ln]
            du_ref[:, ln] = (acc2 - dpool).astype(BF16)

    return pl.pallas_call(
        body,
        grid=(n // tq,),
        in_specs=[
            pl.BlockSpec((tq, POOL_DIM), lambda i: (i, 0)),
            _halo_spec(halo, POOL_DIM, tq, +1)(n // halo),
            pl.BlockSpec((tq, POOL_DIM), lambda i: (i, 0)),
            _halo_spec(halo, POOL_DIM, tq, -1)(n // halo),
            pl.BlockSpec((4, LANE, LANE), lambda i: (0, 0, 0)),
            pl.BlockSpec((1, POOL_DIM), lambda i: (0, 0)),
        ],
        out_specs=[
            pl.BlockSpec((tq, POOL_DIM), lambda i: (i, 0)),
            pl.BlockSpec((4, LANE, LANE), lambda i: (0, 0, 0)),
            pl.BlockSpec((1, POOL_DIM), lambda i: (0, 0)),
        ],
        out_shape=[_sds((n, POOL_DIM), BF16), _sds((4, LANE, LANE), F32), _sds((1, POOL_DIM), F32)],
        scratch_shapes=[pltpu.VMEM((halo + tq, POOL_DIM), F32), pltpu.VMEM((tq + halo, POOL_DIM), F32)],
        compiler_params=_params(("arbitrary",)),
        name=name,
    )(dy, dy, u, u, pw, scale)


CONF_HALO = 32
TQC = 256


def _conf_post(c, g, b):
    return _silu(_ln(c, g, b))


def conf_fwd(vg, w, b, lng, lnb, seq, name):
    n = vg.shape[0]
    tq, halo, kk = TQC, CONF_HALO, CONF_K
    tps = seq // tq
    c = CONF_DIM

    def body(vg_ref, halo_ref, w_ref, b_ref, lng_ref, lnb_ref, y_ref, conv_ref, ext):
        t0 = pl.program_id(0) % tps
        hv = halo_ref[...]
        ext[0:halo, :] = jnp.where(t0 == 0, 0.0, hv[:, :c] * jax.nn.sigmoid(hv[:, c:]))
        ext[halo:, :] = vg_ref[:, :c] * jax.nn.sigmoid(vg_ref[:, c:])
        conv = _conv_taps(ext, w_ref, halo, tq, kk) + b_ref[...]
        conv_ref[...] = conv
        y_ref[...] = _conf_post(conv, lng_ref[...], lnb_ref[...]).astype(BF16)

    one = pl.BlockSpec((1, c), lambda i: (0, 0))
    return pl.pallas_call(
        body,
        grid=(n // tq,),
        in_specs=[pl.BlockSpec((tq, 2 * c), lambda i: (i, 0)), _halo_spec(halo, 2 * c, tq, -1)(n // halo), pl.BlockSpec((kk, c), lambda i: (0, 0)), one, one, one],
        out_specs=[pl.BlockSpec((tq, c), lambda i: (i, 0)), pl.BlockSpec((tq, c), lambda i: (i, 0))],
        out_shape=[_sds((n, c), BF16), _sds((n, c), F32)],
        scratch_shapes=[pltpu.VMEM((halo + tq, c), F32)],
        compiler_params=_params(("arbitrary",)),
        name=name,
    )(vg, vg, w, b, lng, lnb)


def conf_bwd(dy, conv, vg, w, lng, lnb, seq, name):
    n = vg.shape[0]
    tq, halo, kk = TQC, CONF_HALO, CONF_K
    tps = seq // tq
    c = CONF_DIM

    def body(dy_ref, dyn_ref, conv_ref, convn_ref, vg_ref, halo_ref, w_ref, lng_ref, lnb_ref, dvg_ref, dw_ref, db_ref, dlng_ref, dlnb_ref, ext, ext2):
        i = pl.program_id(0)
        t0 = i % tps

        @pl.when(i == 0)
        def _():
            for r in (dw_ref, db_ref, dlng_ref, dlnb_ref):
                r[...] = jnp.zeros_like(r)

        _, vjp = jax.vjp(_conf_post, conv_ref[...], lng_ref[...], lnb_ref[...])
        dconv, dlng, dlnb = vjp(dy_ref[...].astype(F32))
        _, vjpn = jax.vjp(_conf_post, convn_ref[...], lng_ref[...], lnb_ref[...])
        dconv_n = vjpn(dyn_ref[...].astype(F32))[0]
        ext2[0:tq, :] = dconv
        ext2[tq:, :] = jnp.where(t0 == tps - 1, 0.0, dconv_n)
        dh = _conv_taps_t(ext2, w_ref, tq, kk)
        hv = halo_ref[...]
        ext[0:halo, :] = jnp.where(t0 == 0, 0.0, hv[:, :c] * jax.nn.sigmoid(hv[:, c:]))
        v = vg_ref[:, :c]
        s = jax.nn.sigmoid(vg_ref[:, c:])
        ext[halo:, :] = v * s
        _conv_dw(ext, ext2, dw_ref, halo, tq, kk)
        db_ref[...] += jnp.sum(dconv, axis=0, keepdims=True)
        dlng_ref[...] += dlng
        dlnb_ref[...] += dlnb
        dvg_ref[:, :c] = (dh * s).astype(BF16)
        dvg_ref[:, c:] = (dh * v * s * (1.0 - s)).astype(BF16)

    one = pl.BlockSpec((1, c), lambda i: (0, 0))
    tile = pl.BlockSpec((tq, c), lambda i: (i, 0))
    nxt = _halo_spec(halo, c, tq, +1)(n // halo)
    return pl.pallas_call(
        body,
        grid=(n // tq,),
        in_specs=[tile, nxt, tile, nxt, pl.BlockSpec((tq, 2 * c), lambda i: (i, 0)), _halo_spec(halo, 2 * c, tq, -1)(n // halo),
                  pl.BlockSpec((kk, c), lambda i: (0, 0)), one, one],
        out_specs=[pl.BlockSpec((tq, 2 * c), lambda i: (i, 0)), pl.BlockSpec((kk, c), lambda i: (0, 0)), one, one, one],
        out_shape=[_sds((n, 2 * c), BF16), _sds((kk, c), F32), _sds((1, c), F32), _sds((1, c), F32), _sds((1, c), F32)],
        scratch_shapes=[pltpu.VMEM((halo + tq, c), F32), pltpu.VMEM((tq + halo, c), F32)],
        compiler_params=_params(("arbitrary",)),
        name=name,
    )(dy, dy, conv, conv, vg, vg, w, lng, lnb)


TL = 256


def _expm1_neg(t):
    p = t * (1.0 + t * (1.0 / 2 + t * (1.0 / 6 + t * (1.0 / 24 + t * (1.0 / 120 + t * (1.0 / 720 + t * (1.0 / 5040)))))))
    return jnp.where(t > -0.35, p, jnp.exp(t) - 1.0)


def _lru_gate(xc, ra, ia, ba, bx, lam):
    r = jax.nn.sigmoid(ra + ba)
    i = jax.nn.sigmoid(ia + bx)
    log_a = -LRU_C * r * jax.nn.softplus(-lam)
    return jnp.exp(log_a), jnp.sqrt(-_expm1_neg(2.0 * log_a)) * (i * xc)


def _lru_out(h, gr):
    return h * jax.nn.gelu(gr)


def _scan_rows(a, b, tq, reverse):
    row = lax.broadcasted_iota(jnp.int32, (tq, 1), 0)
    d = 1
    while d < tq:
        sh = tq - d if reverse else d
        valid = (row < tq - d) if reverse else (row >= d)
        a_s = pltpu.roll(a, sh, 0)
        b_s = pltpu.roll(b, sh, 0)
        b = jnp.where(valid, a * b_s, 0.0) + b
        a = jnp.where(valid, a * a_s, a)
        d *= 2
    return a, b


def _row_of(v, r, tq):
    row = lax.broadcasted_iota(jnp.int32, (tq, 1), 0)
    return jnp.sum(jnp.where(row == r, v, 0.0), axis=0, keepdims=True)


def _head_mm(xc, w_ref):
    return jnp.concatenate(
        [
            jnp.dot(xc[:, h * LANE : (h + 1) * LANE].astype(BF16), w_ref[h].astype(BF16), preferred_element_type=F32)
            for h in range(LRU_HEADS)
        ],
        axis=1,
    )


def lru_fwd(xr, gr, cw, cb, wa, ba, wx, bx, lam, seq, name):
    n = xr.shape[0]
    tq = TL
    tps = seq // tq
    c = LRU_DIM

    def body(xr_ref, halo_ref, gr_ref, cw_ref, cb_ref, wa_ref, ba_ref, wx_ref, bx_ref, lam_ref, y_ref, h_ref, hc, ext):
        t0 = pl.program_id(0) % tps

        @pl.when(t0 == 0)
        def _():
            hc[...] = jnp.zeros_like(hc)

        ext[0:SUB, :] = jnp.where(t0 == 0, 0.0, halo_ref[...])
        ext[SUB:, :] = xr_ref[...]
        xc = _conv_taps(ext, cw_ref, SUB, tq, 4) + cb_ref[...]
        a, b = _lru_gate(xc, _head_mm(xc, wa_ref), _head_mm(xc, wx_ref), ba_ref[...], bx_ref[...], lam_ref[...])
        acum, h0 = _scan_rows(a, b, tq, False)
        h = h0 + acum * hc[0:1, :]
        h_ref[...] = h
        hc[0:1, :] = h_ref[tq - 1 : tq, :]
        y_ref[...] = _lru_out(h, gr_ref[...]).astype(BF16)

    one = pl.BlockSpec((1, c), lambda i: (0, 0))
    tile = pl.BlockSpec((tq, c), lambda i: (i, 0))
    hw = pl.BlockSpec((LRU_HEADS, LANE, LANE), lambda i: (0, 0, 0))
    return pl.pallas_call(
        body,
        grid=(n // tq,),
        in_specs=[tile, _halo_spec(SUB, c, tq, -1)(n // SUB), tile, pl.BlockSpec((4, c), lambda i: (0, 0)), one, hw, one, hw, one, one],
        out_specs=[tile, tile],
        out_shape=[_sds((n, c), BF16), _sds((n, c), F32)],
        scratch_shapes=[pltpu.VMEM((SUB, c), F32), pltpu.VMEM((SUB + tq, c), F32)],
        compiler_params=_params(("arbitrary",)),
        name=name,
    )(xr, xr, gr, cw, cb, wa, ba, wx, bx, lam)


def lru_bwd(dy, xr, gr, h, cw, cb, wa, ba, wx, bx, lam, seq, name):
    n = xr.shape[0]
    tq = TL
    tps = seq // tq
    ntile = n // tq
    c = LRU_DIM
    per = tq // SUB
    nt = (((1,), (1,)), ((), ()))
    tn = (((0,), (0,)), ((), ()))

    def rev(i):
        return (i // tps) * tps + (tps - 1 - i % tps)

    def body(dy_ref, xr_ref, halo_ref, gr_ref, h_ref, hprev_ref, cw_ref, cb_ref, wa_ref, ba_ref, wx_ref, bx_ref, lam_ref,
             dxr_ref, dgr_ref, dcw_ref, dcb_ref, dwa_ref, dba_ref, dwx_ref, dbx_ref, dlam_ref, carry, ext, ext2):
        i = pl.program_id(0)
        step = i % tps
        t0 = tps - 1 - step

        @pl.when(step == 0)
        def _():
            carry[...] = jnp.zeros_like(carry)
            ext2[tq:, :] = jnp.zeros((SUB, c), F32)

        @pl.when(i == 0)
        def _():
            for r in (dcw_ref, dcb_ref, dwa_ref, dba_ref, dwx_ref, dbx_ref, dlam_ref):
                r[...] = jnp.zeros_like(r)

        ext[0:SUB, :] = jnp.where(t0 == 0, 0.0, halo_ref[...])
        ext[SUB:, :] = xr_ref[...]
        xc = _conv_taps(ext, cw_ref, SUB, tq, 4) + cb_ref[...]
        (a, _), vjp_gate = jax.vjp(_lru_gate, xc, _head_mm(xc, wa_ref), _head_mm(xc, wx_ref), ba_ref[...], bx_ref[...], lam_ref[...])
        hv = h_ref[...]
        _, vjp_out = jax.vjp(_lru_out, hv, gr_ref[...])
        dh, dgr = vjp_out(dy_ref[...].astype(F32))
        dgr_ref[...] = dgr.astype(BF16)
        row = lax.broadcasted_iota(jnp.int32, (tq, 1), 0)
        a_up = jnp.where(row == tq - 1, carry[0:1, :], pltpu.roll(a, tq - 1, 0))
        acum, l0 = _scan_rows(a_up, dh, tq, True)
        lamv = l0 + acum * carry[1:2, :]
        carry[0:1, :] = _row_of(a, 0, tq)
        carry[1:2, :] = _row_of(lamv, 0, tq)
        hprev = jnp.where(row == 0, jnp.where(t0 == 0, 0.0, hprev_ref[SUB - 1 : SUB, :]), pltpu.roll(hv, 1, 0))
        dxc, dra, dia, dba, dbx, dlam = vjp_gate((lamv * hprev, lamv))
        dba_ref[...] += dba
        dbx_ref[...] += dbx
        dlam_ref[...] += dlam
        pieces = []
        for hh in range(LRU_HEADS):
            ln = slice(hh * LANE, (hh + 1) * LANE)
            xh = xc[:, ln].astype(BF16)
            drh = dra[:, ln].astype(BF16)
            dih = dia[:, ln].astype(BF16)
            dwa_ref[hh] += lax.dot_general(xh, drh, tn, preferred_element_type=F32)
            dwx_ref[hh] += lax.dot_general(xh, dih, tn, preferred_element_type=F32)
            pieces.append(
                lax.dot_general(drh, wa_ref[hh].astype(BF16), nt, preferred_element_type=F32)
                + lax.dot_general(dih, wx_ref[hh].astype(BF16), nt, preferred_element_type=F32)
            )
        dxc = dxc + jnp.concatenate(pieces, axis=1)
        ext2[0:tq, :] = dxc
        dxr_ref[...] = _conv_taps_t(ext2, cw_ref, tq, 4).astype(BF16)
        ext2[tq:, :] = ext2[0:SUB, :]
        _conv_dw(ext, ext2, dcw_ref, SUB, tq, 4)
        dcb_ref[...] += jnp.sum(dxc, axis=0, keepdims=True)

    one = pl.BlockSpec((1, c), lambda i: (0, 0))
    tile = pl.BlockSpec((tq, c), lambda i: (rev(i), 0))
    prev = pl.BlockSpec((SUB, c), lambda i: (jnp.maximum(rev(i) * per - 1, 0), 0))
    hw = pl.BlockSpec((LRU_HEADS, LANE, LANE), lambda i: (0, 0, 0))
    cw4 = pl.BlockSpec((4, c), lambda i: (0, 0))
    return pl.pallas_call(
        body,
        grid=(ntile,),
        in_specs=[tile, tile, prev, tile, tile, prev, cw4, one, hw, one, hw, one, one],
        out_specs=[tile, tile, cw4, one, hw, one, hw, one, one],
        out_shape=[_sds((n, c), BF16), _sds((n, c), BF16), _sds((4, c), F32), _sds((1, c), F32), _sds((LRU_HEADS, LANE, LANE), F32),
                   _sds((1, c), F32), _sds((LRU_HEADS, LANE, LANE), F32), _sds((1, c), F32), _sds((1, c), F32)],
        scratch_shapes=[pltpu.VMEM((SUB, c), F32), pltpu.VMEM((SUB + tq, c), F32), pltpu.VMEM((tq + SUB, c), F32)],
        compiler_params=_params(("arbitrary",)),
        name=name,
    )(dy, xr, xr, gr, h, h, cw, cb, wa, ba, wx, bx, lam)


def ada_fwd(c_all, w, b, name):
    nl, _, cols = w.shape
    nb = c_all.shape[0]

    def body(c_ref, w_ref, b_ref, o_ref):
        sc = _silu(c_ref[...]).astype(BF16)
        o_ref[0] = jnp.dot(sc, w_ref[0].astype(BF16), preferred_element_type=F32) + b_ref[0]

    return pl.pallas_call(
        body,
        grid=(nl,),
        in_specs=[pl.BlockSpec((nb, D), lambda l: (0, 0)), pl.BlockSpec((1, D, cols), lambda l: (l, 0, 0)), pl.BlockSpec((1, 1, cols), lambda l: (l, 0, 0))],
        out_specs=pl.BlockSpec((1, nb, cols), lambda l: (l, 0, 0)),
        out_shape=_sds((nl, nb, cols), F32),
        compiler_params=_params(("arbitrary",)),
        name=name,
    )(c_all, w, b)


def ada_bwd(c_all, dmod, name):
    nl, nb, cols = dmod.shape

    def body(c_ref, d_ref, o_ref):
        sc = _silu(c_ref[...]).astype(BF16)
        o_ref[0] = lax.dot_general(sc, d_ref[0].astype(BF16), (((0,), (0,)), ((), ())), preferred_element_type=F32)

    return pl.pallas_call(
        body,
        grid=(nl,),
        in_specs=[pl.BlockSpec((nb, D), lambda l: (0, 0)), pl.BlockSpec((1, nb, cols), lambda l: (l, 0, 0))],
        out_specs=pl.BlockSpec((1, D, cols), lambda l: (l, 0, 0)),
        out_shape=_sds((nl, D, cols), F32),
        compiler_params=_params(("arbitrary",)),
        name=name,
    )(c_all, dmod)


def loss_grad(y, target, name):
    n = y.shape[0]

    def body(y_ref, t_ref, dy_ref, l_ref, acc):
        i = pl.program_id(0)

        @pl.when(i == 0)
        def _():
            acc[...] = jnp.zeros_like(acc)

        e = y_ref[...] - t_ref[...]
        dy_ref[...] = e * (1.0 / D)
        acc[...] += jnp.sum(e * e, axis=0, keepdims=True)

        @pl.when(i == n // TM - 1)
        def _():
            l_ref[...] = jnp.full((1, LANE), 0.5 / D, F32) * jnp.sum(acc[...])

    row = pl.BlockSpec((TM, D), lambda i: (i, 0))
    return pl.pallas_call(
        body,
        grid=(n // TM,),
        in_specs=[row, row],
        out_specs=[row, pl.BlockSpec((1, LANE), lambda i: (0, 0))],
        out_shape=[_sds((n, D), F32), _sds((1, LANE), F32)],
        scratch_shapes=[pltpu.VMEM((1, D), F32)],
        compiler_params=_params(("arbitrary",)),
        name=name,
    )(y, target)


def sum_parts(parts, name):
    ns, r, _ = parts.shape
    tr = _row_tile(r, 1024)

    def body(p_ref, o_ref):
        acc = p_ref[0]
        for k in range(1, ns):
            acc = acc + p_ref[k]
        o_ref[...] = acc

    return pl.pallas_call(
        body,
        grid=(r // tr,),
        in_specs=[pl.BlockSpec((ns, tr, LANE), lambda i: (0, i, 0))],
        out_specs=pl.BlockSpec((tr, LANE), lambda i: (i, 0)),
        out_shape=_sds((r, LANE), F32),
        compiler_params=_params(("arbitrary",)),
        name=name,
    )(parts)


def _row_tile(r, cap):
    if r <= cap:
        return r
    best = None
    for t in range(16, cap + 1, 16):
        if r % t == 0:
            best = t
    assert best is not None, r
    return best


def adamw(w, m, v, gparts, name):
    r, c = w.shape
    ns = gparts.shape[0]
    tr = _row_tile(r, min(512, 256 * 1024 // c))
    c1 = 1.0 - B1**STEP
    c2 = 1.0 - B2**STEP

    def body(w_ref, m_ref, v_ref, g_ref, go_ref, d_ref, mo_ref, vo_ref):
        g = g_ref[0].astype(F32)
        for k in range(1, ns):
            g = g + g_ref[k].astype(F32)
        mn = B1 * m_ref[...] + (1.0 - B1) * g
        vn = B2 * v_ref[...] + (1.0 - B2) * (g * g)
        go_ref[...] = g
        mo_ref[...] = mn
        vo_ref[...] = vn
        d_ref[...] = -LR * ((mn / c1) / (jnp.sqrt(vn / c2) + AEPS) + WD * w_ref[...])

    tile = pl.BlockSpec((tr, c), lambda i: (i, 0))
    return pl.pallas_call(
        body,
        grid=(r // tr,),
        in_specs=[tile, tile, tile, pl.BlockSpec((ns, tr, c), lambda i: (0, i, 0))],
        out_specs=[tile, tile, tile, tile],
        out_shape=[_sds((r, c), F32)] * 4,
        compiler_params=_params(("arbitrary",)),
        name=name,
    )(w, m, v, gparts)


WEIGHTS = ["ada_w", "ada_b", "ln_g", "ln_b", "ffn_w_in", "ffn_w_out", "ev_w_in", "ssd_conv_w", "ssd_conv_b", "ssd_dt_bias",
           "ssd_a_log", "ssd_d", "ssd_norm_g", "pool_w", "pool_scale", "ev_w_out", "od_w_in", "conf_dw_w", "conf_dw_b",
           "conf_ln_g", "conf_ln_b", "lru_conv_w", "lru_conv_b", "lru_wa", "lru_ba", "lru_wx", "lru_bx", "lru_lambda", "od_w_out"]
BIG = ("ada_w", "ffn_w_in", "ffn_w_out", "ev_w_in", "ev_w_out", "od_w_in", "od_w_out")
SMALL = {
    "ada_b": ((4, 9216), None), "ln_g": ((4, 3, 1024), 2), "ln_b": ((4, 3, 1024), 2),
    "ssd_conv_w": ((2, 4, 1536), 2), "ssd_conv_b": ((2, 1536), None), "ssd_dt_bias": ((2, 16), None),
    "ssd_a_log": ((2, 16), None), "ssd_d": ((2, 16), None), "ssd_norm_g": ((2, 1024), None),
    "pool_w": ((2, 4, 128, 128), None), "pool_scale": ((2, 512), None),
    "conf_dw_w": ((2, 31, 512), 2), "conf_dw_b": ((2, 512), 1), "conf_ln_g": ((2, 512), 1), "conf_ln_b": ((2, 512), 1),
    "lru_conv_w": ((2, 4, 1024), 2), "lru_conv_b": ((2, 1024), 1), "lru_wa": ((2, 8, 128, 128), None),
    "lru_ba": ((2, 1024), 1), "lru_wx": ((2, 8, 128, 128), None), "lru_bx": ((2, 1024), 1), "lru_lambda": ((2, 1024), 1),
}
PACK_ROWS = 2 * SUB * LANE


def _pack(arrs, mult=PACK_ROWS):
    flat = jnp.concatenate([a.reshape(-1) for a in arrs])
    pad = (-flat.shape[0]) % mult
    return jnp.pad(flat, (0, pad)).reshape(-1, LANE)


def _unpack(buf, shapes, lead=()):
    flat = buf.reshape(lead + (-1,))
    out, off = [], 0
    for s in shapes:
        k = math.prod(s)
        out.append(flat[..., off : off + k].reshape(lead + tuple(s)))
        off += k
    return out


def _pad_lanes(v):
    return jnp.pad(v, (0, LANE - v.shape[0]))[None]


def kernel(x, c, ada_w, ada_b, ln_g, ln_b, ffn_w_in, ffn_w_out, ev_w_in, ssd_conv_w, ssd_conv_b, ssd_dt_bias, ssd_a_log, ssd_d, ssd_norm_g, pool_w, pool_scale, ev_w_out, od_w_in, conf_dw_w, conf_dw_b, conf_ln_g, conf_ln_b, lru_conv_w, lru_conv_b, lru_wa, lru_ba, lru_wx, lru_bx, lru_lambda, od_w_out, loss_target, m_ada_w, m_ada_b, m_ln_g, m_ln_b, m_ffn_w_in, m_ffn_w_out, m_ev_w_in, m_ssd_conv_w, m_ssd_conv_b, m_ssd_dt_bias, m_ssd_a_log, m_ssd_d, m_ssd_norm_g, m_pool_w, m_pool_scale, m_ev_w_out, m_od_w_in, m_conf_dw_w, m_conf_dw_b, m_conf_ln_g, m_conf_ln_b, m_lru_conv_w, m_lru_conv_b, m_lru_wa, m_lru_ba, m_lru_wx, m_lru_bx, m_lru_lambda, m_od_w_out, v_ada_w, v_ada_b, v_ln_g, v_ln_b, v_ffn_w_in, v_ffn_w_out, v_ev_w_in, v_ssd_conv_w, v_ssd_conv_b, v_ssd_dt_bias, v_ssd_a_log, v_ssd_d, v_ssd_norm_g, v_pool_w, v_pool_scale, v_ev_w_out, v_od_w_in, v_conf_dw_w, v_conf_dw_b, v_conf_ln_g, v_conf_ln_b, v_lru_conv_w, v_lru_conv_b, v_lru_wa, v_lru_ba, v_lru_wx, v_lru_bx, v_lru_lambda, v_od_w_out):
    p = dict(locals())
    nb, seq, _ = x.shape
    n = nb * seq
    me = 4 * lax.axis_index("x") + 2 * lax.axis_index("y") + lax.axis_index("c")
    sharded = [k for k, (_, ax) in SMALL.items() if ax is not None]

    def cols_of(g):
        return jnp.moveaxis(g, 0, 1).reshape(g.shape[1], N_DEV * g.shape[2])

    def rows_of(g):
        return g.reshape(N_DEV * g.shape[1], g.shape[2])

    def ev_in_of(g):
        w = cols_of(g)
        return jnp.concatenate([w[:, :2560], w[:, 2576:], jnp.pad(w[:, 2560:2576], ((0, 0), (0, LANE - SSD_HEADS)))], axis=1)

    sh_ffn_in, sh_ffn_out = ffn_w_in.astype(BF16), ffn_w_out.astype(BF16)
    sh_mix_in = [ev_w_in.astype(BF16), od_w_in.astype(BF16)]
    sh_mix_out = [ev_w_out.astype(BF16), od_w_out.astype(BF16)]

    def ffn_items(l, i):
        return [(sh_ffn_in[l, i], True), (sh_ffn_out[l, i], True)]

    def mix_items(l):
        return [(sh_mix_in[l % 2][l // 2], True), (sh_mix_out[l % 2][l // 2], True)]

    sm_local_shapes = [p[k].shape for k in sharded]
    g_in, g_out, sm_all = exchange(ffn_items(0, 0) + [(_pack([p[k] for k in sharded] + [c]), True)], "ag_first")
    w_ffn = {(0, 0): (cols_of(g_in), rows_of(g_out))}
    w_mix = {}
    got = _unpack(sm_all, sm_local_shapes + [c.shape], lead=(N_DEV,))
    full = {k: p[k] for k, (_, ax) in SMALL.items() if ax is None}
    for k, g in zip(sharded, got[:-1]):
        full[k] = jnp.moveaxis(g, 0, SMALL[k][1]).reshape(SMALL[k][0])
    c_all = got[-1].reshape(N_DEV * nb, D)

    cols = ada_w.shape[-1]
    ada_b_loc = lax.dynamic_slice_in_dim(ada_b, me * cols, cols, axis=1)[:, None, :]
    mod_cols = ada_fwd(c_all, ada_w, ada_b_loc, "ada_fwd")
    (mod_x,) = exchange([(mod_cols.reshape(DEPTH, N_DEV, nb, cols).transpose(1, 0, 2, 3), False)], "a2a_mod")
    mod = mod_x.transpose(1, 2, 0, 3).reshape(DEPTH, nb, N_MOD, 1, D)

    def vec(l, j):
        return mod[l, :, j]

    def row(a):
        return a[None]

    xs = x.reshape(n, D)
    saved = []
    for l in range(DEPTH):
        s = {"x0": xs}
        e = l // 2
        s["h1"], s["g1"], s["u1"], s["a1"], gm_in, gm_out, g_in, g_out = ffn_up(
            xs, vec(l, 0), vec(l, 1), w_ffn[l, 0][0], seq, "ffn_up_c4", carry=mix_items(l) + ffn_items(l, 1))
        w_mix[l] = ((ev_in_of if l % 2 == 0 else cols_of)(gm_in), rows_of(gm_out))
        w_ffn[l, 1] = (cols_of(g_in), rows_of(g_out))
        x1, s["y1"] = mm_postnorm([s["a1"]], w_ffn[l, 0][1], xs, vec(l, 2), row(full["ln_g"][l, 0]), row(full["ln_b"][l, 0]), 0.5, seq, "ffn_down")
        s["x1"] = x1
        if l % 2 == 0:
            s["h2"], s["z"], s["xbc"], s["u"], s["dtr"] = mod_mm(x1, vec(l, 3), vec(l, 4), w_mix[l][0], EV_SPLITS, seq, "ev_in")
            s["ya"], s["sprev"] = ssd_fwd(s["z"], s["xbc"], s["dtr"], full["ssd_conv_w"][e], row(full["ssd_conv_b"][e]), _pad_lanes(full["ssd_dt_bias"][e]),
                                          _pad_lanes(full["ssd_a_log"][e]), _pad_lanes(full["ssd_d"][e]), row(full["ssd_norm_g"][e]), seq, "ssd_fwd")
            s["yb"] = pool_fwd(s["u"], full["pool_w"][e], row(full["pool_scale"][e]), seq, "pool_fwd")
        else:
            s["h2"], s["vg"], s["xr"], s["gr"] = mod_mm(x1, vec(l, 3), vec(l, 4), w_mix[l][0], OD_SPLITS, seq, "od_in")
            s["ya"], s["conv"] = conf_fwd(s["vg"], full["conf_dw_w"][e], row(full["conf_dw_b"][e]), row(full["conf_ln_g"][e]), row(full["conf_ln_b"][e]), seq, "conf_fwd")
            s["yb"], s["hst"] = lru_fwd(s["xr"], s["gr"], full["lru_conv_w"][e], row(full["lru_conv_b"][e]), full["lru_wa"][e], row(full["lru_ba"][e]),
                                        full["lru_wx"][e], row(full["lru_bx"][e]), row(full["lru_lambda"][e]), seq, "lru_fwd")
        x2, s["y2"] = mm_postnorm([s["ya"], s["yb"]], w_mix[l][1], x1, vec(l, 5), row(full["ln_g"][l, 1]), row(full["ln_b"][l, 1]), 1.0, seq, "mix_out")
        s["x2"] = x2
        if l + 1 < DEPTH:
            s["h3"], s["g3"], s["u3"], s["a3"], g_in, g_out = ffn_up(x2, vec(l, 6), vec(l, 7), w_ffn[l, 1][0], seq, "ffn_up_c2", carry=ffn_items(l + 1, 0))
            w_ffn[l + 1, 0] = (cols_of(g_in), rows_of(g_out))
        else:
            s["h3"], s["g3"], s["u3"], s["a3"] = ffn_up(x2, vec(l, 6), vec(l, 7), w_ffn[l, 1][0], seq, "ffn_up")
        xs, s["y3"] = mm_postnorm([s["a3"]], w_ffn[l, 1][1], x2, vec(l, 8), row(full["ln_g"][l, 2]), row(full["ln_b"][l, 2]), 0.5, seq, "ffn_down")
        saved.append(s)

    dx, loss_row = loss_grad(xs, loss_target.reshape(n, D), "loss")
    loss = lax.psum(loss_row[0, 0], ("x", "y", "c"))

    sg = {k: [None] * shape[0] for k, (shape, _) in SMALL.items()}
    sg["ln_g"] = [[None] * 3 for _ in range(DEPTH)]
    sg["ln_b"] = [[None] * 3 for _ in range(DEPTH)]
    dmod = [[None] * N_MOD for _ in range(DEPTH)]
    pending, got_w = [], {}

    def cut_cols(g):
        r, cc = g.shape
        return g.reshape(r, N_DEV, cc // N_DEV).transpose(1, 0, 2).astype(BF16)

    def cut_rows(g):
        r, cc = g.shape
        return g.reshape(N_DEV, r // N_DEV, cc).astype(BF16)

    def take(only_out):
        sel = [j for j, (key, _) in enumerate(pending) if key[0].endswith("_out") or not only_out]
        items = [pending[j] for j in sel]
        pending[:] = [it for j, it in enumerate(pending) if j not in sel]
        return [k for k, _ in items], [(a, False) for _, a in items]

    def postnorm_backward(dxo, xin, y, g, lng, lnb, w, ks, coef, name):
        keys, carry = take(True)
        outs = postnorm_bwd(dxo, xin, y, g, lng, lnb, w, ks, coef, seq, name + "_c%d" % len(keys), carry=carry)
        got_w.update(zip(keys, outs[5 + len(ks):]))
        return outs[0], outs[1], outs[2 : 2 + len(ks)], outs[2 + len(ks)], outs[3 + len(ks)], outs[4 + len(ks)]

    def ffn_backward(l, i, dxo, s, xin, hk, gk, uk, ak, yk, jbase, lnj):
        dxres, dy, (da,), dmod[l][jbase + 2], sg["ln_g"][l][lnj], sg["ln_b"][l][lnj] = postnorm_backward(
            dxo, xin, s[yk], vec(l, jbase + 2), row(full["ln_g"][l, lnj]), row(full["ln_b"][l, lnj]), w_ffn[l, i][1], [FF], 0.5, "ffn_down_bwd")
        keys, carry = take(False)
        outs = ffn_bwd_in(da, s[gk], s[uk], w_ffn[l, i][0], xin, vec(l, jbase + 1), dxres, seq, "ffn_up_bwd_c%d" % len(keys), carry=carry)
        dg, du, dxi, dmod[l][jbase], dmod[l][jbase + 1] = outs[:5]
        got_w.update(zip(keys, outs[5:]))
        pending.append((("ffn_out", l, i), cut_rows(mm_tn(s[ak], dy, "wg_ffn_out"))))
        pending.append((("ffn_in", l, i), cut_cols(jnp.concatenate([mm_tn(s[hk], dg, "wg_ffn_in"), mm_tn(s[hk], du, "wg_ffn_in")], axis=1))))
        return dxi

    for l in reversed(range(DEPTH)):
        s = saved[l]
        e = l // 2
        dx = ffn_backward(l, 1, dx, s, s["x2"], "h3", "g3", "u3", "a3", "y3", 6, 2)
        ks = [1024, POOL_DIM] if l % 2 == 0 else [CONF_DIM, LRU_DIM]
        dxres, dy, (dya, dyb), dmod[l][5], sg["ln_g"][l][1], sg["ln_b"][l][1] = postnorm_backward(
            dx, s["x1"], s["y2"], vec(l, 5), row(full["ln_g"][l, 1]), row(full["ln_b"][l, 1]), w_mix[l][1], ks, 1.0, "mix_out_bwd")
        pending.append((("mix_out", l), cut_rows(jnp.concatenate([mm_tn(s["ya"], dy, "wg_mix_a"), mm_tn(s["yb"], dy, "wg_mix_b")], axis=0))))
        if l % 2 == 0:
            (dz, dxbc, ddt, sg["ssd_conv_w"][e], dcb, ddtb, dalog, ddsk, dng) = ssd_bwd(
                dya, s["z"], s["xbc"], s["dtr"], s["sprev"], full["ssd_conv_w"][e], row(full["ssd_conv_b"][e]), _pad_lanes(full["ssd_dt_bias"][e]),
                _pad_lanes(full["ssd_a_log"][e]), _pad_lanes(full["ssd_d"][e]), row(full["ssd_norm_g"][e]), seq, "ssd_bwd")
            sg["ssd_conv_b"][e], sg["ssd_norm_g"][e] = dcb[0], dng[0]
            sg["ssd_dt_bias"][e], sg["ssd_a_log"][e], sg["ssd_d"][e] = ddtb[0, :SSD_HEADS], dalog[0, :SSD_HEADS], ddsk[0, :SSD_HEADS]
            du, sg["pool_w"][e], dps = pool_bwd(dyb, s["u"], full["pool_w"][e], row(full["pool_scale"][e]), seq, "pool_bwd")
            sg["pool_scale"][e] = dps[0]
            dparts = [dz, dxbc, du, ddt]
            dx, dmod[l][3], dmod[l][4] = proj_bwd_in(dparts, w_mix[l][0], s["x1"], vec(l, 4), dxres, seq, "ev_in_bwd")
            gz, gxbc, gu, gdt = [mm_tn(s["h2"], dp, "wg_ev_in") for dp in dparts]
            pending.append((("mix_in", l), cut_cols(jnp.concatenate([gz, gxbc, gdt[:, :SSD_HEADS], gu], axis=1))))
        else:
            dvg, sg["conf_dw_w"][e], dcb, dlg, dlb = conf_bwd(dya, s["conv"], s["vg"], full["conf_dw_w"][e], row(full["conf_ln_g"][e]), row(full["conf_ln_b"][e]), seq, "conf_bwd")
            sg["conf_dw_b"][e], sg["conf_ln_g"][e], sg["conf_ln_b"][e] = dcb[0], dlg[0], dlb[0]
            (dxr, dgr, sg["lru_conv_w"][e], dcb, sg["lru_wa"][e], dba, sg["lru_wx"][e], dbx, dlam) = lru_bwd(
                dyb, s["xr"], s["gr"], s["hst"], full["lru_conv_w"][e], row(full["lru_conv_b"][e]), full["lru_wa"][e], row(full["lru_ba"][e]),
                full["lru_wx"][e], row(full["lru_bx"][e]), row(full["lru_lambda"][e]), seq, "lru_bwd")
            sg["lru_conv_b"][e], sg["lru_ba"][e], sg["lru_bx"][e], sg["lru_lambda"][e] = dcb[0], dba[0], dbx[0], dlam[0]
            dparts = [dvg, dxr, dgr]
            dx, dmod[l][3], dmod[l][4] = proj_bwd_in(dparts, w_mix[l][0], s["x1"], vec(l, 4), dxres, seq, "od_in_bwd")
            pending.append((("mix_in", l), cut_cols(jnp.concatenate([mm_tn(s["h2"], dp, "wg_od_in") for dp in dparts], axis=1))))
        dx = ffn_backward(l, 0, dx, s, s["x0"], "h1", "g1", "u1", "a1", "y1", 0, 0)
    grad_x = dx.reshape(nb, seq, D)

    dmod_mine = jnp.stack([jnp.concatenate([d[:, 0, :] for d in dmod[l]], axis=-1) for l in range(DEPTH)])
    sg["ada_b"] = [jnp.sum(dmod_mine[l], axis=0) for l in range(DEPTH)]
    sg["ln_g"] = [jnp.concatenate(r, axis=0) for r in sg["ln_g"]]
    sg["ln_b"] = [jnp.concatenate(r, axis=0) for r in sg["ln_b"]]
    small_names = list(SMALL)
    sg_packed = _pack([jnp.stack(sg[k]).reshape(SMALL[k][0]) for k in small_names], N_DEV * PACK_ROWS)
    keys, carry = take(False)
    outs = exchange(carry + [(dmod_mine.reshape(DEPTH, nb, N_DEV, cols).transpose(2, 0, 1, 3), False),
                             (sg_packed.reshape(N_DEV, -1, LANE), False)], "x_last")
    got_w.update(zip(keys, outs))
    dmod_x, parts = outs[len(keys):]
    g_ada_w = ada_bwd(c_all, dmod_x.transpose(1, 0, 2, 3).reshape(DEPTH, N_DEV * nb, cols), "ada_bwd")

    (sg_sum,) = exchange([(sum_parts(parts, "sum_smallgrad"), True)], "ag_smallsum")
    summed = _unpack(sg_sum.reshape(-1, LANE), [SMALL[k][0] for k in small_names])
    grads = {}
    for k, g in zip(small_names, summed):
        ax = SMALL[k][1]
        grads[k] = g if ax is None else lax.dynamic_slice_in_dim(g, me * p[k].shape[ax], p[k].shape[ax], axis=ax)
    loc_shapes = [p[k].shape for k in small_names]
    whole = 512 * LANE
    _, d_s, m_s, v_s = adamw(_pack([p[k] for k in small_names], whole), _pack([p["m_" + k] for k in small_names], whole),
                             _pack([p["v_" + k] for k in small_names], whole), _pack([grads[k] for k in small_names], whole)[None], "adamw_small")
    delta = dict(zip(small_names, _unpack(d_s, loc_shapes)))
    new_m = dict(zip(small_names, _unpack(m_s, loc_shapes)))
    new_v = dict(zip(small_names, _unpack(v_s, loc_shapes)))

    big_parts = {
        "ada_w": g_ada_w[None],
        "ffn_w_in": jnp.stack([jnp.stack([got_w["ffn_in", l, i] for i in range(2)], axis=1) for l in range(DEPTH)], axis=1),
        "ffn_w_out": jnp.stack([jnp.stack([got_w["ffn_out", l, i] for i in range(2)], axis=1) for l in range(DEPTH)], axis=1),
        "ev_w_in": jnp.stack([got_w["mix_in", l] for l in (0, 2)], axis=1),
        "ev_w_out": jnp.stack([got_w["mix_out", l] for l in (0, 2)], axis=1),
        "od_w_in": jnp.stack([got_w["mix_in", l] for l in (1, 3)], axis=1),
        "od_w_out": jnp.stack([got_w["mix_out", l] for l in (1, 3)], axis=1),
    }
    for k in BIG:
        w = p[k]
        r2 = (math.prod(w.shape[:-1]), w.shape[-1])
        gp = big_parts[k]
        out = adamw(w.reshape(r2), p["m_" + k].reshape(r2), p["v_" + k].reshape(r2), gp.reshape((gp.shape[0],) + r2), "adamw_" + k)
        grads[k], delta[k], new_m[k], new_v[k] = [o.reshape(w.shape) for o in out]

    return (loss, grad_x, *[grads[k] for k in WEIGHTS], *[delta[k] for k in WEIGHTS], *[new_m[k] for k in WEIGHTS], *[new_v[k] for k in WEIGHTS])
```

```python
import functools
import math

import jax
import jax.numpy as jnp
from jax import lax
from jax.experimental import pallas as pl
from jax.experimental.pallas import tpu as pltpu

F32 = jnp.float32
BF16 = jnp.bfloat16
HI = lax.Precision.HIGHEST

N_DEV = 8
D = 1024
DEPTH = 4
N_MOD = 9
FF = 2816
ALPHA = (2.0 * DEPTH) ** 0.25
EPS = 1e-5
SSD_Q = 128
SSD_HEADS = 16
SSD_P = 64
SSD_N = 128
SSD_XBC = 1536
POOL_WINDOWS = (2, 4, 8, 16)
POOL_DIM = 512
CONF_DIM = 512
CONF_K = 31
LRU_DIM = 1024
LRU_HEADS = 8
LRU_C = 8.0
EV_SPLITS = (1024, 1536, 512, 128)
OD_SPLITS = (1024, 1024, 1024)
LR, B1, B2, AEPS, WD, STEP = 0.001, 0.9, 0.999, 1e-08, 0.01, 10

LANE = 128
SUB = 8
MIB = 1024 * 1024
VMEM_LIMIT = 48 * MIB
TM = 512
SPLIT_ROWS = 256


def _params(sem, vmem=VMEM_LIMIT):
    return pltpu.CompilerParams(dimension_semantics=sem, vmem_limit_bytes=vmem)


def _sds(shape, dtype):
    return jax.ShapeDtypeStruct(shape, dtype)


def _modulate(x, sh, sc):
    return x * (1.0 + sc) + sh


def _postnorm(x, y, g, lng, lnb, *, coef):
    z = ALPHA * x + coef * (1.0 + g) * y
    mu = jnp.mean(z, axis=-1, keepdims=True)
    zc = z - mu
    var = jnp.mean(zc * zc, axis=-1, keepdims=True)
    return zc * lax.rsqrt(var + EPS) * lng + lnb


def _place():
    mx, my, mc = lax.axis_index("x"), lax.axis_index("y"), lax.axis_index("c")

    def at(r):
        px = 1 - mx if r & 4 else mx
        py = 1 - my if r & 2 else my
        pc = 1 - mc if r & 1 else mc
        return (px, py, pc), 4 * px + 2 * py + pc

    return 4 * mx + 2 * my + mc, at


def _carry_plan(items):
    hbm = pl.BlockSpec(memory_space=pltpu.HBM)
    k = len(items)
    shapes = [_sds((N_DEV,) + a.shape if g else a.shape, a.dtype) for a, g in items]
    scratch = [pltpu.SemaphoreType.DMA((k * (N_DEV - 1),)), pltpu.SemaphoreType.DMA((k * (N_DEV - 1),)), pltpu.SemaphoreType.DMA((k,))] if k else []
    return [hbm] * k, [hbm] * k, shapes, scratch


def _remote(src, dst, sems, s, pos):
    return pltpu.make_async_remote_copy(src_ref=src, dst_ref=dst, send_sem=sems[0].at[s], recv_sem=sems[1].at[s],
                                        device_id=pos, device_id_type=pl.DeviceIdType.MESH)


def _carry_start(gathers, x_refs, o_refs, sems):
    me, at = _place()
    for a, (gather, x_ref, o_ref) in enumerate(zip(gathers, x_refs, o_refs)):
        base = a * (N_DEV - 1)
        pltpu.make_async_copy(x_ref if gather else x_ref.at[me], o_ref.at[me], sems[2].at[a]).start()
        if gather:
            for s, r in enumerate((1, 4, 2, 6)):
                _remote(x_ref, o_ref.at[me], sems, base + s, at(r)[0]).start()
        else:
            for r in range(1, N_DEV):
                pos, pid = at(r)
                _remote(x_ref.at[pid], o_ref.at[me], sems, base + r - 1, pos).start()


def _carry_pass_on(gathers, x_refs, o_refs, sems):
    _, at = _place()
    sibling = at(1)[0]
    for a, (gather, x_ref, o_ref) in enumerate(zip(gathers, x_refs, o_refs)):
        if gather:
            base = a * (N_DEV - 1)
            for j, r in enumerate((4, 2, 6)):
                pos, pid = at(r)
                _remote(x_ref, o_ref.at[pid], sems, base + 1 + j, pos).wait_recv()
                _remote(o_ref.at[pid], o_ref.at[pid], sems, base + 4 + j, sibling).start()


def _carry_wait(gathers, x_refs, o_refs, sems):
    me, at = _place()
    for a, (gather, x_ref, o_ref) in enumerate(zip(gathers, x_refs, o_refs)):
        base = a * (N_DEV - 1)
        if gather:
            sib_pos, sib_id = at(1)
            _remote(x_ref, o_ref.at[sib_id], sems, base, sib_pos).wait_recv()
            for j, r in enumerate((4, 2, 6)):
                _remote(x_ref, o_ref.at[at(r | 1)[1]], sems, base + 4 + j, sib_pos).wait_recv()
            for s in range(N_DEV - 1):
                _remote(x_ref, o_ref.at[me], sems, base + s, sib_pos).wait_send()
            pltpu.make_async_copy(x_ref, o_ref.at[me], sems[2].at[a]).wait()
        else:
            for r in range(1, N_DEV):
                pos, pid = at(r)
                _remote(x_ref.at[pid], o_ref.at[pid], sems, base + r - 1, pos).wait_recv()
            for r in range(1, N_DEV):
                pos, pid = at(r)
                _remote(x_ref.at[pid], o_ref.at[me], sems, base + r - 1, pos).wait_send()
            pltpu.make_async_copy(x_ref.at[me], o_ref.at[me], sems[2].at[a]).wait()


def exchange(items, name):
    gathers = [g for _, g in items]
    k = len(items)
    in_specs, out_specs, shapes, scratch = _carry_plan(items)

    def body(*refs):
        x_refs, o_refs, sems = refs[:k], refs[k : 2 * k], refs[2 * k :]
        _carry_start(gathers, x_refs, o_refs, sems)
        _carry_pass_on(gathers, x_refs, o_refs, sems)
        _carry_wait(gathers, x_refs, o_refs, sems)

    return pl.pallas_call(
        body,
        in_specs=in_specs,
        out_specs=out_specs,
        out_shape=shapes,
        scratch_shapes=scratch,
        compiler_params=pltpu.CompilerParams(has_side_effects=True),
        name=name,
    )(*[a for a, _ in items])


def ffn_up(x, sh, sc, w, seq, name, carry=()):
    n = x.shape[0]
    tn = FF // 2
    nj = FF // tn
    ni = n // TM
    tps = seq // TM
    k = len(carry)
    gathers = [g for _, g in carry]
    c_in, c_out, c_shapes, c_scratch = _carry_plan(carry)

    def body(x_ref, sh_ref, sc_ref, w_hbm, *rest):
        cx, (h_ref, g_ref, u_ref, a_ref), co, (w_ref, w_sem, *sems) = rest[:k], rest[k : k + 4], rest[k + 4 : 2 * k + 4], rest[2 * k + 4 :]
        i = pl.program_id(0)

        @pl.when(i == 0)
        def _():
            if k:
                _carry_start(gathers, cx, co, sems)
            cp = pltpu.make_async_copy(w_hbm, w_ref, w_sem)
            cp.start()
            cp.wait()

        h = _modulate(x_ref[...], sh_ref[0], sc_ref[0]).astype(BF16)
        h_ref[...] = h
        for j in range(nj):
            g = jnp.dot(h, w_ref[:, j * tn : (j + 1) * tn], preferred_element_type=F32)
            u = jnp.dot(h, w_ref[:, FF + j * tn : FF + (j + 1) * tn], preferred_element_type=F32)
            g_ref[:, j * tn : (j + 1) * tn] = g.astype(BF16)
            u_ref[:, j * tn : (j + 1) * tn] = u.astype(BF16)
            a_ref[:, j * tn : (j + 1) * tn] = (g * jax.nn.sigmoid(g) * u).astype(BF16)
        if k:
            @pl.when(i == max(ni - 2, 0))
            def _():
                _carry_pass_on(gathers, cx, co, sems)

            @pl.when(i == ni - 1)
            def _():
                _carry_wait(gathers, cx, co, sems)

    vec = pl.BlockSpec((1, 1, D), lambda i: (i // tps, 0, 0))
    col = pl.BlockSpec((TM, FF), lambda i: (i, 0))
    return pl.pallas_call(
        body,
        grid=(ni,),
        in_specs=[pl.BlockSpec((TM, D), lambda i: (i, 0)), vec, vec, pl.BlockSpec(memory_space=pltpu.HBM)] + c_in,
        out_specs=[pl.BlockSpec((TM, D), lambda i: (i, 0)), col, col, col] + c_out,
        out_shape=[_sds((n, D), BF16), _sds((n, FF), BF16), _sds((n, FF), BF16), _sds((n, FF), BF16)] + c_shapes,
        scratch_shapes=[pltpu.VMEM((D, 2 * FF), BF16), pltpu.SemaphoreType.DMA] + c_scratch,
        compiler_params=_params(("arbitrary",)),
        name=name,
    )(x, sh, sc, w, *[a for a, _ in carry])


def mod_mm(x, sh, sc, w, splits, seq, name):
    n = x.shape[0]
    m = w.shape[1]
    tps = seq // TM
    offs = [sum(splits[:k]) for k in range(len(splits))]

    def body(x_ref, sh_ref, sc_ref, w_ref, h_ref, *outs):
        h = _modulate(x_ref[...], sh_ref[0], sc_ref[0]).astype(BF16)
        h_ref[...] = h
        for o_ref, off, wd in zip(outs, offs, splits):
            o_ref[...] = jnp.dot(h, w_ref[:, off : off + wd], preferred_element_type=F32)

    vec = pl.BlockSpec((1, 1, D), lambda i: (i // tps, 0, 0))
    return pl.pallas_call(
        body,
        grid=(n // TM,),
        in_specs=[pl.BlockSpec((TM, D), lambda i: (i, 0)), vec, vec, pl.BlockSpec((D, m), lambda i: (0, 0))],
        out_specs=[pl.BlockSpec((TM, D), lambda i: (i, 0))] + [pl.BlockSpec((TM, wd), lambda i: (i, 0)) for wd in splits],
        out_shape=[_sds((n, D), BF16)] + [_sds((n, wd), F32) for wd in splits],
        compiler_params=_params(("arbitrary",)),
        name=name,
    )(x, sh, sc, w)


def mm_postnorm(parts, w, x, g, lng, lnb, coef, seq, name):
    n = x.shape[0]
    tps = seq // TM
    ks = [p.shape[1] for p in parts]
    offs = [sum(ks[:k]) for k in range(len(ks))]
    npart = len(parts)

    def body(*refs):
        a_refs = refs[:npart]
        w_ref, x_ref, g_ref, lng_ref, lnb_ref, xn_ref, y_ref = refs[npart:]
        for r0 in range(0, TM, SPLIT_ROWS):
            rows = slice(r0, r0 + SPLIT_ROWS)
            y = None
            for a_ref, off, k in zip(a_refs, offs, ks):
                t = jnp.dot(a_ref[rows, :], w_ref[off : off + k, :], preferred_element_type=F32)
                y = t if y is None else y + t
            y_ref[rows, :] = y
            xn_ref[rows, :] = _postnorm(x_ref[rows, :], y, g_ref[0], lng_ref[...], lnb_ref[...], coef=coef)

    row = pl.BlockSpec((TM, D), lambda i: (i, 0))
    one = pl.BlockSpec((1, D), lambda i: (0, 0))
    return pl.pallas_call(
        body,
        grid=(n // TM,),
        in_specs=[pl.BlockSpec((TM, k), lambda i: (i, 0)) for k in ks]
        + [pl.BlockSpec((sum(ks), D), lambda i: (0, 0)), row, pl.BlockSpec((1, 1, D), lambda i: (i // tps, 0, 0)), one, one],
        out_specs=[row, row],
        out_shape=[_sds((n, D), F32), _sds((n, D), F32)],
        compiler_params=_params(("arbitrary",)),
        name=name,
    )(*parts, w, x, g, lng, lnb)


def postnorm_bwd(dxn, x, y, g, lng, lnb, w, ks, coef, seq, name, carry=()):
    n = x.shape[0]
    ni = n // TM
    tps = seq // TM
    nb = n // seq
    offs = [sum(ks[:k]) for k in range(len(ks))]
    npart = len(ks)
    f = functools.partial(_postnorm, coef=coef)
    nc = len(carry)
    gathers = [gt for _, gt in carry]
    c_in, c_out, c_shapes, c_scratch = _carry_plan(carry)

    def body(dxn_ref, x_ref, y_ref, g_ref, lng_ref, lnb_ref, w_ref, *rest):
        cx, rest = rest[:nc], rest[nc:]
        dx_ref, dy_ref = rest[:2]
        da_refs = rest[2 : 2 + npart]
        dg_ref, dlng_ref, dlnb_ref = rest[2 + npart : 5 + npart]
        co, sems = rest[5 + npart : 5 + npart + nc], rest[5 + npart + nc :]
        i = pl.program_id(0)
        if nc:
            @pl.when(i == 0)
            def _():
                _carry_start(gathers, cx, co, sems)

        @pl.when(i % tps == 0)
        def _():
            dg_ref[...] = jnp.zeros_like(dg_ref)

        @pl.when(i == 0)
        def _():
            dlng_ref[...] = jnp.zeros_like(dlng_ref)
            dlnb_ref[...] = jnp.zeros_like(dlnb_ref)

        for r0 in range(0, TM, SPLIT_ROWS):
            rows = slice(r0, r0 + SPLIT_ROWS)
            _, vjp = jax.vjp(f, x_ref[rows, :], y_ref[rows, :], g_ref[0], lng_ref[...], lnb_ref[...])
            dx, dy, dg, dlng, dlnb = vjp(dxn_ref[rows, :])
            dx_ref[rows, :] = dx
            dyb = dy.astype(BF16)
            dy_ref[rows, :] = dyb
            for da_ref, off, k in zip(da_refs, offs, ks):
                da_ref[rows, :] = lax.dot_general(
                    dyb, w_ref[off : off + k, :], (((1,), (1,)), ((), ())), preferred_element_type=F32
                ).astype(BF16)
            dg_ref[0] += dg
            dlng_ref[...] += dlng
            dlnb_ref[...] += dlnb
        if nc:
            @pl.when(i == ni - 1)
            def _():
                _carry_pass_on(gathers, cx, co, sems)
                _carry_wait(gathers, cx, co, sems)

    row = pl.BlockSpec((TM, D), lambda i: (i, 0))
    one = pl.BlockSpec((1, D), lambda i: (0, 0))
    vec = pl.BlockSpec((1, 1, D), lambda i: (i // tps, 0, 0))
    return pl.pallas_call(
        body,
        grid=(ni,),
        in_specs=[row, row, row, vec, one, one, pl.BlockSpec((sum(ks), D), lambda i: (0, 0))] + c_in,
        out_specs=[row, row] + [pl.BlockSpec((TM, k), lambda i: (i, 0)) for k in ks] + [vec, one, one] + c_out,
        out_shape=[_sds((n, D), F32), _sds((n, D), BF16)]
        + [_sds((n, k), BF16) for k in ks]
        + [_sds((nb, 1, D), F32), _sds((1, D), F32), _sds((1, D), F32)]
        + c_shapes,
        scratch_shapes=c_scratch,
        compiler_params=_params(("arbitrary",)),
        name=name,
    )(dxn, x, y, g, lng, lnb, w, *[a for a, _ in carry])


def _mod_bwd_finish(dh, x_ref, sc_ref, dxres_ref, dx_ref, dsh_ref, dsc_ref, first_of_seq):
    dx_ref[...] = dxres_ref[...] + dh * (1.0 + sc_ref[0])

    @pl.when(first_of_seq)
    def _():
        dsh_ref[...] = jnp.zeros_like(dsh_ref)
        dsc_ref[...] = jnp.zeros_like(dsc_ref)

    dsh_ref[0] += jnp.sum(dh, axis=0, keepdims=True)
    dsc_ref[0] += jnp.sum(dh * x_ref[...], axis=0, keepdims=True)


def ffn_bwd_in(da, g, u, w, x, sc, dxres, seq, name, carry=()):
    n = x.shape[0]
    tn = FF // 2
    nj = FF // tn
    tm = TM // 2
    ni = n // tm
    tps = seq // tm
    nb = n // seq
    k = len(carry)
    gathers = [gt for _, gt in carry]
    c_in, c_out, c_shapes, c_scratch = _carry_plan(carry)

    def body(da_ref, g_ref, u_ref, w_hbm, x_ref, sc_ref, dxres_ref, *rest):
        cx, (dgu_ref, dx_ref, dsh_ref, dsc_ref), co, (w_ref, w_sem, *sems) = rest[:k], rest[k : k + 4], rest[k + 4 : 2 * k + 4], rest[2 * k + 4 :]
        i = pl.program_id(0)

        @pl.when(i == 0)
        def _():
            if k:
                _carry_start(gathers, cx, co, sems)
            cp = pltpu.make_async_copy(w_hbm, w_ref, w_sem)
            cp.start()
            cp.wait()

        nt = (((1,), (1,)), ((), ()))
        dh = None
        for j in range(nj):
            ln = slice(j * tn, (j + 1) * tn)
            gv = g_ref[:, ln].astype(F32)
            uv = u_ref[:, ln].astype(F32)
            dav = da_ref[:, ln].astype(F32)
            s = jax.nn.sigmoid(gv)
            dgv = (dav * uv * s * (1.0 + gv * (1.0 - s))).astype(BF16)
            duv = (dav * gv * s).astype(BF16)
            dgu_ref[:, ln] = dgv
            dgu_ref[:, FF + j * tn : FF + (j + 1) * tn] = duv
            t = lax.dot_general(dgv, w_ref[:, j * tn : (j + 1) * tn], nt, preferred_element_type=F32) + lax.dot_general(
                duv, w_ref[:, FF + j * tn : FF + (j + 1) * tn], nt, preferred_element_type=F32
            )
            dh = t if dh is None else dh + t
        _mod_bwd_finish(dh, x_ref, sc_ref, dxres_ref, dx_ref, dsh_ref, dsc_ref, i % tps == 0)
        if k:
            @pl.when(i == ni - 1)
            def _():
                _carry_wait(gathers, cx, co, sems)

    row = pl.BlockSpec((tm, D), lambda i: (i, 0))
    col = pl.BlockSpec((tm, FF), lambda i: (i, 0))
    vec = pl.BlockSpec((1, 1, D), lambda i: (i // tps, 0, 0))
    return pl.pallas_call(
        body,
        grid=(ni,),
        in_specs=[col, col, col, pl.BlockSpec(memory_space=pltpu.HBM), row, vec, row] + c_in,
        out_specs=[pl.BlockSpec((tm, 2 * FF), lambda i: (i, 0)), row, vec, vec] + c_out,
        out_shape=[_sds((n, 2 * FF), BF16), _sds((n, D), F32), _sds((nb, 1, D), F32), _sds((nb, 1, D), F32)] + c_shapes,
        scratch_shapes=[pltpu.VMEM((D, 2 * FF), BF16), pltpu.SemaphoreType.DMA] + c_scratch,
        compiler_params=_params(("arbitrary",)),
        name=name,
    )(da, g, u, w, x, sc, dxres, *[a for a, _ in carry])


def proj_bwd_in(dparts, w, x, sc, dxres, seq, name, carry=()):
    n = x.shape[0]
    ni = n // TM
    tps = seq // TM
    nb = n // seq
    ms = [p.shape[1] for p in dparts]
    offs = [sum(ms[:k]) for k in range(len(ms))]
    npart = len(ms)
    nc = len(carry)
    gathers = [gt for _, gt in carry]
    c_in, c_out, c_shapes, c_scratch = _carry_plan(carry)

    def body(*refs):
        d_refs = refs[:npart]
        w_ref, x_ref, sc_ref, dxres_ref = refs[npart : npart + 4]
        cx = refs[npart + 4 : npart + 4 + nc]
        dx_ref, dsh_ref, dsc_ref = refs[npart + 4 + nc : npart + 7 + nc]
        co, sems = refs[npart + 7 + nc : npart + 7 + 2 * nc], refs[npart + 7 + 2 * nc :]
        i = pl.program_id(0)
        if nc:
            @pl.when(i == 0)
            def _():
                _carry_start(gathers, cx, co, sems)

        dh = None
        for d_ref, off, m in zip(d_refs, offs, ms):
            t = lax.dot_general(d_ref[...], w_ref[:, off : off + m], (((1,), (1,)), ((), ())), preferred_element_type=F32)
            dh = t if dh is None else dh + t
        _mod_bwd_finish(dh, x_ref, sc_ref, dxres_ref, dx_ref, dsh_ref, dsc_ref, i % tps == 0)
        if nc:
            @pl.when(i == ni - 1)
            def _():
                _carry_pass_on(gathers, cx, co, sems)
                _carry_wait(gathers, cx, co, sems)

    row = pl.BlockSpec((TM, D), lambda i: (i, 0))
    vec = pl.BlockSpec((1, 1, D), lambda i: (i // tps, 0, 0))
    return pl.pallas_call(
        body,
        grid=(ni,),
        in_specs=[pl.BlockSpec((TM, m), lambda i: (i, 0)) for m in ms] + [pl.BlockSpec((D, sum(ms)), lambda i: (0, 0)), row, vec, row] + c_in,
        out_specs=[row, vec, vec] + c_out,
        out_shape=[_sds((n, D), F32), _sds((nb, 1, D), F32), _sds((nb, 1, D), F32)] + c_shapes,
        scratch_shapes=c_scratch,
        compiler_params=_params(("arbitrary",)),
        name=name,
    )(*dparts, w, x, sc, dxres, *[a for a, _ in carry])


def mm_tn(a, b, name):
    n, k1 = a.shape
    k2 = b.shape[1]
    t1 = k1 if k1 <= 1536 else _tile(k1, 1536)
    t2 = k2 if k2 <= 1536 else _tile(k2, 1536)
    tk = 1024 if n % 1024 == 0 else n
    nk = n // tk

    def body(a_ref, b_ref, o_ref, acc):
        t = lax.dot_general(a_ref[...], b_ref[...], (((0,), (0,)), ((), ())), preferred_element_type=F32)

        @pl.when(pl.program_id(2) == 0)
        def _():
            acc[...] = t

        @pl.when(pl.program_id(2) > 0)
        def _():
            acc[...] += t

        @pl.when(pl.program_id(2) == nk - 1)
        def _():
            o_ref[...] = acc[...].astype(BF16)

    return pl.pallas_call(
        body,
        grid=(k1 // t1, k2 // t2, nk),
        in_specs=[pl.BlockSpec((tk, t1), lambda i, j, k: (k, i)), pl.BlockSpec((tk, t2), lambda i, j, k: (k, j))],
        out_specs=pl.BlockSpec((t1, t2), lambda i, j, k: (i, j)),
        out_shape=_sds((k1, k2), BF16),
        scratch_shapes=[pltpu.VMEM((t1, t2), F32)],
        compiler_params=_params(("arbitrary", "arbitrary", "arbitrary")),
        name=name,
    )(a, b)


def _tile(n, cap):
    best = LANE
    for t in range(LANE, cap + 1, LANE):
        if n % t == 0:
            best = t
    return best


def _lane_tiles(ref):
    return [slice(c0, c0 + LANE) for c0 in range(0, ref.shape[1], LANE)]


def _conv_taps(ext_ref, w_ref, halo, tq, kk):
    out = []
    for ln in _lane_tiles(w_ref):
        acc = None
        for k in range(kk):
            t = w_ref[k : k + 1, ln] * ext_ref[pl.ds(halo - (kk - 1 - k), tq), ln]
            acc = t if acc is None else acc + t
        out.append(acc)
    return jnp.concatenate(out, axis=1)


def _conv_taps_t(ext2_ref, w_ref, tq, kk):
    out = []
    for ln in _lane_tiles(w_ref):
        acc = None
        for k in range(kk):
            t = w_ref[k : k + 1, ln] * ext2_ref[pl.ds(kk - 1 - k, tq), ln]
            acc = t if acc is None else acc + t
        out.append(acc)
    return jnp.concatenate(out, axis=1)


def _conv_dw(ext_ref, ext2_ref, dw_ref, halo, tq, kk):
    for ln in _lane_tiles(dw_ref):
        dy = ext2_ref[pl.ds(0, tq), ln]
        for k in range(kk):
            dw_ref[k : k + 1, ln] += jnp.sum(dy * ext_ref[pl.ds(halo - (kk - 1 - k), tq), ln], axis=0, keepdims=True)


def _halo_spec(rows, width, tq, shift):
    per = tq // rows

    if shift < 0:
        return lambda nblocks: pl.BlockSpec((rows, width), lambda i: (jnp.maximum(i * per - 1, 0), 0))
    return lambda nblocks: pl.BlockSpec((rows, width), lambda i: (jnp.minimum((i + 1) * per, nblocks - 1), 0))


def _ln(c, g, b):
    mu = jnp.mean(c, axis=-1, keepdims=True)
    cc = c - mu
    var = jnp.mean(cc * cc, axis=-1, keepdims=True)
    return cc * lax.rsqrt(var + EPS) * g + b


def _silu(v):
    return v * jax.nn.sigmoid(v)


@jax.custom_vjp
def _expand(v, e):
    hi = v.astype(BF16)
    r1 = v - hi.astype(F32)
    mid = r1.astype(BF16)
    lo = (r1 - mid.astype(F32)).astype(BF16)
    return (jnp.dot(hi, e, preferred_element_type=F32) + jnp.dot(mid, e, preferred_element_type=F32)
            + jnp.dot(lo, e, preferred_element_type=F32))


def _expand_fwd(v, e):
    return _expand(v, e), e


def _expand_bwd(e, g):
    nt = (((1,), (1,)), ((), ()))
    hi = g.astype(BF16)
    mid = (g - hi.astype(F32)).astype(BF16)
    dv = lax.dot_general(hi, e, nt, preferred_element_type=F32) + lax.dot_general(mid, e, nt, preferred_element_type=F32)
    return dv, jnp.zeros_like(e)


_expand.defvjp(_expand_fwd, _expand_bwd)


def _ssd_chunk(conv, dtr, z, s, dt_bias, a_log, dskip, norm_g, tril, e):
    q = SSD_Q
    act = _silu(conv)
    xs, bm, cm = act[:, :1024], act[:, 1024:1280], act[:, 1280:1536]
    lane = lax.broadcasted_iota(jnp.int32, (1, LANE), 1)
    lane_q = lax.broadcasted_iota(jnp.int32, (q, LANE), 1)
    sub_q = lax.broadcasted_iota(jnp.int32, (LANE, q), 0)
    causal = lax.broadcasted_iota(jnp.int32, (q, q), 0) >= lax.broadcasted_iota(jnp.int32, (q, q), 1)
    real = lane < SSD_HEADS
    dt = jnp.where(real, jax.nn.softplus(dtr + dt_bias), 0.0)
    a = jnp.where(real, -jnp.exp(a_log), 0.0)
    da = dt * a
    acs = jnp.dot(tril, da, precision=HI, preferred_element_type=F32)
    acs_t = lax.dot_general(da, tril, (((0,), (1,)), ((), ())), precision=HI, preferred_element_type=F32)
    dt_e = _expand(dt, e)
    acs_e = _expand(acs, e)
    one8 = jnp.ones((SUB, 1), F32)
    alast_e = _expand(one8 * jnp.sum(da, axis=0, keepdims=True), e)[0:1]
    d_e = _expand(one8 * jnp.where(real, dskip, 0.0), e)[0:1]
    xdt = xs * dt_e
    nt = (((1,), (1,)), ((), ()))
    tn = (((0,), (0,)), ((), ()))
    ys, snews = [], []
    for g in range(2):
        gl = slice(g * 512, (g + 1) * 512)
        bg = bm[:, g * 128 : (g + 1) * 128].astype(BF16)
        cg = cm[:, g * 128 : (g + 1) * 128].astype(BF16)
        cb = lax.dot_general(cg, bg, nt, preferred_element_type=F32)
        sg = s[:, gl]
        yoff = jnp.dot(cg, sg.astype(BF16), preferred_element_type=F32) * jnp.exp(acs_e[:, gl])
        pairs = []
        for j in range(4):
            xp = xdt[:, g * 512 + j * 128 : g * 512 + (j + 1) * 128].astype(BF16)
            outs = []
            for hh in (g * 8 + 2 * j, g * 8 + 2 * j + 1):
                col = jnp.sum(jnp.where(lane_q == hh, acs, 0.0), axis=1, keepdims=True)
                row = jnp.sum(jnp.where(sub_q == hh, acs_t, 0.0), axis=0, keepdims=True)
                m = cb * jnp.exp(jnp.where(causal, col - row, -1e30))
                outs.append(jnp.dot(m.astype(BF16), xp, preferred_element_type=F32))
            pairs.append(jnp.where(lane_q < SSD_P, outs[0], outs[1]))
        ys.append(jnp.concatenate(pairs, axis=1) + yoff)
        decay = jnp.exp(alast_e[:, gl] - acs_e[:, gl])
        snews.append(
            sg * jnp.exp(alast_e[:, gl]) + lax.dot_general(bg, (xdt[:, gl] * decay).astype(BF16), tn, preferred_element_type=F32)
        )
    y = jnp.concatenate(ys, axis=1) + xs * d_e
    gated = y * _silu(z)
    out = gated * lax.rsqrt(jnp.mean(gated * gated, axis=-1, keepdims=True) + EPS) * norm_g
    return out, jnp.concatenate(snews, axis=1)


def _ssd_consts():
    tril = (lax.broadcasted_iota(jnp.int32, (SSD_Q, SSD_Q), 0) >= lax.broadcasted_iota(jnp.int32, (SSD_Q, SSD_Q), 1)).astype(F32)
    e = (lax.broadcasted_iota(jnp.int32, (LANE, 1024), 0) == lax.broadcasted_iota(jnp.int32, (LANE, 1024), 1) // SSD_P).astype(BF16)
    return tril, e


def ssd_fwd(z, xbc, dtr, cw, cb, dt_bias, a_log, dskip, norm_g, seq, name):
    n = z.shape[0]
    q = SSD_Q
    nc = seq // q
    tril, e = _ssd_consts()

    def body(z_ref, xbc_ref, halo_ref, dtr_ref, cw_ref, cb_ref, dtb_ref, alog_ref, dsk_ref, ng_ref, tril_ref, e_ref, y_ref, sprev_ref, s_scr, ext):
        c = pl.program_id(0) % nc

        @pl.when(c == 0)
        def _():
            s_scr[...] = jnp.zeros_like(s_scr)

        ext[0:SUB, :] = jnp.where(c == 0, 0.0, halo_ref[...])
        ext[SUB:, :] = xbc_ref[...]
        conv = _conv_taps(ext, cw_ref, SUB, q, 4) + cb_ref[...]
        sprev_ref[0] = s_scr[...]
        y, snew = _ssd_chunk(conv, dtr_ref[...], z_ref[...], s_scr[...], dtb_ref[...], alog_ref[...], dsk_ref[...], ng_ref[...], tril_ref[...], e_ref[...])
        y_ref[...] = y.astype(BF16)
        s_scr[...] = snew

    def full(shape):
        return pl.BlockSpec(shape, lambda i: (0,) * len(shape))

    return pl.pallas_call(
        body,
        grid=(n // q,),
        in_specs=[
            pl.BlockSpec((q, 1024), lambda i: (i, 0)),
            pl.BlockSpec((q, SSD_XBC), lambda i: (i, 0)),
            _halo_spec(SUB, SSD_XBC, q, -1)(n // SUB),
            pl.BlockSpec((q, LANE), lambda i: (i, 0)),
            full((4, SSD_XBC)),
            full((1, SSD_XBC)),
            full((1, LANE)),
            full((1, LANE)),
            full((1, LANE)),
            full((1, 1024)),
            full((q, q)),
            full((LANE, 1024)),
        ],
        out_specs=[pl.BlockSpec((q, 1024), lambda i: (i, 0)), pl.BlockSpec((1, LANE, 1024), lambda i: (i, 0, 0))],
        out_shape=[_sds((n, 1024), BF16), _sds((n // q, LANE, 1024), F32)],
        scratch_shapes=[pltpu.VMEM((LANE, 1024), F32), pltpu.VMEM((SUB + q, SSD_XBC), F32)],
        compiler_params=_params(("arbitrary",)),
        name=name,
    )(z, xbc, xbc, dtr, cw, cb, dt_bias, a_log, dskip, norm_g, tril, e)


def ssd_bwd(dy, z, xbc, dtr, sprev, cw, cb, dt_bias, a_log, dskip, norm_g, seq, name):
    n = z.shape[0]
    q = SSD_Q
    nc = seq // q
    nchunks = n // q
    tril, e = _ssd_consts()

    def rev(i):
        return (i // nc) * nc + (nc - 1 - i % nc)

    def body(dy_ref, z_ref, xbc_ref, halo_ref, dtr_ref, sprev_ref, cw_ref, cb_ref, dtb_ref, alog_ref, dsk_ref, ng_ref, tril_ref, e_ref,
             dz_ref, dxbc_ref, ddt_ref, dcw_ref, dcb_ref, ddtb_ref, dalog_ref, ddsk_ref, dng_ref, ds_scr, ext, ext2):
        i = pl.program_id(0)
        step = i % nc
        c = nc - 1 - step

        @pl.when(step == 0)
        def _():
            ds_scr[...] = jnp.zeros_like(ds_scr)
            ext2[q:, :] = jnp.zeros((SUB, SSD_XBC), F32)

        @pl.when(i == 0)
        def _():
            for r in (dcw_ref, dcb_ref, ddtb_ref, dalog_ref, ddsk_ref, dng_ref):
                r[...] = jnp.zeros_like(r)

        ext[0:SUB, :] = jnp.where(c == 0, 0.0, halo_ref[...])
        ext[SUB:, :] = xbc_ref[...]
        conv = _conv_taps(ext, cw_ref, SUB, q, 4) + cb_ref[...]
        tril_v, e_v = tril_ref[...], e_ref[...]

        def f(conv, dtr, z, s, dtb, alog, dsk, ng):
            return _ssd_chunk(conv, dtr, z, s, dtb, alog, dsk, ng, tril_v, e_v)

        _, vjp = jax.vjp(f, conv, dtr_ref[...], z_ref[...], sprev_ref[0], dtb_ref[...], alog_ref[...], dsk_ref[...], ng_ref[...])
        dconv, ddtr, dz, dsprev, ddtb, dalog, ddsk, dng = vjp((dy_ref[...].astype(F32), ds_scr[...]))
        ds_scr[...] = dsprev
        dz_ref[...] = dz.astype(BF16)
        ddt_ref[...] = ddtr.astype(BF16)
        ext2[0:q, :] = dconv
        dxbc_ref[...] = _conv_taps_t(ext2, cw_ref, q, 4).astype(BF16)
        ext2[q:, :] = dconv[0:SUB, :]
        _conv_dw(ext, ext2, dcw_ref, SUB, q, 4)
        dcb_ref[...] += jnp.sum(dconv, axis=0, keepdims=True)
        ddtb_ref[...] += ddtb
        dalog_ref[...] += dalog
        ddsk_ref[...] += ddsk
        dng_ref[...] += dng

    def full(shape):
        return pl.BlockSpec(shape, lambda i: (0,) * len(shape))

    per = q // SUB
    return pl.pallas_call(
        body,
        grid=(nchunks,),
        in_specs=[
            pl.BlockSpec((q, 1024), lambda i: (rev(i), 0)),
            pl.BlockSpec((q, 1024), lambda i: (rev(i), 0)),
            pl.BlockSpec((q, SSD_XBC), lambda i: (rev(i), 0)),
            pl.BlockSpec((SUB, SSD_XBC), lambda i: (jnp.maximum(rev(i) * per - 1, 0), 0)),
            pl.BlockSpec((q, LANE), lambda i: (rev(i), 0)),
            pl.BlockSpec((1, LANE, 1024), lambda i: (rev(i), 0, 0)),
            full((4, SSD_XBC)),
            full((1, SSD_XBC)),
            full((1, LANE)),
            full((1, LANE)),
            full((1, LANE)),
            full((1, 1024)),
            full((q, q)),
            full((LANE, 1024)),
        ],
        out_specs=[
            pl.BlockSpec((q, 1024), lambda i: (rev(i), 0)),
            pl.BlockSpec((q, SSD_XBC), lambda i: (rev(i), 0)),
            pl.BlockSpec((q, LANE), lambda i: (rev(i), 0)),
            full((4, SSD_XBC)),
            full((1, SSD_XBC)),
            full((1, LANE)),
            full((1, LANE)),
            full((1, LANE)),
            full((1, 1024)),
        ],
        out_shape=[
            _sds((n, 1024), BF16),
            _sds((n, SSD_XBC), BF16),
            _sds((n, LANE), BF16),
            _sds((4, SSD_XBC), F32),
            _sds((1, SSD_XBC), F32),
            _sds((1, LANE), F32),
            _sds((1, LANE), F32),
            _sds((1, LANE), F32),
            _sds((1, 1024), F32),
        ],
        scratch_shapes=[pltpu.VMEM((LANE, 1024), F32), pltpu.VMEM((SUB + q, SSD_XBC), F32), pltpu.VMEM((q + SUB, SSD_XBC), F32)],
        compiler_params=_params(("arbitrary",)),
        name=name,
    )(dy, z, xbc, xbc, dtr, sprev, cw, cb, dt_bias, a_log, dskip, norm_g, tril, e)


POOL_HALO = 16
TQ = 512


def _pool_count(pos, w):
    return jnp.minimum(pos + 1.0, float(w))


def pool_fwd(u, pw, scale, seq, name):
    n = u.shape[0]
    tq, halo = TQ, POOL_HALO
    tps = seq // tq

    def body(u_ref, halo_ref, pw_ref, sc_ref, y_ref, ext):
        t0 = pl.program_id(0) % tps
        ext[0:halo, :] = jnp.where(t0 == 0, 0.0, halo_ref[...])
        ext[halo:, :] = u_ref[...]
        pos = (t0 * tq + lax.broadcasted_iota(jnp.int32, (tq, 1), 0)).astype(F32)
        for g, w in enumerate(POOL_WINDOWS):
            ln = slice(g * LANE, (g + 1) * LANE)
            acc = ext[pl.ds(halo, tq), ln]
            for j in range(1, w):
                acc = acc + ext[pl.ds(halo - j, tq), ln]
            pooled = acc / _pool_count(pos, w) - u_ref[:, ln]
            mixed = jnp.dot(pooled.astype(BF16), pw_ref[g].astype(BF16), preferred_element_type=F32)
            y_ref[:, ln] = (mixed * sc_ref[:, ln]).astype(BF16)

    return pl.pallas_call(
        body,
        grid=(n // tq,),
        in_specs=[
            pl.BlockSpec((tq, POOL_DIM), lambda i: (i, 0)),
            _halo_spec(halo, POOL_DIM, tq, -1)(n // halo),
            pl.BlockSpec((4, LANE, LANE), lambda i: (0, 0, 0)),
            pl.BlockSpec((1, POOL_DIM), lambda i: (0, 0)),
        ],
        out_specs=pl.BlockSpec((tq, POOL_DIM), lambda i: (i, 0)),
        out_shape=_sds((n, POOL_DIM), BF16),
        scratch_shapes=[pltpu.VMEM((halo + tq, POOL_DIM), F32)],
        compiler_params=_params(("arbitrary",)),
        name=name,
    )(u, u, pw, scale)


def pool_bwd(dy, u, pw, scale, seq, name):
    n = u.shape[0]
    tq, halo = TQ, POOL_HALO
    tps = seq // tq
    nt = (((1,), (1,)), ((), ()))
    tn = (((0,), (0,)), ((), ()))

    def body(dy_ref, dyn_ref, u_ref, halo_ref, pw_ref, sc_ref, du_ref, dpw_ref, dsc_ref, ext, ext2):
        i = pl.program_id(0)
        t0 = i % tps

        @pl.when(i == 0)
        def _():
            dpw_ref[...] = jnp.zeros_like(dpw_ref)
            dsc_ref[...] = jnp.zeros_like(dsc_ref)

        ext[0:halo, :] = jnp.where(t0 == 0, 0.0, halo_ref[...])
        ext[halo:, :] = u_ref[...]
        pos = (t0 * tq + lax.broadcasted_iota(jnp.int32, (tq, 1), 0)).astype(F32)
        dyv = dy_ref[...].astype(F32)
        dynv = jnp.where(t0 == tps - 1, 0.0, dyn_ref[...].astype(F32))
        for g, w in enumerate(POOL_WINDOWS):
            ln = slice(g * LANE, (g + 1) * LANE)
            wg = pw_ref[g].astype(BF16)
            acc = ext[pl.ds(halo, tq), ln]
            for j in range(1, w):
                acc = acc + ext[pl.ds(halo - j, tq), ln]
            pooled = (acc / _pool_count(pos, w) - u_ref[:, ln]).astype(BF16)
            mixed = jnp.dot(pooled, wg, preferred_element_type=F32)
            dsc_ref[:, ln] += jnp.sum(dyv[:, ln] * mixed, axis=0, keepdims=True)
            dmix = (dyv[:, ln] * sc_ref[:, ln]).astype(BF16)
            dpw_ref[g] += lax.dot_general(pooled, dmix, tn, preferred_element_type=F32)
            dpool = lax.dot_general(dmix, wg, nt, preferred_element_type=F32)
            dmix_n = (dynv[:, ln] * sc_ref[:, ln]).astype(BF16)
            dpool_n = lax.dot_general(dmix_n, wg, nt, preferred_element_type=F32)
            ext2[0:tq, ln] = dpool / _pool_count(pos, w)
            ext2[tq:, ln] = dpool_n * (1.0 / w)
            acc2 = ext2[pl.ds(0, tq), ln]
            for j in range(1, w):
                acc2 = acc2 + ext2[pl.ds(j, tq), ln]
            du_ref[:, ln] = (acc2 - dpool).astype(BF16)

    return pl.pallas_call(
        body,
        grid=(n // tq,),
        in_specs=[
            pl.BlockSpec((tq, POOL_DIM), lambda i: (i, 0)),
            _halo_spec(halo, POOL_DIM, tq, +1)(n // halo),
            pl.BlockSpec((tq, POOL_DIM), lambda i: (i, 0)),
            _halo_spec(halo, POOL_DIM, tq, -1)(n // halo),
            pl.BlockSpec((4, LANE, LANE), lambda i: (0, 0, 0)),
            pl.BlockSpec((1, POOL_DIM), lambda i: (0, 0)),
        ],
        out_specs=[
            pl.BlockSpec((tq, POOL_DIM), lambda i: (i, 0)),
            pl.BlockSpec((4, LANE, LANE), lambda i: (0, 0, 0)),
            pl.BlockSpec((1, POOL_DIM), lambda i: (0, 0)),
        ],
        out_shape=[_sds((n, POOL_DIM), BF16), _sds((4, LANE, LANE), F32), _sds((1, POOL_DIM), F32)],
        scratch_shapes=[pltpu.VMEM((halo + tq, POOL_DIM), F32), pltpu.VMEM((tq + halo, POOL_DIM), F32)],
        compiler_params=_params(("arbitrary",)),
        name=name,
    )(dy, dy, u, u, pw, scale)


CONF_HALO = 32
TQC = 256


def _conf_post(c, g, b):
    return _silu(_ln(c, g, b))


def conf_fwd(vg, w, b, lng, lnb, seq, name):
    n = vg.shape[0]
    tq, halo, kk = TQC, CONF_HALO, CONF_K
    tps = seq // tq
    c = CONF_DIM

    def body(vg_ref, halo_ref, w_ref, b_ref, lng_ref, lnb_ref, y_ref, conv_ref, ext):
        t0 = pl.program_id(0) % tps
        hv = halo_ref[...]
        ext[0:halo, :] = jnp.where(t0 == 0, 0.0, hv[:, :c] * jax.nn.sigmoid(hv[:, c:]))
        ext[halo:, :] = vg_ref[:, :c] * jax.nn.sigmoid(vg_ref[:, c:])
        conv = _conv_taps(ext, w_ref, halo, tq, kk) + b_ref[...]
        conv_ref[...] = conv
        y_ref[...] = _conf_post(conv, lng_ref[...], lnb_ref[...]).astype(BF16)

    one = pl.BlockSpec((1, c), lambda i: (0, 0))
    return pl.pallas_call(
        body,
        grid=(n // tq,),
        in_specs=[pl.BlockSpec((tq, 2 * c), lambda i: (i, 0)), _halo_spec(halo, 2 * c, tq, -1)(n // halo), pl.BlockSpec((kk, c), lambda i: (0, 0)), one, one, one],
        out_specs=[pl.BlockSpec((tq, c), lambda i: (i, 0)), pl.BlockSpec((tq, c), lambda i: (i, 0))],
        out_shape=[_sds((n, c), BF16), _sds((n, c), F32)],
        scratch_shapes=[pltpu.VMEM((halo + tq, c), F32)],
        compiler_params=_params(("arbitrary",)),
        name=name,
    )(vg, vg, w, b, lng, lnb)


def conf_bwd(dy, conv, vg, w, lng, lnb, seq, name):
    n = vg.shape[0]
    tq, halo, kk = TQC, CONF_HALO, CONF_K
    tps = seq // tq
    c = CONF_DIM

    def body(dy_ref, dyn_ref, conv_ref, convn_ref, vg_ref, halo_ref, w_ref, lng_ref, lnb_ref, dvg_ref, dw_ref, db_ref, dlng_ref, dlnb_ref, ext, ext2):
        i = pl.program_id(0)
        t0 = i % tps

        @pl.when(i == 0)
        def _():
            for r in (dw_ref, db_ref, dlng_ref, dlnb_ref):
                r[...] = jnp.zeros_like(r)

        _, vjp = jax.vjp(_conf_post, conv_ref[...], lng_ref[...], lnb_ref[...])
        dconv, dlng, dlnb = vjp(dy_ref[...].astype(F32))
        _, vjpn = jax.vjp(_conf_post, convn_ref[...], lng_ref[...], lnb_ref[...])
        dconv_n = vjpn(dyn_ref[...].astype(F32))[0]
        ext2[0:tq, :] = dconv
        ext2[tq:, :] = jnp.where(t0 == tps - 1, 0.0, dconv_n)
        dh = _conv_taps_t(ext2, w_ref, tq, kk)
        hv = halo_ref[...]
        ext[0:halo, :] = jnp.where(t0 == 0, 0.0, hv[:, :c] * jax.nn.sigmoid(hv[:, c:]))
        v = vg_ref[:, :c]
        s = jax.nn.sigmoid(vg_ref[:, c:])
        ext[halo:, :] = v * s
        _conv_dw(ext, ext2, dw_ref, halo, tq, kk)
        db_ref[...] += jnp.sum(dconv, axis=0, keepdims=True)
        dlng_ref[...] += dlng
        dlnb_ref[...] += dlnb
        dvg_ref[:, :c] = (dh * s).astype(BF16)
        dvg_ref[:, c:] = (dh * v * s * (1.0 - s)).astype(BF16)

    one = pl.BlockSpec((1, c), lambda i: (0, 0))
    tile = pl.BlockSpec((tq, c), lambda i: (i, 0))
    nxt = _halo_spec(halo, c, tq, +1)(n // halo)
    return pl.pallas_call(
        body,
        grid=(n // tq,),
        in_specs=[tile, nxt, tile, nxt, pl.BlockSpec((tq, 2 * c), lambda i: (i, 0)), _halo_spec(halo, 2 * c, tq, -1)(n // halo),
                  pl.BlockSpec((kk, c), lambda i: (0, 0)), one, one],
        out_specs=[pl.BlockSpec((tq, 2 * c), lambda i: (i, 0)), pl.BlockSpec((kk, c), lambda i: (0, 0)), one, one, one],
        out_shape=[_sds((n, 2 * c), BF16), _sds((kk, c), F32), _sds((1, c), F32), _sds((1, c), F32), _sds((1, c), F32)],
        scratch_shapes=[pltpu.VMEM((halo + tq, c), F32), pltpu.VMEM((tq + halo, c), F32)],
        compiler_params=_params(("arbitrary",)),
        name=name,
    )(dy, dy, conv, conv, vg, vg, w, lng, lnb)


TL = 256


def _expm1_neg(t):
    p = t * (1.0 + t * (1.0 / 2 + t * (1.0 / 6 + t * (1.0 / 24 + t * (1.0 / 120 + t * (1.0 / 720 + t * (1.0 / 5040)))))))
    return jnp.where(t > -0.35, p, jnp.exp(t) - 1.0)


def _lru_gate(xc, ra, ia, ba, bx, lam):
    r = jax.nn.sigmoid(ra + ba)
    i = jax.nn.sigmoid(ia + bx)
    log_a = -LRU_C * r * jax.nn.softplus(-lam)
    return jnp.exp(log_a), jnp.sqrt(-_expm1_neg(2.0 * log_a)) * (i * xc)


def _lru_out(h, gr):
    return h * jax.nn.gelu(gr)


def _scan_rows(a, b, tq, reverse):
    r8 = lax.broadcasted_iota(jnp.int32, (tq, 1), 0) % SUB
    for d in (1, 2, 4):
        sh = tq - d if reverse else d
        valid = (r8 < SUB - d) if reverse else (r8 >= d)
        a_s = pltpu.roll(a, sh, 0)
        b_s = pltpu.roll(b, sh, 0)
        b = jnp.where(valid, a * b_s, 0.0) + b
        a = jnp.where(valid, a * a_s, a)
    ng = tq // SUB
    edge = 0 if reverse else SUB - 1
    out_a, out_b = [None] * ng, [None] * ng
    ca = cb = None
    for g in (reversed(range(ng)) if reverse else range(ng)):
        ag, bg = a[g * SUB : (g + 1) * SUB, :], b[g * SUB : (g + 1) * SUB, :]
        if ca is not None:
            bg = bg + ag * cb
            ag = ag * ca
        out_a[g], out_b[g] = ag, bg
        ca, cb = ag[edge : edge + 1, :], bg[edge : edge + 1, :]
    return jnp.concatenate(out_a, axis=0), jnp.concatenate(out_b, axis=0)


def _row_of(v, r, tq):
    row = lax.broadcasted_iota(jnp.int32, (tq, 1), 0)
    return jnp.sum(jnp.where(row == r, v, 0.0), axis=0, keepdims=True)


def _head_mm(xc, w_ref):
    return jnp.concatenate(
        [
            jnp.dot(xc[:, h * LANE : (h + 1) * LANE].astype(BF16), w_ref[h].astype(BF16), preferred_element_type=F32)
            for h in range(LRU_HEADS)
        ],
        axis=1,
    )


def lru_fwd(xr, gr, cw, cb, wa, ba, wx, bx, lam, seq, name):
    n = xr.shape[0]
    tq = TL
    tps = seq // tq
    c = LRU_DIM

    def body(xr_ref, halo_ref, gr_ref, cw_ref, cb_ref, wa_ref, ba_ref, wx_ref, bx_ref, lam_ref, y_ref, h_ref, hc, ext):
        t0 = pl.program_id(0) % tps

        @pl.when(t0 == 0)
        def _():
            hc[...] = jnp.zeros_like(hc)

        ext[0:SUB, :] = jnp.where(t0 == 0, 0.0, halo_ref[...])
        ext[SUB:, :] = xr_ref[...]
        xc = _conv_taps(ext, cw_ref, SUB, tq, 4) + cb_ref[...]
        a, b = _lru_gate(xc, _head_mm(xc, wa_ref), _head_mm(xc, wx_ref), ba_ref[...], bx_ref[...], lam_ref[...])
        acum, h0 = _scan_rows(a, b, tq, False)
        h = h0 + acum * hc[0:1, :]
        h_ref[...] = h
        hc[0:1, :] = h_ref[tq - 1 : tq, :]
        y_ref[...] = _lru_out(h, gr_ref[...]).astype(BF16)

    one = pl.BlockSpec((1, c), lambda i: (0, 0))
    tile = pl.BlockSpec((tq, c), lambda i: (i, 0))
    hw = pl.BlockSpec((LRU_HEADS, LANE, LANE), lambda i: (0, 0, 0))
    return pl.pallas_call(
        body,
        grid=(n // tq,),
        in_specs=[tile, _halo_spec(SUB, c, tq, -1)(n // SUB), tile, pl.BlockSpec((4, c), lambda i: (0, 0)), one, hw, one, hw, one, one],
        out_specs=[tile, tile],
        out_shape=[_sds((n, c), BF16), _sds((n, c), F32)],
        scratch_shapes=[pltpu.VMEM((SUB, c), F32), pltpu.VMEM((SUB + tq, c), F32)],
        compiler_params=_params(("arbitrary",)),
        name=name,
    )(xr, xr, gr, cw, cb, wa, ba, wx, bx, lam)


def lru_bwd(dy, xr, gr, h, cw, cb, wa, ba, wx, bx, lam, seq, name):
    n = xr.shape[0]
    tq = TL
    tps = seq // tq
    ntile = n // tq
    c = LRU_DIM
    per = tq // SUB
    nt = (((1,), (1,)), ((), ()))
    tn = (((0,), (0,)), ((), ()))

    def rev(i):
        return (i // tps) * tps + (tps - 1 - i % tps)

    def body(dy_ref, xr_ref, halo_ref, gr_ref, h_ref, hprev_ref, cw_ref, cb_ref, wa_ref, ba_ref, wx_ref, bx_ref, lam_ref,
             dxr_ref, dgr_ref, dcw_ref, dcb_ref, dwa_ref, dba_ref, dwx_ref, dbx_ref, dlam_ref, carry, ext, ext2):
        i = pl.program_id(0)
        step = i % tps
        t0 = tps - 1 - step

        @pl.when(step == 0)
        def _():
            carry[...] = jnp.zeros_like(carry)
            ext2[tq:, :] = jnp.zeros((SUB, c), F32)

        @pl.when(i == 0)
        def _():
            for r in (dcw_ref, dcb_ref, dwa_ref, dba_ref, dwx_ref, dbx_ref, dlam_ref):
                r[...] = jnp.zeros_like(r)

        ext[0:SUB, :] = jnp.where(t0 == 0, 0.0, halo_ref[...])
        ext[SUB:, :] = xr_ref[...]
        xc = _conv_taps(ext, cw_ref, SUB, tq, 4) + cb_ref[...]
        (a, _), vjp_gate = jax.vjp(_lru_gate, xc, _head_mm(xc, wa_ref), _head_mm(xc, wx_ref), ba_ref[...], bx_ref[...], lam_ref[...])
        hv = h_ref[...]
        _, vjp_out = jax.vjp(_lru_out, hv, gr_ref[...])
        dh, dgr = vjp_out(dy_ref[...].astype(F32))
        dgr_ref[...] = dgr.astype(BF16)
        row = lax.broadcasted_iota(jnp.int32, (tq, 1), 0)
        a_up = jnp.where(row == tq - 1, carry[0:1, :], pltpu.roll(a, tq - 1, 0))
        acum, l0 = _scan_rows(a_up, dh, tq, True)
        lamv = l0 + acum * carry[1:2, :]
        carry[0:1, :] = _row_of(a, 0, tq)
        carry[1:2, :] = _row_of(lamv, 0, tq)
        hprev = jnp.where(row == 0, jnp.where(t0 == 0, 0.0, hprev_ref[SUB - 1 : SUB, :]), pltpu.roll(hv, 1, 0))
        dxc, dra, dia, dba, dbx, dlam = vjp_gate((lamv * hprev, lamv))
        dba_ref[...] += dba
        dbx_ref[...] += dbx
        dlam_ref[...] += dlam
        pieces = []
        for hh in range(LRU_HEADS):
            ln = slice(hh * LANE, (hh + 1) * LANE)
            xh = xc[:, ln].astype(BF16)
            drh = dra[:, ln].astype(BF16)
            dih = dia[:, ln].astype(BF16)
            dwa_ref[hh] += lax.dot_general(xh, drh, tn, preferred_element_type=F32)
            dwx_ref[hh] += lax.dot_general(xh, dih, tn, preferred_element_type=F32)
            pieces.append(
                lax.dot_general(drh, wa_ref[hh].astype(BF16), nt, preferred_element_type=F32)
                + lax.dot_general(dih, wx_ref[hh].astype(BF16), nt, preferred_element_type=F32)
            )
        dxc = dxc + jnp.concatenate(pieces, axis=1)
        ext2[0:tq, :] = dxc
        dxr_ref[...] = _conv_taps_t(ext2, cw_ref, tq, 4).astype(BF16)
        ext2[tq:, :] = ext2[0:SUB, :]
        _conv_dw(ext, ext2, dcw_ref, SUB, tq, 4)
        dcb_ref[...] += jnp.sum(dxc, axis=0, keepdims=True)

    one = pl.BlockSpec((1, c), lambda i: (0, 0))
    tile = pl.BlockSpec((tq, c), lambda i: (rev(i), 0))
    prev = pl.BlockSpec((SUB, c), lambda i: (jnp.maximum(rev(i) * per - 1, 0), 0))
    hw = pl.BlockSpec((LRU_HEADS, LANE, LANE), lambda i: (0, 0, 0))
    cw4 = pl.BlockSpec((4, c), lambda i: (0, 0))
    return pl.pallas_call(
        body,
        grid=(ntile,),
        in_specs=[tile, tile, prev, tile, tile, prev, cw4, one, hw, one, hw, one, one],
        out_specs=[tile, tile, cw4, one, hw, one, hw, one, one],
        out_shape=[_sds((n, c), BF16), _sds((n, c), BF16), _sds((4, c), F32), _sds((1, c), F32), _sds((LRU_HEADS, LANE, LANE), F32),
                   _sds((1, c), F32), _sds((LRU_HEADS, LANE, LANE), F32), _sds((1, c), F32), _sds((1, c), F32)],
        scratch_shapes=[pltpu.VMEM((SUB, c), F32), pltpu.VMEM((SUB + tq, c), F32), pltpu.VMEM((tq + SUB, c), F32)],
        compiler_params=_params(("arbitrary",)),
        name=name,
    )(dy, xr, xr, gr, h, h, cw, cb, wa, ba, wx, bx, lam)


def ada_fwd(c_all, w, b, name):
    nl, _, cols = w.shape
    nb = c_all.shape[0]

    def body(c_ref, w_ref, b_ref, o_ref):
        sc = _silu(c_ref[...]).astype(BF16)
        o_ref[0] = jnp.dot(sc, w_ref[0].astype(BF16), preferred_element_type=F32) + b_ref[0]

    return pl.pallas_call(
        body,
        grid=(nl,),
        in_specs=[pl.BlockSpec((nb, D), lambda l: (0, 0)), pl.BlockSpec((1, D, cols), lambda l: (l, 0, 0)), pl.BlockSpec((1, 1, cols), lambda l: (l, 0, 0))],
        out_specs=pl.BlockSpec((1, nb, cols), lambda l: (l, 0, 0)),
        out_shape=_sds((nl, nb, cols), F32),
        compiler_params=_params(("arbitrary",)),
        name=name,
    )(c_all, w, b)


def ada_bwd(c_all, dmod, name):
    nl, nb, cols = dmod.shape

    def body(c_ref, d_ref, o_ref):
        sc = _silu(c_ref[...]).astype(BF16)
        o_ref[0] = lax.dot_general(sc, d_ref[0].astype(BF16), (((0,), (0,)), ((), ())), preferred_element_type=F32)

    return pl.pallas_call(
        body,
        grid=(nl,),
        in_specs=[pl.BlockSpec((nb, D), lambda l: (0, 0)), pl.BlockSpec((1, nb, cols), lambda l: (l, 0, 0))],
        out_specs=pl.BlockSpec((1, D, cols), lambda l: (l, 0, 0)),
        out_shape=_sds((nl, D, cols), F32),
        compiler_params=_params(("arbitrary",)),
        name=name,
    )(c_all, dmod)


def loss_grad(y, target, name):
    n = y.shape[0]

    def body(y_ref, t_ref, dy_ref, l_ref, acc):
        i = pl.program_id(0)

        @pl.when(i == 0)
        def _():
            acc[...] = jnp.zeros_like(acc)

        e = y_ref[...] - t_ref[...]
        dy_ref[...] = e * (1.0 / D)
        acc[...] += jnp.sum(e * e, axis=0, keepdims=True)

        @pl.when(i == n // TM - 1)
        def _():
            l_ref[...] = jnp.full((1, LANE), 0.5 / D, F32) * jnp.sum(acc[...])

    row = pl.BlockSpec((TM, D), lambda i: (i, 0))
    return pl.pallas_call(
        body,
        grid=(n // TM,),
        in_specs=[row, row],
        out_specs=[row, pl.BlockSpec((1, LANE), lambda i: (0, 0))],
        out_shape=[_sds((n, D), F32), _sds((1, LANE), F32)],
        scratch_shapes=[pltpu.VMEM((1, D), F32)],
        compiler_params=_params(("arbitrary",)),
        name=name,
    )(y, target)


def sum_parts(parts, name):
    ns, r, _ = parts.shape
    tr = _row_tile(r, 1024)

    def body(p_ref, o_ref):
        acc = p_ref[0]
        for k in range(1, ns):
            acc = acc + p_ref[k]
        o_ref[...] = acc

    return pl.pallas_call(
        body,
        grid=(r // tr,),
        in_specs=[pl.BlockSpec((ns, tr, LANE), lambda i: (0, i, 0))],
        out_specs=pl.BlockSpec((tr, LANE), lambda i: (i, 0)),
        out_shape=_sds((r, LANE), F32),
        compiler_params=_params(("arbitrary",)),
        name=name,
    )(parts)


def _row_tile(r, cap):
    if r <= cap:
        return r
    best = None
    for t in range(16, cap + 1, 16):
        if r % t == 0:
            best = t
    assert best is not None, r
    return best


def adamw(w, m, v, gparts, name):
    r, c = w.shape
    ns = gparts.shape[0]
    tr = _row_tile(r, min(512, 256 * 1024 // c))
    c1 = 1.0 - B1**STEP
    c2 = 1.0 - B2**STEP

    def body(w_ref, m_ref, v_ref, g_ref, go_ref, d_ref, mo_ref, vo_ref):
        g = g_ref[0].astype(F32)
        for k in range(1, ns):
            g = g + g_ref[k].astype(F32)
        mn = B1 * m_ref[...] + (1.0 - B1) * g
        vn = B2 * v_ref[...] + (1.0 - B2) * (g * g)
        go_ref[...] = g
        mo_ref[...] = mn
        vo_ref[...] = vn
        d_ref[...] = -LR * ((mn / c1) / (jnp.sqrt(vn / c2) + AEPS) + WD * w_ref[...])

    tile = pl.BlockSpec((tr, c), lambda i: (i, 0))
    return pl.pallas_call(
        body,
        grid=(r // tr,),
        in_specs=[tile, tile, tile, pl.BlockSpec((ns, tr, c), lambda i: (0, i, 0))],
        out_specs=[tile, tile, tile, tile],
        out_shape=[_sds((r, c), F32)] * 4,
        compiler_params=_params(("arbitrary",)),
        name=name,
    )(w, m, v, gparts)


WEIGHTS = ["ada_w", "ada_b", "ln_g", "ln_b", "ffn_w_in", "ffn_w_out", "ev_w_in", "ssd_conv_w", "ssd_conv_b", "ssd_dt_bias",
           "ssd_a_log", "ssd_d", "ssd_norm_g", "pool_w", "pool_scale", "ev_w_out", "od_w_in", "conf_dw_w", "conf_dw_b",
           "conf_ln_g", "conf_ln_b", "lru_conv_w", "lru_conv_b", "lru_wa", "lru_ba", "lru_wx", "lru_bx", "lru_lambda", "od_w_out"]
BIG = ("ada_w", "ffn_w_in", "ffn_w_out", "ev_w_in", "ev_w_out", "od_w_in", "od_w_out")
SMALL = {
    "ada_b": ((4, 9216), None), "ln_g": ((4, 3, 1024), 2), "ln_b": ((4, 3, 1024), 2),
    "ssd_conv_w": ((2, 4, 1536), 2), "ssd_conv_b": ((2, 1536), None), "ssd_dt_bias": ((2, 16), None),
    "ssd_a_log": ((2, 16), None), "ssd_d": ((2, 16), None), "ssd_norm_g": ((2, 1024), None),
    "pool_w": ((2, 4, 128, 128), None), "pool_scale": ((2, 512), None),
    "conf_dw_w": ((2, 31, 512), 2), "conf_dw_b": ((2, 512), 1), "conf_ln_g": ((2, 512), 1), "conf_ln_b": ((2, 512), 1),
    "lru_conv_w": ((2, 4, 1024), 2), "lru_conv_b": ((2, 1024), 1), "lru_wa": ((2, 8, 128, 128), None),
    "lru_ba": ((2, 1024), 1), "lru_wx": ((2, 8, 128, 128), None), "lru_bx": ((2, 1024), 1), "lru_lambda": ((2, 1024), 1),
}
PACK_ROWS = 2 * SUB * LANE


def _pack(arrs, mult=PACK_ROWS):
    flat = jnp.concatenate([a.reshape(-1) for a in arrs])
    pad = (-flat.shape[0]) % mult
    return jnp.pad(flat, (0, pad)).reshape(-1, LANE)


def _unpack(buf, shapes, lead=()):
    flat = buf.reshape(lead + (-1,))
    out, off = [], 0
    for s in shapes:
        k = math.prod(s)
        out.append(flat[..., off : off + k].reshape(lead + tuple(s)))
        off += k
    return out


def _pad_lanes(v):
    return jnp.pad(v, (0, LANE - v.shape[0]))[None]


def kernel(x, c, ada_w, ada_b, ln_g, ln_b, ffn_w_in, ffn_w_out, ev_w_in, ssd_conv_w, ssd_conv_b, ssd_dt_bias, ssd_a_log, ssd_d, ssd_norm_g, pool_w, pool_scale, ev_w_out, od_w_in, conf_dw_w, conf_dw_b, conf_ln_g, conf_ln_b, lru_conv_w, lru_conv_b, lru_wa, lru_ba, lru_wx, lru_bx, lru_lambda, od_w_out, loss_target, m_ada_w, m_ada_b, m_ln_g, m_ln_b, m_ffn_w_in, m_ffn_w_out, m_ev_w_in, m_ssd_conv_w, m_ssd_conv_b, m_ssd_dt_bias, m_ssd_a_log, m_ssd_d, m_ssd_norm_g, m_pool_w, m_pool_scale, m_ev_w_out, m_od_w_in, m_conf_dw_w, m_conf_dw_b, m_conf_ln_g, m_conf_ln_b, m_lru_conv_w, m_lru_conv_b, m_lru_wa, m_lru_ba, m_lru_wx, m_lru_bx, m_lru_lambda, m_od_w_out, v_ada_w, v_ada_b, v_ln_g, v_ln_b, v_ffn_w_in, v_ffn_w_out, v_ev_w_in, v_ssd_conv_w, v_ssd_conv_b, v_ssd_dt_bias, v_ssd_a_log, v_ssd_d, v_ssd_norm_g, v_pool_w, v_pool_scale, v_ev_w_out, v_od_w_in, v_conf_dw_w, v_conf_dw_b, v_conf_ln_g, v_conf_ln_b, v_lru_conv_w, v_lru_conv_b, v_lru_wa, v_lru_ba, v_lru_wx, v_lru_bx, v_lru_lambda, v_od_w_out):
    p = dict(locals())
    nb, seq, _ = x.shape
    n = nb * seq
    me = 4 * lax.axis_index("x") + 2 * lax.axis_index("y") + lax.axis_index("c")
    sharded = [k for k, (_, ax) in SMALL.items() if ax is not None]

    def cols_of(g):
        return jnp.moveaxis(g, 0, 1).reshape(g.shape[1], N_DEV * g.shape[2])

    def rows_of(g):
        return g.reshape(N_DEV * g.shape[1], g.shape[2])

    def ev_in_of(g):
        w = cols_of(g)
        return jnp.concatenate([w[:, :2560], w[:, 2576:], jnp.pad(w[:, 2560:2576], ((0, 0), (0, LANE - SSD_HEADS)))], axis=1)

    sh_ffn_in, sh_ffn_out = ffn_w_in.astype(BF16), ffn_w_out.astype(BF16)
    sh_mix_in = [ev_w_in.astype(BF16), od_w_in.astype(BF16)]
    sh_mix_out = [ev_w_out.astype(BF16), od_w_out.astype(BF16)]

    def ffn_items(l, i):
        return [(sh_ffn_in[l, i], True), (sh_ffn_out[l, i], True)]

    def mix_items(l):
        return [(sh_mix_in[l % 2][l // 2], True), (sh_mix_out[l % 2][l // 2], True)]

    sm_local_shapes = [p[k].shape for k in sharded]
    g_in, g_out, sm_all = exchange(ffn_items(0, 0) + [(_pack([p[k] for k in sharded] + [c]), True)], "ag_first")
    w_ffn = {(0, 0): (cols_of(g_in), rows_of(g_out))}
    w_mix = {}
    got = _unpack(sm_all, sm_local_shapes + [c.shape], lead=(N_DEV,))
    full = {k: p[k] for k, (_, ax) in SMALL.items() if ax is None}
    for k, g in zip(sharded, got[:-1]):
        full[k] = jnp.moveaxis(g, 0, SMALL[k][1]).reshape(SMALL[k][0])
    c_all = got[-1].reshape(N_DEV * nb, D)

    cols = ada_w.shape[-1]
    ada_b_loc = lax.dynamic_slice_in_dim(ada_b, me * cols, cols, axis=1)[:, None, :]
    mod_cols = ada_fwd(c_all, ada_w, ada_b_loc, "ada_fwd")
    (mod_x,) = exchange([(mod_cols.reshape(DEPTH, N_DEV, nb, cols).transpose(1, 0, 2, 3), False)], "a2a_mod")
    mod = mod_x.transpose(1, 2, 0, 3).reshape(DEPTH, nb, N_MOD, 1, D)

    def vec(l, j):
        return mod[l, :, j]

    def row(a):
        return a[None]

    xs = x.reshape(n, D)
    saved = []
    for l in range(DEPTH):
        s = {"x0": xs}
        e = l // 2
        s["h1"], s["g1"], s["u1"], s["a1"], gm_in, gm_out, g_in, g_out = ffn_up(
            xs, vec(l, 0), vec(l, 1), w_ffn[l, 0][0], seq, "ffn_up_c4", carry=mix_items(l) + ffn_items(l, 1))
        w_mix[l] = ((ev_in_of if l % 2 == 0 else cols_of)(gm_in), rows_of(gm_out))
        w_ffn[l, 1] = (cols_of(g_in), rows_of(g_out))
        x1, s["y1"] = mm_postnorm([s["a1"]], w_ffn[l, 0][1], xs, vec(l, 2), row(full["ln_g"][l, 0]), row(full["ln_b"][l, 0]), 0.5, seq, "ffn_down")
        s["x1"] = x1
        if l % 2 == 0:
            s["h2"], s["z"], s["xbc"], s["u"], s["dtr"] = mod_mm(x1, vec(l, 3), vec(l, 4), w_mix[l][0], EV_SPLITS, seq, "ev_in")
            s["ya"], s["sprev"] = ssd_fwd(s["z"], s["xbc"], s["dtr"], full["ssd_conv_w"][e], row(full["ssd_conv_b"][e]), _pad_lanes(full["ssd_dt_bias"][e]),
                                          _pad_lanes(full["ssd_a_log"][e]), _pad_lanes(full["ssd_d"][e]), row(full["ssd_norm_g"][e]), seq, "ssd_fwd")
            s["yb"] = pool_fwd(s["u"], full["pool_w"][e], row(full["pool_scale"][e]), seq, "pool_fwd")
        else:
            s["h2"], s["vg"], s["xr"], s["gr"] = mod_mm(x1, vec(l, 3), vec(l, 4), w_mix[l][0], OD_SPLITS, seq, "od_in")
            s["ya"], s["conv"] = conf_fwd(s["vg"], full["conf_dw_w"][e], row(full["conf_dw_b"][e]), row(full["conf_ln_g"][e]), row(full["conf_ln_b"][e]), seq, "conf_fwd")
            s["yb"], s["hst"] = lru_fwd(s["xr"], s["gr"], full["lru_conv_w"][e], row(full["lru_conv_b"][e]), full["lru_wa"][e], row(full["lru_ba"][e]),
                                        full["lru_wx"][e], row(full["lru_bx"][e]), row(full["lru_lambda"][e]), seq, "lru_fwd")
        x2, s["y2"] = mm_postnorm([s["ya"], s["yb"]], w_mix[l][1], x1, vec(l, 5), row(full["ln_g"][l, 1]), row(full["ln_b"][l, 1]), 1.0, seq, "mix_out")
        s["x2"] = x2
        if l + 1 < DEPTH:
            s["h3"], s["g3"], s["u3"], s["a3"], g_in, g_out = ffn_up(x2, vec(l, 6), vec(l, 7), w_ffn[l, 1][0], seq, "ffn_up_c2", carry=ffn_items(l + 1, 0))
            w_ffn[l + 1, 0] = (cols_of(g_in), rows_of(g_out))
        else:
            s["h3"], s["g3"], s["u3"], s["a3"] = ffn_up(x2, vec(l, 6), vec(l, 7), w_ffn[l, 1][0], seq, "ffn_up")
        xs, s["y3"] = mm_postnorm([s["a3"]], w_ffn[l, 1][1], x2, vec(l, 8), row(full["ln_g"][l, 2]), row(full["ln_b"][l, 2]), 0.5, seq, "ffn_down")
        saved.append(s)

    dx, loss_row = loss_grad(xs, loss_target.reshape(n, D), "loss")
    loss = lax.psum(loss_row[0, 0], ("x", "y", "c"))

    sg = {k: [None] * shape[0] for k, (shape, _) in SMALL.items()}
    sg["ln_g"] = [[None] * 3 for _ in range(DEPTH)]
    sg["ln_b"] = [[None] * 3 for _ in range(DEPTH)]
    dmod = [[None] * N_MOD for _ in range(DEPTH)]
    pending, got_w = [], {}

    def cut_cols(g):
        r, cc = g.shape
        return g.reshape(r, N_DEV, cc // N_DEV).transpose(1, 0, 2)

    def cut_rows(g):
        r, cc = g.shape
        return g.reshape(N_DEV, r // N_DEV, cc)

    def take(room):
        sel = []
        for j, (_, a) in enumerate(pending):
            if a.size * a.dtype.itemsize <= room:
                sel.append(j)
                room -= a.size * a.dtype.itemsize
        items = [pending[j] for j in sel]
        pending[:] = [it for j, it in enumerate(pending) if j not in sel]
        return [k for k, _ in items], [(a, False) for _, a in items]

    room_down, room_up, room_mix = 8 * MIB, 12 * MIB, 6 * MIB

    def postnorm_backward(dxo, xin, y, g, lng, lnb, w, ks, coef, name, room):
        keys, carry = take(room)
        outs = postnorm_bwd(dxo, xin, y, g, lng, lnb, w, ks, coef, seq, name + "_c%d" % len(keys), carry=carry)
        got_w.update(zip(keys, outs[5 + len(ks):]))
        return outs[0], outs[1], outs[2 : 2 + len(ks)], outs[2 + len(ks)], outs[3 + len(ks)], outs[4 + len(ks)]

    def proj_backward(dparts, w, xin, scv, dxres, name):
        keys, carry = take(room_mix)
        outs = proj_bwd_in(dparts, w, xin, scv, dxres, seq, name + "_c%d" % len(keys), carry=carry)
        got_w.update(zip(keys, outs[3:]))
        return outs[:3]

    def ffn_backward(l, i, dxo, s, xin, hk, gk, uk, ak, yk, jbase, lnj):
        dxres, dy, (da,), dmod[l][jbase + 2], sg["ln_g"][l][lnj], sg["ln_b"][l][lnj] = postnorm_backward(
            dxo, xin, s[yk], vec(l, jbase + 2), row(full["ln_g"][l, lnj]), row(full["ln_b"][l, lnj]), w_ffn[l, i][1], [FF], 0.5, "ffn_down_bwd", room_down)
        keys, carry = take(room_up)
        outs = ffn_bwd_in(da, s[gk], s[uk], w_ffn[l, i][0], xin, vec(l, jbase + 1), dxres, seq, "ffn_up_bwd_c%d" % len(keys), carry=carry)
        dgu, dxi, dmod[l][jbase], dmod[l][jbase + 1] = outs[:4]
        got_w.update(zip(keys, outs[4:]))
        pending.append((("ffn_out", l, i), cut_rows(mm_tn(s[ak], dy, "wg_ffn_out"))))
        pending.append((("ffn_in", l, i), cut_cols(mm_tn(s[hk], dgu, "wg_ffn_in"))))
        return dxi

    for l in reversed(range(DEPTH)):
        s = saved[l]
        e = l // 2
        dx = ffn_backward(l, 1, dx, s, s["x2"], "h3", "g3", "u3", "a3", "y3", 6, 2)
        ks = [1024, POOL_DIM] if l % 2 == 0 else [CONF_DIM, LRU_DIM]
        dxres, dy, (dya, dyb), dmod[l][5], sg["ln_g"][l][1], sg["ln_b"][l][1] = postnorm_backward(
            dx, s["x1"], s["y2"], vec(l, 5), row(full["ln_g"][l, 1]), row(full["ln_b"][l, 1]), w_mix[l][1], ks, 1.0, "mix_out_bwd", room_mix)
        pending.append((("mix_out", l), cut_rows(jnp.concatenate([mm_tn(s["ya"], dy, "wg_mix_a"), mm_tn(s["yb"], dy, "wg_mix_b")], axis=0))))
        if l % 2 == 0:
            (dz, dxbc, ddt, sg["ssd_conv_w"][e], dcb, ddtb, dalog, ddsk, dng) = ssd_bwd(
                dya, s["z"], s["xbc"], s["dtr"], s["sprev"], full["ssd_conv_w"][e], row(full["ssd_conv_b"][e]), _pad_lanes(full["ssd_dt_bias"][e]),
                _pad_lanes(full["ssd_a_log"][e]), _pad_lanes(full["ssd_d"][e]), row(full["ssd_norm_g"][e]), seq, "ssd_bwd")
            sg["ssd_conv_b"][e], sg["ssd_norm_g"][e] = dcb[0], dng[0]
            sg["ssd_dt_bias"][e], sg["ssd_a_log"][e], sg["ssd_d"][e] = ddtb[0, :SSD_HEADS], dalog[0, :SSD_HEADS], ddsk[0, :SSD_HEADS]
            du, sg["pool_w"][e], dps = pool_bwd(dyb, s["u"], full["pool_w"][e], row(full["pool_scale"][e]), seq, "pool_bwd")
            sg["pool_scale"][e] = dps[0]
            dparts = [dz, dxbc, du, ddt]
            dx, dmod[l][3], dmod[l][4] = proj_backward(dparts, w_mix[l][0], s["x1"], vec(l, 4), dxres, "ev_in_bwd")
            gz, gxbc, gu, gdt = [mm_tn(s["h2"], dp, "wg_ev_in") for dp in dparts]
            pending.append((("mix_in", l), cut_cols(jnp.concatenate([gz, gxbc, gdt[:, :SSD_HEADS], gu], axis=1))))
        else:
            dvg, sg["conf_dw_w"][e], dcb, dlg, dlb = conf_bwd(dya, s["conv"], s["vg"], full["conf_dw_w"][e], row(full["conf_ln_g"][e]), row(full["conf_ln_b"][e]), seq, "conf_bwd")
            sg["conf_dw_b"][e], sg["conf_ln_g"][e], sg["conf_ln_b"][e] = dcb[0], dlg[0], dlb[0]
            (dxr, dgr, sg["lru_conv_w"][e], dcb, sg["lru_wa"][e], dba, sg["lru_wx"][e], dbx, dlam) = lru_bwd(
                dyb, s["xr"], s["gr"], s["hst"], full["lru_conv_w"][e], row(full["lru_conv_b"][e]), full["lru_wa"][e], row(full["lru_ba"][e]),
                full["lru_wx"][e], row(full["lru_bx"][e]), row(full["lru_lambda"][e]), seq, "lru_bwd")
            sg["lru_conv_b"][e], sg["lru_ba"][e], sg["lru_bx"][e], sg["lru_lambda"][e] = dcb[0], dba[0], dbx[0], dlam[0]
            dparts = [dvg, dxr, dgr]
            dx, dmod[l][3], dmod[l][4] = proj_backward(dparts, w_mix[l][0], s["x1"], vec(l, 4), dxres, "od_in_bwd")
            pending.append((("mix_in", l), cut_cols(jnp.concatenate([mm_tn(s["h2"], dp, "wg_od_in") for dp in dparts], axis=1))))
        dx = ffn_backward(l, 0, dx, s, s["x0"], "h1", "g1", "u1", "a1", "y1", 0, 0)
    grad_x = dx.reshape(nb, seq, D)

    dmod_mine = jnp.stack([jnp.concatenate([d[:, 0, :] for d in dmod[l]], axis=-1) for l in range(DEPTH)])
    sg["ada_b"] = [jnp.sum(dmod_mine[l], axis=0) for l in range(DEPTH)]
    sg["ln_g"] = [jnp.concatenate(r, axis=0) for r in sg["ln_g"]]
    sg["ln_b"] = [jnp.concatenate(r, axis=0) for r in sg["ln_b"]]
    small_names = list(SMALL)
    sg_packed = _pack([jnp.stack(sg[k]).reshape(SMALL[k][0]) for k in small_names], N_DEV * PACK_ROWS)
    keys, carry = take(1 << 40)
    outs = exchange(carry + [(dmod_mine.reshape(DEPTH, nb, N_DEV, cols).transpose(2, 0, 1, 3), False),
                             (sg_packed.reshape(N_DEV, -1, LANE), False)], "x_last")
    got_w.update(zip(keys, outs))
    dmod_x, parts = outs[len(keys):]
    g_ada_w = ada_bwd(c_all, dmod_x.transpose(1, 0, 2, 3).reshape(DEPTH, N_DEV * nb, cols), "ada_bwd")

    (sg_sum,) = exchange([(sum_parts(parts, "sum_smallgrad"), True)], "ag_smallsum")
    summed = _unpack(sg_sum.reshape(-1, LANE), [SMALL[k][0] for k in small_names])
    grads = {}
    for k, g in zip(small_names, summed):
        ax = SMALL[k][1]
        grads[k] = g if ax is None else lax.dynamic_slice_in_dim(g, me * p[k].shape[ax], p[k].shape[ax], axis=ax)
    loc_shapes = [p[k].shape for k in small_names]
    whole = 512 * LANE
    _, d_s, m_s, v_s = adamw(_pack([p[k] for k in small_names], whole), _pack([p["m_" + k] for k in small_names], whole),
                             _pack([p["v_" + k] for k in small_names], whole), _pack([grads[k] for k in small_names], whole)[None], "adamw_small")
    delta = dict(zip(small_names, _unpack(d_s, loc_shapes)))
    new_m = dict(zip(small_names, _unpack(m_s, loc_shapes)))
    new_v = dict(zip(small_names, _unpack(v_s, loc_shapes)))

    big_parts = {
        "ada_w": g_ada_w[None],
        "ffn_w_in": jnp.stack([jnp.stack([got_w["ffn_in", l, i] for i in range(2)], axis=1) for l in range(DEPTH)], axis=1),
        "ffn_w_out": jnp.stack([jnp.stack([got_w["ffn_out", l, i] for i in range(2)], axis=1) for l in range(DEPTH)], axis=1),
        "ev_w_in": jnp.stack([got_w["mix_in", l] for l in (0, 2)], axis=1),
        "ev_w_out": jnp.stack([got_w["mix_out", l] for l in (0, 2)], axis=1),
        "od_w_in": jnp.stack([got_w["mix_in", l] for l in (1, 3)], axis=1),
        "od_w_out": jnp.stack([got_w["mix_out", l] for l in (1, 3)], axis=1),
    }
    for k in BIG:
        w = p[k]
        r2 = (math.prod(w.shape[:-1]), w.shape[-1])
        gp = big_parts[k]
        out = adamw(w.reshape(r2), p["m_" + k].reshape(r2), p["v_" + k].reshape(r2), gp.reshape((gp.shape[0],) + r2), "adamw_" + k)
        grads[k], delta[k], new_m[k], new_v[k] = [o.reshape(w.shape) for o in out]

    return (loss, grad_x, *[grads[k] for k in WEIGHTS], *[delta[k] for k in WEIGHTS], *[new_m[k] for k in WEIGHTS], *[new_v[k] for k in WEIGHTS])
```

```python
import functools
import math

import jax
import jax.numpy as jnp
from jax import lax
from jax.experimental import pallas as pl
from jax.experimental.pallas import tpu as pltpu

F32 = jnp.float32
BF16 = jnp.bfloat16
HI = lax.Precision.HIGHEST

N_DEV = 8
D = 1024
DEPTH = 4
N_MOD = 9
FF = 2816
ALPHA = (2.0 * DEPTH) ** 0.25
EPS = 1e-5
SSD_Q = 128
SSD_HEADS = 16
SSD_P = 64
SSD_N = 128
SSD_XBC = 1536
POOL_WINDOWS = (2, 4, 8, 16)
POOL_DIM = 512
CONF_DIM = 512
CONF_K = 31
LRU_DIM = 1024
LRU_HEADS = 8
LRU_C = 8.0
EV_SPLITS = (1024, 1536, 512, 128)
OD_SPLITS = (1024, 1024, 1024)
LR, B1, B2, AEPS, WD, STEP = 0.001, 0.9, 0.999, 1e-08, 0.01, 10

LANE = 128
SUB = 8
MIB = 1024 * 1024
VMEM_LIMIT = 48 * MIB
TM = 512
SPLIT_ROWS = 256


def _params(sem, vmem=VMEM_LIMIT):
    return pltpu.CompilerParams(dimension_semantics=sem, vmem_limit_bytes=vmem)


def _sds(shape, dtype):
    return jax.ShapeDtypeStruct(shape, dtype)


def _modulate(x, sh, sc):
    return x * (1.0 + sc) + sh


def _postnorm(x, y, g, lng, lnb, *, coef):
    z = ALPHA * x + coef * (1.0 + g) * y
    mu = jnp.mean(z, axis=-1, keepdims=True)
    zc = z - mu
    var = jnp.mean(zc * zc, axis=-1, keepdims=True)
    return zc * lax.rsqrt(var + EPS) * lng + lnb


def _place():
    mx, my, mc = lax.axis_index("x"), lax.axis_index("y"), lax.axis_index("c")

    def at(r):
        px = 1 - mx if r & 4 else mx
        py = 1 - my if r & 2 else my
        pc = 1 - mc if r & 1 else mc
        return (px, py, pc), 4 * px + 2 * py + pc

    return 4 * mx + 2 * my + mc, at


def _carry_plan(items):
    hbm = pl.BlockSpec(memory_space=pltpu.HBM)
    k = len(items)
    shapes = [_sds((N_DEV,) + a.shape if g else a.shape, a.dtype) for a, g in items]
    scratch = [pltpu.SemaphoreType.DMA((k * (N_DEV - 1),)), pltpu.SemaphoreType.DMA((k * (N_DEV - 1),)), pltpu.SemaphoreType.DMA((k,))] if k else []
    return [hbm] * k, [hbm] * k, shapes, scratch


def _remote(src, dst, sems, s, pos):
    return pltpu.make_async_remote_copy(src_ref=src, dst_ref=dst, send_sem=sems[0].at[s], recv_sem=sems[1].at[s],
                                        device_id=pos, device_id_type=pl.DeviceIdType.MESH)


def _carry_start(gathers, x_refs, o_refs, sems):
    me, at = _place()
    for a, (gather, x_ref, o_ref) in enumerate(zip(gathers, x_refs, o_refs)):
        base = a * (N_DEV - 1)
        pltpu.make_async_copy(x_ref if gather else x_ref.at[me], o_ref.at[me], sems[2].at[a]).start()
        if gather:
            for s, r in enumerate((1, 4, 2, 6)):
                _remote(x_ref, o_ref.at[me], sems, base + s, at(r)[0]).start()
        else:
            for r in range(1, N_DEV):
                pos, pid = at(r)
                _remote(x_ref.at[pid], o_ref.at[me], sems, base + r - 1, pos).start()


def _carry_pass_on(gathers, x_refs, o_refs, sems):
    _, at = _place()
    sibling = at(1)[0]
    for a, (gather, x_ref, o_ref) in enumerate(zip(gathers, x_refs, o_refs)):
        if gather:
            base = a * (N_DEV - 1)
            for j, r in enumerate((4, 2, 6)):
                pos, pid = at(r)
                _remote(x_ref, o_ref.at[pid], sems, base + 1 + j, pos).wait_recv()
                _remote(o_ref.at[pid], o_ref.at[pid], sems, base + 4 + j, sibling).start()


def _carry_wait(gathers, x_refs, o_refs, sems):
    me, at = _place()
    for a, (gather, x_ref, o_ref) in enumerate(zip(gathers, x_refs, o_refs)):
        base = a * (N_DEV - 1)
        if gather:
            sib_pos, sib_id = at(1)
            _remote(x_ref, o_ref.at[sib_id], sems, base, sib_pos).wait_recv()
            for j, r in enumerate((4, 2, 6)):
                _remote(x_ref, o_ref.at[at(r | 1)[1]], sems, base + 4 + j, sib_pos).wait_recv()
            for s in range(N_DEV - 1):
                _remote(x_ref, o_ref.at[me], sems, base + s, sib_pos).wait_send()
            pltpu.make_async_copy(x_ref, o_ref.at[me], sems[2].at[a]).wait()
        else:
            for r in range(1, N_DEV):
                pos, pid = at(r)
                _remote(x_ref.at[pid], o_ref.at[pid], sems, base + r - 1, pos).wait_recv()
            for r in range(1, N_DEV):
                pos, pid = at(r)
                _remote(x_ref.at[pid], o_ref.at[me], sems, base + r - 1, pos).wait_send()
            pltpu.make_async_copy(x_ref.at[me], o_ref.at[me], sems[2].at[a]).wait()


def exchange(items, name):
    gathers = [g for _, g in items]
    k = len(items)
    in_specs, out_specs, shapes, scratch = _carry_plan(items)

    def body(*refs):
        x_refs, o_refs, sems = refs[:k], refs[k : 2 * k], refs[2 * k :]
        _carry_start(gathers, x_refs, o_refs, sems)
        _carry_pass_on(gathers, x_refs, o_refs, sems)
        _carry_wait(gathers, x_refs, o_refs, sems)

    return pl.pallas_call(
        body,
        in_specs=in_specs,
        out_specs=out_specs,
        out_shape=shapes,
        scratch_shapes=scratch,
        compiler_params=pltpu.CompilerParams(has_side_effects=True),
        name=name,
    )(*[a for a, _ in items])


def ffn_up(x, sh, sc, w, seq, name, carry=()):
    n = x.shape[0]
    tn = FF // 2
    nj = FF // tn
    ni = n // TM
    tps = seq // TM
    k = len(carry)
    gathers = [g for _, g in carry]
    c_in, c_out, c_shapes, c_scratch = _carry_plan(carry)

    def body(x_ref, sh_ref, sc_ref, w_hbm, *rest):
        cx, (h_ref, g_ref, u_ref, a_ref), co, (w_ref, w_sem, *sems) = rest[:k], rest[k : k + 4], rest[k + 4 : 2 * k + 4], rest[2 * k + 4 :]
        i = pl.program_id(0)

        @pl.when(i == 0)
        def _():
            if k:
                _carry_start(gathers, cx, co, sems)
            cp = pltpu.make_async_copy(w_hbm, w_ref, w_sem)
            cp.start()
            cp.wait()

        h = _modulate(x_ref[...], sh_ref[0], sc_ref[0]).astype(BF16)
        h_ref[...] = h
        for j in range(nj):
            g = jnp.dot(h, w_ref[:, j * tn : (j + 1) * tn], preferred_element_type=F32)
            u = jnp.dot(h, w_ref[:, FF + j * tn : FF + (j + 1) * tn], preferred_element_type=F32)
            g_ref[:, j * tn : (j + 1) * tn] = g.astype(BF16)
            u_ref[:, j * tn : (j + 1) * tn] = u.astype(BF16)
            a_ref[:, j * tn : (j + 1) * tn] = (g * jax.nn.sigmoid(g) * u).astype(BF16)
        if k:
            @pl.when(i == max(ni - 2, 0))
            def _():
                _carry_pass_on(gathers, cx, co, sems)

            @pl.when(i == ni - 1)
            def _():
                _carry_wait(gathers, cx, co, sems)

    vec = pl.BlockSpec((1, 1, D), lambda i: (i // tps, 0, 0))
    col = pl.BlockSpec((TM, FF), lambda i: (i, 0))
    return pl.pallas_call(
        body,
        grid=(ni,),
        in_specs=[pl.BlockSpec((TM, D), lambda i: (i, 0)), vec, vec, pl.BlockSpec(memory_space=pltpu.HBM)] + c_in,
        out_specs=[pl.BlockSpec((TM, D), lambda i: (i, 0)), col, col, col] + c_out,
        out_shape=[_sds((n, D), BF16), _sds((n, FF), BF16), _sds((n, FF), BF16), _sds((n, FF), BF16)] + c_shapes,
        scratch_shapes=[pltpu.VMEM((D, 2 * FF), BF16), pltpu.SemaphoreType.DMA] + c_scratch,
        compiler_params=_params(("arbitrary",)),
        name=name,
    )(x, sh, sc, w, *[a for a, _ in carry])


def mod_mm(x, sh, sc, w, splits, seq, name):
    n = x.shape[0]
    m = w.shape[1]
    tps = seq // TM
    offs = [sum(splits[:k]) for k in range(len(splits))]

    def body(x_ref, sh_ref, sc_ref, w_ref, h_ref, *outs):
        h = _modulate(x_ref[...], sh_ref[0], sc_ref[0]).astype(BF16)
        h_ref[...] = h
        for o_ref, off, wd in zip(outs, offs, splits):
            o_ref[...] = jnp.dot(h, w_ref[:, off : off + wd], preferred_element_type=F32)

    vec = pl.BlockSpec((1, 1, D), lambda i: (i // tps, 0, 0))
    return pl.pallas_call(
        body,
        grid=(n // TM,),
        in_specs=[pl.BlockSpec((TM, D), lambda i: (i, 0)), vec, vec, pl.BlockSpec((D, m), lambda i: (0, 0))],
        out_specs=[pl.BlockSpec((TM, D), lambda i: (i, 0))] + [pl.BlockSpec((TM, wd), lambda i: (i, 0)) for wd in splits],
        out_shape=[_sds((n, D), BF16)] + [_sds((n, wd), F32) for wd in splits],
        compiler_params=_params(("arbitrary",)),
        name=name,
    )(x, sh, sc, w)


def mm_postnorm(parts, w, x, g, lng, lnb, coef, seq, name, carry=()):
    n = x.shape[0]
    ni = n // TM
    tps = seq // TM
    ks = [p.shape[1] for p in parts]
    offs = [sum(ks[:k]) for k in range(len(ks))]
    npart = len(parts)
    nc = len(carry)
    gathers = [gt for _, gt in carry]
    c_in, c_out, c_shapes, c_scratch = _carry_plan(carry)

    def body(*refs):
        a_refs = refs[:npart]
        w_ref, x_ref, g_ref, lng_ref, lnb_ref = refs[npart : npart + 5]
        cx = refs[npart + 5 : npart + 5 + nc]
        xn_ref, y_ref = refs[npart + 5 + nc : npart + 7 + nc]
        co, sems = refs[npart + 7 + nc : npart + 7 + 2 * nc], refs[npart + 7 + 2 * nc :]
        i = pl.program_id(0)
        if nc:
            @pl.when(i == 0)
            def _():
                _carry_start(gathers, cx, co, sems)

        for r0 in range(0, TM, SPLIT_ROWS):
            rows = slice(r0, r0 + SPLIT_ROWS)
            y = None
            for a_ref, off, k in zip(a_refs, offs, ks):
                t = jnp.dot(a_ref[rows, :], w_ref[off : off + k, :], preferred_element_type=F32)
                y = t if y is None else y + t
            y_ref[rows, :] = y
            xn_ref[rows, :] = _postnorm(x_ref[rows, :], y, g_ref[0], lng_ref[...], lnb_ref[...], coef=coef)
        if nc:
            @pl.when(i == max(ni - 2, 0))
            def _():
                _carry_pass_on(gathers, cx, co, sems)

            @pl.when(i == ni - 1)
            def _():
                _carry_wait(gathers, cx, co, sems)

    row = pl.BlockSpec((TM, D), lambda i: (i, 0))
    one = pl.BlockSpec((1, D), lambda i: (0, 0))
    return pl.pallas_call(
        body,
        grid=(ni,),
        in_specs=[pl.BlockSpec((TM, k), lambda i: (i, 0)) for k in ks]
        + [pl.BlockSpec((sum(ks), D), lambda i: (0, 0)), row, pl.BlockSpec((1, 1, D), lambda i: (i // tps, 0, 0)), one, one]
        + c_in,
        out_specs=[row, row] + c_out,
        out_shape=[_sds((n, D), F32), _sds((n, D), F32)] + c_shapes,
        scratch_shapes=c_scratch,
        compiler_params=_params(("arbitrary",)),
        name=name,
    )(*parts, w, x, g, lng, lnb, *[a for a, _ in carry])


def postnorm_bwd(dxn, x, y, g, lng, lnb, w, ks, coef, seq, name, carry=()):
    n = x.shape[0]
    ni = n // TM
    tps = seq // TM
    nb = n // seq
    offs = [sum(ks[:k]) for k in range(len(ks))]
    npart = len(ks)
    f = functools.partial(_postnorm, coef=coef)
    nc = len(carry)
    gathers = [gt for _, gt in carry]
    c_in, c_out, c_shapes, c_scratch = _carry_plan(carry)

    def body(dxn_ref, x_ref, y_ref, g_ref, lng_ref, lnb_ref, w_ref, *rest):
        cx, rest = rest[:nc], rest[nc:]
        dx_ref, dy_ref = rest[:2]
        da_refs = rest[2 : 2 + npart]
        dg_ref, dlng_ref, dlnb_ref = rest[2 + npart : 5 + npart]
        co, sems = rest[5 + npart : 5 + npart + nc], rest[5 + npart + nc :]
        i = pl.program_id(0)
        if nc:
            @pl.when(i == 0)
            def _():
                _carry_start(gathers, cx, co, sems)

        @pl.when(i % tps == 0)
        def _():
            dg_ref[...] = jnp.zeros_like(dg_ref)

        @pl.when(i == 0)
        def _():
            dlng_ref[...] = jnp.zeros_like(dlng_ref)
            dlnb_ref[...] = jnp.zeros_like(dlnb_ref)

        for r0 in range(0, TM, SPLIT_ROWS):
            rows = slice(r0, r0 + SPLIT_ROWS)
            _, vjp = jax.vjp(f, x_ref[rows, :], y_ref[rows, :], g_ref[0], lng_ref[...], lnb_ref[...])
            dx, dy, dg, dlng, dlnb = vjp(dxn_ref[rows, :])
            dx_ref[rows, :] = dx
            dyb = dy.astype(BF16)
            dy_ref[rows, :] = dyb
            for da_ref, off, k in zip(da_refs, offs, ks):
                da_ref[rows, :] = lax.dot_general(
                    dyb, w_ref[off : off + k, :], (((1,), (1,)), ((), ())), preferred_element_type=F32
                ).astype(BF16)
            dg_ref[0] += dg
            dlng_ref[...] += dlng
            dlnb_ref[...] += dlnb
        if nc:
            @pl.when(i == ni - 1)
            def _():
                _carry_pass_on(gathers, cx, co, sems)
                _carry_wait(gathers, cx, co, sems)

    row = pl.BlockSpec((TM, D), lambda i: (i, 0))
    one = pl.BlockSpec((1, D), lambda i: (0, 0))
    vec = pl.BlockSpec((1, 1, D), lambda i: (i // tps, 0, 0))
    return pl.pallas_call(
        body,
        grid=(ni,),
        in_specs=[row, row, row, vec, one, one, pl.BlockSpec((sum(ks), D), lambda i: (0, 0))] + c_in,
        out_specs=[row, row] + [pl.BlockSpec((TM, k), lambda i: (i, 0)) for k in ks] + [vec, one, one] + c_out,
        out_shape=[_sds((n, D), F32), _sds((n, D), BF16)]
        + [_sds((n, k), BF16) for k in ks]
        + [_sds((nb, 1, D), F32), _sds((1, D), F32), _sds((1, D), F32)]
        + c_shapes,
        scratch_shapes=c_scratch,
        compiler_params=_params(("arbitrary",)),
        name=name,
    )(dxn, x, y, g, lng, lnb, w, *[a for a, _ in carry])


def _mod_bwd_finish(dh, x_ref, sc_ref, dxres_ref, dx_ref, dsh_ref, dsc_ref, first_of_seq):
    dx_ref[...] = dxres_ref[...] + dh * (1.0 + sc_ref[0])

    @pl.when(first_of_seq)
    def _():
        dsh_ref[...] = jnp.zeros_like(dsh_ref)
        dsc_ref[...] = jnp.zeros_like(dsc_ref)

    dsh_ref[0] += jnp.sum(dh, axis=0, keepdims=True)
    dsc_ref[0] += jnp.sum(dh * x_ref[...], axis=0, keepdims=True)


def ffn_bwd_in(da, g, u, w, x, sc, dxres, seq, name, carry=()):
    n = x.shape[0]
    tn = FF // 2
    nj = FF // tn
    tm = TM // 2
    ni = n // tm
    tps = seq // tm
    nb = n // seq
    k = len(carry)
    gathers = [gt for _, gt in carry]
    c_in, c_out, c_shapes, c_scratch = _carry_plan(carry)

    def body(da_ref, g_ref, u_ref, w_hbm, x_ref, sc_ref, dxres_ref, *rest):
        cx, (dgu_ref, dx_ref, dsh_ref, dsc_ref), co, (w_ref, w_sem, *sems) = rest[:k], rest[k : k + 4], rest[k + 4 : 2 * k + 4], rest[2 * k + 4 :]
        i = pl.program_id(0)

        @pl.when(i == 0)
        def _():
            if k:
                _carry_start(gathers, cx, co, sems)
            cp = pltpu.make_async_copy(w_hbm, w_ref, w_sem)
            cp.start()
            cp.wait()

        nt = (((1,), (1,)), ((), ()))
        dh = None
        for j in range(nj):
            ln = slice(j * tn, (j + 1) * tn)
            gv = g_ref[:, ln].astype(F32)
            uv = u_ref[:, ln].astype(F32)
            dav = da_ref[:, ln].astype(F32)
            s = jax.nn.sigmoid(gv)
            dgv = (dav * uv * s * (1.0 + gv * (1.0 - s))).astype(BF16)
            duv = (dav * gv * s).astype(BF16)
            dgu_ref[:, ln] = dgv
            dgu_ref[:, FF + j * tn : FF + (j + 1) * tn] = duv
            t = lax.dot_general(dgv, w_ref[:, j * tn : (j + 1) * tn], nt, preferred_element_type=F32) + lax.dot_general(
                duv, w_ref[:, FF + j * tn : FF + (j + 1) * tn], nt, preferred_element_type=F32
            )
            dh = t if dh is None else dh + t
        _mod_bwd_finish(dh, x_ref, sc_ref, dxres_ref, dx_ref, dsh_ref, dsc_ref, i % tps == 0)
        if k:
            @pl.when(i == ni - 1)
            def _():
                _carry_wait(gathers, cx, co, sems)

    row = pl.BlockSpec((tm, D), lambda i: (i, 0))
    col = pl.BlockSpec((tm, FF), lambda i: (i, 0))
    vec = pl.BlockSpec((1, 1, D), lambda i: (i // tps, 0, 0))
    return pl.pallas_call(
        body,
        grid=(ni,),
        in_specs=[col, col, col, pl.BlockSpec(memory_space=pltpu.HBM), row, vec, row] + c_in,
        out_specs=[pl.BlockSpec((tm, 2 * FF), lambda i: (i, 0)), row, vec, vec] + c_out,
        out_shape=[_sds((n, 2 * FF), BF16), _sds((n, D), F32), _sds((nb, 1, D), F32), _sds((nb, 1, D), F32)] + c_shapes,
        scratch_shapes=[pltpu.VMEM((D, 2 * FF), BF16), pltpu.SemaphoreType.DMA] + c_scratch,
        compiler_params=_params(("arbitrary",)),
        name=name,
    )(da, g, u, w, x, sc, dxres, *[a for a, _ in carry])


def proj_bwd_in(dparts, w, x, sc, dxres, seq, name, carry=()):
    n = x.shape[0]
    ni = n // TM
    tps = seq // TM
    nb = n // seq
    ms = [p.shape[1] for p in dparts]
    offs = [sum(ms[:k]) for k in range(len(ms))]
    npart = len(ms)
    nc = len(carry)
    gathers = [gt for _, gt in carry]
    c_in, c_out, c_shapes, c_scratch = _carry_plan(carry)

    def body(*refs):
        d_refs = refs[:npart]
        w_ref, x_ref, sc_ref, dxres_ref = refs[npart : npart + 4]
        cx = refs[npart + 4 : npart + 4 + nc]
        dx_ref, dsh_ref, dsc_ref = refs[npart + 4 + nc : npart + 7 + nc]
        co, sems = refs[npart + 7 + nc : npart + 7 + 2 * nc], refs[npart + 7 + 2 * nc :]
        i = pl.program_id(0)
        if nc:
            @pl.when(i == 0)
            def _():
                _carry_start(gathers, cx, co, sems)

        dh = None
        for d_ref, off, m in zip(d_refs, offs, ms):
            t = lax.dot_general(d_ref[...], w_ref[:, off : off + m], (((1,), (1,)), ((), ())), preferred_element_type=F32)
            dh = t if dh is None else dh + t
        _mod_bwd_finish(dh, x_ref, sc_ref, dxres_ref, dx_ref, dsh_ref, dsc_ref, i % tps == 0)
        if nc:
            @pl.when(i == ni - 1)
            def _():
                _carry_pass_on(gathers, cx, co, sems)
                _carry_wait(gathers, cx, co, sems)

    row = pl.BlockSpec((TM, D), lambda i: (i, 0))
    vec = pl.BlockSpec((1, 1, D), lambda i: (i // tps, 0, 0))
    return pl.pallas_call(
        body,
        grid=(ni,),
        in_specs=[pl.BlockSpec((TM, m), lambda i: (i, 0)) for m in ms] + [pl.BlockSpec((D, sum(ms)), lambda i: (0, 0)), row, vec, row] + c_in,
        out_specs=[row, vec, vec] + c_out,
        out_shape=[_sds((n, D), F32), _sds((nb, 1, D), F32), _sds((nb, 1, D), F32)] + c_shapes,
        scratch_shapes=c_scratch,
        compiler_params=_params(("arbitrary",)),
        name=name,
    )(*dparts, w, x, sc, dxres, *[a for a, _ in carry])


def mm_tn(a, b, name):
    n, k1 = a.shape
    k2 = b.shape[1]
    t1 = k1 if k1 <= 1536 else _tile(k1, 1536)
    t2 = k2 if k2 <= 1536 else _tile(k2, 1536)
    tk = 2048 if n % 2048 == 0 else n
    nk = n // tk

    def body(a_ref, b_ref, o_ref, acc):
        t = lax.dot_general(a_ref[...], b_ref[...], (((0,), (0,)), ((), ())), preferred_element_type=F32)

        @pl.when(pl.program_id(2) == 0)
        def _():
            acc[...] = t

        @pl.when(pl.program_id(2) > 0)
        def _():
            acc[...] += t

        @pl.when(pl.program_id(2) == nk - 1)
        def _():
            o_ref[...] = acc[...].astype(BF16)

    return pl.pallas_call(
        body,
        grid=(k1 // t1, k2 // t2, nk),
        in_specs=[pl.BlockSpec((tk, t1), lambda i, j, k: (k, i)), pl.BlockSpec((tk, t2), lambda i, j, k: (k, j))],
        out_specs=pl.BlockSpec((t1, t2), lambda i, j, k: (i, j)),
        out_shape=_sds((k1, k2), BF16),
        scratch_shapes=[pltpu.VMEM((t1, t2), F32)],
        compiler_params=_params(("arbitrary", "arbitrary", "arbitrary")),
        name=name,
    )(a, b)


def _tile(n, cap):
    best = LANE
    for t in range(LANE, cap + 1, LANE):
        if n % t == 0:
            best = t
    return best


def _lane_tiles(ref):
    return [slice(c0, c0 + LANE) for c0 in range(0, ref.shape[1], LANE)]


def _conv_taps(ext_ref, w_ref, halo, tq, kk):
    out = []
    for ln in _lane_tiles(w_ref):
        acc = None
        for k in range(kk):
            t = w_ref[k : k + 1, ln] * ext_ref[pl.ds(halo - (kk - 1 - k), tq), ln]
            acc = t if acc is None else acc + t
        out.append(acc)
    return jnp.concatenate(out, axis=1)


def _conv_taps_t(ext2_ref, w_ref, tq, kk):
    out = []
    for ln in _lane_tiles(w_ref):
        acc = None
        for k in range(kk):
            t = w_ref[k : k + 1, ln] * ext2_ref[pl.ds(kk - 1 - k, tq), ln]
            acc = t if acc is None else acc + t
        out.append(acc)
    return jnp.concatenate(out, axis=1)


def _conv_dw(ext_ref, ext2_ref, dw_ref, halo, tq, kk):
    for ln in _lane_tiles(dw_ref):
        dy = ext2_ref[pl.ds(0, tq), ln]
        for k in range(kk):
            dw_ref[k : k + 1, ln] += jnp.sum(dy * ext_ref[pl.ds(halo - (kk - 1 - k), tq), ln], axis=0, keepdims=True)


def _halo_spec(rows, width, tq, shift):
    per = tq // rows

    if shift < 0:
        return lambda nblocks: pl.BlockSpec((rows, width), lambda i: (jnp.maximum(i * per - 1, 0), 0))
    return lambda nblocks: pl.BlockSpec((rows, width), lambda i: (jnp.minimum((i + 1) * per, nblocks - 1), 0))


def _ln(c, g, b):
    mu = jnp.mean(c, axis=-1, keepdims=True)
    cc = c - mu
    var = jnp.mean(cc * cc, axis=-1, keepdims=True)
    return cc * lax.rsqrt(var + EPS) * g + b


def _silu(v):
    return v * jax.nn.sigmoid(v)


@jax.custom_vjp
def _expand(v, e):
    hi = v.astype(BF16)
    r1 = v - hi.astype(F32)
    mid = r1.astype(BF16)
    lo = (r1 - mid.astype(F32)).astype(BF16)
    return (jnp.dot(hi, e, preferred_element_type=F32) + jnp.dot(mid, e, preferred_element_type=F32)
            + jnp.dot(lo, e, preferred_element_type=F32))


def _expand_fwd(v, e):
    return _expand(v, e), e


def _expand_bwd(e, g):
    nt = (((1,), (1,)), ((), ()))
    hi = g.astype(BF16)
    mid = (g - hi.astype(F32)).astype(BF16)
    dv = lax.dot_general(hi, e, nt, preferred_element_type=F32) + lax.dot_general(mid, e, nt, preferred_element_type=F32)
    return dv, jnp.zeros_like(e)


_expand.defvjp(_expand_fwd, _expand_bwd)


def _ssd_chunk(conv, dtr, z, s, dt_bias, a_log, dskip, norm_g, tril, e):
    q = SSD_Q
    act = _silu(conv)
    xs, bm, cm = act[:, :1024], act[:, 1024:1280], act[:, 1280:1536]
    lane = lax.broadcasted_iota(jnp.int32, (1, LANE), 1)
    lane_q = lax.broadcasted_iota(jnp.int32, (q, LANE), 1)
    sub_q = lax.broadcasted_iota(jnp.int32, (LANE, q), 0)
    causal = lax.broadcasted_iota(jnp.int32, (q, q), 0) >= lax.broadcasted_iota(jnp.int32, (q, q), 1)
    real = lane < SSD_HEADS
    dt = jnp.where(real, jax.nn.softplus(dtr + dt_bias), 0.0)
    a = jnp.where(real, -jnp.exp(a_log), 0.0)
    da = dt * a
    acs = jnp.dot(tril, da, precision=HI, preferred_element_type=F32)
    acs_t = lax.dot_general(da, tril, (((0,), (1,)), ((), ())), precision=HI, preferred_element_type=F32)
    dt_e = _expand(dt, e)
    acs_e = _expand(acs, e)
    one8 = jnp.ones((SUB, 1), F32)
    alast_e = _expand(one8 * jnp.sum(da, axis=0, keepdims=True), e)[0:1]
    d_e = _expand(one8 * jnp.where(real, dskip, 0.0), e)[0:1]
    xdt = xs * dt_e
    nt = (((1,), (1,)), ((), ()))
    tn = (((0,), (0,)), ((), ()))
    ys, snews = [], []
    for g in range(2):
        gl = slice(g * 512, (g + 1) * 512)
        bg = bm[:, g * 128 : (g + 1) * 128].astype(BF16)
        cg = cm[:, g * 128 : (g + 1) * 128].astype(BF16)
        cb = lax.dot_general(cg, bg, nt, preferred_element_type=F32)
        sg = s[:, gl]
        yoff = jnp.dot(cg, sg.astype(BF16), preferred_element_type=F32) * jnp.exp(acs_e[:, gl])
        pairs = []
        for j in range(4):
            xp = xdt[:, g * 512 + j * 128 : g * 512 + (j + 1) * 128].astype(BF16)
            outs = []
            for hh in (g * 8 + 2 * j, g * 8 + 2 * j + 1):
                col = jnp.sum(jnp.where(lane_q == hh, acs, 0.0), axis=1, keepdims=True)
                row = jnp.sum(jnp.where(sub_q == hh, acs_t, 0.0), axis=0, keepdims=True)
                m = cb * jnp.exp(jnp.where(causal, col - row, -1e30))
                outs.append(jnp.dot(m.astype(BF16), xp, preferred_element_type=F32))
            pairs.append(jnp.where(lane_q < SSD_P, outs[0], outs[1]))
        ys.append(jnp.concatenate(pairs, axis=1) + yoff)
        decay = jnp.exp(alast_e[:, gl] - acs_e[:, gl])
        snews.append(
            sg * jnp.exp(alast_e[:, gl]) + lax.dot_general(bg, (xdt[:, gl] * decay).astype(BF16), tn, preferred_element_type=F32)
        )
    y = jnp.concatenate(ys, axis=1) + xs * d_e
    gated = y * _silu(z)
    out = gated * lax.rsqrt(jnp.mean(gated * gated, axis=-1, keepdims=True) + EPS) * norm_g
    return out, jnp.concatenate(snews, axis=1)


def _ssd_consts():
    tril = (lax.broadcasted_iota(jnp.int32, (SSD_Q, SSD_Q), 0) >= lax.broadcasted_iota(jnp.int32, (SSD_Q, SSD_Q), 1)).astype(F32)
    e = (lax.broadcasted_iota(jnp.int32, (LANE, 1024), 0) == lax.broadcasted_iota(jnp.int32, (LANE, 1024), 1) // SSD_P).astype(BF16)
    return tril, e


def ssd_fwd(z, xbc, dtr, cw, cb, dt_bias, a_log, dskip, norm_g, seq, name):
    n = z.shape[0]
    q = SSD_Q
    nc = seq // q
    tril, e = _ssd_consts()

    def body(z_ref, xbc_ref, halo_ref, dtr_ref, cw_ref, cb_ref, dtb_ref, alog_ref, dsk_ref, ng_ref, tril_ref, e_ref, y_ref, sprev_ref, s_scr, ext):
        c = pl.program_id(0) % nc

        @pl.when(c == 0)
        def _():
            s_scr[...] = jnp.zeros_like(s_scr)

        ext[0:SUB, :] = jnp.where(c == 0, 0.0, halo_ref[...])
        ext[SUB:, :] = xbc_ref[...]
        conv = _conv_taps(ext, cw_ref, SUB, q, 4) + cb_ref[...]
        sprev_ref[0] = s_scr[...]
        y, snew = _ssd_chunk(conv, dtr_ref[...], z_ref[...], s_scr[...], dtb_ref[...], alog_ref[...], dsk_ref[...], ng_ref[...], tril_ref[...], e_ref[...])
        y_ref[...] = y.astype(BF16)
        s_scr[...] = snew

    def full(shape):
        return pl.BlockSpec(shape, lambda i: (0,) * len(shape))

    return pl.pallas_call(
        body,
        grid=(n // q,),
        in_specs=[
            pl.BlockSpec((q, 1024), lambda i: (i, 0)),
            pl.BlockSpec((q, SSD_XBC), lambda i: (i, 0)),
            _halo_spec(SUB, SSD_XBC, q, -1)(n // SUB),
            pl.BlockSpec((q, LANE), lambda i: (i, 0)),
            full((4, SSD_XBC)),
            full((1, SSD_XBC)),
            full((1, LANE)),
            full((1, LANE)),
            full((1, LANE)),
            full((1, 1024)),
            full((q, q)),
            full((LANE, 1024)),
        ],
        out_specs=[pl.BlockSpec((q, 1024), lambda i: (i, 0)), pl.BlockSpec((1, LANE, 1024), lambda i: (i, 0, 0))],
        out_shape=[_sds((n, 1024), BF16), _sds((n // q, LANE, 1024), F32)],
        scratch_shapes=[pltpu.VMEM((LANE, 1024), F32), pltpu.VMEM((SUB + q, SSD_XBC), F32)],
        compiler_params=_params(("arbitrary",)),
        name=name,
    )(z, xbc, xbc, dtr, cw, cb, dt_bias, a_log, dskip, norm_g, tril, e)


def ssd_bwd(dy, z, xbc, dtr, sprev, cw, cb, dt_bias, a_log, dskip, norm_g, seq, name):
    n = z.shape[0]
    q = SSD_Q
    nc = seq // q
    nchunks = n // q
    tril, e = _ssd_consts()

    def rev(i):
        return (i // nc) * nc + (nc - 1 - i % nc)

    def body(dy_ref, z_ref, xbc_ref, halo_ref, dtr_ref, sprev_ref, cw_ref, cb_ref, dtb_ref, alog_ref, dsk_ref, ng_ref, tril_ref, e_ref,
             dz_ref, dxbc_ref, ddt_ref, dcw_ref, dcb_ref, ddtb_ref, dalog_ref, ddsk_ref, dng_ref, ds_scr, ext, ext2):
        i = pl.program_id(0)
        step = i % nc
        c = nc - 1 - step

        @pl.when(step == 0)
        def _():
            ds_scr[...] = jnp.zeros_like(ds_scr)
            ext2[q:, :] = jnp.zeros((SUB, SSD_XBC), F32)

        @pl.when(i == 0)
        def _():
            for r in (dcw_ref, dcb_ref, ddtb_ref, dalog_ref, ddsk_ref, dng_ref):
                r[...] = jnp.zeros_like(r)

        ext[0:SUB, :] = jnp.where(c == 0, 0.0, halo_ref[...])
        ext[SUB:, :] = xbc_ref[...]
        conv = _conv_taps(ext, cw_ref, SUB, q, 4) + cb_ref[...]
        tril_v, e_v = tril_ref[...], e_ref[...]

        def f(conv, dtr, z, s, dtb, alog, dsk, ng):
            return _ssd_chunk(conv, dtr, z, s, dtb, alog, dsk, ng, tril_v, e_v)

        _, vjp = jax.vjp(f, conv, dtr_ref[...], z_ref[...], sprev_ref[0], dtb_ref[...], alog_ref[...], dsk_ref[...], ng_ref[...])
        dconv, ddtr, dz, dsprev, ddtb, dalog, ddsk, dng = vjp((dy_ref[...].astype(F32), ds_scr[...]))
        ds_scr[...] = dsprev
        dz_ref[...] = dz.astype(BF16)
        ddt_ref[...] = ddtr.astype(BF16)
        ext2[0:q, :] = dconv
        dxbc_ref[...] = _conv_taps_t(ext2, cw_ref, q, 4).astype(BF16)
        ext2[q:, :] = dconv[0:SUB, :]
        _conv_dw(ext, ext2, dcw_ref, SUB, q, 4)
        dcb_ref[...] += jnp.sum(dconv, axis=0, keepdims=True)
        ddtb_ref[...] += ddtb
        dalog_ref[...] += dalog
        ddsk_ref[...] += ddsk
        dng_ref[...] += dng

    def full(shape):
        return pl.BlockSpec(shape, lambda i: (0,) * len(shape))

    per = q // SUB
    return pl.pallas_call(
        body,
        grid=(nchunks,),
        in_specs=[
            pl.BlockSpec((q, 1024), lambda i: (rev(i), 0)),
            pl.BlockSpec((q, 1024), lambda i: (rev(i), 0)),
            pl.BlockSpec((q, SSD_XBC), lambda i: (rev(i), 0)),
            pl.BlockSpec((SUB, SSD_XBC), lambda i: (jnp.maximum(rev(i) * per - 1, 0), 0)),
            pl.BlockSpec((q, LANE), lambda i: (rev(i), 0)),
            pl.BlockSpec((1, LANE, 1024), lambda i: (rev(i), 0, 0)),
            full((4, SSD_XBC)),
            full((1, SSD_XBC)),
            full((1, LANE)),
            full((1, LANE)),
            full((1, LANE)),
            full((1, 1024)),
            full((q, q)),
            full((LANE, 1024)),
        ],
        out_specs=[
            pl.BlockSpec((q, 1024), lambda i: (rev(i), 0)),
            pl.BlockSpec((q, SSD_XBC), lambda i: (rev(i), 0)),
            pl.BlockSpec((q, LANE), lambda i: (rev(i), 0)),
            full((4, SSD_XBC)),
            full((1, SSD_XBC)),
            full((1, LANE)),
            full((1, LANE)),
            full((1, LANE)),
            full((1, 1024)),
        ],
        out_shape=[
            _sds((n, 1024), BF16),
            _sds((n, SSD_XBC), BF16),
            _sds((n, LANE), BF16),
            _sds((4, SSD_XBC), F32),
            _sds((1, SSD_XBC), F32),
            _sds((1, LANE), F32),
            _sds((1, LANE), F32),
            _sds((1, LANE), F32),
            _sds((1, 1024), F32),
        ],
        scratch_shapes=[pltpu.VMEM((LANE, 1024), F32), pltpu.VMEM((SUB + q, SSD_XBC), F32), pltpu.VMEM((q + SUB, SSD_XBC), F32)],
        compiler_params=_params(("arbitrary",)),
        name=name,
    )(dy, z, xbc, xbc, dtr, sprev, cw, cb, dt_bias, a_log, dskip, norm_g, tril, e)


POOL_HALO = 16
TQ = 512


def _pool_count(pos, w):
    return jnp.minimum(pos + 1.0, float(w))


def pool_fwd(u, pw, scale, seq, name):
    n = u.shape[0]
    tq, halo = TQ, POOL_HALO
    tps = seq // tq

    def body(u_ref, halo_ref, pw_ref, sc_ref, y_ref, ext):
        t0 = pl.program_id(0) % tps
        ext[0:halo, :] = jnp.where(t0 == 0, 0.0, halo_ref[...])
        ext[halo:, :] = u_ref[...]
        pos = (t0 * tq + lax.broadcasted_iota(jnp.int32, (tq, 1), 0)).astype(F32)
        for g, w in enumerate(POOL_WINDOWS):
            ln = slice(g * LANE, (g + 1) * LANE)
            acc = ext[pl.ds(halo, tq), ln]
            for j in range(1, w):
                acc = acc + ext[pl.ds(halo - j, tq), ln]
            pooled = acc / _pool_count(pos, w) - u_ref[:, ln]
            mixed = jnp.dot(pooled.astype(BF16), pw_ref[g].astype(BF16), preferred_element_type=F32)
            y_ref[:, ln] = (mixed * sc_ref[:, ln]).astype(BF16)

    return pl.pallas_call(
        body,
        grid=(n // tq,),
        in_specs=[
            pl.BlockSpec((tq, POOL_DIM), lambda i: (i, 0)),
            _halo_spec(halo, POOL_DIM, tq, -1)(n // halo),
            pl.BlockSpec((4, LANE, LANE), lambda i: (0, 0, 0)),
            pl.BlockSpec((1, POOL_DIM), lambda i: (0, 0)),
        ],
        out_specs=pl.BlockSpec((tq, POOL_DIM), lambda i: (i, 0)),
        out_shape=_sds((n, POOL_DIM), BF16),
        scratch_shapes=[pltpu.VMEM((halo + tq, POOL_DIM), F32)],
        compiler_params=_params(("arbitrary",)),
        name=name,
    )(u, u, pw, scale)


def pool_bwd(dy, u, pw, scale, seq, name):
    n = u.shape[0]
    tq, halo = TQ, POOL_HALO
    tps = seq // tq
    nt = (((1,), (1,)), ((), ()))
    tn = (((0,), (0,)), ((), ()))

    def body(dy_ref, dyn_ref, u_ref, halo_ref, pw_ref, sc_ref, du_ref, dpw_ref, dsc_ref, ext, ext2):
        i = pl.program_id(0)
        t0 = i % tps

        @pl.when(i == 0)
        def _():
            dpw_ref[...] = jnp.zeros_like(dpw_ref)
            dsc_ref[...] = jnp.zeros_like(dsc_ref)

        ext[0:halo, :] = jnp.where(t0 == 0, 0.0, halo_ref[...])
        ext[halo:, :] = u_ref[...]
        pos = (t0 * tq + lax.broadcasted_iota(jnp.int32, (tq, 1), 0)).astype(F32)
        dyv = dy_ref[...].astype(F32)
        dynv = jnp.where(t0 == tps - 1, 0.0, dyn_ref[...].astype(F32))
        for g, w in enumerate(POOL_WINDOWS):
            ln = slice(g * LANE, (g + 1) * LANE)
            wg = pw_ref[g].astype(BF16)
            acc = ext[pl.ds(halo, tq), ln]
            for j in range(1, w):
                acc = acc + ext[pl.ds(halo - j, tq), ln]
            pooled = (acc / _pool_count(pos, w) - u_ref[:, ln]).astype(BF16)
            mixed = jnp.dot(pooled, wg, preferred_element_type=F32)
            dsc_ref[:, ln] += jnp.sum(dyv[:, ln] * mixed, axis=0, keepdims=True)
            dmix = (dyv[:, ln] * sc_ref[:, ln]).astype(BF16)
            dpw_ref[g] += lax.dot_general(pooled, dmix, tn, preferred_element_type=F32)
            dpool = lax.dot_general(dmix, wg, nt, preferred_element_type=F32)
            dmix_n = (dynv[:, ln] * sc_ref[:, ln]).astype(BF16)
            dpool_n = lax.dot_general(dmix_n, wg, nt, preferred_element_type=F32)
            ext2[0:tq, ln] = dpool / _pool_count(pos, w)
            ext2[tq:, ln] = dpool_n * (1.0 / w)
            acc2 = ext2[pl.ds(0, tq), ln]
            for j in range(1, w):
                acc2 = acc2 + ext2[pl.ds(j, tq), ln]
            du_ref[:, ln] = (acc2 - dpool).astype(BF16)

    return pl.pallas_call(
        body,
        grid=(n // tq,),
        in_specs=[
            pl.BlockSpec((tq, POOL_DIM), lambda i: (i, 0)),
            _halo_spec(halo, POOL_DIM, tq, +1)(n // halo),
            pl.BlockSpec((tq, POOL_DIM), lambda i: (i, 0)),
            _halo_spec(halo, POOL_DIM, tq, -1)(n // halo),
            pl.BlockSpec((4, LANE, LANE), lambda i: (0, 0, 0)),
            pl.BlockSpec((1, POOL_DIM), lambda i: (0, 0)),
        ],
        out_specs=[
            pl.BlockSpec((tq, POOL_DIM), lambda i: (i, 0)),
            pl.BlockSpec((4, LANE, LANE), lambda i: (0, 0, 0)),
            pl.BlockSpec((1, POOL_DIM), lambda i: (0, 0)),
        ],
        out_shape=[_sds((n, POOL_DIM), BF16), _sds((4, LANE, LANE), F32), _sds((1, POOL_DIM), F32)],
        scratch_shapes=[pltpu.VMEM((halo + tq, POOL_DIM), F32), pltpu.VMEM((tq + halo, POOL_DIM), F32)],
        compiler_params=_params(("arbitrary",)),
        name=name,
    )(dy, dy, u, u, pw, scale)


CONF_HALO = 32
TQC = 256


def _conf_post(c, g, b):
    return _silu(_ln(c, g, b))


def conf_fwd(vg, w, b, lng, lnb, seq, name):
    n = vg.shape[0]
    tq, halo, kk = TQC, CONF_HALO, CONF_K
    tps = seq // tq
    c = CONF_DIM

    def body(vg_ref, halo_ref, w_ref, b_ref, lng_ref, lnb_ref, y_ref, conv_ref, ext):
        t0 = pl.program_id(0) % tps
        hv = halo_ref[...]
        ext[0:halo, :] = jnp.where(t0 == 0, 0.0, hv[:, :c] * jax.nn.sigmoid(hv[:, c:]))
        ext[halo:, :] = vg_ref[:, :c] * jax.nn.sigmoid(vg_ref[:, c:])
        conv = _conv_taps(ext, w_ref, halo, tq, kk) + b_ref[...]
        conv_ref[...] = conv
        y_ref[...] = _conf_post(conv, lng_ref[...], lnb_ref[...]).astype(BF16)

    one = pl.BlockSpec((1, c), lambda i: (0, 0))
    return pl.pallas_call(
        body,
        grid=(n // tq,),
        in_specs=[pl.BlockSpec((tq, 2 * c), lambda i: (i, 0)), _halo_spec(halo, 2 * c, tq, -1)(n // halo), pl.BlockSpec((kk, c), lambda i: (0, 0)), one, one, one],
        out_specs=[pl.BlockSpec((tq, c), lambda i: (i, 0)), pl.BlockSpec((tq, c), lambda i: (i, 0))],
        out_shape=[_sds((n, c), BF16), _sds((n, c), F32)],
        scratch_shapes=[pltpu.VMEM((halo + tq, c), F32)],
        compiler_params=_params(("arbitrary",)),
        name=name,
    )(vg, vg, w, b, lng, lnb)


def conf_bwd(dy, conv, vg, w, lng, lnb, seq, name):
    n = vg.shape[0]
    tq, halo, kk = TQC, CONF_HALO, CONF_K
    tps = seq // tq
    c = CONF_DIM

    def body(dy_ref, dyn_ref, conv_ref, convn_ref, vg_ref, halo_ref, w_ref, lng_ref, lnb_ref, dvg_ref, dw_ref, db_ref, dlng_ref, dlnb_ref, ext, ext2):
        i = pl.program_id(0)
        t0 = i % tps

        @pl.when(i == 0)
        def _():
            for r in (dw_ref, db_ref, dlng_ref, dlnb_ref):
                r[...] = jnp.zeros_like(r)

        _, vjp = jax.vjp(_conf_post, conv_ref[...], lng_ref[...], lnb_ref[...])
        dconv, dlng, dlnb = vjp(dy_ref[...].astype(F32))
        _, vjpn = jax.vjp(_conf_post, convn_ref[...], lng_ref[...], lnb_ref[...])
        dconv_n = vjpn(dyn_ref[...].astype(F32))[0]
        ext2[0:tq, :] = dconv
        ext2[tq:, :] = jnp.where(t0 == tps - 1, 0.0, dconv_n)
        dh = _conv_taps_t(ext2, w_ref, tq, kk)
        hv = halo_ref[...]
        ext[0:halo, :] = jnp.where(t0 == 0, 0.0, hv[:, :c] * jax.nn.sigmoid(hv[:, c:]))
        v = vg_ref[:, :c]
        s = jax.nn.sigmoid(vg_ref[:, c:])
        ext[halo:, :] = v * s
        _conv_dw(ext, ext2, dw_ref, halo, tq, kk)
        db_ref[...] += jnp.sum(dconv, axis=0, keepdims=True)
        dlng_ref[...] += dlng
        dlnb_ref[...] += dlnb
        dvg_ref[:, :c] = (dh * s).astype(BF16)
        dvg_ref[:, c:] = (dh * v * s * (1.0 - s)).astype(BF16)

    one = pl.BlockSpec((1, c), lambda i: (0, 0))
    tile = pl.BlockSpec((tq, c), lambda i: (i, 0))
    nxt = _halo_spec(halo, c, tq, +1)(n // halo)
    return pl.pallas_call(
        body,
        grid=(n // tq,),
        in_specs=[tile, nxt, tile, nxt, pl.BlockSpec((tq, 2 * c), lambda i: (i, 0)), _halo_spec(halo, 2 * c, tq, -1)(n // halo),
                  pl.BlockSpec((kk, c), lambda i: (0, 0)), one, one],
        out_specs=[pl.BlockSpec((tq, 2 * c), lambda i: (i, 0)), pl.BlockSpec((kk, c), lambda i: (0, 0)), one, one, one],
        out_shape=[_sds((n, 2 * c), BF16), _sds((kk, c), F32), _sds((1, c), F32), _sds((1, c), F32), _sds((1, c), F32)],
        scratch_shapes=[pltpu.VMEM((halo + tq, c), F32), pltpu.VMEM((tq + halo, c), F32)],
        compiler_params=_params(("arbitrary",)),
        name=name,
    )(dy, dy, conv, conv, vg, vg, w, lng, lnb)


TL = 256


def _expm1_neg(t):
    p = t * (1.0 + t * (1.0 / 2 + t * (1.0 / 6 + t * (1.0 / 24 + t * (1.0 / 120 + t * (1.0 / 720 + t * (1.0 / 5040)))))))
    return jnp.where(t > -0.35, p, jnp.exp(t) - 1.0)


def _lru_gate(xc, ra, ia, ba, bx, lam):
    r = jax.nn.sigmoid(ra + ba)
    i = jax.nn.sigmoid(ia + bx)
    log_a = -LRU_C * r * jax.nn.softplus(-lam)
    return jnp.exp(log_a), jnp.sqrt(-_expm1_neg(2.0 * log_a)) * (i * xc)


def _lru_out(h, gr):
    return h * jax.nn.gelu(gr)


def _scan_rows(a, b, tq, reverse):
    r8 = lax.broadcasted_iota(jnp.int32, (tq, 1), 0) % SUB
    for d in (1, 2, 4):
        sh = tq - d if reverse else d
        valid = (r8 < SUB - d) if reverse else (r8 >= d)
        a_s = pltpu.roll(a, sh, 0)
        b_s = pltpu.roll(b, sh, 0)
        b = jnp.where(valid, a * b_s, 0.0) + b
        a = jnp.where(valid, a * a_s, a)
    ng = tq // SUB
    edge = 0 if reverse else SUB - 1
    out_a, out_b = [None] * ng, [None] * ng
    ca = cb = None
    for g in (reversed(range(ng)) if reverse else range(ng)):
        ag, bg = a[g * SUB : (g + 1) * SUB, :], b[g * SUB : (g + 1) * SUB, :]
        if ca is not None:
            bg = bg + ag * cb
            ag = ag * ca
        out_a[g], out_b[g] = ag, bg
        ca, cb = ag[edge : edge + 1, :], bg[edge : edge + 1, :]
    return jnp.concatenate(out_a, axis=0), jnp.concatenate(out_b, axis=0)


def _row_of(v, r, tq):
    row = lax.broadcasted_iota(jnp.int32, (tq, 1), 0)
    return jnp.sum(jnp.where(row == r, v, 0.0), axis=0, keepdims=True)


def _head_mm(xc, w_ref):
    return jnp.concatenate(
        [
            jnp.dot(xc[:, h * LANE : (h + 1) * LANE].astype(BF16), w_ref[h].astype(BF16), preferred_element_type=F32)
            for h in range(LRU_HEADS)
        ],
        axis=1,
    )


def lru_fwd(xr, gr, cw, cb, wa, ba, wx, bx, lam, seq, name):
    n = xr.shape[0]
    tq = TL
    tps = seq // tq
    c = LRU_DIM

    def body(xr_ref, halo_ref, gr_ref, cw_ref, cb_ref, wa_ref, ba_ref, wx_ref, bx_ref, lam_ref, y_ref, h_ref, hc, ext):
        t0 = pl.program_id(0) % tps

        @pl.when(t0 == 0)
        def _():
            hc[...] = jnp.zeros_like(hc)

        ext[0:SUB, :] = jnp.where(t0 == 0, 0.0, halo_ref[...])
        ext[SUB:, :] = xr_ref[...]
        xc = _conv_taps(ext, cw_ref, SUB, tq, 4) + cb_ref[...]
        a, b = _lru_gate(xc, _head_mm(xc, wa_ref), _head_mm(xc, wx_ref), ba_ref[...], bx_ref[...], lam_ref[...])
        acum, h0 = _scan_rows(a, b, tq, False)
        h = h0 + acum * hc[0:1, :]
        h_ref[...] = h
        hc[0:1, :] = h_ref[tq - 1 : tq, :]
        y_ref[...] = _lru_out(h, gr_ref[...]).astype(BF16)

    one = pl.BlockSpec((1, c), lambda i: (0, 0))
    tile = pl.BlockSpec((tq, c), lambda i: (i, 0))
    hw = pl.BlockSpec((LRU_HEADS, LANE, LANE), lambda i: (0, 0, 0))
    return pl.pallas_call(
        body,
        grid=(n // tq,),
        in_specs=[tile, _halo_spec(SUB, c, tq, -1)(n // SUB), tile, pl.BlockSpec((4, c), lambda i: (0, 0)), one, hw, one, hw, one, one],
        out_specs=[tile, tile],
        out_shape=[_sds((n, c), BF16), _sds((n, c), F32)],
        scratch_shapes=[pltpu.VMEM((SUB, c), F32), pltpu.VMEM((SUB + tq, c), F32)],
        compiler_params=_params(("arbitrary",)),
        name=name,
    )(xr, xr, gr, cw, cb, wa, ba, wx, bx, lam)


def lru_bwd(dy, xr, gr, h, cw, cb, wa, ba, wx, bx, lam, seq, name):
    n = xr.shape[0]
    tq = TL
    tps = seq // tq
    ntile = n // tq
    c = LRU_DIM
    per = tq // SUB
    nt = (((1,), (1,)), ((), ()))
    tn = (((0,), (0,)), ((), ()))

    def rev(i):
        return (i // tps) * tps + (tps - 1 - i % tps)

    def body(dy_ref, xr_ref, halo_ref, gr_ref, h_ref, hprev_ref, cw_ref, cb_ref, wa_ref, ba_ref, wx_ref, bx_ref, lam_ref,
             dxr_ref, dgr_ref, dcw_ref, dcb_ref, dwa_ref, dba_ref, dwx_ref, dbx_ref, dlam_ref, carry, ext, ext2):
        i = pl.program_id(0)
        step = i % tps
        t0 = tps - 1 - step

        @pl.when(step == 0)
        def _():
            carry[...] = jnp.zeros_like(carry)
            ext2[tq:, :] = jnp.zeros((SUB, c), F32)

        @pl.when(i == 0)
        def _():
            for r in (dcw_ref, dcb_ref, dwa_ref, dba_ref, dwx_ref, dbx_ref, dlam_ref):
                r[...] = jnp.zeros_like(r)

        ext[0:SUB, :] = jnp.where(t0 == 0, 0.0, halo_ref[...])
        ext[SUB:, :] = xr_ref[...]
        xc = _conv_taps(ext, cw_ref, SUB, tq, 4) + cb_ref[...]
        (a, _), vjp_gate = jax.vjp(_lru_gate, xc, _head_mm(xc, wa_ref), _head_mm(xc, wx_ref), ba_ref[...], bx_ref[...], lam_ref[...])
        hv = h_ref[...]
        _, vjp_out = jax.vjp(_lru_out, hv, gr_ref[...])
        dh, dgr = vjp_out(dy_ref[...].astype(F32))
        dgr_ref[...] = dgr.astype(BF16)
        row = lax.broadcasted_iota(jnp.int32, (tq, 1), 0)
        a_up = jnp.where(row == tq - 1, carry[0:1, :], pltpu.roll(a, tq - 1, 0))
        acum, l0 = _scan_rows(a_up, dh, tq, True)
        lamv = l0 + acum * carry[1:2, :]
        carry[0:1, :] = _row_of(a, 0, tq)
        carry[1:2, :] = _row_of(lamv, 0, tq)
        hprev = jnp.where(row == 0, jnp.where(t0 == 0, 0.0, hprev_ref[SUB - 1 : SUB, :]), pltpu.roll(hv, 1, 0))
        dxc, dra, dia, dba, dbx, dlam = vjp_gate((lamv * hprev, lamv))
        dba_ref[...] += dba
        dbx_ref[...] += dbx
        dlam_ref[...] += dlam
        pieces = []
        for hh in range(LRU_HEADS):
            ln = slice(hh * LANE, (hh + 1) * LANE)
            xh = xc[:, ln].astype(BF16)
            drh = dra[:, ln].astype(BF16)
            dih = dia[:, ln].astype(BF16)
            dwa_ref[hh] += lax.dot_general(xh, drh, tn, preferred_element_type=F32)
            dwx_ref[hh] += lax.dot_general(xh, dih, tn, preferred_element_type=F32)
            pieces.append(
                lax.dot_general(drh, wa_ref[hh].astype(BF16), nt, preferred_element_type=F32)
                + lax.dot_general(dih, wx_ref[hh].astype(BF16), nt, preferred_element_type=F32)
            )
        dxc = dxc + jnp.concatenate(pieces, axis=1)
        ext2[0:tq, :] = dxc
        dxr_ref[...] = _conv_taps_t(ext2, cw_ref, tq, 4).astype(BF16)
        ext2[tq:, :] = ext2[0:SUB, :]
        _conv_dw(ext, ext2, dcw_ref, SUB, tq, 4)
        dcb_ref[...] += jnp.sum(dxc, axis=0, keepdims=True)

    one = pl.BlockSpec((1, c), lambda i: (0, 0))
    tile = pl.BlockSpec((tq, c), lambda i: (rev(i), 0))
    prev = pl.BlockSpec((SUB, c), lambda i: (jnp.maximum(rev(i) * per - 1, 0), 0))
    hw = pl.BlockSpec((LRU_HEADS, LANE, LANE), lambda i: (0, 0, 0))
    cw4 = pl.BlockSpec((4, c), lambda i: (0, 0))
    return pl.pallas_call(
        body,
        grid=(ntile,),
        in_specs=[tile, tile, prev, tile, tile, prev, cw4, one, hw, one, hw, one, one],
        out_specs=[tile, tile, cw4, one, hw, one, hw, one, one],
        out_shape=[_sds((n, c), BF16), _sds((n, c), BF16), _sds((4, c), F32), _sds((1, c), F32), _sds((LRU_HEADS, LANE, LANE), F32),
                   _sds((1, c), F32), _sds((LRU_HEADS, LANE, LANE), F32), _sds((1, c), F32), _sds((1, c), F32)],
        scratch_shapes=[pltpu.VMEM((SUB, c), F32), pltpu.VMEM((SUB + tq, c), F32), pltpu.VMEM((tq + SUB, c), F32)],
        compiler_params=_params(("arbitrary",)),
        name=name,
    )(dy, xr, xr, gr, h, h, cw, cb, wa, ba, wx, bx, lam)


def ada_fwd(c_all, w, b, name):
    nl, _, cols = w.shape
    nb = c_all.shape[0]

    def body(c_ref, w_ref, b_ref, o_ref):
        sc = _silu(c_ref[...]).astype(BF16)
        o_ref[0] = jnp.dot(sc, w_ref[0].astype(BF16), preferred_element_type=F32) + b_ref[0]

    return pl.pallas_call(
        body,
        grid=(nl,),
        in_specs=[pl.BlockSpec((nb, D), lambda l: (0, 0)), pl.BlockSpec((1, D, cols), lambda l: (l, 0, 0)), pl.BlockSpec((1, 1, cols), lambda l: (l, 0, 0))],
        out_specs=pl.BlockSpec((1, nb, cols), lambda l: (l, 0, 0)),
        out_shape=_sds((nl, nb, cols), F32),
        compiler_params=_params(("arbitrary",)),
        name=name,
    )(c_all, w, b)


def ada_bwd(c_all, dmod, name):
    nl, nb, cols = dmod.shape

    def body(c_ref, d_ref, o_ref):
        sc = _silu(c_ref[...]).astype(BF16)
        o_ref[0] = lax.dot_general(sc, d_ref[0].astype(BF16), (((0,), (0,)), ((), ())), preferred_element_type=F32)

    return pl.pallas_call(
        body,
        grid=(nl,),
        in_specs=[pl.BlockSpec((nb, D), lambda l: (0, 0)), pl.BlockSpec((1, nb, cols), lambda l: (l, 0, 0))],
        out_specs=pl.BlockSpec((1, D, cols), lambda l: (l, 0, 0)),
        out_shape=_sds((nl, D, cols), F32),
        compiler_params=_params(("arbitrary",)),
        name=name,
    )(c_all, dmod)


def loss_grad(y, target, name):
    n = y.shape[0]

    def body(y_ref, t_ref, dy_ref, l_ref, acc):
        i = pl.program_id(0)

        @pl.when(i == 0)
        def _():
            acc[...] = jnp.zeros_like(acc)

        e = y_ref[...] - t_ref[...]
        dy_ref[...] = e * (1.0 / D)
        acc[...] += jnp.sum(e * e, axis=0, keepdims=True)

        @pl.when(i == n // TM - 1)
        def _():
            l_ref[...] = jnp.full((1, LANE), 0.5 / D, F32) * jnp.sum(acc[...])

    row = pl.BlockSpec((TM, D), lambda i: (i, 0))
    return pl.pallas_call(
        body,
        grid=(n // TM,),
        in_specs=[row, row],
        out_specs=[row, pl.BlockSpec((1, LANE), lambda i: (0, 0))],
        out_shape=[_sds((n, D), F32), _sds((1, LANE), F32)],
        scratch_shapes=[pltpu.VMEM((1, D), F32)],
        compiler_params=_params(("arbitrary",)),
        name=name,
    )(y, target)


def sum_parts(parts, name):
    ns, r, _ = parts.shape
    tr = _row_tile(r, 1024)

    def body(p_ref, o_ref):
        acc = p_ref[0]
        for k in range(1, ns):
            acc = acc + p_ref[k]
        o_ref[...] = acc

    return pl.pallas_call(
        body,
        grid=(r // tr,),
        in_specs=[pl.BlockSpec((ns, tr, LANE), lambda i: (0, i, 0))],
        out_specs=pl.BlockSpec((tr, LANE), lambda i: (i, 0)),
        out_shape=_sds((r, LANE), F32),
        compiler_params=_params(("arbitrary",)),
        name=name,
    )(parts)


def _row_tile(r, cap):
    if r <= cap:
        return r
    best = None
    for t in range(16, cap + 1, 16):
        if r % t == 0:
            best = t
    assert best is not None, r
    return best


def adamw(w, m, v, gparts, name):
    r, c = w.shape
    ns = gparts.shape[0]
    tr = _row_tile(r, min(512, 256 * 1024 // c))
    c1 = 1.0 - B1**STEP
    c2 = 1.0 - B2**STEP

    def body(w_ref, m_ref, v_ref, g_ref, go_ref, d_ref, mo_ref, vo_ref):
        g = g_ref[0].astype(F32)
        for k in range(1, ns):
            g = g + g_ref[k].astype(F32)
        mn = B1 * m_ref[...] + (1.0 - B1) * g
        vn = B2 * v_ref[...] + (1.0 - B2) * (g * g)
        go_ref[...] = g
        mo_ref[...] = mn
        vo_ref[...] = vn
        d_ref[...] = -LR * ((mn / c1) / (jnp.sqrt(vn / c2) + AEPS) + WD * w_ref[...])

    tile = pl.BlockSpec((tr, c), lambda i: (i, 0))
    return pl.pallas_call(
        body,
        grid=(r // tr,),
        in_specs=[tile, tile, tile, pl.BlockSpec((ns, tr, c), lambda i: (0, i, 0))],
        out_specs=[tile, tile, tile, tile],
        out_shape=[_sds((r, c), F32)] * 4,
        compiler_params=_params(("arbitrary",)),
        name=name,
    )(w, m, v, gparts)


WEIGHTS = ["ada_w", "ada_b", "ln_g", "ln_b", "ffn_w_in", "ffn_w_out", "ev_w_in", "ssd_conv_w", "ssd_conv_b", "ssd_dt_bias",
           "ssd_a_log", "ssd_d", "ssd_norm_g", "pool_w", "pool_scale", "ev_w_out", "od_w_in", "conf_dw_w", "conf_dw_b",
           "conf_ln_g", "conf_ln_b", "lru_conv_w", "lru_conv_b", "lru_wa", "lru_ba", "lru_wx", "lru_bx", "lru_lambda", "od_w_out"]
BIG = ("ada_w", "ffn_w_in", "ffn_w_out", "ev_w_in", "ev_w_out", "od_w_in", "od_w_out")
SMALL = {
    "ada_b": ((4, 9216), None), "ln_g": ((4, 3, 1024), 2), "ln_b": ((4, 3, 1024), 2),
    "ssd_conv_w": ((2, 4, 1536), 2), "ssd_conv_b": ((2, 1536), None), "ssd_dt_bias": ((2, 16), None),
    "ssd_a_log": ((2, 16), None), "ssd_d": ((2, 16), None), "ssd_norm_g": ((2, 1024), None),
    "pool_w": ((2, 4, 128, 128), None), "pool_scale": ((2, 512), None),
    "conf_dw_w": ((2, 31, 512), 2), "conf_dw_b": ((2, 512), 1), "conf_ln_g": ((2, 512), 1), "conf_ln_b": ((2, 512), 1),
    "lru_conv_w": ((2, 4, 1024), 2), "lru_conv_b": ((2, 1024), 1), "lru_wa": ((2, 8, 128, 128), None),
    "lru_ba": ((2, 1024), 1), "lru_wx": ((2, 8, 128, 128), None), "lru_bx": ((2, 1024), 1), "lru_lambda": ((2, 1024), 1),
}
PACK_ROWS = 2 * SUB * LANE


def _pack(arrs, mult=PACK_ROWS):
    flat = jnp.concatenate([a.reshape(-1) for a in arrs])
    pad = (-flat.shape[0]) % mult
    return jnp.pad(flat, (0, pad)).reshape(-1, LANE)


def _unpack(buf, shapes, lead=()):
    flat = buf.reshape(lead + (-1,))
    out, off = [], 0
    for s in shapes:
        k = math.prod(s)
        out.append(flat[..., off : off + k].reshape(lead + tuple(s)))
        off += k
    return out


def _pad_lanes(v):
    return jnp.pad(v, (0, LANE - v.shape[0]))[None]


def kernel(x, c, ada_w, ada_b, ln_g, ln_b, ffn_w_in, ffn_w_out, ev_w_in, ssd_conv_w, ssd_conv_b, ssd_dt_bias, ssd_a_log, ssd_d, ssd_norm_g, pool_w, pool_scale, ev_w_out, od_w_in, conf_dw_w, conf_dw_b, conf_ln_g, conf_ln_b, lru_conv_w, lru_conv_b, lru_wa, lru_ba, lru_wx, lru_bx, lru_lambda, od_w_out, loss_target, m_ada_w, m_ada_b, m_ln_g, m_ln_b, m_ffn_w_in, m_ffn_w_out, m_ev_w_in, m_ssd_conv_w, m_ssd_conv_b, m_ssd_dt_bias, m_ssd_a_log, m_ssd_d, m_ssd_norm_g, m_pool_w, m_pool_scale, m_ev_w_out, m_od_w_in, m_conf_dw_w, m_conf_dw_b, m_conf_ln_g, m_conf_ln_b, m_lru_conv_w, m_lru_conv_b, m_lru_wa, m_lru_ba, m_lru_wx, m_lru_bx, m_lru_lambda, m_od_w_out, v_ada_w, v_ada_b, v_ln_g, v_ln_b, v_ffn_w_in, v_ffn_w_out, v_ev_w_in, v_ssd_conv_w, v_ssd_conv_b, v_ssd_dt_bias, v_ssd_a_log, v_ssd_d, v_ssd_norm_g, v_pool_w, v_pool_scale, v_ev_w_out, v_od_w_in, v_conf_dw_w, v_conf_dw_b, v_conf_ln_g, v_conf_ln_b, v_lru_conv_w, v_lru_conv_b, v_lru_wa, v_lru_ba, v_lru_wx, v_lru_bx, v_lru_lambda, v_od_w_out):
    p = dict(locals())
    nb, seq, _ = x.shape
    n = nb * seq
    me = 4 * lax.axis_index("x") + 2 * lax.axis_index("y") + lax.axis_index("c")
    sharded = [k for k, (_, ax) in SMALL.items() if ax is not None]

    def cols_of(g):
        return jnp.moveaxis(g, 0, 1).reshape(g.shape[1], N_DEV * g.shape[2])

    def rows_of(g):
        return g.reshape(N_DEV * g.shape[1], g.shape[2])

    def ev_in_of(g):
        w = cols_of(g)
        return jnp.concatenate([w[:, :2560], w[:, 2576:], jnp.pad(w[:, 2560:2576], ((0, 0), (0, LANE - SSD_HEADS)))], axis=1)

    sh_ffn_in, sh_ffn_out = ffn_w_in.astype(BF16), ffn_w_out.astype(BF16)
    sh_mix_in = [ev_w_in.astype(BF16), od_w_in.astype(BF16)]
    sh_mix_out = [ev_w_out.astype(BF16), od_w_out.astype(BF16)]

    def ffn_items(l, i):
        return [(sh_ffn_in[l, i], True), (sh_ffn_out[l, i], True)]

    def mix_items(l):
        return [(sh_mix_in[l % 2][l // 2], True), (sh_mix_out[l % 2][l // 2], True)]

    sm_local_shapes = [p[k].shape for k in sharded]
    g_in, g_out, sm_all = exchange(ffn_items(0, 0) + [(_pack([p[k] for k in sharded] + [c]), True)], "ag_first")
    w_ffn = {(0, 0): (cols_of(g_in), rows_of(g_out))}
    w_mix = {}
    got = _unpack(sm_all, sm_local_shapes + [c.shape], lead=(N_DEV,))
    full = {k: p[k] for k, (_, ax) in SMALL.items() if ax is None}
    for k, g in zip(sharded, got[:-1]):
        full[k] = jnp.moveaxis(g, 0, SMALL[k][1]).reshape(SMALL[k][0])
    c_all = got[-1].reshape(N_DEV * nb, D)

    cols = ada_w.shape[-1]
    ada_b_loc = lax.dynamic_slice_in_dim(ada_b, me * cols, cols, axis=1)[:, None, :]
    mod_cols = ada_fwd(c_all, ada_w, ada_b_loc, "ada_fwd")
    (mod_x,) = exchange([(mod_cols.reshape(DEPTH, N_DEV, nb, cols).transpose(1, 0, 2, 3), False)], "a2a_mod")
    mod = mod_x.transpose(1, 2, 0, 3).reshape(DEPTH, nb, N_MOD, 1, D)

    def vec(l, j):
        return mod[l, :, j]

    def row(a):
        return a[None]

    xs = x.reshape(n, D)
    saved = []
    for l in range(DEPTH):
        s = {"x0": xs}
        e = l // 2
        s["h1"], s["g1"], s["u1"], s["a1"], g_in, g_out = ffn_up(xs, vec(l, 0), vec(l, 1), w_ffn[l, 0][0], seq, "ffn_up_c2", carry=ffn_items(l, 1))
        w_ffn[l, 1] = (cols_of(g_in), rows_of(g_out))
        x1, s["y1"], gm_in, gm_out = mm_postnorm([s["a1"]], w_ffn[l, 0][1], xs, vec(l, 2), row(full["ln_g"][l, 0]), row(full["ln_b"][l, 0]), 0.5, seq,
                                                 "ffn_down_c2", carry=mix_items(l))
        w_mix[l] = ((ev_in_of if l % 2 == 0 else cols_of)(gm_in), rows_of(gm_out))
        s["x1"] = x1
        if l % 2 == 0:
            s["h2"], s["z"], s["xbc"], s["u"], s["dtr"] = mod_mm(x1, vec(l, 3), vec(l, 4), w_mix[l][0], EV_SPLITS, seq, "ev_in")
            s["ya"], s["sprev"] = ssd_fwd(s["z"], s["xbc"], s["dtr"], full["ssd_conv_w"][e], row(full["ssd_conv_b"][e]), _pad_lanes(full["ssd_dt_bias"][e]),
                                          _pad_lanes(full["ssd_a_log"][e]), _pad_lanes(full["ssd_d"][e]), row(full["ssd_norm_g"][e]), seq, "ssd_fwd")
            s["yb"] = pool_fwd(s["u"], full["pool_w"][e], row(full["pool_scale"][e]), seq, "pool_fwd")
        else:
            s["h2"], s["vg"], s["xr"], s["gr"] = mod_mm(x1, vec(l, 3), vec(l, 4), w_mix[l][0], OD_SPLITS, seq, "od_in")
            s["ya"], s["conv"] = conf_fwd(s["vg"], full["conf_dw_w"][e], row(full["conf_dw_b"][e]), row(full["conf_ln_g"][e]), row(full["conf_ln_b"][e]), seq, "conf_fwd")
            s["yb"], s["hst"] = lru_fwd(s["xr"], s["gr"], full["lru_conv_w"][e], row(full["lru_conv_b"][e]), full["lru_wa"][e], row(full["lru_ba"][e]),
                                        full["lru_wx"][e], row(full["lru_bx"][e]), row(full["lru_lambda"][e]), seq, "lru_fwd")
        x2, s["y2"] = mm_postnorm([s["ya"], s["yb"]], w_mix[l][1], x1, vec(l, 5), row(full["ln_g"][l, 1]), row(full["ln_b"][l, 1]), 1.0, seq, "mix_out")
        s["x2"] = x2
        if l + 1 < DEPTH:
            s["h3"], s["g3"], s["u3"], s["a3"], g_in, g_out = ffn_up(x2, vec(l, 6), vec(l, 7), w_ffn[l, 1][0], seq, "ffn_up_c2", carry=ffn_items(l + 1, 0))
            w_ffn[l + 1, 0] = (cols_of(g_in), rows_of(g_out))
        else:
            s["h3"], s["g3"], s["u3"], s["a3"] = ffn_up(x2, vec(l, 6), vec(l, 7), w_ffn[l, 1][0], seq, "ffn_up")
        xs, s["y3"] = mm_postnorm([s["a3"]], w_ffn[l, 1][1], x2, vec(l, 8), row(full["ln_g"][l, 2]), row(full["ln_b"][l, 2]), 0.5, seq, "ffn_down")
        saved.append(s)

    dx, loss_row = loss_grad(xs, loss_target.reshape(n, D), "loss")
    loss = lax.psum(loss_row[0, 0], ("x", "y", "c"))

    sg = {k: [None] * shape[0] for k, (shape, _) in SMALL.items()}
    sg["ln_g"] = [[None] * 3 for _ in range(DEPTH)]
    sg["ln_b"] = [[None] * 3 for _ in range(DEPTH)]
    dmod = [[None] * N_MOD for _ in range(DEPTH)]
    pending, got_w = [], {}

    def cut_cols(g):
        r, cc = g.shape
        return g.reshape(r, N_DEV, cc // N_DEV).transpose(1, 0, 2)

    def cut_rows(g):
        r, cc = g.shape
        return g.reshape(N_DEV, r // N_DEV, cc)

    def take(room):
        sel = []
        for j, (_, a) in enumerate(pending):
            if a.size * a.dtype.itemsize <= room:
                sel.append(j)
                room -= a.size * a.dtype.itemsize
        items = [pending[j] for j in sel]
        pending[:] = [it for j, it in enumerate(pending) if j not in sel]
        return [k for k, _ in items], [(a, False) for _, a in items]

    room_down, room_up, room_mix = 8 * MIB, 12 * MIB, 6 * MIB

    def postnorm_backward(dxo, xin, y, g, lng, lnb, w, ks, coef, name, room):
        keys, carry = take(room)
        outs = postnorm_bwd(dxo, xin, y, g, lng, lnb, w, ks, coef, seq, name + "_c%d" % len(keys), carry=carry)
        got_w.update(zip(keys, outs[5 + len(ks):]))
        return outs[0], outs[1], outs[2 : 2 + len(ks)], outs[2 + len(ks)], outs[3 + len(ks)], outs[4 + len(ks)]

    def proj_backward(dparts, w, xin, scv, dxres, name):
        keys, carry = take(room_mix)
        outs = proj_bwd_in(dparts, w, xin, scv, dxres, seq, name + "_c%d" % len(keys), carry=carry)
        got_w.update(zip(keys, outs[3:]))
        return outs[:3]

    def ffn_backward(l, i, dxo, s, xin, hk, gk, uk, ak, yk, jbase, lnj):
        dxres, dy, (da,), dmod[l][jbase + 2], sg["ln_g"][l][lnj], sg["ln_b"][l][lnj] = postnorm_backward(
            dxo, xin, s[yk], vec(l, jbase + 2), row(full["ln_g"][l, lnj]), row(full["ln_b"][l, lnj]), w_ffn[l, i][1], [FF], 0.5, "ffn_down_bwd", room_down)
        keys, carry = take(room_up)
        outs = ffn_bwd_in(da, s[gk], s[uk], w_ffn[l, i][0], xin, vec(l, jbase + 1), dxres, seq, "ffn_up_bwd_c%d" % len(keys), carry=carry)
        dgu, dxi, dmod[l][jbase], dmod[l][jbase + 1] = outs[:4]
        got_w.update(zip(keys, outs[4:]))
        pending.append((("ffn_out", l, i), cut_rows(mm_tn(s[ak], dy, "wg_ffn_out"))))
        pending.append((("ffn_in", l, i), cut_cols(mm_tn(s[hk], dgu, "wg_ffn_in"))))
        return dxi

    for l in reversed(range(DEPTH)):
        s = saved[l]
        e = l // 2
        dx = ffn_backward(l, 1, dx, s, s["x2"], "h3", "g3", "u3", "a3", "y3", 6, 2)
        ks = [1024, POOL_DIM] if l % 2 == 0 else [CONF_DIM, LRU_DIM]
        dxres, dy, (dya, dyb), dmod[l][5], sg["ln_g"][l][1], sg["ln_b"][l][1] = postnorm_backward(
            dx, s["x1"], s["y2"], vec(l, 5), row(full["ln_g"][l, 1]), row(full["ln_b"][l, 1]), w_mix[l][1], ks, 1.0, "mix_out_bwd", room_mix)
        pending.append((("mix_out", l), cut_rows(jnp.concatenate([mm_tn(s["ya"], dy, "wg_mix_a"), mm_tn(s["yb"], dy, "wg_mix_b")], axis=0))))
        if l % 2 == 0:
            (dz, dxbc, ddt, sg["ssd_conv_w"][e], dcb, ddtb, dalog, ddsk, dng) = ssd_bwd(
                dya, s["z"], s["xbc"], s["dtr"], s["sprev"], full["ssd_conv_w"][e], row(full["ssd_conv_b"][e]), _pad_lanes(full["ssd_dt_bias"][e]),
                _pad_lanes(full["ssd_a_log"][e]), _pad_lanes(full["ssd_d"][e]), row(full["ssd_norm_g"][e]), seq, "ssd_bwd")
            sg["ssd_conv_b"][e], sg["ssd_norm_g"][e] = dcb[0], dng[0]
            sg["ssd_dt_bias"][e], sg["ssd_a_log"][e], sg["ssd_d"][e] = ddtb[0, :SSD_HEADS], dalog[0, :SSD_HEADS], ddsk[0, :SSD_HEADS]
            du, sg["pool_w"][e], dps = pool_bwd(dyb, s["u"], full["pool_w"][e], row(full["pool_scale"][e]), seq, "pool_bwd")
            sg["pool_scale"][e] = dps[0]
            dparts = [dz, dxbc, du, ddt]
            dx, dmod[l][3], dmod[l][4] = proj_backward(dparts, w_mix[l][0], s["x1"], vec(l, 4), dxres, "ev_in_bwd")
            gz, gxbc, gu, gdt = [mm_tn(s["h2"], dp, "wg_ev_in") for dp in dparts]
            pending.append((("mix_in", l), cut_cols(jnp.concatenate([gz, gxbc, gdt[:, :SSD_HEADS], gu], axis=1))))
        else:
            dvg, sg["conf_dw_w"][e], dcb, dlg, dlb = conf_bwd(dya, s["conv"], s["vg"], full["conf_dw_w"][e], row(full["conf_ln_g"][e]), row(full["conf_ln_b"][e]), seq, "conf_bwd")
            sg["conf_dw_b"][e], sg["conf_ln_g"][e], sg["conf_ln_b"][e] = dcb[0], dlg[0], dlb[0]
            (dxr, dgr, sg["lru_conv_w"][e], dcb, sg["lru_wa"][e], dba, sg["lru_wx"][e], dbx, dlam) = lru_bwd(
                dyb, s["xr"], s["gr"], s["hst"], full["lru_conv_w"][e], row(full["lru_conv_b"][e]), full["lru_wa"][e], row(full["lru_ba"][e]),
                full["lru_wx"][e], row(full["lru_bx"][e]), row(full["lru_lambda"][e]), seq, "lru_bwd")
            sg["lru_conv_b"][e], sg["lru_ba"][e], sg["lru_bx"][e], sg["lru_lambda"][e] = dcb[0], dba[0], dbx[0], dlam[0]
            dparts = [dvg, dxr, dgr]
            dx, dmod[l][3], dmod[l][4] = proj_backward(dparts, w_mix[l][0], s["x1"], vec(l, 4), dxres, "od_in_bwd")
            pending.append((("mix_in", l), cut_cols(jnp.concatenate([mm_tn(s["h2"], dp, "wg_od_in") for dp in dparts], axis=1))))
        dx = ffn_backward(l, 0, dx, s, s["x0"], "h1", "g1", "u1", "a1", "y1", 0, 0)
    grad_x = dx.reshape(nb, seq, D)

    dmod_mine = jnp.stack([jnp.concatenate([d[:, 0, :] for d in dmod[l]], axis=-1) for l in range(DEPTH)])
    sg["ada_b"] = [jnp.sum(dmod_mine[l], axis=0) for l in range(DEPTH)]
    sg["ln_g"] = [jnp.concatenate(r, axis=0) for r in sg["ln_g"]]
    sg["ln_b"] = [jnp.concatenate(r, axis=0) for r in sg["ln_b"]]
    small_names = list(SMALL)
    sg_packed = _pack([jnp.stack(sg[k]).reshape(SMALL[k][0]) for k in small_names], N_DEV * PACK_ROWS)
    keys, carry = take(1 << 40)
    outs = exchange(carry + [(dmod_mine.reshape(DEPTH, nb, N_DEV, cols).transpose(2, 0, 1, 3), False),
                             (sg_packed.reshape(N_DEV, -1, LANE), False)], "x_last")
    got_w.update(zip(keys, outs))
    dmod_x, parts = outs[len(keys):]
    g_ada_w = ada_bwd(c_all, dmod_x.transpose(1, 0, 2, 3).reshape(DEPTH, N_DEV * nb, cols), "ada_bwd")

    (sg_sum,) = exchange([(sum_parts(parts, "sum_smallgrad"), True)], "ag_smallsum")
    summed = _unpack(sg_sum.reshape(-1, LANE), [SMALL[k][0] for k in small_names])
    grads = {}
    for k, g in zip(small_names, summed):
        ax = SMALL[k][1]
        grads[k] = g if ax is None else lax.dynamic_slice_in_dim(g, me * p[k].shape[ax], p[k].shape[ax], axis=ax)
    loc_shapes = [p[k].shape for k in small_names]
    whole = 512 * LANE
    _, d_s, m_s, v_s = adamw(_pack([p[k] for k in small_names], whole), _pack([p["m_" + k] for k in small_names], whole),
                             _pack([p["v_" + k] for k in small_names], whole), _pack([grads[k] for k in small_names], whole)[None], "adamw_small")
    delta = dict(zip(small_names, _unpack(d_s, loc_shapes)))
    new_m = dict(zip(small_names, _unpack(m_s, loc_shapes)))
    new_v = dict(zip(small_names, _unpack(v_s, loc_shapes)))

    big_parts = {
        "ada_w": g_ada_w[None],
        "ffn_w_in": jnp.stack([jnp.stack([got_w["ffn_in", l, i] for i in range(2)], axis=1) for l in range(DEPTH)], axis=1),
        "ffn_w_out": jnp.stack([jnp.stack([got_w["ffn_out", l, i] for i in range(2)], axis=1) for l in range(DEPTH)], axis=1),
        "ev_w_in": jnp.stack([got_w["mix_in", l] for l in (0, 2)], axis=1),
        "ev_w_out": jnp.stack([got_w["mix_out", l] for l in (0, 2)], axis=1),
        "od_w_in": jnp.stack([got_w["mix_in", l] for l in (1, 3)], axis=1),
        "od_w_out": jnp.stack([got_w["mix_out", l] for l in (1, 3)], axis=1),
    }
    for k in BIG:
        w = p[k]
        r2 = (math.prod(w.shape[:-1]), w.shape[-1])
        gp = big_parts[k]
        out = adamw(w.reshape(r2), p["m_" + k].reshape(r2), p["v_" + k].reshape(r2), gp.reshape((gp.shape[0],) + r2), "adamw_" + k)
        grads[k], delta[k], new_m[k], new_v[k] = [o.reshape(w.shape) for o in out]

    return (loss, grad_x, *[grads[k] for k in WEIGHTS], *[delta[k] for k in WEIGHTS], *[new_m[k] for k in WEIGHTS], *[new_v[k] for k in WEIGHTS])
```

```python
import math

import jax
import jax.numpy as jnp
from jax import lax
from jax.experimental import pallas as pl
from jax.experimental.pallas import tpu as pltpu

F32 = jnp.float32
BF16 = jnp.bfloat16
HI = lax.Precision.HIGHEST

N_DEV = 8
D = 1024
DEPTH = 4
N_MOD = 9
FF = 2816
ALPHA = (2.0 * DEPTH) ** 0.25
EPS = 1e-5
SSD_Q = 128
SSD_HEADS = 16
SSD_P = 64
SSD_N = 128
SSD_XBC = 1536
POOL_WINDOWS = (2, 4, 8, 16)
POOL_DIM = 512
CONF_DIM = 512
CONF_K = 31
LRU_DIM = 1024
LRU_HEADS = 8
LRU_C = 8.0
EV_SPLITS = (1024, 1536, 512, 128)
OD_SPLITS = (1024, 1024, 1024)
LR, B1, B2, AEPS, WD, STEP = 0.001, 0.9, 0.999, 1e-08, 0.01, 10

LANE = 128
SUB = 8
MIB = 1024 * 1024
VMEM_LIMIT = 48 * MIB
TM = 512
SPLIT_ROWS = 256


def _params(sem, vmem=VMEM_LIMIT):
    return pltpu.CompilerParams(dimension_semantics=sem, vmem_limit_bytes=vmem)


def _sds(shape, dtype):
    return jax.ShapeDtypeStruct(shape, dtype)


def _modulate(x, sh, sc):
    return x * (1.0 + sc) + sh


def _postnorm(x, y, g, lng, lnb, *, coef):
    z = ALPHA * x + coef * (1.0 + g) * y
    mu = jnp.mean(z, axis=-1, keepdims=True)
    zc = z - mu
    var = jnp.mean(zc * zc, axis=-1, keepdims=True)
    return zc * lax.rsqrt(var + EPS) * lng + lnb


def _postnorm_grads(x, y, g, lng, dxn, *, coef):
    kk = coef * (1.0 + g)
    z = ALPHA * x + kk * y
    zc = z - jnp.mean(z, axis=-1, keepdims=True)
    r = lax.rsqrt(jnp.mean(zc * zc, axis=-1, keepdims=True) + EPS)
    xhat = zc * r
    dxhat = dxn * lng
    dz = r * (dxhat - jnp.mean(dxhat, axis=-1, keepdims=True) - xhat * jnp.mean(dxhat * xhat, axis=-1, keepdims=True))
    rows = lambda v: jnp.sum(v, axis=0, keepdims=True)
    return ALPHA * dz, kk * dz, coef * rows(y * dz), rows(dxn * xhat), rows(dxn)


def _place():
    mx, my, mc = lax.axis_index("x"), lax.axis_index("y"), lax.axis_index("c")

    def at(r):
        px = 1 - mx if r & 4 else mx
        py = 1 - my if r & 2 else my
        pc = 1 - mc if r & 1 else mc
        return (px, py, pc), 4 * px + 2 * py + pc

    return 4 * mx + 2 * my + mc, at


def _carry_plan(items):
    hbm = pl.BlockSpec(memory_space=pltpu.HBM)
    k = len(items)
    shapes = [_sds((N_DEV,) + a.shape if g else a.shape, a.dtype) for a, g in items]
    scratch = [pltpu.SemaphoreType.DMA((k * (N_DEV - 1),)), pltpu.SemaphoreType.DMA((k * (N_DEV - 1),)), pltpu.SemaphoreType.DMA((k,))] if k else []
    return [hbm] * k, [hbm] * k, shapes, scratch


def _remote(src, dst, sems, s, pos):
    return pltpu.make_async_remote_copy(src_ref=src, dst_ref=dst, send_sem=sems[0].at[s], recv_sem=sems[1].at[s],
                                        device_id=pos, device_id_type=pl.DeviceIdType.MESH)


def _carry_start(gathers, x_refs, o_refs, sems):
    me, at = _place()
    for a, (gather, x_ref, o_ref) in enumerate(zip(gathers, x_refs, o_refs)):
        base = a * (N_DEV - 1)
        pltpu.make_async_copy(x_ref if gather else x_ref.at[me], o_ref.at[me], sems[2].at[a]).start()
        if gather:
            for s, r in enumerate((1, 4, 2, 6)):
                _remote(x_ref, o_ref.at[me], sems, base + s, at(r)[0]).start()
        else:
            for r in range(1, N_DEV):
                pos, pid = at(r)
                _remote(x_ref.at[pid], o_ref.at[me], sems, base + r - 1, pos).start()


def _carry_pass_on(gathers, x_refs, o_refs, sems):
    _, at = _place()
    sibling = at(1)[0]
    for a, (gather, x_ref, o_ref) in enumerate(zip(gathers, x_refs, o_refs)):
        if gather:
            base = a * (N_DEV - 1)
            for j, r in enumerate((4, 2, 6)):
                pos, pid = at(r)
                _remote(x_ref, o_ref.at[pid], sems, base + 1 + j, pos).wait_recv()
                _remote(o_ref.at[pid], o_ref.at[pid], sems, base + 4 + j, sibling).start()


def _carry_wait(gathers, x_refs, o_refs, sems):
    me, at = _place()
    for a, (gather, x_ref, o_ref) in enumerate(zip(gathers, x_refs, o_refs)):
        base = a * (N_DEV - 1)
        if gather:
            sib_pos, sib_id = at(1)
            _remote(x_ref, o_ref.at[sib_id], sems, base, sib_pos).wait_recv()
            for j, r in enumerate((4, 2, 6)):
                _remote(x_ref, o_ref.at[at(r | 1)[1]], sems, base + 4 + j, sib_pos).wait_recv()
            for s in range(N_DEV - 1):
                _remote(x_ref, o_ref.at[me], sems, base + s, sib_pos).wait_send()
            pltpu.make_async_copy(x_ref, o_ref.at[me], sems[2].at[a]).wait()
        else:
            for r in range(1, N_DEV):
                pos, pid = at(r)
                _remote(x_ref.at[pid], o_ref.at[pid], sems, base + r - 1, pos).wait_recv()
            for r in range(1, N_DEV):
                pos, pid = at(r)
                _remote(x_ref.at[pid], o_ref.at[me], sems, base + r - 1, pos).wait_send()
            pltpu.make_async_copy(x_ref.at[me], o_ref.at[me], sems[2].at[a]).wait()


def exchange(items, name):
    gathers = [g for _, g in items]
    k = len(items)
    in_specs, out_specs, shapes, scratch = _carry_plan(items)

    def body(*refs):
        x_refs, o_refs, sems = refs[:k], refs[k : 2 * k], refs[2 * k :]
        _carry_start(gathers, x_refs, o_refs, sems)
        _carry_pass_on(gathers, x_refs, o_refs, sems)
        _carry_wait(gathers, x_refs, o_refs, sems)

    return pl.pallas_call(
        body,
        in_specs=in_specs,
        out_specs=out_specs,
        out_shape=shapes,
        scratch_shapes=scratch,
        compiler_params=pltpu.CompilerParams(has_side_effects=True),
        name=name,
    )(*[a for a, _ in items])


def ffn_up(x, sh, sc, w, seq, name, carry=()):
    n = x.shape[0]
    tn = FF // 2
    nj = FF // tn
    ni = n // TM
    tps = seq // TM
    k = len(carry)
    gathers = [g for _, g in carry]
    c_in, c_out, c_shapes, c_scratch = _carry_plan(carry)

    def body(x_ref, sh_ref, sc_ref, w_hbm, *rest):
        cx, (h_ref, g_ref, u_ref, a_ref), co, (w_ref, w_sem, *sems) = rest[:k], rest[k : k + 4], rest[k + 4 : 2 * k + 4], rest[2 * k + 4 :]
        i = pl.program_id(0)

        @pl.when(i == 0)
        def _():
            if k:
                _carry_start(gathers, cx, co, sems)
            cp = pltpu.make_async_copy(w_hbm, w_ref, w_sem)
            cp.start()
            cp.wait()

        h = _modulate(x_ref[...], sh_ref[0], sc_ref[0]).astype(BF16)
        h_ref[...] = h
        for j in range(nj):
            g = jnp.dot(h, w_ref[:, j * tn : (j + 1) * tn], preferred_element_type=F32)
            u = jnp.dot(h, w_ref[:, FF + j * tn : FF + (j + 1) * tn], preferred_element_type=F32)
            g_ref[:, j * tn : (j + 1) * tn] = g.astype(BF16)
            u_ref[:, j * tn : (j + 1) * tn] = u.astype(BF16)
            a_ref[:, j * tn : (j + 1) * tn] = (g * jax.nn.sigmoid(g) * u).astype(BF16)
        if k:
            @pl.when(i == max(ni - 2, 0))
            def _():
                _carry_pass_on(gathers, cx, co, sems)

            @pl.when(i == ni - 1)
            def _():
                _carry_wait(gathers, cx, co, sems)

    vec = pl.BlockSpec((1, 1, D), lambda i: (i // tps, 0, 0))
    col = pl.BlockSpec((TM, FF), lambda i: (i, 0))
    return pl.pallas_call(
        body,
        grid=(ni,),
        in_specs=[pl.BlockSpec((TM, D), lambda i: (i, 0)), vec, vec, pl.BlockSpec(memory_space=pltpu.HBM)] + c_in,
        out_specs=[pl.BlockSpec((TM, D), lambda i: (i, 0)), col, col, col] + c_out,
        out_shape=[_sds((n, D), BF16), _sds((n, FF), BF16), _sds((n, FF), BF16), _sds((n, FF), BF16)] + c_shapes,
        scratch_shapes=[pltpu.VMEM((D, 2 * FF), BF16), pltpu.SemaphoreType.DMA] + c_scratch,
        compiler_params=_params(("arbitrary",)),
        name=name,
    )(x, sh, sc, w, *[a for a, _ in carry])


def mod_mm(x, sh, sc, w, splits, seq, name):
    n = x.shape[0]
    m = w.shape[1]
    tps = seq // TM
    offs = [sum(splits[:k]) for k in range(len(splits))]

    def body(x_ref, sh_ref, sc_ref, w_ref, h_ref, *outs):
        h = _modulate(x_ref[...], sh_ref[0], sc_ref[0]).astype(BF16)
        h_ref[...] = h
        for o_ref, off, wd in zip(outs, offs, splits):
            o_ref[...] = jnp.dot(h, w_ref[:, off : off + wd], preferred_element_type=F32)

    vec = pl.BlockSpec((1, 1, D), lambda i: (i // tps, 0, 0))
    return pl.pallas_call(
        body,
        grid=(n // TM,),
        in_specs=[pl.BlockSpec((TM, D), lambda i: (i, 0)), vec, vec, pl.BlockSpec((D, m), lambda i: (0, 0))],
        out_specs=[pl.BlockSpec((TM, D), lambda i: (i, 0))] + [pl.BlockSpec((TM, wd), lambda i: (i, 0)) for wd in splits],
        out_shape=[_sds((n, D), BF16)] + [_sds((n, wd), F32) for wd in splits],
        compiler_params=_params(("arbitrary",)),
        name=name,
    )(x, sh, sc, w)


def mm_postnorm(parts, w, x, g, lng, lnb, coef, seq, name, carry=()):
    n = x.shape[0]
    ni = n // TM
    tps = seq // TM
    ks = [p.shape[1] for p in parts]
    offs = [sum(ks[:k]) for k in range(len(ks))]
    npart = len(parts)
    nc = len(carry)
    gathers = [gt for _, gt in carry]
    c_in, c_out, c_shapes, c_scratch = _carry_plan(carry)

    def body(*refs):
        a_refs = refs[:npart]
        w_ref, x_ref, g_ref, lng_ref, lnb_ref = refs[npart : npart + 5]
        cx = refs[npart + 5 : npart + 5 + nc]
        xn_ref, y_ref = refs[npart + 5 + nc : npart + 7 + nc]
        co, sems = refs[npart + 7 + nc : npart + 7 + 2 * nc], refs[npart + 7 + 2 * nc :]
        i = pl.program_id(0)
        if nc:
            @pl.when(i == 0)
            def _():
                _carry_start(gathers, cx, co, sems)

        for r0 in range(0, TM, SPLIT_ROWS):
            rows = slice(r0, r0 + SPLIT_ROWS)
            y = None
            for a_ref, off, k in zip(a_refs, offs, ks):
                t = jnp.dot(a_ref[rows, :], w_ref[off : off + k, :], preferred_element_type=F32)
                y = t if y is None else y + t
            y_ref[rows, :] = y
            xn_ref[rows, :] = _postnorm(x_ref[rows, :], y, g_ref[0], lng_ref[...], lnb_ref[...], coef=coef)
        if nc:
            @pl.when(i == max(ni - 2, 0))
            def _():
                _carry_pass_on(gathers, cx, co, sems)

            @pl.when(i == ni - 1)
            def _():
                _carry_wait(gathers, cx, co, sems)

    row = pl.BlockSpec((TM, D), lambda i: (i, 0))
    one = pl.BlockSpec((1, D), lambda i: (0, 0))
    return pl.pallas_call(
        body,
        grid=(ni,),
        in_specs=[pl.BlockSpec((TM, k), lambda i: (i, 0)) for k in ks]
        + [pl.BlockSpec((sum(ks), D), lambda i: (0, 0)), row, pl.BlockSpec((1, 1, D), lambda i: (i // tps, 0, 0)), one, one]
        + c_in,
        out_specs=[row, row] + c_out,
        out_shape=[_sds((n, D), F32), _sds((n, D), F32)] + c_shapes,
        scratch_shapes=c_scratch,
        compiler_params=_params(("arbitrary",)),
        name=name,
    )(*parts, w, x, g, lng, lnb, *[a for a, _ in carry])


def postnorm_bwd(dxn, x, y, g, lng, lnb, w, ks, coef, seq, name, carry=()):
    n = x.shape[0]
    ni = n // TM
    tps = seq // TM
    nb = n // seq
    offs = [sum(ks[:k]) for k in range(len(ks))]
    npart = len(ks)
    nc = len(carry)
    gathers = [gt for _, gt in carry]
    c_in, c_out, c_shapes, c_scratch = _carry_plan(carry)

    def body(dxn_ref, x_ref, y_ref, g_ref, lng_ref, lnb_ref, w_ref, *rest):
        cx, rest = rest[:nc], rest[nc:]
        dx_ref, dy_ref = rest[:2]
        da_refs = rest[2 : 2 + npart]
        dg_ref, dlng_ref, dlnb_ref = rest[2 + npart : 5 + npart]
        co, sems = rest[5 + npart : 5 + npart + nc], rest[5 + npart + nc :]
        i = pl.program_id(0)
        if nc:
            @pl.when(i == 0)
            def _():
                _carry_start(gathers, cx, co, sems)

        @pl.when(i % tps == 0)
        def _():
            dg_ref[...] = jnp.zeros_like(dg_ref)

        @pl.when(i == 0)
        def _():
            dlng_ref[...] = jnp.zeros_like(dlng_ref)
            dlnb_ref[...] = jnp.zeros_like(dlnb_ref)

        for r0 in range(0, TM, SPLIT_ROWS):
            rows = slice(r0, r0 + SPLIT_ROWS)
            dx, dy, dg, dlng, dlnb = _postnorm_grads(x_ref[rows, :], y_ref[rows, :], g_ref[0], lng_ref[...], dxn_ref[rows, :], coef=coef)
            dx_ref[rows, :] = dx
            dyb = dy.astype(BF16)
            dy_ref[rows, :] = dyb
            for da_ref, off, k in zip(da_refs, offs, ks):
                da_ref[rows, :] = lax.dot_general(
                    dyb, w_ref[off : off + k, :], (((1,), (1,)), ((), ())), preferred_element_type=F32
                ).astype(BF16)
            dg_ref[0] += dg
            dlng_ref[...] += dlng
            dlnb_ref[...] += dlnb
        if nc:
            @pl.when(i == ni - 1)
            def _():
                _carry_pass_on(gathers, cx, co, sems)
                _carry_wait(gathers, cx, co, sems)

    row = pl.BlockSpec((TM, D), lambda i: (i, 0))
    one = pl.BlockSpec((1, D), lambda i: (0, 0))
    vec = pl.BlockSpec((1, 1, D), lambda i: (i // tps, 0, 0))
    return pl.pallas_call(
        body,
        grid=(ni,),
        in_specs=[row, row, row, vec, one, one, pl.BlockSpec((sum(ks), D), lambda i: (0, 0))] + c_in,
        out_specs=[row, row] + [pl.BlockSpec((TM, k), lambda i: (i, 0)) for k in ks] + [vec, one, one] + c_out,
        out_shape=[_sds((n, D), F32), _sds((n, D), BF16)]
        + [_sds((n, k), BF16) for k in ks]
        + [_sds((nb, 1, D), F32), _sds((1, D), F32), _sds((1, D), F32)]
        + c_shapes,
        scratch_shapes=c_scratch,
        compiler_params=_params(("arbitrary",)),
        name=name,
    )(dxn, x, y, g, lng, lnb, w, *[a for a, _ in carry])


def _mod_bwd_finish(dh, x_ref, sc_ref, dxres_ref, dx_ref, dsh_ref, dsc_ref, first_of_seq):
    dx_ref[...] = dxres_ref[...] + dh * (1.0 + sc_ref[0])

    @pl.when(first_of_seq)
    def _():
        dsh_ref[...] = jnp.zeros_like(dsh_ref)
        dsc_ref[...] = jnp.zeros_like(dsc_ref)

    dsh_ref[0] += jnp.sum(dh, axis=0, keepdims=True)
    dsc_ref[0] += jnp.sum(dh * x_ref[...], axis=0, keepdims=True)


def ffn_bwd_in(da, g, u, w, x, sc, dxres, seq, name, carry=()):
    n = x.shape[0]
    tn = FF // 2
    nj = FF // tn
    tm = TM // 2
    ni = n // tm
    tps = seq // tm
    nb = n // seq
    k = len(carry)
    gathers = [gt for _, gt in carry]
    c_in, c_out, c_shapes, c_scratch = _carry_plan(carry)

    def body(da_ref, g_ref, u_ref, w_hbm, x_ref, sc_ref, dxres_ref, *rest):
        cx, (dgu_ref, dx_ref, dsh_ref, dsc_ref), co, (w_ref, w_sem, *sems) = rest[:k], rest[k : k + 4], rest[k + 4 : 2 * k + 4], rest[2 * k + 4 :]
        i = pl.program_id(0)

        @pl.when(i == 0)
        def _():
            if k:
                _carry_start(gathers, cx, co, sems)
            cp = pltpu.make_async_copy(w_hbm, w_ref, w_sem)
            cp.start()
            cp.wait()

        nt = (((1,), (1,)), ((), ()))
        dh = None
        for j in range(nj):
            ln = slice(j * tn, (j + 1) * tn)
            gv = g_ref[:, ln].astype(F32)
            uv = u_ref[:, ln].astype(F32)
            dav = da_ref[:, ln].astype(F32)
            s = jax.nn.sigmoid(gv)
            dgv = (dav * uv * s * (1.0 + gv * (1.0 - s))).astype(BF16)
            duv = (dav * gv * s).astype(BF16)
            dgu_ref[:, ln] = dgv
            dgu_ref[:, FF + j * tn : FF + (j + 1) * tn] = duv
            t = lax.dot_general(dgv, w_ref[:, j * tn : (j + 1) * tn], nt, preferred_element_type=F32) + lax.dot_general(
                duv, w_ref[:, FF + j * tn : FF + (j + 1) * tn], nt, preferred_element_type=F32
            )
            dh = t if dh is None else dh + t
        _mod_bwd_finish(dh, x_ref, sc_ref, dxres_ref, dx_ref, dsh_ref, dsc_ref, i % tps == 0)
        if k:
            @pl.when(i == ni - 1)
            def _():
                _carry_wait(gathers, cx, co, sems)

    row = pl.BlockSpec((tm, D), lambda i: (i, 0))
    col = pl.BlockSpec((tm, FF), lambda i: (i, 0))
    vec = pl.BlockSpec((1, 1, D), lambda i: (i // tps, 0, 0))
    return pl.pallas_call(
        body,
        grid=(ni,),
        in_specs=[col, col, col, pl.BlockSpec(memory_space=pltpu.HBM), row, vec, row] + c_in,
        out_specs=[pl.BlockSpec((tm, 2 * FF), lambda i: (i, 0)), row, vec, vec] + c_out,
        out_shape=[_sds((n, 2 * FF), BF16), _sds((n, D), F32), _sds((nb, 1, D), F32), _sds((nb, 1, D), F32)] + c_shapes,
        scratch_shapes=[pltpu.VMEM((D, 2 * FF), BF16), pltpu.SemaphoreType.DMA] + c_scratch,
        compiler_params=_params(("arbitrary",)),
        name=name,
    )(da, g, u, w, x, sc, dxres, *[a for a, _ in carry])


def proj_bwd_in(dparts, w, x, sc, dxres, seq, name, carry=()):
    n = x.shape[0]
    ni = n // TM
    tps = seq // TM
    nb = n // seq
    ms = [p.shape[1] for p in dparts]
    offs = [sum(ms[:k]) for k in range(len(ms))]
    npart = len(ms)
    nc = len(carry)
    gathers = [gt for _, gt in carry]
    c_in, c_out, c_shapes, c_scratch = _carry_plan(carry)

    def body(*refs):
        d_refs = refs[:npart]
        w_ref, x_ref, sc_ref, dxres_ref = refs[npart : npart + 4]
        cx = refs[npart + 4 : npart + 4 + nc]
        dx_ref, dsh_ref, dsc_ref = refs[npart + 4 + nc : npart + 7 + nc]
        co, sems = refs[npart + 7 + nc : npart + 7 + 2 * nc], refs[npart + 7 + 2 * nc :]
        i = pl.program_id(0)
        if nc:
            @pl.when(i == 0)
            def _():
                _carry_start(gathers, cx, co, sems)

        dh = None
        for d_ref, off, m in zip(d_refs, offs, ms):
            t = lax.dot_general(d_ref[...], w_ref[:, off : off + m], (((1,), (1,)), ((), ())), preferred_element_type=F32)
            dh = t if dh is None else dh + t
        _mod_bwd_finish(dh, x_ref, sc_ref, dxres_ref, dx_ref, dsh_ref, dsc_ref, i % tps == 0)
        if nc:
            @pl.when(i == ni - 1)
            def _():
                _carry_pass_on(gathers, cx, co, sems)
                _carry_wait(gathers, cx, co, sems)

    row = pl.BlockSpec((TM, D), lambda i: (i, 0))
    vec = pl.BlockSpec((1, 1, D), lambda i: (i // tps, 0, 0))
    return pl.pallas_call(
        body,
        grid=(ni,),
        in_specs=[pl.BlockSpec((TM, m), lambda i: (i, 0)) for m in ms] + [pl.BlockSpec((D, sum(ms)), lambda i: (0, 0)), row, vec, row] + c_in,
        out_specs=[row, vec, vec] + c_out,
        out_shape=[_sds((n, D), F32), _sds((nb, 1, D), F32), _sds((nb, 1, D), F32)] + c_shapes,
        scratch_shapes=c_scratch,
        compiler_params=_params(("arbitrary",)),
        name=name,
    )(*dparts, w, x, sc, dxres, *[a for a, _ in carry])


def mm_tn(a, b, name, cut=0):
    n, k1 = a.shape
    k2 = b.shape[1]
    t1 = k1 if k1 <= 1536 else _tile(k1, 1536)
    t2 = k2 if k2 <= 1536 else _tile(k2, 1536)
    tk = 2048 if n % 2048 == 0 else n
    nk = n // tk
    cw = k2 // cut if cut else t2
    per = t2 // cw
    assert per * cw == t2 and (not cut or t1 == k1)

    def body(a_ref, b_ref, o_ref, acc):
        t = lax.dot_general(a_ref[...], b_ref[...], (((0,), (0,)), ((), ())), preferred_element_type=F32)

        @pl.when(pl.program_id(2) == 0)
        def _():
            acc[...] = t

        @pl.when(pl.program_id(2) > 0)
        def _():
            acc[...] += t

        @pl.when(pl.program_id(2) == nk - 1)
        def _():
            if cut:
                for s in range(per):
                    o_ref[s] = acc[:, s * cw : (s + 1) * cw].astype(BF16)
            else:
                o_ref[...] = acc[...].astype(BF16)

    return pl.pallas_call(
        body,
        grid=(k1 // t1, k2 // t2, nk),
        in_specs=[pl.BlockSpec((tk, t1), lambda i, j, k: (k, i)), pl.BlockSpec((tk, t2), lambda i, j, k: (k, j))],
        out_specs=pl.BlockSpec((per, k1, cw), lambda i, j, k: (j, 0, 0)) if cut else pl.BlockSpec((t1, t2), lambda i, j, k: (i, j)),
        out_shape=_sds((cut, k1, cw), BF16) if cut else _sds((k1, k2), BF16),
        scratch_shapes=[pltpu.VMEM((t1, t2), F32)],
        compiler_params=_params(("arbitrary", "arbitrary", "arbitrary")),
        name=name,
    )(a, b)


def _tile(n, cap):
    best = LANE
    for t in range(LANE, cap + 1, LANE):
        if n % t == 0:
            best = t
    return best


def _lane_tiles(ref):
    return [slice(c0, c0 + LANE) for c0 in range(0, ref.shape[1], LANE)]


def _conv_taps(ext_ref, w_ref, halo, tq, kk):
    out = []
    for ln in _lane_tiles(w_ref):
        acc = None
        for k in range(kk):
            t = w_ref[k : k + 1, ln] * ext_ref[pl.ds(halo - (kk - 1 - k), tq), ln]
            acc = t if acc is None else acc + t
        out.append(acc)
    return jnp.concatenate(out, axis=1)


def _conv_taps_t(ext2_ref, w_ref, tq, kk):
    out = []
    for ln in _lane_tiles(w_ref):
        acc = None
        for k in range(kk):
            t = w_ref[k : k + 1, ln] * ext2_ref[pl.ds(kk - 1 - k, tq), ln]
            acc = t if acc is None else acc + t
        out.append(acc)
    return jnp.concatenate(out, axis=1)


def _conv_dw(ext_ref, ext2_ref, dw_ref, halo, tq, kk):
    for ln in _lane_tiles(dw_ref):
        dy = ext2_ref[pl.ds(0, tq), ln]
        for k in range(kk):
            dw_ref[k : k + 1, ln] += jnp.sum(dy * ext_ref[pl.ds(halo - (kk - 1 - k), tq), ln], axis=0, keepdims=True)


def _halo_spec(rows, width, tq, shift):
    per = tq // rows

    if shift < 0:
        return lambda nblocks: pl.BlockSpec((rows, width), lambda i: (jnp.maximum(i * per - 1, 0), 0))
    return lambda nblocks: pl.BlockSpec((rows, width), lambda i: (jnp.minimum((i + 1) * per, nblocks - 1), 0))


def _ln(c, g, b):
    mu = jnp.mean(c, axis=-1, keepdims=True)
    cc = c - mu
    var = jnp.mean(cc * cc, axis=-1, keepdims=True)
    return cc * lax.rsqrt(var + EPS) * g + b


def _silu(v):
    return v * jax.nn.sigmoid(v)


@jax.custom_vjp
def _expand(v, e):
    hi = v.astype(BF16)
    r1 = v - hi.astype(F32)
    mid = r1.astype(BF16)
    lo = (r1 - mid.astype(F32)).astype(BF16)
    return (jnp.dot(hi, e, preferred_element_type=F32) + jnp.dot(mid, e, preferred_element_type=F32)
            + jnp.dot(lo, e, preferred_element_type=F32))


def _expand_fwd(v, e):
    return _expand(v, e), e


def _expand_bwd(e, g):
    nt = (((1,), (1,)), ((), ()))
    hi = g.astype(BF16)
    mid = (g - hi.astype(F32)).astype(BF16)
    dv = lax.dot_general(hi, e, nt, preferred_element_type=F32) + lax.dot_general(mid, e, nt, preferred_element_type=F32)
    return dv, jnp.zeros_like(e)


_expand.defvjp(_expand_fwd, _expand_bwd)


def _ssd_chunk(conv, dtr, z, s, dt_bias, a_log, dskip, norm_g, tril, e):
    q = SSD_Q
    act = _silu(conv)
    xs, bm, cm = act[:, :1024], act[:, 1024:1280], act[:, 1280:1536]
    lane = lax.broadcasted_iota(jnp.int32, (1, LANE), 1)
    lane_q = lax.broadcasted_iota(jnp.int32, (q, LANE), 1)
    sub_q = lax.broadcasted_iota(jnp.int32, (LANE, q), 0)
    causal = lax.broadcasted_iota(jnp.int32, (q, q), 0) >= lax.broadcasted_iota(jnp.int32, (q, q), 1)
    real = lane < SSD_HEADS
    dt = jnp.where(real, jax.nn.softplus(dtr + dt_bias), 0.0)
    a = jnp.where(real, -jnp.exp(a_log), 0.0)
    da = dt * a
    acs = jnp.dot(tril, da, precision=HI, preferred_element_type=F32)
    acs_t = lax.dot_general(da, tril, (((0,), (1,)), ((), ())), precision=HI, preferred_element_type=F32)
    dt_e = _expand(dt, e)
    acs_e = _expand(acs, e)
    one8 = jnp.ones((SUB, 1), F32)
    alast_e = _expand(one8 * jnp.sum(da, axis=0, keepdims=True), e)[0:1]
    d_e = _expand(one8 * jnp.where(real, dskip, 0.0), e)[0:1]
    xdt = xs * dt_e
    nt = (((1,), (1,)), ((), ()))
    tn = (((0,), (0,)), ((), ()))
    ys, snews = [], []
    for g in range(2):
        gl = slice(g * 512, (g + 1) * 512)
        bg = bm[:, g * 128 : (g + 1) * 128].astype(BF16)
        cg = cm[:, g * 128 : (g + 1) * 128].astype(BF16)
        cb = lax.dot_general(cg, bg, nt, preferred_element_type=F32)
        sg = s[:, gl]
        yoff = jnp.dot(cg, sg.astype(BF16), preferred_element_type=F32) * jnp.exp(acs_e[:, gl])
        pairs = []
        for j in range(4):
            xp = xdt[:, g * 512 + j * 128 : g * 512 + (j + 1) * 128].astype(BF16)
            outs = []
            for hh in (g * 8 + 2 * j, g * 8 + 2 * j + 1):
                col = jnp.sum(jnp.where(lane_q == hh, acs, 0.0), axis=1, keepdims=True)
                row = jnp.sum(jnp.where(sub_q == hh, acs_t, 0.0), axis=0, keepdims=True)
                m = cb * jnp.exp(jnp.where(causal, col - row, -1e30))
                outs.append(jnp.dot(m.astype(BF16), xp, preferred_element_type=F32))
            pairs.append(jnp.where(lane_q < SSD_P, outs[0], outs[1]))
        ys.append(jnp.concatenate(pairs, axis=1) + yoff)
        decay = jnp.exp(alast_e[:, gl] - acs_e[:, gl])
        snews.append(
            sg * jnp.exp(alast_e[:, gl]) + lax.dot_general(bg, (xdt[:, gl] * decay).astype(BF16), tn, preferred_element_type=F32)
        )
    y = jnp.concatenate(ys, axis=1) + xs * d_e
    gated = y * _silu(z)
    out = gated * lax.rsqrt(jnp.mean(gated * gated, axis=-1, keepdims=True) + EPS) * norm_g
    return out, jnp.concatenate(snews, axis=1)


def _ssd_consts():
    tril = (lax.broadcasted_iota(jnp.int32, (SSD_Q, SSD_Q), 0) >= lax.broadcasted_iota(jnp.int32, (SSD_Q, SSD_Q), 1)).astype(F32)
    e = (lax.broadcasted_iota(jnp.int32, (LANE, 1024), 0) == lax.broadcasted_iota(jnp.int32, (LANE, 1024), 1) // SSD_P).astype(BF16)
    return tril, e


def ssd_fwd(z, xbc, dtr, cw, cb, dt_bias, a_log, dskip, norm_g, seq, name):
    n = z.shape[0]
    q = SSD_Q
    nc = seq // q
    tril, e = _ssd_consts()

    def body(z_ref, xbc_ref, halo_ref, dtr_ref, cw_ref, cb_ref, dtb_ref, alog_ref, dsk_ref, ng_ref, tril_ref, e_ref, y_ref, sprev_ref, s_scr, ext):
        c = pl.program_id(0) % nc

        @pl.when(c == 0)
        def _():
            s_scr[...] = jnp.zeros_like(s_scr)

        ext[0:SUB, :] = jnp.where(c == 0, 0.0, halo_ref[...])
        ext[SUB:, :] = xbc_ref[...]
        conv = _conv_taps(ext, cw_ref, SUB, q, 4) + cb_ref[...]
        sprev_ref[0] = s_scr[...]
        y, snew = _ssd_chunk(conv, dtr_ref[...], z_ref[...], s_scr[...], dtb_ref[...], alog_ref[...], dsk_ref[...], ng_ref[...], tril_ref[...], e_ref[...])
        y_ref[...] = y.astype(BF16)
        s_scr[...] = snew

    def full(shape):
        return pl.BlockSpec(shape, lambda i: (0,) * len(shape))

    return pl.pallas_call(
        body,
        grid=(n // q,),
        in_specs=[
            pl.BlockSpec((q, 1024), lambda i: (i, 0)),
            pl.BlockSpec((q, SSD_XBC), lambda i: (i, 0)),
            _halo_spec(SUB, SSD_XBC, q, -1)(n // SUB),
            pl.BlockSpec((q, LANE), lambda i: (i, 0)),
            full((4, SSD_XBC)),
            full((1, SSD_XBC)),
            full((1, LANE)),
            full((1, LANE)),
            full((1, LANE)),
            full((1, 1024)),
            full((q, q)),
            full((LANE, 1024)),
        ],
        out_specs=[pl.BlockSpec((q, 1024), lambda i: (i, 0)), pl.BlockSpec((1, LANE, 1024), lambda i: (i, 0, 0))],
        out_shape=[_sds((n, 1024), BF16), _sds((n // q, LANE, 1024), F32)],
        scratch_shapes=[pltpu.VMEM((LANE, 1024), F32), pltpu.VMEM((SUB + q, SSD_XBC), F32)],
        compiler_params=_params(("arbitrary",)),
        name=name,
    )(z, xbc, xbc, dtr, cw, cb, dt_bias, a_log, dskip, norm_g, tril, e)


def ssd_bwd(dy, z, xbc, dtr, sprev, cw, cb, dt_bias, a_log, dskip, norm_g, seq, name):
    n = z.shape[0]
    q = SSD_Q
    nc = seq // q
    nchunks = n // q
    tril, e = _ssd_consts()

    def rev(i):
        return (i // nc) * nc + (nc - 1 - i % nc)

    def body(dy_ref, z_ref, xbc_ref, halo_ref, dtr_ref, sprev_ref, cw_ref, cb_ref, dtb_ref, alog_ref, dsk_ref, ng_ref, tril_ref, e_ref,
             dz_ref, dxbc_ref, ddt_ref, dcw_ref, dcb_ref, ddtb_ref, dalog_ref, ddsk_ref, dng_ref, ds_scr, ext, ext2):
        i = pl.program_id(0)
        step = i % nc
        c = nc - 1 - step

        @pl.when(step == 0)
        def _():
            ds_scr[...] = jnp.zeros_like(ds_scr)
            ext2[q:, :] = jnp.zeros((SUB, SSD_XBC), F32)

        @pl.when(i == 0)
        def _():
            for r in (dcw_ref, dcb_ref, ddtb_ref, dalog_ref, ddsk_ref, dng_ref):
                r[...] = jnp.zeros_like(r)

        ext[0:SUB, :] = jnp.where(c == 0, 0.0, halo_ref[...])
        ext[SUB:, :] = xbc_ref[...]
        conv = _conv_taps(ext, cw_ref, SUB, q, 4) + cb_ref[...]
        tril_v, e_v = tril_ref[...], e_ref[...]

        def f(conv, dtr, z, s, dtb, alog, dsk, ng):
            return _ssd_chunk(conv, dtr, z, s, dtb, alog, dsk, ng, tril_v, e_v)

        _, vjp = jax.vjp(f, conv, dtr_ref[...], z_ref[...], sprev_ref[0], dtb_ref[...], alog_ref[...], dsk_ref[...], ng_ref[...])
        dconv, ddtr, dz, dsprev, ddtb, dalog, ddsk, dng = vjp((dy_ref[...].astype(F32), ds_scr[...]))
        ds_scr[...] = dsprev
        dz_ref[...] = dz.astype(BF16)
        ddt_ref[...] = ddtr.astype(BF16)
        ext2[0:q, :] = dconv
        dxbc_ref[...] = _conv_taps_t(ext2, cw_ref, q, 4).astype(BF16)
        ext2[q:, :] = dconv[0:SUB, :]
        _conv_dw(ext, ext2, dcw_ref, SUB, q, 4)
        dcb_ref[...] += jnp.sum(dconv, axis=0, keepdims=True)
        ddtb_ref[...] += ddtb
        dalog_ref[...] += dalog
        ddsk_ref[...] += ddsk
        dng_ref[...] += dng

    def full(shape):
        return pl.BlockSpec(shape, lambda i: (0,) * len(shape))

    per = q // SUB
    return pl.pallas_call(
        body,
        grid=(nchunks,),
        in_specs=[
            pl.BlockSpec((q, 1024), lambda i: (rev(i), 0)),
            pl.BlockSpec((q, 1024), lambda i: (rev(i), 0)),
            pl.BlockSpec((q, SSD_XBC), lambda i: (rev(i), 0)),
            pl.BlockSpec((SUB, SSD_XBC), lambda i: (jnp.maximum(rev(i) * per - 1, 0), 0)),
            pl.BlockSpec((q, LANE), lambda i: (rev(i), 0)),
            pl.BlockSpec((1, LANE, 1024), lambda i: (rev(i), 0, 0)),
            full((4, SSD_XBC)),
            full((1, SSD_XBC)),
            full((1, LANE)),
            full((1, LANE)),
            full((1, LANE)),
            full((1, 1024)),
            full((q, q)),
            full((LANE, 1024)),
        ],
        out_specs=[
            pl.BlockSpec((q, 1024), lambda i: (rev(i), 0)),
            pl.BlockSpec((q, SSD_XBC), lambda i: (rev(i), 0)),
            pl.BlockSpec((q, LANE), lambda i: (rev(i), 0)),
            full((4, SSD_XBC)),
            full((1, SSD_XBC)),
            full((1, LANE)),
            full((1, LANE)),
            full((1, LANE)),
            full((1, 1024)),
        ],
        out_shape=[
            _sds((n, 1024), BF16),
            _sds((n, SSD_XBC), BF16),
            _sds((n, LANE), BF16),
            _sds((4, SSD_XBC), F32),
            _sds((1, SSD_XBC), F32),
            _sds((1, LANE), F32),
            _sds((1, LANE), F32),
            _sds((1, LANE), F32),
            _sds((1, 1024), F32),
        ],
        scratch_shapes=[pltpu.VMEM((LANE, 1024), F32), pltpu.VMEM((SUB + q, SSD_XBC), F32), pltpu.VMEM((q + SUB, SSD_XBC), F32)],
        compiler_params=_params(("arbitrary",)),
        name=name,
    )(dy, z, xbc, xbc, dtr, sprev, cw, cb, dt_bias, a_log, dskip, norm_g, tril, e)


POOL_HALO = 16
TQ = 512


def _pool_count(pos, w):
    return jnp.minimum(pos + 1.0, float(w))


def pool_fwd(u, pw, scale, seq, name):
    n = u.shape[0]
    tq, halo = TQ, POOL_HALO
    tps = seq // tq

    def body(u_ref, halo_ref, pw_ref, sc_ref, y_ref, ext):
        t0 = pl.program_id(0) % tps
        ext[0:halo, :] = jnp.where(t0 == 0, 0.0, halo_ref[...])
        ext[halo:, :] = u_ref[...]
        pos = (t0 * tq + lax.broadcasted_iota(jnp.int32, (tq, 1), 0)).astype(F32)
        for g, w in enumerate(POOL_WINDOWS):
            ln = slice(g * LANE, (g + 1) * LANE)
            acc = ext[pl.ds(halo, tq), ln]
            for j in range(1, w):
                acc = acc + ext[pl.ds(halo - j, tq), ln]
            pooled = acc / _pool_count(pos, w) - u_ref[:, ln]
            mixed = jnp.dot(pooled.astype(BF16), pw_ref[g].astype(BF16), preferred_element_type=F32)
            y_ref[:, ln] = (mixed * sc_ref[:, ln]).astype(BF16)

    return pl.pallas_call(
        body,
        grid=(n // tq,),
        in_specs=[
            pl.BlockSpec((tq, POOL_DIM), lambda i: (i, 0)),
            _halo_spec(halo, POOL_DIM, tq, -1)(n // halo),
            pl.BlockSpec((4, LANE, LANE), lambda i: (0, 0, 0)),
            pl.BlockSpec((1, POOL_DIM), lambda i: (0, 0)),
        ],
        out_specs=pl.BlockSpec((tq, POOL_DIM), lambda i: (i, 0)),
        out_shape=_sds((n, POOL_DIM), BF16),
        scratch_shapes=[pltpu.VMEM((halo + tq, POOL_DIM), F32)],
        compiler_params=_params(("arbitrary",)),
        name=name,
    )(u, u, pw, scale)


def pool_bwd(dy, u, pw, scale, seq, name):
    n = u.shape[0]
    tq, halo = TQ, POOL_HALO
    tps = seq // tq
    nt = (((1,), (1,)), ((), ()))
    tn = (((0,), (0,)), ((), ()))

    def body(dy_ref, dyn_ref, u_ref, halo_ref, pw_ref, sc_ref, du_ref, dpw_ref, dsc_ref, ext, ext2):
        i = pl.program_id(0)
        t0 = i % tps

        @pl.when(i == 0)
        def _():
            dpw_ref[...] = jnp.zeros_like(dpw_ref)
            dsc_ref[...] = jnp.zeros_like(dsc_ref)

        ext[0:halo, :] = jnp.where(t0 == 0, 0.0, halo_ref[...])
        ext[halo:, :] = u_ref[...]
        pos = (t0 * tq + lax.broadcasted_iota(jnp.int32, (tq, 1), 0)).astype(F32)
        dyv = dy_ref[...].astype(F32)
        dynv = jnp.where(t0 == tps - 1, 0.0, dyn_ref[...].astype(F32))
        for g, w in enumerate(POOL_WINDOWS):
            ln = slice(g * LANE, (g + 1) * LANE)
            wg = pw_ref[g].astype(BF16)
            acc = ext[pl.ds(halo, tq), ln]
            for j in range(1, w):
                acc = acc + ext[pl.ds(halo - j, tq), ln]
            pooled = (acc / _pool_count(pos, w) - u_ref[:, ln]).astype(BF16)
            mixed = jnp.dot(pooled, wg, preferred_element_type=F32)
            dsc_ref[:, ln] += jnp.sum(dyv[:, ln] * mixed, axis=0, keepdims=True)
            dmix = (dyv[:, ln] * sc_ref[:, ln]).astype(BF16)
            dpw_ref[g] += lax.dot_general(pooled, dmix, tn, preferred_element_type=F32)
            dpool = lax.dot_general(dmix, wg, nt, preferred_element_type=F32)
            dmix_n = (dynv[:, ln] * sc_ref[:, ln]).astype(BF16)
            dpool_n = lax.dot_general(dmix_n, wg, nt, preferred_element_type=F32)
            ext2[0:tq, ln] = dpool / _pool_count(pos, w)
            ext2[tq:, ln] = dpool_n * (1.0 / w)
            acc2 = ext2[pl.ds(0, tq), ln]
            for j in range(1, w):
                acc2 = acc2 + ext2[pl.ds(j, tq), ln]
            du_ref[:, ln] = (acc2 - dpool).astype(BF16)

    return pl.pallas_call(
        body,
        grid=(n // tq,),
        in_specs=[
            pl.BlockSpec((tq, POOL_DIM), lambda i: (i, 0)),
            _halo_spec(halo, POOL_DIM, tq, +1)(n // halo),
            pl.BlockSpec((tq, POOL_DIM), lambda i: (i, 0)),
            _halo_spec(halo, POOL_DIM, tq, -1)(n // halo),
            pl.BlockSpec((4, LANE, LANE), lambda i: (0, 0, 0)),
            pl.BlockSpec((1, POOL_DIM), lambda i: (0, 0)),
        ],
        out_specs=[
            pl.BlockSpec((tq, POOL_DIM), lambda i: (i, 0)),
            pl.BlockSpec((4, LANE, LANE), lambda i: (0, 0, 0)),
            pl.BlockSpec((1, POOL_DIM), lambda i: (0, 0)),
        ],
        out_shape=[_sds((n, POOL_DIM), BF16), _sds((4, LANE, LANE), F32), _sds((1, POOL_DIM), F32)],
        scratch_shapes=[pltpu.VMEM((halo + tq, POOL_DIM), F32), pltpu.VMEM((tq + halo, POOL_DIM), F32)],
        compiler_params=_params(("arbitrary",)),
        name=name,
    )(dy, dy, u, u, pw, scale)


CONF_HALO = 32
TQC = 256


def _conf_post(c, g, b):
    return _silu(_ln(c, g, b))


def conf_fwd(vg, w, b, lng, lnb, seq, name):
    n = vg.shape[0]
    tq, halo, kk = TQC, CONF_HALO, CONF_K
    tps = seq // tq
    c = CONF_DIM

    def body(vg_ref, halo_ref, w_ref, b_ref, lng_ref, lnb_ref, y_ref, conv_ref, ext):
        t0 = pl.program_id(0) % tps
        hv = halo_ref[...]
        ext[0:halo, :] = jnp.where(t0 == 0, 0.0, hv[:, :c] * jax.nn.sigmoid(hv[:, c:]))
        ext[halo:, :] = vg_ref[:, :c] * jax.nn.sigmoid(vg_ref[:, c:])
        conv = _conv_taps(ext, w_ref, halo, tq, kk) + b_ref[...]
        conv_ref[...] = conv
        y_ref[...] = _conf_post(conv, lng_ref[...], lnb_ref[...]).astype(BF16)

    one = pl.BlockSpec((1, c), lambda i: (0, 0))
    return pl.pallas_call(
        body,
        grid=(n // tq,),
        in_specs=[pl.BlockSpec((tq, 2 * c), lambda i: (i, 0)), _halo_spec(halo, 2 * c, tq, -1)(n // halo), pl.BlockSpec((kk, c), lambda i: (0, 0)), one, one, one],
        out_specs=[pl.BlockSpec((tq, c), lambda i: (i, 0)), pl.BlockSpec((tq, c), lambda i: (i, 0))],
        out_shape=[_sds((n, c), BF16), _sds((n, c), F32)],
        scratch_shapes=[pltpu.VMEM((halo + tq, c), F32)],
        compiler_params=_params(("arbitrary",)),
        name=name,
    )(vg, vg, w, b, lng, lnb)


def conf_bwd(dy, conv, vg, w, lng, lnb, seq, name):
    n = vg.shape[0]
    tq, halo, kk = TQC, CONF_HALO, CONF_K
    tps = seq // tq
    c = CONF_DIM

    def body(dy_ref, dyn_ref, conv_ref, convn_ref, vg_ref, halo_ref, w_ref, lng_ref, lnb_ref, dvg_ref, dw_ref, db_ref, dlng_ref, dlnb_ref, ext, ext2):
        i = pl.program_id(0)
        t0 = i % tps

        @pl.when(i == 0)
        def _():
            for r in (dw_ref, db_ref, dlng_ref, dlnb_ref):
                r[...] = jnp.zeros_like(r)

        _, vjp = jax.vjp(_conf_post, conv_ref[...], lng_ref[...], lnb_ref[...])
        dconv, dlng, dlnb = vjp(dy_ref[...].astype(F32))
        _, vjpn = jax.vjp(_conf_post, convn_ref[...], lng_ref[...], lnb_ref[...])
        dconv_n = vjpn(dyn_ref[...].astype(F32))[0]
        ext2[0:tq, :] = dconv
        ext2[tq:, :] = jnp.where(t0 == tps - 1, 0.0, dconv_n)
        dh = _conv_taps_t(ext2, w_ref, tq, kk)
        hv = halo_ref[...]
        ext[0:halo, :] = jnp.where(t0 == 0, 0.0, hv[:, :c] * jax.nn.sigmoid(hv[:, c:]))
        v = vg_ref[:, :c]
        s = jax.nn.sigmoid(vg_ref[:, c:])
        ext[halo:, :] = v * s
        _conv_dw(ext, ext2, dw_ref, halo, tq, kk)
        db_ref[...] += jnp.sum(dconv, axis=0, keepdims=True)
        dlng_ref[...] += dlng
        dlnb_ref[...] += dlnb
        dvg_ref[:, :c] = (dh * s).astype(BF16)
        dvg_ref[:, c:] = (dh * v * s * (1.0 - s)).astype(BF16)

    one = pl.BlockSpec((1, c), lambda i: (0, 0))
    tile = pl.BlockSpec((tq, c), lambda i: (i, 0))
    nxt = _halo_spec(halo, c, tq, +1)(n // halo)
    return pl.pallas_call(
        body,
        grid=(n // tq,),
        in_specs=[tile, nxt, tile, nxt, pl.BlockSpec((tq, 2 * c), lambda i: (i, 0)), _halo_spec(halo, 2 * c, tq, -1)(n // halo),
                  pl.BlockSpec((kk, c), lambda i: (0, 0)), one, one],
        out_specs=[pl.BlockSpec((tq, 2 * c), lambda i: (i, 0)), pl.BlockSpec((kk, c), lambda i: (0, 0)), one, one, one],
        out_shape=[_sds((n, 2 * c), BF16), _sds((kk, c), F32), _sds((1, c), F32), _sds((1, c), F32), _sds((1, c), F32)],
        scratch_shapes=[pltpu.VMEM((halo + tq, c), F32), pltpu.VMEM((tq + halo, c), F32)],
        compiler_params=_params(("arbitrary",)),
        name=name,
    )(dy, dy, conv, conv, vg, vg, w, lng, lnb)


TL = 256


def _expm1_neg(t):
    p = t * (1.0 + t * (1.0 / 2 + t * (1.0 / 6 + t * (1.0 / 24 + t * (1.0 / 120 + t * (1.0 / 720 + t * (1.0 / 5040)))))))
    return jnp.where(t > -0.35, p, jnp.exp(t) - 1.0)


def _lru_gate(xc, ra, ia, ba, bx, lam):
    r = jax.nn.sigmoid(ra + ba)
    i = jax.nn.sigmoid(ia + bx)
    log_a = -LRU_C * r * jax.nn.softplus(-lam)
    return jnp.exp(log_a), jnp.sqrt(-_expm1_neg(2.0 * log_a)) * (i * xc)


def _lru_out(h, gr):
    return h * jax.nn.gelu(gr)


def _scan_rows(a, b, tq, reverse):
    r8 = lax.broadcasted_iota(jnp.int32, (tq, 1), 0) % SUB
    for d in (1, 2, 4):
        sh = tq - d if reverse else d
        valid = (r8 < SUB - d) if reverse else (r8 >= d)
        a_s = pltpu.roll(a, sh, 0)
        b_s = pltpu.roll(b, sh, 0)
        b = jnp.where(valid, a * b_s, 0.0) + b
        a = jnp.where(valid, a * a_s, a)
    ng = tq // SUB
    edge = 0 if reverse else SUB - 1
    out_a, out_b = [None] * ng, [None] * ng
    ca = cb = None
    for g in (reversed(range(ng)) if reverse else range(ng)):
        ag, bg = a[g * SUB : (g + 1) * SUB, :], b[g * SUB : (g + 1) * SUB, :]
        if ca is not None:
            bg = bg + ag * cb
            ag = ag * ca
        out_a[g], out_b[g] = ag, bg
        ca, cb = ag[edge : edge + 1, :], bg[edge : edge + 1, :]
    return jnp.concatenate(out_a, axis=0), jnp.concatenate(out_b, axis=0)


def _row_of(v, r, tq):
    row = lax.broadcasted_iota(jnp.int32, (tq, 1), 0)
    return jnp.sum(jnp.where(row == r, v, 0.0), axis=0, keepdims=True)


def _head_mm(xc, w_ref):
    return jnp.concatenate(
        [
            jnp.dot(xc[:, h * LANE : (h + 1) * LANE].astype(BF16), w_ref[h].astype(BF16), preferred_element_type=F32)
            for h in range(LRU_HEADS)
        ],
        axis=1,
    )


def lru_fwd(xr, gr, cw, cb, wa, ba, wx, bx, lam, seq, name):
    n = xr.shape[0]
    tq = TL
    tps = seq // tq
    c = LRU_DIM

    def body(xr_ref, halo_ref, gr_ref, cw_ref, cb_ref, wa_ref, ba_ref, wx_ref, bx_ref, lam_ref, y_ref, h_ref, hc, ext):
        t0 = pl.program_id(0) % tps

        @pl.when(t0 == 0)
        def _():
            hc[...] = jnp.zeros_like(hc)

        ext[0:SUB, :] = jnp.where(t0 == 0, 0.0, halo_ref[...])
        ext[SUB:, :] = xr_ref[...]
        xc = _conv_taps(ext, cw_ref, SUB, tq, 4) + cb_ref[...]
        a, b = _lru_gate(xc, _head_mm(xc, wa_ref), _head_mm(xc, wx_ref), ba_ref[...], bx_ref[...], lam_ref[...])
        acum, h0 = _scan_rows(a, b, tq, False)
        h = h0 + acum * hc[0:1, :]
        h_ref[...] = h
        hc[0:1, :] = h_ref[tq - 1 : tq, :]
        y_ref[...] = _lru_out(h, gr_ref[...]).astype(BF16)

    one = pl.BlockSpec((1, c), lambda i: (0, 0))
    tile = pl.BlockSpec((tq, c), lambda i: (i, 0))
    hw = pl.BlockSpec((LRU_HEADS, LANE, LANE), lambda i: (0, 0, 0))
    return pl.pallas_call(
        body,
        grid=(n // tq,),
        in_specs=[tile, _halo_spec(SUB, c, tq, -1)(n // SUB), tile, pl.BlockSpec((4, c), lambda i: (0, 0)), one, hw, one, hw, one, one],
        out_specs=[tile, tile],
        out_shape=[_sds((n, c), BF16), _sds((n, c), F32)],
        scratch_shapes=[pltpu.VMEM((SUB, c), F32), pltpu.VMEM((SUB + tq, c), F32)],
        compiler_params=_params(("arbitrary",)),
        name=name,
    )(xr, xr, gr, cw, cb, wa, ba, wx, bx, lam)


def lru_bwd(dy, xr, gr, h, cw, cb, wa, ba, wx, bx, lam, seq, name):
    n = xr.shape[0]
    tq = TL
    tps = seq // tq
    ntile = n // tq
    c = LRU_DIM
    per = tq // SUB
    nt = (((1,), (1,)), ((), ()))
    tn = (((0,), (0,)), ((), ()))

    def rev(i):
        return (i // tps) * tps + (tps - 1 - i % tps)

    def body(dy_ref, xr_ref, halo_ref, gr_ref, h_ref, hprev_ref, cw_ref, cb_ref, wa_ref, ba_ref, wx_ref, bx_ref, lam_ref,
             dxr_ref, dgr_ref, dcw_ref, dcb_ref, dwa_ref, dba_ref, dwx_ref, dbx_ref, dlam_ref, carry, ext, ext2):
        i = pl.program_id(0)
        step = i % tps
        t0 = tps - 1 - step

        @pl.when(step == 0)
        def _():
            carry[...] = jnp.zeros_like(carry)
            ext2[tq:, :] = jnp.zeros((SUB, c), F32)

        @pl.when(i == 0)
        def _():
            for r in (dcw_ref, dcb_ref, dwa_ref, dba_ref, dwx_ref, dbx_ref, dlam_ref):
                r[...] = jnp.zeros_like(r)

        ext[0:SUB, :] = jnp.where(t0 == 0, 0.0, halo_ref[...])
        ext[SUB:, :] = xr_ref[...]
        xc = _conv_taps(ext, cw_ref, SUB, tq, 4) + cb_ref[...]
        (a, _), vjp_gate = jax.vjp(_lru_gate, xc, _head_mm(xc, wa_ref), _head_mm(xc, wx_ref), ba_ref[...], bx_ref[...], lam_ref[...])
        hv = h_ref[...]
        _, vjp_out = jax.vjp(_lru_out, hv, gr_ref[...])
        dh, dgr = vjp_out(dy_ref[...].astype(F32))
        dgr_ref[...] = dgr.astype(BF16)
        row = lax.broadcasted_iota(jnp.int32, (tq, 1), 0)
        a_up = jnp.where(row == tq - 1, carry[0:1, :], pltpu.roll(a, tq - 1, 0))
        acum, l0 = _scan_rows(a_up, dh, tq, True)
        lamv = l0 + acum * carry[1:2, :]
        carry[0:1, :] = _row_of(a, 0, tq)
        carry[1:2, :] = _row_of(lamv, 0, tq)
        hprev = jnp.where(row == 0, jnp.where(t0 == 0, 0.0, hprev_ref[SUB - 1 : SUB, :]), pltpu.roll(hv, 1, 0))
        dxc, dra, dia, dba, dbx, dlam = vjp_gate((lamv * hprev, lamv))
        dba_ref[...] += dba
        dbx_ref[...] += dbx
        dlam_ref[...] += dlam
        pieces = []
        for hh in range(LRU_HEADS):
            ln = slice(hh * LANE, (hh + 1) * LANE)
            xh = xc[:, ln].astype(BF16)
            drh = dra[:, ln].astype(BF16)
            dih = dia[:, ln].astype(BF16)
            dwa_ref[hh] += lax.dot_general(xh, drh, tn, preferred_element_type=F32)
            dwx_ref[hh] += lax.dot_general(xh, dih, tn, preferred_element_type=F32)
            pieces.append(
                lax.dot_general(drh, wa_ref[hh].astype(BF16), nt, preferred_element_type=F32)
                + lax.dot_general(dih, wx_ref[hh].astype(BF16), nt, preferred_element_type=F32)
            )
        dxc = dxc + jnp.concatenate(pieces, axis=1)
        ext2[0:tq, :] = dxc
        dxr_ref[...] = _conv_taps_t(ext2, cw_ref, tq, 4).astype(BF16)
        ext2[tq:, :] = ext2[0:SUB, :]
        _conv_dw(ext, ext2, dcw_ref, SUB, tq, 4)
        dcb_ref[...] += jnp.sum(dxc, axis=0, keepdims=True)

    one = pl.BlockSpec((1, c), lambda i: (0, 0))
    tile = pl.BlockSpec((tq, c), lambda i: (rev(i), 0))
    prev = pl.BlockSpec((SUB, c), lambda i: (jnp.maximum(rev(i) * per - 1, 0), 0))
    hw = pl.BlockSpec((LRU_HEADS, LANE, LANE), lambda i: (0, 0, 0))
    cw4 = pl.BlockSpec((4, c), lambda i: (0, 0))
    return pl.pallas_call(
        body,
        grid=(ntile,),
        in_specs=[tile, tile, prev, tile, tile, prev, cw4, one, hw, one, hw, one, one],
        out_specs=[tile, tile, cw4, one, hw, one, hw, one, one],
        out_shape=[_sds((n, c), BF16), _sds((n, c), BF16), _sds((4, c), F32), _sds((1, c), F32), _sds((LRU_HEADS, LANE, LANE), F32),
                   _sds((1, c), F32), _sds((LRU_HEADS, LANE, LANE), F32), _sds((1, c), F32), _sds((1, c), F32)],
        scratch_shapes=[pltpu.VMEM((SUB, c), F32), pltpu.VMEM((SUB + tq, c), F32), pltpu.VMEM((tq + SUB, c), F32)],
        compiler_params=_params(("arbitrary",)),
        name=name,
    )(dy, xr, xr, gr, h, h, cw, cb, wa, ba, wx, bx, lam)


def ada_fwd(c_all, w, b, name):
    nl, _, cols = w.shape
    nb = c_all.shape[0]

    def body(c_ref, w_ref, b_ref, o_ref):
        sc = _silu(c_ref[...]).astype(BF16)
        o_ref[0] = jnp.dot(sc, w_ref[0].astype(BF16), preferred_element_type=F32) + b_ref[0]

    return pl.pallas_call(
        body,
        grid=(nl,),
        in_specs=[pl.BlockSpec((nb, D), lambda l: (0, 0)), pl.BlockSpec((1, D, cols), lambda l: (l, 0, 0)), pl.BlockSpec((1, 1, cols), lambda l: (l, 0, 0))],
        out_specs=pl.BlockSpec((1, nb, cols), lambda l: (l, 0, 0)),
        out_shape=_sds((nl, nb, cols), F32),
        compiler_params=_params(("arbitrary",)),
        name=name,
    )(c_all, w, b)


def ada_bwd(c_all, dmod, name):
    nl, nb, cols = dmod.shape

    def body(c_ref, d_ref, o_ref):
        sc = _silu(c_ref[...]).astype(BF16)
        o_ref[0] = lax.dot_general(sc, d_ref[0].astype(BF16), (((0,), (0,)), ((), ())), preferred_element_type=F32)

    return pl.pallas_call(
        body,
        grid=(nl,),
        in_specs=[pl.BlockSpec((nb, D), lambda l: (0, 0)), pl.BlockSpec((1, nb, cols), lambda l: (l, 0, 0))],
        out_specs=pl.BlockSpec((1, D, cols), lambda l: (l, 0, 0)),
        out_shape=_sds((nl, D, cols), F32),
        compiler_params=_params(("arbitrary",)),
        name=name,
    )(c_all, dmod)


def loss_grad(y, target, name):
    n = y.shape[0]

    def body(y_ref, t_ref, dy_ref, l_ref, acc):
        i = pl.program_id(0)

        @pl.when(i == 0)
        def _():
            acc[...] = jnp.zeros_like(acc)

        e = y_ref[...] - t_ref[...]
        dy_ref[...] = e * (1.0 / D)
        acc[...] += jnp.sum(e * e, axis=0, keepdims=True)

        @pl.when(i == n // TM - 1)
        def _():
            l_ref[...] = jnp.full((1, LANE), 0.5 / D, F32) * jnp.sum(acc[...])

    row = pl.BlockSpec((TM, D), lambda i: (i, 0))
    return pl.pallas_call(
        body,
        grid=(n // TM,),
        in_specs=[row, row],
        out_specs=[row, pl.BlockSpec((1, LANE), lambda i: (0, 0))],
        out_shape=[_sds((n, D), F32), _sds((1, LANE), F32)],
        scratch_shapes=[pltpu.VMEM((1, D), F32)],
        compiler_params=_params(("arbitrary",)),
        name=name,
    )(y, target)


def sum_parts(parts, name):
    ns, r, _ = parts.shape
    tr = _row_tile(r, 1024)

    def body(p_ref, o_ref):
        acc = p_ref[0]
        for k in range(1, ns):
            acc = acc + p_ref[k]
        o_ref[...] = acc

    return pl.pallas_call(
        body,
        grid=(r // tr,),
        in_specs=[pl.BlockSpec((ns, tr, LANE), lambda i: (0, i, 0))],
        out_specs=pl.BlockSpec((tr, LANE), lambda i: (i, 0)),
        out_shape=_sds((r, LANE), F32),
        compiler_params=_params(("arbitrary",)),
        name=name,
    )(parts)


def _row_tile(r, cap):
    if r <= cap:
        return r
    best = None
    for t in range(16, cap + 1, 16):
        if r % t == 0:
            best = t
    assert best is not None, r
    return best


def adamw(w, m, v, gparts, name):
    r, c = w.shape
    ns = gparts.shape[0]
    tr = _row_tile(r, min(512, 256 * 1024 // c))
    c1 = 1.0 - B1**STEP
    c2 = 1.0 - B2**STEP

    def body(w_ref, m_ref, v_ref, g_ref, go_ref, d_ref, mo_ref, vo_ref):
        g = g_ref[0].astype(F32)
        for k in range(1, ns):
            g = g + g_ref[k].astype(F32)
        mn = B1 * m_ref[...] + (1.0 - B1) * g
        vn = B2 * v_ref[...] + (1.0 - B2) * (g * g)
        go_ref[...] = g
        mo_ref[...] = mn
        vo_ref[...] = vn
        d_ref[...] = -LR * ((mn / c1) / (jnp.sqrt(vn / c2) + AEPS) + WD * w_ref[...])

    tile = pl.BlockSpec((tr, c), lambda i: (i, 0))
    return pl.pallas_call(
        body,
        grid=(r // tr,),
        in_specs=[tile, tile, tile, pl.BlockSpec((ns, tr, c), lambda i: (0, i, 0))],
        out_specs=[tile, tile, tile, tile],
        out_shape=[_sds((r, c), F32)] * 4,
        compiler_params=_params(("arbitrary",)),
        name=name,
    )(w, m, v, gparts)


WEIGHTS = ["ada_w", "ada_b", "ln_g", "ln_b", "ffn_w_in", "ffn_w_out", "ev_w_in", "ssd_conv_w", "ssd_conv_b", "ssd_dt_bias",
           "ssd_a_log", "ssd_d", "ssd_norm_g", "pool_w", "pool_scale", "ev_w_out", "od_w_in", "conf_dw_w", "conf_dw_b",
           "conf_ln_g", "conf_ln_b", "lru_conv_w", "lru_conv_b", "lru_wa", "lru_ba", "lru_wx", "lru_bx", "lru_lambda", "od_w_out"]
BIG = ("ada_w", "ffn_w_in", "ffn_w_out", "ev_w_in", "ev_w_out", "od_w_in", "od_w_out")
SMALL = {
    "ada_b": ((4, 9216), None), "ln_g": ((4, 3, 1024), 2), "ln_b": ((4, 3, 1024), 2),
    "ssd_conv_w": ((2, 4, 1536), 2), "ssd_conv_b": ((2, 1536), None), "ssd_dt_bias": ((2, 16), None),
    "ssd_a_log": ((2, 16), None), "ssd_d": ((2, 16), None), "ssd_norm_g": ((2, 1024), None),
    "pool_w": ((2, 4, 128, 128), None), "pool_scale": ((2, 512), None),
    "conf_dw_w": ((2, 31, 512), 2), "conf_dw_b": ((2, 512), 1), "conf_ln_g": ((2, 512), 1), "conf_ln_b": ((2, 512), 1),
    "lru_conv_w": ((2, 4, 1024), 2), "lru_conv_b": ((2, 1024), 1), "lru_wa": ((2, 8, 128, 128), None),
    "lru_ba": ((2, 1024), 1), "lru_wx": ((2, 8, 128, 128), None), "lru_bx": ((2, 1024), 1), "lru_lambda": ((2, 1024), 1),
}
PACK_ROWS = 2 * SUB * LANE


def _pack(arrs, mult=PACK_ROWS):
    flat = jnp.concatenate([a.reshape(-1) for a in arrs])
    pad = (-flat.shape[0]) % mult
    return jnp.pad(flat, (0, pad)).reshape(-1, LANE)


def _unpack(buf, shapes, lead=()):
    flat = buf.reshape(lead + (-1,))
    out, off = [], 0
    for s in shapes:
        k = math.prod(s)
        out.append(flat[..., off : off + k].reshape(lead + tuple(s)))
        off += k
    return out


def _pad_lanes(v):
    return jnp.pad(v, (0, LANE - v.shape[0]))[None]


def kernel(x, c, ada_w, ada_b, ln_g, ln_b, ffn_w_in, ffn_w_out, ev_w_in, ssd_conv_w, ssd_conv_b, ssd_dt_bias, ssd_a_log, ssd_d, ssd_norm_g, pool_w, pool_scale, ev_w_out, od_w_in, conf_dw_w, conf_dw_b, conf_ln_g, conf_ln_b, lru_conv_w, lru_conv_b, lru_wa, lru_ba, lru_wx, lru_bx, lru_lambda, od_w_out, loss_target, m_ada_w, m_ada_b, m_ln_g, m_ln_b, m_ffn_w_in, m_ffn_w_out, m_ev_w_in, m_ssd_conv_w, m_ssd_conv_b, m_ssd_dt_bias, m_ssd_a_log, m_ssd_d, m_ssd_norm_g, m_pool_w, m_pool_scale, m_ev_w_out, m_od_w_in, m_conf_dw_w, m_conf_dw_b, m_conf_ln_g, m_conf_ln_b, m_lru_conv_w, m_lru_conv_b, m_lru_wa, m_lru_ba, m_lru_wx, m_lru_bx, m_lru_lambda, m_od_w_out, v_ada_w, v_ada_b, v_ln_g, v_ln_b, v_ffn_w_in, v_ffn_w_out, v_ev_w_in, v_ssd_conv_w, v_ssd_conv_b, v_ssd_dt_bias, v_ssd_a_log, v_ssd_d, v_ssd_norm_g, v_pool_w, v_pool_scale, v_ev_w_out, v_od_w_in, v_conf_dw_w, v_conf_dw_b, v_conf_ln_g, v_conf_ln_b, v_lru_conv_w, v_lru_conv_b, v_lru_wa, v_lru_ba, v_lru_wx, v_lru_bx, v_lru_lambda, v_od_w_out):
    p = dict(locals())
    nb, seq, _ = x.shape
    n = nb * seq
    me = 4 * lax.axis_index("x") + 2 * lax.axis_index("y") + lax.axis_index("c")
    sharded = [k for k, (_, ax) in SMALL.items() if ax is not None]

    def cols_of(g):
        return jnp.moveaxis(g, 0, 1).reshape(g.shape[1], N_DEV * g.shape[2])

    def rows_of(g):
        return g.reshape(N_DEV * g.shape[1], g.shape[2])

    def ev_in_of(g):
        w = cols_of(g)
        return jnp.concatenate([w[:, :2560], w[:, 2576:], jnp.pad(w[:, 2560:2576], ((0, 0), (0, LANE - SSD_HEADS)))], axis=1)

    sh_ffn_in, sh_ffn_out = ffn_w_in.astype(BF16), ffn_w_out.astype(BF16)
    sh_mix_in = [ev_w_in.astype(BF16), od_w_in.astype(BF16)]
    sh_mix_out = [ev_w_out.astype(BF16), od_w_out.astype(BF16)]

    def ffn_items(l, i):
        return [(sh_ffn_in[l, i], True), (sh_ffn_out[l, i], True)]

    def mix_items(l):
        return [(sh_mix_in[l % 2][l // 2], True), (sh_mix_out[l % 2][l // 2], True)]

    sm_local_shapes = [p[k].shape for k in sharded]
    g_in, g_out, sm_all = exchange(ffn_items(0, 0) + [(_pack([p[k] for k in sharded] + [c]), True)], "ag_first")
    w_ffn = {(0, 0): (cols_of(g_in), rows_of(g_out))}
    w_mix = {}
    got = _unpack(sm_all, sm_local_shapes + [c.shape], lead=(N_DEV,))
    full = {k: p[k] for k, (_, ax) in SMALL.items() if ax is None}
    for k, g in zip(sharded, got[:-1]):
        full[k] = jnp.moveaxis(g, 0, SMALL[k][1]).reshape(SMALL[k][0])
    c_all = got[-1].reshape(N_DEV * nb, D)

    cols = ada_w.shape[-1]
    ada_b_loc = lax.dynamic_slice_in_dim(ada_b, me * cols, cols, axis=1)[:, None, :]
    mod_cols = ada_fwd(c_all, ada_w, ada_b_loc, "ada_fwd")
    (mod_x,) = exchange([(mod_cols.reshape(DEPTH, N_DEV, nb, cols).transpose(1, 0, 2, 3), False)], "a2a_mod")
    mod = mod_x.transpose(1, 2, 0, 3).reshape(DEPTH, nb, N_MOD, 1, D)

    def vec(l, j):
        return mod[l, :, j]

    def row(a):
        return a[None]

    xs = x.reshape(n, D)
    saved = []
    for l in range(DEPTH):
        s = {"x0": xs}
        e = l // 2
        s["h1"], s["g1"], s["u1"], s["a1"], g_in, g_out = ffn_up(xs, vec(l, 0), vec(l, 1), w_ffn[l, 0][0], seq, "ffn_up_c2", carry=ffn_items(l, 1))
        w_ffn[l, 1] = (cols_of(g_in), rows_of(g_out))
        x1, s["y1"], gm_in, gm_out = mm_postnorm([s["a1"]], w_ffn[l, 0][1], xs, vec(l, 2), row(full["ln_g"][l, 0]), row(full["ln_b"][l, 0]), 0.5, seq,
                                                 "ffn_down_c2", carry=mix_items(l))
        w_mix[l] = ((ev_in_of if l % 2 == 0 else cols_of)(gm_in), rows_of(gm_out))
        s["x1"] = x1
        if l % 2 == 0:
            s["h2"], s["z"], s["xbc"], s["u"], s["dtr"] = mod_mm(x1, vec(l, 3), vec(l, 4), w_mix[l][0], EV_SPLITS, seq, "ev_in")
            s["ya"], s["sprev"] = ssd_fwd(s["z"], s["xbc"], s["dtr"], full["ssd_conv_w"][e], row(full["ssd_conv_b"][e]), _pad_lanes(full["ssd_dt_bias"][e]),
                                          _pad_lanes(full["ssd_a_log"][e]), _pad_lanes(full["ssd_d"][e]), row(full["ssd_norm_g"][e]), seq, "ssd_fwd")
            s["yb"] = pool_fwd(s["u"], full["pool_w"][e], row(full["pool_scale"][e]), seq, "pool_fwd")
        else:
            s["h2"], s["vg"], s["xr"], s["gr"] = mod_mm(x1, vec(l, 3), vec(l, 4), w_mix[l][0], OD_SPLITS, seq, "od_in")
            s["ya"], s["conv"] = conf_fwd(s["vg"], full["conf_dw_w"][e], row(full["conf_dw_b"][e]), row(full["conf_ln_g"][e]), row(full["conf_ln_b"][e]), seq, "conf_fwd")
            s["yb"], s["hst"] = lru_fwd(s["xr"], s["gr"], full["lru_conv_w"][e], row(full["lru_conv_b"][e]), full["lru_wa"][e], row(full["lru_ba"][e]),
                                        full["lru_wx"][e], row(full["lru_bx"][e]), row(full["lru_lambda"][e]), seq, "lru_fwd")
        x2, s["y2"] = mm_postnorm([s["ya"], s["yb"]], w_mix[l][1], x1, vec(l, 5), row(full["ln_g"][l, 1]), row(full["ln_b"][l, 1]), 1.0, seq, "mix_out")
        s["x2"] = x2
        if l + 1 < DEPTH:
            s["h3"], s["g3"], s["u3"], s["a3"], g_in, g_out = ffn_up(x2, vec(l, 6), vec(l, 7), w_ffn[l, 1][0], seq, "ffn_up_c2", carry=ffn_items(l + 1, 0))
            w_ffn[l + 1, 0] = (cols_of(g_in), rows_of(g_out))
        else:
            s["h3"], s["g3"], s["u3"], s["a3"] = ffn_up(x2, vec(l, 6), vec(l, 7), w_ffn[l, 1][0], seq, "ffn_up")
        xs, s["y3"] = mm_postnorm([s["a3"]], w_ffn[l, 1][1], x2, vec(l, 8), row(full["ln_g"][l, 2]), row(full["ln_b"][l, 2]), 0.5, seq, "ffn_down")
        saved.append(s)

    dx, loss_row = loss_grad(xs, loss_target.reshape(n, D), "loss")
    loss = lax.psum(loss_row[0, 0], ("x", "y", "c"))

    sg = {k: [None] * shape[0] for k, (shape, _) in SMALL.items()}
    sg["ln_g"] = [[None] * 3 for _ in range(DEPTH)]
    sg["ln_b"] = [[None] * 3 for _ in range(DEPTH)]
    dmod = [[None] * N_MOD for _ in range(DEPTH)]
    pending, got_w = [], {}

    def cut_cols(g):
        r, cc = g.shape
        return g.reshape(r, N_DEV, cc // N_DEV).transpose(1, 0, 2)

    def cut_rows(g):
        r, cc = g.shape
        return g.reshape(N_DEV, r // N_DEV, cc)

    def take(room):
        sel = []
        for j, (_, a) in enumerate(pending):
            if a.size * a.dtype.itemsize <= room:
                sel.append(j)
                room -= a.size * a.dtype.itemsize
        items = [pending[j] for j in sel]
        pending[:] = [it for j, it in enumerate(pending) if j not in sel]
        return [k for k, _ in items], [(a, False) for _, a in items]

    room_down, room_up, room_mix = 8 * MIB, 12 * MIB, 6 * MIB

    def postnorm_backward(dxo, xin, y, g, lng, lnb, w, ks, coef, name, room):
        keys, carry = take(room)
        outs = postnorm_bwd(dxo, xin, y, g, lng, lnb, w, ks, coef, seq, name + "_c%d" % len(keys), carry=carry)
        got_w.update(zip(keys, outs[5 + len(ks):]))
        return outs[0], outs[1], outs[2 : 2 + len(ks)], outs[2 + len(ks)], outs[3 + len(ks)], outs[4 + len(ks)]

    def proj_backward(dparts, w, xin, scv, dxres, name):
        keys, carry = take(room_mix)
        outs = proj_bwd_in(dparts, w, xin, scv, dxres, seq, name + "_c%d" % len(keys), carry=carry)
        got_w.update(zip(keys, outs[3:]))
        return outs[:3]

    def ffn_backward(l, i, dxo, s, xin, hk, gk, uk, ak, yk, jbase, lnj):
        dxres, dy, (da,), dmod[l][jbase + 2], sg["ln_g"][l][lnj], sg["ln_b"][l][lnj] = postnorm_backward(
            dxo, xin, s[yk], vec(l, jbase + 2), row(full["ln_g"][l, lnj]), row(full["ln_b"][l, lnj]), w_ffn[l, i][1], [FF], 0.5, "ffn_down_bwd", room_down)
        keys, carry = take(room_up)
        outs = ffn_bwd_in(da, s[gk], s[uk], w_ffn[l, i][0], xin, vec(l, jbase + 1), dxres, seq, "ffn_up_bwd_c%d" % len(keys), carry=carry)
        dgu, dxi, dmod[l][jbase], dmod[l][jbase + 1] = outs[:4]
        got_w.update(zip(keys, outs[4:]))
        pending.append((("ffn_out", l, i), cut_rows(mm_tn(s[ak], dy, "wg_ffn_out"))))
        pending.append((("ffn_in", l, i), mm_tn(s[hk], dgu, "wg_ffn_in", cut=N_DEV)))
        return dxi

    for l in reversed(range(DEPTH)):
        s = saved[l]
        e = l // 2
        dx = ffn_backward(l, 1, dx, s, s["x2"], "h3", "g3", "u3", "a3", "y3", 6, 2)
        ks = [1024, POOL_DIM] if l % 2 == 0 else [CONF_DIM, LRU_DIM]
        dxres, dy, (dya, dyb), dmod[l][5], sg["ln_g"][l][1], sg["ln_b"][l][1] = postnorm_backward(
            dx, s["x1"], s["y2"], vec(l, 5), row(full["ln_g"][l, 1]), row(full["ln_b"][l, 1]), w_mix[l][1], ks, 1.0, "mix_out_bwd", room_mix)
        pending.append((("mix_out", l), cut_rows(jnp.concatenate([mm_tn(s["ya"], dy, "wg_mix_a"), mm_tn(s["yb"], dy, "wg_mix_b")], axis=0))))
        if l % 2 == 0:
            (dz, dxbc, ddt, sg["ssd_conv_w"][e], dcb, ddtb, dalog, ddsk, dng) = ssd_bwd(
                dya, s["z"], s["xbc"], s["dtr"], s["sprev"], full["ssd_conv_w"][e], row(full["ssd_conv_b"][e]), _pad_lanes(full["ssd_dt_bias"][e]),
                _pad_lanes(full["ssd_a_log"][e]), _pad_lanes(full["ssd_d"][e]), row(full["ssd_norm_g"][e]), seq, "ssd_bwd")
            sg["ssd_conv_b"][e], sg["ssd_norm_g"][e] = dcb[0], dng[0]
            sg["ssd_dt_bias"][e], sg["ssd_a_log"][e], sg["ssd_d"][e] = ddtb[0, :SSD_HEADS], dalog[0, :SSD_HEADS], ddsk[0, :SSD_HEADS]
            du, sg["pool_w"][e], dps = pool_bwd(dyb, s["u"], full["pool_w"][e], row(full["pool_scale"][e]), seq, "pool_bwd")
            sg["pool_scale"][e] = dps[0]
            dparts = [dz, dxbc, du, ddt]
            dx, dmod[l][3], dmod[l][4] = proj_backward(dparts, w_mix[l][0], s["x1"], vec(l, 4), dxres, "ev_in_bwd")
            gz, gxbc, gu, gdt = [mm_tn(s["h2"], dp, "wg_ev_in") for dp in dparts]
            pending.append((("mix_in", l), cut_cols(jnp.concatenate([gz, gxbc, gdt[:, :SSD_HEADS], gu], axis=1))))
        else:
            dvg, sg["conf_dw_w"][e], dcb, dlg, dlb = conf_bwd(dya, s["conv"], s["vg"], full["conf_dw_w"][e], row(full["conf_ln_g"][e]), row(full["conf_ln_b"][e]), seq, "conf_bwd")
            sg["conf_dw_b"][e], sg["conf_ln_g"][e], sg["conf_ln_b"][e] = dcb[0], dlg[0], dlb[0]
            (dxr, dgr, sg["lru_conv_w"][e], dcb, sg["lru_wa"][e], dba, sg["lru_wx"][e], dbx, dlam) = lru_bwd(
                dyb, s["xr"], s["gr"], s["hst"], full["lru_conv_w"][e], row(full["lru_conv_b"][e]), full["lru_wa"][e], row(full["lru_ba"][e]),
                full["lru_wx"][e], row(full["lru_bx"][e]), row(full["lru_lambda"][e]), seq, "lru_bwd")
            sg["lru_conv_b"][e], sg["lru_ba"][e], sg["lru_bx"][e], sg["lru_lambda"][e] = dcb[0], dba[0], dbx[0], dlam[0]
            dparts = [dvg, dxr, dgr]
            dx, dmod[l][3], dmod[l][4] = proj_backward(dparts, w_mix[l][0], s["x1"], vec(l, 4), dxres, "od_in_bwd")
            pending.append((("mix_in", l), cut_cols(jnp.concatenate([mm_tn(s["h2"], dp, "wg_od_in") for dp in dparts], axis=1))))
        dx = ffn_backward(l, 0, dx, s, s["x0"], "h1", "g1", "u1", "a1", "y1", 0, 0)
    grad_x = dx.reshape(nb, seq, D)

    dmod_mine = jnp.stack([jnp.concatenate([d[:, 0, :] for d in dmod[l]], axis=-1) for l in range(DEPTH)])
    sg["ada_b"] = [jnp.sum(dmod_mine[l], axis=0) for l in range(DEPTH)]
    sg["ln_g"] = [jnp.concatenate(r, axis=0) for r in sg["ln_g"]]
    sg["ln_b"] = [jnp.concatenate(r, axis=0) for r in sg["ln_b"]]
    small_names = list(SMALL)
    sg_packed = _pack([jnp.stack(sg[k]).reshape(SMALL[k][0]) for k in small_names], N_DEV * PACK_ROWS)
    keys, carry = take(1 << 40)
    outs = exchange(carry + [(dmod_mine.reshape(DEPTH, nb, N_DEV, cols).transpose(2, 0, 1, 3), False),
                             (sg_packed.reshape(N_DEV, -1, LANE), False)], "x_last")
    got_w.update(zip(keys, outs))
    dmod_x, parts = outs[len(keys):]
    g_ada_w = ada_bwd(c_all, dmod_x.transpose(1, 0, 2, 3).reshape(DEPTH, N_DEV * nb, cols), "ada_bwd")

    (sg_sum,) = exchange([(sum_parts(parts, "sum_smallgrad"), True)], "ag_smallsum")
    summed = _unpack(sg_sum.reshape(-1, LANE), [SMALL[k][0] for k in small_names])
    grads = {}
    for k, g in zip(small_names, summed):
        ax = SMALL[k][1]
        grads[k] = g if ax is None else lax.dynamic_slice_in_dim(g, me * p[k].shape[ax], p[k].shape[ax], axis=ax)
    loc_shapes = [p[k].shape for k in small_names]
    whole = 512 * LANE
    _, d_s, m_s, v_s = adamw(_pack([p[k] for k in small_names], whole), _pack([p["m_" + k] for k in small_names], whole),
                             _pack([p["v_" + k] for k in small_names], whole), _pack([grads[k] for k in small_names], whole)[None], "adamw_small")
    delta = dict(zip(small_names, _unpack(d_s, loc_shapes)))
    new_m = dict(zip(small_names, _unpack(m_s, loc_shapes)))
    new_v = dict(zip(small_names, _unpack(v_s, loc_shapes)))

    big_parts = {
        "ada_w": g_ada_w[None],
        "ffn_w_in": jnp.stack([jnp.stack([got_w["ffn_in", l, i] for i in range(2)], axis=1) for l in range(DEPTH)], axis=1),
        "ffn_w_out": jnp.stack([jnp.stack([got_w["ffn_out", l, i] for i in range(2)], axis=1) for l in range(DEPTH)], axis=1),
        "ev_w_in": jnp.stack([got_w["mix_in", l] for l in (0, 2)], axis=1),
        "ev_w_out": jnp.stack([got_w["mix_out", l] for l in (0, 2)], axis=1),
        "od_w_in": jnp.stack([got_w["mix_in", l] for l in (1, 3)], axis=1),
        "od_w_out": jnp.stack([got_w["mix_out", l] for l in (1, 3)], axis=1),
    }
    for k in BIG:
        w = p[k]
        r2 = (math.prod(w.shape[:-1]), w.shape[-1])
        gp = big_parts[k]
        out = adamw(w.reshape(r2), p["m_" + k].reshape(r2), p["v_" + k].reshape(r2), gp.reshape((gp.shape[0],) + r2), "adamw_" + k)
        grads[k], delta[k], new_m[k], new_v[k] = [o.reshape(w.shape) for o in out]

    return (loss, grad_x, *[grads[k] for k in WEIGHTS], *[delta[k] for k in WEIGHTS], *[new_m[k] for k in WEIGHTS], *[new_v[k] for k in WEIGHTS])
```

```python
import math

import jax
import jax.numpy as jnp
from jax import lax
from jax.experimental import pallas as pl
from jax.experimental.pallas import tpu as pltpu

F32 = jnp.float32
BF16 = jnp.bfloat16
HI = lax.Precision.HIGHEST

N_DEV = 8
D = 1024
DEPTH = 4
N_MOD = 9
FF = 2816
ALPHA = (2.0 * DEPTH) ** 0.25
EPS = 1e-5
SSD_Q = 128
SSD_HEADS = 16
SSD_P = 64
SSD_N = 128
SSD_XBC = 1536
POOL_WINDOWS = (2, 4, 8, 16)
POOL_DIM = 512
CONF_DIM = 512
CONF_K = 31
LRU_DIM = 1024
LRU_HEADS = 8
LRU_C = 8.0
EV_SPLITS = (1024, 1536, 512, 128)
OD_SPLITS = (1024, 1024, 1024)
LR, B1, B2, AEPS, WD, STEP = 0.001, 0.9, 0.999, 1e-08, 0.01, 10

LANE = 128
SUB = 8
MIB = 1024 * 1024
VMEM_LIMIT = 48 * MIB
TM = 512
SPLIT_ROWS = 256


def _params(sem, vmem=VMEM_LIMIT):
    return pltpu.CompilerParams(dimension_semantics=sem, vmem_limit_bytes=vmem)


def _sds(shape, dtype):
    return jax.ShapeDtypeStruct(shape, dtype)


def _modulate(x, sh, sc):
    return x * (1.0 + sc) + sh


def _postnorm(x, y, g, lng, lnb, *, coef):
    z = ALPHA * x + coef * (1.0 + g) * y
    mu = jnp.mean(z, axis=-1, keepdims=True)
    zc = z - mu
    var = jnp.mean(zc * zc, axis=-1, keepdims=True)
    return zc * lax.rsqrt(var + EPS) * lng + lnb


def _postnorm_grads(x, y, g, lng, dxn, *, coef):
    kk = coef * (1.0 + g)
    z = ALPHA * x + kk * y
    zc = z - jnp.mean(z, axis=-1, keepdims=True)
    r = lax.rsqrt(jnp.mean(zc * zc, axis=-1, keepdims=True) + EPS)
    xhat = zc * r
    dxhat = dxn * lng
    dz = r * (dxhat - jnp.mean(dxhat, axis=-1, keepdims=True) - xhat * jnp.mean(dxhat * xhat, axis=-1, keepdims=True))
    rows = lambda v: jnp.sum(v, axis=0, keepdims=True)
    return ALPHA * dz, kk * dz, coef * rows(y * dz), rows(dxn * xhat), rows(dxn)


def _place():
    mx, my, mc = lax.axis_index("x"), lax.axis_index("y"), lax.axis_index("c")

    def at(r):
        px = 1 - mx if r & 4 else mx
        py = 1 - my if r & 2 else my
        pc = 1 - mc if r & 1 else mc
        return (px, py, pc), 4 * px + 2 * py + pc

    return 4 * mx + 2 * my + mc, at


def _carry_plan(items):
    hbm = pl.BlockSpec(memory_space=pltpu.HBM)
    k = len(items)
    shapes = [_sds((N_DEV,) + a.shape if g else a.shape, a.dtype) for a, g in items]
    scratch = [pltpu.SemaphoreType.DMA((k * (N_DEV - 1),)), pltpu.SemaphoreType.DMA((k * (N_DEV - 1),)), pltpu.SemaphoreType.DMA((k,))] if k else []
    return [hbm] * k, [hbm] * k, shapes, scratch


def _remote(src, dst, sems, s, pos):
    return pltpu.make_async_remote_copy(src_ref=src, dst_ref=dst, send_sem=sems[0].at[s], recv_sem=sems[1].at[s],
                                        device_id=pos, device_id_type=pl.DeviceIdType.MESH)


def _carry_start(gathers, x_refs, o_refs, sems):
    me, at = _place()
    for a, (gather, x_ref, o_ref) in enumerate(zip(gathers, x_refs, o_refs)):
        base = a * (N_DEV - 1)
        pltpu.make_async_copy(x_ref if gather else x_ref.at[me], o_ref.at[me], sems[2].at[a]).start()
        if gather:
            for s, r in enumerate((1, 4, 2, 6)):
                _remote(x_ref, o_ref.at[me], sems, base + s, at(r)[0]).start()
        else:
            for r in range(1, N_DEV):
                pos, pid = at(r)
                _remote(x_ref.at[pid], o_ref.at[me], sems, base + r - 1, pos).start()


def _carry_pass_on(gathers, x_refs, o_refs, sems):
    _, at = _place()
    sibling = at(1)[0]
    for a, (gather, x_ref, o_ref) in enumerate(zip(gathers, x_refs, o_refs)):
        if gather:
            base = a * (N_DEV - 1)
            for j, r in enumerate((4, 2, 6)):
                pos, pid = at(r)
                _remote(x_ref, o_ref.at[pid], sems, base + 1 + j, pos).wait_recv()
                _remote(o_ref.at[pid], o_ref.at[pid], sems, base + 4 + j, sibling).start()


def _carry_wait(gathers, x_refs, o_refs, sems):
    me, at = _place()
    for a, (gather, x_ref, o_ref) in enumerate(zip(gathers, x_refs, o_refs)):
        base = a * (N_DEV - 1)
        if gather:
            sib_pos, sib_id = at(1)
            _remote(x_ref, o_ref.at[sib_id], sems, base, sib_pos).wait_recv()
            for j, r in enumerate((4, 2, 6)):
                _remote(x_ref, o_ref.at[at(r | 1)[1]], sems, base + 4 + j, sib_pos).wait_recv()
            for s in range(N_DEV - 1):
                _remote(x_ref, o_ref.at[me], sems, base + s, sib_pos).wait_send()
            pltpu.make_async_copy(x_ref, o_ref.at[me], sems[2].at[a]).wait()
        else:
            for r in range(1, N_DEV):
                pos, pid = at(r)
                _remote(x_ref.at[pid], o_ref.at[pid], sems, base + r - 1, pos).wait_recv()
            for r in range(1, N_DEV):
                pos, pid = at(r)
                _remote(x_ref.at[pid], o_ref.at[me], sems, base + r - 1, pos).wait_send()
            pltpu.make_async_copy(x_ref.at[me], o_ref.at[me], sems[2].at[a]).wait()


def exchange(items, name):
    gathers = [g for _, g in items]
    k = len(items)
    in_specs, out_specs, shapes, scratch = _carry_plan(items)

    def body(*refs):
        x_refs, o_refs, sems = refs[:k], refs[k : 2 * k], refs[2 * k :]
        _carry_start(gathers, x_refs, o_refs, sems)
        _carry_pass_on(gathers, x_refs, o_refs, sems)
        _carry_wait(gathers, x_refs, o_refs, sems)

    return pl.pallas_call(
        body,
        in_specs=in_specs,
        out_specs=out_specs,
        out_shape=shapes,
        scratch_shapes=scratch,
        compiler_params=pltpu.CompilerParams(has_side_effects=True),
        name=name,
    )(*[a for a, _ in items])


def ffn_up(x, sh, sc, w, seq, name, carry=()):
    n = x.shape[0]
    tn = FF // 2
    nj = FF // tn
    ni = n // TM
    tps = seq // TM
    k = len(carry)
    gathers = [g for _, g in carry]
    c_in, c_out, c_shapes, c_scratch = _carry_plan(carry)

    def body(x_ref, sh_ref, sc_ref, w_hbm, *rest):
        cx, (h_ref, g_ref, u_ref, a_ref), co, (w_ref, w_sem, *sems) = rest[:k], rest[k : k + 4], rest[k + 4 : 2 * k + 4], rest[2 * k + 4 :]
        i = pl.program_id(0)

        @pl.when(i == 0)
        def _():
            if k:
                _carry_start(gathers, cx, co, sems)
            cp = pltpu.make_async_copy(w_hbm, w_ref, w_sem)
            cp.start()
            cp.wait()

        h = _modulate(x_ref[...], sh_ref[0], sc_ref[0]).astype(BF16)
        h_ref[...] = h
        for j in range(nj):
            g = jnp.dot(h, w_ref[:, j * tn : (j + 1) * tn], preferred_element_type=F32)
            u = jnp.dot(h, w_ref[:, FF + j * tn : FF + (j + 1) * tn], preferred_element_type=F32)
            g_ref[:, j * tn : (j + 1) * tn] = g.astype(BF16)
            u_ref[:, j * tn : (j + 1) * tn] = u.astype(BF16)
            a_ref[:, j * tn : (j + 1) * tn] = (g * jax.nn.sigmoid(g) * u).astype(BF16)
        if k:
            @pl.when(i == max(ni - 2, 0))
            def _():
                _carry_pass_on(gathers, cx, co, sems)

            @pl.when(i == ni - 1)
            def _():
                _carry_wait(gathers, cx, co, sems)

    vec = pl.BlockSpec((1, 1, D), lambda i: (i // tps, 0, 0))
    col = pl.BlockSpec((TM, FF), lambda i: (i, 0))
    return pl.pallas_call(
        body,
        grid=(ni,),
        in_specs=[pl.BlockSpec((TM, D), lambda i: (i, 0)), vec, vec, pl.BlockSpec(memory_space=pltpu.HBM)] + c_in,
        out_specs=[pl.BlockSpec((TM, D), lambda i: (i, 0)), col, col, col] + c_out,
        out_shape=[_sds((n, D), BF16), _sds((n, FF), BF16), _sds((n, FF), BF16), _sds((n, FF), BF16)] + c_shapes,
        scratch_shapes=[pltpu.VMEM((D, 2 * FF), BF16), pltpu.SemaphoreType.DMA] + c_scratch,
        compiler_params=_params(("arbitrary",)),
        name=name,
    )(x, sh, sc, w, *[a for a, _ in carry])


def mod_mm(x, sh, sc, w, splits, seq, name):
    n = x.shape[0]
    m = w.shape[1]
    tps = seq // TM
    offs = [sum(splits[:k]) for k in range(len(splits))]

    def body(x_ref, sh_ref, sc_ref, w_ref, h_ref, *outs):
        h = _modulate(x_ref[...], sh_ref[0], sc_ref[0]).astype(BF16)
        h_ref[...] = h
        for o_ref, off, wd in zip(outs, offs, splits):
            o_ref[...] = jnp.dot(h, w_ref[:, off : off + wd], preferred_element_type=F32)

    vec = pl.BlockSpec((1, 1, D), lambda i: (i // tps, 0, 0))
    return pl.pallas_call(
        body,
        grid=(n // TM,),
        in_specs=[pl.BlockSpec((TM, D), lambda i: (i, 0)), vec, vec, pl.BlockSpec((D, m), lambda i: (0, 0))],
        out_specs=[pl.BlockSpec((TM, D), lambda i: (i, 0))] + [pl.BlockSpec((TM, wd), lambda i: (i, 0)) for wd in splits],
        out_shape=[_sds((n, D), BF16)] + [_sds((n, wd), F32) for wd in splits],
        compiler_params=_params(("arbitrary",)),
        name=name,
    )(x, sh, sc, w)


def mm_postnorm(parts, w, x, g, lng, lnb, coef, seq, name, carry=()):
    n = x.shape[0]
    ni = n // TM
    tps = seq // TM
    ks = [p.shape[1] for p in parts]
    offs = [sum(ks[:k]) for k in range(len(ks))]
    npart = len(parts)
    nc = len(carry)
    gathers = [gt for _, gt in carry]
    c_in, c_out, c_shapes, c_scratch = _carry_plan(carry)

    def body(*refs):
        a_refs = refs[:npart]
        w_ref, x_ref, g_ref, lng_ref, lnb_ref = refs[npart : npart + 5]
        cx = refs[npart + 5 : npart + 5 + nc]
        xn_ref, y_ref = refs[npart + 5 + nc : npart + 7 + nc]
        co, sems = refs[npart + 7 + nc : npart + 7 + 2 * nc], refs[npart + 7 + 2 * nc :]
        i = pl.program_id(0)
        if nc:
            @pl.when(i == 0)
            def _():
                _carry_start(gathers, cx, co, sems)

        for r0 in range(0, TM, SPLIT_ROWS):
            rows = slice(r0, r0 + SPLIT_ROWS)
            y = None
            for a_ref, off, k in zip(a_refs, offs, ks):
                t = jnp.dot(a_ref[rows, :], w_ref[off : off + k, :], preferred_element_type=F32)
                y = t if y is None else y + t
            y_ref[rows, :] = y
            xn_ref[rows, :] = _postnorm(x_ref[rows, :], y, g_ref[0], lng_ref[...], lnb_ref[...], coef=coef)
        if nc:
            @pl.when(i == max(ni - 2, 0))
            def _():
                _carry_pass_on(gathers, cx, co, sems)

            @pl.when(i == ni - 1)
            def _():
                _carry_wait(gathers, cx, co, sems)

    row = pl.BlockSpec((TM, D), lambda i: (i, 0))
    one = pl.BlockSpec((1, D), lambda i: (0, 0))
    return pl.pallas_call(
        body,
        grid=(ni,),
        in_specs=[pl.BlockSpec((TM, k), lambda i: (i, 0)) for k in ks]
        + [pl.BlockSpec((sum(ks), D), lambda i: (0, 0)), row, pl.BlockSpec((1, 1, D), lambda i: (i // tps, 0, 0)), one, one]
        + c_in,
        out_specs=[row, row] + c_out,
        out_shape=[_sds((n, D), F32), _sds((n, D), F32)] + c_shapes,
        scratch_shapes=c_scratch,
        compiler_params=_params(("arbitrary",)),
        name=name,
    )(*parts, w, x, g, lng, lnb, *[a for a, _ in carry])


def postnorm_bwd(dxn, x, y, g, lng, lnb, w, ks, coef, seq, name, carry=()):
    n = x.shape[0]
    ni = n // TM
    tps = seq // TM
    nb = n // seq
    offs = [sum(ks[:k]) for k in range(len(ks))]
    npart = len(ks)
    nc = len(carry)
    gathers = [gt for _, gt in carry]
    c_in, c_out, c_shapes, c_scratch = _carry_plan(carry)

    def body(dxn_ref, x_ref, y_ref, g_ref, lng_ref, lnb_ref, w_ref, *rest):
        cx, rest = rest[:nc], rest[nc:]
        dx_ref, dy_ref = rest[:2]
        da_refs = rest[2 : 2 + npart]
        dg_ref, dlng_ref, dlnb_ref = rest[2 + npart : 5 + npart]
        co, sems = rest[5 + npart : 5 + npart + nc], rest[5 + npart + nc :]
        i = pl.program_id(0)
        if nc:
            @pl.when(i == 0)
            def _():
                _carry_start(gathers, cx, co, sems)

        @pl.when(i % tps == 0)
        def _():
            dg_ref[...] = jnp.zeros_like(dg_ref)

        @pl.when(i == 0)
        def _():
            dlng_ref[...] = jnp.zeros_like(dlng_ref)
            dlnb_ref[...] = jnp.zeros_like(dlnb_ref)

        for r0 in range(0, TM, SPLIT_ROWS):
            rows = slice(r0, r0 + SPLIT_ROWS)
            dx, dy, dg, dlng, dlnb = _postnorm_grads(x_ref[rows, :], y_ref[rows, :], g_ref[0], lng_ref[...], dxn_ref[rows, :], coef=coef)
            dx_ref[rows, :] = dx
            dyb = dy.astype(BF16)
            dy_ref[rows, :] = dyb
            for da_ref, off, k in zip(da_refs, offs, ks):
                da_ref[rows, :] = lax.dot_general(
                    dyb, w_ref[off : off + k, :], (((1,), (1,)), ((), ())), preferred_element_type=F32
                ).astype(BF16)
            dg_ref[0] += dg
            dlng_ref[...] += dlng
            dlnb_ref[...] += dlnb
        if nc:
            @pl.when(i == ni - 1)
            def _():
                _carry_pass_on(gathers, cx, co, sems)
                _carry_wait(gathers, cx, co, sems)

    row = pl.BlockSpec((TM, D), lambda i: (i, 0))
    one = pl.BlockSpec((1, D), lambda i: (0, 0))
    vec = pl.BlockSpec((1, 1, D), lambda i: (i // tps, 0, 0))
    return pl.pallas_call(
        body,
        grid=(ni,),
        in_specs=[row, row, row, vec, one, one, pl.BlockSpec((sum(ks), D), lambda i: (0, 0))] + c_in,
        out_specs=[row, row] + [pl.BlockSpec((TM, k), lambda i: (i, 0)) for k in ks] + [vec, one, one] + c_out,
        out_shape=[_sds((n, D), F32), _sds((n, D), BF16)]
        + [_sds((n, k), BF16) for k in ks]
        + [_sds((nb, 1, D), F32), _sds((1, D), F32), _sds((1, D), F32)]
        + c_shapes,
        scratch_shapes=c_scratch,
        compiler_params=_params(("arbitrary",)),
        name=name,
    )(dxn, x, y, g, lng, lnb, w, *[a for a, _ in carry])


def _mod_bwd_finish(dh, x_ref, sc_ref, dxres_ref, dx_ref, dsh_ref, dsc_ref, first_of_seq):
    dx_ref[...] = dxres_ref[...] + dh * (1.0 + sc_ref[0])

    @pl.when(first_of_seq)
    def _():
        dsh_ref[...] = jnp.zeros_like(dsh_ref)
        dsc_ref[...] = jnp.zeros_like(dsc_ref)

    dsh_ref[0] += jnp.sum(dh, axis=0, keepdims=True)
    dsc_ref[0] += jnp.sum(dh * x_ref[...], axis=0, keepdims=True)


def ffn_bwd_in(da, g, u, w, x, sc, dxres, seq, name, carry=()):
    n = x.shape[0]
    tn = FF // 2
    nj = FF // tn
    tm = TM // 2
    ni = n // tm
    tps = seq // tm
    nb = n // seq
    k = len(carry)
    gathers = [gt for _, gt in carry]
    c_in, c_out, c_shapes, c_scratch = _carry_plan(carry)

    def body(da_ref, g_ref, u_ref, w_hbm, x_ref, sc_ref, dxres_ref, *rest):
        cx, (dgu_ref, dx_ref, dsh_ref, dsc_ref), co, (w_ref, w_sem, *sems) = rest[:k], rest[k : k + 4], rest[k + 4 : 2 * k + 4], rest[2 * k + 4 :]
        i = pl.program_id(0)

        @pl.when(i == 0)
        def _():
            if k:
                _carry_start(gathers, cx, co, sems)
            cp = pltpu.make_async_copy(w_hbm, w_ref, w_sem)
            cp.start()
            cp.wait()

        nt = (((1,), (1,)), ((), ()))
        dh = None
        for j in range(nj):
            ln = slice(j * tn, (j + 1) * tn)
            gv = g_ref[:, ln].astype(F32)
            uv = u_ref[:, ln].astype(F32)
            dav = da_ref[:, ln].astype(F32)
            s = jax.nn.sigmoid(gv)
            dgv = (dav * uv * s * (1.0 + gv * (1.0 - s))).astype(BF16)
            duv = (dav * gv * s).astype(BF16)
            dgu_ref[:, ln] = dgv
            dgu_ref[:, FF + j * tn : FF + (j + 1) * tn] = duv
            t = lax.dot_general(dgv, w_ref[:, j * tn : (j + 1) * tn], nt, preferred_element_type=F32) + lax.dot_general(
                duv, w_ref[:, FF + j * tn : FF + (j + 1) * tn], nt, preferred_element_type=F32
            )
            dh = t if dh is None else dh + t
        _mod_bwd_finish(dh, x_ref, sc_ref, dxres_ref, dx_ref, dsh_ref, dsc_ref, i % tps == 0)
        if k:
            @pl.when(i == ni - 1)
            def _():
                _carry_wait(gathers, cx, co, sems)

    row = pl.BlockSpec((tm, D), lambda i: (i, 0))
    col = pl.BlockSpec((tm, FF), lambda i: (i, 0))
    vec = pl.BlockSpec((1, 1, D), lambda i: (i // tps, 0, 0))
    return pl.pallas_call(
        body,
        grid=(ni,),
        in_specs=[col, col, col, pl.BlockSpec(memory_space=pltpu.HBM), row, vec, row] + c_in,
        out_specs=[pl.BlockSpec((tm, 2 * FF), lambda i: (i, 0)), row, vec, vec] + c_out,
        out_shape=[_sds((n, 2 * FF), BF16), _sds((n, D), F32), _sds((nb, 1, D), F32), _sds((nb, 1, D), F32)] + c_shapes,
        scratch_shapes=[pltpu.VMEM((D, 2 * FF), BF16), pltpu.SemaphoreType.DMA] + c_scratch,
        compiler_params=_params(("arbitrary",)),
        name=name,
    )(da, g, u, w, x, sc, dxres, *[a for a, _ in carry])


def proj_bwd_in(dparts, w, x, sc, dxres, seq, name, carry=()):
    n = x.shape[0]
    ni = n // TM
    tps = seq // TM
    nb = n // seq
    ms = [p.shape[1] for p in dparts]
    offs = [sum(ms[:k]) for k in range(len(ms))]
    npart = len(ms)
    nc = len(carry)
    gathers = [gt for _, gt in carry]
    c_in, c_out, c_shapes, c_scratch = _carry_plan(carry)

    def body(*refs):
        d_refs = refs[:npart]
        w_ref, x_ref, sc_ref, dxres_ref = refs[npart : npart + 4]
        cx = refs[npart + 4 : npart + 4 + nc]
        dx_ref, dsh_ref, dsc_ref = refs[npart + 4 + nc : npart + 7 + nc]
        co, sems = refs[npart + 7 + nc : npart + 7 + 2 * nc], refs[npart + 7 + 2 * nc :]
        i = pl.program_id(0)
        if nc:
            @pl.when(i == 0)
            def _():
                _carry_start(gathers, cx, co, sems)

        dh = None
        for d_ref, off, m in zip(d_refs, offs, ms):
            t = lax.dot_general(d_ref[...], w_ref[:, off : off + m], (((1,), (1,)), ((), ())), preferred_element_type=F32)
            dh = t if dh is None else dh + t
        _mod_bwd_finish(dh, x_ref, sc_ref, dxres_ref, dx_ref, dsh_ref, dsc_ref, i % tps == 0)
        if nc:
            @pl.when(i == ni - 1)
            def _():
                _carry_pass_on(gathers, cx, co, sems)
                _carry_wait(gathers, cx, co, sems)

    row = pl.BlockSpec((TM, D), lambda i: (i, 0))
    vec = pl.BlockSpec((1, 1, D), lambda i: (i // tps, 0, 0))
    return pl.pallas_call(
        body,
        grid=(ni,),
        in_specs=[pl.BlockSpec((TM, m), lambda i: (i, 0)) for m in ms] + [pl.BlockSpec((D, sum(ms)), lambda i: (0, 0)), row, vec, row] + c_in,
        out_specs=[row, vec, vec] + c_out,
        out_shape=[_sds((n, D), F32), _sds((nb, 1, D), F32), _sds((nb, 1, D), F32)] + c_shapes,
        scratch_shapes=c_scratch,
        compiler_params=_params(("arbitrary",)),
        name=name,
    )(*dparts, w, x, sc, dxres, *[a for a, _ in carry])


def mm_tn(a, b, name, cut=0):
    n, k1 = a.shape
    k2 = b.shape[1]
    t1 = k1 if k1 <= 1536 else _tile(k1, 1536)
    t2 = k2 if k2 <= 1536 else _tile(k2, 1536)
    tk = 2048 if n % 2048 == 0 else n
    nk = n // tk
    cw = k2 // cut if cut else t2
    per = t2 // cw
    assert per * cw == t2 and (not cut or t1 == k1)

    def body(a_ref, b_ref, o_ref, acc):
        t = lax.dot_general(a_ref[...], b_ref[...], (((0,), (0,)), ((), ())), preferred_element_type=F32)

        @pl.when(pl.program_id(2) == 0)
        def _():
            acc[...] = t

        @pl.when(pl.program_id(2) > 0)
        def _():
            acc[...] += t

        @pl.when(pl.program_id(2) == nk - 1)
        def _():
            if cut:
                for s in range(per):
                    o_ref[s] = acc[:, s * cw : (s + 1) * cw].astype(BF16)
            else:
                o_ref[...] = acc[...].astype(BF16)

    return pl.pallas_call(
        body,
        grid=(k1 // t1, k2 // t2, nk),
        in_specs=[pl.BlockSpec((tk, t1), lambda i, j, k: (k, i)), pl.BlockSpec((tk, t2), lambda i, j, k: (k, j))],
        out_specs=pl.BlockSpec((per, k1, cw), lambda i, j, k: (j, 0, 0)) if cut else pl.BlockSpec((t1, t2), lambda i, j, k: (i, j)),
        out_shape=_sds((cut, k1, cw), BF16) if cut else _sds((k1, k2), BF16),
        scratch_shapes=[pltpu.VMEM((t1, t2), F32)],
        compiler_params=_params(("arbitrary", "arbitrary", "arbitrary")),
        name=name,
    )(a, b)


def _tile(n, cap):
    best = LANE
    for t in range(LANE, cap + 1, LANE):
        if n % t == 0:
            best = t
    return best


def _lane_tiles(ref):
    return [slice(c0, c0 + LANE) for c0 in range(0, ref.shape[1], LANE)]


def _conv_taps(ext_ref, w_ref, halo, tq, kk):
    out = []
    for ln in _lane_tiles(w_ref):
        acc = None
        for k in range(kk):
            t = w_ref[k : k + 1, ln] * ext_ref[pl.ds(halo - (kk - 1 - k), tq), ln]
            acc = t if acc is None else acc + t
        out.append(acc)
    return jnp.concatenate(out, axis=1)


def _conv_taps_t(ext2_ref, w_ref, tq, kk):
    out = []
    for ln in _lane_tiles(w_ref):
        acc = None
        for k in range(kk):
            t = w_ref[k : k + 1, ln] * ext2_ref[pl.ds(kk - 1 - k, tq), ln]
            acc = t if acc is None else acc + t
        out.append(acc)
    return jnp.concatenate(out, axis=1)


def _conv_dw(ext_ref, ext2_ref, dw_ref, halo, tq, kk):
    for ln in _lane_tiles(dw_ref):
        dy = ext2_ref[pl.ds(0, tq), ln]
        for k in range(kk):
            dw_ref[k : k + 1, ln] += jnp.sum(dy * ext_ref[pl.ds(halo - (kk - 1 - k), tq), ln], axis=0, keepdims=True)


def _halo_spec(rows, width, tq, shift):
    per = tq // rows

    if shift < 0:
        return lambda nblocks: pl.BlockSpec((rows, width), lambda i: (jnp.maximum(i * per - 1, 0), 0))
    return lambda nblocks: pl.BlockSpec((rows, width), lambda i: (jnp.minimum((i + 1) * per, nblocks - 1), 0))


def _ln(c, g, b):
    mu = jnp.mean(c, axis=-1, keepdims=True)
    cc = c - mu
    var = jnp.mean(cc * cc, axis=-1, keepdims=True)
    return cc * lax.rsqrt(var + EPS) * g + b


def _silu(v):
    return v * jax.nn.sigmoid(v)


@jax.custom_vjp
def _expand(v, e):
    hi = v.astype(BF16)
    r1 = v - hi.astype(F32)
    mid = r1.astype(BF16)
    lo = (r1 - mid.astype(F32)).astype(BF16)
    return (jnp.dot(hi, e, preferred_element_type=F32) + jnp.dot(mid, e, preferred_element_type=F32)
            + jnp.dot(lo, e, preferred_element_type=F32))


def _expand_fwd(v, e):
    return _expand(v, e), e


def _expand_bwd(e, g):
    nt = (((1,), (1,)), ((), ()))
    hi = g.astype(BF16)
    mid = (g - hi.astype(F32)).astype(BF16)
    dv = lax.dot_general(hi, e, nt, preferred_element_type=F32) + lax.dot_general(mid, e, nt, preferred_element_type=F32)
    return dv, jnp.zeros_like(e)


_expand.defvjp(_expand_fwd, _expand_bwd)


def _ssd_chunk(conv, dtr, z, s, dt_bias, a_log, dskip, norm_g, tril, e):
    q = SSD_Q
    act = _silu(conv)
    xs, bm, cm = act[:, :1024], act[:, 1024:1280], act[:, 1280:1536]
    lane = lax.broadcasted_iota(jnp.int32, (1, LANE), 1)
    lane_q = lax.broadcasted_iota(jnp.int32, (q, LANE), 1)
    sub_q = lax.broadcasted_iota(jnp.int32, (LANE, q), 0)
    causal = lax.broadcasted_iota(jnp.int32, (q, q), 0) >= lax.broadcasted_iota(jnp.int32, (q, q), 1)
    real = lane < SSD_HEADS
    dt = jnp.where(real, jax.nn.softplus(dtr + dt_bias), 0.0)
    a = jnp.where(real, -jnp.exp(a_log), 0.0)
    da = dt * a
    acs = jnp.dot(tril, da, precision=HI, preferred_element_type=F32)
    acs_t = lax.dot_general(da, tril, (((0,), (1,)), ((), ())), precision=HI, preferred_element_type=F32)
    dt_e = _expand(dt, e)
    acs_e = _expand(acs, e)
    one8 = jnp.ones((SUB, 1), F32)
    alast_e = _expand(one8 * jnp.sum(da, axis=0, keepdims=True), e)[0:1]
    d_e = _expand(one8 * jnp.where(real, dskip, 0.0), e)[0:1]
    xdt = xs * dt_e
    nt = (((1,), (1,)), ((), ()))
    tn = (((0,), (0,)), ((), ()))
    ys, snews = [], []
    for g in range(2):
        gl = slice(g * 512, (g + 1) * 512)
        bg = bm[:, g * 128 : (g + 1) * 128].astype(BF16)
        cg = cm[:, g * 128 : (g + 1) * 128].astype(BF16)
        cb = lax.dot_general(cg, bg, nt, preferred_element_type=F32)
        sg = s[:, gl]
        yoff = jnp.dot(cg, sg.astype(BF16), preferred_element_type=F32) * jnp.exp(acs_e[:, gl])
        pairs = []
        for j in range(4):
            xp = xdt[:, g * 512 + j * 128 : g * 512 + (j + 1) * 128].astype(BF16)
            outs = []
            for hh in (g * 8 + 2 * j, g * 8 + 2 * j + 1):
                col = jnp.sum(jnp.where(lane_q == hh, acs, 0.0), axis=1, keepdims=True)
                row = jnp.sum(jnp.where(sub_q == hh, acs_t, 0.0), axis=0, keepdims=True)
                m = cb * jnp.exp(jnp.where(causal, col - row, -1e30))
                outs.append(jnp.dot(m.astype(BF16), xp, preferred_element_type=F32))
            pairs.append(jnp.where(lane_q < SSD_P, outs[0], outs[1]))
        ys.append(jnp.concatenate(pairs, axis=1) + yoff)
        decay = jnp.exp(alast_e[:, gl] - acs_e[:, gl])
        snews.append(
            sg * jnp.exp(alast_e[:, gl]) + lax.dot_general(bg, (xdt[:, gl] * decay).astype(BF16), tn, preferred_element_type=F32)
        )
    y = jnp.concatenate(ys, axis=1) + xs * d_e
    gated = y * _silu(z)
    out = gated * lax.rsqrt(jnp.mean(gated * gated, axis=-1, keepdims=True) + EPS) * norm_g
    return out, jnp.concatenate(snews, axis=1)


def _ssd_consts():
    tril = (lax.broadcasted_iota(jnp.int32, (SSD_Q, SSD_Q), 0) >= lax.broadcasted_iota(jnp.int32, (SSD_Q, SSD_Q), 1)).astype(F32)
    e = (lax.broadcasted_iota(jnp.int32, (LANE, 1024), 0) == lax.broadcasted_iota(jnp.int32, (LANE, 1024), 1) // SSD_P).astype(BF16)
    return tril, e


def ssd_fwd(z, xbc, dtr, cw, cb, dt_bias, a_log, dskip, norm_g, seq, name):
    n = z.shape[0]
    q = SSD_Q
    nc = seq // q
    tril, e = _ssd_consts()

    def body(z_ref, xbc_ref, halo_ref, dtr_ref, cw_ref, cb_ref, dtb_ref, alog_ref, dsk_ref, ng_ref, tril_ref, e_ref, y_ref, sprev_ref, s_scr, ext):
        c = pl.program_id(0) % nc

        @pl.when(c == 0)
        def _():
            s_scr[...] = jnp.zeros_like(s_scr)

        ext[0:SUB, :] = jnp.where(c == 0, 0.0, halo_ref[...])
        ext[SUB:, :] = xbc_ref[...]
        conv = _conv_taps(ext, cw_ref, SUB, q, 4) + cb_ref[...]
        sprev_ref[0] = s_scr[...]
        y, snew = _ssd_chunk(conv, dtr_ref[...], z_ref[...], s_scr[...], dtb_ref[...], alog_ref[...], dsk_ref[...], ng_ref[...], tril_ref[...], e_ref[...])
        y_ref[...] = y.astype(BF16)
        s_scr[...] = snew

    def full(shape):
        return pl.BlockSpec(shape, lambda i: (0,) * len(shape))

    return pl.pallas_call(
        body,
        grid=(n // q,),
        in_specs=[
            pl.BlockSpec((q, 1024), lambda i: (i, 0)),
            pl.BlockSpec((q, SSD_XBC), lambda i: (i, 0)),
            _halo_spec(SUB, SSD_XBC, q, -1)(n // SUB),
            pl.BlockSpec((q, LANE), lambda i: (i, 0)),
            full((4, SSD_XBC)),
            full((1, SSD_XBC)),
            full((1, LANE)),
            full((1, LANE)),
            full((1, LANE)),
            full((1, 1024)),
            full((q, q)),
            full((LANE, 1024)),
        ],
        out_specs=[pl.BlockSpec((q, 1024), lambda i: (i, 0)), pl.BlockSpec((1, LANE, 1024), lambda i: (i, 0, 0))],
        out_shape=[_sds((n, 1024), BF16), _sds((n // q, LANE, 1024), F32)],
        scratch_shapes=[pltpu.VMEM((LANE, 1024), F32), pltpu.VMEM((SUB + q, SSD_XBC), F32)],
        compiler_params=_params(("arbitrary",)),
        name=name,
    )(z, xbc, xbc, dtr, cw, cb, dt_bias, a_log, dskip, norm_g, tril, e)


def ssd_bwd(dy, z, xbc, dtr, sprev, cw, cb, dt_bias, a_log, dskip, norm_g, seq, name):
    n = z.shape[0]
    q = SSD_Q
    nc = seq // q
    nchunks = n // q
    tril, e = _ssd_consts()

    def rev(i):
        return (i // nc) * nc + (nc - 1 - i % nc)

    def body(dy_ref, z_ref, xbc_ref, halo_ref, dtr_ref, sprev_ref, cw_ref, cb_ref, dtb_ref, alog_ref, dsk_ref, ng_ref, tril_ref, e_ref,
             dz_ref, dxbc_ref, ddt_ref, dcw_ref, dcb_ref, ddtb_ref, dalog_ref, ddsk_ref, dng_ref, ds_scr, ext, ext2):
        i = pl.program_id(0)
        step = i % nc
        c = nc - 1 - step

        @pl.when(step == 0)
        def _():
            ds_scr[...] = jnp.zeros_like(ds_scr)
            ext2[q:, :] = jnp.zeros((SUB, SSD_XBC), F32)

        @pl.when(i == 0)
        def _():
            for r in (dcw_ref, dcb_ref, ddtb_ref, dalog_ref, ddsk_ref, dng_ref):
                r[...] = jnp.zeros_like(r)

        ext[0:SUB, :] = jnp.where(c == 0, 0.0, halo_ref[...])
        ext[SUB:, :] = xbc_ref[...]
        conv = _conv_taps(ext, cw_ref, SUB, q, 4) + cb_ref[...]
        tril_v, e_v = tril_ref[...], e_ref[...]

        def f(conv, dtr, z, s, dtb, alog, dsk, ng):
            return _ssd_chunk(conv, dtr, z, s, dtb, alog, dsk, ng, tril_v, e_v)

        _, vjp = jax.vjp(f, conv, dtr_ref[...], z_ref[...], sprev_ref[0], dtb_ref[...], alog_ref[...], dsk_ref[...], ng_ref[...])
        dconv, ddtr, dz, dsprev, ddtb, dalog, ddsk, dng = vjp((dy_ref[...].astype(F32), ds_scr[...]))
        ds_scr[...] = dsprev
        dz_ref[...] = dz.astype(BF16)
        ddt_ref[...] = ddtr.astype(BF16)
        ext2[0:q, :] = dconv
        dxbc_ref[...] = _conv_taps_t(ext2, cw_ref, q, 4).astype(BF16)
        ext2[q:, :] = dconv[0:SUB, :]
        _conv_dw(ext, ext2, dcw_ref, SUB, q, 4)
        dcb_ref[...] += jnp.sum(dconv, axis=0, keepdims=True)
        ddtb_ref[...] += ddtb
        dalog_ref[...] += dalog
        ddsk_ref[...] += ddsk
        dng_ref[...] += dng

    def full(shape):
        return pl.BlockSpec(shape, lambda i: (0,) * len(shape))

    per = q // SUB
    return pl.pallas_call(
        body,
        grid=(nchunks,),
        in_specs=[
            pl.BlockSpec((q, 1024), lambda i: (rev(i), 0)),
            pl.BlockSpec((q, 1024), lambda i: (rev(i), 0)),
            pl.BlockSpec((q, SSD_XBC), lambda i: (rev(i), 0)),
            pl.BlockSpec((SUB, SSD_XBC), lambda i: (jnp.maximum(rev(i) * per - 1, 0), 0)),
            pl.BlockSpec((q, LANE), lambda i: (rev(i), 0)),
            pl.BlockSpec((1, LANE, 1024), lambda i: (rev(i), 0, 0)),
            full((4, SSD_XBC)),
            full((1, SSD_XBC)),
            full((1, LANE)),
            full((1, LANE)),
            full((1, LANE)),
            full((1, 1024)),
            full((q, q)),
            full((LANE, 1024)),
        ],
        out_specs=[
            pl.BlockSpec((q, 1024), lambda i: (rev(i), 0)),
            pl.BlockSpec((q, SSD_XBC), lambda i: (rev(i), 0)),
            pl.BlockSpec((q, LANE), lambda i: (rev(i), 0)),
            full((4, SSD_XBC)),
            full((1, SSD_XBC)),
            full((1, LANE)),
            full((1, LANE)),
            full((1, LANE)),
            full((1, 1024)),
        ],
        out_shape=[
            _sds((n, 1024), BF16),
            _sds((n, SSD_XBC), BF16),
            _sds((n, LANE), BF16),
            _sds((4, SSD_XBC), F32),
            _sds((1, SSD_XBC), F32),
            _sds((1, LANE), F32),
            _sds((1, LANE), F32),
            _sds((1, LANE), F32),
            _sds((1, 1024), F32),
        ],
        scratch_shapes=[pltpu.VMEM((LANE, 1024), F32), pltpu.VMEM((SUB + q, SSD_XBC), F32), pltpu.VMEM((q + SUB, SSD_XBC), F32)],
        compiler_params=_params(("arbitrary",)),
        name=name,
    )(dy, z, xbc, xbc, dtr, sprev, cw, cb, dt_bias, a_log, dskip, norm_g, tril, e)


POOL_HALO = 16
TQ = 512


def _pool_count(pos, w):
    return jnp.minimum(pos + 1.0, float(w))


def pool_fwd(u, pw, scale, seq, name):
    n = u.shape[0]
    tq, halo = TQ, POOL_HALO
    tps = seq // tq

    def body(u_ref, halo_ref, pw_ref, sc_ref, y_ref, ext):
        t0 = pl.program_id(0) % tps
        ext[0:halo, :] = jnp.where(t0 == 0, 0.0, halo_ref[...])
        ext[halo:, :] = u_ref[...]
        pos = (t0 * tq + lax.broadcasted_iota(jnp.int32, (tq, 1), 0)).astype(F32)
        for g, w in enumerate(POOL_WINDOWS):
            ln = slice(g * LANE, (g + 1) * LANE)
            acc = ext[pl.ds(halo, tq), ln]
            for j in range(1, w):
                acc = acc + ext[pl.ds(halo - j, tq), ln]
            pooled = acc / _pool_count(pos, w) - u_ref[:, ln]
            mixed = jnp.dot(pooled.astype(BF16), pw_ref[g].astype(BF16), preferred_element_type=F32)
            y_ref[:, ln] = (mixed * sc_ref[:, ln]).astype(BF16)

    return pl.pallas_call(
        body,
        grid=(n // tq,),
        in_specs=[
            pl.BlockSpec((tq, POOL_DIM), lambda i: (i, 0)),
            _halo_spec(halo, POOL_DIM, tq, -1)(n // halo),
            pl.BlockSpec((4, LANE, LANE), lambda i: (0, 0, 0)),
            pl.BlockSpec((1, POOL_DIM), lambda i: (0, 0)),
        ],
        out_specs=pl.BlockSpec((tq, POOL_DIM), lambda i: (i, 0)),
        out_shape=_sds((n, POOL_DIM), BF16),
        scratch_shapes=[pltpu.VMEM((halo + tq, POOL_DIM), F32)],
        compiler_params=_params(("arbitrary",)),
        name=name,
    )(u, u, pw, scale)


def pool_bwd(dy, u, pw, scale, seq, name):
    n = u.shape[0]
    tq, halo = TQ, POOL_HALO
    tps = seq // tq
    nt = (((1,), (1,)), ((), ()))
    tn = (((0,), (0,)), ((), ()))

    def body(dy_ref, dyn_ref, u_ref, halo_ref, pw_ref, sc_ref, du_ref, dpw_ref, dsc_ref, ext, ext2):
        i = pl.program_id(0)
        t0 = i % tps

        @pl.when(i == 0)
        def _():
            dpw_ref[...] = jnp.zeros_like(dpw_ref)
            dsc_ref[...] = jnp.zeros_like(dsc_ref)

        ext[0:halo, :] = jnp.where(t0 == 0, 0.0, halo_ref[...])
        ext[halo:, :] = u_ref[...]
        pos = (t0 * tq + lax.broadcasted_iota(jnp.int32, (tq, 1), 0)).astype(F32)
        dyv = dy_ref[...].astype(F32)
        dynv = jnp.where(t0 == tps - 1, 0.0, dyn_ref[...].astype(F32))
        for g, w in enumerate(POOL_WINDOWS):
            ln = slice(g * LANE, (g + 1) * LANE)
            wg = pw_ref[g].astype(BF16)
            acc = ext[pl.ds(halo, tq), ln]
            for j in range(1, w):
                acc = acc + ext[pl.ds(halo - j, tq), ln]
            pooled = (acc / _pool_count(pos, w) - u_ref[:, ln]).astype(BF16)
            mixed = jnp.dot(pooled, wg, preferred_element_type=F32)
            dsc_ref[:, ln] += jnp.sum(dyv[:, ln] * mixed, axis=0, keepdims=True)
            dmix = (dyv[:, ln] * sc_ref[:, ln]).astype(BF16)
            dpw_ref[g] += lax.dot_general(pooled, dmix, tn, preferred_element_type=F32)
            dpool = lax.dot_general(dmix, wg, nt, preferred_element_type=F32)
            dmix_n = (dynv[:, ln] * sc_ref[:, ln]).astype(BF16)
            dpool_n = lax.dot_general(dmix_n, wg, nt, preferred_element_type=F32)
            ext2[0:tq, ln] = dpool / _pool_count(pos, w)
            ext2[tq:, ln] = dpool_n * (1.0 / w)
            acc2 = ext2[pl.ds(0, tq), ln]
            for j in range(1, w):
                acc2 = acc2 + ext2[pl.ds(j, tq), ln]
            du_ref[:, ln] = (acc2 - dpool).astype(BF16)

    return pl.pallas_call(
        body,
        grid=(n // tq,),
        in_specs=[
            pl.BlockSpec((tq, POOL_DIM), lambda i: (i, 0)),
            _halo_spec(halo, POOL_DIM, tq, +1)(n // halo),
            pl.BlockSpec((tq, POOL_DIM), lambda i: (i, 0)),
            _halo_spec(halo, POOL_DIM, tq, -1)(n // halo),
            pl.BlockSpec((4, LANE, LANE), lambda i: (0, 0, 0)),
            pl.BlockSpec((1, POOL_DIM), lambda i: (0, 0)),
        ],
        out_specs=[
            pl.BlockSpec((tq, POOL_DIM), lambda i: (i, 0)),
            pl.BlockSpec((4, LANE, LANE), lambda i: (0, 0, 0)),
            pl.BlockSpec((1, POOL_DIM), lambda i: (0, 0)),
        ],
        out_shape=[_sds((n, POOL_DIM), BF16), _sds((4, LANE, LANE), F32), _sds((1, POOL_DIM), F32)],
        scratch_shapes=[pltpu.VMEM((halo + tq, POOL_DIM), F32), pltpu.VMEM((tq + halo, POOL_DIM), F32)],
        compiler_params=_params(("arbitrary",)),
        name=name,
    )(dy, dy, u, u, pw, scale)


CONF_HALO = 32
TQC = 256


def _conf_post(c, g, b):
    return _silu(_ln(c, g, b))


def conf_fwd(vg, w, b, lng, lnb, seq, name):
    n = vg.shape[0]
    tq, halo, kk = TQC, CONF_HALO, CONF_K
    tps = seq // tq
    c = CONF_DIM

    def body(vg_ref, halo_ref, w_ref, b_ref, lng_ref, lnb_ref, y_ref, conv_ref, ext):
        t0 = pl.program_id(0) % tps
        hv = halo_ref[...]
        ext[0:halo, :] = jnp.where(t0 == 0, 0.0, hv[:, :c] * jax.nn.sigmoid(hv[:, c:]))
        ext[halo:, :] = vg_ref[:, :c] * jax.nn.sigmoid(vg_ref[:, c:])
        conv = _conv_taps(ext, w_ref, halo, tq, kk) + b_ref[...]
        conv_ref[...] = conv
        y_ref[...] = _conf_post(conv, lng_ref[...], lnb_ref[...]).astype(BF16)

    one = pl.BlockSpec((1, c), lambda i: (0, 0))
    return pl.pallas_call(
        body,
        grid=(n // tq,),
        in_specs=[pl.BlockSpec((tq, 2 * c), lambda i: (i, 0)), _halo_spec(halo, 2 * c, tq, -1)(n // halo), pl.BlockSpec((kk, c), lambda i: (0, 0)), one, one, one],
        out_specs=[pl.BlockSpec((tq, c), lambda i: (i, 0)), pl.BlockSpec((tq, c), lambda i: (i, 0))],
        out_shape=[_sds((n, c), BF16), _sds((n, c), F32)],
        scratch_shapes=[pltpu.VMEM((halo + tq, c), F32)],
        compiler_params=_params(("arbitrary",)),
        name=name,
    )(vg, vg, w, b, lng, lnb)


def conf_bwd(dy, conv, vg, w, lng, lnb, seq, name):
    n = vg.shape[0]
    tq, halo, kk = TQC, CONF_HALO, CONF_K
    tps = seq // tq
    c = CONF_DIM

    def body(dy_ref, dyn_ref, conv_ref, convn_ref, vg_ref, halo_ref, w_ref, lng_ref, lnb_ref, dvg_ref, dw_ref, db_ref, dlng_ref, dlnb_ref, ext, ext2):
        i = pl.program_id(0)
        t0 = i % tps

        @pl.when(i == 0)
        def _():
            for r in (dw_ref, db_ref, dlng_ref, dlnb_ref):
                r[...] = jnp.zeros_like(r)

        _, vjp = jax.vjp(_conf_post, conv_ref[...], lng_ref[...], lnb_ref[...])
        dconv, dlng, dlnb = vjp(dy_ref[...].astype(F32))
        _, vjpn = jax.vjp(_conf_post, convn_ref[...], lng_ref[...], lnb_ref[...])
        dconv_n = vjpn(dyn_ref[...].astype(F32))[0]
        ext2[0:tq, :] = dconv
        ext2[tq:, :] = jnp.where(t0 == tps - 1, 0.0, dconv_n)
        dh = _conv_taps_t(ext2, w_ref, tq, kk)
        hv = halo_ref[...]
        ext[0:halo, :] = jnp.where(t0 == 0, 0.0, hv[:, :c] * jax.nn.sigmoid(hv[:, c:]))
        v = vg_ref[:, :c]
        s = jax.nn.sigmoid(vg_ref[:, c:])
        ext[halo:, :] = v * s
        _conv_dw(ext, ext2, dw_ref, halo, tq, kk)
        db_ref[...] += jnp.sum(dconv, axis=0, keepdims=True)
        dlng_ref[...] += dlng
        dlnb_ref[...] += dlnb
        dvg_ref[:, :c] = (dh * s).astype(BF16)
        dvg_ref[:, c:] = (dh * v * s * (1.0 - s)).astype(BF16)

    one = pl.BlockSpec((1, c), lambda i: (0, 0))
    tile = pl.BlockSpec((tq, c), lambda i: (i, 0))
    nxt = _halo_spec(halo, c, tq, +1)(n // halo)
    return pl.pallas_call(
        body,
        grid=(n // tq,),
        in_specs=[tile, nxt, tile, nxt, pl.BlockSpec((tq, 2 * c), lambda i: (i, 0)), _halo_spec(halo, 2 * c, tq, -1)(n // halo),
                  pl.BlockSpec((kk, c), lambda i: (0, 0)), one, one],
        out_specs=[pl.BlockSpec((tq, 2 * c), lambda i: (i, 0)), pl.BlockSpec((kk, c), lambda i: (0, 0)), one, one, one],
        out_shape=[_sds((n, 2 * c), BF16), _sds((kk, c), F32), _sds((1, c), F32), _sds((1, c), F32), _sds((1, c), F32)],
        scratch_shapes=[pltpu.VMEM((halo + tq, c), F32), pltpu.VMEM((tq + halo, c), F32)],
        compiler_params=_params(("arbitrary",)),
        name=name,
    )(dy, dy, conv, conv, vg, vg, w, lng, lnb)


TL = 256


def _expm1_neg(t):
    p = t * (1.0 + t * (1.0 / 2 + t * (1.0 / 6 + t * (1.0 / 24 + t * (1.0 / 120)))))
    return jnp.where(t > -0.1, p, jnp.exp(t) - 1.0)


def _lru_gate(xc, ra, ia, ba, bx, lam):
    r = jax.nn.sigmoid(ra + ba)
    i = jax.nn.sigmoid(ia + bx)
    log_a = -LRU_C * r * jax.nn.softplus(-lam)
    return jnp.exp(log_a), jnp.sqrt(-_expm1_neg(2.0 * log_a)) * (i * xc)


def _lru_out(h, gr):
    return h * jax.nn.gelu(gr)


def _scan_rows(a, b, tq, reverse):
    r8 = lax.broadcasted_iota(jnp.int32, (tq, 1), 0) % SUB
    for d in (1, 2, 4):
        sh = tq - d if reverse else d
        valid = (r8 < SUB - d) if reverse else (r8 >= d)
        a_s = pltpu.roll(a, sh, 0)
        b_s = pltpu.roll(b, sh, 0)
        b = jnp.where(valid, a * b_s, 0.0) + b
        a = jnp.where(valid, a * a_s, a)
    ng = tq // SUB
    edge = 0 if reverse else SUB - 1
    out_a, out_b = [None] * ng, [None] * ng
    ca = cb = None
    for g in (reversed(range(ng)) if reverse else range(ng)):
        ag, bg = a[g * SUB : (g + 1) * SUB, :], b[g * SUB : (g + 1) * SUB, :]
        if ca is not None:
            bg = bg + ag * cb
            ag = ag * ca
        out_a[g], out_b[g] = ag, bg
        ca, cb = ag[edge : edge + 1, :], bg[edge : edge + 1, :]
    return jnp.concatenate(out_a, axis=0), jnp.concatenate(out_b, axis=0)


def _row_of(v, r, tq):
    row = lax.broadcasted_iota(jnp.int32, (tq, 1), 0)
    return jnp.sum(jnp.where(row == r, v, 0.0), axis=0, keepdims=True)


def _head_mm(xc, w_ref):
    return jnp.concatenate(
        [
            jnp.dot(xc[:, h * LANE : (h + 1) * LANE].astype(BF16), w_ref[h].astype(BF16), preferred_element_type=F32)
            for h in range(LRU_HEADS)
        ],
        axis=1,
    )


def lru_fwd(xr, gr, cw, cb, wa, ba, wx, bx, lam, seq, name):
    n = xr.shape[0]
    tq = TL
    tps = seq // tq
    c = LRU_DIM

    def body(xr_ref, halo_ref, gr_ref, cw_ref, cb_ref, wa_ref, ba_ref, wx_ref, bx_ref, lam_ref, y_ref, h_ref, hc, ext):
        t0 = pl.program_id(0) % tps

        @pl.when(t0 == 0)
        def _():
            hc[...] = jnp.zeros_like(hc)

        ext[0:SUB, :] = jnp.where(t0 == 0, 0.0, halo_ref[...])
        ext[SUB:, :] = xr_ref[...]
        xc = _conv_taps(ext, cw_ref, SUB, tq, 4) + cb_ref[...]
        a, b = _lru_gate(xc, _head_mm(xc, wa_ref), _head_mm(xc, wx_ref), ba_ref[...], bx_ref[...], lam_ref[...])
        acum, h0 = _scan_rows(a, b, tq, False)
        h = h0 + acum * hc[0:1, :]
        h_ref[...] = h
        hc[0:1, :] = h_ref[tq - 1 : tq, :]
        y_ref[...] = _lru_out(h, gr_ref[...]).astype(BF16)

    one = pl.BlockSpec((1, c), lambda i: (0, 0))
    tile = pl.BlockSpec((tq, c), lambda i: (i, 0))
    hw = pl.BlockSpec((LRU_HEADS, LANE, LANE), lambda i: (0, 0, 0))
    return pl.pallas_call(
        body,
        grid=(n // tq,),
        in_specs=[tile, _halo_spec(SUB, c, tq, -1)(n // SUB), tile, pl.BlockSpec((4, c), lambda i: (0, 0)), one, hw, one, hw, one, one],
        out_specs=[tile, tile],
        out_shape=[_sds((n, c), BF16), _sds((n, c), F32)],
        scratch_shapes=[pltpu.VMEM((SUB, c), F32), pltpu.VMEM((SUB + tq, c), F32)],
        compiler_params=_params(("arbitrary",)),
        name=name,
    )(xr, xr, gr, cw, cb, wa, ba, wx, bx, lam)


def lru_bwd(dy, xr, gr, h, cw, cb, wa, ba, wx, bx, lam, seq, name):
    n = xr.shape[0]
    tq = TL
    tps = seq // tq
    ntile = n // tq
    c = LRU_DIM
    per = tq // SUB
    nt = (((1,), (1,)), ((), ()))
    tn = (((0,), (0,)), ((), ()))

    def rev(i):
        return (i // tps) * tps + (tps - 1 - i % tps)

    def body(dy_ref, xr_ref, halo_ref, gr_ref, h_ref, hprev_ref, cw_ref, cb_ref, wa_ref, ba_ref, wx_ref, bx_ref, lam_ref,
             dxr_ref, dgr_ref, dcw_ref, dcb_ref, dwa_ref, dba_ref, dwx_ref, dbx_ref, dlam_ref, carry, ext, ext2):
        i = pl.program_id(0)
        step = i % tps
        t0 = tps - 1 - step

        @pl.when(step == 0)
        def _():
            carry[...] = jnp.zeros_like(carry)
            ext2[tq:, :] = jnp.zeros((SUB, c), F32)

        @pl.when(i == 0)
        def _():
            for r in (dcw_ref, dcb_ref, dwa_ref, dba_ref, dwx_ref, dbx_ref, dlam_ref):
                r[...] = jnp.zeros_like(r)

        ext[0:SUB, :] = jnp.where(t0 == 0, 0.0, halo_ref[...])
        ext[SUB:, :] = xr_ref[...]
        xc = _conv_taps(ext, cw_ref, SUB, tq, 4) + cb_ref[...]
        (a, _), vjp_gate = jax.vjp(_lru_gate, xc, _head_mm(xc, wa_ref), _head_mm(xc, wx_ref), ba_ref[...], bx_ref[...], lam_ref[...])
        hv = h_ref[...]
        _, vjp_out = jax.vjp(_lru_out, hv, gr_ref[...])
        dh, dgr = vjp_out(dy_ref[...].astype(F32))
        dgr_ref[...] = dgr.astype(BF16)
        row = lax.broadcasted_iota(jnp.int32, (tq, 1), 0)
        a_up = jnp.where(row == tq - 1, carry[0:1, :], pltpu.roll(a, tq - 1, 0))
        acum, l0 = _scan_rows(a_up, dh, tq, True)
        lamv = l0 + acum * carry[1:2, :]
        carry[0:1, :] = _row_of(a, 0, tq)
        carry[1:2, :] = _row_of(lamv, 0, tq)
        hprev = jnp.where(row == 0, jnp.where(t0 == 0, 0.0, hprev_ref[SUB - 1 : SUB, :]), pltpu.roll(hv, 1, 0))
        dxc, dra, dia, dba, dbx, dlam = vjp_gate((lamv * hprev, lamv))
        dba_ref[...] += dba
        dbx_ref[...] += dbx
        dlam_ref[...] += dlam
        pieces = []
        for hh in range(LRU_HEADS):
            ln = slice(hh * LANE, (hh + 1) * LANE)
            xh = xc[:, ln].astype(BF16)
            drh = dra[:, ln].astype(BF16)
            dih = dia[:, ln].astype(BF16)
            dwa_ref[hh] += lax.dot_general(xh, drh, tn, preferred_element_type=F32)
            dwx_ref[hh] += lax.dot_general(xh, dih, tn, preferred_element_type=F32)
            pieces.append(
                lax.dot_general(drh, wa_ref[hh].astype(BF16), nt, preferred_element_type=F32)
                + lax.dot_general(dih, wx_ref[hh].astype(BF16), nt, preferred_element_type=F32)
            )
        dxc = dxc + jnp.concatenate(pieces, axis=1)
        ext2[0:tq, :] = dxc
        dxr_ref[...] = _conv_taps_t(ext2, cw_ref, tq, 4).astype(BF16)
        ext2[tq:, :] = ext2[0:SUB, :]
        _conv_dw(ext, ext2, dcw_ref, SUB, tq, 4)
        dcb_ref[...] += jnp.sum(dxc, axis=0, keepdims=True)

    one = pl.BlockSpec((1, c), lambda i: (0, 0))
    tile = pl.BlockSpec((tq, c), lambda i: (rev(i), 0))
    prev = pl.BlockSpec((SUB, c), lambda i: (jnp.maximum(rev(i) * per - 1, 0), 0))
    hw = pl.BlockSpec((LRU_HEADS, LANE, LANE), lambda i: (0, 0, 0))
    cw4 = pl.BlockSpec((4, c), lambda i: (0, 0))
    return pl.pallas_call(
        body,
        grid=(ntile,),
        in_specs=[tile, tile, prev, tile, tile, prev, cw4, one, hw, one, hw, one, one],
        out_specs=[tile, tile, cw4, one, hw, one, hw, one, one],
        out_shape=[_sds((n, c), BF16), _sds((n, c), BF16), _sds((4, c), F32), _sds((1, c), F32), _sds((LRU_HEADS, LANE, LANE), F32),
                   _sds((1, c), F32), _sds((LRU_HEADS, LANE, LANE), F32), _sds((1, c), F32), _sds((1, c), F32)],
        scratch_shapes=[pltpu.VMEM((SUB, c), F32), pltpu.VMEM((SUB + tq, c), F32), pltpu.VMEM((tq + SUB, c), F32)],
        compiler_params=_params(("arbitrary",)),
        name=name,
    )(dy, xr, xr, gr, h, h, cw, cb, wa, ba, wx, bx, lam)


def ada_fwd(c_all, w, b, name):
    nl, _, cols = w.shape
    nb = c_all.shape[0]

    def body(c_ref, w_ref, b_ref, o_ref):
        sc = _silu(c_ref[...]).astype(BF16)
        o_ref[0] = jnp.dot(sc, w_ref[0].astype(BF16), preferred_element_type=F32) + b_ref[0]

    return pl.pallas_call(
        body,
        grid=(nl,),
        in_specs=[pl.BlockSpec((nb, D), lambda l: (0, 0)), pl.BlockSpec((1, D, cols), lambda l: (l, 0, 0)), pl.BlockSpec((1, 1, cols), lambda l: (l, 0, 0))],
        out_specs=pl.BlockSpec((1, nb, cols), lambda l: (l, 0, 0)),
        out_shape=_sds((nl, nb, cols), F32),
        compiler_params=_params(("arbitrary",)),
        name=name,
    )(c_all, w, b)


def ada_bwd(c_all, dmod, name):
    nl, nb, cols = dmod.shape

    def body(c_ref, d_ref, o_ref):
        sc = _silu(c_ref[...]).astype(BF16)
        o_ref[0] = lax.dot_general(sc, d_ref[0].astype(BF16), (((0,), (0,)), ((), ())), preferred_element_type=F32)

    return pl.pallas_call(
        body,
        grid=(nl,),
        in_specs=[pl.BlockSpec((nb, D), lambda l: (0, 0)), pl.BlockSpec((1, nb, cols), lambda l: (l, 0, 0))],
        out_specs=pl.BlockSpec((1, D, cols), lambda l: (l, 0, 0)),
        out_shape=_sds((nl, D, cols), F32),
        compiler_params=_params(("arbitrary",)),
        name=name,
    )(c_all, dmod)


def loss_grad(y, target, name):
    n = y.shape[0]

    def body(y_ref, t_ref, dy_ref, l_ref, acc):
        i = pl.program_id(0)

        @pl.when(i == 0)
        def _():
            acc[...] = jnp.zeros_like(acc)

        e = y_ref[...] - t_ref[...]
        dy_ref[...] = e * (1.0 / D)
        acc[...] += jnp.sum(e * e, axis=0, keepdims=True)

        @pl.when(i == n // TM - 1)
        def _():
            l_ref[...] = jnp.full((1, LANE), 0.5 / D, F32) * jnp.sum(acc[...])

    row = pl.BlockSpec((TM, D), lambda i: (i, 0))
    return pl.pallas_call(
        body,
        grid=(n // TM,),
        in_specs=[row, row],
        out_specs=[row, pl.BlockSpec((1, LANE), lambda i: (0, 0))],
        out_shape=[_sds((n, D), F32), _sds((1, LANE), F32)],
        scratch_shapes=[pltpu.VMEM((1, D), F32)],
        compiler_params=_params(("arbitrary",)),
        name=name,
    )(y, target)


def sum_parts(parts, name):
    ns, r, _ = parts.shape
    tr = _row_tile(r, 1024)

    def body(p_ref, o_ref):
        acc = p_ref[0]
        for k in range(1, ns):
            acc = acc + p_ref[k]
        o_ref[...] = acc

    return pl.pallas_call(
        body,
        grid=(r // tr,),
        in_specs=[pl.BlockSpec((ns, tr, LANE), lambda i: (0, i, 0))],
        out_specs=pl.BlockSpec((tr, LANE), lambda i: (i, 0)),
        out_shape=_sds((r, LANE), F32),
        compiler_params=_params(("arbitrary",)),
        name=name,
    )(parts)


def _row_tile(r, cap):
    if r <= cap:
        return r
    best = None
    for t in range(16, cap + 1, 16):
        if r % t == 0:
            best = t
    assert best is not None, r
    return best


def adamw(w, m, v, gparts, name):
    ng, r, c = w.shape
    ns = gparts.shape[0]
    tr = _row_tile(r, min(512, 256 * 1024 // c))
    c1 = 1.0 - B1**STEP
    c2 = 1.0 - B2**STEP

    def body(w_ref, m_ref, v_ref, g_ref, go_ref, d_ref, mo_ref, vo_ref):
        g = g_ref[0, 0].astype(F32)
        for k in range(1, ns):
            g = g + g_ref[k, 0].astype(F32)
        mn = B1 * m_ref[0] + (1.0 - B1) * g
        vn = B2 * v_ref[0] + (1.0 - B2) * (g * g)
        go_ref[0] = g
        mo_ref[0] = mn
        vo_ref[0] = vn
        d_ref[0] = -LR * ((mn / c1) / (jnp.sqrt(vn / c2) + AEPS) + WD * w_ref[0])

    tile = pl.BlockSpec((1, tr, c), lambda b, i: (b, i, 0))
    return pl.pallas_call(
        body,
        grid=(ng, r // tr),
        in_specs=[tile, tile, tile, pl.BlockSpec((ns, 1, tr, c), lambda b, i: (0, b, i, 0))],
        out_specs=[tile, tile, tile, tile],
        out_shape=[_sds((ng, r, c), F32)] * 4,
        compiler_params=_params(("arbitrary", "arbitrary")),
        name=name,
    )(w, m, v, gparts)


WEIGHTS = ["ada_w", "ada_b", "ln_g", "ln_b", "ffn_w_in", "ffn_w_out", "ev_w_in", "ssd_conv_w", "ssd_conv_b", "ssd_dt_bias",
           "ssd_a_log", "ssd_d", "ssd_norm_g", "pool_w", "pool_scale", "ev_w_out", "od_w_in", "conf_dw_w", "conf_dw_b",
           "conf_ln_g", "conf_ln_b", "lru_conv_w", "lru_conv_b", "lru_wa", "lru_ba", "lru_wx", "lru_bx", "lru_lambda", "od_w_out"]
BIG = ("ada_w", "ffn_w_in", "ffn_w_out", "ev_w_in", "ev_w_out", "od_w_in", "od_w_out")
SMALL = {
    "ada_b": ((4, 9216), None), "ln_g": ((4, 3, 1024), 2), "ln_b": ((4, 3, 1024), 2),
    "ssd_conv_w": ((2, 4, 1536), 2), "ssd_conv_b": ((2, 1536), None), "ssd_dt_bias": ((2, 16), None),
    "ssd_a_log": ((2, 16), None), "ssd_d": ((2, 16), None), "ssd_norm_g": ((2, 1024), None),
    "pool_w": ((2, 4, 128, 128), None), "pool_scale": ((2, 512), None),
    "conf_dw_w": ((2, 31, 512), 2), "conf_dw_b": ((2, 512), 1), "conf_ln_g": ((2, 512), 1), "conf_ln_b": ((2, 512), 1),
    "lru_conv_w": ((2, 4, 1024), 2), "lru_conv_b": ((2, 1024), 1), "lru_wa": ((2, 8, 128, 128), None),
    "lru_ba": ((2, 1024), 1), "lru_wx": ((2, 8, 128, 128), None), "lru_bx": ((2, 1024), 1), "lru_lambda": ((2, 1024), 1),
}
PACK_ROWS = 2 * SUB * LANE


def _rows_of_piece(shape):
    return -(-math.prod(shape) // (SUB * LANE)) * SUB


def _pack(arrs, mult=PACK_ROWS):
    rows = [jnp.pad(a.reshape(-1), (0, _rows_of_piece(a.shape) * LANE - a.size)).reshape(-1, LANE) for a in arrs]
    buf = jnp.concatenate(rows, axis=0)
    pad = (-buf.shape[0]) % (mult // LANE)
    return jnp.pad(buf, ((0, pad), (0, 0)))


def _unpack(buf, shapes, lead=()):
    out, off = [], 0
    for s in shapes:
        k, nr = math.prod(s), _rows_of_piece(s)
        piece = buf[..., off : off + nr, :].reshape(lead + (nr * LANE,))
        out.append(piece[..., :k].reshape(lead + tuple(s)))
        off += nr
    return out


def _pad_lanes(v):
    return jnp.pad(v, (0, LANE - v.shape[0]))[None]


def kernel(x, c, ada_w, ada_b, ln_g, ln_b, ffn_w_in, ffn_w_out, ev_w_in, ssd_conv_w, ssd_conv_b, ssd_dt_bias, ssd_a_log, ssd_d, ssd_norm_g, pool_w, pool_scale, ev_w_out, od_w_in, conf_dw_w, conf_dw_b, conf_ln_g, conf_ln_b, lru_conv_w, lru_conv_b, lru_wa, lru_ba, lru_wx, lru_bx, lru_lambda, od_w_out, loss_target, m_ada_w, m_ada_b, m_ln_g, m_ln_b, m_ffn_w_in, m_ffn_w_out, m_ev_w_in, m_ssd_conv_w, m_ssd_conv_b, m_ssd_dt_bias, m_ssd_a_log, m_ssd_d, m_ssd_norm_g, m_pool_w, m_pool_scale, m_ev_w_out, m_od_w_in, m_conf_dw_w, m_conf_dw_b, m_conf_ln_g, m_conf_ln_b, m_lru_conv_w, m_lru_conv_b, m_lru_wa, m_lru_ba, m_lru_wx, m_lru_bx, m_lru_lambda, m_od_w_out, v_ada_w, v_ada_b, v_ln_g, v_ln_b, v_ffn_w_in, v_ffn_w_out, v_ev_w_in, v_ssd_conv_w, v_ssd_conv_b, v_ssd_dt_bias, v_ssd_a_log, v_ssd_d, v_ssd_norm_g, v_pool_w, v_pool_scale, v_ev_w_out, v_od_w_in, v_conf_dw_w, v_conf_dw_b, v_conf_ln_g, v_conf_ln_b, v_lru_conv_w, v_lru_conv_b, v_lru_wa, v_lru_ba, v_lru_wx, v_lru_bx, v_lru_lambda, v_od_w_out):
    p = dict(locals())
    nb, seq, _ = x.shape
    n = nb * seq
    me = 4 * lax.axis_index("x") + 2 * lax.axis_index("y") + lax.axis_index("c")
    sharded = [k for k, (_, ax) in SMALL.items() if ax is not None]

    def cols_of(g):
        return jnp.moveaxis(g, 0, 1).reshape(g.shape[1], N_DEV * g.shape[2])

    def rows_of(g):
        return g.reshape(N_DEV * g.shape[1], g.shape[2])

    def ev_in_of(g):
        w = cols_of(g)
        return jnp.concatenate([w[:, :2560], w[:, 2576:], jnp.pad(w[:, 2560:2576], ((0, 0), (0, LANE - SSD_HEADS)))], axis=1)

    sh_ffn_in, sh_ffn_out = ffn_w_in.astype(BF16), ffn_w_out.astype(BF16)
    sh_mix_in = [ev_w_in.astype(BF16), od_w_in.astype(BF16)]
    sh_mix_out = [ev_w_out.astype(BF16), od_w_out.astype(BF16)]

    def ffn_items(l, i):
        return [(sh_ffn_in[l, i], True), (sh_ffn_out[l, i], True)]

    def mix_items(l):
        return [(sh_mix_in[l % 2][l // 2], True), (sh_mix_out[l % 2][l // 2], True)]

    sm_local_shapes = [p[k].shape for k in sharded]
    g_in, g_out, sm_all = exchange(ffn_items(0, 0) + [(_pack([p[k] for k in sharded] + [c]), True)], "ag_first")
    w_ffn = {(0, 0): (cols_of(g_in), rows_of(g_out))}
    w_mix = {}
    got = _unpack(sm_all, sm_local_shapes + [c.shape], lead=(N_DEV,))
    full = {k: p[k] for k, (_, ax) in SMALL.items() if ax is None}
    for k, g in zip(sharded, got[:-1]):
        full[k] = jnp.moveaxis(g, 0, SMALL[k][1]).reshape(SMALL[k][0])
    c_all = got[-1].reshape(N_DEV * nb, D)

    cols = ada_w.shape[-1]
    ada_b_loc = lax.dynamic_slice_in_dim(ada_b, me * cols, cols, axis=1)[:, None, :]
    mod_cols = ada_fwd(c_all, ada_w, ada_b_loc, "ada_fwd")
    (mod_x,) = exchange([(mod_cols.reshape(DEPTH, N_DEV, nb, cols).transpose(1, 0, 2, 3), False)], "a2a_mod")
    mod = mod_x.transpose(1, 2, 0, 3).reshape(DEPTH, nb, N_MOD, 1, D)

    def vec(l, j):
        return mod[l, :, j]

    def row(a):
        return a[None]

    xs = x.reshape(n, D)
    saved = []
    for l in range(DEPTH):
        s = {"x0": xs}
        e = l // 2
        s["h1"], s["g1"], s["u1"], s["a1"], g_in, g_out = ffn_up(xs, vec(l, 0), vec(l, 1), w_ffn[l, 0][0], seq, "ffn_up_c2", carry=ffn_items(l, 1))
        w_ffn[l, 1] = (cols_of(g_in), rows_of(g_out))
        x1, s["y1"], gm_in, gm_out = mm_postnorm([s["a1"]], w_ffn[l, 0][1], xs, vec(l, 2), row(full["ln_g"][l, 0]), row(full["ln_b"][l, 0]), 0.5, seq,
                                                 "ffn_down_c2", carry=mix_items(l))
        w_mix[l] = ((ev_in_of if l % 2 == 0 else cols_of)(gm_in), rows_of(gm_out))
        s["x1"] = x1
        if l % 2 == 0:
            s["h2"], s["z"], s["xbc"], s["u"], s["dtr"] = mod_mm(x1, vec(l, 3), vec(l, 4), w_mix[l][0], EV_SPLITS, seq, "ev_in")
            s["ya"], s["sprev"] = ssd_fwd(s["z"], s["xbc"], s["dtr"], full["ssd_conv_w"][e], row(full["ssd_conv_b"][e]), _pad_lanes(full["ssd_dt_bias"][e]),
                                          _pad_lanes(full["ssd_a_log"][e]), _pad_lanes(full["ssd_d"][e]), row(full["ssd_norm_g"][e]), seq, "ssd_fwd")
            s["yb"] = pool_fwd(s["u"], full["pool_w"][e], row(full["pool_scale"][e]), seq, "pool_fwd")
        else:
            s["h2"], s["vg"], s["xr"], s["gr"] = mod_mm(x1, vec(l, 3), vec(l, 4), w_mix[l][0], OD_SPLITS, seq, "od_in")
            s["ya"], s["conv"] = conf_fwd(s["vg"], full["conf_dw_w"][e], row(full["conf_dw_b"][e]), row(full["conf_ln_g"][e]), row(full["conf_ln_b"][e]), seq, "conf_fwd")
            s["yb"], s["hst"] = lru_fwd(s["xr"], s["gr"], full["lru_conv_w"][e], row(full["lru_conv_b"][e]), full["lru_wa"][e], row(full["lru_ba"][e]),
                                        full["lru_wx"][e], row(full["lru_bx"][e]), row(full["lru_lambda"][e]), seq, "lru_fwd")
        x2, s["y2"] = mm_postnorm([s["ya"], s["yb"]], w_mix[l][1], x1, vec(l, 5), row(full["ln_g"][l, 1]), row(full["ln_b"][l, 1]), 1.0, seq, "mix_out")
        s["x2"] = x2
        if l + 1 < DEPTH:
            s["h3"], s["g3"], s["u3"], s["a3"], g_in, g_out = ffn_up(x2, vec(l, 6), vec(l, 7), w_ffn[l, 1][0], seq, "ffn_up_c2", carry=ffn_items(l + 1, 0))
            w_ffn[l + 1, 0] = (cols_of(g_in), rows_of(g_out))
        else:
            s["h3"], s["g3"], s["u3"], s["a3"] = ffn_up(x2, vec(l, 6), vec(l, 7), w_ffn[l, 1][0], seq, "ffn_up")
        xs, s["y3"] = mm_postnorm([s["a3"]], w_ffn[l, 1][1], x2, vec(l, 8), row(full["ln_g"][l, 2]), row(full["ln_b"][l, 2]), 0.5, seq, "ffn_down")
        saved.append(s)

    dx, loss_row = loss_grad(xs, loss_target.reshape(n, D), "loss")
    loss = lax.psum(loss_row[0, 0], ("x", "y", "c"))

    sg = {k: [None] * shape[0] for k, (shape, _) in SMALL.items()}
    sg["ln_g"] = [[None] * 3 for _ in range(DEPTH)]
    sg["ln_b"] = [[None] * 3 for _ in range(DEPTH)]
    dmod = [[None] * N_MOD for _ in range(DEPTH)]
    pending, got_w = [], {}

    def cut_cols(g):
        r, cc = g.shape
        return g.reshape(r, N_DEV, cc // N_DEV).transpose(1, 0, 2)

    def cut_rows(g):
        r, cc = g.shape
        return g.reshape(N_DEV, r // N_DEV, cc)

    def take(room):
        sel = []
        for j, (_, a) in enumerate(pending):
            if a.size * a.dtype.itemsize <= room:
                sel.append(j)
                room -= a.size * a.dtype.itemsize
        items = [pending[j] for j in sel]
        pending[:] = [it for j, it in enumerate(pending) if j not in sel]
        return [k for k, _ in items], [(a, False) for _, a in items]

    room_down, room_up, room_mix = 8 * MIB, 12 * MIB, 6 * MIB

    def postnorm_backward(dxo, xin, y, g, lng, lnb, w, ks, coef, name, room):
        keys, carry = take(room)
        outs = postnorm_bwd(dxo, xin, y, g, lng, lnb, w, ks, coef, seq, name + "_c%d" % len(keys), carry=carry)
        got_w.update(zip(keys, outs[5 + len(ks):]))
        return outs[0], outs[1], outs[2 : 2 + len(ks)], outs[2 + len(ks)], outs[3 + len(ks)], outs[4 + len(ks)]

    def proj_backward(dparts, w, xin, scv, dxres, name):
        keys, carry = take(room_mix)
        outs = proj_bwd_in(dparts, w, xin, scv, dxres, seq, name + "_c%d" % len(keys), carry=carry)
        got_w.update(zip(keys, outs[3:]))
        return outs[:3]

    def ffn_backward(l, i, dxo, s, xin, hk, gk, uk, ak, yk, jbase, lnj):
        dxres, dy, (da,), dmod[l][jbase + 2], sg["ln_g"][l][lnj], sg["ln_b"][l][lnj] = postnorm_backward(
            dxo, xin, s[yk], vec(l, jbase + 2), row(full["ln_g"][l, lnj]), row(full["ln_b"][l, lnj]), w_ffn[l, i][1], [FF], 0.5, "ffn_down_bwd", room_down)
        keys, carry = take(room_up)
        outs = ffn_bwd_in(da, s[gk], s[uk], w_ffn[l, i][0], xin, vec(l, jbase + 1), dxres, seq, "ffn_up_bwd_c%d" % len(keys), carry=carry)
        dgu, dxi, dmod[l][jbase], dmod[l][jbase + 1] = outs[:4]
        got_w.update(zip(keys, outs[4:]))
        pending.append((("ffn_out", l, i), cut_rows(mm_tn(s[ak], dy, "wg_ffn_out"))))
        pending.append((("ffn_in", l, i), mm_tn(s[hk], dgu, "wg_ffn_in", cut=N_DEV)))
        return dxi

    for l in reversed(range(DEPTH)):
        s = saved[l]
        e = l // 2
        dx = ffn_backward(l, 1, dx, s, s["x2"], "h3", "g3", "u3", "a3", "y3", 6, 2)
        ks = [1024, POOL_DIM] if l % 2 == 0 else [CONF_DIM, LRU_DIM]
        dxres, dy, (dya, dyb), dmod[l][5], sg["ln_g"][l][1], sg["ln_b"][l][1] = postnorm_backward(
            dx, s["x1"], s["y2"], vec(l, 5), row(full["ln_g"][l, 1]), row(full["ln_b"][l, 1]), w_mix[l][1], ks, 1.0, "mix_out_bwd", room_mix)
        pending.append((("mix_out", l), cut_rows(jnp.concatenate([mm_tn(s["ya"], dy, "wg_mix_a"), mm_tn(s["yb"], dy, "wg_mix_b")], axis=0))))
        if l % 2 == 0:
            (dz, dxbc, ddt, sg["ssd_conv_w"][e], dcb, ddtb, dalog, ddsk, dng) = ssd_bwd(
                dya, s["z"], s["xbc"], s["dtr"], s["sprev"], full["ssd_conv_w"][e], row(full["ssd_conv_b"][e]), _pad_lanes(full["ssd_dt_bias"][e]),
                _pad_lanes(full["ssd_a_log"][e]), _pad_lanes(full["ssd_d"][e]), row(full["ssd_norm_g"][e]), seq, "ssd_bwd")
            sg["ssd_conv_b"][e], sg["ssd_norm_g"][e] = dcb[0], dng[0]
            sg["ssd_dt_bias"][e], sg["ssd_a_log"][e], sg["ssd_d"][e] = ddtb[0, :SSD_HEADS], dalog[0, :SSD_HEADS], ddsk[0, :SSD_HEADS]
            du, sg["pool_w"][e], dps = pool_bwd(dyb, s["u"], full["pool_w"][e], row(full["pool_scale"][e]), seq, "pool_bwd")
            sg["pool_scale"][e] = dps[0]
            dparts = [dz, dxbc, du, ddt]
            dx, dmod[l][3], dmod[l][4] = proj_backward(dparts, w_mix[l][0], s["x1"], vec(l, 4), dxres, "ev_in_bwd")
            gz, gxbc, gu, gdt = [mm_tn(s["h2"], dp, "wg_ev_in") for dp in dparts]
            pending.append((("mix_in", l), cut_cols(jnp.concatenate([gz, gxbc, gdt[:, :SSD_HEADS], gu], axis=1))))
        else:
            dvg, sg["conf_dw_w"][e], dcb, dlg, dlb = conf_bwd(dya, s["conv"], s["vg"], full["conf_dw_w"][e], row(full["conf_ln_g"][e]), row(full["conf_ln_b"][e]), seq, "conf_bwd")
            sg["conf_dw_b"][e], sg["conf_ln_g"][e], sg["conf_ln_b"][e] = dcb[0], dlg[0], dlb[0]
            (dxr, dgr, sg["lru_conv_w"][e], dcb, sg["lru_wa"][e], dba, sg["lru_wx"][e], dbx, dlam) = lru_bwd(
                dyb, s["xr"], s["gr"], s["hst"], full["lru_conv_w"][e], row(full["lru_conv_b"][e]), full["lru_wa"][e], row(full["lru_ba"][e]),
                full["lru_wx"][e], row(full["lru_bx"][e]), row(full["lru_lambda"][e]), seq, "lru_bwd")
            sg["lru_conv_b"][e], sg["lru_ba"][e], sg["lru_bx"][e], sg["lru_lambda"][e] = dcb[0], dba[0], dbx[0], dlam[0]
            dparts = [dvg, dxr, dgr]
            dx, dmod[l][3], dmod[l][4] = proj_backward(dparts, w_mix[l][0], s["x1"], vec(l, 4), dxres, "od_in_bwd")
            pending.append((("mix_in", l), cut_cols(jnp.concatenate([mm_tn(s["h2"], dp, "wg_od_in") for dp in dparts], axis=1))))
        dx = ffn_backward(l, 0, dx, s, s["x0"], "h1", "g1", "u1", "a1", "y1", 0, 0)
    grad_x = dx.reshape(nb, seq, D)

    dmod_mine = jnp.stack([jnp.concatenate([d[:, 0, :] for d in dmod[l]], axis=-1) for l in range(DEPTH)])
    sg["ada_b"] = [jnp.sum(dmod_mine[l], axis=0) for l in range(DEPTH)]
    sg["ln_g"] = [jnp.concatenate(r, axis=0) for r in sg["ln_g"]]
    sg["ln_b"] = [jnp.concatenate(r, axis=0) for r in sg["ln_b"]]
    small_names = list(SMALL)
    sg_packed = _pack([jnp.stack(sg[k]).reshape(SMALL[k][0]) for k in small_names], N_DEV * PACK_ROWS)
    keys, carry = take(1 << 40)
    outs = exchange(carry + [(dmod_mine.reshape(DEPTH, nb, N_DEV, cols).transpose(2, 0, 1, 3), False),
                             (sg_packed.reshape(N_DEV, -1, LANE), False)], "x_last")
    got_w.update(zip(keys, outs))
    dmod_x, parts = outs[len(keys):]
    g_ada_w = ada_bwd(c_all, dmod_x.transpose(1, 0, 2, 3).reshape(DEPTH, N_DEV * nb, cols), "ada_bwd")

    (sg_sum,) = exchange([(sum_parts(parts, "sum_smallgrad"), True)], "ag_smallsum")
    summed = _unpack(sg_sum.reshape(-1, LANE), [SMALL[k][0] for k in small_names])
    grads = {}
    for k, g in zip(small_names, summed):
        ax = SMALL[k][1]
        grads[k] = g if ax is None else lax.dynamic_slice_in_dim(g, me * p[k].shape[ax], p[k].shape[ax], axis=ax)
    loc_shapes = [p[k].shape for k in small_names]
    whole = 512 * LANE
    _, d_s, m_s, v_s = adamw(_pack([p[k] for k in small_names], whole)[None], _pack([p["m_" + k] for k in small_names], whole)[None],
                             _pack([p["v_" + k] for k in small_names], whole)[None], _pack([grads[k] for k in small_names], whole)[None, None], "adamw_small")
    delta = dict(zip(small_names, _unpack(d_s[0], loc_shapes)))
    new_m = dict(zip(small_names, _unpack(m_s[0], loc_shapes)))
    new_v = dict(zip(small_names, _unpack(v_s[0], loc_shapes)))

    big_parts = {
        "ada_w": g_ada_w[None],
        "ffn_w_in": jnp.stack([jnp.stack([got_w["ffn_in", l, i] for i in range(2)], axis=1) for l in range(DEPTH)], axis=1),
        "ffn_w_out": jnp.stack([jnp.stack([got_w["ffn_out", l, i] for i in range(2)], axis=1) for l in range(DEPTH)], axis=1),
        "ev_w_in": jnp.stack([got_w["mix_in", l] for l in (0, 2)], axis=1),
        "ev_w_out": jnp.stack([got_w["mix_out", l] for l in (0, 2)], axis=1),
        "od_w_in": jnp.stack([got_w["mix_in", l] for l in (1, 3)], axis=1),
        "od_w_out": jnp.stack([got_w["mix_out", l] for l in (1, 3)], axis=1),
    }
    for k in BIG:
        w = p[k]
        r2 = (math.prod(w.shape[:-2]),) + w.shape[-2:]
        gp = big_parts[k]
        out = adamw(w.reshape(r2), p["m_" + k].reshape(r2), p["v_" + k].reshape(r2), gp.reshape((gp.shape[0],) + r2), "adamw_" + k)
        grads[k], delta[k], new_m[k], new_v[k] = [o.reshape(w.shape) for o in out]

    return (loss, grad_x, *[grads[k] for k in WEIGHTS], *[delta[k] for k in WEIGHTS], *[new_m[k] for k in WEIGHTS], *[new_v[k] for k in WEIGHTS])
```

```python
import math

import jax
import jax.numpy as jnp
from jax import lax
from jax.experimental import pallas as pl
from jax.experimental.pallas import tpu as pltpu

F32 = jnp.float32
BF16 = jnp.bfloat16
HI = lax.Precision.HIGHEST

N_DEV = 8
D = 1024
DEPTH = 4
N_MOD = 9
FF = 2816
ALPHA = (2.0 * DEPTH) ** 0.25
EPS = 1e-5
SSD_Q = 128
SSD_HEADS = 16
SSD_P = 64
SSD_N = 128
SSD_XBC = 1536
POOL_WINDOWS = (2, 4, 8, 16)
POOL_DIM = 512
CONF_DIM = 512
CONF_K = 31
LRU_DIM = 1024
LRU_HEADS = 8
LRU_C = 8.0
EV_SPLITS = (1024, 1536, 512, 128)
OD_SPLITS = (1024, 1024, 1024)
LR, B1, B2, AEPS, WD, STEP = 0.001, 0.9, 0.999, 1e-08, 0.01, 10

LANE = 128
SUB = 8
MIB = 1024 * 1024
VMEM_LIMIT = 48 * MIB
TM = 512
SPLIT_ROWS = 256


def _params(sem, vmem=VMEM_LIMIT):
    return pltpu.CompilerParams(dimension_semantics=sem, vmem_limit_bytes=vmem)


def _sds(shape, dtype):
    return jax.ShapeDtypeStruct(shape, dtype)


def _modulate(x, sh, sc):
    return x * (1.0 + sc) + sh


def _postnorm(x, y, g, lng, lnb, *, coef):
    z = ALPHA * x + coef * (1.0 + g) * y
    mu = jnp.mean(z, axis=-1, keepdims=True)
    zc = z - mu
    var = jnp.mean(zc * zc, axis=-1, keepdims=True)
    return zc * lax.rsqrt(var + EPS) * lng + lnb


def _postnorm_grads(x, y, g, lng, dxn, *, coef):
    kk = coef * (1.0 + g)
    z = ALPHA * x + kk * y
    zc = z - jnp.mean(z, axis=-1, keepdims=True)
    r = lax.rsqrt(jnp.mean(zc * zc, axis=-1, keepdims=True) + EPS)
    xhat = zc * r
    dxhat = dxn * lng
    dz = r * (dxhat - jnp.mean(dxhat, axis=-1, keepdims=True) - xhat * jnp.mean(dxhat * xhat, axis=-1, keepdims=True))
    rows = lambda v: jnp.sum(v, axis=0, keepdims=True)
    return ALPHA * dz, kk * dz, coef * rows(y * dz), rows(dxn * xhat), rows(dxn)


def _place():
    mx, my, mc = lax.axis_index("x"), lax.axis_index("y"), lax.axis_index("c")

    def at(r):
        px = 1 - mx if r & 4 else mx
        py = 1 - my if r & 2 else my
        pc = 1 - mc if r & 1 else mc
        return (px, py, pc), 4 * px + 2 * py + pc

    return 4 * mx + 2 * my + mc, at


def _carry_plan(items):
    hbm = pl.BlockSpec(memory_space=pltpu.HBM)
    k = len(items)
    shapes = [_sds((N_DEV,) + a.shape if g else a.shape, a.dtype) for a, g in items]
    scratch = [pltpu.SemaphoreType.DMA((k * (N_DEV - 1),)), pltpu.SemaphoreType.DMA((k * (N_DEV - 1),)), pltpu.SemaphoreType.DMA((k,))] if k else []
    return [hbm] * k, [hbm] * k, shapes, scratch


def _remote(src, dst, sems, s, pos):
    return pltpu.make_async_remote_copy(src_ref=src, dst_ref=dst, send_sem=sems[0].at[s], recv_sem=sems[1].at[s],
                                        device_id=pos, device_id_type=pl.DeviceIdType.MESH)


def _carry_start(gathers, x_refs, o_refs, sems):
    me, at = _place()
    for a, (gather, x_ref, o_ref) in enumerate(zip(gathers, x_refs, o_refs)):
        base = a * (N_DEV - 1)
        pltpu.make_async_copy(x_ref if gather else x_ref.at[me], o_ref.at[me], sems[2].at[a]).start()
        if gather:
            for s, r in enumerate((1, 4, 2, 6)):
                _remote(x_ref, o_ref.at[me], sems, base + s, at(r)[0]).start()
        else:
            for r in range(1, N_DEV):
                pos, pid = at(r)
                _remote(x_ref.at[pid], o_ref.at[me], sems, base + r - 1, pos).start()


def _carry_pass_on(gathers, x_refs, o_refs, sems):
    _, at = _place()
    sibling = at(1)[0]
    for a, (gather, x_ref, o_ref) in enumerate(zip(gathers, x_refs, o_refs)):
        if gather:
            base = a * (N_DEV - 1)
            for j, r in enumerate((4, 2, 6)):
                pos, pid = at(r)
                _remote(x_ref, o_ref.at[pid], sems, base + 1 + j, pos).wait_recv()
                _remote(o_ref.at[pid], o_ref.at[pid], sems, base + 4 + j, sibling).start()


def _carry_wait(gathers, x_refs, o_refs, sems):
    me, at = _place()
    for a, (gather, x_ref, o_ref) in enumerate(zip(gathers, x_refs, o_refs)):
        base = a * (N_DEV - 1)
        if gather:
            sib_pos, sib_id = at(1)
            _remote(x_ref, o_ref.at[sib_id], sems, base, sib_pos).wait_recv()
            for j, r in enumerate((4, 2, 6)):
                _remote(x_ref, o_ref.at[at(r | 1)[1]], sems, base + 4 + j, sib_pos).wait_recv()
            for s in range(N_DEV - 1):
                _remote(x_ref, o_ref.at[me], sems, base + s, sib_pos).wait_send()
            pltpu.make_async_copy(x_ref, o_ref.at[me], sems[2].at[a]).wait()
        else:
            for r in range(1, N_DEV):
                pos, pid = at(r)
                _remote(x_ref.at[pid], o_ref.at[pid], sems, base + r - 1, pos).wait_recv()
            for r in range(1, N_DEV):
                pos, pid = at(r)
                _remote(x_ref.at[pid], o_ref.at[me], sems, base + r - 1, pos).wait_send()
            pltpu.make_async_copy(x_ref.at[me], o_ref.at[me], sems[2].at[a]).wait()


def exchange(items, name):
    gathers = [g for _, g in items]
    k = len(items)
    in_specs, out_specs, shapes, scratch = _carry_plan(items)

    def body(*refs):
        x_refs, o_refs, sems = refs[:k], refs[k : 2 * k], refs[2 * k :]
        _carry_start(gathers, x_refs, o_refs, sems)
        _carry_pass_on(gathers, x_refs, o_refs, sems)
        _carry_wait(gathers, x_refs, o_refs, sems)

    return pl.pallas_call(
        body,
        in_specs=in_specs,
        out_specs=out_specs,
        out_shape=shapes,
        scratch_shapes=scratch,
        compiler_params=pltpu.CompilerParams(has_side_effects=True),
        name=name,
    )(*[a for a, _ in items])


def ffn_up(x, sh, sc, w, seq, name, carry=()):
    n = x.shape[0]
    tn = FF // 2
    nj = FF // tn
    ni = n // TM
    tps = seq // TM
    k = len(carry)
    gathers = [g for _, g in carry]
    c_in, c_out, c_shapes, c_scratch = _carry_plan(carry)

    def body(x_ref, sh_ref, sc_ref, w_hbm, *rest):
        cx, (h_ref, g_ref, u_ref, a_ref), co, (w_ref, w_sem, *sems) = rest[:k], rest[k : k + 4], rest[k + 4 : 2 * k + 4], rest[2 * k + 4 :]
        i = pl.program_id(0)

        @pl.when(i == 0)
        def _():
            if k:
                _carry_start(gathers, cx, co, sems)
            cp = pltpu.make_async_copy(w_hbm, w_ref, w_sem)
            cp.start()
            cp.wait()

        h = _modulate(x_ref[...], sh_ref[0], sc_ref[0]).astype(BF16)
        h_ref[...] = h
        for j in range(nj):
            g = jnp.dot(h, w_ref[:, j * tn : (j + 1) * tn], preferred_element_type=F32)
            u = jnp.dot(h, w_ref[:, FF + j * tn : FF + (j + 1) * tn], preferred_element_type=F32)
            g_ref[:, j * tn : (j + 1) * tn] = g.astype(BF16)
            u_ref[:, j * tn : (j + 1) * tn] = u.astype(BF16)
            a_ref[:, j * tn : (j + 1) * tn] = (g * jax.nn.sigmoid(g) * u).astype(BF16)
        if k:
            @pl.when(i == max(ni - 2, 0))
            def _():
                _carry_pass_on(gathers, cx, co, sems)

            @pl.when(i == ni - 1)
            def _():
                _carry_wait(gathers, cx, co, sems)

    vec = pl.BlockSpec((1, 1, D), lambda i: (i // tps, 0, 0))
    col = pl.BlockSpec((TM, FF), lambda i: (i, 0))
    return pl.pallas_call(
        body,
        grid=(ni,),
        in_specs=[pl.BlockSpec((TM, D), lambda i: (i, 0)), vec, vec, pl.BlockSpec(memory_space=pltpu.HBM)] + c_in,
        out_specs=[pl.BlockSpec((TM, D), lambda i: (i, 0)), col, col, col] + c_out,
        out_shape=[_sds((n, D), BF16), _sds((n, FF), BF16), _sds((n, FF), BF16), _sds((n, FF), BF16)] + c_shapes,
        scratch_shapes=[pltpu.VMEM((D, 2 * FF), BF16), pltpu.SemaphoreType.DMA] + c_scratch,
        compiler_params=_params(("arbitrary",)),
        name=name,
    )(x, sh, sc, w, *[a for a, _ in carry])


def mod_mm(x, sh, sc, w, splits, seq, name):
    n = x.shape[0]
    m = w.shape[1]
    tps = seq // TM
    offs = [sum(splits[:k]) for k in range(len(splits))]

    def body(x_ref, sh_ref, sc_ref, w_ref, h_ref, *outs):
        h = _modulate(x_ref[...], sh_ref[0], sc_ref[0]).astype(BF16)
        h_ref[...] = h
        for o_ref, off, wd in zip(outs, offs, splits):
            o_ref[...] = jnp.dot(h, w_ref[:, off : off + wd], preferred_element_type=F32)

    vec = pl.BlockSpec((1, 1, D), lambda i: (i // tps, 0, 0))
    return pl.pallas_call(
        body,
        grid=(n // TM,),
        in_specs=[pl.BlockSpec((TM, D), lambda i: (i, 0)), vec, vec, pl.BlockSpec((D, m), lambda i: (0, 0))],
        out_specs=[pl.BlockSpec((TM, D), lambda i: (i, 0))] + [pl.BlockSpec((TM, wd), lambda i: (i, 0)) for wd in splits],
        out_shape=[_sds((n, D), BF16)] + [_sds((n, wd), F32) for wd in splits],
        compiler_params=_params(("arbitrary",)),
        name=name,
    )(x, sh, sc, w)


def mm_postnorm(parts, w, x, g, lng, lnb, coef, seq, name, carry=()):
    n = x.shape[0]
    ni = n // TM
    tps = seq // TM
    ks = [p.shape[1] for p in parts]
    offs = [sum(ks[:k]) for k in range(len(ks))]
    npart = len(parts)
    nc = len(carry)
    gathers = [gt for _, gt in carry]
    c_in, c_out, c_shapes, c_scratch = _carry_plan(carry)

    def body(*refs):
        a_refs = refs[:npart]
        w_ref, x_ref, g_ref, lng_ref, lnb_ref = refs[npart : npart + 5]
        cx = refs[npart + 5 : npart + 5 + nc]
        xn_ref, y_ref = refs[npart + 5 + nc : npart + 7 + nc]
        co, sems = refs[npart + 7 + nc : npart + 7 + 2 * nc], refs[npart + 7 + 2 * nc :]
        i = pl.program_id(0)
        if nc:
            @pl.when(i == 0)
            def _():
                _carry_start(gathers, cx, co, sems)

        for r0 in range(0, TM, SPLIT_ROWS):
            rows = slice(r0, r0 + SPLIT_ROWS)
            y = None
            for a_ref, off, k in zip(a_refs, offs, ks):
                t = jnp.dot(a_ref[rows, :], w_ref[off : off + k, :], preferred_element_type=F32)
                y = t if y is None else y + t
            y_ref[rows, :] = y
            xn_ref[rows, :] = _postnorm(x_ref[rows, :], y, g_ref[0], lng_ref[...], lnb_ref[...], coef=coef)
        if nc:
            @pl.when(i == max(ni - 2, 0))
            def _():
                _carry_pass_on(gathers, cx, co, sems)

            @pl.when(i == ni - 1)
            def _():
                _carry_wait(gathers, cx, co, sems)

    row = pl.BlockSpec((TM, D), lambda i: (i, 0))
    one = pl.BlockSpec((1, D), lambda i: (0, 0))
    return pl.pallas_call(
        body,
        grid=(ni,),
        in_specs=[pl.BlockSpec((TM, k), lambda i: (i, 0)) for k in ks]
        + [pl.BlockSpec((sum(ks), D), lambda i: (0, 0)), row, pl.BlockSpec((1, 1, D), lambda i: (i // tps, 0, 0)), one, one]
        + c_in,
        out_specs=[row, row] + c_out,
        out_shape=[_sds((n, D), F32), _sds((n, D), F32)] + c_shapes,
        scratch_shapes=c_scratch,
        compiler_params=_params(("arbitrary",)),
        name=name,
    )(*parts, w, x, g, lng, lnb, *[a for a, _ in carry])


def postnorm_bwd(dxn, x, y, g, lng, lnb, w, ks, coef, seq, name, carry=()):
    n = x.shape[0]
    ni = n // TM
    tps = seq // TM
    nb = n // seq
    offs = [sum(ks[:k]) for k in range(len(ks))]
    npart = len(ks)
    nc = len(carry)
    gathers = [gt for _, gt in carry]
    c_in, c_out, c_shapes, c_scratch = _carry_plan(carry)

    def body(dxn_ref, x_ref, y_ref, g_ref, lng_ref, lnb_ref, w_ref, *rest):
        cx, rest = rest[:nc], rest[nc:]
        dx_ref, dy_ref = rest[:2]
        da_refs = rest[2 : 2 + npart]
        dg_ref, dlng_ref, dlnb_ref = rest[2 + npart : 5 + npart]
        co, sems = rest[5 + npart : 5 + npart + nc], rest[5 + npart + nc :]
        i = pl.program_id(0)
        if nc:
            @pl.when(i == 0)
            def _():
                _carry_start(gathers, cx, co, sems)

        @pl.when(i % tps == 0)
        def _():
            dg_ref[...] = jnp.zeros_like(dg_ref)

        @pl.when(i == 0)
        def _():
            dlng_ref[...] = jnp.zeros_like(dlng_ref)
            dlnb_ref[...] = jnp.zeros_like(dlnb_ref)

        for r0 in range(0, TM, SPLIT_ROWS):
            rows = slice(r0, r0 + SPLIT_ROWS)
            dx, dy, dg, dlng, dlnb = _postnorm_grads(x_ref[rows, :], y_ref[rows, :], g_ref[0], lng_ref[...], dxn_ref[rows, :], coef=coef)
            dx_ref[rows, :] = dx
            dyb = dy.astype(BF16)
            dy_ref[rows, :] = dyb
            for da_ref, off, k in zip(da_refs, offs, ks):
                da_ref[rows, :] = lax.dot_general(
                    dyb, w_ref[off : off + k, :], (((1,), (1,)), ((), ())), preferred_element_type=F32
                ).astype(BF16)
            dg_ref[0] += dg
            dlng_ref[...] += dlng
            dlnb_ref[...] += dlnb
        if nc:
            @pl.when(i == ni - 1)
            def _():
                _carry_pass_on(gathers, cx, co, sems)
                _carry_wait(gathers, cx, co, sems)

    row = pl.BlockSpec((TM, D), lambda i: (i, 0))
    one = pl.BlockSpec((1, D), lambda i: (0, 0))
    vec = pl.BlockSpec((1, 1, D), lambda i: (i // tps, 0, 0))
    return pl.pallas_call(
        body,
        grid=(ni,),
        in_specs=[row, row, row, vec, one, one, pl.BlockSpec((sum(ks), D), lambda i: (0, 0))] + c_in,
        out_specs=[row, row] + [pl.BlockSpec((TM, k), lambda i: (i, 0)) for k in ks] + [vec, one, one] + c_out,
        out_shape=[_sds((n, D), F32), _sds((n, D), BF16)]
        + [_sds((n, k), BF16) for k in ks]
        + [_sds((nb, 1, D), F32), _sds((1, D), F32), _sds((1, D), F32)]
        + c_shapes,
        scratch_shapes=c_scratch,
        compiler_params=_params(("arbitrary",)),
        name=name,
    )(dxn, x, y, g, lng, lnb, w, *[a for a, _ in carry])


def _mod_bwd_finish(dh, x_ref, sc_ref, dxres_ref, dx_ref, dsh_ref, dsc_ref, first_of_seq):
    dx_ref[...] = dxres_ref[...] + dh * (1.0 + sc_ref[0])

    @pl.when(first_of_seq)
    def _():
        dsh_ref[...] = jnp.zeros_like(dsh_ref)
        dsc_ref[...] = jnp.zeros_like(dsc_ref)

    dsh_ref[0] += jnp.sum(dh, axis=0, keepdims=True)
    dsc_ref[0] += jnp.sum(dh * x_ref[...], axis=0, keepdims=True)


def ffn_bwd_in(da, g, u, w, x, sc, dxres, seq, name, carry=()):
    n = x.shape[0]
    tn = FF // 2
    nj = FF // tn
    tm = TM // 2
    ni = n // tm
    tps = seq // tm
    nb = n // seq
    k = len(carry)
    gathers = [gt for _, gt in carry]
    c_in, c_out, c_shapes, c_scratch = _carry_plan(carry)

    def body(da_ref, g_ref, u_ref, w_hbm, x_ref, sc_ref, dxres_ref, *rest):
        cx, (dgu_ref, dx_ref, dsh_ref, dsc_ref), co, (w_ref, w_sem, *sems) = rest[:k], rest[k : k + 4], rest[k + 4 : 2 * k + 4], rest[2 * k + 4 :]
        i = pl.program_id(0)

        @pl.when(i == 0)
        def _():
            if k:
                _carry_start(gathers, cx, co, sems)
            cp = pltpu.make_async_copy(w_hbm, w_ref, w_sem)
            cp.start()
            cp.wait()

        nt = (((1,), (1,)), ((), ()))
        dh = None
        for j in range(nj):
            ln = slice(j * tn, (j + 1) * tn)
            gv = g_ref[:, ln].astype(F32)
            uv = u_ref[:, ln].astype(F32)
            dav = da_ref[:, ln].astype(F32)
            s = jax.nn.sigmoid(gv)
            dgv = (dav * uv * s * (1.0 + gv * (1.0 - s))).astype(BF16)
            duv = (dav * gv * s).astype(BF16)
            dgu_ref[:, ln] = dgv
            dgu_ref[:, FF + j * tn : FF + (j + 1) * tn] = duv
            t = lax.dot_general(dgv, w_ref[:, j * tn : (j + 1) * tn], nt, preferred_element_type=F32) + lax.dot_general(
                duv, w_ref[:, FF + j * tn : FF + (j + 1) * tn], nt, preferred_element_type=F32
            )
            dh = t if dh is None else dh + t
        _mod_bwd_finish(dh, x_ref, sc_ref, dxres_ref, dx_ref, dsh_ref, dsc_ref, i % tps == 0)
        if k:
            @pl.when(i == ni - 1)
            def _():
                _carry_wait(gathers, cx, co, sems)

    row = pl.BlockSpec((tm, D), lambda i: (i, 0))
    col = pl.BlockSpec((tm, FF), lambda i: (i, 0))
    vec = pl.BlockSpec((1, 1, D), lambda i: (i // tps, 0, 0))
    return pl.pallas_call(
        body,
        grid=(ni,),
        in_specs=[col, col, col, pl.BlockSpec(memory_space=pltpu.HBM), row, vec, row] + c_in,
        out_specs=[pl.BlockSpec((tm, 2 * FF), lambda i: (i, 0)), row, vec, vec] + c_out,
        out_shape=[_sds((n, 2 * FF), BF16), _sds((n, D), F32), _sds((nb, 1, D), F32), _sds((nb, 1, D), F32)] + c_shapes,
        scratch_shapes=[pltpu.VMEM((D, 2 * FF), BF16), pltpu.SemaphoreType.DMA] + c_scratch,
        compiler_params=_params(("arbitrary",)),
        name=name,
    )(da, g, u, w, x, sc, dxres, *[a for a, _ in carry])


def proj_bwd_in(dparts, w, x, sc, dxres, seq, name, carry=()):
    n = x.shape[0]
    ni = n // TM
    tps = seq // TM
    nb = n // seq
    ms = [p.shape[1] for p in dparts]
    offs = [sum(ms[:k]) for k in range(len(ms))]
    npart = len(ms)
    nc = len(carry)
    gathers = [gt for _, gt in carry]
    c_in, c_out, c_shapes, c_scratch = _carry_plan(carry)

    def body(*refs):
        d_refs = refs[:npart]
        w_ref, x_ref, sc_ref, dxres_ref = refs[npart : npart + 4]
        cx = refs[npart + 4 : npart + 4 + nc]
        dx_ref, dsh_ref, dsc_ref = refs[npart + 4 + nc : npart + 7 + nc]
        co, sems = refs[npart + 7 + nc : npart + 7 + 2 * nc], refs[npart + 7 + 2 * nc :]
        i = pl.program_id(0)
        if nc:
            @pl.when(i == 0)
            def _():
                _carry_start(gathers, cx, co, sems)

        dh = None
        for d_ref, off, m in zip(d_refs, offs, ms):
            t = lax.dot_general(d_ref[...], w_ref[:, off : off + m], (((1,), (1,)), ((), ())), preferred_element_type=F32)
            dh = t if dh is None else dh + t
        _mod_bwd_finish(dh, x_ref, sc_ref, dxres_ref, dx_ref, dsh_ref, dsc_ref, i % tps == 0)
        if nc:
            @pl.when(i == ni - 1)
            def _():
                _carry_pass_on(gathers, cx, co, sems)
                _carry_wait(gathers, cx, co, sems)

    row = pl.BlockSpec((TM, D), lambda i: (i, 0))
    vec = pl.BlockSpec((1, 1, D), lambda i: (i // tps, 0, 0))
    return pl.pallas_call(
        body,
        grid=(ni,),
        in_specs=[pl.BlockSpec((TM, m), lambda i: (i, 0)) for m in ms] + [pl.BlockSpec((D, sum(ms)), lambda i: (0, 0)), row, vec, row] + c_in,
        out_specs=[row, vec, vec] + c_out,
        out_shape=[_sds((n, D), F32), _sds((nb, 1, D), F32), _sds((nb, 1, D), F32)] + c_shapes,
        scratch_shapes=c_scratch,
        compiler_params=_params(("arbitrary",)),
        name=name,
    )(*dparts, w, x, sc, dxres, *[a for a, _ in carry])


def mm_tn(a, b, name, cut=0, carry=()):
    n, k1 = a.shape
    k2 = b.shape[1]
    t1 = k1 if k1 <= 1536 else _tile(k1, 1536)
    t2 = k2 if k2 <= 1536 else _tile(k2, 1536)
    tk = 2048 if n % 2048 == 0 else n
    nk = n // tk
    cw = k2 // cut if cut else t2
    per = t2 // cw
    assert per * cw == t2 and (not cut or t1 == k1)
    n1, n2 = k1 // t1, k2 // t2
    nc = len(carry)
    gathers = [gt for _, gt in carry]
    c_in, c_out, c_shapes, c_scratch = _carry_plan(carry)

    def body(a_ref, b_ref, *rest):
        cx, o_ref, co, (acc, *sems) = rest[:nc], rest[nc], rest[nc + 1 : 2 * nc + 1], rest[2 * nc + 1 :]
        if nc:
            @pl.when((pl.program_id(0) == 0) & (pl.program_id(1) == 0) & (pl.program_id(2) == 0))
            def _():
                _carry_start(gathers, cx, co, sems)

        t = lax.dot_general(a_ref[...], b_ref[...], (((0,), (0,)), ((), ())), preferred_element_type=F32)

        @pl.when(pl.program_id(2) == 0)
        def _():
            acc[...] = t

        @pl.when(pl.program_id(2) > 0)
        def _():
            acc[...] += t

        @pl.when(pl.program_id(2) == nk - 1)
        def _():
            if cut:
                for s in range(per):
                    o_ref[s] = acc[:, s * cw : (s + 1) * cw].astype(BF16)
            else:
                o_ref[...] = acc[...].astype(BF16)

        if nc:
            @pl.when((pl.program_id(0) == n1 - 1) & (pl.program_id(1) == n2 - 1) & (pl.program_id(2) == nk - 1))
            def _():
                _carry_pass_on(gathers, cx, co, sems)
                _carry_wait(gathers, cx, co, sems)

    o_spec = pl.BlockSpec((per, k1, cw), lambda i, j, k: (j, 0, 0)) if cut else pl.BlockSpec((t1, t2), lambda i, j, k: (i, j))
    outs = pl.pallas_call(
        body,
        grid=(n1, n2, nk),
        in_specs=[pl.BlockSpec((tk, t1), lambda i, j, k: (k, i)), pl.BlockSpec((tk, t2), lambda i, j, k: (k, j))] + c_in,
        out_specs=[o_spec] + c_out,
        out_shape=[_sds((cut, k1, cw), BF16) if cut else _sds((k1, k2), BF16)] + c_shapes,
        scratch_shapes=[pltpu.VMEM((t1, t2), F32)] + c_scratch,
        compiler_params=_params(("arbitrary", "arbitrary", "arbitrary")),
        name=name,
    )(a, b, *[x for x, _ in carry])
    return outs if nc else outs[0]


def _tile(n, cap):
    best = LANE
    for t in range(LANE, cap + 1, LANE):
        if n % t == 0:
            best = t
    return best


def _lane_tiles(ref):
    return [slice(c0, c0 + LANE) for c0 in range(0, ref.shape[1], LANE)]


def _conv_taps(ext_ref, w_ref, halo, tq, kk):
    out = []
    for ln in _lane_tiles(w_ref):
        acc = None
        for k in range(kk):
            t = w_ref[k : k + 1, ln] * ext_ref[pl.ds(halo - (kk - 1 - k), tq), ln]
            acc = t if acc is None else acc + t
        out.append(acc)
    return jnp.concatenate(out, axis=1)


def _conv_taps_t(ext2_ref, w_ref, tq, kk):
    out = []
    for ln in _lane_tiles(w_ref):
        acc = None
        for k in range(kk):
            t = w_ref[k : k + 1, ln] * ext2_ref[pl.ds(kk - 1 - k, tq), ln]
            acc = t if acc is None else acc + t
        out.append(acc)
    return jnp.concatenate(out, axis=1)


def _conv_dw(ext_ref, ext2_ref, dw_ref, halo, tq, kk):
    for ln in _lane_tiles(dw_ref):
        dy = ext2_ref[pl.ds(0, tq), ln]
        for k in range(kk):
            dw_ref[k : k + 1, ln] += jnp.sum(dy * ext_ref[pl.ds(halo - (kk - 1 - k), tq), ln], axis=0, keepdims=True)


def _halo_spec(rows, width, tq, shift):
    per = tq // rows

    if shift < 0:
        return lambda nblocks: pl.BlockSpec((rows, width), lambda i: (jnp.maximum(i * per - 1, 0), 0))
    return lambda nblocks: pl.BlockSpec((rows, width), lambda i: (jnp.minimum((i + 1) * per, nblocks - 1), 0))


def _ln(c, g, b):
    mu = jnp.mean(c, axis=-1, keepdims=True)
    cc = c - mu
    var = jnp.mean(cc * cc, axis=-1, keepdims=True)
    return cc * lax.rsqrt(var + EPS) * g + b


def _silu(v):
    return v * jax.nn.sigmoid(v)


@jax.custom_vjp
def _expand(v, e):
    hi = v.astype(BF16)
    r1 = v - hi.astype(F32)
    mid = r1.astype(BF16)
    lo = (r1 - mid.astype(F32)).astype(BF16)
    return (jnp.dot(hi, e, preferred_element_type=F32) + jnp.dot(mid, e, preferred_element_type=F32)
            + jnp.dot(lo, e, preferred_element_type=F32))


def _expand_fwd(v, e):
    return _expand(v, e), e


def _expand_bwd(e, g):
    nt = (((1,), (1,)), ((), ()))
    hi = g.astype(BF16)
    mid = (g - hi.astype(F32)).astype(BF16)
    dv = lax.dot_general(hi, e, nt, preferred_element_type=F32) + lax.dot_general(mid, e, nt, preferred_element_type=F32)
    return dv, jnp.zeros_like(e)


_expand.defvjp(_expand_fwd, _expand_bwd)


def _ssd_chunk(conv, dtr, z, s, dt_bias, a_log, dskip, norm_g, tril, e):
    q = SSD_Q
    act = _silu(conv)
    xs, bm, cm = act[:, :1024], act[:, 1024:1280], act[:, 1280:1536]
    lane = lax.broadcasted_iota(jnp.int32, (1, LANE), 1)
    lane_q = lax.broadcasted_iota(jnp.int32, (q, LANE), 1)
    sub_q = lax.broadcasted_iota(jnp.int32, (LANE, q), 0)
    causal = lax.broadcasted_iota(jnp.int32, (q, q), 0) >= lax.broadcasted_iota(jnp.int32, (q, q), 1)
    real = lane < SSD_HEADS
    dt = jnp.where(real, jax.nn.softplus(dtr + dt_bias), 0.0)
    a = jnp.where(real, -jnp.exp(a_log), 0.0)
    da = dt * a
    acs = jnp.dot(tril, da, precision=HI, preferred_element_type=F32)
    acs_t = lax.dot_general(da, tril, (((0,), (1,)), ((), ())), precision=HI, preferred_element_type=F32)
    dt_e = _expand(dt, e)
    acs_e = _expand(acs, e)
    one8 = jnp.ones((SUB, 1), F32)
    alast_e = _expand(one8 * jnp.sum(da, axis=0, keepdims=True), e)[0:1]
    d_e = _expand(one8 * jnp.where(real, dskip, 0.0), e)[0:1]
    xdt = xs * dt_e
    nt = (((1,), (1,)), ((), ()))
    tn = (((0,), (0,)), ((), ()))
    ys, snews = [], []
    for g in range(2):
        gl = slice(g * 512, (g + 1) * 512)
        bg = bm[:, g * 128 : (g + 1) * 128].astype(BF16)
        cg = cm[:, g * 128 : (g + 1) * 128].astype(BF16)
        cb = lax.dot_general(cg, bg, nt, preferred_element_type=F32)
        sg = s[:, gl]
        yoff = jnp.dot(cg, sg.astype(BF16), preferred_element_type=F32) * jnp.exp(acs_e[:, gl])
        pairs = []
        for j in range(4):
            xp = xdt[:, g * 512 + j * 128 : g * 512 + (j + 1) * 128].astype(BF16)
            outs = []
            for hh in (g * 8 + 2 * j, g * 8 + 2 * j + 1):
                col = jnp.sum(jnp.where(lane_q == hh, acs, 0.0), axis=1, keepdims=True)
                row = jnp.sum(jnp.where(sub_q == hh, acs_t, 0.0), axis=0, keepdims=True)
                m = cb * jnp.exp(jnp.where(causal, col - row, -1e30))
                outs.append(jnp.dot(m.astype(BF16), xp, preferred_element_type=F32))
            pairs.append(jnp.where(lane_q < SSD_P, outs[0], outs[1]))
        ys.append(jnp.concatenate(pairs, axis=1) + yoff)
        decay = jnp.exp(alast_e[:, gl] - acs_e[:, gl])
        snews.append(
            sg * jnp.exp(alast_e[:, gl]) + lax.dot_general(bg, (xdt[:, gl] * decay).astype(BF16), tn, preferred_element_type=F32)
        )
    y = jnp.concatenate(ys, axis=1) + xs * d_e
    gated = y * _silu(z)
    out = gated * lax.rsqrt(jnp.mean(gated * gated, axis=-1, keepdims=True) + EPS) * norm_g
    return out, jnp.concatenate(snews, axis=1)


def _ssd_consts():
    tril = (lax.broadcasted_iota(jnp.int32, (SSD_Q, SSD_Q), 0) >= lax.broadcasted_iota(jnp.int32, (SSD_Q, SSD_Q), 1)).astype(F32)
    e = (lax.broadcasted_iota(jnp.int32, (LANE, 1024), 0) == lax.broadcasted_iota(jnp.int32, (LANE, 1024), 1) // SSD_P).astype(BF16)
    return tril, e


def ssd_fwd(z, xbc, dtr, cw, cb, dt_bias, a_log, dskip, norm_g, seq, name):
    n = z.shape[0]
    q = SSD_Q
    nc = seq // q
    tril, e = _ssd_consts()

    def body(z_ref, xbc_ref, halo_ref, dtr_ref, cw_ref, cb_ref, dtb_ref, alog_ref, dsk_ref, ng_ref, tril_ref, e_ref, y_ref, sprev_ref, s_scr, ext):
        c = pl.program_id(0) % nc

        @pl.when(c == 0)
        def _():
            s_scr[...] = jnp.zeros_like(s_scr)

        ext[0:SUB, :] = jnp.where(c == 0, 0.0, halo_ref[...])
        ext[SUB:, :] = xbc_ref[...]
        conv = _conv_taps(ext, cw_ref, SUB, q, 4) + cb_ref[...]
        sprev_ref[0] = s_scr[...]
        y, snew = _ssd_chunk(conv, dtr_ref[...], z_ref[...], s_scr[...], dtb_ref[...], alog_ref[...], dsk_ref[...], ng_ref[...], tril_ref[...], e_ref[...])
        y_ref[...] = y.astype(BF16)
        s_scr[...] = snew

    def full(shape):
        return pl.BlockSpec(shape, lambda i: (0,) * len(shape))

    return pl.pallas_call(
        body,
        grid=(n // q,),
        in_specs=[
            pl.BlockSpec((q, 1024), lambda i: (i, 0)),
            pl.BlockSpec((q, SSD_XBC), lambda i: (i, 0)),
            _halo_spec(SUB, SSD_XBC, q, -1)(n // SUB),
            pl.BlockSpec((q, LANE), lambda i: (i, 0)),
            full((4, SSD_XBC)),
            full((1, SSD_XBC)),
            full((1, LANE)),
            full((1, LANE)),
            full((1, LANE)),
            full((1, 1024)),
            full((q, q)),
            full((LANE, 1024)),
        ],
        out_specs=[pl.BlockSpec((q, 1024), lambda i: (i, 0)), pl.BlockSpec((1, LANE, 1024), lambda i: (i, 0, 0))],
        out_shape=[_sds((n, 1024), BF16), _sds((n // q, LANE, 1024), F32)],
        scratch_shapes=[pltpu.VMEM((LANE, 1024), F32), pltpu.VMEM((SUB + q, SSD_XBC), F32)],
        compiler_params=_params(("arbitrary",)),
        name=name,
    )(z, xbc, xbc, dtr, cw, cb, dt_bias, a_log, dskip, norm_g, tril, e)


def ssd_bwd(dy, z, xbc, dtr, sprev, cw, cb, dt_bias, a_log, dskip, norm_g, seq, name):
    n = z.shape[0]
    q = SSD_Q
    nc = seq // q
    nchunks = n // q
    tril, e = _ssd_consts()

    def rev(i):
        return (i // nc) * nc + (nc - 1 - i % nc)

    def body(dy_ref, z_ref, xbc_ref, halo_ref, dtr_ref, sprev_ref, cw_ref, cb_ref, dtb_ref, alog_ref, dsk_ref, ng_ref, tril_ref, e_ref,
             dz_ref, dxbc_ref, ddt_ref, dcw_ref, dcb_ref, ddtb_ref, dalog_ref, ddsk_ref, dng_ref, ds_scr, ext, ext2):
        i = pl.program_id(0)
        step = i % nc
        c = nc - 1 - step

        @pl.when(step == 0)
        def _():
            ds_scr[...] = jnp.zeros_like(ds_scr)
            ext2[q:, :] = jnp.zeros((SUB, SSD_XBC), F32)

        @pl.when(i == 0)
        def _():
            for r in (dcw_ref, dcb_ref, ddtb_ref, dalog_ref, ddsk_ref, dng_ref):
                r[...] = jnp.zeros_like(r)

        ext[0:SUB, :] = jnp.where(c == 0, 0.0, halo_ref[...])
        ext[SUB:, :] = xbc_ref[...]
        conv = _conv_taps(ext, cw_ref, SUB, q, 4) + cb_ref[...]
        tril_v, e_v = tril_ref[...], e_ref[...]

        def f(conv, dtr, z, s, dtb, alog, dsk, ng):
            return _ssd_chunk(conv, dtr, z, s, dtb, alog, dsk, ng, tril_v, e_v)

        _, vjp = jax.vjp(f, conv, dtr_ref[...], z_ref[...], sprev_ref[0], dtb_ref[...], alog_ref[...], dsk_ref[...], ng_ref[...])
        dconv, ddtr, dz, dsprev, ddtb, dalog, ddsk, dng = vjp((dy_ref[...].astype(F32), ds_scr[...]))
        ds_scr[...] = dsprev
        dz_ref[...] = dz.astype(BF16)
        ddt_ref[...] = ddtr.astype(BF16)
        ext2[0:q, :] = dconv
        dxbc_ref[...] = _conv_taps_t(ext2, cw_ref, q, 4).astype(BF16)
        ext2[q:, :] = dconv[0:SUB, :]
        _conv_dw(ext, ext2, dcw_ref, SUB, q, 4)
        dcb_ref[...] += jnp.sum(dconv, axis=0, keepdims=True)
        ddtb_ref[...] += ddtb
        dalog_ref[...] += dalog
        ddsk_ref[...] += ddsk
        dng_ref[...] += dng

    def full(shape):
        return pl.BlockSpec(shape, lambda i: (0,) * len(shape))

    per = q // SUB
    return pl.pallas_call(
        body,
        grid=(nchunks,),
        in_specs=[
            pl.BlockSpec((q, 1024), lambda i: (rev(i), 0)),
            pl.BlockSpec((q, 1024), lambda i: (rev(i), 0)),
            pl.BlockSpec((q, SSD_XBC), lambda i: (rev(i), 0)),
            pl.BlockSpec((SUB, SSD_XBC), lambda i: (jnp.maximum(rev(i) * per - 1, 0), 0)),
            pl.BlockSpec((q, LANE), lambda i: (rev(i), 0)),
            pl.BlockSpec((1, LANE, 1024), lambda i: (rev(i), 0, 0)),
            full((4, SSD_XBC)),
            full((1, SSD_XBC)),
            full((1, LANE)),
            full((1, LANE)),
            full((1, LANE)),
            full((1, 1024)),
            full((q, q)),
            full((LANE, 1024)),
        ],
        out_specs=[
            pl.BlockSpec((q, 1024), lambda i: (rev(i), 0)),
            pl.BlockSpec((q, SSD_XBC), lambda i: (rev(i), 0)),
            pl.BlockSpec((q, LANE), lambda i: (rev(i), 0)),
            full((4, SSD_XBC)),
            full((1, SSD_XBC)),
            full((1, LANE)),
            full((1, LANE)),
            full((1, LANE)),
            full((1, 1024)),
        ],
        out_shape=[
            _sds((n, 1024), BF16),
            _sds((n, SSD_XBC), BF16),
            _sds((n, LANE), BF16),
            _sds((4, SSD_XBC), F32),
            _sds((1, SSD_XBC), F32),
            _sds((1, LANE), F32),
            _sds((1, LANE), F32),
            _sds((1, LANE), F32),
            _sds((1, 1024), F32),
        ],
        scratch_shapes=[pltpu.VMEM((LANE, 1024), F32), pltpu.VMEM((SUB + q, SSD_XBC), F32), pltpu.VMEM((q + SUB, SSD_XBC), F32)],
        compiler_params=_params(("arbitrary",)),
        name=name,
    )(dy, z, xbc, xbc, dtr, sprev, cw, cb, dt_bias, a_log, dskip, norm_g, tril, e)


POOL_HALO = 16
TQ = 512


def _pool_count(pos, w):
    return jnp.minimum(pos + 1.0, float(w))


def pool_fwd(u, pw, scale, seq, name):
    n = u.shape[0]
    tq, halo = TQ, POOL_HALO
    tps = seq // tq

    def body(u_ref, halo_ref, pw_ref, sc_ref, y_ref, ext):
        t0 = pl.program_id(0) % tps
        ext[0:halo, :] = jnp.where(t0 == 0, 0.0, halo_ref[...])
        ext[halo:, :] = u_ref[...]
        pos = (t0 * tq + lax.broadcasted_iota(jnp.int32, (tq, 1), 0)).astype(F32)
        for g, w in enumerate(POOL_WINDOWS):
            ln = slice(g * LANE, (g + 1) * LANE)
            acc = ext[pl.ds(halo, tq), ln]
            for j in range(1, w):
                acc = acc + ext[pl.ds(halo - j, tq), ln]
            pooled = acc / _pool_count(pos, w) - u_ref[:, ln]
            mixed = jnp.dot(pooled.astype(BF16), pw_ref[g].astype(BF16), preferred_element_type=F32)
            y_ref[:, ln] = (mixed * sc_ref[:, ln]).astype(BF16)

    return pl.pallas_call(
        body,
        grid=(n // tq,),
        in_specs=[
            pl.BlockSpec((tq, POOL_DIM), lambda i: (i, 0)),
            _halo_spec(halo, POOL_DIM, tq, -1)(n // halo),
            pl.BlockSpec((4, LANE, LANE), lambda i: (0, 0, 0)),
            pl.BlockSpec((1, POOL_DIM), lambda i: (0, 0)),
        ],
        out_specs=pl.BlockSpec((tq, POOL_DIM), lambda i: (i, 0)),
        out_shape=_sds((n, POOL_DIM), BF16),
        scratch_shapes=[pltpu.VMEM((halo + tq, POOL_DIM), F32)],
        compiler_params=_params(("arbitrary",)),
        name=name,
    )(u, u, pw, scale)


def pool_bwd(dy, u, pw, scale, seq, name):
    n = u.shape[0]
    tq, halo = TQ, POOL_HALO
    tps = seq // tq
    nt = (((1,), (1,)), ((), ()))
    tn = (((0,), (0,)), ((), ()))

    def body(dy_ref, dyn_ref, u_ref, halo_ref, pw_ref, sc_ref, du_ref, dpw_ref, dsc_ref, ext, ext2):
        i = pl.program_id(0)
        t0 = i % tps

        @pl.when(i == 0)
        def _():
            dpw_ref[...] = jnp.zeros_like(dpw_ref)
            dsc_ref[...] = jnp.zeros_like(dsc_ref)

        ext[0:halo, :] = jnp.where(t0 == 0, 0.0, halo_ref[...])
        ext[halo:, :] = u_ref[...]
        pos = (t0 * tq + lax.broadcasted_iota(jnp.int32, (tq, 1), 0)).astype(F32)
        dyv = dy_ref[...].astype(F32)
        dynv = jnp.where(t0 == tps - 1, 0.0, dyn_ref[...].astype(F32))
        for g, w in enumerate(POOL_WINDOWS):
            ln = slice(g * LANE, (g + 1) * LANE)
            wg = pw_ref[g].astype(BF16)
            acc = ext[pl.ds(halo, tq), ln]
            for j in range(1, w):
                acc = acc + ext[pl.ds(halo - j, tq), ln]
            pooled = (acc / _pool_count(pos, w) - u_ref[:, ln]).astype(BF16)
            mixed = jnp.dot(pooled, wg, preferred_element_type=F32)
            dsc_ref[:, ln] += jnp.sum(dyv[:, ln] * mixed, axis=0, keepdims=True)
            dmix = (dyv[:, ln] * sc_ref[:, ln]).astype(BF16)
            dpw_ref[g] += lax.dot_general(pooled, dmix, tn, preferred_element_type=F32)
            dpool = lax.dot_general(dmix, wg, nt, preferred_element_type=F32)
            dmix_n = (dynv[:, ln] * sc_ref[:, ln]).astype(BF16)
            dpool_n = lax.dot_general(dmix_n, wg, nt, preferred_element_type=F32)
            ext2[0:tq, ln] = dpool / _pool_count(pos, w)
            ext2[tq:, ln] = dpool_n * (1.0 / w)
            acc2 = ext2[pl.ds(0, tq), ln]
            for j in range(1, w):
                acc2 = acc2 + ext2[pl.ds(j, tq), ln]
            du_ref[:, ln] = (acc2 - dpool).astype(BF16)

    return pl.pallas_call(
        body,
        grid=(n // tq,),
        in_specs=[
            pl.BlockSpec((tq, POOL_DIM), lambda i: (i, 0)),
            _halo_spec(halo, POOL_DIM, tq, +1)(n // halo),
            pl.BlockSpec((tq, POOL_DIM), lambda i: (i, 0)),
            _halo_spec(halo, POOL_DIM, tq, -1)(n // halo),
            pl.BlockSpec((4, LANE, LANE), lambda i: (0, 0, 0)),
            pl.BlockSpec((1, POOL_DIM), lambda i: (0, 0)),
        ],
        out_specs=[
            pl.BlockSpec((tq, POOL_DIM), lambda i: (i, 0)),
            pl.BlockSpec((4, LANE, LANE), lambda i: (0, 0, 0)),
            pl.BlockSpec((1, POOL_DIM), lambda i: (0, 0)),
        ],
        out_shape=[_sds((n, POOL_DIM), BF16), _sds((4, LANE, LANE), F32), _sds((1, POOL_DIM), F32)],
        scratch_shapes=[pltpu.VMEM((halo + tq, POOL_DIM), F32), pltpu.VMEM((tq + halo, POOL_DIM), F32)],
        compiler_params=_params(("arbitrary",)),
        name=name,
    )(dy, dy, u, u, pw, scale)


CONF_HALO = 32
TQC = 256


def _conf_post(c, g, b):
    return _silu(_ln(c, g, b))


def conf_fwd(vg, w, b, lng, lnb, seq, name):
    n = vg.shape[0]
    tq, halo, kk = TQC, CONF_HALO, CONF_K
    tps = seq // tq
    c = CONF_DIM

    def body(vg_ref, halo_ref, w_ref, b_ref, lng_ref, lnb_ref, y_ref, conv_ref, ext):
        t0 = pl.program_id(0) % tps
        hv = halo_ref[...]
        ext[0:halo, :] = jnp.where(t0 == 0, 0.0, hv[:, :c] * jax.nn.sigmoid(hv[:, c:]))
        ext[halo:, :] = vg_ref[:, :c] * jax.nn.sigmoid(vg_ref[:, c:])
        conv = _conv_taps(ext, w_ref, halo, tq, kk) + b_ref[...]
        conv_ref[...] = conv
        y_ref[...] = _conf_post(conv, lng_ref[...], lnb_ref[...]).astype(BF16)

    one = pl.BlockSpec((1, c), lambda i: (0, 0))
    return pl.pallas_call(
        body,
        grid=(n // tq,),
        in_specs=[pl.BlockSpec((tq, 2 * c), lambda i: (i, 0)), _halo_spec(halo, 2 * c, tq, -1)(n // halo), pl.BlockSpec((kk, c), lambda i: (0, 0)), one, one, one],
        out_specs=[pl.BlockSpec((tq, c), lambda i: (i, 0)), pl.BlockSpec((tq, c), lambda i: (i, 0))],
        out_shape=[_sds((n, c), BF16), _sds((n, c), F32)],
        scratch_shapes=[pltpu.VMEM((halo + tq, c), F32)],
        compiler_params=_params(("arbitrary",)),
        name=name,
    )(vg, vg, w, b, lng, lnb)


def conf_bwd(dy, conv, vg, w, lng, lnb, seq, name):
    n = vg.shape[0]
    tq, halo, kk = TQC, CONF_HALO, CONF_K
    tps = seq // tq
    c = CONF_DIM

    def body(dy_ref, dyn_ref, conv_ref, convn_ref, vg_ref, halo_ref, w_ref, lng_ref, lnb_ref, dvg_ref, dw_ref, db_ref, dlng_ref, dlnb_ref, ext, ext2):
        i = pl.program_id(0)
        t0 = i % tps

        @pl.when(i == 0)
        def _():
            for r in (dw_ref, db_ref, dlng_ref, dlnb_ref):
                r[...] = jnp.zeros_like(r)

        _, vjp = jax.vjp(_conf_post, conv_ref[...], lng_ref[...], lnb_ref[...])
        dconv, dlng, dlnb = vjp(dy_ref[...].astype(F32))
        _, vjpn = jax.vjp(_conf_post, convn_ref[...], lng_ref[...], lnb_ref[...])
        dconv_n = vjpn(dyn_ref[...].astype(F32))[0]
        ext2[0:tq, :] = dconv
        ext2[tq:, :] = jnp.where(t0 == tps - 1, 0.0, dconv_n)
        dh = _conv_taps_t(ext2, w_ref, tq, kk)
        hv = halo_ref[...]
        ext[0:halo, :] = jnp.where(t0 == 0, 0.0, hv[:, :c] * jax.nn.sigmoid(hv[:, c:]))
        v = vg_ref[:, :c]
        s = jax.nn.sigmoid(vg_ref[:, c:])
        ext[halo:, :] = v * s
        _conv_dw(ext, ext2, dw_ref, halo, tq, kk)
        db_ref[...] += jnp.sum(dconv, axis=0, keepdims=True)
        dlng_ref[...] += dlng
        dlnb_ref[...] += dlnb
        dvg_ref[:, :c] = (dh * s).astype(BF16)
        dvg_ref[:, c:] = (dh * v * s * (1.0 - s)).astype(BF16)

    one = pl.BlockSpec((1, c), lambda i: (0, 0))
    tile = pl.BlockSpec((tq, c), lambda i: (i, 0))
    nxt = _halo_spec(halo, c, tq, +1)(n // halo)
    return pl.pallas_call(
        body,
        grid=(n // tq,),
        in_specs=[tile, nxt, tile, nxt, pl.BlockSpec((tq, 2 * c), lambda i: (i, 0)), _halo_spec(halo, 2 * c, tq, -1)(n // halo),
                  pl.BlockSpec((kk, c), lambda i: (0, 0)), one, one],
        out_specs=[pl.BlockSpec((tq, 2 * c), lambda i: (i, 0)), pl.BlockSpec((kk, c), lambda i: (0, 0)), one, one, one],
        out_shape=[_sds((n, 2 * c), BF16), _sds((kk, c), F32), _sds((1, c), F32), _sds((1, c), F32), _sds((1, c), F32)],
        scratch_shapes=[pltpu.VMEM((halo + tq, c), F32), pltpu.VMEM((tq + halo, c), F32)],
        compiler_params=_params(("arbitrary",)),
        name=name,
    )(dy, dy, conv, conv, vg, vg, w, lng, lnb)


TL = 256


def _expm1_neg(t):
    p = t * (1.0 + t * (1.0 / 2 + t * (1.0 / 6 + t * (1.0 / 24 + t * (1.0 / 120)))))
    return jnp.where(t > -0.1, p, jnp.exp(t) - 1.0)


def _lru_gate(xc, ra, ia, ba, bx, lam):
    r = jax.nn.sigmoid(ra + ba)
    i = jax.nn.sigmoid(ia + bx)
    log_a = -LRU_C * r * jax.nn.softplus(-lam)
    return jnp.exp(log_a), jnp.sqrt(-_expm1_neg(2.0 * log_a)) * (i * xc)


def _lru_out(h, gr):
    return h * jax.nn.gelu(gr)


def _scan_rows(a, b, tq, reverse):
    r8 = lax.broadcasted_iota(jnp.int32, (tq, 1), 0) % SUB
    for d in (1, 2, 4):
        sh = tq - d if reverse else d
        valid = (r8 < SUB - d) if reverse else (r8 >= d)
        a_s = pltpu.roll(a, sh, 0)
        b_s = pltpu.roll(b, sh, 0)
        b = jnp.where(valid, a * b_s, 0.0) + b
        a = jnp.where(valid, a * a_s, a)
    ng = tq // SUB
    edge = 0 if reverse else SUB - 1
    out_a, out_b = [None] * ng, [None] * ng
    ca = cb = None
    for g in (reversed(range(ng)) if reverse else range(ng)):
        ag, bg = a[g * SUB : (g + 1) * SUB, :], b[g * SUB : (g + 1) * SUB, :]
        if ca is not None:
            bg = bg + ag * cb
            ag = ag * ca
        out_a[g], out_b[g] = ag, bg
        ca, cb = ag[edge : edge + 1, :], bg[edge : edge + 1, :]
    return jnp.concatenate(out_a, axis=0), jnp.concatenate(out_b, axis=0)


def _row_of(v, r, tq):
    row = lax.broadcasted_iota(jnp.int32, (tq, 1), 0)
    return jnp.sum(jnp.where(row == r, v, 0.0), axis=0, keepdims=True)


def _head_mm(xc, w_ref):
    return jnp.concatenate(
        [
            jnp.dot(xc[:, h * LANE : (h + 1) * LANE].astype(BF16), w_ref[h].astype(BF16), preferred_element_type=F32)
            for h in range(LRU_HEADS)
        ],
        axis=1,
    )


def lru_fwd(xr, gr, cw, cb, wa, ba, wx, bx, lam, seq, name):
    n = xr.shape[0]
    tq = TL
    tps = seq // tq
    c = LRU_DIM

    def body(xr_ref, halo_ref, gr_ref, cw_ref, cb_ref, wa_ref, ba_ref, wx_ref, bx_ref, lam_ref, y_ref, h_ref, hc, ext):
        t0 = pl.program_id(0) % tps

        @pl.when(t0 == 0)
        def _():
            hc[...] = jnp.zeros_like(hc)

        ext[0:SUB, :] = jnp.where(t0 == 0, 0.0, halo_ref[...])
        ext[SUB:, :] = xr_ref[...]
        xc = _conv_taps(ext, cw_ref, SUB, tq, 4) + cb_ref[...]
        a, b = _lru_gate(xc, _head_mm(xc, wa_ref), _head_mm(xc, wx_ref), ba_ref[...], bx_ref[...], lam_ref[...])
        acum, h0 = _scan_rows(a, b, tq, False)
        h = h0 + acum * hc[0:1, :]
        h_ref[...] = h
        hc[0:1, :] = h_ref[tq - 1 : tq, :]
        y_ref[...] = _lru_out(h, gr_ref[...]).astype(BF16)

    one = pl.BlockSpec((1, c), lambda i: (0, 0))
    tile = pl.BlockSpec((tq, c), lambda i: (i, 0))
    hw = pl.BlockSpec((LRU_HEADS, LANE, LANE), lambda i: (0, 0, 0))
    return pl.pallas_call(
        body,
        grid=(n // tq,),
        in_specs=[tile, _halo_spec(SUB, c, tq, -1)(n // SUB), tile, pl.BlockSpec((4, c), lambda i: (0, 0)), one, hw, one, hw, one, one],
        out_specs=[tile, tile],
        out_shape=[_sds((n, c), BF16), _sds((n, c), F32)],
        scratch_shapes=[pltpu.VMEM((SUB, c), F32), pltpu.VMEM((SUB + tq, c), F32)],
        compiler_params=_params(("arbitrary",)),
        name=name,
    )(xr, xr, gr, cw, cb, wa, ba, wx, bx, lam)


def lru_bwd(dy, xr, gr, h, cw, cb, wa, ba, wx, bx, lam, seq, name):
    n = xr.shape[0]
    tq = TL
    tps = seq // tq
    ntile = n // tq
    c = LRU_DIM
    per = tq // SUB
    nt = (((1,), (1,)), ((), ()))
    tn = (((0,), (0,)), ((), ()))

    def rev(i):
        return (i // tps) * tps + (tps - 1 - i % tps)

    def body(dy_ref, xr_ref, halo_ref, gr_ref, h_ref, hprev_ref, cw_ref, cb_ref, wa_ref, ba_ref, wx_ref, bx_ref, lam_ref,
             dxr_ref, dgr_ref, dcw_ref, dcb_ref, dwa_ref, dba_ref, dwx_ref, dbx_ref, dlam_ref, carry, ext, ext2):
        i = pl.program_id(0)
        step = i % tps
        t0 = tps - 1 - step

        @pl.when(step == 0)
        def _():
            carry[...] = jnp.zeros_like(carry)
            ext2[tq:, :] = jnp.zeros((SUB, c), F32)

        @pl.when(i == 0)
        def _():
            for r in (dcw_ref, dcb_ref, dwa_ref, dba_ref, dwx_ref, dbx_ref, dlam_ref):
                r[...] = jnp.zeros_like(r)

        ext[0:SUB, :] = jnp.where(t0 == 0, 0.0, halo_ref[...])
        ext[SUB:, :] = xr_ref[...]
        xc = _conv_taps(ext, cw_ref, SUB, tq, 4) + cb_ref[...]
        (a, _), vjp_gate = jax.vjp(_lru_gate, xc, _head_mm(xc, wa_ref), _head_mm(xc, wx_ref), ba_ref[...], bx_ref[...], lam_ref[...])
        hv = h_ref[...]
        _, vjp_out = jax.vjp(_lru_out, hv, gr_ref[...])
        dh, dgr = vjp_out(dy_ref[...].astype(F32))
        dgr_ref[...] = dgr.astype(BF16)
        row = lax.broadcasted_iota(jnp.int32, (tq, 1), 0)
        a_up = jnp.where(row == tq - 1, carry[0:1, :], pltpu.roll(a, tq - 1, 0))
        acum, l0 = _scan_rows(a_up, dh, tq, True)
        lamv = l0 + acum * carry[1:2, :]
        carry[0:1, :] = _row_of(a, 0, tq)
        carry[1:2, :] = _row_of(lamv, 0, tq)
        hprev = jnp.where(row == 0, jnp.where(t0 == 0, 0.0, hprev_ref[SUB - 1 : SUB, :]), pltpu.roll(hv, 1, 0))
        dxc, dra, dia, dba, dbx, dlam = vjp_gate((lamv * hprev, lamv))
        dba_ref[...] += dba
        dbx_ref[...] += dbx
        dlam_ref[...] += dlam
        pieces = []
        for hh in range(LRU_HEADS):
            ln = slice(hh * LANE, (hh + 1) * LANE)
            xh = xc[:, ln].astype(BF16)
            drh = dra[:, ln].astype(BF16)
            dih = dia[:, ln].astype(BF16)
            dwa_ref[hh] += lax.dot_general(xh, drh, tn, preferred_element_type=F32)
            dwx_ref[hh] += lax.dot_general(xh, dih, tn, preferred_element_type=F32)
            pieces.append(
                lax.dot_general(drh, wa_ref[hh].astype(BF16), nt, preferred_element_type=F32)
                + lax.dot_general(dih, wx_ref[hh].astype(BF16), nt, preferred_element_type=F32)
            )
        dxc = dxc + jnp.concatenate(pieces, axis=1)
        ext2[0:tq, :] = dxc
        dxr_ref[...] = _conv_taps_t(ext2, cw_ref, tq, 4).astype(BF16)
        ext2[tq:, :] = ext2[0:SUB, :]
        _conv_dw(ext, ext2, dcw_ref, SUB, tq, 4)
        dcb_ref[...] += jnp.sum(dxc, axis=0, keepdims=True)

    one = pl.BlockSpec((1, c), lambda i: (0, 0))
    tile = pl.BlockSpec((tq, c), lambda i: (rev(i), 0))
    prev = pl.BlockSpec((SUB, c), lambda i: (jnp.maximum(rev(i) * per - 1, 0), 0))
    hw = pl.BlockSpec((LRU_HEADS, LANE, LANE), lambda i: (0, 0, 0))
    cw4 = pl.BlockSpec((4, c), lambda i: (0, 0))
    return pl.pallas_call(
        body,
        grid=(ntile,),
        in_specs=[tile, tile, prev, tile, tile, prev, cw4, one, hw, one, hw, one, one],
        out_specs=[tile, tile, cw4, one, hw, one, hw, one, one],
        out_shape=[_sds((n, c), BF16), _sds((n, c), BF16), _sds((4, c), F32), _sds((1, c), F32), _sds((LRU_HEADS, LANE, LANE), F32),
                   _sds((1, c), F32), _sds((LRU_HEADS, LANE, LANE), F32), _sds((1, c), F32), _sds((1, c), F32)],
        scratch_shapes=[pltpu.VMEM((SUB, c), F32), pltpu.VMEM((SUB + tq, c), F32), pltpu.VMEM((tq + SUB, c), F32)],
        compiler_params=_params(("arbitrary",)),
        name=name,
    )(dy, xr, xr, gr, h, h, cw, cb, wa, ba, wx, bx, lam)


def ada_fwd(c_all, w, b, name):
    nl, _, cols = w.shape
    nb = c_all.shape[0]

    def body(c_ref, w_ref, b_ref, o_ref):
        sc = _silu(c_ref[...]).astype(BF16)
        o_ref[0] = jnp.dot(sc, w_ref[0].astype(BF16), preferred_element_type=F32) + b_ref[0]

    return pl.pallas_call(
        body,
        grid=(nl,),
        in_specs=[pl.BlockSpec((nb, D), lambda l: (0, 0)), pl.BlockSpec((1, D, cols), lambda l: (l, 0, 0)), pl.BlockSpec((1, 1, cols), lambda l: (l, 0, 0))],
        out_specs=pl.BlockSpec((1, nb, cols), lambda l: (l, 0, 0)),
        out_shape=_sds((nl, nb, cols), F32),
        compiler_params=_params(("arbitrary",)),
        name=name,
    )(c_all, w, b)


def ada_bwd(c_all, dmod, name):
    nl, nb, cols = dmod.shape

    def body(c_ref, d_ref, o_ref):
        sc = _silu(c_ref[...]).astype(BF16)
        o_ref[0] = lax.dot_general(sc, d_ref[0].astype(BF16), (((0,), (0,)), ((), ())), preferred_element_type=F32)

    return pl.pallas_call(
        body,
        grid=(nl,),
        in_specs=[pl.BlockSpec((nb, D), lambda l: (0, 0)), pl.BlockSpec((1, nb, cols), lambda l: (l, 0, 0))],
        out_specs=pl.BlockSpec((1, D, cols), lambda l: (l, 0, 0)),
        out_shape=_sds((nl, D, cols), F32),
        compiler_params=_params(("arbitrary",)),
        name=name,
    )(c_all, dmod)


def loss_grad(y, target, name):
    n = y.shape[0]

    def body(y_ref, t_ref, dy_ref, l_ref, acc):
        i = pl.program_id(0)

        @pl.when(i == 0)
        def _():
            acc[...] = jnp.zeros_like(acc)

        e = y_ref[...] - t_ref[...]
        dy_ref[...] = e * (1.0 / D)
        acc[...] += jnp.sum(e * e, axis=0, keepdims=True)

        @pl.when(i == n // TM - 1)
        def _():
            l_ref[...] = jnp.full((1, LANE), 0.5 / D, F32) * jnp.sum(acc[...])

    row = pl.BlockSpec((TM, D), lambda i: (i, 0))
    return pl.pallas_call(
        body,
        grid=(n // TM,),
        in_specs=[row, row],
        out_specs=[row, pl.BlockSpec((1, LANE), lambda i: (0, 0))],
        out_shape=[_sds((n, D), F32), _sds((1, LANE), F32)],
        scratch_shapes=[pltpu.VMEM((1, D), F32)],
        compiler_params=_params(("arbitrary",)),
        name=name,
    )(y, target)


def sum_parts(parts, name):
    ns, r, _ = parts.shape
    tr = _row_tile(r, 1024)

    def body(p_ref, o_ref):
        acc = p_ref[0]
        for k in range(1, ns):
            acc = acc + p_ref[k]
        o_ref[...] = acc

    return pl.pallas_call(
        body,
        grid=(r // tr,),
        in_specs=[pl.BlockSpec((ns, tr, LANE), lambda i: (0, i, 0))],
        out_specs=pl.BlockSpec((tr, LANE), lambda i: (i, 0)),
        out_shape=_sds((r, LANE), F32),
        compiler_params=_params(("arbitrary",)),
        name=name,
    )(parts)


def _row_tile(r, cap):
    if r <= cap:
        return r
    best = None
    for t in range(16, cap + 1, 16):
        if r % t == 0:
            best = t
    assert best is not None, r
    return best


def adamw(w, m, v, gparts, name):
    ng, r, c = w.shape
    ns = gparts.shape[0]
    tr = _row_tile(r, min(512, 256 * 1024 // c))
    c1 = 1.0 - B1**STEP
    c2 = 1.0 - B2**STEP

    def body(w_ref, m_ref, v_ref, g_ref, go_ref, d_ref, mo_ref, vo_ref):
        g = g_ref[0, 0].astype(F32)
        for k in range(1, ns):
            g = g + g_ref[k, 0].astype(F32)
        mn = B1 * m_ref[0] + (1.0 - B1) * g
        vn = B2 * v_ref[0] + (1.0 - B2) * (g * g)
        go_ref[0] = g
        mo_ref[0] = mn
        vo_ref[0] = vn
        d_ref[0] = -LR * ((mn / c1) / (jnp.sqrt(vn / c2) + AEPS) + WD * w_ref[0])

    tile = pl.BlockSpec((1, tr, c), lambda b, i: (b, i, 0))
    return pl.pallas_call(
        body,
        grid=(ng, r // tr),
        in_specs=[tile, tile, tile, pl.BlockSpec((ns, 1, tr, c), lambda b, i: (0, b, i, 0))],
        out_specs=[tile, tile, tile, tile],
        out_shape=[_sds((ng, r, c), F32)] * 4,
        compiler_params=_params(("arbitrary", "arbitrary")),
        name=name,
    )(w, m, v, gparts)


WEIGHTS = ["ada_w", "ada_b", "ln_g", "ln_b", "ffn_w_in", "ffn_w_out", "ev_w_in", "ssd_conv_w", "ssd_conv_b", "ssd_dt_bias",
           "ssd_a_log", "ssd_d", "ssd_norm_g", "pool_w", "pool_scale", "ev_w_out", "od_w_in", "conf_dw_w", "conf_dw_b",
           "conf_ln_g", "conf_ln_b", "lru_conv_w", "lru_conv_b", "lru_wa", "lru_ba", "lru_wx", "lru_bx", "lru_lambda", "od_w_out"]
BIG = ("ada_w", "ffn_w_in", "ffn_w_out", "ev_w_in", "ev_w_out", "od_w_in", "od_w_out")
SMALL = {
    "ada_b": ((4, 9216), None), "ln_g": ((4, 3, 1024), 2), "ln_b": ((4, 3, 1024), 2),
    "ssd_conv_w": ((2, 4, 1536), 2), "ssd_conv_b": ((2, 1536), None), "ssd_dt_bias": ((2, 16), None),
    "ssd_a_log": ((2, 16), None), "ssd_d": ((2, 16), None), "ssd_norm_g": ((2, 1024), None),
    "pool_w": ((2, 4, 128, 128), None), "pool_scale": ((2, 512), None),
    "conf_dw_w": ((2, 31, 512), 2), "conf_dw_b": ((2, 512), 1), "conf_ln_g": ((2, 512), 1), "conf_ln_b": ((2, 512), 1),
    "lru_conv_w": ((2, 4, 1024), 2), "lru_conv_b": ((2, 1024), 1), "lru_wa": ((2, 8, 128, 128), None),
    "lru_ba": ((2, 1024), 1), "lru_wx": ((2, 8, 128, 128), None), "lru_bx": ((2, 1024), 1), "lru_lambda": ((2, 1024), 1),
}
PACK_ROWS = 2 * SUB * LANE


def _rows_of_piece(shape):
    return -(-math.prod(shape) // (SUB * LANE)) * SUB


def _pack(arrs, mult=PACK_ROWS):
    rows = [jnp.pad(a.reshape(-1), (0, _rows_of_piece(a.shape) * LANE - a.size)).reshape(-1, LANE) for a in arrs]
    buf = jnp.concatenate(rows, axis=0)
    pad = (-buf.shape[0]) % (mult // LANE)
    return jnp.pad(buf, ((0, pad), (0, 0)))


def _unpack(buf, shapes, lead=()):
    out, off = [], 0
    for s in shapes:
        k, nr = math.prod(s), _rows_of_piece(s)
        piece = buf[..., off : off + nr, :].reshape(lead + (nr * LANE,))
        out.append(piece[..., :k].reshape(lead + tuple(s)))
        off += nr
    return out


def _pad_lanes(v):
    return jnp.pad(v, (0, LANE - v.shape[0]))[None]


def kernel(x, c, ada_w, ada_b, ln_g, ln_b, ffn_w_in, ffn_w_out, ev_w_in, ssd_conv_w, ssd_conv_b, ssd_dt_bias, ssd_a_log, ssd_d, ssd_norm_g, pool_w, pool_scale, ev_w_out, od_w_in, conf_dw_w, conf_dw_b, conf_ln_g, conf_ln_b, lru_conv_w, lru_conv_b, lru_wa, lru_ba, lru_wx, lru_bx, lru_lambda, od_w_out, loss_target, m_ada_w, m_ada_b, m_ln_g, m_ln_b, m_ffn_w_in, m_ffn_w_out, m_ev_w_in, m_ssd_conv_w, m_ssd_conv_b, m_ssd_dt_bias, m_ssd_a_log, m_ssd_d, m_ssd_norm_g, m_pool_w, m_pool_scale, m_ev_w_out, m_od_w_in, m_conf_dw_w, m_conf_dw_b, m_conf_ln_g, m_conf_ln_b, m_lru_conv_w, m_lru_conv_b, m_lru_wa, m_lru_ba, m_lru_wx, m_lru_bx, m_lru_lambda, m_od_w_out, v_ada_w, v_ada_b, v_ln_g, v_ln_b, v_ffn_w_in, v_ffn_w_out, v_ev_w_in, v_ssd_conv_w, v_ssd_conv_b, v_ssd_dt_bias, v_ssd_a_log, v_ssd_d, v_ssd_norm_g, v_pool_w, v_pool_scale, v_ev_w_out, v_od_w_in, v_conf_dw_w, v_conf_dw_b, v_conf_ln_g, v_conf_ln_b, v_lru_conv_w, v_lru_conv_b, v_lru_wa, v_lru_ba, v_lru_wx, v_lru_bx, v_lru_lambda, v_od_w_out):
    p = dict(locals())
    nb, seq, _ = x.shape
    n = nb * seq
    me = 4 * lax.axis_index("x") + 2 * lax.axis_index("y") + lax.axis_index("c")
    sharded = [k for k, (_, ax) in SMALL.items() if ax is not None]

    def cols_of(g):
        return jnp.moveaxis(g, 0, 1).reshape(g.shape[1], N_DEV * g.shape[2])

    def rows_of(g):
        return g.reshape(N_DEV * g.shape[1], g.shape[2])

    def ev_in_of(g):
        w = cols_of(g)
        return jnp.concatenate([w[:, :2560], w[:, 2576:], jnp.pad(w[:, 2560:2576], ((0, 0), (0, LANE - SSD_HEADS)))], axis=1)

    sh_ffn_in, sh_ffn_out = ffn_w_in.astype(BF16), ffn_w_out.astype(BF16)
    sh_mix_in = [ev_w_in.astype(BF16), od_w_in.astype(BF16)]
    sh_mix_out = [ev_w_out.astype(BF16), od_w_out.astype(BF16)]

    def ffn_items(l, i):
        return [(sh_ffn_in[l, i], True), (sh_ffn_out[l, i], True)]

    def mix_items(l):
        return [(sh_mix_in[l % 2][l // 2], True), (sh_mix_out[l % 2][l // 2], True)]

    sm_local_shapes = [p[k].shape for k in sharded]
    g_in, sm_all = exchange([ffn_items(0, 0)[0], (_pack([p[k] for k in sharded] + [c]), True)], "ag_first")
    w_ffn = {(0, 0): (cols_of(g_in), None)}
    w_mix = {}
    got = _unpack(sm_all, sm_local_shapes + [c.shape], lead=(N_DEV,))
    full = {k: p[k] for k, (_, ax) in SMALL.items() if ax is None}
    for k, g in zip(sharded, got[:-1]):
        full[k] = jnp.moveaxis(g, 0, SMALL[k][1]).reshape(SMALL[k][0])
    c_all = got[-1].reshape(N_DEV * nb, D)

    cols = ada_w.shape[-1]
    ada_b_loc = lax.dynamic_slice_in_dim(ada_b, me * cols, cols, axis=1)[:, None, :]
    mod_cols = ada_fwd(c_all, ada_w, ada_b_loc, "ada_fwd")
    (mod_x,) = exchange([(mod_cols.reshape(DEPTH, N_DEV, nb, cols).transpose(1, 0, 2, 3), False)], "a2a_mod")
    mod = mod_x.transpose(1, 2, 0, 3).reshape(DEPTH, nb, N_MOD, 1, D)

    def vec(l, j):
        return mod[l, :, j]

    def row(a):
        return a[None]

    xs = x.reshape(n, D)
    saved = []
    for l in range(DEPTH):
        s = {"x0": xs}
        e = l // 2
        late = [] if l else [ffn_items(0, 0)[1]]
        s["h1"], s["g1"], s["u1"], s["a1"], g_in, g_out, *g_late = ffn_up(
            xs, vec(l, 0), vec(l, 1), w_ffn[l, 0][0], seq, "ffn_up_c%d" % (2 + len(late)), carry=ffn_items(l, 1) + late)
        w_ffn[l, 1] = (cols_of(g_in), rows_of(g_out))
        if late:
            w_ffn[0, 0] = (w_ffn[0, 0][0], rows_of(g_late[0]))
        x1, s["y1"], gm_in, gm_out = mm_postnorm([s["a1"]], w_ffn[l, 0][1], xs, vec(l, 2), row(full["ln_g"][l, 0]), row(full["ln_b"][l, 0]), 0.5, seq,
                                                 "ffn_down_c2", carry=mix_items(l))
        w_mix[l] = ((ev_in_of if l % 2 == 0 else cols_of)(gm_in), rows_of(gm_out))
        s["x1"] = x1
        if l % 2 == 0:
            s["h2"], s["z"], s["xbc"], s["u"], s["dtr"] = mod_mm(x1, vec(l, 3), vec(l, 4), w_mix[l][0], EV_SPLITS, seq, "ev_in")
            s["ya"], s["sprev"] = ssd_fwd(s["z"], s["xbc"], s["dtr"], full["ssd_conv_w"][e], row(full["ssd_conv_b"][e]), _pad_lanes(full["ssd_dt_bias"][e]),
                                          _pad_lanes(full["ssd_a_log"][e]), _pad_lanes(full["ssd_d"][e]), row(full["ssd_norm_g"][e]), seq, "ssd_fwd")
            s["yb"] = pool_fwd(s["u"], full["pool_w"][e], row(full["pool_scale"][e]), seq, "pool_fwd")
        else:
            s["h2"], s["vg"], s["xr"], s["gr"] = mod_mm(x1, vec(l, 3), vec(l, 4), w_mix[l][0], OD_SPLITS, seq, "od_in")
            s["ya"], s["conv"] = conf_fwd(s["vg"], full["conf_dw_w"][e], row(full["conf_dw_b"][e]), row(full["conf_ln_g"][e]), row(full["conf_ln_b"][e]), seq, "conf_fwd")
            s["yb"], s["hst"] = lru_fwd(s["xr"], s["gr"], full["lru_conv_w"][e], row(full["lru_conv_b"][e]), full["lru_wa"][e], row(full["lru_ba"][e]),
                                        full["lru_wx"][e], row(full["lru_bx"][e]), row(full["lru_lambda"][e]), seq, "lru_fwd")
        x2, s["y2"] = mm_postnorm([s["ya"], s["yb"]], w_mix[l][1], x1, vec(l, 5), row(full["ln_g"][l, 1]), row(full["ln_b"][l, 1]), 1.0, seq, "mix_out")
        s["x2"] = x2
        if l + 1 < DEPTH:
            s["h3"], s["g3"], s["u3"], s["a3"], g_in, g_out = ffn_up(x2, vec(l, 6), vec(l, 7), w_ffn[l, 1][0], seq, "ffn_up_c2", carry=ffn_items(l + 1, 0))
            w_ffn[l + 1, 0] = (cols_of(g_in), rows_of(g_out))
        else:
            s["h3"], s["g3"], s["u3"], s["a3"] = ffn_up(x2, vec(l, 6), vec(l, 7), w_ffn[l, 1][0], seq, "ffn_up")
        xs, s["y3"] = mm_postnorm([s["a3"]], w_ffn[l, 1][1], x2, vec(l, 8), row(full["ln_g"][l, 2]), row(full["ln_b"][l, 2]), 0.5, seq, "ffn_down")
        saved.append(s)

    dx, loss_row = loss_grad(xs, loss_target.reshape(n, D), "loss")
    loss = lax.psum(loss_row[0, 0], ("x", "y", "c"))

    sg = {k: [None] * shape[0] for k, (shape, _) in SMALL.items()}
    sg["ln_g"] = [[None] * 3 for _ in range(DEPTH)]
    sg["ln_b"] = [[None] * 3 for _ in range(DEPTH)]
    dmod = [[None] * N_MOD for _ in range(DEPTH)]
    pending, got_w = [], {}

    def cut_cols(g):
        r, cc = g.shape
        return g.reshape(r, N_DEV, cc // N_DEV).transpose(1, 0, 2)

    def cut_rows(g):
        r, cc = g.shape
        return g.reshape(N_DEV, r // N_DEV, cc)

    def take(room):
        sel = []
        for j, (_, a) in enumerate(pending):
            if a.size * a.dtype.itemsize <= room:
                sel.append(j)
                room -= a.size * a.dtype.itemsize
        items = [pending[j] for j in sel]
        pending[:] = [it for j, it in enumerate(pending) if j not in sel]
        return [k for k, _ in items], [(a, False) for _, a in items]

    room_down, room_up, room_mix = 8 * MIB, 12 * MIB, 6 * MIB

    def postnorm_backward(dxo, xin, y, g, lng, lnb, w, ks, coef, name, room):
        keys, carry = take(room)
        outs = postnorm_bwd(dxo, xin, y, g, lng, lnb, w, ks, coef, seq, name + "_c%d" % len(keys), carry=carry)
        got_w.update(zip(keys, outs[5 + len(ks):]))
        return outs[0], outs[1], outs[2 : 2 + len(ks)], outs[2 + len(ks)], outs[3 + len(ks)], outs[4 + len(ks)]

    def proj_backward(dparts, w, xin, scv, dxres, name):
        keys, carry = take(room_mix)
        outs = proj_bwd_in(dparts, w, xin, scv, dxres, seq, name + "_c%d" % len(keys), carry=carry)
        got_w.update(zip(keys, outs[3:]))
        return outs[:3]

    def ffn_backward(l, i, dxo, s, xin, hk, gk, uk, ak, yk, jbase, lnj):
        dxres, dy, (da,), dmod[l][jbase + 2], sg["ln_g"][l][lnj], sg["ln_b"][l][lnj] = postnorm_backward(
            dxo, xin, s[yk], vec(l, jbase + 2), row(full["ln_g"][l, lnj]), row(full["ln_b"][l, lnj]), w_ffn[l, i][1], [FF], 0.5, "ffn_down_bwd", room_down)
        keys, carry = take(room_up)
        outs = ffn_bwd_in(da, s[gk], s[uk], w_ffn[l, i][0], xin, vec(l, jbase + 1), dxres, seq, "ffn_up_bwd_c%d" % len(keys), carry=carry)
        dgu, dxi, dmod[l][jbase], dmod[l][jbase + 1] = outs[:4]
        got_w.update(zip(keys, outs[4:]))
        pending.append((("ffn_out", l, i), cut_rows(mm_tn(s[ak], dy, "wg_ffn_out"))))
        keys, carry = take(room_down)
        outs = mm_tn(s[hk], dgu, "wg_ffn_in_c%d" % len(keys), cut=N_DEV, carry=carry)
        got_w.update(zip(keys, outs[1:]) if keys else ())
        pending.append((("ffn_in", l, i), outs[0] if keys else outs))
        return dxi

    for l in reversed(range(DEPTH)):
        s = saved[l]
        e = l // 2
        dx = ffn_backward(l, 1, dx, s, s["x2"], "h3", "g3", "u3", "a3", "y3", 6, 2)
        ks = [1024, POOL_DIM] if l % 2 == 0 else [CONF_DIM, LRU_DIM]
        dxres, dy, (dya, dyb), dmod[l][5], sg["ln_g"][l][1], sg["ln_b"][l][1] = postnorm_backward(
            dx, s["x1"], s["y2"], vec(l, 5), row(full["ln_g"][l, 1]), row(full["ln_b"][l, 1]), w_mix[l][1], ks, 1.0, "mix_out_bwd", room_mix)
        pending.append((("mix_out", l), cut_rows(jnp.concatenate([mm_tn(s["ya"], dy, "wg_mix_a"), mm_tn(s["yb"], dy, "wg_mix_b")], axis=0))))
        if l % 2 == 0:
            (dz, dxbc, ddt, sg["ssd_conv_w"][e], dcb, ddtb, dalog, ddsk, dng) = ssd_bwd(
                dya, s["z"], s["xbc"], s["dtr"], s["sprev"], full["ssd_conv_w"][e], row(full["ssd_conv_b"][e]), _pad_lanes(full["ssd_dt_bias"][e]),
                _pad_lanes(full["ssd_a_log"][e]), _pad_lanes(full["ssd_d"][e]), row(full["ssd_norm_g"][e]), seq, "ssd_bwd")
            sg["ssd_conv_b"][e], sg["ssd_norm_g"][e] = dcb[0], dng[0]
            sg["ssd_dt_bias"][e], sg["ssd_a_log"][e], sg["ssd_d"][e] = ddtb[0, :SSD_HEADS], dalog[0, :SSD_HEADS], ddsk[0, :SSD_HEADS]
            du, sg["pool_w"][e], dps = pool_bwd(dyb, s["u"], full["pool_w"][e], row(full["pool_scale"][e]), seq, "pool_bwd")
            sg["pool_scale"][e] = dps[0]
            dparts = [dz, dxbc, du, ddt]
            dx, dmod[l][3], dmod[l][4] = proj_backward(dparts, w_mix[l][0], s["x1"], vec(l, 4), dxres, "ev_in_bwd")
            gz, gxbc, gu, gdt = [mm_tn(s["h2"], dp, "wg_ev_in") for dp in dparts]
            pending.append((("mix_in", l), cut_cols(jnp.concatenate([gz, gxbc, gdt[:, :SSD_HEADS], gu], axis=1))))
        else:
            dvg, sg["conf_dw_w"][e], dcb, dlg, dlb = conf_bwd(dya, s["conv"], s["vg"], full["conf_dw_w"][e], row(full["conf_ln_g"][e]), row(full["conf_ln_b"][e]), seq, "conf_bwd")
            sg["conf_dw_b"][e], sg["conf_ln_g"][e], sg["conf_ln_b"][e] = dcb[0], dlg[0], dlb[0]
            (dxr, dgr, sg["lru_conv_w"][e], dcb, sg["lru_wa"][e], dba, sg["lru_wx"][e], dbx, dlam) = lru_bwd(
                dyb, s["xr"], s["gr"], s["hst"], full["lru_conv_w"][e], row(full["lru_conv_b"][e]), full["lru_wa"][e], row(full["lru_ba"][e]),
                full["lru_wx"][e], row(full["lru_bx"][e]), row(full["lru_lambda"][e]), seq, "lru_bwd")
            sg["lru_conv_b"][e], sg["lru_ba"][e], sg["lru_bx"][e], sg["lru_lambda"][e] = dcb[0], dba[0], dbx[0], dlam[0]
            dparts = [dvg, dxr, dgr]
            dx, dmod[l][3], dmod[l][4] = proj_backward(dparts, w_mix[l][0], s["x1"], vec(l, 4), dxres, "od_in_bwd")
            pending.append((("mix_in", l), cut_cols(jnp.concatenate([mm_tn(s["h2"], dp, "wg_od_in") for dp in dparts], axis=1))))
        dx = ffn_backward(l, 0, dx, s, s["x0"], "h1", "g1", "u1", "a1", "y1", 0, 0)
    grad_x = dx.reshape(nb, seq, D)

    dmod_mine = jnp.stack([jnp.concatenate([d[:, 0, :] for d in dmod[l]], axis=-1) for l in range(DEPTH)])
    sg["ada_b"] = [jnp.sum(dmod_mine[l], axis=0) for l in range(DEPTH)]
    sg["ln_g"] = [jnp.concatenate(r, axis=0) for r in sg["ln_g"]]
    sg["ln_b"] = [jnp.concatenate(r, axis=0) for r in sg["ln_b"]]
    small_names = list(SMALL)
    sg_packed = _pack([jnp.stack(sg[k]).reshape(SMALL[k][0]) for k in small_names], N_DEV * PACK_ROWS)
    keys, carry = take(1 << 40)
    outs = exchange(carry + [(dmod_mine.reshape(DEPTH, nb, N_DEV, cols).transpose(2, 0, 1, 3), False),
                             (sg_packed.reshape(N_DEV, -1, LANE), False)], "x_last")
    got_w.update(zip(keys, outs))
    dmod_x, parts = outs[len(keys):]
    g_ada_w = ada_bwd(c_all, dmod_x.transpose(1, 0, 2, 3).reshape(DEPTH, N_DEV * nb, cols), "ada_bwd")

    (sg_sum,) = exchange([(sum_parts(parts, "sum_smallgrad"), True)], "ag_smallsum")
    summed = _unpack(sg_sum.reshape(-1, LANE), [SMALL[k][0] for k in small_names])
    grads = {}
    for k, g in zip(small_names, summed):
        ax = SMALL[k][1]
        grads[k] = g if ax is None else lax.dynamic_slice_in_dim(g, me * p[k].shape[ax], p[k].shape[ax], axis=ax)
    loc_shapes = [p[k].shape for k in small_names]
    whole = 512 * LANE
    _, d_s, m_s, v_s = adamw(_pack([p[k] for k in small_names], whole)[None], _pack([p["m_" + k] for k in small_names], whole)[None],
                             _pack([p["v_" + k] for k in small_names], whole)[None], _pack([grads[k] for k in small_names], whole)[None, None], "adamw_small")
    delta = dict(zip(small_names, _unpack(d_s[0], loc_shapes)))
    new_m = dict(zip(small_names, _unpack(m_s[0], loc_shapes)))
    new_v = dict(zip(small_names, _unpack(v_s[0], loc_shapes)))

    big_parts = {
        "ada_w": g_ada_w[None],
        "ffn_w_in": jnp.stack([jnp.stack([got_w["ffn_in", l, i] for i in range(2)], axis=1) for l in range(DEPTH)], axis=1),
        "ffn_w_out": jnp.stack([jnp.stack([got_w["ffn_out", l, i] for i in range(2)], axis=1) for l in range(DEPTH)], axis=1),
        "ev_w_in": jnp.stack([got_w["mix_in", l] for l in (0, 2)], axis=1),
        "ev_w_out": jnp.stack([got_w["mix_out", l] for l in (0, 2)], axis=1),
        "od_w_in": jnp.stack([got_w["mix_in", l] for l in (1, 3)], axis=1),
        "od_w_out": jnp.stack([got_w["mix_out", l] for l in (1, 3)], axis=1),
    }
    for k in BIG:
        w = p[k]
        r2 = (math.prod(w.shape[:-2]),) + w.shape[-2:]
        gp = big_parts[k]
        out = adamw(w.reshape(r2), p["m_" + k].reshape(r2), p["v_" + k].reshape(r2), gp.reshape((gp.shape[0],) + r2), "adamw_" + k)
        grads[k], delta[k], new_m[k], new_v[k] = [o.reshape(w.shape) for o in out]

    return (loss, grad_x, *[grads[k] for k in WEIGHTS], *[delta[k] for k in WEIGHTS], *[new_m[k] for k in WEIGHTS], *[new_v[k] for k in WEIGHTS])
```

```python
import math

import jax
import jax.numpy as jnp
from jax import lax
from jax.experimental import pallas as pl
from jax.experimental.pallas import tpu as pltpu

F32 = jnp.float32
BF16 = jnp.bfloat16
HI = lax.Precision.HIGHEST

N_DEV = 8
D = 1024
DEPTH = 4
N_MOD = 9
FF = 2816
ALPHA = (2.0 * DEPTH) ** 0.25
EPS = 1e-5
SSD_Q = 128
SSD_HEADS = 16
SSD_P = 64
SSD_N = 128
SSD_XBC = 1536
POOL_WINDOWS = (2, 4, 8, 16)
POOL_DIM = 512
CONF_DIM = 512
CONF_K = 31
LRU_DIM = 1024
LRU_HEADS = 8
LRU_C = 8.0
EV_SPLITS = (1024, 1536, 512, 128)
OD_SPLITS = (1024, 1024, 1024)
LR, B1, B2, AEPS, WD, STEP = 0.001, 0.9, 0.999, 1e-08, 0.01, 10

LANE = 128
SUB = 8
MIB = 1024 * 1024
VMEM_LIMIT = 48 * MIB
TM = 512
DEEP = 3
SPLIT_ROWS = 256


def _params(sem, vmem=VMEM_LIMIT):
    return pltpu.CompilerParams(dimension_semantics=sem, vmem_limit_bytes=vmem)


def _sds(shape, dtype):
    return jax.ShapeDtypeStruct(shape, dtype)


def _modulate(x, sh, sc):
    return x * (1.0 + sc) + sh


def _postnorm(x, y, g, lng, lnb, *, coef):
    z = ALPHA * x + coef * (1.0 + g) * y
    mu = jnp.mean(z, axis=-1, keepdims=True)
    zc = z - mu
    var = jnp.mean(zc * zc, axis=-1, keepdims=True)
    return zc * lax.rsqrt(var + EPS) * lng + lnb


def _postnorm_grads(x, y, g, lng, dxn, *, coef):
    kk = coef * (1.0 + g)
    z = ALPHA * x + kk * y
    zc = z - jnp.mean(z, axis=-1, keepdims=True)
    r = lax.rsqrt(jnp.mean(zc * zc, axis=-1, keepdims=True) + EPS)
    xhat = zc * r
    dxhat = dxn * lng
    dz = r * (dxhat - jnp.mean(dxhat, axis=-1, keepdims=True) - xhat * jnp.mean(dxhat * xhat, axis=-1, keepdims=True))
    rows = lambda v: jnp.sum(v, axis=0, keepdims=True)
    return ALPHA * dz, kk * dz, coef * rows(y * dz), rows(dxn * xhat), rows(dxn)


def _place():
    mx, my, mc = lax.axis_index("x"), lax.axis_index("y"), lax.axis_index("c")

    def at(r):
        px = 1 - mx if r & 4 else mx
        py = 1 - my if r & 2 else my
        pc = 1 - mc if r & 1 else mc
        return (px, py, pc), 4 * px + 2 * py + pc

    return 4 * mx + 2 * my + mc, at


def _carry_plan(items):
    hbm = pl.BlockSpec(memory_space=pltpu.HBM)
    k = len(items)
    shapes = [_sds((N_DEV,) + a.shape if g else a.shape, a.dtype) for a, g in items]
    scratch = [pltpu.SemaphoreType.DMA((k * (N_DEV - 1),)), pltpu.SemaphoreType.DMA((k * (N_DEV - 1),)), pltpu.SemaphoreType.DMA((k,))] if k else []
    return [hbm] * k, [hbm] * k, shapes, scratch


def _remote(src, dst, sems, s, pos):
    return pltpu.make_async_remote_copy(src_ref=src, dst_ref=dst, send_sem=sems[0].at[s], recv_sem=sems[1].at[s],
                                        device_id=pos, device_id_type=pl.DeviceIdType.MESH)


def _carry_start(gathers, x_refs, o_refs, sems):
    me, at = _place()
    for a, (gather, x_ref, o_ref) in enumerate(zip(gathers, x_refs, o_refs)):
        base = a * (N_DEV - 1)
        pltpu.make_async_copy(x_ref if gather else x_ref.at[me], o_ref.at[me], sems[2].at[a]).start()
        if gather:
            for s, r in enumerate((1, 4, 2, 6)):
                _remote(x_ref, o_ref.at[me], sems, base + s, at(r)[0]).start()
        else:
            for r in range(1, N_DEV):
                pos, pid = at(r)
                _remote(x_ref.at[pid], o_ref.at[me], sems, base + r - 1, pos).start()


def _carry_pass_on(gathers, x_refs, o_refs, sems):
    _, at = _place()
    sibling = at(1)[0]
    for a, (gather, x_ref, o_ref) in enumerate(zip(gathers, x_refs, o_refs)):
        if gather:
            base = a * (N_DEV - 1)
            for j, r in enumerate((4, 2, 6)):
                pos, pid = at(r)
                _remote(x_ref, o_ref.at[pid], sems, base + 1 + j, pos).wait_recv()
                _remote(o_ref.at[pid], o_ref.at[pid], sems, base + 4 + j, sibling).start()


def _carry_wait(gathers, x_refs, o_refs, sems):
    me, at = _place()
    for a, (gather, x_ref, o_ref) in enumerate(zip(gathers, x_refs, o_refs)):
        base = a * (N_DEV - 1)
        if gather:
            sib_pos, sib_id = at(1)
            _remote(x_ref, o_ref.at[sib_id], sems, base, sib_pos).wait_recv()
            for j, r in enumerate((4, 2, 6)):
                _remote(x_ref, o_ref.at[at(r | 1)[1]], sems, base + 4 + j, sib_pos).wait_recv()
            for s in range(N_DEV - 1):
                _remote(x_ref, o_ref.at[me], sems, base + s, sib_pos).wait_send()
            pltpu.make_async_copy(x_ref, o_ref.at[me], sems[2].at[a]).wait()
        else:
            for r in range(1, N_DEV):
                pos, pid = at(r)
                _remote(x_ref.at[pid], o_ref.at[pid], sems, base + r - 1, pos).wait_recv()
            for r in range(1, N_DEV):
                pos, pid = at(r)
                _remote(x_ref.at[pid], o_ref.at[me], sems, base + r - 1, pos).wait_send()
            pltpu.make_async_copy(x_ref.at[me], o_ref.at[me], sems[2].at[a]).wait()


def exchange(items, name):
    gathers = [g for _, g in items]
    k = len(items)
    in_specs, out_specs, shapes, scratch = _carry_plan(items)

    def body(*refs):
        x_refs, o_refs, sems = refs[:k], refs[k : 2 * k], refs[2 * k :]
        _carry_start(gathers, x_refs, o_refs, sems)
        _carry_pass_on(gathers, x_refs, o_refs, sems)
        _carry_wait(gathers, x_refs, o_refs, sems)

    return pl.pallas_call(
        body,
        in_specs=in_specs,
        out_specs=out_specs,
        out_shape=shapes,
        scratch_shapes=scratch,
        compiler_params=pltpu.CompilerParams(has_side_effects=True),
        name=name,
    )(*[a for a, _ in items])


def ffn_up(x, sh, sc, w, seq, name, carry=()):
    n = x.shape[0]
    tn = FF // 2
    nj = FF // tn
    ni = n // TM
    tps = seq // TM
    k = len(carry)
    gathers = [g for _, g in carry]
    c_in, c_out, c_shapes, c_scratch = _carry_plan(carry)

    def body(x_ref, sh_ref, sc_ref, w_hbm, *rest):
        cx, (h_ref, g_ref, u_ref, a_ref), co, (w_ref, w_sem, *sems) = rest[:k], rest[k : k + 4], rest[k + 4 : 2 * k + 4], rest[2 * k + 4 :]
        i = pl.program_id(0)

        @pl.when(i == 0)
        def _():
            if k:
                _carry_start(gathers, cx, co, sems)
            cp = pltpu.make_async_copy(w_hbm, w_ref, w_sem)
            cp.start()
            cp.wait()

        h = _modulate(x_ref[...], sh_ref[0], sc_ref[0]).astype(BF16)
        h_ref[...] = h
        for j in range(nj):
            g = jnp.dot(h, w_ref[:, j * tn : (j + 1) * tn], preferred_element_type=F32)
            u = jnp.dot(h, w_ref[:, FF + j * tn : FF + (j + 1) * tn], preferred_element_type=F32)
            g_ref[:, j * tn : (j + 1) * tn] = g.astype(BF16)
            u_ref[:, j * tn : (j + 1) * tn] = u.astype(BF16)
            a_ref[:, j * tn : (j + 1) * tn] = (g * jax.nn.sigmoid(g) * u).astype(BF16)
        if k:
            @pl.when(i == max(ni - 2, 0))
            def _():
                _carry_pass_on(gathers, cx, co, sems)

            @pl.when(i == ni - 1)
            def _():
                _carry_wait(gathers, cx, co, sems)

    vec = pl.BlockSpec((1, 1, D), lambda i: (i // tps, 0, 0))
    col = pl.BlockSpec((TM, FF), lambda i: (i, 0))
    return pl.pallas_call(
        body,
        grid=(ni,),
        in_specs=[pl.BlockSpec((TM, D), lambda i: (i, 0)), vec, vec, pl.BlockSpec(memory_space=pltpu.HBM)] + c_in,
        out_specs=[pl.BlockSpec((TM, D), lambda i: (i, 0)), col, col, col] + c_out,
        out_shape=[_sds((n, D), BF16), _sds((n, FF), BF16), _sds((n, FF), BF16), _sds((n, FF), BF16)] + c_shapes,
        scratch_shapes=[pltpu.VMEM((D, 2 * FF), BF16), pltpu.SemaphoreType.DMA] + c_scratch,
        compiler_params=_params(("arbitrary",)),
        name=name,
    )(x, sh, sc, w, *[a for a, _ in carry])


def mod_mm(x, sh, sc, w, splits, seq, name):
    n = x.shape[0]
    m = w.shape[1]
    tps = seq // TM
    offs = [sum(splits[:k]) for k in range(len(splits))]

    def body(x_ref, sh_ref, sc_ref, w_ref, h_ref, *outs):
        h = _modulate(x_ref[...], sh_ref[0], sc_ref[0]).astype(BF16)
        h_ref[...] = h
        for o_ref, off, wd in zip(outs, offs, splits):
            o_ref[...] = jnp.dot(h, w_ref[:, off : off + wd], preferred_element_type=F32)

    vec = pl.BlockSpec((1, 1, D), lambda i: (i // tps, 0, 0))
    return pl.pallas_call(
        body,
        grid=(n // TM,),
        in_specs=[pl.BlockSpec((TM, D), lambda i: (i, 0)), vec, vec, pl.BlockSpec((D, m), lambda i: (0, 0))],
        out_specs=[pl.BlockSpec((TM, D), lambda i: (i, 0))] + [pl.BlockSpec((TM, wd), lambda i: (i, 0)) for wd in splits],
        out_shape=[_sds((n, D), BF16)] + [_sds((n, wd), F32) for wd in splits],
        compiler_params=_params(("arbitrary",)),
        name=name,
    )(x, sh, sc, w)


def mm_postnorm(parts, w, x, g, lng, lnb, coef, seq, name, carry=()):
    n = x.shape[0]
    ni = n // TM
    tps = seq // TM
    ks = [p.shape[1] for p in parts]
    offs = [sum(ks[:k]) for k in range(len(ks))]
    npart = len(parts)
    nc = len(carry)
    gathers = [gt for _, gt in carry]
    c_in, c_out, c_shapes, c_scratch = _carry_plan(carry)

    def body(*refs):
        a_refs = refs[:npart]
        w_ref, x_ref, g_ref, lng_ref, lnb_ref = refs[npart : npart + 5]
        cx = refs[npart + 5 : npart + 5 + nc]
        xn_ref, y_ref = refs[npart + 5 + nc : npart + 7 + nc]
        co, sems = refs[npart + 7 + nc : npart + 7 + 2 * nc], refs[npart + 7 + 2 * nc :]
        i = pl.program_id(0)
        if nc:
            @pl.when(i == 0)
            def _():
                _carry_start(gathers, cx, co, sems)

        for r0 in range(0, TM, SPLIT_ROWS):
            rows = slice(r0, r0 + SPLIT_ROWS)
            y = None
            for a_ref, off, k in zip(a_refs, offs, ks):
                t = jnp.dot(a_ref[rows, :], w_ref[off : off + k, :], preferred_element_type=F32)
                y = t if y is None else y + t
            y_ref[rows, :] = y
            xn_ref[rows, :] = _postnorm(x_ref[rows, :], y, g_ref[0], lng_ref[...], lnb_ref[...], coef=coef)
        if nc:
            @pl.when(i == max(ni - 2, 0))
            def _():
                _carry_pass_on(gathers, cx, co, sems)

            @pl.when(i == ni - 1)
            def _():
                _carry_wait(gathers, cx, co, sems)

    row = pl.BlockSpec((TM, D), lambda i: (i, 0))
    one = pl.BlockSpec((1, D), lambda i: (0, 0))
    return pl.pallas_call(
        body,
        grid=(ni,),
        in_specs=[pl.BlockSpec((TM, k), lambda i: (i, 0)) for k in ks]
        + [pl.BlockSpec((sum(ks), D), lambda i: (0, 0)), row, pl.BlockSpec((1, 1, D), lambda i: (i // tps, 0, 0)), one, one]
        + c_in,
        out_specs=[row, row] + c_out,
        out_shape=[_sds((n, D), F32), _sds((n, D), F32)] + c_shapes,
        scratch_shapes=c_scratch,
        compiler_params=_params(("arbitrary",)),
        name=name,
    )(*parts, w, x, g, lng, lnb, *[a for a, _ in carry])


def postnorm_bwd(dxn, x, y, g, lng, lnb, w, ks, coef, seq, name, carry=()):
    n = x.shape[0]
    ni = n // TM
    tps = seq // TM
    nb = n // seq
    offs = [sum(ks[:k]) for k in range(len(ks))]
    npart = len(ks)
    nc = len(carry)
    gathers = [gt for _, gt in carry]
    c_in, c_out, c_shapes, c_scratch = _carry_plan(carry)

    def body(dxn_ref, x_ref, y_ref, g_ref, lng_ref, lnb_ref, w_ref, *rest):
        cx, rest = rest[:nc], rest[nc:]
        dx_ref, dy_ref = rest[:2]
        da_refs = rest[2 : 2 + npart]
        dg_ref, dlng_ref, dlnb_ref = rest[2 + npart : 5 + npart]
        co, sems = rest[5 + npart : 5 + npart + nc], rest[5 + npart + nc :]
        i = pl.program_id(0)
        if nc:
            @pl.when(i == 0)
            def _():
                _carry_start(gathers, cx, co, sems)

        @pl.when(i % tps == 0)
        def _():
            dg_ref[...] = jnp.zeros_like(dg_ref)

        @pl.when(i == 0)
        def _():
            dlng_ref[...] = jnp.zeros_like(dlng_ref)
            dlnb_ref[...] = jnp.zeros_like(dlnb_ref)

        for r0 in range(0, TM, SPLIT_ROWS):
            rows = slice(r0, r0 + SPLIT_ROWS)
            dx, dy, dg, dlng, dlnb = _postnorm_grads(x_ref[rows, :], y_ref[rows, :], g_ref[0], lng_ref[...], dxn_ref[rows, :], coef=coef)
            dx_ref[rows, :] = dx
            dyb = dy.astype(BF16)
            dy_ref[rows, :] = dyb
            for da_ref, off, k in zip(da_refs, offs, ks):
                da_ref[rows, :] = lax.dot_general(
                    dyb, w_ref[off : off + k, :], (((1,), (1,)), ((), ())), preferred_element_type=F32
                ).astype(BF16)
            dg_ref[0] += dg
            dlng_ref[...] += dlng
            dlnb_ref[...] += dlnb
        if nc:
            @pl.when(i == ni - 1)
            def _():
                _carry_pass_on(gathers, cx, co, sems)
                _carry_wait(gathers, cx, co, sems)

    row = pl.BlockSpec((TM, D), lambda i: (i, 0))
    one = pl.BlockSpec((1, D), lambda i: (0, 0))
    vec = pl.BlockSpec((1, 1, D), lambda i: (i // tps, 0, 0))
    return pl.pallas_call(
        body,
        grid=(ni,),
        in_specs=[row, row, row, vec, one, one, pl.BlockSpec((sum(ks), D), lambda i: (0, 0))] + c_in,
        out_specs=[row, row] + [pl.BlockSpec((TM, k), lambda i: (i, 0)) for k in ks] + [vec, one, one] + c_out,
        out_shape=[_sds((n, D), F32), _sds((n, D), BF16)]
        + [_sds((n, k), BF16) for k in ks]
        + [_sds((nb, 1, D), F32), _sds((1, D), F32), _sds((1, D), F32)]
        + c_shapes,
        scratch_shapes=c_scratch,
        compiler_params=_params(("arbitrary",)),
        name=name,
    )(dxn, x, y, g, lng, lnb, w, *[a for a, _ in carry])


def _mod_bwd_finish(dh, x_ref, sc_ref, dxres_ref, dx_ref, dsh_ref, dsc_ref, first_of_seq):
    dx_ref[...] = dxres_ref[...] + dh * (1.0 + sc_ref[0])

    @pl.when(first_of_seq)
    def _():
        dsh_ref[...] = jnp.zeros_like(dsh_ref)
        dsc_ref[...] = jnp.zeros_like(dsc_ref)

    dsh_ref[0] += jnp.sum(dh, axis=0, keepdims=True)
    dsc_ref[0] += jnp.sum(dh * x_ref[...], axis=0, keepdims=True)


def ffn_bwd_in(da, g, u, w, x, sc, dxres, seq, name, carry=()):
    n = x.shape[0]
    tn = FF // 2
    nj = FF // tn
    tm = TM // 2
    ni = n // tm
    tps = seq // tm
    nb = n // seq
    k = len(carry)
    gathers = [gt for _, gt in carry]
    c_in, c_out, c_shapes, c_scratch = _carry_plan(carry)

    def body(da_hbm, g_hbm, u_hbm, w_hbm, x_ref, sc_ref, dxres_ref, *rest):
        cx, (dgu_ref, dx_ref, dsh_ref, dsc_ref), co = rest[:k], rest[k : k + 4], rest[k + 4 : 2 * k + 4]
        w_ref, w_sem, da_buf, g_buf, u_buf, in_sems, *sems = rest[2 * k + 4 :]
        i = pl.program_id(0)

        def fetch(step, slot):
            rows = pl.ds(pl.multiple_of(step * tm, tm), tm)
            return [pltpu.make_async_copy(src.at[rows, :], buf.at[slot], in_sems.at[q, slot])
                    for q, (src, buf) in enumerate(((da_hbm, da_buf), (g_hbm, g_buf), (u_hbm, u_buf)))]

        @pl.when(i == 0)
        def _():
            if k:
                _carry_start(gathers, cx, co, sems)
            for ahead in range(DEEP - 1):
                for cp in fetch(ahead, ahead):
                    cp.start()
            cp = pltpu.make_async_copy(w_hbm, w_ref, w_sem)
            cp.start()
            cp.wait()

        @pl.when(i + DEEP - 1 < ni)
        def _():
            for cp in fetch(i + DEEP - 1, (i + DEEP - 1) % DEEP):
                cp.start()

        slot = i % DEEP
        for cp in fetch(i, slot):
            cp.wait()
        nt = (((1,), (1,)), ((), ()))
        dh = None
        for j in range(nj):
            ln = slice(j * tn, (j + 1) * tn)
            gv = g_buf[slot, :, ln].astype(F32)
            uv = u_buf[slot, :, ln].astype(F32)
            dav = da_buf[slot, :, ln].astype(F32)
            s = jax.nn.sigmoid(gv)
            dgv = (dav * uv * s * (1.0 + gv * (1.0 - s))).astype(BF16)
            duv = (dav * gv * s).astype(BF16)
            dgu_ref[:, ln] = dgv
            dgu_ref[:, FF + j * tn : FF + (j + 1) * tn] = duv
            t = lax.dot_general(dgv, w_ref[:, j * tn : (j + 1) * tn], nt, preferred_element_type=F32) + lax.dot_general(
                duv, w_ref[:, FF + j * tn : FF + (j + 1) * tn], nt, preferred_element_type=F32
            )
            dh = t if dh is None else dh + t
        _mod_bwd_finish(dh, x_ref, sc_ref, dxres_ref, dx_ref, dsh_ref, dsc_ref, i % tps == 0)
        if k:
            @pl.when(i == ni - 1)
            def _():
                _carry_wait(gathers, cx, co, sems)

    row = pl.BlockSpec((tm, D), lambda i: (i, 0))
    hbm = pl.BlockSpec(memory_space=pltpu.HBM)
    vec = pl.BlockSpec((1, 1, D), lambda i: (i // tps, 0, 0))
    stream = pltpu.VMEM((DEEP, tm, FF), BF16)
    return pl.pallas_call(
        body,
        grid=(ni,),
        in_specs=[hbm, hbm, hbm, hbm, row, vec, row] + c_in,
        out_specs=[pl.BlockSpec((tm, 2 * FF), lambda i: (i, 0)), row, vec, vec] + c_out,
        out_shape=[_sds((n, 2 * FF), BF16), _sds((n, D), F32), _sds((nb, 1, D), F32), _sds((nb, 1, D), F32)] + c_shapes,
        scratch_shapes=[pltpu.VMEM((D, 2 * FF), BF16), pltpu.SemaphoreType.DMA, stream, stream, stream, pltpu.SemaphoreType.DMA((3, DEEP))] + c_scratch,
        compiler_params=_params(("arbitrary",)),
        name=name,
    )(da, g, u, w, x, sc, dxres, *[a for a, _ in carry])


def proj_bwd_in(dparts, w, x, sc, dxres, seq, name, carry=()):
    n = x.shape[0]
    ni = n // TM
    tps = seq // TM
    nb = n // seq
    ms = [p.shape[1] for p in dparts]
    offs = [sum(ms[:k]) for k in range(len(ms))]
    npart = len(ms)
    nc = len(carry)
    gathers = [gt for _, gt in carry]
    c_in, c_out, c_shapes, c_scratch = _carry_plan(carry)

    def body(*refs):
        d_refs = refs[:npart]
        w_ref, x_ref, sc_ref, dxres_ref = refs[npart : npart + 4]
        cx = refs[npart + 4 : npart + 4 + nc]
        dx_ref, dsh_ref, dsc_ref = refs[npart + 4 + nc : npart + 7 + nc]
        co, sems = refs[npart + 7 + nc : npart + 7 + 2 * nc], refs[npart + 7 + 2 * nc :]
        i = pl.program_id(0)
        if nc:
            @pl.when(i == 0)
            def _():
                _carry_start(gathers, cx, co, sems)

        dh = None
        for d_ref, off, m in zip(d_refs, offs, ms):
            t = lax.dot_general(d_ref[...], w_ref[:, off : off + m], (((1,), (1,)), ((), ())), preferred_element_type=F32)
            dh = t if dh is None else dh + t
        _mod_bwd_finish(dh, x_ref, sc_ref, dxres_ref, dx_ref, dsh_ref, dsc_ref, i % tps == 0)
        if nc:
            @pl.when(i == ni - 1)
            def _():
                _carry_pass_on(gathers, cx, co, sems)
                _carry_wait(gathers, cx, co, sems)

    row = pl.BlockSpec((TM, D), lambda i: (i, 0))
    vec = pl.BlockSpec((1, 1, D), lambda i: (i // tps, 0, 0))
    return pl.pallas_call(
        body,
        grid=(ni,),
        in_specs=[pl.BlockSpec((TM, m), lambda i: (i, 0)) for m in ms] + [pl.BlockSpec((D, sum(ms)), lambda i: (0, 0)), row, vec, row] + c_in,
        out_specs=[row, vec, vec] + c_out,
        out_shape=[_sds((n, D), F32), _sds((nb, 1, D), F32), _sds((nb, 1, D), F32)] + c_shapes,
        scratch_shapes=c_scratch,
        compiler_params=_params(("arbitrary",)),
        name=name,
    )(*dparts, w, x, sc, dxres, *[a for a, _ in carry])


def mm_tn(a, b, name, cut=0, carry=()):
    n, k1 = a.shape
    k2 = b.shape[1]
    t1 = k1 if k1 <= 1536 else _tile(k1, 1536)
    t2 = k2 if k2 <= 1536 else _tile(k2, 1536)
    tk = 2048 if n % 2048 == 0 else n
    nk = n // tk
    cw = k2 // cut if cut else t2
    per = t2 // cw
    assert per * cw == t2 and (not cut or t1 == k1)
    n1, n2 = k1 // t1, k2 // t2
    nc = len(carry)
    gathers = [gt for _, gt in carry]
    c_in, c_out, c_shapes, c_scratch = _carry_plan(carry)

    def body(a_ref, b_ref, *rest):
        cx, o_ref, co, (acc, *sems) = rest[:nc], rest[nc], rest[nc + 1 : 2 * nc + 1], rest[2 * nc + 1 :]
        if nc:
            @pl.when((pl.program_id(0) == 0) & (pl.program_id(1) == 0) & (pl.program_id(2) == 0))
            def _():
                _carry_start(gathers, cx, co, sems)

        t = lax.dot_general(a_ref[...], b_ref[...], (((0,), (0,)), ((), ())), preferred_element_type=F32)

        @pl.when(pl.program_id(2) == 0)
        def _():
            acc[...] = t

        @pl.when(pl.program_id(2) > 0)
        def _():
            acc[...] += t

        @pl.when(pl.program_id(2) == nk - 1)
        def _():
            if cut:
                for s in range(per):
                    o_ref[s] = acc[:, s * cw : (s + 1) * cw].astype(BF16)
            else:
                o_ref[...] = acc[...].astype(BF16)

        if nc:
            @pl.when((pl.program_id(0) == n1 - 1) & (pl.program_id(1) == n2 - 1) & (pl.program_id(2) == nk - 1))
            def _():
                _carry_pass_on(gathers, cx, co, sems)
                _carry_wait(gathers, cx, co, sems)

    o_spec = pl.BlockSpec((per, k1, cw), lambda i, j, k: (j, 0, 0)) if cut else pl.BlockSpec((t1, t2), lambda i, j, k: (i, j))
    outs = pl.pallas_call(
        body,
        grid=(n1, n2, nk),
        in_specs=[pl.BlockSpec((tk, t1), lambda i, j, k: (k, i)), pl.BlockSpec((tk, t2), lambda i, j, k: (k, j))] + c_in,
        out_specs=[o_spec] + c_out,
        out_shape=[_sds((cut, k1, cw), BF16) if cut else _sds((k1, k2), BF16)] + c_shapes,
        scratch_shapes=[pltpu.VMEM((t1, t2), F32)] + c_scratch,
        compiler_params=_params(("arbitrary", "arbitrary", "arbitrary")),
        name=name,
    )(a, b, *[x for x, _ in carry])
    return outs if nc else outs[0]


def _tile(n, cap):
    best = LANE
    for t in range(LANE, cap + 1, LANE):
        if n % t == 0:
            best = t
    return best


def _lane_tiles(ref):
    return [slice(c0, c0 + LANE) for c0 in range(0, ref.shape[1], LANE)]


def _conv_taps(ext_ref, w_ref, halo, tq, kk):
    out = []
    for ln in _lane_tiles(w_ref):
        acc = None
        for k in range(kk):
            t = w_ref[k : k + 1, ln] * ext_ref[pl.ds(halo - (kk - 1 - k), tq), ln]
            acc = t if acc is None else acc + t
        out.append(acc)
    return jnp.concatenate(out, axis=1)


def _conv_taps_t(ext2_ref, w_ref, tq, kk):
    out = []
    for ln in _lane_tiles(w_ref):
        acc = None
        for k in range(kk):
            t = w_ref[k : k + 1, ln] * ext2_ref[pl.ds(kk - 1 - k, tq), ln]
            acc = t if acc is None else acc + t
        out.append(acc)
    return jnp.concatenate(out, axis=1)


def _conv_dw(ext_ref, ext2_ref, dw_ref, halo, tq, kk):
    for ln in _lane_tiles(dw_ref):
        dy = ext2_ref[pl.ds(0, tq), ln]
        for k in range(kk):
            dw_ref[k : k + 1, ln] += jnp.sum(dy * ext_ref[pl.ds(halo - (kk - 1 - k), tq), ln], axis=0, keepdims=True)


def _halo_spec(rows, width, tq, shift):
    per = tq // rows

    if shift < 0:
        return lambda nblocks: pl.BlockSpec((rows, width), lambda i: (jnp.maximum(i * per - 1, 0), 0))
    return lambda nblocks: pl.BlockSpec((rows, width), lambda i: (jnp.minimum((i + 1) * per, nblocks - 1), 0))


def _ln(c, g, b):
    mu = jnp.mean(c, axis=-1, keepdims=True)
    cc = c - mu
    var = jnp.mean(cc * cc, axis=-1, keepdims=True)
    return cc * lax.rsqrt(var + EPS) * g + b


def _silu(v):
    return v * jax.nn.sigmoid(v)


@jax.custom_vjp
def _expand(v, e):
    hi = v.astype(BF16)
    r1 = v - hi.astype(F32)
    mid = r1.astype(BF16)
    lo = (r1 - mid.astype(F32)).astype(BF16)
    return (jnp.dot(hi, e, preferred_element_type=F32) + jnp.dot(mid, e, preferred_element_type=F32)
            + jnp.dot(lo, e, preferred_element_type=F32))


def _expand_fwd(v, e):
    return _expand(v, e), e


def _expand_bwd(e, g):
    nt = (((1,), (1,)), ((), ()))
    hi = g.astype(BF16)
    mid = (g - hi.astype(F32)).astype(BF16)
    dv = lax.dot_general(hi, e, nt, preferred_element_type=F32) + lax.dot_general(mid, e, nt, preferred_element_type=F32)
    return dv, jnp.zeros_like(e)


_expand.defvjp(_expand_fwd, _expand_bwd)


def _ssd_chunk(conv, dtr, z, s, dt_bias, a_log, dskip, norm_g, tril, e):
    q = SSD_Q
    act = _silu(conv)
    xs, bm, cm = act[:, :1024], act[:, 1024:1280], act[:, 1280:1536]
    lane = lax.broadcasted_iota(jnp.int32, (1, LANE), 1)
    lane_q = lax.broadcasted_iota(jnp.int32, (q, LANE), 1)
    sub_q = lax.broadcasted_iota(jnp.int32, (LANE, q), 0)
    causal = lax.broadcasted_iota(jnp.int32, (q, q), 0) >= lax.broadcasted_iota(jnp.int32, (q, q), 1)
    real = lane < SSD_HEADS
    dt = jnp.where(real, jax.nn.softplus(dtr + dt_bias), 0.0)
    a = jnp.where(real, -jnp.exp(a_log), 0.0)
    da = dt * a
    acs = jnp.dot(tril, da, precision=HI, preferred_element_type=F32)
    acs_t = lax.dot_general(da, tril, (((0,), (1,)), ((), ())), precision=HI, preferred_element_type=F32)
    dt_e = _expand(dt, e)
    acs_e = _expand(acs, e)
    one8 = jnp.ones((SUB, 1), F32)
    alast_e = _expand(one8 * jnp.sum(da, axis=0, keepdims=True), e)[0:1]
    d_e = _expand(one8 * jnp.where(real, dskip, 0.0), e)[0:1]
    xdt = xs * dt_e
    nt = (((1,), (1,)), ((), ()))
    tn = (((0,), (0,)), ((), ()))
    ys, snews = [], []
    for g in range(2):
        gl = slice(g * 512, (g + 1) * 512)
        bg = bm[:, g * 128 : (g + 1) * 128].astype(BF16)
        cg = cm[:, g * 128 : (g + 1) * 128].astype(BF16)
        cb = lax.dot_general(cg, bg, nt, preferred_element_type=F32)
        sg = s[:, gl]
        yoff = jnp.dot(cg, sg.astype(BF16), preferred_element_type=F32) * jnp.exp(acs_e[:, gl])
        pairs = []
        for j in range(4):
            xp = xdt[:, g * 512 + j * 128 : g * 512 + (j + 1) * 128].astype(BF16)
            outs = []
            for hh in (g * 8 + 2 * j, g * 8 + 2 * j + 1):
                col = jnp.sum(jnp.where(lane_q == hh, acs, 0.0), axis=1, keepdims=True)
                row = jnp.sum(jnp.where(sub_q == hh, acs_t, 0.0), axis=0, keepdims=True)
                m = cb * jnp.exp(jnp.where(causal, col - row, -1e30))
                outs.append(jnp.dot(m.astype(BF16), xp, preferred_element_type=F32))
            pairs.append(jnp.where(lane_q < SSD_P, outs[0], outs[1]))
        ys.append(jnp.concatenate(pairs, axis=1) + yoff)
        decay = jnp.exp(alast_e[:, gl] - acs_e[:, gl])
        snews.append(
            sg * jnp.exp(alast_e[:, gl]) + lax.dot_general(bg, (xdt[:, gl] * decay).astype(BF16), tn, preferred_element_type=F32)
        )
    y = jnp.concatenate(ys, axis=1) + xs * d_e
    gated = y * _silu(z)
    out = gated * lax.rsqrt(jnp.mean(gated * gated, axis=-1, keepdims=True) + EPS) * norm_g
    return out, jnp.concatenate(snews, axis=1)


def _ssd_consts():
    tril = (lax.broadcasted_iota(jnp.int32, (SSD_Q, SSD_Q), 0) >= lax.broadcasted_iota(jnp.int32, (SSD_Q, SSD_Q), 1)).astype(F32)
    e = (lax.broadcasted_iota(jnp.int32, (LANE, 1024), 0) == lax.broadcasted_iota(jnp.int32, (LANE, 1024), 1) // SSD_P).astype(BF16)
    return tril, e


def ssd_fwd(z, xbc, dtr, cw, cb, dt_bias, a_log, dskip, norm_g, seq, name):
    n = z.shape[0]
    q = SSD_Q
    nc = seq // q
    tril, e = _ssd_consts()

    def body(z_ref, xbc_ref, halo_ref, dtr_ref, cw_ref, cb_ref, dtb_ref, alog_ref, dsk_ref, ng_ref, tril_ref, e_ref, y_ref, sprev_ref, s_scr, ext):
        c = pl.program_id(0) % nc

        @pl.when(c == 0)
        def _():
            s_scr[...] = jnp.zeros_like(s_scr)

        ext[0:SUB, :] = jnp.where(c == 0, 0.0, halo_ref[...])
        ext[SUB:, :] = xbc_ref[...]
        conv = _conv_taps(ext, cw_ref, SUB, q, 4) + cb_ref[...]
        sprev_ref[0] = s_scr[...]
        y, snew = _ssd_chunk(conv, dtr_ref[...], z_ref[...], s_scr[...], dtb_ref[...], alog_ref[...], dsk_ref[...], ng_ref[...], tril_ref[...], e_ref[...])
        y_ref[...] = y.astype(BF16)
        s_scr[...] = snew

    def full(shape):
        return pl.BlockSpec(shape, lambda i: (0,) * len(shape))

    return pl.pallas_call(
        body,
        grid=(n // q,),
        in_specs=[
            pl.BlockSpec((q, 1024), lambda i: (i, 0)),
            pl.BlockSpec((q, SSD_XBC), lambda i: (i, 0)),
            _halo_spec(SUB, SSD_XBC, q, -1)(n // SUB),
            pl.BlockSpec((q, LANE), lambda i: (i, 0)),
            full((4, SSD_XBC)),
            full((1, SSD_XBC)),
            full((1, LANE)),
            full((1, LANE)),
            full((1, LANE)),
            full((1, 1024)),
            full((q, q)),
            full((LANE, 1024)),
        ],
        out_specs=[pl.BlockSpec((q, 1024), lambda i: (i, 0)), pl.BlockSpec((1, LANE, 1024), lambda i: (i, 0, 0))],
        out_shape=[_sds((n, 1024), BF16), _sds((n // q, LANE, 1024), F32)],
        scratch_shapes=[pltpu.VMEM((LANE, 1024), F32), pltpu.VMEM((SUB + q, SSD_XBC), F32)],
        compiler_params=_params(("arbitrary",)),
        name=name,
    )(z, xbc, xbc, dtr, cw, cb, dt_bias, a_log, dskip, norm_g, tril, e)


def ssd_bwd(dy, z, xbc, dtr, sprev, cw, cb, dt_bias, a_log, dskip, norm_g, seq, name):
    n = z.shape[0]
    q = SSD_Q
    nc = seq // q
    nchunks = n // q
    tril, e = _ssd_consts()

    def rev(i):
        return (i // nc) * nc + (nc - 1 - i % nc)

    def body(dy_ref, z_ref, xbc_ref, halo_ref, dtr_ref, sprev_ref, cw_ref, cb_ref, dtb_ref, alog_ref, dsk_ref, ng_ref, tril_ref, e_ref,
             dz_ref, dxbc_ref, ddt_ref, dcw_ref, dcb_ref, ddtb_ref, dalog_ref, ddsk_ref, dng_ref, ds_scr, ext, ext2):
        i = pl.program_id(0)
        step = i % nc
        c = nc - 1 - step

        @pl.when(step == 0)
        def _():
            ds_scr[...] = jnp.zeros_like(ds_scr)
            ext2[q:, :] = jnp.zeros((SUB, SSD_XBC), F32)

        @pl.when(i == 0)
        def _():
            for r in (dcw_ref, dcb_ref, ddtb_ref, dalog_ref, ddsk_ref, dng_ref):
                r[...] = jnp.zeros_like(r)

        ext[0:SUB, :] = jnp.where(c == 0, 0.0, halo_ref[...])
        ext[SUB:, :] = xbc_ref[...]
        conv = _conv_taps(ext, cw_ref, SUB, q, 4) + cb_ref[...]
        tril_v, e_v = tril_ref[...], e_ref[...]

        def f(conv, dtr, z, s, dtb, alog, dsk, ng):
            return _ssd_chunk(conv, dtr, z, s, dtb, alog, dsk, ng, tril_v, e_v)

        _, vjp = jax.vjp(f, conv, dtr_ref[...], z_ref[...], sprev_ref[0], dtb_ref[...], alog_ref[...], dsk_ref[...], ng_ref[...])
        dconv, ddtr, dz, dsprev, ddtb, dalog, ddsk, dng = vjp((dy_ref[...].astype(F32), ds_scr[...]))
        ds_scr[...] = dsprev
        dz_ref[...] = dz.astype(BF16)
        ddt_ref[...] = ddtr.astype(BF16)
        ext2[0:q, :] = dconv
        dxbc_ref[...] = _conv_taps_t(ext2, cw_ref, q, 4).astype(BF16)
        ext2[q:, :] = dconv[0:SUB, :]
        _conv_dw(ext, ext2, dcw_ref, SUB, q, 4)
        dcb_ref[...] += jnp.sum(dconv, axis=0, keepdims=True)
        ddtb_ref[...] += ddtb
        dalog_ref[...] += dalog
        ddsk_ref[...] += ddsk
        dng_ref[...] += dng

    def full(shape):
        return pl.BlockSpec(shape, lambda i: (0,) * len(shape))

    per = q // SUB
    return pl.pallas_call(
        body,
        grid=(nchunks,),
        in_specs=[
            pl.BlockSpec((q, 1024), lambda i: (rev(i), 0)),
            pl.BlockSpec((q, 1024), lambda i: (rev(i), 0)),
            pl.BlockSpec((q, SSD_XBC), lambda i: (rev(i), 0)),
            pl.BlockSpec((SUB, SSD_XBC), lambda i: (jnp.maximum(rev(i) * per - 1, 0), 0)),
            pl.BlockSpec((q, LANE), lambda i: (rev(i), 0)),
            pl.BlockSpec((1, LANE, 1024), lambda i: (rev(i), 0, 0)),
            full((4, SSD_XBC)),
            full((1, SSD_XBC)),
            full((1, LANE)),
            full((1, LANE)),
            full((1, LANE)),
            full((1, 1024)),
            full((q, q)),
            full((LANE, 1024)),
        ],
        out_specs=[
            pl.BlockSpec((q, 1024), lambda i: (rev(i), 0)),
            pl.BlockSpec((q, SSD_XBC), lambda i: (rev(i), 0)),
            pl.BlockSpec((q, LANE), lambda i: (rev(i), 0)),
            full((4, SSD_XBC)),
            full((1, SSD_XBC)),
            full((1, LANE)),
            full((1, LANE)),
            full((1, LANE)),
            full((1, 1024)),
        ],
        out_shape=[
            _sds((n, 1024), BF16),
            _sds((n, SSD_XBC), BF16),
            _sds((n, LANE), BF16),
            _sds((4, SSD_XBC), F32),
            _sds((1, SSD_XBC), F32),
            _sds((1, LANE), F32),
            _sds((1, LANE), F32),
            _sds((1, LANE), F32),
            _sds((1, 1024), F32),
        ],
        scratch_shapes=[pltpu.VMEM((LANE, 1024), F32), pltpu.VMEM((SUB + q, SSD_XBC), F32), pltpu.VMEM((q + SUB, SSD_XBC), F32)],
        compiler_params=_params(("arbitrary",)),
        name=name,
    )(dy, z, xbc, xbc, dtr, sprev, cw, cb, dt_bias, a_log, dskip, norm_g, tril, e)


POOL_HALO = 16
TQ = 512


def _pool_count(pos, w):
    return jnp.minimum(pos + 1.0, float(w))


def pool_fwd(u, pw, scale, seq, name):
    n = u.shape[0]
    tq, halo = TQ, POOL_HALO
    tps = seq // tq

    def body(u_ref, halo_ref, pw_ref, sc_ref, y_ref, ext):
        t0 = pl.program_id(0) % tps
        ext[0:halo, :] = jnp.where(t0 == 0, 0.0, halo_ref[...])
        ext[halo:, :] = u_ref[...]
        pos = (t0 * tq + lax.broadcasted_iota(jnp.int32, (tq, 1), 0)).astype(F32)
        for g, w in enumerate(POOL_WINDOWS):
            ln = slice(g * LANE, (g + 1) * LANE)
            acc = ext[pl.ds(halo, tq), ln]
            for j in range(1, w):
                acc = acc + ext[pl.ds(halo - j, tq), ln]
            pooled = acc / _pool_count(pos, w) - u_ref[:, ln]
            mixed = jnp.dot(pooled.astype(BF16), pw_ref[g].astype(BF16), preferred_element_type=F32)
            y_ref[:, ln] = (mixed * sc_ref[:, ln]).astype(BF16)

    return pl.pallas_call(
        body,
        grid=(n // tq,),
        in_specs=[
            pl.BlockSpec((tq, POOL_DIM), lambda i: (i, 0)),
            _halo_spec(halo, POOL_DIM, tq, -1)(n // halo),
            pl.BlockSpec((4, LANE, LANE), lambda i: (0, 0, 0)),
            pl.BlockSpec((1, POOL_DIM), lambda i: (0, 0)),
        ],
        out_specs=pl.BlockSpec((tq, POOL_DIM), lambda i: (i, 0)),
        out_shape=_sds((n, POOL_DIM), BF16),
        scratch_shapes=[pltpu.VMEM((halo + tq, POOL_DIM), F32)],
        compiler_params=_params(("arbitrary",)),
        name=name,
    )(u, u, pw, scale)


def pool_bwd(dy, u, pw, scale, seq, name):
    n = u.shape[0]
    tq, halo = TQ, POOL_HALO
    tps = seq // tq
    nt = (((1,), (1,)), ((), ()))
    tn = (((0,), (0,)), ((), ()))

    def body(dy_ref, dyn_ref, u_ref, halo_ref, pw_ref, sc_ref, du_ref, dpw_ref, dsc_ref, ext, ext2):
        i = pl.program_id(0)
        t0 = i % tps

        @pl.when(i == 0)
        def _():
            dpw_ref[...] = jnp.zeros_like(dpw_ref)
            dsc_ref[...] = jnp.zeros_like(dsc_ref)

        ext[0:halo, :] = jnp.where(t0 == 0, 0.0, halo_ref[...])
        ext[halo:, :] = u_ref[...]
        pos = (t0 * tq + lax.broadcasted_iota(jnp.int32, (tq, 1), 0)).astype(F32)
        dyv = dy_ref[...].astype(F32)
        dynv = jnp.where(t0 == tps - 1, 0.0, dyn_ref[...].astype(F32))
        for g, w in enumerate(POOL_WINDOWS):
            ln = slice(g * LANE, (g + 1) * LANE)
            wg = pw_ref[g].astype(BF16)
            acc = ext[pl.ds(halo, tq), ln]
            for j in range(1, w):
                acc = acc + ext[pl.ds(halo - j, tq), ln]
            pooled = (acc / _pool_count(pos, w) - u_ref[:, ln]).astype(BF16)
            mixed = jnp.dot(pooled, wg, preferred_element_type=F32)
            dsc_ref[:, ln] += jnp.sum(dyv[:, ln] * mixed, axis=0, keepdims=True)
            dmix = (dyv[:, ln] * sc_ref[:, ln]).astype(BF16)
            dpw_ref[g] += lax.dot_general(pooled, dmix, tn, preferred_element_type=F32)
            dpool = lax.dot_general(dmix, wg, nt, preferred_element_type=F32)
            dmix_n = (dynv[:, ln] * sc_ref[:, ln]).astype(BF16)
            dpool_n = lax.dot_general(dmix_n, wg, nt, preferred_element_type=F32)
            ext2[0:tq, ln] = dpool / _pool_count(pos, w)
            ext2[tq:, ln] = dpool_n * (1.0 / w)
            acc2 = ext2[pl.ds(0, tq), ln]
            for j in range(1, w):
                acc2 = acc2 + ext2[pl.ds(j, tq), ln]
            du_ref[:, ln] = (acc2 - dpool).astype(BF16)

    return pl.pallas_call(
        body,
        grid=(n // tq,),
        in_specs=[
            pl.BlockSpec((tq, POOL_DIM), lambda i: (i, 0)),
            _halo_spec(halo, POOL_DIM, tq, +1)(n // halo),
            pl.BlockSpec((tq, POOL_DIM), lambda i: (i, 0)),
            _halo_spec(halo, POOL_DIM, tq, -1)(n // halo),
            pl.BlockSpec((4, LANE, LANE), lambda i: (0, 0, 0)),
            pl.BlockSpec((1, POOL_DIM), lambda i: (0, 0)),
        ],
        out_specs=[
            pl.BlockSpec((tq, POOL_DIM), lambda i: (i, 0)),
            pl.BlockSpec((4, LANE, LANE), lambda i: (0, 0, 0)),
            pl.BlockSpec((1, POOL_DIM), lambda i: (0, 0)),
        ],
        out_shape=[_sds((n, POOL_DIM), BF16), _sds((4, LANE, LANE), F32), _sds((1, POOL_DIM), F32)],
        scratch_shapes=[pltpu.VMEM((halo + tq, POOL_DIM), F32), pltpu.VMEM((tq + halo, POOL_DIM), F32)],
        compiler_params=_params(("arbitrary",)),
        name=name,
    )(dy, dy, u, u, pw, scale)


CONF_HALO = 32
TQC = 256


def _conf_post(c, g, b):
    return _silu(_ln(c, g, b))


def conf_fwd(vg, w, b, lng, lnb, seq, name):
    n = vg.shape[0]
    tq, halo, kk = TQC, CONF_HALO, CONF_K
    tps = seq // tq
    c = CONF_DIM

    def body(vg_ref, halo_ref, w_ref, b_ref, lng_ref, lnb_ref, y_ref, conv_ref, ext):
        t0 = pl.program_id(0) % tps
        hv = halo_ref[...]
        ext[0:halo, :] = jnp.where(t0 == 0, 0.0, hv[:, :c] * jax.nn.sigmoid(hv[:, c:]))
        ext[halo:, :] = vg_ref[:, :c] * jax.nn.sigmoid(vg_ref[:, c:])
        conv = _conv_taps(ext, w_ref, halo, tq, kk) + b_ref[...]
        conv_ref[...] = conv
        y_ref[...] = _conf_post(conv, lng_ref[...], lnb_ref[...]).astype(BF16)

    one = pl.BlockSpec((1, c), lambda i: (0, 0))
    return pl.pallas_call(
        body,
        grid=(n // tq,),
        in_specs=[pl.BlockSpec((tq, 2 * c), lambda i: (i, 0)), _halo_spec(halo, 2 * c, tq, -1)(n // halo), pl.BlockSpec((kk, c), lambda i: (0, 0)), one, one, one],
        out_specs=[pl.BlockSpec((tq, c), lambda i: (i, 0)), pl.BlockSpec((tq, c), lambda i: (i, 0))],
        out_shape=[_sds((n, c), BF16), _sds((n, c), F32)],
        scratch_shapes=[pltpu.VMEM((halo + tq, c), F32)],
        compiler_params=_params(("arbitrary",)),
        name=name,
    )(vg, vg, w, b, lng, lnb)


def conf_bwd(dy, conv, vg, w, lng, lnb, seq, name):
    n = vg.shape[0]
    tq, halo, kk = TQC, CONF_HALO, CONF_K
    tps = seq // tq
    c = CONF_DIM

    def body(dy_ref, dyn_ref, conv_ref, convn_ref, vg_ref, halo_ref, w_ref, lng_ref, lnb_ref, dvg_ref, dw_ref, db_ref, dlng_ref, dlnb_ref, ext, ext2):
        i = pl.program_id(0)
        t0 = i % tps

        @pl.when(i == 0)
        def _():
            for r in (dw_ref, db_ref, dlng_ref, dlnb_ref):
                r[...] = jnp.zeros_like(r)

        _, vjp = jax.vjp(_conf_post, conv_ref[...], lng_ref[...], lnb_ref[...])
        dconv, dlng, dlnb = vjp(dy_ref[...].astype(F32))
        _, vjpn = jax.vjp(_conf_post, convn_ref[...], lng_ref[...], lnb_ref[...])
        dconv_n = vjpn(dyn_ref[...].astype(F32))[0]
        ext2[0:tq, :] = dconv
        ext2[tq:, :] = jnp.where(t0 == tps - 1, 0.0, dconv_n)
        dh = _conv_taps_t(ext2, w_ref, tq, kk)
        hv = halo_ref[...]
        ext[0:halo, :] = jnp.where(t0 == 0, 0.0, hv[:, :c] * jax.nn.sigmoid(hv[:, c:]))
        v = vg_ref[:, :c]
        s = jax.nn.sigmoid(vg_ref[:, c:])
        ext[halo:, :] = v * s
        _conv_dw(ext, ext2, dw_ref, halo, tq, kk)
        db_ref[...] += jnp.sum(dconv, axis=0, keepdims=True)
        dlng_ref[...] += dlng
        dlnb_ref[...] += dlnb
        dvg_ref[:, :c] = (dh * s).astype(BF16)
        dvg_ref[:, c:] = (dh * v * s * (1.0 - s)).astype(BF16)

    one = pl.BlockSpec((1, c), lambda i: (0, 0))
    tile = pl.BlockSpec((tq, c), lambda i: (i, 0))
    nxt = _halo_spec(halo, c, tq, +1)(n // halo)
    return pl.pallas_call(
        body,
        grid=(n // tq,),
        in_specs=[tile, nxt, tile, nxt, pl.BlockSpec((tq, 2 * c), lambda i: (i, 0)), _halo_spec(halo, 2 * c, tq, -1)(n // halo),
                  pl.BlockSpec((kk, c), lambda i: (0, 0)), one, one],
        out_specs=[pl.BlockSpec((tq, 2 * c), lambda i: (i, 0)), pl.BlockSpec((kk, c), lambda i: (0, 0)), one, one, one],
        out_shape=[_sds((n, 2 * c), BF16), _sds((kk, c), F32), _sds((1, c), F32), _sds((1, c), F32), _sds((1, c), F32)],
        scratch_shapes=[pltpu.VMEM((halo + tq, c), F32), pltpu.VMEM((tq + halo, c), F32)],
        compiler_params=_params(("arbitrary",)),
        name=name,
    )(dy, dy, conv, conv, vg, vg, w, lng, lnb)


TL = 256


def _expm1_neg(t):
    p = t * (1.0 + t * (1.0 / 2 + t * (1.0 / 6 + t * (1.0 / 24 + t * (1.0 / 120)))))
    return jnp.where(t > -0.1, p, jnp.exp(t) - 1.0)


def _lru_gate(xc, ra, ia, ba, bx, lam):
    r = jax.nn.sigmoid(ra + ba)
    i = jax.nn.sigmoid(ia + bx)
    log_a = -LRU_C * r * jax.nn.softplus(-lam)
    return jnp.exp(log_a), jnp.sqrt(-_expm1_neg(2.0 * log_a)) * (i * xc)


def _lru_out(h, gr):
    return h * jax.nn.gelu(gr)


def _scan_rows(a, b, tq, reverse):
    r8 = lax.broadcasted_iota(jnp.int32, (tq, 1), 0) % SUB
    for d in (1, 2, 4):
        sh = tq - d if reverse else d
        valid = (r8 < SUB - d) if reverse else (r8 >= d)
        a_s = pltpu.roll(a, sh, 0)
        b_s = pltpu.roll(b, sh, 0)
        b = jnp.where(valid, a * b_s, 0.0) + b
        a = jnp.where(valid, a * a_s, a)
    ng = tq // SUB
    edge = 0 if reverse else SUB - 1
    out_a, out_b = [None] * ng, [None] * ng
    ca = cb = None
    for g in (reversed(range(ng)) if reverse else range(ng)):
        ag, bg = a[g * SUB : (g + 1) * SUB, :], b[g * SUB : (g + 1) * SUB, :]
        if ca is not None:
            bg = bg + ag * cb
            ag = ag * ca
        out_a[g], out_b[g] = ag, bg
        ca, cb = ag[edge : edge + 1, :], bg[edge : edge + 1, :]
    return jnp.concatenate(out_a, axis=0), jnp.concatenate(out_b, axis=0)


def _row_of(v, r, tq):
    row = lax.broadcasted_iota(jnp.int32, (tq, 1), 0)
    return jnp.sum(jnp.where(row == r, v, 0.0), axis=0, keepdims=True)


def _head_mm(xc, w_ref):
    return jnp.concatenate(
        [
            jnp.dot(xc[:, h * LANE : (h + 1) * LANE].astype(BF16), w_ref[h].astype(BF16), preferred_element_type=F32)
            for h in range(LRU_HEADS)
        ],
        axis=1,
    )


def lru_fwd(xr, gr, cw, cb, wa, ba, wx, bx, lam, seq, name):
    n = xr.shape[0]
    tq = TL
    tps = seq // tq
    c = LRU_DIM

    def body(xr_ref, halo_ref, gr_ref, cw_ref, cb_ref, wa_ref, ba_ref, wx_ref, bx_ref, lam_ref, y_ref, h_ref, hc, ext):
        t0 = pl.program_id(0) % tps

        @pl.when(t0 == 0)
        def _():
            hc[...] = jnp.zeros_like(hc)

        ext[0:SUB, :] = jnp.where(t0 == 0, 0.0, halo_ref[...])
        ext[SUB:, :] = xr_ref[...]
        xc = _conv_taps(ext, cw_ref, SUB, tq, 4) + cb_ref[...]
        a, b = _lru_gate(xc, _head_mm(xc, wa_ref), _head_mm(xc, wx_ref), ba_ref[...], bx_ref[...], lam_ref[...])
        acum, h0 = _scan_rows(a, b, tq, False)
        h = h0 + acum * hc[0:1, :]
        h_ref[...] = h
        hc[0:1, :] = h_ref[tq - 1 : tq, :]
        y_ref[...] = _lru_out(h, gr_ref[...]).astype(BF16)

    one = pl.BlockSpec((1, c), lambda i: (0, 0))
    tile = pl.BlockSpec((tq, c), lambda i: (i, 0))
    hw = pl.BlockSpec((LRU_HEADS, LANE, LANE), lambda i: (0, 0, 0))
    return pl.pallas_call(
        body,
        grid=(n // tq,),
        in_specs=[tile, _halo_spec(SUB, c, tq, -1)(n // SUB), tile, pl.BlockSpec((4, c), lambda i: (0, 0)), one, hw, one, hw, one, one],
        out_specs=[tile, tile],
        out_shape=[_sds((n, c), BF16), _sds((n, c), F32)],
        scratch_shapes=[pltpu.VMEM((SUB, c), F32), pltpu.VMEM((SUB + tq, c), F32)],
        compiler_params=_params(("arbitrary",)),
        name=name,
    )(xr, xr, gr, cw, cb, wa, ba, wx, bx, lam)


def lru_bwd(dy, xr, gr, h, cw, cb, wa, ba, wx, bx, lam, seq, name):
    n = xr.shape[0]
    tq = TL
    tps = seq // tq
    ntile = n // tq
    c = LRU_DIM
    per = tq // SUB
    nt = (((1,), (1,)), ((), ()))
    tn = (((0,), (0,)), ((), ()))

    def rev(i):
        return (i // tps) * tps + (tps - 1 - i % tps)

    def body(dy_ref, xr_ref, halo_ref, gr_ref, h_ref, hprev_ref, cw_ref, cb_ref, wa_ref, ba_ref, wx_ref, bx_ref, lam_ref,
             dxr_ref, dgr_ref, dcw_ref, dcb_ref, dwa_ref, dba_ref, dwx_ref, dbx_ref, dlam_ref, carry, ext, ext2):
        i = pl.program_id(0)
        step = i % tps
        t0 = tps - 1 - step

        @pl.when(step == 0)
        def _():
            carry[...] = jnp.zeros_like(carry)
            ext2[tq:, :] = jnp.zeros((SUB, c), F32)

        @pl.when(i == 0)
        def _():
            for r in (dcw_ref, dcb_ref, dwa_ref, dba_ref, dwx_ref, dbx_ref, dlam_ref):
                r[...] = jnp.zeros_like(r)

        ext[0:SUB, :] = jnp.where(t0 == 0, 0.0, halo_ref[...])
        ext[SUB:, :] = xr_ref[...]
        xc = _conv_taps(ext, cw_ref, SUB, tq, 4) + cb_ref[...]
        (a, _), vjp_gate = jax.vjp(_lru_gate, xc, _head_mm(xc, wa_ref), _head_mm(xc, wx_ref), ba_ref[...], bx_ref[...], lam_ref[...])
        hv = h_ref[...]
        _, vjp_out = jax.vjp(_lru_out, hv, gr_ref[...])
        dh, dgr = vjp_out(dy_ref[...].astype(F32))
        dgr_ref[...] = dgr.astype(BF16)
        row = lax.broadcasted_iota(jnp.int32, (tq, 1), 0)
        a_up = jnp.where(row == tq - 1, carry[0:1, :], pltpu.roll(a, tq - 1, 0))
        acum, l0 = _scan_rows(a_up, dh, tq, True)
        lamv = l0 + acum * carry[1:2, :]
        carry[0:1, :] = _row_of(a, 0, tq)
        carry[1:2, :] = _row_of(lamv, 0, tq)
        hprev = jnp.where(row == 0, jnp.where(t0 == 0, 0.0, hprev_ref[SUB - 1 : SUB, :]), pltpu.roll(hv, 1, 0))
        dxc, dra, dia, dba, dbx, dlam = vjp_gate((lamv * hprev, lamv))
        dba_ref[...] += dba
        dbx_ref[...] += dbx
        dlam_ref[...] += dlam
        pieces = []
        for hh in range(LRU_HEADS):
            ln = slice(hh * LANE, (hh + 1) * LANE)
            xh = xc[:, ln].astype(BF16)
            drh = dra[:, ln].astype(BF16)
            dih = dia[:, ln].astype(BF16)
            dwa_ref[hh] += lax.dot_general(xh, drh, tn, preferred_element_type=F32)
            dwx_ref[hh] += lax.dot_general(xh, dih, tn, preferred_element_type=F32)
            pieces.append(
                lax.dot_general(drh, wa_ref[hh].astype(BF16), nt, preferred_element_type=F32)
                + lax.dot_general(dih, wx_ref[hh].astype(BF16), nt, preferred_element_type=F32)
            )
        dxc = dxc + jnp.concatenate(pieces, axis=1)
        ext2[0:tq, :] = dxc
        dxr_ref[...] = _conv_taps_t(ext2, cw_ref, tq, 4).astype(BF16)
        ext2[tq:, :] = ext2[0:SUB, :]
        _conv_dw(ext, ext2, dcw_ref, SUB, tq, 4)
        dcb_ref[...] += jnp.sum(dxc, axis=0, keepdims=True)

    one = pl.BlockSpec((1, c), lambda i: (0, 0))
    tile = pl.BlockSpec((tq, c), lambda i: (rev(i), 0))
    prev = pl.BlockSpec((SUB, c), lambda i: (jnp.maximum(rev(i) * per - 1, 0), 0))
    hw = pl.BlockSpec((LRU_HEADS, LANE, LANE), lambda i: (0, 0, 0))
    cw4 = pl.BlockSpec((4, c), lambda i: (0, 0))
    return pl.pallas_call(
        body,
        grid=(ntile,),
        in_specs=[tile, tile, prev, tile, tile, prev, cw4, one, hw, one, hw, one, one],
        out_specs=[tile, tile, cw4, one, hw, one, hw, one, one],
        out_shape=[_sds((n, c), BF16), _sds((n, c), BF16), _sds((4, c), F32), _sds((1, c), F32), _sds((LRU_HEADS, LANE, LANE), F32),
                   _sds((1, c), F32), _sds((LRU_HEADS, LANE, LANE), F32), _sds((1, c), F32), _sds((1, c), F32)],
        scratch_shapes=[pltpu.VMEM((SUB, c), F32), pltpu.VMEM((SUB + tq, c), F32), pltpu.VMEM((tq + SUB, c), F32)],
        compiler_params=_params(("arbitrary",)),
        name=name,
    )(dy, xr, xr, gr, h, h, cw, cb, wa, ba, wx, bx, lam)


def ada_fwd(c_all, w, b, name):
    nl, _, cols = w.shape
    nb = c_all.shape[0]

    def body(c_ref, w_ref, b_ref, o_ref):
        sc = _silu(c_ref[...]).astype(BF16)
        o_ref[0] = jnp.dot(sc, w_ref[0].astype(BF16), preferred_element_type=F32) + b_ref[0]

    return pl.pallas_call(
        body,
        grid=(nl,),
        in_specs=[pl.BlockSpec((nb, D), lambda l: (0, 0)), pl.BlockSpec((1, D, cols), lambda l: (l, 0, 0)), pl.BlockSpec((1, 1, cols), lambda l: (l, 0, 0))],
        out_specs=pl.BlockSpec((1, nb, cols), lambda l: (l, 0, 0)),
        out_shape=_sds((nl, nb, cols), F32),
        compiler_params=_params(("arbitrary",)),
        name=name,
    )(c_all, w, b)


def ada_bwd(c_all, dmod, name):
    nl, nb, cols = dmod.shape

    def body(c_ref, d_ref, o_ref):
        sc = _silu(c_ref[...]).astype(BF16)
        o_ref[0] = lax.dot_general(sc, d_ref[0].astype(BF16), (((0,), (0,)), ((), ())), preferred_element_type=F32)

    return pl.pallas_call(
        body,
        grid=(nl,),
        in_specs=[pl.BlockSpec((nb, D), lambda l: (0, 0)), pl.BlockSpec((1, nb, cols), lambda l: (l, 0, 0))],
        out_specs=pl.BlockSpec((1, D, cols), lambda l: (l, 0, 0)),
        out_shape=_sds((nl, D, cols), F32),
        compiler_params=_params(("arbitrary",)),
        name=name,
    )(c_all, dmod)


def loss_grad(y, target, name):
    n = y.shape[0]

    def body(y_ref, t_ref, dy_ref, l_ref, acc):
        i = pl.program_id(0)

        @pl.when(i == 0)
        def _():
            acc[...] = jnp.zeros_like(acc)

        e = y_ref[...] - t_ref[...]
        dy_ref[...] = e * (1.0 / D)
        acc[...] += jnp.sum(e * e, axis=0, keepdims=True)

        @pl.when(i == n // TM - 1)
        def _():
            l_ref[...] = jnp.full((1, LANE), 0.5 / D, F32) * jnp.sum(acc[...])

    row = pl.BlockSpec((TM, D), lambda i: (i, 0))
    return pl.pallas_call(
        body,
        grid=(n // TM,),
        in_specs=[row, row],
        out_specs=[row, pl.BlockSpec((1, LANE), lambda i: (0, 0))],
        out_shape=[_sds((n, D), F32), _sds((1, LANE), F32)],
        scratch_shapes=[pltpu.VMEM((1, D), F32)],
        compiler_params=_params(("arbitrary",)),
        name=name,
    )(y, target)


def sum_parts(parts, name):
    ns, r, _ = parts.shape
    tr = _row_tile(r, 1024)

    def body(p_ref, o_ref):
        acc = p_ref[0]
        for k in range(1, ns):
            acc = acc + p_ref[k]
        o_ref[...] = acc

    return pl.pallas_call(
        body,
        grid=(r // tr,),
        in_specs=[pl.BlockSpec((ns, tr, LANE), lambda i: (0, i, 0))],
        out_specs=pl.BlockSpec((tr, LANE), lambda i: (i, 0)),
        out_shape=_sds((r, LANE), F32),
        compiler_params=_params(("arbitrary",)),
        name=name,
    )(parts)


def _row_tile(r, cap):
    if r <= cap:
        return r
    best = None
    for t in range(16, cap + 1, 16):
        if r % t == 0:
            best = t
    assert best is not None, r
    return best


def adamw(w, m, v, gparts, name):
    ng, r, c = w.shape
    ns = gparts.shape[0]
    tr = _row_tile(r, min(512, 256 * 1024 // c))
    c1 = 1.0 - B1**STEP
    c2 = 1.0 - B2**STEP

    def body(w_ref, m_ref, v_ref, g_ref, go_ref, d_ref, mo_ref, vo_ref):
        g = g_ref[0, 0].astype(F32)
        for k in range(1, ns):
            g = g + g_ref[k, 0].astype(F32)
        mn = B1 * m_ref[0] + (1.0 - B1) * g
        vn = B2 * v_ref[0] + (1.0 - B2) * (g * g)
        go_ref[0] = g
        mo_ref[0] = mn
        vo_ref[0] = vn
        d_ref[0] = -LR * ((mn / c1) / (jnp.sqrt(vn / c2) + AEPS) + WD * w_ref[0])

    tile = pl.BlockSpec((1, tr, c), lambda b, i: (b, i, 0))
    return pl.pallas_call(
        body,
        grid=(ng, r // tr),
        in_specs=[tile, tile, tile, pl.BlockSpec((ns, 1, tr, c), lambda b, i: (0, b, i, 0))],
        out_specs=[tile, tile, tile, tile],
        out_shape=[_sds((ng, r, c), F32)] * 4,
        compiler_params=_params(("arbitrary", "arbitrary")),
        name=name,
    )(w, m, v, gparts)


WEIGHTS = ["ada_w", "ada_b", "ln_g", "ln_b", "ffn_w_in", "ffn_w_out", "ev_w_in", "ssd_conv_w", "ssd_conv_b", "ssd_dt_bias",
           "ssd_a_log", "ssd_d", "ssd_norm_g", "pool_w", "pool_scale", "ev_w_out", "od_w_in", "conf_dw_w", "conf_dw_b",
           "conf_ln_g", "conf_ln_b", "lru_conv_w", "lru_conv_b", "lru_wa", "lru_ba", "lru_wx", "lru_bx", "lru_lambda", "od_w_out"]
BIG = ("ada_w", "ffn_w_in", "ffn_w_out", "ev_w_in", "ev_w_out", "od_w_in", "od_w_out")
SMALL = {
    "ada_b": ((4, 9216), None), "ln_g": ((4, 3, 1024), 2), "ln_b": ((4, 3, 1024), 2),
    "ssd_conv_w": ((2, 4, 1536), 2), "ssd_conv_b": ((2, 1536), None), "ssd_dt_bias": ((2, 16), None),
    "ssd_a_log": ((2, 16), None), "ssd_d": ((2, 16), None), "ssd_norm_g": ((2, 1024), None),
    "pool_w": ((2, 4, 128, 128), None), "pool_scale": ((2, 512), None),
    "conf_dw_w": ((2, 31, 512), 2), "conf_dw_b": ((2, 512), 1), "conf_ln_g": ((2, 512), 1), "conf_ln_b": ((2, 512), 1),
    "lru_conv_w": ((2, 4, 1024), 2), "lru_conv_b": ((2, 1024), 1), "lru_wa": ((2, 8, 128, 128), None),
    "lru_ba": ((2, 1024), 1), "lru_wx": ((2, 8, 128, 128), None), "lru_bx": ((2, 1024), 1), "lru_lambda": ((2, 1024), 1),
}
PACK_ROWS = 2 * SUB * LANE


def _rows_of_piece(shape):
    return -(-math.prod(shape) // (SUB * LANE)) * SUB


def _pack(arrs, mult=PACK_ROWS):
    rows = [jnp.pad(a.reshape(-1), (0, _rows_of_piece(a.shape) * LANE - a.size)).reshape(-1, LANE) for a in arrs]
    buf = jnp.concatenate(rows, axis=0)
    pad = (-buf.shape[0]) % (mult // LANE)
    return jnp.pad(buf, ((0, pad), (0, 0)))


def _unpack(buf, shapes, lead=()):
    out, off = [], 0
    for s in shapes:
        k, nr = math.prod(s), _rows_of_piece(s)
        piece = buf[..., off : off + nr, :].reshape(lead + (nr * LANE,))
        out.append(piece[..., :k].reshape(lead + tuple(s)))
        off += nr
    return out


def _pad_lanes(v):
    return jnp.pad(v, (0, LANE - v.shape[0]))[None]


def kernel(x, c, ada_w, ada_b, ln_g, ln_b, ffn_w_in, ffn_w_out, ev_w_in, ssd_conv_w, ssd_conv_b, ssd_dt_bias, ssd_a_log, ssd_d, ssd_norm_g, pool_w, pool_scale, ev_w_out, od_w_in, conf_dw_w, conf_dw_b, conf_ln_g, conf_ln_b, lru_conv_w, lru_conv_b, lru_wa, lru_ba, lru_wx, lru_bx, lru_lambda, od_w_out, loss_target, m_ada_w, m_ada_b, m_ln_g, m_ln_b, m_ffn_w_in, m_ffn_w_out, m_ev_w_in, m_ssd_conv_w, m_ssd_conv_b, m_ssd_dt_bias, m_ssd_a_log, m_ssd_d, m_ssd_norm_g, m_pool_w, m_pool_scale, m_ev_w_out, m_od_w_in, m_conf_dw_w, m_conf_dw_b, m_conf_ln_g, m_conf_ln_b, m_lru_conv_w, m_lru_conv_b, m_lru_wa, m_lru_ba, m_lru_wx, m_lru_bx, m_lru_lambda, m_od_w_out, v_ada_w, v_ada_b, v_ln_g, v_ln_b, v_ffn_w_in, v_ffn_w_out, v_ev_w_in, v_ssd_conv_w, v_ssd_conv_b, v_ssd_dt_bias, v_ssd_a_log, v_ssd_d, v_ssd_norm_g, v_pool_w, v_pool_scale, v_ev_w_out, v_od_w_in, v_conf_dw_w, v_conf_dw_b, v_conf_ln_g, v_conf_ln_b, v_lru_conv_w, v_lru_conv_b, v_lru_wa, v_lru_ba, v_lru_wx, v_lru_bx, v_lru_lambda, v_od_w_out):
    p = dict(locals())
    nb, seq, _ = x.shape
    n = nb * seq
    me = 4 * lax.axis_index("x") + 2 * lax.axis_index("y") + lax.axis_index("c")
    sharded = [k for k, (_, ax) in SMALL.items() if ax is not None]

    def cols_of(g):
        return jnp.moveaxis(g, 0, 1).reshape(g.shape[1], N_DEV * g.shape[2])

    def rows_of(g):
        return g.reshape(N_DEV * g.shape[1], g.shape[2])

    def ev_in_of(g):
        w = cols_of(g)
        return jnp.concatenate([w[:, :2560], w[:, 2576:], jnp.pad(w[:, 2560:2576], ((0, 0), (0, LANE - SSD_HEADS)))], axis=1)

    sh_ffn_in, sh_ffn_out = ffn_w_in.astype(BF16), ffn_w_out.astype(BF16)
    sh_mix_in = [ev_w_in.astype(BF16), od_w_in.astype(BF16)]
    sh_mix_out = [ev_w_out.astype(BF16), od_w_out.astype(BF16)]

    def ffn_items(l, i):
        return [(sh_ffn_in[l, i], True), (sh_ffn_out[l, i], True)]

    def mix_items(l):
        return [(sh_mix_in[l % 2][l // 2], True), (sh_mix_out[l % 2][l // 2], True)]

    sm_local_shapes = [p[k].shape for k in sharded]
    g_in, sm_all = exchange([ffn_items(0, 0)[0], (_pack([p[k] for k in sharded] + [c]), True)], "ag_first")
    w_ffn = {(0, 0): (cols_of(g_in), None)}
    w_mix = {}
    got = _unpack(sm_all, sm_local_shapes + [c.shape], lead=(N_DEV,))
    full = {k: p[k] for k, (_, ax) in SMALL.items() if ax is None}
    for k, g in zip(sharded, got[:-1]):
        full[k] = jnp.moveaxis(g, 0, SMALL[k][1]).reshape(SMALL[k][0])
    c_all = got[-1].reshape(N_DEV * nb, D)

    cols = ada_w.shape[-1]
    ada_b_loc = lax.dynamic_slice_in_dim(ada_b, me * cols, cols, axis=1)[:, None, :]
    mod_cols = ada_fwd(c_all, ada_w, ada_b_loc, "ada_fwd")
    (mod_x,) = exchange([(mod_cols.reshape(DEPTH, N_DEV, nb, cols).transpose(1, 0, 2, 3), False)], "a2a_mod")
    mod = mod_x.transpose(1, 2, 0, 3).reshape(DEPTH, nb, N_MOD, 1, D)

    def vec(l, j):
        return mod[l, :, j]

    def row(a):
        return a[None]

    xs = x.reshape(n, D)
    saved = []
    for l in range(DEPTH):
        s = {"x0": xs}
        e = l // 2
        late = [] if l else [ffn_items(0, 0)[1]]
        s["h1"], s["g1"], s["u1"], s["a1"], g_in, g_out, *g_late = ffn_up(
            xs, vec(l, 0), vec(l, 1), w_ffn[l, 0][0], seq, "ffn_up_c%d" % (2 + len(late)), carry=ffn_items(l, 1) + late)
        w_ffn[l, 1] = (cols_of(g_in), rows_of(g_out))
        if late:
            w_ffn[0, 0] = (w_ffn[0, 0][0], rows_of(g_late[0]))
        x1, s["y1"], gm_in, gm_out = mm_postnorm([s["a1"]], w_ffn[l, 0][1], xs, vec(l, 2), row(full["ln_g"][l, 0]), row(full["ln_b"][l, 0]), 0.5, seq,
                                                 "ffn_down_c2", carry=mix_items(l))
        w_mix[l] = ((ev_in_of if l % 2 == 0 else cols_of)(gm_in), rows_of(gm_out))
        s["x1"] = x1
        if l % 2 == 0:
            s["h2"], s["z"], s["xbc"], s["u"], s["dtr"] = mod_mm(x1, vec(l, 3), vec(l, 4), w_mix[l][0], EV_SPLITS, seq, "ev_in")
            s["ya"], s["sprev"] = ssd_fwd(s["z"], s["xbc"], s["dtr"], full["ssd_conv_w"][e], row(full["ssd_conv_b"][e]), _pad_lanes(full["ssd_dt_bias"][e]),
                                          _pad_lanes(full["ssd_a_log"][e]), _pad_lanes(full["ssd_d"][e]), row(full["ssd_norm_g"][e]), seq, "ssd_fwd")
            s["yb"] = pool_fwd(s["u"], full["pool_w"][e], row(full["pool_scale"][e]), seq, "pool_fwd")
        else:
            s["h2"], s["vg"], s["xr"], s["gr"] = mod_mm(x1, vec(l, 3), vec(l, 4), w_mix[l][0], OD_SPLITS, seq, "od_in")
            s["ya"], s["conv"] = conf_fwd(s["vg"], full["conf_dw_w"][e], row(full["conf_dw_b"][e]), row(full["conf_ln_g"][e]), row(full["conf_ln_b"][e]), seq, "conf_fwd")
            s["yb"], s["hst"] = lru_fwd(s["xr"], s["gr"], full["lru_conv_w"][e], row(full["lru_conv_b"][e]), full["lru_wa"][e], row(full["lru_ba"][e]),
                                        full["lru_wx"][e], row(full["lru_bx"][e]), row(full["lru_lambda"][e]), seq, "lru_fwd")
        x2, s["y2"] = mm_postnorm([s["ya"], s["yb"]], w_mix[l][1], x1, vec(l, 5), row(full["ln_g"][l, 1]), row(full["ln_b"][l, 1]), 1.0, seq, "mix_out")
        s["x2"] = x2
        if l + 1 < DEPTH:
            s["h3"], s["g3"], s["u3"], s["a3"], g_in, g_out = ffn_up(x2, vec(l, 6), vec(l, 7), w_ffn[l, 1][0], seq, "ffn_up_c2", carry=ffn_items(l + 1, 0))
            w_ffn[l + 1, 0] = (cols_of(g_in), rows_of(g_out))
        else:
            s["h3"], s["g3"], s["u3"], s["a3"] = ffn_up(x2, vec(l, 6), vec(l, 7), w_ffn[l, 1][0], seq, "ffn_up")
        xs, s["y3"] = mm_postnorm([s["a3"]], w_ffn[l, 1][1], x2, vec(l, 8), row(full["ln_g"][l, 2]), row(full["ln_b"][l, 2]), 0.5, seq, "ffn_down")
        saved.append(s)

    dx, loss_row = loss_grad(xs, loss_target.reshape(n, D), "loss")
    loss = lax.psum(loss_row[0, 0], ("x", "y", "c"))

    sg = {k: [None] * shape[0] for k, (shape, _) in SMALL.items()}
    sg["ln_g"] = [[None] * 3 for _ in range(DEPTH)]
    sg["ln_b"] = [[None] * 3 for _ in range(DEPTH)]
    dmod = [[None] * N_MOD for _ in range(DEPTH)]
    pending, got_w = [], {}

    def cut_cols(g):
        r, cc = g.shape
        return g.reshape(r, N_DEV, cc // N_DEV).transpose(1, 0, 2)

    def cut_rows(g):
        r, cc = g.shape
        return g.reshape(N_DEV, r // N_DEV, cc)

    def take(room):
        sel = []
        for j, (_, a) in enumerate(pending):
            if a.size * a.dtype.itemsize <= room:
                sel.append(j)
                room -= a.size * a.dtype.itemsize
        items = [pending[j] for j in sel]
        pending[:] = [it for j, it in enumerate(pending) if j not in sel]
        return [k for k, _ in items], [(a, False) for _, a in items]

    room_down, room_up, room_mix = 8 * MIB, 12 * MIB, 6 * MIB

    def postnorm_backward(dxo, xin, y, g, lng, lnb, w, ks, coef, name, room):
        keys, carry = take(room)
        outs = postnorm_bwd(dxo, xin, y, g, lng, lnb, w, ks, coef, seq, name + "_c%d" % len(keys), carry=carry)
        got_w.update(zip(keys, outs[5 + len(ks):]))
        return outs[0], outs[1], outs[2 : 2 + len(ks)], outs[2 + len(ks)], outs[3 + len(ks)], outs[4 + len(ks)]

    def proj_backward(dparts, w, xin, scv, dxres, name):
        keys, carry = take(room_mix)
        outs = proj_bwd_in(dparts, w, xin, scv, dxres, seq, name + "_c%d" % len(keys), carry=carry)
        got_w.update(zip(keys, outs[3:]))
        return outs[:3]

    def ffn_backward(l, i, dxo, s, xin, hk, gk, uk, ak, yk, jbase, lnj):
        dxres, dy, (da,), dmod[l][jbase + 2], sg["ln_g"][l][lnj], sg["ln_b"][l][lnj] = postnorm_backward(
            dxo, xin, s[yk], vec(l, jbase + 2), row(full["ln_g"][l, lnj]), row(full["ln_b"][l, lnj]), w_ffn[l, i][1], [FF], 0.5, "ffn_down_bwd", room_down)
        keys, carry = take(room_up)
        outs = ffn_bwd_in(da, s[gk], s[uk], w_ffn[l, i][0], xin, vec(l, jbase + 1), dxres, seq, "ffn_up_bwd_c%d" % len(keys), carry=carry)
        dgu, dxi, dmod[l][jbase], dmod[l][jbase + 1] = outs[:4]
        got_w.update(zip(keys, outs[4:]))
        pending.append((("ffn_out", l, i), cut_rows(mm_tn(s[ak], dy, "wg_ffn_out"))))
        keys, carry = take(room_down)
        outs = mm_tn(s[hk], dgu, "wg_ffn_in_c%d" % len(keys), cut=N_DEV, carry=carry)
        got_w.update(zip(keys, outs[1:]) if keys else ())
        pending.append((("ffn_in", l, i), outs[0] if keys else outs))
        return dxi

    for l in reversed(range(DEPTH)):
        s = saved[l]
        e = l // 2
        dx = ffn_backward(l, 1, dx, s, s["x2"], "h3", "g3", "u3", "a3", "y3", 6, 2)
        ks = [1024, POOL_DIM] if l % 2 == 0 else [CONF_DIM, LRU_DIM]
        dxres, dy, (dya, dyb), dmod[l][5], sg["ln_g"][l][1], sg["ln_b"][l][1] = postnorm_backward(
            dx, s["x1"], s["y2"], vec(l, 5), row(full["ln_g"][l, 1]), row(full["ln_b"][l, 1]), w_mix[l][1], ks, 1.0, "mix_out_bwd", room_mix)
        pending.append((("mix_out", l), cut_rows(jnp.concatenate([mm_tn(s["ya"], dy, "wg_mix_a"), mm_tn(s["yb"], dy, "wg_mix_b")], axis=0))))
        if l % 2 == 0:
            (dz, dxbc, ddt, sg["ssd_conv_w"][e], dcb, ddtb, dalog, ddsk, dng) = ssd_bwd(
                dya, s["z"], s["xbc"], s["dtr"], s["sprev"], full["ssd_conv_w"][e], row(full["ssd_conv_b"][e]), _pad_lanes(full["ssd_dt_bias"][e]),
                _pad_lanes(full["ssd_a_log"][e]), _pad_lanes(full["ssd_d"][e]), row(full["ssd_norm_g"][e]), seq, "ssd_bwd")
            sg["ssd_conv_b"][e], sg["ssd_norm_g"][e] = dcb[0], dng[0]
            sg["ssd_dt_bias"][e], sg["ssd_a_log"][e], sg["ssd_d"][e] = ddtb[0, :SSD_HEADS], dalog[0, :SSD_HEADS], ddsk[0, :SSD_HEADS]
            du, sg["pool_w"][e], dps = pool_bwd(dyb, s["u"], full["pool_w"][e], row(full["pool_scale"][e]), seq, "pool_bwd")
            sg["pool_scale"][e] = dps[0]
            dparts = [dz, dxbc, du, ddt]
            dx, dmod[l][3], dmod[l][4] = proj_backward(dparts, w_mix[l][0], s["x1"], vec(l, 4), dxres, "ev_in_bwd")
            gz, gxbc, gu, gdt = [mm_tn(s["h2"], dp, "wg_ev_in") for dp in dparts]
            pending.append((("mix_in", l), cut_cols(jnp.concatenate([gz, gxbc, gdt[:, :SSD_HEADS], gu], axis=1))))
        else:
            dvg, sg["conf_dw_w"][e], dcb, dlg, dlb = conf_bwd(dya, s["conv"], s["vg"], full["conf_dw_w"][e], row(full["conf_ln_g"][e]), row(full["conf_ln_b"][e]), seq, "conf_bwd")
            sg["conf_dw_b"][e], sg["conf_ln_g"][e], sg["conf_ln_b"][e] = dcb[0], dlg[0], dlb[0]
            (dxr, dgr, sg["lru_conv_w"][e], dcb, sg["lru_wa"][e], dba, sg["lru_wx"][e], dbx, dlam) = lru_bwd(
                dyb, s["xr"], s["gr"], s["hst"], full["lru_conv_w"][e], row(full["lru_conv_b"][e]), full["lru_wa"][e], row(full["lru_ba"][e]),
                full["lru_wx"][e], row(full["lru_bx"][e]), row(full["lru_lambda"][e]), seq, "lru_bwd")
            sg["lru_conv_b"][e], sg["lru_ba"][e], sg["lru_bx"][e], sg["lru_lambda"][e] = dcb[0], dba[0], dbx[0], dlam[0]
            dparts = [dvg, dxr, dgr]
            dx, dmod[l][3], dmod[l][4] = proj_backward(dparts, w_mix[l][0], s["x1"], vec(l, 4), dxres, "od_in_bwd")
            pending.append((("mix_in", l), cut_cols(jnp.concatenate([mm_tn(s["h2"], dp, "wg_od_in") for dp in dparts], axis=1))))
        dx = ffn_backward(l, 0, dx, s, s["x0"], "h1", "g1", "u1", "a1", "y1", 0, 0)
    grad_x = dx.reshape(nb, seq, D)

    dmod_mine = jnp.stack([jnp.concatenate([d[:, 0, :] for d in dmod[l]], axis=-1) for l in range(DEPTH)])
    sg["ada_b"] = [jnp.sum(dmod_mine[l], axis=0) for l in range(DEPTH)]
    sg["ln_g"] = [jnp.concatenate(r, axis=0) for r in sg["ln_g"]]
    sg["ln_b"] = [jnp.concatenate(r, axis=0) for r in sg["ln_b"]]
    small_names = list(SMALL)
    sg_packed = _pack([jnp.stack(sg[k]).reshape(SMALL[k][0]) for k in small_names], N_DEV * PACK_ROWS)
    keys, carry = take(1 << 40)
    outs = exchange(carry + [(dmod_mine.reshape(DEPTH, nb, N_DEV, cols).transpose(2, 0, 1, 3), False),
                             (sg_packed.reshape(N_DEV, -1, LANE), False)], "x_last")
    got_w.update(zip(keys, outs))
    dmod_x, parts = outs[len(keys):]
    g_ada_w = ada_bwd(c_all, dmod_x.transpose(1, 0, 2, 3).reshape(DEPTH, N_DEV * nb, cols), "ada_bwd")

    (sg_sum,) = exchange([(sum_parts(parts, "sum_smallgrad"), True)], "ag_smallsum")
    summed = _unpack(sg_sum.reshape(-1, LANE), [SMALL[k][0] for k in small_names])
    grads = {}
    for k, g in zip(small_names, summed):
        ax = SMALL[k][1]
        grads[k] = g if ax is None else lax.dynamic_slice_in_dim(g, me * p[k].shape[ax], p[k].shape[ax], axis=ax)
    loc_shapes = [p[k].shape for k in small_names]
    whole = 512 * LANE
    _, d_s, m_s, v_s = adamw(_pack([p[k] for k in small_names], whole)[None], _pack([p["m_" + k] for k in small_names], whole)[None],
                             _pack([p["v_" + k] for k in small_names], whole)[None], _pack([grads[k] for k in small_names], whole)[None, None], "adamw_small")
    delta = dict(zip(small_names, _unpack(d_s[0], loc_shapes)))
    new_m = dict(zip(small_names, _unpack(m_s[0], loc_shapes)))
    new_v = dict(zip(small_names, _unpack(v_s[0], loc_shapes)))

    big_parts = {
        "ada_w": g_ada_w[None],
        "ffn_w_in": jnp.stack([jnp.stack([got_w["ffn_in", l, i] for i in range(2)], axis=1) for l in range(DEPTH)], axis=1),
        "ffn_w_out": jnp.stack([jnp.stack([got_w["ffn_out", l, i] for i in range(2)], axis=1) for l in range(DEPTH)], axis=1),
        "ev_w_in": jnp.stack([got_w["mix_in", l] for l in (0, 2)], axis=1),
        "ev_w_out": jnp.stack([got_w["mix_out", l] for l in (0, 2)], axis=1),
        "od_w_in": jnp.stack([got_w["mix_in", l] for l in (1, 3)], axis=1),
        "od_w_out": jnp.stack([got_w["mix_out", l] for l in (1, 3)], axis=1),
    }
    for k in BIG:
        w = p[k]
        r2 = (math.prod(w.shape[:-2]),) + w.shape[-2:]
        gp = big_parts[k]
        out = adamw(w.reshape(r2), p["m_" + k].reshape(r2), p["v_" + k].reshape(r2), gp.reshape((gp.shape[0],) + r2), "adamw_" + k)
        grads[k], delta[k], new_m[k], new_v[k] = [o.reshape(w.shape) for o in out]

    return (loss, grad_x, *[grads[k] for k in WEIGHTS], *[delta[k] for k in WEIGHTS], *[new_m[k] for k in WEIGHTS], *[new_v[k] for k in WEIGHTS])
```
